```python
import math
import jax
import jax.numpy as jnp
from jax import lax
import numpy as np

D_MODEL = 1024
BATCH = 16
SEQ = 256
DEPTH = 2
DEC_BATCH = 2
DEC_SEQ = 4096
PAST_LEN = 512

F32 = jnp.float32
GRID_W = 64
MIX_WIDTH = D_MODEL
GROUP_WIDTH = MIX_WIDTH // 4
HEAD_DIM = 64
N_HEADS = GROUP_WIDTH // HEAD_DIM
DIFF_HALF = HEAD_DIM // 2
ROPE_THETA = 10000.0
RET_DECAY_EXP0 = 5
Q_LORA = 256
KV_LORA = 128
QK_NOPE = 64
QK_ROPE = 32
V_HEAD = 64
N_EXPERTS = 64
TOP_K = 8
N_GROUPS = 8
TOPK_GROUPS = 4
D_EXPERT = 256
ROUTED_SCALE = 2.5
MOE_BLOCK = 128
CHUNK = 128
QBLOCK = 128
EPS = 1e-6
IN_WIDTH = 3 * GROUP_WIDTH + 4 * GROUP_WIDTH + 4 * N_HEADS + 4 * GROUP_WIDTH + Q_LORA + KV_LORA + QK_ROPE

kernel_name = 'hybrid_diffusion_parallel_heads_step'


def rmsnorm(x, g=None):
    xf = x.astype(F32)
    y = xf * lax.rsqrt(jnp.mean(jnp.square(xf), axis=-1, keepdims=True) + EPS)
    if g is not None:
        y = y * g.astype(F32)
    return y.astype(x.dtype)


def flip(a):
    return jnp.flip(a, axis=1)


def split_columns(u):
    gw = GROUP_WIDTH
    sizes = (gw, gw, gw,
             gw, gw, gw, gw, 4 * N_HEADS,
             gw, gw, gw, gw,
             Q_LORA, KV_LORA, QK_ROPE)
    idx = [int(i) for i in np.cumsum(sizes)[:-1]]
    return jnp.split(u, idx, axis=-1)


def rope_2d(x):
    n, r = x.shape[1], x.shape[-1]
    rows = n // GRID_W
    row = jnp.repeat(jnp.arange(rows), GRID_W).astype(F32)
    col = jnp.tile(jnp.arange(GRID_W), rows).astype(F32)
    half = r // 2
    quarter = half // 2
    inv = 1.0 / (ROPE_THETA ** (jnp.arange(quarter, dtype=F32) / quarter))
    bshape = (n,) + (1,) * (x.ndim - 3) + (quarter,)

    def rot(xa, pos):
        ang = pos[:, None] * inv[None, :]
        cos = jnp.cos(ang).reshape(bshape).astype(x.dtype)
        sin = jnp.sin(ang).reshape(bshape).astype(x.dtype)
        x1, x2 = xa[..., :quarter], xa[..., quarter:]
        return jnp.concatenate([x1 * cos - x2 * sin, x2 * cos + x1 * sin], axis=-1)

    return jnp.concatenate([rot(x[..., :half], row), rot(x[..., half:], col)], axis=-1)


def rope_halves(x):
    return jnp.concatenate([rope_2d(x[..., :DIFF_HALF]), rope_2d(x[..., DIFF_HALF:])], axis=-1)


def map_query_blocks(fn, qs):
    nq = qs[0].shape[1]
    nb = nq // QBLOCK
    blocked = tuple(a.reshape((a.shape[0], nb, QBLOCK) + a.shape[2:]).swapaxes(0, 1) for a in qs)
    out = lax.map(lambda args: fn(*args), blocked).swapaxes(0, 1)
    return out.reshape((out.shape[0], nq) + out.shape[3:])


def diff_attention(q, k, v, lam):
    k1, k2 = k[..., :DIFF_HALF], k[..., DIFF_HALF:]
    scale = DIFF_HALF ** -0.5

    def blk(qb):
        q1, q2 = qb[..., :DIFF_HALF], qb[..., DIFF_HALF:]
        p1 = jax.nn.softmax(jnp.einsum('bqhd,bkhd->bhqk', q1, k1).astype(F32) * scale, axis=-1)
        p2 = jax.nn.softmax(jnp.einsum('bqhd,bkhd->bhqk', q2, k2).astype(F32) * scale, axis=-1)
        return jnp.einsum('bhqk,bkhd->bqhd', (p1 - lam * p2).astype(v.dtype), v)

    return map_query_blocks(blk, (q,))


def mla_attention(q_nope, q_rope, k_nope, k_rope, v):
    scale = (QK_NOPE + QK_ROPE) ** -0.5

    def blk(qn, qr):
        s = jnp.einsum('bqhd,bkhd->bhqk', qn, k_nope) + jnp.einsum('bqhr,bkr->bhqk', qr, k_rope)
        p = jax.nn.softmax(s.astype(F32) * scale, axis=-1).astype(v.dtype)
        return jnp.einsum('bhqk,bkhd->bqhd', p, v)

    return map_query_blocks(blk, (q_nope, q_rope))


def to_chunks(a):
    b, t = a.shape[:2]
    return a.reshape((b, t // CHUNK, CHUNK) + a.shape[2:]).swapaxes(0, 1)


def from_chunks(a):
    a = a.swapaxes(0, 1)
    return a.reshape((a.shape[0], a.shape[1] * a.shape[2]) + a.shape[3:])


def mlstm_scan(q, k, v, log_i, log_f, state):
    tri = jnp.tril(jnp.ones((CHUNK, CHUNK), dtype=bool))[None, :, :, None]

    def step(carry, inp):
        c_mat, n_vec, m = carry
        qc, kc, vc, li, lf = inp
        bcum = jnp.cumsum(lf, axis=1)
        log_d = jnp.where(tri, bcum[:, :, None] - bcum[:, None] + li[:, None], -jnp.inf)
        log_inter = bcum + m[:, None]
        m_t = jnp.maximum(log_inter, log_d.max(axis=2))
        w = jnp.einsum('bthd,bshd->btsh', qc, kc) * jnp.exp(log_d - m_t[:, :, None])
        w_inter = jnp.exp(log_inter - m_t)
        num = jnp.einsum('btsh,bshv->bthv', w, vc) + w_inter[..., None] * jnp.einsum('bthk,bhkv->bthv', qc, c_mat)
        den = w.sum(axis=2) + w_inter * jnp.einsum('bthk,bhk->bth', qc, n_vec)
        h = num / jnp.maximum(jnp.abs(den), jnp.exp(-m_t))[..., None]
        m_new = m_t[:, -1]
        w_end = jnp.exp(bcum[:, -1:] - bcum + li - m_new[:, None])
        decay = jnp.exp(bcum[:, -1] + m - m_new)
        c_mat = decay[..., None, None] * c_mat + jnp.einsum('bsh,bshk,bshv->bhkv', w_end, kc, vc)
        n_vec = decay[..., None] * n_vec + jnp.einsum('bsh,bshk->bhk', w_end, kc)
        return (c_mat, n_vec, m_new), h

    final, h = lax.scan(step, state, tuple(to_chunks(a) for a in (q, k, v, log_i, log_f)))
    return from_chunks(h), final


def retention_scan(q, k, v, log_gamma, s0):
    pos = jnp.arange(CHUNK, dtype=F32)
    lag = pos[:, None] - pos[None, :]
    intra = jnp.where((lag >= 0)[..., None], jnp.exp(jnp.maximum(lag, 0.0)[..., None] * log_gamma), 0.0)
    inter = jnp.exp((pos + 1.0)[:, None] * log_gamma)
    tail = jnp.exp((CHUNK - 1.0 - pos)[:, None] * log_gamma)
    chunk_decay = jnp.exp(CHUNK * log_gamma)

    def step(s, inp):
        qc, kc, vc = inp
        a = jnp.einsum('bthd,bshd->btsh', qc, kc) * intra
        o = jnp.einsum('btsh,bshv->bthv', a, vc) + inter[:, :, None] * jnp.einsum('bthk,bhkv->bthv', qc, s)
        s = chunk_decay[:, None, None] * s + jnp.einsum('sh,bshk,bshv->bhkv', tail, kc, vc)
        return s, o

    s, o = lax.scan(step, s0, (to_chunks(q), to_chunks(k), to_chunks(v)))
    return from_chunks(o), s


def token_mixers(u, lam_init, lam_vec, diff_g, gate_bias, mlstm_g, decay_logit,
                 q_norm_g, w_uq, kv_norm_g, w_ukv, ctx):
    bsz, n = u.shape[:2]
    (a_q, a_k, a_v, m_q, m_k, m_v, m_o, m_gates,
     r_q, r_k, r_v, r_g, d_cq, d_ckv, d_kr) = split_columns(u)
    heads = lambda t: t.reshape(bsz, n, N_HEADS, -1)
    latent = ctx is not None
    if latent:
        ctx_k, ctx_v, st_c, st_n, st_m, st_s, ctx_ckv, ctx_kr = ctx

    q, k, v = heads(a_q), heads(a_k), heads(a_v)
    lv = lam_vec.astype(F32)
    lam = jnp.exp(jnp.sum(lv[0] * lv[1])) - jnp.exp(jnp.sum(lv[2] * lv[3])) + lam_init
    if latent:
        a = diff_attention(rope_halves(q), jnp.concatenate([ctx_k, rope_halves(k)], axis=1),
                           jnp.concatenate([ctx_v, v], axis=1), lam)
    else:
        a = diff_attention(q, k, v, lam)
    a_out = (rmsnorm(a, diff_g) * (1.0 - lam_init)).reshape(bsz, n, GROUP_WIDTH)

    qm = heads(m_q).astype(F32)
    km = heads(m_k).astype(F32) * HEAD_DIM ** -0.5
    vm = heads(m_v).astype(F32)
    g = m_gates.astype(F32).reshape(bsz, n, 4, N_HEADS) + gate_bias.astype(F32)
    li_f, lf_f = g[:, :, 0], jax.nn.log_sigmoid(g[:, :, 1])
    li_b, lf_b = g[:, :, 2], jax.nn.log_sigmoid(g[:, :, 3])
    if latent:
        st = (st_c.astype(F32), st_n.astype(F32), st_m.astype(F32))
        init_f = tuple(s[:, 0] for s in st)
        init_b = tuple(s[:, 1] for s in st)
    else:
        init_f = (jnp.zeros((bsz, N_HEADS, HEAD_DIM, HEAD_DIM), F32),
                  jnp.zeros((bsz, N_HEADS, HEAD_DIM), F32), jnp.zeros((bsz, N_HEADS), F32))
        init_b = init_f
    h_f, fin_f = mlstm_scan(qm, km, vm, li_f, lf_f, init_f)
    h_b, fin_b = mlstm_scan(flip(qm), flip(km), flip(vm), flip(li_b), flip(lf_b), init_b)
    h = rmsnorm(h_f + flip(h_b)).reshape(bsz, n, GROUP_WIDTH) * mlstm_g.astype(F32)
    b_out = (jax.nn.sigmoid(m_o.astype(F32)) * h).astype(u.dtype)

    log_gamma = jax.nn.log_sigmoid(decay_logit.astype(F32))
    qr = heads(r_q).astype(F32)
    kr = heads(r_k).astype(F32) * HEAD_DIM ** -0.5
    vr = heads(r_v).astype(F32)
    if latent:
        s_init_f, s_init_b = st_s[:, 0].astype(F32), st_s[:, 1].astype(F32)
    else:
        s_init_f = jnp.zeros((bsz, N_HEADS, HEAD_DIM, HEAD_DIM), F32)
        s_init_b = s_init_f
    o_f, s_f = retention_scan(qr, kr, vr, log_gamma[0], s_init_f)
    o_b, s_b = retention_scan(flip(qr), flip(kr), flip(vr), log_gamma[1], s_init_b)
    r_out = (rmsnorm(o_f + flip(o_b)).reshape(bsz, n, GROUP_WIDTH)
             * jax.nn.silu(r_g.astype(F32))).astype(u.dtype)

    ckv = rmsnorm(d_ckv, kv_norm_g)
    cq = heads(rmsnorm(d_cq, q_norm_g) @ w_uq)
    q_nope, q_rope = cq[..., :QK_NOPE], cq[..., QK_NOPE:]
    if latent:
        q_rope = rope_2d(q_rope)
        ckv_all = jnp.concatenate([ctx_ckv, ckv], axis=1)
        kr_all = jnp.concatenate([ctx_kr, rope_2d(d_kr)], axis=1)
    else:
        ckv_all, kr_all = ckv, d_kr
    kv = (ckv_all @ w_ukv).reshape(bsz, ckv_all.shape[1], N_HEADS, QK_NOPE + V_HEAD)
    d_out = mla_attention(q_nope, q_rope, kv[..., :QK_NOPE], kr_all, kv[..., QK_NOPE:]).reshape(bsz, n, GROUP_WIDTH)

    mixed = jnp.concatenate([a_out, b_out, r_out, d_out], axis=-1)
    if latent:
        return mixed, None
    ctx_tensors = (k, v,
                   jnp.stack([fin_f[0], fin_b[0]], axis=1),
                   jnp.stack([fin_f[1], fin_b[1]], axis=1),
                   jnp.stack([fin_f[2], fin_b[2]], axis=1),
                   jnp.stack([s_f, s_b], axis=1),
                   ckv, d_kr)
    return mixed, ctx_tensors


def routed_experts(x, idx, wts, w_gate, w_up, w_down):
    t, d = x.shape
    n_assign = t * TOP_K
    n_blocks = -(-(n_assign + N_EXPERTS * (MOE_BLOCK - 1)) // MOE_BLOCK)
    flat_e = idx.reshape(-1)
    order = jnp.argsort(flat_e)
    sorted_e = flat_e[order]
    sorted_tok = (order // TOP_K).astype(jnp.int32)
    sorted_w = wts.reshape(-1)[order]
    counts = jax.ops.segment_sum(jnp.ones_like(flat_e), flat_e, num_segments=N_EXPERTS)
    padded = (counts + MOE_BLOCK - 1) // MOE_BLOCK * MOE_BLOCK
    pad_end = jnp.cumsum(padded)
    pad_start = pad_end - padded
    raw_start = jnp.cumsum(counts) - counts
    dest = pad_start[sorted_e] + jnp.arange(n_assign) - raw_start[sorted_e]
    slot_tok = jnp.full((n_blocks * MOE_BLOCK,), t, jnp.int32).at[dest].set(sorted_tok)
    slot_w = jnp.zeros((n_blocks * MOE_BLOCK,), x.dtype).at[dest].set(sorted_w)
    block_e = jnp.minimum(jnp.searchsorted(pad_end, jnp.arange(n_blocks) * MOE_BLOCK, side='right'), N_EXPERTS - 1)
    x_pad = jnp.concatenate([x, jnp.zeros((1, d), x.dtype)], axis=0)
    xb = x_pad[slot_tok].reshape(n_blocks, MOE_BLOCK, d)

    def expert_block(args):
        xe, e = args
        return (jax.nn.silu(xe @ w_gate[e]) * (xe @ w_up[e])) @ w_down[e]

    yb = lax.map(expert_block, (xb, block_e))
    y = jnp.zeros((t + 1, d), x.dtype).at[slot_tok].add(yb.reshape(-1, d) * slot_w[:, None])
    return y[:t]


def moe_ffn(h, w_router, router_bias, w_gate, w_up, w_down, s_gate, s_up, s_down):
    shp = h.shape
    x = h.reshape(-1, shp[-1])
    t = x.shape[0]
    scores = jax.nn.sigmoid((x @ w_router).astype(F32))
    sel = scores + router_bias.astype(F32)
    grp_score = lax.top_k(sel.reshape(t, N_GROUPS, N_EXPERTS // N_GROUPS), 2)[0].sum(-1)
    _, gidx = lax.top_k(grp_score, TOPK_GROUPS)
    gmask = jax.nn.one_hot(gidx, N_GROUPS, dtype=F32).sum(1) > 0
    sel = jnp.where(jnp.repeat(gmask, N_EXPERTS // N_GROUPS, axis=1), sel, -jnp.inf)
    _, idx = lax.top_k(sel, TOP_K)
    w = jnp.take_along_axis(scores, idx, axis=1)
    w = w / w.sum(-1, keepdims=True) * ROUTED_SCALE
    routed = routed_experts(x, idx, w.astype(x.dtype), w_gate, w_up, w_down)
    shared = (jax.nn.silu(x @ s_gate) * (x @ s_up)) @ s_down
    return (routed + shared).reshape(shp)


def trunk_layer(x, cond, layer_args, mix_args, ffn_args, ctx):
    w_mod_l, b_mod_l, npre, npost, w_in_l, w_out_l = layer_args
    mod = (jax.nn.silu(cond) @ w_mod_l + b_mod_l).reshape(cond.shape[0], 1, 6, x.shape[-1])
    sh1, sc1, g1, sh2, sc2, g2 = [mod[:, :, i] for i in range(6)]
    h = rmsnorm(x, npre[0]) * (1.0 + sc1) + sh1
    mixed, ctx_t = token_mixers(h @ w_in_l, *mix_args, ctx)
    x = x + g1 * rmsnorm(mixed @ w_out_l, npost[0])
    h = rmsnorm(x, npre[1]) * (1.0 + sc2) + sh2
    x = x + g2 * rmsnorm(moe_ffn(h, *ffn_args), npost[1])
    return x, ctx_t


def setup_inputs(seed: int = 0) -> dict:
    key = jax.random.key(seed)
    ks = iter(jax.random.split(key, 48))
    nrm = lambda shape, s=1.0: jax.random.normal(next(ks), shape, F32) * s
    H = N_HEADS
    gate_base = jnp.stack([jnp.zeros((H,), F32), jnp.linspace(3.0, 6.0, H, dtype=F32),
                           jnp.zeros((H,), F32), jnp.linspace(3.0, 6.0, H, dtype=F32)])
    decay_base = jnp.log(2.0 ** (RET_DECAY_EXP0 + jnp.arange(H, dtype=F32)) - 1.0)
    return {
        'x_prompt': nrm((BATCH, SEQ, D_MODEL)),
        'x_sample': nrm((DEC_BATCH, DEC_SEQ, D_MODEL)),
        'cache_diff_k': nrm((DEC_BATCH, DEPTH, PAST_LEN, H, HEAD_DIM)),
        'cache_diff_v': nrm((DEC_BATCH, DEPTH, PAST_LEN, H, HEAD_DIM)),
        'state_mlstm_C': nrm((DEC_BATCH, DEPTH, 2, H, HEAD_DIM, HEAD_DIM), 0.5),
        'state_mlstm_n': nrm((DEC_BATCH, DEPTH, 2, H, HEAD_DIM), 0.5),
        'state_mlstm_m': nrm((DEC_BATCH, DEPTH, 2, H), 0.5),
        'state_ret_S': nrm((DEC_BATCH, DEPTH, 2, H, HEAD_DIM, HEAD_DIM), 0.5),
        'cache_mla_ckv': nrm((DEC_BATCH, DEPTH, PAST_LEN, KV_LORA)),
        'cache_mla_krope': nrm((DEC_BATCH, DEPTH, PAST_LEN, QK_ROPE)),
        'c': nrm((DEC_BATCH, D_MODEL)),
        'c_ctx': nrm((D_MODEL,)),
        'w_mod': nrm((DEPTH, D_MODEL, 6 * D_MODEL), 0.5 * D_MODEL ** -0.5),
        'b_mod': nrm((DEPTH, 6 * D_MODEL), 0.01),
        'norm_pre': 1.0 + nrm((DEPTH, 2, D_MODEL), 0.02),
        'norm_post': 1.0 + nrm((DEPTH, 2, D_MODEL), 0.02),
        'w_in': nrm((DEPTH, D_MODEL, IN_WIDTH), D_MODEL ** -0.5),
        'w_out': nrm((DEPTH, MIX_WIDTH, D_MODEL), MIX_WIDTH ** -0.5),
        'diff_lambda': nrm((DEPTH, 4, DIFF_HALF), 0.1),
        'diff_norm': 1.0 + nrm((DEPTH, HEAD_DIM), 0.02),
        'mlstm_gate_bias': gate_base + nrm((DEPTH, 4, H), 0.1),
        'mlstm_norm': 1.0 + nrm((DEPTH, GROUP_WIDTH), 0.02),
        'ret_decay_logit': decay_base + nrm((DEPTH, 2, H), 0.01),
        'mla_q_norm': 1.0 + nrm((DEPTH, Q_LORA), 0.02),
        'mla_w_uq': nrm((DEPTH, Q_LORA, H * (QK_NOPE + QK_ROPE)), Q_LORA ** -0.5),
        'mla_kv_norm': 1.0 + nrm((DEPTH, KV_LORA), 0.02),
        'mla_w_ukv': nrm((DEPTH, KV_LORA, H * (QK_NOPE + V_HEAD)), KV_LORA ** -0.5),
        'moe_w_router': nrm((DEPTH, D_MODEL, N_EXPERTS), D_MODEL ** -0.5),
        'moe_router_bias': nrm((DEPTH, N_EXPERTS), 0.01),
        'moe_w_gate': nrm((DEPTH, N_EXPERTS, D_MODEL, D_EXPERT), D_MODEL ** -0.5),
        'moe_w_up': nrm((DEPTH, N_EXPERTS, D_MODEL, D_EXPERT), D_MODEL ** -0.5),
        'moe_w_down': nrm((DEPTH, N_EXPERTS, D_EXPERT, D_MODEL), D_EXPERT ** -0.5),
        'shared_w_gate': nrm((DEPTH, D_MODEL, D_EXPERT), D_MODEL ** -0.5),
        'shared_w_up': nrm((DEPTH, D_MODEL, D_EXPERT), D_MODEL ** -0.5),
        'shared_w_down': nrm((DEPTH, D_EXPERT, D_MODEL), D_EXPERT ** -0.5),
    }


def reference(x_prompt, x_sample, cache_diff_k, cache_diff_v, state_mlstm_C, state_mlstm_n, state_mlstm_m,
              state_ret_S, cache_mla_ckv, cache_mla_krope, c, c_ctx, w_mod, b_mod, norm_pre, norm_post,
              w_in, w_out, diff_lambda, diff_norm, mlstm_gate_bias, mlstm_norm, ret_decay_logit,
              mla_q_norm, mla_w_uq, mla_kv_norm, mla_w_ukv, moe_w_router, moe_router_bias,
              moe_w_gate, moe_w_up, moe_w_down, shared_w_gate, shared_w_up, shared_w_down):
    xp, xs = x_prompt, x_sample
    outs = [[] for _ in range(8)]
    for l in range(DEPTH):
        lam_init = 0.8 - 0.6 * math.exp(-0.3 * l)
        layer_args = (w_mod[l], b_mod[l], norm_pre[l], norm_post[l], w_in[l], w_out[l])
        mix_args = (lam_init, diff_lambda[l], diff_norm[l], mlstm_gate_bias[l], mlstm_norm[l],
                    ret_decay_logit[l], mla_q_norm[l], mla_w_uq[l], mla_kv_norm[l], mla_w_ukv[l])
        ffn_args = (moe_w_router[l], moe_router_bias[l], moe_w_gate[l], moe_w_up[l], moe_w_down[l],
                    shared_w_gate[l], shared_w_up[l], shared_w_down[l])
        xp, ctx_t = trunk_layer(xp, c_ctx[None, :], layer_args, mix_args, ffn_args, None)
        for lst, tns in zip(outs, ctx_t):
            lst.append(tns)
        cache_l = (cache_diff_k[:, l], cache_diff_v[:, l], state_mlstm_C[:, l], state_mlstm_n[:, l],
                   state_mlstm_m[:, l], state_ret_S[:, l], cache_mla_ckv[:, l], cache_mla_krope[:, l])
        xs, _ = trunk_layer(xs, c, layer_args, mix_args, ffn_args, cache_l)
    new_diff_k = jnp.stack(outs[0], axis=1)
    new_diff_v = jnp.stack(outs[1], axis=1)
    new_mlstm_C = jnp.stack(outs[2], axis=1)
    new_mlstm_n = jnp.stack(outs[3], axis=1)
    new_mlstm_m = jnp.stack(outs[4], axis=1)
    new_ret_S = jnp.stack(outs[5], axis=1)
    new_mla_ckv = jnp.stack(outs[6], axis=1)
    new_mla_krope = jnp.stack(outs[7], axis=1)
    return (xp, xs, new_diff_k, new_diff_v, new_mlstm_C, new_mlstm_n, new_mlstm_m, new_ret_S, new_mla_ckv, new_mla_krope)
```

```python
import functools
import math

import numpy as np
import jax
import jax.numpy as jnp
from jax import lax
from jax.experimental import pallas as pl
from jax.experimental.pallas import tpu as pltpu

F32 = jnp.float32
BF16 = jnp.bfloat16
HIGHEST = lax.Precision.HIGHEST

D_MODEL = 1024
GRID_W = 64
GROUP_WIDTH = 256
HEAD_DIM = 64
N_HEADS = 4
DIFF_HALF = 32
ROPE_THETA = 10000.0
Q_LORA = 256
KV_LORA = 128
QK_NOPE = 64
QK_ROPE = 32
V_HEAD = 64
N_EXPERTS = 64
TOP_K = 8
N_GROUPS = 8
TOPK_GROUPS = 4
D_EXPERT = 256
ROUTED_SCALE = 2.5
CHUNK = 128
EPS = 1e-6
Q_MLA = N_HEADS * (QK_NOPE + QK_ROPE)
KV_MLA = N_HEADS * (QK_NOPE + V_HEAD)

LANES = 128
VMEM_LIMIT = 56 * 1024 * 1024

TM = 256
TM_MOE = 1024
TQ = 256
TK = 512

C_AQ, C_AK, C_AV = 0, 256, 512
C_MQKV, C_MO = 768, 1536
C_RQKV, C_RG = 1792, 2560
C_CQ, C_CKV, C_TAIL = 2816, 3072, 3200
C_AQS, C_AKS, C_TAILS = 3328, 3584, 3840
W_ALL = 3968
TAIL_GATE0 = QK_ROPE


def _cparams(sem):
    return pltpu.CompilerParams(dimension_semantics=sem, vmem_limit_bytes=VMEM_LIMIT)


def _rms(x):
    return x * lax.rsqrt(jnp.mean(x * x, axis=-1, keepdims=True) + EPS)


def _head_mean_matrix(width):
    r = lax.broadcasted_iota(jnp.int32, (width, width), 0) // HEAD_DIM
    c = lax.broadcasted_iota(jnp.int32, (width, width), 1) // HEAD_DIM
    return jnp.where(r == c, 1.0 / HEAD_DIM, 0.0).astype(F32)


def _rms_heads(x):
    ms = jnp.dot(x * x, _head_mean_matrix(x.shape[-1]), precision=HIGHEST, preferred_element_type=F32)
    return x * lax.rsqrt(ms + EPS)


def _sigmoid(x):
    return 1.0 / (1.0 + jnp.exp(-x))


def _silu(x):
    return x * _sigmoid(x)


def _log_sigmoid(x):
    return jnp.minimum(x, 0.0) - jnp.log1p(jnp.exp(-jnp.abs(x)))


def _bdot(a, b):
    return jnp.dot(a.astype(BF16), b.astype(BF16), preferred_element_type=F32)


def _mod_kernel(c_ref, w_ref, b_ref, o_ref):
    o_ref[0] = _bdot(_silu(c_ref[...]), w_ref[0]) + b_ref[0]


def _modulation(cond, w_mod, b_mod):
    depth, _, n = w_mod.shape
    tn = 1536
    return pl.pallas_call(
        _mod_kernel,
        grid=(depth, n // tn),
        in_specs=[pl.BlockSpec((8, D_MODEL), lambda l, j: (0, 0)),
                  pl.BlockSpec((1, D_MODEL, tn), lambda l, j: (l, 0, j)),
                  pl.BlockSpec((1, 1, tn), lambda l, j: (l, 0, j))],
        out_specs=pl.BlockSpec((1, 8, tn), lambda l, j: (l, 0, j)),
        out_shape=jax.ShapeDtypeStruct((depth, 8, n), F32),
        compiler_params=_cparams(("parallel", "parallel")),
        name="adaln_mod",
    )(cond, w_mod, b_mod.reshape(depth, 1, n))


def _in_kernel(x_ref, mod_ref, npre_ref, w_ref, cos_ref, sin_ref, cosq_ref, sinq_ref,
               qg_ref, wuq_ref, wuqs_ref, kvg_ref, wukv_ref,
               aq1_ref, aq2_ref, ak1_ref, ak2_ref, avh_ref, ak_ref, av_ref,
               mqkv_ref, mo_ref, rqkv_ref, rg_ref, tail_ref, tailr_ref,
               qmla_ref, ckv_ref, kv_ref):
    x = x_ref[...]
    m = mod_ref[0]
    h = (_rms(x) * npre_ref[...] * (1.0 + m[1:2]) + m[0:1]).astype(BF16)

    def proj(c0, width):
        return jnp.dot(h, w_ref[:, c0:c0 + width], preferred_element_type=F32)

    cos = cos_ref[...]
    sin = sin_ref[...]
    aq = (proj(C_AQ, 256) * cos + proj(C_AQS, 256) * sin) * (DIFF_HALF ** -0.5)
    ak = proj(C_AK, 256) * cos + proj(C_AKS, 256) * sin
    av = proj(C_AV, 256)
    ak_ref[...] = ak
    av_ref[...] = av
    for hd in range(N_HEADS):
        lo = hd * HEAD_DIM
        aq1_ref[hd] = aq[:, lo:lo + DIFF_HALF].astype(BF16)
        aq2_ref[hd] = aq[:, lo + DIFF_HALF:lo + HEAD_DIM].astype(BF16)
        ak1_ref[hd] = ak[:, lo:lo + DIFF_HALF].astype(BF16)
        ak2_ref[hd] = ak[:, lo + DIFF_HALF:lo + HEAD_DIM].astype(BF16)
        avh_ref[hd] = av[:, lo:lo + HEAD_DIM].astype(BF16)

    mqkv_ref[...] = proj(C_MQKV, 768).astype(BF16)
    mo_ref[...] = proj(C_MO, 256)
    rqkv_ref[...] = proj(C_RQKV, 768).astype(BF16)
    rg_ref[...] = proj(C_RG, 256)

    tail = proj(C_TAIL, LANES)
    tail_ref[...] = tail
    tailr_ref[...] = tail * cos[:, :LANES] + proj(C_TAILS, LANES) * sin[:, :LANES]

    cqn = (_rms(proj(C_CQ, Q_LORA)) * qg_ref[...]).astype(BF16)
    q = jnp.dot(cqn, wuq_ref[...], preferred_element_type=F32)
    qs = jnp.dot(cqn, wuqs_ref[...], preferred_element_type=F32)
    qmla = (q * cosq_ref[...] + qs * sinq_ref[...]) * ((QK_NOPE + QK_ROPE) ** -0.5)
    for hd in range(N_HEADS):
        lo = hd * (QK_NOPE + QK_ROPE)
        qmla_ref[hd] = qmla[:, lo:lo + QK_NOPE + QK_ROPE].astype(BF16)
    ckvn = _rms(proj(C_CKV, KV_LORA)) * kvg_ref[...]
    ckv_ref[...] = ckvn
    kv_ref[...] = jnp.dot(ckvn.astype(BF16), wukv_ref[...], preferred_element_type=F32).astype(BF16)


def _mod_row(i, npt, tps):
    return jnp.where(i < npt, 0, 1 + (i - npt) // tps)


def _input_stage(x, mod, npre, w_all, tabs, qg, wuq, wuqs, kvg, wukv, dims):
    t_all = x.shape[0]
    npt, tps = dims["tp"] // TM, dims["ns"] // TM
    row = lambda i: (i, 0)
    hrow = lambda i: (0, i, 0)
    const2 = lambda i: (0, 0)
    tok = lambda w: pl.BlockSpec((TM, w), row)
    headed = lambda w: pl.BlockSpec((N_HEADS, TM, w), hrow)
    full = lambda a: pl.BlockSpec(a.shape, const2)
    cos, sin, cosq, sinq = tabs
    out_shapes = [
        (headed(DIFF_HALF), (N_HEADS, t_all, DIFF_HALF), BF16),
        (headed(DIFF_HALF), (N_HEADS, t_all, DIFF_HALF), BF16),
        (headed(DIFF_HALF), (N_HEADS, t_all, DIFF_HALF), BF16),
        (headed(DIFF_HALF), (N_HEADS, t_all, DIFF_HALF), BF16),
        (headed(HEAD_DIM), (N_HEADS, t_all, HEAD_DIM), BF16),
        (tok(256), (t_all, 256), F32),
        (tok(256), (t_all, 256), F32),
        (tok(768), (t_all, 768), BF16),
        (tok(256), (t_all, 256), F32),
        (tok(768), (t_all, 768), BF16),
        (tok(256), (t_all, 256), F32),
        (tok(LANES), (t_all, LANES), F32),
        (tok(LANES), (t_all, LANES), F32),
        (headed(QK_NOPE + QK_ROPE), (N_HEADS, t_all, QK_NOPE + QK_ROPE), BF16),
        (tok(KV_LORA), (t_all, KV_LORA), F32),
        (tok(KV_MLA), (t_all, KV_MLA), BF16),
    ]
    return pl.pallas_call(
        _in_kernel,
        grid=(t_all // TM,),
        in_specs=[tok(D_MODEL),
                  pl.BlockSpec((1, 6, D_MODEL), lambda i: (_mod_row(i, npt, tps), 0, 0)),
                  full(npre), full(w_all), tok(256), tok(256), tok(Q_MLA), tok(Q_MLA),
                  full(qg), full(wuq), full(wuqs), full(kvg), full(wukv)],
        out_specs=[s for s, _, _ in out_shapes],
        out_shape=[jax.ShapeDtypeStruct(shp, dt) for _, shp, dt in out_shapes],
        compiler_params=_cparams(("parallel",)),
        name="input_stage",
    )(x, mod, npre, w_all, cos, sin, cosq, sinq, qg, wuq, wuqs, kvg, wukv)


def _softmax_step(q, k, v, carry):
    m, l, acc = carry
    s = lax.dot_general(q, k, (((1,), (1,)), ((), ())), preferred_element_type=F32)
    m_new = jnp.maximum(m, jnp.max(s, axis=-1, keepdims=True))
    p = jnp.exp(s - m_new)
    alpha = jnp.exp(m - m_new)
    l = alpha * l + jnp.sum(p, axis=-1, keepdims=True)
    acc = alpha * acc + jnp.dot(p.astype(BF16), v, preferred_element_type=F32)
    return m_new, l, acc


def _attend(q, key_refs, val_refs, tk):
    tq = q.shape[0]
    carry = (jnp.full((tq, 1), -jnp.inf, F32), jnp.zeros((tq, 1), F32), jnp.zeros((tq, V_HEAD), F32))
    for (k_ref, n), (v_ref, _) in zip(key_refs, val_refs):
        step = min(tk, n)

        def body(j, c, k_ref=k_ref, v_ref=v_ref, step=step):
            start = pl.multiple_of(j * step, step)
            return _softmax_step(q, k_ref[0, pl.ds(start, step), :], v_ref[0, pl.ds(start, step), :], c)

        carry = lax.fori_loop(0, n // step, body, carry)
    _, l, acc = carry
    return acc / l


def _diff_attn_kernel(*refs, n_new, n_ctx, lam_init):
    if n_ctx:
        q1_ref, q2_ref, k1_ref, k2_ref, v_ref, c1_ref, c2_ref, cv_ref, lam_ref, g_ref, o_ref = refs
        ks1, ks2 = [(c1_ref, n_ctx), (k1_ref, n_new)], [(c2_ref, n_ctx), (k2_ref, n_new)]
        vs = [(cv_ref, n_ctx), (v_ref, n_new)]
    else:
        q1_ref, q2_ref, k1_ref, k2_ref, v_ref, lam_ref, g_ref, o_ref = refs
        ks1, ks2, vs = [(k1_ref, n_new)], [(k2_ref, n_new)], [(v_ref, n_new)]
    lv = lam_ref[...]
    lam = (jnp.exp(jnp.sum(lv[0:1] * lv[1:2], axis=-1, keepdims=True))
           - jnp.exp(jnp.sum(lv[2:3] * lv[3:4], axis=-1, keepdims=True)) + lam_init)
    a = _attend(q1_ref[0], ks1, vs, TK) - lam * _attend(q2_ref[0], ks2, vs, TK)
    o_ref[0] = _rms(a) * g_ref[...] * (1.0 - lam_init)


def _mla_attn_kernel(*refs, n_new, n_ctx):
    if n_ctx:
        q_ref, k_ref, v_ref, ck_ref, cv_ref, o_ref = refs
        ks, vs = [(ck_ref, n_ctx), (k_ref, n_new)], [(cv_ref, n_ctx), (v_ref, n_new)]
    else:
        q_ref, k_ref, v_ref, o_ref = refs
        ks, vs = [(k_ref, n_new)], [(v_ref, n_new)]
    o_ref[0] = _attend(q_ref[0], ks, vs, TK)


def _attention(kernel, qs, ks, v, ctx_ks, ctx_v, extras, row0, nb, n, n_ctx):
    tq = min(TQ, n)
    nqt = n // tq
    qmap = lambda h, b, i: (h, row0 // tq + b * nqt + i, 0)
    kmap = lambda h, b, i: (h, row0 // n + b, 0)
    cmap = lambda h, b, i: (h, b, 0)
    in_specs = [pl.BlockSpec((1, tq, a.shape[-1]), qmap) for a in qs]
    in_specs += [pl.BlockSpec((1, n, a.shape[-1]), kmap) for a in ks + [v]]
    args = list(qs) + list(ks) + [v]
    if n_ctx:
        in_specs += [pl.BlockSpec((1, n_ctx, a.shape[-1]), cmap) for a in ctx_ks + [ctx_v]]
        args += list(ctx_ks) + [ctx_v]
    in_specs += [pl.BlockSpec(a.shape, lambda h, b, i: (0, 0)) for a in extras]
    args += list(extras)
    return pl.pallas_call(
        kernel,
        grid=(N_HEADS, nb, nqt),
        in_specs=in_specs,
        out_specs=pl.BlockSpec((1, tq, V_HEAD), lambda h, b, i: (h, b * nqt + i, 0)),
        out_shape=jax.ShapeDtypeStruct((N_HEADS, nb * n, V_HEAD), F32),
        compiler_params=_cparams(("parallel", "parallel", "parallel")),
        name=kernel.func.__name__.strip("_"),
    )(*args)


def _tri(lower):
    r = lax.broadcasted_iota(jnp.int32, (CHUNK, CHUNK), 0)
    c = lax.broadcasted_iota(jnp.int32, (CHUNK, CHUNK), 1)
    return (c <= r) if lower else (c >= r)


def _nt(a, b):
    return lax.dot_general(a, b, (((1,), (1,)), ((), ())), preferred_element_type=F32)


def _tn(a, b):
    return lax.dot_general(a, b, (((0,), (0,)), ((), ())), preferred_element_type=F32)


def _mlstm_dir(d, qkv_ref, tail_ref, bias_ref, c_s, n_s, m_s, h_ref):
    g = tail_ref[...] + bias_ref[...]
    ls = _log_sigmoid(g)
    g_t, ls_t = g.T, ls.T
    lower = _tri(True).astype(F32)
    upper = _tri(False).astype(F32)
    left, right = (lower, upper) if d == 0 else (upper, lower)
    cum_col = jnp.dot(left, ls, precision=HIGHEST, preferred_element_type=F32)
    cum_row = jnp.dot(ls_t, right, precision=HIGHEST, preferred_element_type=F32)
    valid = _tri(d == 0)
    last = CHUNK - 1 if d == 0 else 0
    qkv = qkv_ref[...]
    for hd in range(N_HEADS):
        ci = TAIL_GATE0 + 4 * (2 * d) + hd
        cf = ci + 4
        li_row, li_col = g_t[ci:ci + 1, :], g[:, ci:ci + 1]
        b_col, b_row = cum_col[:, cf:cf + 1], cum_row[cf:cf + 1, :]
        m_prev = m_s[d:d + 1, hd:hd + 1]
        log_d = jnp.where(valid, b_col - b_row + li_row, -jnp.inf)
        log_inter = b_col + m_prev
        m_t = jnp.maximum(log_inter, jnp.max(log_d, axis=1, keepdims=True))
        lo = hd * HEAD_DIM
        qh = qkv[:, lo:lo + HEAD_DIM]
        kh = qkv[:, 256 + lo:256 + lo + HEAD_DIM]
        vh = qkv[:, 512 + lo:512 + lo + HEAD_DIM]
        k_scale = HEAD_DIM ** -0.5
        w = _nt(qh, kh) * k_scale * jnp.exp(log_d - m_t)
        w_inter = jnp.exp(log_inter - m_t)
        c_mat = c_s[d, hd]
        n_vec = n_s[d, hd:hd + 1, :]
        num = (jnp.dot(w.astype(BF16), vh, preferred_element_type=F32)
               + w_inter * jnp.dot(qh, c_mat.astype(BF16), preferred_element_type=F32))
        den = jnp.sum(w, axis=1, keepdims=True) + w_inter * jnp.sum(qh.astype(F32) * n_vec, axis=1, keepdims=True)
        h_ref[:, lo:lo + HEAD_DIM] = num / jnp.maximum(jnp.abs(den), jnp.exp(-m_t))
        m_new = m_t[last:last + 1, :]
        b_last = b_col[last:last + 1, :]
        w_end = jnp.exp(b_last - b_col + li_col - m_new)
        decay = jnp.exp(b_last + m_prev - m_new)
        kw = kh.astype(F32) * k_scale * w_end
        c_s[d, hd] = decay * c_mat + _tn(kw.astype(BF16), vh)
        n_s[d, hd:hd + 1, :] = decay * n_vec + jnp.sum(kw, axis=0, keepdims=True)
        m_s[d:d + 1, hd:hd + 1] = m_new


def _mlstm_kernel(tab_ref, qkvf_ref, tailf_ref, qkvb_ref, tailb_ref, bias_ref, c0_ref, n0_ref, m0_ref,
                  hf_ref, hb_ref, cn_ref, nn_ref, mn_ref, c_s, n_s, m_s):
    i = pl.program_id(0)

    @pl.when(tab_ref[3, i] == 1)
    def _():
        c_s[...] = c0_ref[0]
        n_s[...] = n0_ref[0]
        m_s[...] = m0_ref[0]

    _mlstm_dir(0, qkvf_ref, tailf_ref, bias_ref, c_s, n_s, m_s, hf_ref)
    _mlstm_dir(1, qkvb_ref, tailb_ref, bias_ref, c_s, n_s, m_s, hb_ref)

    @pl.when(tab_ref[4, i] == 1)
    def _():
        cn_ref[0] = c_s[...]
        nn_ref[0] = n_s[...]
        mn_ref[0] = m_s[...]


def _ret_dir(d, qkv_ref, lg_ref, s_s, o_ref):
    t = lax.broadcasted_iota(jnp.int32, (CHUNK, CHUNK), 0)
    s = lax.broadcasted_iota(jnp.int32, (CHUNK, CHUNK), 1)
    lag = (t - s) if d == 0 else (s - t)
    lag_f = jnp.maximum(lag, 0).astype(F32)
    pos_c = lax.broadcasted_iota(jnp.int32, (CHUNK, 1), 0)
    pos_c = (pos_c if d == 0 else CHUNK - 1 - pos_c).astype(F32)
    qkv = qkv_ref[...]
    for hd in range(N_HEADS):
        lg = _log_sigmoid(lg_ref[d:d + 1, hd:hd + 1])
        intra = jnp.where(lag >= 0, jnp.exp(lag_f * lg), 0.0)
        inter = jnp.exp((pos_c + 1.0) * lg)
        tail = jnp.exp((CHUNK - 1.0 - pos_c) * lg)
        chunk_decay = jnp.exp(CHUNK * lg)
        lo = hd * HEAD_DIM
        qh = qkv[:, lo:lo + HEAD_DIM]
        kh = qkv[:, 256 + lo:256 + lo + HEAD_DIM]
        vh = qkv[:, 512 + lo:512 + lo + HEAD_DIM]
        k_scale = HEAD_DIM ** -0.5
        a = _nt(qh, kh) * k_scale * intra
        st = s_s[d, hd]
        o_ref[:, lo:lo + HEAD_DIM] = (jnp.dot(a.astype(BF16), vh, preferred_element_type=F32)
                                      + inter * jnp.dot(qh, st.astype(BF16), preferred_element_type=F32))
        kw = kh.astype(F32) * k_scale * tail
        s_s[d, hd] = chunk_decay * st + _tn(kw.astype(BF16), vh)


def _ret_kernel(tab_ref, qkvf_ref, qkvb_ref, lg_ref, s0_ref, of_ref, ob_ref, sn_ref, s_s):
    i = pl.program_id(0)

    @pl.when(tab_ref[3, i] == 1)
    def _():
        s_s[...] = s0_ref[0]

    _ret_dir(0, qkvf_ref, lg_ref, s_s, of_ref)
    _ret_dir(1, qkvb_ref, lg_ref, s_s, ob_ref)

    @pl.when(tab_ref[4, i] == 1)
    def _():
        sn_ref[0] = s_s[...]


def _scan_table(dims):
    rows = []
    seq = 0
    for base, nb, n in ((0, dims["bp"], dims["np"]), (dims["tp"], dims["bs"], dims["ns"])):
        nch = n // CHUNK
        for b in range(nb):
            blk0 = (base + b * n) // CHUNK
            for c in range(nch):
                rows.append((blk0 + c, blk0 + nch - 1 - c, seq, int(c == 0), int(c == nch - 1)))
            seq += 1
    return jnp.asarray(np.array(rows, dtype=np.int32).T)


def _mlstm_scan(tab, mqkv, tail, bias_row, c0, n0, m0):
    t_all = mqkv.shape[0]
    nseq = c0.shape[0]
    fwd = lambda i, tab: (tab[0, i], 0)
    bwd = lambda i, tab: (tab[1, i], 0)
    st = lambda nd: (lambda i, tab: (tab[2, i],) + (0,) * nd)
    grid_spec = pltpu.PrefetchScalarGridSpec(
        num_scalar_prefetch=1,
        grid=(tab.shape[1],),
        in_specs=[pl.BlockSpec((CHUNK, 768), fwd), pl.BlockSpec((CHUNK, LANES), fwd),
                  pl.BlockSpec((CHUNK, 768), bwd), pl.BlockSpec((CHUNK, LANES), bwd),
                  pl.BlockSpec((1, LANES), lambda i, tab: (0, 0)),
                  pl.BlockSpec((1,) + c0.shape[1:], st(4)),
                  pl.BlockSpec((1,) + n0.shape[1:], st(3)),
                  pl.BlockSpec((1,) + m0.shape[1:], st(2))],
        out_specs=[pl.BlockSpec((CHUNK, 256), fwd), pl.BlockSpec((CHUNK, 256), bwd),
                   pl.BlockSpec((1,) + c0.shape[1:], st(4)),
                   pl.BlockSpec((1,) + n0.shape[1:], st(3)),
                   pl.BlockSpec((1,) + m0.shape[1:], st(2))],
        scratch_shapes=[pltpu.VMEM(c0.shape[1:], F32), pltpu.VMEM(n0.shape[1:], F32),
                        pltpu.VMEM(m0.shape[1:], F32)],
    )
    return pl.pallas_call(
        _mlstm_kernel,
        grid_spec=grid_spec,
        out_shape=[jax.ShapeDtypeStruct((t_all, 256), F32), jax.ShapeDtypeStruct((t_all, 256), F32),
                   jax.ShapeDtypeStruct(c0.shape, F32), jax.ShapeDtypeStruct(n0.shape, F32),
                   jax.ShapeDtypeStruct(m0.shape, F32)],
        compiler_params=_cparams(("arbitrary",)),
        name="mlstm_scan",
    )(tab, mqkv, tail, mqkv, tail, bias_row, c0, n0, m0)


def _ret_scan(tab, rqkv, decay_logit, s0):
    t_all = rqkv.shape[0]
    fwd = lambda i, tab: (tab[0, i], 0)
    bwd = lambda i, tab: (tab[1, i], 0)
    st = lambda i, tab: (tab[2, i], 0, 0, 0, 0)
    grid_spec = pltpu.PrefetchScalarGridSpec(
        num_scalar_prefetch=1,
        grid=(tab.shape[1],),
        in_specs=[pl.BlockSpec((CHUNK, 768), fwd), pl.BlockSpec((CHUNK, 768), bwd),
                  pl.BlockSpec(decay_logit.shape, lambda i, tab: (0, 0)),
                  pl.BlockSpec((1,) + s0.shape[1:], st)],
        out_specs=[pl.BlockSpec((CHUNK, 256), fwd), pl.BlockSpec((CHUNK, 256), bwd),
                   pl.BlockSpec((1,) + s0.shape[1:], st)],
        scratch_shapes=[pltpu.VMEM(s0.shape[1:], F32)],
    )
    return pl.pallas_call(
        _ret_kernel,
        grid_spec=grid_spec,
        out_shape=[jax.ShapeDtypeStruct((t_all, 256), F32), jax.ShapeDtypeStruct((t_all, 256), F32),
                   jax.ShapeDtypeStruct(s0.shape, F32)],
        compiler_params=_cparams(("arbitrary",)),
        name="retention_scan",
    )(tab, rqkv, rqkv, decay_logit, s0)


def _proj_kernel(x_ref, w_ref, o_ref):
    o_ref[...] = _bdot(x_ref[...], w_ref[...]).astype(o_ref.dtype)


def _project(x, w, dtype):
    return pl.pallas_call(
        _proj_kernel,
        out_shape=jax.ShapeDtypeStruct((x.shape[0], w.shape[1]), dtype),
        name="ctx_kv_proj",
    )(x, w)


def _first_argmax_mask(cur, axis, size):
    io = lax.broadcasted_iota(jnp.int32, cur.shape, axis)
    mx = jnp.max(cur, axis=axis, keepdims=True)
    ix = jnp.min(jnp.where(cur == mx, io, size), axis=axis, keepdims=True)
    return io == ix


def _route(scores_t, bias_col):
    tm = scores_t.shape[1]
    per = N_EXPERTS // N_GROUPS
    sel = scores_t + bias_col
    s3 = sel.reshape(N_GROUPS, per, tm)
    hit1 = _first_argmax_mask(s3, 1, per)
    m1 = jnp.max(s3, axis=1, keepdims=True)
    m2 = jnp.max(jnp.where(hit1, -jnp.inf, s3), axis=1, keepdims=True)
    cur = m1 + m2
    gsel = None
    for _ in range(TOPK_GROUPS):
        hit = _first_argmax_mask(cur, 0, N_GROUPS)
        gsel = hit if gsel is None else jnp.logical_or(gsel, hit)
        cur = jnp.where(hit, -jnp.inf, cur)
    cur = jnp.where(gsel, s3, -jnp.inf).reshape(N_EXPERTS, tm)
    chosen = None
    for _ in range(TOP_K):
        hit = _first_argmax_mask(cur, 0, N_EXPERTS)
        chosen = hit if chosen is None else jnp.logical_or(chosen, hit)
        cur = jnp.where(hit, -jnp.inf, cur)
    w = jnp.where(chosen, scores_t, 0.0)
    return w / jnp.sum(w, axis=0, keepdims=True) * ROUTED_SCALE


def _out_kernel(oap_ref, oas_ref, odp_ref, ods_ref, hf_ref, hb_ref, mo_ref, of_ref, ob_ref, rg_ref,
                x_ref, mod_ref, mg_ref, npost_ref, npre_ref, wout_ref, wrt_ref, rb_ref,
                x1_ref, h2_ref, wd_ref, *, npt):
    i = pl.program_id(0)
    is_p = i < npt
    a = jnp.concatenate([jnp.where(is_p, oap_ref[hd], oas_ref[hd]) for hd in range(N_HEADS)], axis=-1)
    dd = jnp.concatenate([jnp.where(is_p, odp_ref[hd], ods_ref[hd]) for hd in range(N_HEADS)], axis=-1)
    b = _rms_heads(hf_ref[...] + hb_ref[...]) * mg_ref[...] * _sigmoid(mo_ref[...])
    r = _rms_heads(of_ref[...] + ob_ref[...]) * _silu(rg_ref[...])
    mix = (jnp.dot(a.astype(BF16), wout_ref[0:256, :], preferred_element_type=F32)
           + jnp.dot(b.astype(BF16), wout_ref[256:512, :], preferred_element_type=F32)
           + jnp.dot(r.astype(BF16), wout_ref[512:768, :], preferred_element_type=F32)
           + jnp.dot(dd.astype(BF16), wout_ref[768:1024, :], preferred_element_type=F32))
    m = mod_ref[0]
    x1 = x_ref[...] + m[2:3] * (_rms(mix) * npost_ref[...])
    x1_ref[...] = x1
    h2 = (_rms(x1) * npre_ref[...] * (1.0 + m[4:5]) + m[3:4]).astype(BF16)
    h2_ref[...] = h2
    logits_t = lax.dot_general(wrt_ref[...], h2, (((1,), (1,)), ((), ())), preferred_element_type=F32)
    w_t = _route(_sigmoid(logits_t), rb_ref[...])
    w_t = jnp.concatenate([w_t, jnp.zeros((LANES - N_EXPERTS, w_t.shape[1]), F32)], axis=0)
    wd_ref[...] = w_t.T


def _output_stage(oa, od, hf, hb, mo, of, ob, rg, x, mod, mg, npost, npre, wout, wrt, rb, dims):
    t_all = x.shape[0]
    npt, nst, tps = dims["tp"] // TM, (t_all - dims["tp"]) // TM, dims["ns"] // TM
    row = lambda i: (i, 0)
    const2 = lambda i: (0, 0)
    tok = lambda w: pl.BlockSpec((TM, w), row)
    full = lambda a: pl.BlockSpec(a.shape, const2)
    hp = pl.BlockSpec((N_HEADS, TM, V_HEAD), lambda i: (0, jnp.minimum(i, npt - 1), 0))
    hs = pl.BlockSpec((N_HEADS, TM, V_HEAD), lambda i: (0, jnp.clip(i - npt, 0, nst - 1), 0))
    return pl.pallas_call(
        functools.partial(_out_kernel, npt=npt),
        grid=(t_all // TM,),
        in_specs=[hp, hs, hp, hs, tok(256), tok(256), tok(256), tok(256), tok(256), tok(256),
                  tok(D_MODEL), pl.BlockSpec((1, 6, D_MODEL), lambda i: (_mod_row(i, npt, tps), 0, 0)),
                  full(mg), full(npost), full(npre), full(wout), full(wrt), full(rb)],
        out_specs=[tok(D_MODEL), tok(D_MODEL), tok(LANES)],
        out_shape=[jax.ShapeDtypeStruct((t_all, D_MODEL), F32),
                   jax.ShapeDtypeStruct((t_all, D_MODEL), BF16),
                   jax.ShapeDtypeStruct((t_all, LANES), F32)],
        compiler_params=_cparams(("parallel",)),
        name="output_stage",
    )(oa[0], oa[1], od[0], od[1], hf, hb, mo, of, ob, rg, x, mod, mg, npost, npre, wout, wrt, rb)


def _swiglu(h, wg, wu, wd):
    act = _silu(_bdot(h, wg)) * _bdot(h, wu)
    return _bdot(act, wd)


def _moe_kernel(h_ref, wd_ref, x1_ref, mod_ref, npost_ref, wg_ref, wu_ref, wdn_ref, sg_ref, su_ref, sd_ref,
                o_ref, acc):
    e = pl.program_id(1)
    h = h_ref[...]

    @pl.when(e == 0)
    def _():
        acc[...] = _swiglu(h, sg_ref[...], su_ref[...], sd_ref[...])

    lane = lax.broadcasted_iota(jnp.int32, wd_ref.shape, 1)
    wcol = jnp.sum(jnp.where(lane == e, wd_ref[...], 0.0), axis=1, keepdims=True)
    acc[...] += _swiglu(h, wg_ref[0], wu_ref[0], wdn_ref[0]) * wcol

    @pl.when(e == pl.num_programs(1) - 1)
    def _():
        m = mod_ref[0]
        o_ref[...] = x1_ref[...] + m[5:6] * (_rms(acc[...]) * npost_ref[...])


def _moe_stage(h2, wd, x1, mod, npost, wg, wu, wdn, sg, su, sd, dims):
    t_all = h2.shape[0]
    tm = TM_MOE if (dims["tp"] % TM_MOE == 0 and dims["ns"] % TM_MOE == 0) else TM
    npt, tps = dims["tp"] // tm, dims["ns"] // tm
    row = lambda i, e: (i, 0)
    const2 = lambda i, e: (0, 0)
    exp3 = lambda i, e: (e, 0, 0)
    full = lambda a: pl.BlockSpec(a.shape, const2)
    return pl.pallas_call(
        _moe_kernel,
        grid=(t_all // tm, N_EXPERTS),
        in_specs=[pl.BlockSpec((tm, D_MODEL), row), pl.BlockSpec((tm, LANES), row),
                  pl.BlockSpec((tm, D_MODEL), row),
                  pl.BlockSpec((1, 6, D_MODEL), lambda i, e: (_mod_row(i, npt, tps), 0, 0)),
                  full(npost),
                  pl.BlockSpec((1, D_MODEL, D_EXPERT), exp3), pl.BlockSpec((1, D_MODEL, D_EXPERT), exp3),
                  pl.BlockSpec((1, D_EXPERT, D_MODEL), exp3),
                  full(sg), full(su), full(sd)],
        out_specs=pl.BlockSpec((tm, D_MODEL), row),
        out_shape=jax.ShapeDtypeStruct((t_all, D_MODEL), F32),
        scratch_shapes=[pltpu.VMEM((tm, D_MODEL), F32)],
        compiler_params=_cparams(("parallel", "arbitrary")),
        name="moe_experts",
    )(h2, wd, x1, mod, npost, wg, wu, wdn, sg, su, sd)


_PERM32 = np.concatenate([np.arange(8, 16), np.arange(0, 8), np.arange(24, 32), np.arange(16, 24)])


def _rope_tables(dims):
    n = dims["ns"]
    pos = jnp.arange(n)
    quarter = QK_ROPE // 4
    inv = 1.0 / (ROPE_THETA ** (jnp.arange(quarter, dtype=F32) / quarter))
    ang_r = (pos // GRID_W).astype(F32)[:, None] * inv[None, :]
    ang_c = (pos % GRID_W).astype(F32)[:, None] * inv[None, :]
    cos32 = jnp.concatenate([jnp.cos(ang_r)] * 2 + [jnp.cos(ang_c)] * 2, axis=-1)
    sin32 = jnp.concatenate([-jnp.sin(ang_r), jnp.sin(ang_r), -jnp.sin(ang_c), jnp.sin(ang_c)], axis=-1)
    cos32 = jnp.concatenate([jnp.ones((dims["tp"], 32), F32)] + [cos32] * dims["bs"], axis=0)
    sin32 = jnp.concatenate([jnp.zeros((dims["tp"], 32), F32)] + [sin32] * dims["bs"], axis=0)
    t_all = cos32.shape[0]
    cos256, sin256 = jnp.tile(cos32, (1, 8)), jnp.tile(sin32, (1, 8))
    one, zero = jnp.ones((t_all, QK_NOPE), F32), jnp.zeros((t_all, QK_NOPE), F32)
    cosq = jnp.concatenate([one, cos32] * N_HEADS, axis=-1)
    sinq = jnp.concatenate([zero, sin32] * N_HEADS, axis=-1)
    return cos256, sin256, cosq, sinq


def _split_w_in(w_in_l):
    sizes = (256, 256, 256, 256, 256, 256, 256, 16, 256, 256, 256, 256, Q_LORA, KV_LORA, QK_ROPE)
    idx = np.cumsum(sizes)[:-1]
    return jnp.split(w_in_l, [int(v) for v in idx], axis=-1)


def _layer_weights(w_in_l, w_uq_l):
    (a_q, a_k, a_v, m_q, m_k, m_v, m_o, m_g, r_q, r_k, r_v, r_g, d_cq, d_ckv, d_kr) = _split_w_in(w_in_l)
    perm256 = np.concatenate([_PERM32 + 32 * j for j in range(8)])
    pad = lambda w, n: jnp.pad(w, ((0, 0), (0, n - w.shape[1])))
    tail = pad(jnp.concatenate([d_kr, m_g], axis=-1), LANES)
    tail_s = pad(d_kr[:, _PERM32], LANES)
    w_all = jnp.concatenate([a_q, a_k, a_v, m_q, m_k, m_v, m_o, r_q, r_k, r_v, r_g, d_cq, d_ckv, tail,
                             a_q[:, perm256], a_k[:, perm256], tail_s], axis=-1).astype(BF16)
    per = QK_NOPE + QK_ROPE
    permq = np.concatenate([np.concatenate([np.arange(QK_NOPE), QK_NOPE + _PERM32]) + per * j
                            for j in range(N_HEADS)])
    return w_all, w_uq_l.astype(BF16), w_uq_l[:, permq].astype(BF16)


def kernel(x_prompt, x_sample, cache_diff_k, cache_diff_v, state_mlstm_C, state_mlstm_n, state_mlstm_m, state_ret_S, cache_mla_ckv, cache_mla_krope, c, c_ctx, w_mod, b_mod, norm_pre, norm_post, w_in, w_out, diff_lambda, diff_norm, mlstm_gate_bias, mlstm_norm, ret_decay_logit, mla_q_norm, mla_w_uq, mla_kv_norm, mla_w_ukv, moe_w_router, moe_router_bias, moe_w_gate, moe_w_up, moe_w_down, shared_w_gate, shared_w_up, shared_w_down):
    bp, n_p, _ = x_prompt.shape
    bs, n_s, _ = x_sample.shape
    depth = w_in.shape[0]
    past = cache_diff_k.shape[2]
    dims = dict(bp=bp, np=n_p, bs=bs, ns=n_s, tp=bp * n_p, past=past)
    tp, ts = bp * n_p, bs * n_s
    assert n_p % TM == 0 and n_s % TM == 0 and n_p % CHUNK == 0 and n_s % min(TK, n_s) == 0
    assert past % min(TK, past) == 0 and tp % n_s == 0 and bs + 1 <= 8 and n_s % GRID_W == 0

    x = jnp.concatenate([x_prompt.reshape(tp, D_MODEL), x_sample.reshape(ts, D_MODEL)], axis=0)
    cond = jnp.zeros((8, D_MODEL), F32).at[0].set(c_ctx).at[1:1 + bs].set(c)
    mod_all = _modulation(cond, w_mod, b_mod).reshape(depth, 8, 6, D_MODEL)
    tabs = _rope_tables(dims)
    scan_tab = _scan_table(dims)

    outs = [[] for _ in range(8)]
    for l in range(depth):
        lam_init = 0.8 - 0.6 * math.exp(-0.3 * l)
        mod = mod_all[l]
        w_all, wuq, wuqs = _layer_weights(w_in[l], mla_w_uq[l])
        wukv = mla_w_ukv[l].astype(BF16)
        (aq1, aq2, ak1, ak2, avh, ak, av, mqkv, mo, rqkv, rg, tail, tailr, qmla, ckv, kv) = _input_stage(
            x, mod, norm_pre[l, 0:1], w_all, tabs, mla_q_norm[l][None], wuq, wuqs,
            mla_kv_norm[l][None], wukv, dims)

        lamv, dg = diff_lambda[l], diff_norm[l][None]
        ck = jnp.transpose(cache_diff_k[:, l], (2, 0, 1, 3)).reshape(N_HEADS, bs * past, HEAD_DIM).astype(BF16)
        cv = jnp.transpose(cache_diff_v[:, l], (2, 0, 1, 3)).reshape(N_HEADS, bs * past, HEAD_DIM).astype(BF16)
        diff = lambda n_ctx: functools.partial(_diff_attn_kernel, n_new=n_p if not n_ctx else n_s, n_ctx=n_ctx,
                                               lam_init=lam_init)
        oa_p = _attention(diff(0), [aq1, aq2], [ak1, ak2], avh, None, None, [lamv, dg], 0, bp, n_p, 0)
        oa_s = _attention(diff(past), [aq1, aq2], [ak1, ak2], avh,
                          [ck[..., :DIFF_HALF], ck[..., DIFF_HALF:]], cv, [lamv, dg], tp, bs, n_s, past)

        kv4 = kv.reshape(-1, N_HEADS, QK_NOPE + V_HEAD)
        kr = jnp.broadcast_to(tailr[:, None, :QK_ROPE].astype(BF16), (kv4.shape[0], N_HEADS, QK_ROPE))
        kmla = jnp.transpose(jnp.concatenate([kv4[..., :QK_NOPE], kr], axis=-1), (1, 0, 2))
        vmla = jnp.transpose(kv4[..., QK_NOPE:], (1, 0, 2))
        ckv_ctx = cache_mla_ckv[:, l].reshape(bs * past, KV_LORA)
        kvc = _project(ckv_ctx, wukv, BF16).reshape(bs * past, N_HEADS, QK_NOPE + V_HEAD)
        krc = jnp.broadcast_to(cache_mla_krope[:, l].reshape(bs * past, 1, QK_ROPE).astype(BF16),
                               (bs * past, N_HEADS, QK_ROPE))
        kc = jnp.transpose(jnp.concatenate([kvc[..., :QK_NOPE], krc], axis=-1), (1, 0, 2))
        vc = jnp.transpose(kvc[..., QK_NOPE:], (1, 0, 2))
        mla = lambda n_ctx: functools.partial(_mla_attn_kernel, n_new=n_p if not n_ctx else n_s, n_ctx=n_ctx)
        od_p = _attention(mla(0), [qmla], [kmla], vmla, None, None, [], 0, bp, n_p, 0)
        od_s = _attention(mla(past), [qmla], [kmla], vmla, [kc], vc, [], tp, bs, n_s, past)

        zeros = lambda a: jnp.zeros((bp,) + a.shape[1:], F32)
        st_c, st_n, st_m, st_s = state_mlstm_C[:, l], state_mlstm_n[:, l], state_mlstm_m[:, l], state_ret_S[:, l]
        bias_row = jnp.zeros((1, LANES), F32).at[0, TAIL_GATE0:TAIL_GATE0 + 16].set(mlstm_gate_bias[l].reshape(16))
        hf, hb, c_n, n_n, m_n = _mlstm_scan(scan_tab, mqkv, tail, bias_row,
                                            jnp.concatenate([zeros(st_c), st_c]),
                                            jnp.concatenate([zeros(st_n), st_n]),
                                            jnp.concatenate([zeros(st_m), st_m]))
        of, ob, s_n = _ret_scan(scan_tab, rqkv, ret_decay_logit[l], jnp.concatenate([zeros(st_s), st_s]))

        x1, h2, wd = _output_stage((oa_p, oa_s), (od_p, od_s), hf, hb, mo, of, ob, rg, x, mod,
                                   mlstm_norm[l][None], norm_post[l, 0:1], norm_pre[l, 1:2],
                                   w_out[l].astype(BF16), moe_w_router[l].T.astype(BF16),
                                   moe_router_bias[l][:, None], dims)
        x = _moe_stage(h2, wd, x1, mod, norm_post[l, 1:2], moe_w_gate[l], moe_w_up[l], moe_w_down[l],
                       shared_w_gate[l], shared_w_up[l], shared_w_down[l], dims)

        outs[0].append(ak[:tp].reshape(bp, n_p, N_HEADS, HEAD_DIM))
        outs[1].append(av[:tp].reshape(bp, n_p, N_HEADS, HEAD_DIM))
        outs[2].append(c_n[:bp])
        outs[3].append(n_n[:bp])
        outs[4].append(m_n[:bp])
        outs[5].append(s_n[:bp])
        outs[6].append(ckv[:tp].reshape(bp, n_p, KV_LORA))
        outs[7].append(tail[:tp, :QK_ROPE].reshape(bp, n_p, QK_ROPE))

    return (x[:tp].reshape(bp, n_p, D_MODEL), x[tp:].reshape(bs, n_s, D_MODEL)) + tuple(
        jnp.stack(o, axis=1) for o in outs)
```

```python
import functools
import math

import numpy as np
import jax
import jax.numpy as jnp
from jax import lax
from jax.experimental import pallas as pl
from jax.experimental.pallas import tpu as pltpu

F32 = jnp.float32
BF16 = jnp.bfloat16
HIGHEST = lax.Precision.HIGHEST

D_MODEL = 1024
GRID_W = 64
GROUP_WIDTH = 256
HEAD_DIM = 64
N_HEADS = 4
DIFF_HALF = 32
ROPE_THETA = 10000.0
Q_LORA = 256
KV_LORA = 128
QK_NOPE = 64
QK_ROPE = 32
V_HEAD = 64
N_EXPERTS = 64
TOP_K = 8
N_GROUPS = 8
TOPK_GROUPS = 4
D_EXPERT = 256
ROUTED_SCALE = 2.5
CHUNK = 128
EPS = 1e-6
Q_MLA = N_HEADS * (QK_NOPE + QK_ROPE)
KV_MLA = N_HEADS * (QK_NOPE + V_HEAD)

LANES = 128
VMEM_LIMIT = 56 * 1024 * 1024

TM = 256
TM_MOE = 1024
TQ = 256
TK = 512

C_AQ, C_AK, C_AV = 0, 256, 512
C_MQKV, C_MO = 768, 1536
C_RQKV, C_RG = 1792, 2560
C_CQ, C_CKV, C_TAIL = 2816, 3072, 3200
C_AQS, C_AKS, C_TAILS = 3328, 3584, 3840
W_ALL = 3968
TAIL_GATE0 = QK_ROPE


def _cparams(sem):
    return pltpu.CompilerParams(dimension_semantics=sem, vmem_limit_bytes=VMEM_LIMIT)


def _rms(x):
    return x * lax.rsqrt(jnp.mean(x * x, axis=-1, keepdims=True) + EPS)


def _head_mean_matrix(width):
    r = lax.broadcasted_iota(jnp.int32, (width, width), 0) // HEAD_DIM
    c = lax.broadcasted_iota(jnp.int32, (width, width), 1) // HEAD_DIM
    return jnp.where(r == c, 1.0 / HEAD_DIM, 0.0).astype(F32)


def _rms_heads(x):
    ms = jnp.dot(x * x, _head_mean_matrix(x.shape[-1]), precision=HIGHEST, preferred_element_type=F32)
    return x * lax.rsqrt(ms + EPS)


def _sigmoid(x):
    return 1.0 / (1.0 + jnp.exp(-x))


def _silu(x):
    return x * _sigmoid(x)


def _log_sigmoid(x):
    return jnp.minimum(x, 0.0) - jnp.log1p(jnp.exp(-jnp.abs(x)))


def _bdot(a, b):
    return jnp.dot(a.astype(BF16), b.astype(BF16), preferred_element_type=F32)


def _mod_kernel(c_ref, w_ref, b_ref, o_ref):
    o_ref[0] = _bdot(_silu(c_ref[...]), w_ref[0]) + b_ref[0]


def _modulation(cond, w_mod, b_mod):
    depth, _, n = w_mod.shape
    tn = 1536
    return pl.pallas_call(
        _mod_kernel,
        grid=(depth, n // tn),
        in_specs=[pl.BlockSpec((8, D_MODEL), lambda l, j: (0, 0)),
                  pl.BlockSpec((1, D_MODEL, tn), lambda l, j: (l, 0, j)),
                  pl.BlockSpec((1, 1, tn), lambda l, j: (l, 0, j))],
        out_specs=pl.BlockSpec((1, 8, tn), lambda l, j: (l, 0, j)),
        out_shape=jax.ShapeDtypeStruct((depth, 8, n), F32),
        compiler_params=_cparams(("parallel", "parallel")),
        name="adaln_mod",
    )(cond, w_mod, b_mod.reshape(depth, 1, n))


def _in_kernel(x_ref, mod_ref, npre_ref, w_ref, cos_ref, sin_ref, cosq_ref, sinq_ref,
               qg_ref, wuq_ref, wuqs_ref, kvg_ref, wukv_ref,
               aq1_ref, aq2_ref, ak1t_ref, ak2t_ref, avh_ref, ak_ref, av_ref,
               mqkv_ref, mo_ref, rqkv_ref, rg_ref, tail_ref,
               qmla_ref, ckv_ref, kmlat_ref, vmla_ref):
    x = x_ref[...]
    m = mod_ref[0]
    h = (_rms(x) * npre_ref[...] * (1.0 + m[1:2]) + m[0:1]).astype(BF16)

    def proj(c0, width):
        return jnp.dot(h, w_ref[:, c0:c0 + width], preferred_element_type=F32)

    cos = cos_ref[...]
    sin = sin_ref[...]
    aq = (proj(C_AQ, 256) * cos + proj(C_AQS, 256) * sin) * (DIFF_HALF ** -0.5)
    ak = proj(C_AK, 256) * cos + proj(C_AKS, 256) * sin
    av = proj(C_AV, 256)
    ak_ref[...] = ak
    av_ref[...] = av
    ak_t = ak.T.astype(BF16)
    for hd in range(N_HEADS):
        lo = hd * HEAD_DIM
        aq1_ref[hd] = aq[:, lo:lo + DIFF_HALF].astype(BF16)
        aq2_ref[hd] = aq[:, lo + DIFF_HALF:lo + HEAD_DIM].astype(BF16)
        ak1t_ref[hd] = ak_t[lo:lo + DIFF_HALF, :]
        ak2t_ref[hd] = ak_t[lo + DIFF_HALF:lo + HEAD_DIM, :]
        avh_ref[hd] = av[:, lo:lo + HEAD_DIM].astype(BF16)

    mqkv_ref[...] = proj(C_MQKV, 768).astype(BF16)
    mo_ref[...] = proj(C_MO, 256)
    rqkv_ref[...] = proj(C_RQKV, 768).astype(BF16)
    rg_ref[...] = proj(C_RG, 256)

    tail = proj(C_TAIL, LANES)
    tail_ref[...] = tail
    kr_t = (tail * cos[:, :LANES] + proj(C_TAILS, LANES) * sin[:, :LANES]).T[:QK_ROPE, :].astype(BF16)

    cqn = (_rms(proj(C_CQ, Q_LORA)) * qg_ref[...]).astype(BF16)
    q = jnp.dot(cqn, wuq_ref[...], preferred_element_type=F32)
    qs = jnp.dot(cqn, wuqs_ref[...], preferred_element_type=F32)
    qmla = (q * cosq_ref[...] + qs * sinq_ref[...]) * ((QK_NOPE + QK_ROPE) ** -0.5)
    for hd in range(N_HEADS):
        lo = hd * (QK_NOPE + QK_ROPE)
        qmla_ref[hd] = qmla[:, lo:lo + QK_NOPE + QK_ROPE].astype(BF16)
    ckvn = _rms(proj(C_CKV, KV_LORA)) * kvg_ref[...]
    ckv_ref[...] = ckvn
    kv = jnp.dot(ckvn.astype(BF16), wukv_ref[...], preferred_element_type=F32)
    kv_t = kv.T.astype(BF16)
    per = QK_NOPE + V_HEAD
    for hd in range(N_HEADS):
        kmlat_ref[hd, 0:QK_NOPE, :] = kv_t[hd * per:hd * per + QK_NOPE, :]
        kmlat_ref[hd, QK_NOPE:QK_NOPE + QK_ROPE, :] = kr_t
        vmla_ref[hd] = kv[:, hd * per + QK_NOPE:(hd + 1) * per].astype(BF16)


def _mod_row(i, npt, tps):
    return jnp.where(i < npt, 0, 1 + (i - npt) // tps)


def _input_stage(x, mod, npre, w_all, tabs, qg, wuq, wuqs, kvg, wukv, dims):
    t_all = x.shape[0]
    npt, tps = dims["tp"] // TM, dims["ns"] // TM
    row = lambda i: (i, 0)
    hrow = lambda i: (0, i, 0)
    const2 = lambda i: (0, 0)
    tok = lambda w: pl.BlockSpec((TM, w), row)
    headed = lambda w: pl.BlockSpec((N_HEADS, TM, w), hrow)
    headed_t = lambda d: pl.BlockSpec((N_HEADS, d, TM), lambda i: (0, 0, i))
    full = lambda a: pl.BlockSpec(a.shape, const2)
    cos, sin, cosq, sinq = tabs
    out_shapes = [
        (headed(DIFF_HALF), (N_HEADS, t_all, DIFF_HALF), BF16),
        (headed(DIFF_HALF), (N_HEADS, t_all, DIFF_HALF), BF16),
        (headed_t(DIFF_HALF), (N_HEADS, DIFF_HALF, t_all), BF16),
        (headed_t(DIFF_HALF), (N_HEADS, DIFF_HALF, t_all), BF16),
        (headed(HEAD_DIM), (N_HEADS, t_all, HEAD_DIM), BF16),
        (tok(256), (t_all, 256), F32),
        (tok(256), (t_all, 256), F32),
        (tok(768), (t_all, 768), BF16),
        (tok(256), (t_all, 256), F32),
        (tok(768), (t_all, 768), BF16),
        (tok(256), (t_all, 256), F32),
        (tok(LANES), (t_all, LANES), F32),
        (headed(QK_NOPE + QK_ROPE), (N_HEADS, t_all, QK_NOPE + QK_ROPE), BF16),
        (tok(KV_LORA), (t_all, KV_LORA), F32),
        (headed_t(QK_NOPE + QK_ROPE), (N_HEADS, QK_NOPE + QK_ROPE, t_all), BF16),
        (headed(V_HEAD), (N_HEADS, t_all, V_HEAD), BF16),
    ]
    return pl.pallas_call(
        _in_kernel,
        grid=(t_all // TM,),
        in_specs=[tok(D_MODEL),
                  pl.BlockSpec((1, 6, D_MODEL), lambda i: (_mod_row(i, npt, tps), 0, 0)),
                  full(npre), full(w_all), tok(256), tok(256), tok(Q_MLA), tok(Q_MLA),
                  full(qg), full(wuq), full(wuqs), full(kvg), full(wukv)],
        out_specs=[s for s, _, _ in out_shapes],
        out_shape=[jax.ShapeDtypeStruct(shp, dt) for _, shp, dt in out_shapes],
        compiler_params=_cparams(("parallel",)),
        name="input_stage",
    )(x, mod, npre, w_all, cos, sin, cosq, sinq, qg, wuq, wuqs, kvg, wukv)


def _attn_kernel(*refs, n_soft, n_new, n_ctx, lam_init):
    refs = list(refs)
    q_refs, kt_refs, v_ref = refs[:n_soft], refs[n_soft:2 * n_soft], refs[2 * n_soft]
    pos = 2 * n_soft + 1
    if n_ctx:
        ckt_refs, cv_ref = refs[pos:pos + n_soft], refs[pos + n_soft]
        pos += n_soft + 1
    if n_soft == 2:
        lam_ref, g_ref = refs[pos:pos + 2]
        pos += 2
    o_ref = refs[pos]
    tq = o_ref.shape[0]
    nchain = n_soft * N_HEADS

    def chunk(state, kt_of, v_of):
        ms, ls, accs = state
        new_m, new_l, new_acc = list(ms), list(ls), list(accs)
        for hd in range(N_HEADS):
            v = v_of(hd)
            for j in range(n_soft):
                c = j * N_HEADS + hd
                s = jnp.dot(q_refs[j][hd], kt_of(j, hd), preferred_element_type=F32)
                m_new = jnp.maximum(ms[c], jnp.broadcast_to(jnp.max(s, axis=-1, keepdims=True), (tq, LANES)))
                p = jnp.exp(s - jnp.tile(m_new, (1, s.shape[1] // LANES)))
                alpha = jnp.exp(ms[c] - m_new)
                new_l[c] = alpha * ls[c] + jnp.broadcast_to(jnp.sum(p, axis=-1, keepdims=True), (tq, LANES))
                new_acc[c] = alpha[:, :V_HEAD] * accs[c] + jnp.dot(p.astype(BF16), v, preferred_element_type=F32)
                new_m[c] = m_new
        return tuple(new_m), tuple(new_l), tuple(new_acc)

    state = (tuple(jnp.full((tq, LANES), -jnp.inf, F32) for _ in range(nchain)),
             tuple(jnp.zeros((tq, LANES), F32) for _ in range(nchain)),
             tuple(jnp.zeros((tq, V_HEAD), F32) for _ in range(nchain)))
    if n_ctx:
        cstep = min(TK, n_ctx)
        for i in range(n_ctx // cstep):
            state = chunk(state, lambda j, hd, i=i: ckt_refs[j][hd, :, i * cstep:(i + 1) * cstep],
                          lambda hd, i=i: cv_ref[hd, i * cstep:(i + 1) * cstep, :])
    step = min(TK, n_new)

    def body(i, st):
        start = pl.multiple_of(i * step, step)
        return chunk(st, lambda j, hd: kt_refs[j][hd, :, pl.ds(start, step)],
                     lambda hd: v_ref[hd, pl.ds(start, step), :])

    _, ls, accs = lax.fori_loop(0, n_new // step, body, state)

    if n_soft == 2:
        lv = lam_ref[...]
        lam = (jnp.exp(jnp.sum(lv[0:1] * lv[1:2], axis=-1, keepdims=True))
               - jnp.exp(jnp.sum(lv[2:3] * lv[3:4], axis=-1, keepdims=True)) + lam_init)
    for hd in range(N_HEADS):
        out = accs[hd] / ls[hd][:, :V_HEAD]
        if n_soft == 2:
            a = out - lam * (accs[N_HEADS + hd] / ls[N_HEADS + hd][:, :V_HEAD])
            out = _rms(a) * g_ref[...] * (1.0 - lam_init)
        o_ref[:, hd * V_HEAD:(hd + 1) * V_HEAD] = out


def _attention(name, qs, kts, v, ctx_kts, ctx_v, extras, row0, nb, n, n_ctx, lam_init=0.0):
    tq = min(TQ, n)
    nqt = n // tq
    n_soft = len(qs)
    qmap = lambda b, i: (0, row0 // tq + b * nqt + i, 0)
    in_specs = [pl.BlockSpec((N_HEADS, tq, a.shape[-1]), qmap) for a in qs]
    in_specs += [pl.BlockSpec((N_HEADS, a.shape[1], n), lambda b, i: (0, 0, row0 // n + b)) for a in kts]
    in_specs += [pl.BlockSpec((N_HEADS, n, V_HEAD), lambda b, i: (0, row0 // n + b, 0))]
    args = list(qs) + list(kts) + [v]
    if n_ctx:
        in_specs += [pl.BlockSpec((N_HEADS, a.shape[1], n_ctx), lambda b, i: (0, 0, b)) for a in ctx_kts]
        in_specs += [pl.BlockSpec((N_HEADS, n_ctx, V_HEAD), lambda b, i: (0, b, 0))]
        args += list(ctx_kts) + [ctx_v]
    in_specs += [pl.BlockSpec(a.shape, lambda b, i: (0, 0)) for a in extras]
    args += list(extras)
    return pl.pallas_call(
        functools.partial(_attn_kernel, n_soft=n_soft, n_new=n, n_ctx=n_ctx, lam_init=lam_init),
        grid=(nb, nqt),
        in_specs=in_specs,
        out_specs=pl.BlockSpec((tq, N_HEADS * V_HEAD), lambda b, i: (b * nqt + i, 0)),
        out_shape=jax.ShapeDtypeStruct((nb * n, N_HEADS * V_HEAD), F32),
        compiler_params=_cparams(("parallel", "parallel")),
        name=name,
    )(*args)


def _tri(lower):
    r = lax.broadcasted_iota(jnp.int32, (CHUNK, CHUNK), 0)
    c = lax.broadcasted_iota(jnp.int32, (CHUNK, CHUNK), 1)
    return (c <= r) if lower else (c >= r)


def _nt(a, b):
    return lax.dot_general(a, b, (((1,), (1,)), ((), ())), preferred_element_type=F32)


def _tn(a, b):
    return lax.dot_general(a, b, (((0,), (0,)), ((), ())), preferred_element_type=F32)


def _mlstm_dir(d, qkv_ref, tail_ref, bias_ref, c_s, n_s, m_s, h_ref):
    g = tail_ref[...] + bias_ref[...]
    ls = _log_sigmoid(g)
    g_t, ls_t = g.T, ls.T
    lower = _tri(True).astype(F32)
    upper = _tri(False).astype(F32)
    left, right = (lower, upper) if d == 0 else (upper, lower)
    cum_col = jnp.dot(left, ls, precision=HIGHEST, preferred_element_type=F32)
    cum_row = jnp.dot(ls_t, right, precision=HIGHEST, preferred_element_type=F32)
    valid = _tri(d == 0)
    last = CHUNK - 1 if d == 0 else 0
    qkv = qkv_ref[...]
    for hd in range(N_HEADS):
        ci = TAIL_GATE0 + 4 * (2 * d) + hd
        cf = ci + 4
        li_row, li_col = g_t[ci:ci + 1, :], g[:, ci:ci + 1]
        b_col, b_row = cum_col[:, cf:cf + 1], cum_row[cf:cf + 1, :]
        m_prev = m_s[d:d + 1, hd:hd + 1]
        log_d = jnp.where(valid, b_col - b_row + li_row, -jnp.inf)
        log_inter = b_col + m_prev
        m_t = jnp.maximum(log_inter, jnp.max(log_d, axis=1, keepdims=True))
        lo = hd * HEAD_DIM
        qh = qkv[:, lo:lo + HEAD_DIM]
        kh = qkv[:, 256 + lo:256 + lo + HEAD_DIM]
        vh = qkv[:, 512 + lo:512 + lo + HEAD_DIM]
        k_scale = HEAD_DIM ** -0.5
        w = _nt(qh, kh) * k_scale * jnp.exp(log_d - m_t)
        w_inter = jnp.exp(log_inter - m_t)
        c_mat = c_s[d, hd]
        n_vec = n_s[d, hd:hd + 1, :]
        num = (jnp.dot(w.astype(BF16), vh, preferred_element_type=F32)
               + w_inter * jnp.dot(qh, c_mat.astype(BF16), preferred_element_type=F32))
        den = jnp.sum(w, axis=1, keepdims=True) + w_inter * jnp.sum(qh.astype(F32) * n_vec, axis=1, keepdims=True)
        h_ref[:, lo:lo + HEAD_DIM] = num / jnp.maximum(jnp.abs(den), jnp.exp(-m_t))
        m_new = m_t[last:last + 1, :]
        b_last = b_col[last:last + 1, :]
        w_end = jnp.exp(b_last - b_col + li_col - m_new)
        decay = jnp.exp(b_last + m_prev - m_new)
        kw = kh.astype(F32) * k_scale * w_end
        c_s[d, hd] = decay * c_mat + _tn(kw.astype(BF16), vh)
        n_s[d, hd:hd + 1, :] = decay * n_vec + jnp.sum(kw, axis=0, keepdims=True)
        m_s[d:d + 1, hd:hd + 1] = m_new


def _mlstm_kernel(tab_ref, qkvf_ref, tailf_ref, qkvb_ref, tailb_ref, bias_ref, c0_ref, n0_ref, m0_ref,
                  hf_ref, hb_ref, cn_ref, nn_ref, mn_ref, c_s, n_s, m_s):
    i = pl.program_id(0)

    @pl.when(tab_ref[3, i] == 1)
    def _():
        c_s[...] = c0_ref[0]
        n_s[...] = n0_ref[0]
        m_s[...] = m0_ref[0]

    _mlstm_dir(0, qkvf_ref, tailf_ref, bias_ref, c_s, n_s, m_s, hf_ref)
    _mlstm_dir(1, qkvb_ref, tailb_ref, bias_ref, c_s, n_s, m_s, hb_ref)

    @pl.when(tab_ref[4, i] == 1)
    def _():
        cn_ref[0] = c_s[...]
        nn_ref[0] = n_s[...]
        mn_ref[0] = m_s[...]


def _ret_dir(d, qkv_ref, lg_ref, s_s, o_ref):
    t = lax.broadcasted_iota(jnp.int32, (CHUNK, CHUNK), 0)
    s = lax.broadcasted_iota(jnp.int32, (CHUNK, CHUNK), 1)
    lag = (t - s) if d == 0 else (s - t)
    lag_f = jnp.maximum(lag, 0).astype(F32)
    pos_c = lax.broadcasted_iota(jnp.int32, (CHUNK, 1), 0)
    pos_c = (pos_c if d == 0 else CHUNK - 1 - pos_c).astype(F32)
    qkv = qkv_ref[...]
    for hd in range(N_HEADS):
        lg = _log_sigmoid(lg_ref[d:d + 1, hd:hd + 1])
        intra = jnp.where(lag >= 0, jnp.exp(lag_f * lg), 0.0)
        inter = jnp.exp((pos_c + 1.0) * lg)
        tail = jnp.exp((CHUNK - 1.0 - pos_c) * lg)
        chunk_decay = jnp.exp(CHUNK * lg)
        lo = hd * HEAD_DIM
        qh = qkv[:, lo:lo + HEAD_DIM]
        kh = qkv[:, 256 + lo:256 + lo + HEAD_DIM]
        vh = qkv[:, 512 + lo:512 + lo + HEAD_DIM]
        k_scale = HEAD_DIM ** -0.5
        a = _nt(qh, kh) * k_scale * intra
        st = s_s[d, hd]
        o_ref[:, lo:lo + HEAD_DIM] = (jnp.dot(a.astype(BF16), vh, preferred_element_type=F32)
                                      + inter * jnp.dot(qh, st.astype(BF16), preferred_element_type=F32))
        kw = kh.astype(F32) * k_scale * tail
        s_s[d, hd] = chunk_decay * st + _tn(kw.astype(BF16), vh)


def _ret_kernel(tab_ref, qkvf_ref, qkvb_ref, lg_ref, s0_ref, of_ref, ob_ref, sn_ref, s_s):
    i = pl.program_id(0)

    @pl.when(tab_ref[3, i] == 1)
    def _():
        s_s[...] = s0_ref[0]

    _ret_dir(0, qkvf_ref, lg_ref, s_s, of_ref)
    _ret_dir(1, qkvb_ref, lg_ref, s_s, ob_ref)

    @pl.when(tab_ref[4, i] == 1)
    def _():
        sn_ref[0] = s_s[...]


def _scan_table(dims):
    rows = []
    seq = 0
    for base, nb, n in ((0, dims["bp"], dims["np"]), (dims["tp"], dims["bs"], dims["ns"])):
        nch = n // CHUNK
        for b in range(nb):
            blk0 = (base + b * n) // CHUNK
            for c in range(nch):
                rows.append((blk0 + c, blk0 + nch - 1 - c, seq, int(c == 0), int(c == nch - 1)))
            seq += 1
    return jnp.asarray(np.array(rows, dtype=np.int32).T)


def _mlstm_scan(tab, mqkv, tail, bias_row, c0, n0, m0):
    t_all = mqkv.shape[0]
    nseq = c0.shape[0]
    fwd = lambda i, tab: (tab[0, i], 0)
    bwd = lambda i, tab: (tab[1, i], 0)
    st = lambda nd: (lambda i, tab: (tab[2, i],) + (0,) * nd)
    grid_spec = pltpu.PrefetchScalarGridSpec(
        num_scalar_prefetch=1,
        grid=(tab.shape[1],),
        in_specs=[pl.BlockSpec((CHUNK, 768), fwd), pl.BlockSpec((CHUNK, LANES), fwd),
                  pl.BlockSpec((CHUNK, 768), bwd), pl.BlockSpec((CHUNK, LANES), bwd),
                  pl.BlockSpec((1, LANES), lambda i, tab: (0, 0)),
                  pl.BlockSpec((1,) + c0.shape[1:], st(4)),
                  pl.BlockSpec((1,) + n0.shape[1:], st(3)),
                  pl.BlockSpec((1,) + m0.shape[1:], st(2))],
        out_specs=[pl.BlockSpec((CHUNK, 256), fwd), pl.BlockSpec((CHUNK, 256), bwd),
                   pl.BlockSpec((1,) + c0.shape[1:], st(4)),
                   pl.BlockSpec((1,) + n0.shape[1:], st(3)),
                   pl.BlockSpec((1,) + m0.shape[1:], st(2))],
        scratch_shapes=[pltpu.VMEM(c0.shape[1:], F32), pltpu.VMEM(n0.shape[1:], F32),
                        pltpu.VMEM(m0.shape[1:], F32)],
    )
    return pl.pallas_call(
        _mlstm_kernel,
        grid_spec=grid_spec,
        out_shape=[jax.ShapeDtypeStruct((t_all, 256), F32), jax.ShapeDtypeStruct((t_all, 256), F32),
                   jax.ShapeDtypeStruct(c0.shape, F32), jax.ShapeDtypeStruct(n0.shape, F32),
                   jax.ShapeDtypeStruct(m0.shape, F32)],
        compiler_params=_cparams(("arbitrary",)),
        name="mlstm_scan",
    )(tab, mqkv, tail, mqkv, tail, bias_row, c0, n0, m0)


def _ret_scan(tab, rqkv, decay_logit, s0):
    t_all = rqkv.shape[0]
    fwd = lambda i, tab: (tab[0, i], 0)
    bwd = lambda i, tab: (tab[1, i], 0)
    st = lambda i, tab: (tab[2, i], 0, 0, 0, 0)
    grid_spec = pltpu.PrefetchScalarGridSpec(
        num_scalar_prefetch=1,
        grid=(tab.shape[1],),
        in_specs=[pl.BlockSpec((CHUNK, 768), fwd), pl.BlockSpec((CHUNK, 768), bwd),
                  pl.BlockSpec(decay_logit.shape, lambda i, tab: (0, 0)),
                  pl.BlockSpec((1,) + s0.shape[1:], st)],
        out_specs=[pl.BlockSpec((CHUNK, 256), fwd), pl.BlockSpec((CHUNK, 256), bwd),
                   pl.BlockSpec((1,) + s0.shape[1:], st)],
        scratch_shapes=[pltpu.VMEM(s0.shape[1:], F32)],
    )
    return pl.pallas_call(
        _ret_kernel,
        grid_spec=grid_spec,
        out_shape=[jax.ShapeDtypeStruct((t_all, 256), F32), jax.ShapeDtypeStruct((t_all, 256), F32),
                   jax.ShapeDtypeStruct(s0.shape, F32)],
        compiler_params=_cparams(("arbitrary",)),
        name="retention_scan",
    )(tab, rqkv, rqkv, decay_logit, s0)


def _proj_kernel(x_ref, w_ref, o_ref):
    o_ref[...] = _bdot(x_ref[...], w_ref[...]).astype(o_ref.dtype)


def _project(x, w, dtype):
    return pl.pallas_call(
        _proj_kernel,
        out_shape=jax.ShapeDtypeStruct((x.shape[0], w.shape[1]), dtype),
        name="ctx_kv_proj",
    )(x, w)


def _first_argmax_mask(cur, axis, size):
    io = lax.broadcasted_iota(jnp.int32, cur.shape, axis)
    mx = jnp.max(cur, axis=axis, keepdims=True)
    ix = jnp.min(jnp.where(cur == mx, io, size), axis=axis, keepdims=True)
    return io == ix


def _route(scores_t, bias_col):
    tm = scores_t.shape[1]
    per = N_EXPERTS // N_GROUPS
    sel = scores_t + bias_col
    s3 = sel.reshape(N_GROUPS, per, tm)
    hit1 = _first_argmax_mask(s3, 1, per)
    m1 = jnp.max(s3, axis=1, keepdims=True)
    m2 = jnp.max(jnp.where(hit1, -jnp.inf, s3), axis=1, keepdims=True)
    cur = m1 + m2
    gsel = None
    for _ in range(TOPK_GROUPS):
        hit = _first_argmax_mask(cur, 0, N_GROUPS)
        gsel = hit if gsel is None else jnp.logical_or(gsel, hit)
        cur = jnp.where(hit, -jnp.inf, cur)
    cur = jnp.where(gsel, s3, -jnp.inf).reshape(N_EXPERTS, tm)
    chosen = None
    for _ in range(TOP_K):
        hit = _first_argmax_mask(cur, 0, N_EXPERTS)
        chosen = hit if chosen is None else jnp.logical_or(chosen, hit)
        cur = jnp.where(hit, -jnp.inf, cur)
    w = jnp.where(chosen, scores_t, 0.0)
    return w / jnp.sum(w, axis=0, keepdims=True) * ROUTED_SCALE


def _out_kernel(oap_ref, oas_ref, odp_ref, ods_ref, hf_ref, hb_ref, mo_ref, of_ref, ob_ref, rg_ref,
                x_ref, mod_ref, mg_ref, npost_ref, npre_ref, wout_ref, wrt_ref, rb_ref,
                x1_ref, h2_ref, wd_ref, *, npt):
    i = pl.program_id(0)
    is_p = i < npt
    a = jnp.where(is_p, oap_ref[...], oas_ref[...])
    dd = jnp.where(is_p, odp_ref[...], ods_ref[...])
    b = _rms_heads(hf_ref[...] + hb_ref[...]) * mg_ref[...] * _sigmoid(mo_ref[...])
    r = _rms_heads(of_ref[...] + ob_ref[...]) * _silu(rg_ref[...])
    mix = (jnp.dot(a.astype(BF16), wout_ref[0:256, :], preferred_element_type=F32)
           + jnp.dot(b.astype(BF16), wout_ref[256:512, :], preferred_element_type=F32)
           + jnp.dot(r.astype(BF16), wout_ref[512:768, :], preferred_element_type=F32)
           + jnp.dot(dd.astype(BF16), wout_ref[768:1024, :], preferred_element_type=F32))
    m = mod_ref[0]
    x1 = x_ref[...] + m[2:3] * (_rms(mix) * npost_ref[...])
    x1_ref[...] = x1
    h2 = (_rms(x1) * npre_ref[...] * (1.0 + m[4:5]) + m[3:4]).astype(BF16)
    h2_ref[...] = h2
    logits_t = lax.dot_general(wrt_ref[...], h2, (((1,), (1,)), ((), ())), preferred_element_type=F32)
    w_t = _route(_sigmoid(logits_t), rb_ref[...])
    w_t = jnp.concatenate([w_t, jnp.zeros((LANES - N_EXPERTS, w_t.shape[1]), F32)], axis=0)
    wd_ref[...] = w_t.T


def _output_stage(oa, od, hf, hb, mo, of, ob, rg, x, mod, mg, npost, npre, wout, wrt, rb, dims):
    t_all = x.shape[0]
    npt, nst, tps = dims["tp"] // TM, (t_all - dims["tp"]) // TM, dims["ns"] // TM
    row = lambda i: (i, 0)
    const2 = lambda i: (0, 0)
    tok = lambda w: pl.BlockSpec((TM, w), row)
    full = lambda a: pl.BlockSpec(a.shape, const2)
    hp = pl.BlockSpec((TM, 256), lambda i: (jnp.minimum(i, npt - 1), 0))
    hs = pl.BlockSpec((TM, 256), lambda i: (jnp.clip(i - npt, 0, nst - 1), 0))
    return pl.pallas_call(
        functools.partial(_out_kernel, npt=npt),
        grid=(t_all // TM,),
        in_specs=[hp, hs, hp, hs, tok(256), tok(256), tok(256), tok(256), tok(256), tok(256),
                  tok(D_MODEL), pl.BlockSpec((1, 6, D_MODEL), lambda i: (_mod_row(i, npt, tps), 0, 0)),
                  full(mg), full(npost), full(npre), full(wout), full(wrt), full(rb)],
        out_specs=[tok(D_MODEL), tok(D_MODEL), tok(LANES)],
        out_shape=[jax.ShapeDtypeStruct((t_all, D_MODEL), F32),
                   jax.ShapeDtypeStruct((t_all, D_MODEL), BF16),
                   jax.ShapeDtypeStruct((t_all, LANES), F32)],
        compiler_params=_cparams(("parallel",)),
        name="output_stage",
    )(oa[0], oa[1], od[0], od[1], hf, hb, mo, of, ob, rg, x, mod, mg, npost, npre, wout, wrt, rb)


def _swiglu(h, wg, wu, wd):
    act = _silu(_bdot(h, wg)) * _bdot(h, wu)
    return _bdot(act, wd)


def _moe_kernel(h_ref, wd_ref, x1_ref, mod_ref, npost_ref, wg_ref, wu_ref, wdn_ref, sg_ref, su_ref, sd_ref,
                o_ref, acc):
    e = pl.program_id(1)
    h = h_ref[...]

    @pl.when(e == 0)
    def _():
        acc[...] = _swiglu(h, sg_ref[...], su_ref[...], sd_ref[...])

    lane = lax.broadcasted_iota(jnp.int32, wd_ref.shape, 1)
    wcol = jnp.sum(jnp.where(lane == e, wd_ref[...], 0.0), axis=1, keepdims=True)
    acc[...] += _swiglu(h, wg_ref[0], wu_ref[0], wdn_ref[0]) * wcol

    @pl.when(e == pl.num_programs(1) - 1)
    def _():
        m = mod_ref[0]
        o_ref[...] = x1_ref[...] + m[5:6] * (_rms(acc[...]) * npost_ref[...])


def _moe_stage(h2, wd, x1, mod, npost, wg, wu, wdn, sg, su, sd, dims):
    t_all = h2.shape[0]
    tm = TM_MOE if (dims["tp"] % TM_MOE == 0 and dims["ns"] % TM_MOE == 0) else TM
    npt, tps = dims["tp"] // tm, dims["ns"] // tm
    row = lambda i, e: (i, 0)
    const2 = lambda i, e: (0, 0)
    exp3 = lambda i, e: (e, 0, 0)
    full = lambda a: pl.BlockSpec(a.shape, const2)
    return pl.pallas_call(
        _moe_kernel,
        grid=(t_all // tm, N_EXPERTS),
        in_specs=[pl.BlockSpec((tm, D_MODEL), row), pl.BlockSpec((tm, LANES), row),
                  pl.BlockSpec((tm, D_MODEL), row),
                  pl.BlockSpec((1, 6, D_MODEL), lambda i, e: (_mod_row(i, npt, tps), 0, 0)),
                  full(npost),
                  pl.BlockSpec((1, D_MODEL, D_EXPERT), exp3), pl.BlockSpec((1, D_MODEL, D_EXPERT), exp3),
                  pl.BlockSpec((1, D_EXPERT, D_MODEL), exp3),
                  full(sg), full(su), full(sd)],
        out_specs=pl.BlockSpec((tm, D_MODEL), row),
        out_shape=jax.ShapeDtypeStruct((t_all, D_MODEL), F32),
        scratch_shapes=[pltpu.VMEM((tm, D_MODEL), F32)],
        compiler_params=_cparams(("parallel", "arbitrary")),
        name="moe_experts",
    )(h2, wd, x1, mod, npost, wg, wu, wdn, sg, su, sd)


_PERM32 = np.concatenate([np.arange(8, 16), np.arange(0, 8), np.arange(24, 32), np.arange(16, 24)])


def _rope_tables(dims):
    n = dims["ns"]
    pos = jnp.arange(n)
    quarter = QK_ROPE // 4
    inv = 1.0 / (ROPE_THETA ** (jnp.arange(quarter, dtype=F32) / quarter))
    ang_r = (pos // GRID_W).astype(F32)[:, None] * inv[None, :]
    ang_c = (pos % GRID_W).astype(F32)[:, None] * inv[None, :]
    cos32 = jnp.concatenate([jnp.cos(ang_r)] * 2 + [jnp.cos(ang_c)] * 2, axis=-1)
    sin32 = jnp.concatenate([-jnp.sin(ang_r), jnp.sin(ang_r), -jnp.sin(ang_c), jnp.sin(ang_c)], axis=-1)
    cos32 = jnp.concatenate([jnp.ones((dims["tp"], 32), F32)] + [cos32] * dims["bs"], axis=0)
    sin32 = jnp.concatenate([jnp.zeros((dims["tp"], 32), F32)] + [sin32] * dims["bs"], axis=0)
    t_all = cos32.shape[0]
    cos256, sin256 = jnp.tile(cos32, (1, 8)), jnp.tile(sin32, (1, 8))
    one, zero = jnp.ones((t_all, QK_NOPE), F32), jnp.zeros((t_all, QK_NOPE), F32)
    cosq = jnp.concatenate([one, cos32] * N_HEADS, axis=-1)
    sinq = jnp.concatenate([zero, sin32] * N_HEADS, axis=-1)
    return cos256, sin256, cosq, sinq


def _split_w_in(w_in_l):
    sizes = (256, 256, 256, 256, 256, 256, 256, 16, 256, 256, 256, 256, Q_LORA, KV_LORA, QK_ROPE)
    idx = np.cumsum(sizes)[:-1]
    return jnp.split(w_in_l, [int(v) for v in idx], axis=-1)


def _layer_weights(w_in_l, w_uq_l):
    (a_q, a_k, a_v, m_q, m_k, m_v, m_o, m_g, r_q, r_k, r_v, r_g, d_cq, d_ckv, d_kr) = _split_w_in(w_in_l)
    perm256 = np.concatenate([_PERM32 + 32 * j for j in range(8)])
    pad = lambda w, n: jnp.pad(w, ((0, 0), (0, n - w.shape[1])))
    tail = pad(jnp.concatenate([d_kr, m_g], axis=-1), LANES)
    tail_s = pad(d_kr[:, _PERM32], LANES)
    w_all = jnp.concatenate([a_q, a_k, a_v, m_q, m_k, m_v, m_o, r_q, r_k, r_v, r_g, d_cq, d_ckv, tail,
                             a_q[:, perm256], a_k[:, perm256], tail_s], axis=-1).astype(BF16)
    per = QK_NOPE + QK_ROPE
    permq = np.concatenate([np.concatenate([np.arange(QK_NOPE), QK_NOPE + _PERM32]) + per * j
                            for j in range(N_HEADS)])
    return w_all, w_uq_l.astype(BF16), w_uq_l[:, permq].astype(BF16)


def kernel(x_prompt, x_sample, cache_diff_k, cache_diff_v, state_mlstm_C, state_mlstm_n, state_mlstm_m, state_ret_S, cache_mla_ckv, cache_mla_krope, c, c_ctx, w_mod, b_mod, norm_pre, norm_post, w_in, w_out, diff_lambda, diff_norm, mlstm_gate_bias, mlstm_norm, ret_decay_logit, mla_q_norm, mla_w_uq, mla_kv_norm, mla_w_ukv, moe_w_router, moe_router_bias, moe_w_gate, moe_w_up, moe_w_down, shared_w_gate, shared_w_up, shared_w_down):
    bp, n_p, _ = x_prompt.shape
    bs, n_s, _ = x_sample.shape
    depth = w_in.shape[0]
    past = cache_diff_k.shape[2]
    dims = dict(bp=bp, np=n_p, bs=bs, ns=n_s, tp=bp * n_p, past=past)
    tp, ts = bp * n_p, bs * n_s
    assert n_p % TM == 0 and n_s % TM == 0 and n_p % CHUNK == 0 and n_s % min(TK, n_s) == 0
    assert past % min(TK, past) == 0 and tp % n_s == 0 and bs + 1 <= 8 and n_s % GRID_W == 0

    x = jnp.concatenate([x_prompt.reshape(tp, D_MODEL), x_sample.reshape(ts, D_MODEL)], axis=0)
    cond = jnp.zeros((8, D_MODEL), F32).at[0].set(c_ctx).at[1:1 + bs].set(c)
    mod_all = _modulation(cond, w_mod, b_mod).reshape(depth, 8, 6, D_MODEL)
    tabs = _rope_tables(dims)
    scan_tab = _scan_table(dims)

    outs = [[] for _ in range(8)]
    for l in range(depth):
        lam_init = 0.8 - 0.6 * math.exp(-0.3 * l)
        mod = mod_all[l]
        w_all, wuq, wuqs = _layer_weights(w_in[l], mla_w_uq[l])
        wukv = mla_w_ukv[l].astype(BF16)
        (aq1, aq2, ak1t, ak2t, avh, ak, av, mqkv, mo, rqkv, rg, tail, qmla, ckv, kmlat, vmla) = _input_stage(
            x, mod, norm_pre[l, 0:1], w_all, tabs, mla_q_norm[l][None], wuq, wuqs,
            mla_kv_norm[l][None], wukv, dims)

        lamv, dg = diff_lambda[l], diff_norm[l][None]
        ckt = jnp.transpose(cache_diff_k[:, l], (2, 3, 0, 1)).reshape(N_HEADS, HEAD_DIM, bs * past).astype(BF16)
        cv = jnp.transpose(cache_diff_v[:, l], (2, 0, 1, 3)).reshape(N_HEADS, bs * past, HEAD_DIM).astype(BF16)
        oa_p = _attention("diff_attn_ctx", [aq1, aq2], [ak1t, ak2t], avh, None, None, [lamv, dg],
                          0, bp, n_p, 0, lam_init)
        oa_s = _attention("diff_attn_latent", [aq1, aq2], [ak1t, ak2t], avh,
                          [ckt[:, :DIFF_HALF], ckt[:, DIFF_HALF:]], cv, [lamv, dg], tp, bs, n_s, past, lam_init)

        ckv_ctx = cache_mla_ckv[:, l].reshape(bs * past, KV_LORA)
        kvc = _project(ckv_ctx, wukv, BF16).reshape(bs * past, N_HEADS, QK_NOPE + V_HEAD)
        krc = jnp.broadcast_to(cache_mla_krope[:, l].reshape(bs * past, 1, QK_ROPE).astype(BF16),
                               (bs * past, N_HEADS, QK_ROPE))
        kct = jnp.transpose(jnp.concatenate([kvc[..., :QK_NOPE], krc], axis=-1), (1, 2, 0))
        vc = jnp.transpose(kvc[..., QK_NOPE:], (1, 0, 2))
        od_p = _attention("mla_attn_ctx", [qmla], [kmlat], vmla, None, None, [], 0, bp, n_p, 0)
        od_s = _attention("mla_attn_latent", [qmla], [kmlat], vmla, [kct], vc, [], tp, bs, n_s, past)

        zeros = lambda a: jnp.zeros((bp,) + a.shape[1:], F32)
        st_c, st_n, st_m, st_s = state_mlstm_C[:, l], state_mlstm_n[:, l], state_mlstm_m[:, l], state_ret_S[:, l]
        bias_row = jnp.zeros((1, LANES), F32).at[0, TAIL_GATE0:TAIL_GATE0 + 16].set(mlstm_gate_bias[l].reshape(16))
        hf, hb, c_n, n_n, m_n = _mlstm_scan(scan_tab, mqkv, tail, bias_row,
                                            jnp.concatenate([zeros(st_c), st_c]),
                                            jnp.concatenate([zeros(st_n), st_n]),
                                            jnp.concatenate([zeros(st_m), st_m]))
        of, ob, s_n = _ret_scan(scan_tab, rqkv, ret_decay_logit[l], jnp.concatenate([zeros(st_s), st_s]))

        x1, h2, wd = _output_stage((oa_p, oa_s), (od_p, od_s), hf, hb, mo, of, ob, rg, x, mod,
                                   mlstm_norm[l][None], norm_post[l, 0:1], norm_pre[l, 1:2],
                                   w_out[l].astype(BF16), moe_w_router[l].T.astype(BF16),
                                   moe_router_bias[l][:, None], dims)
        x = _moe_stage(h2, wd, x1, mod, norm_post[l, 1:2], moe_w_gate[l], moe_w_up[l], moe_w_down[l],
                       shared_w_gate[l], shared_w_up[l], shared_w_down[l], dims)

        outs[0].append(ak[:tp].reshape(bp, n_p, N_HEADS, HEAD_DIM))
        outs[1].append(av[:tp].reshape(bp, n_p, N_HEADS, HEAD_DIM))
        outs[2].append(c_n[:bp])
        outs[3].append(n_n[:bp])
        outs[4].append(m_n[:bp])
        outs[5].append(s_n[:bp])
        outs[6].append(ckv[:tp].reshape(bp, n_p, KV_LORA))
        outs[7].append(tail[:tp, :QK_ROPE].reshape(bp, n_p, QK_ROPE))

    return (x[:tp].reshape(bp, n_p, D_MODEL), x[tp:].reshape(bs, n_s, D_MODEL)) + tuple(
        jnp.stack(o, axis=1) for o in outs)
```

```python
import functools
import math

import numpy as np
import jax
import jax.numpy as jnp
from jax import lax
from jax.experimental import pallas as pl
from jax.experimental.pallas import tpu as pltpu

F32 = jnp.float32
BF16 = jnp.bfloat16
HIGHEST = lax.Precision.HIGHEST

D_MODEL = 1024
GRID_W = 64
GROUP_WIDTH = 256
HEAD_DIM = 64
N_HEADS = 4
DIFF_HALF = 32
ROPE_THETA = 10000.0
Q_LORA = 256
KV_LORA = 128
QK_NOPE = 64
QK_ROPE = 32
V_HEAD = 64
N_EXPERTS = 64
TOP_K = 8
N_GROUPS = 8
TOPK_GROUPS = 4
D_EXPERT = 256
ROUTED_SCALE = 2.5
CHUNK = 128
EPS = 1e-6
Q_MLA = N_HEADS * (QK_NOPE + QK_ROPE)
KV_MLA = N_HEADS * (QK_NOPE + V_HEAD)

LANES = 128
VMEM_LIMIT = 56 * 1024 * 1024

TM = 256
TM_MOE = 1024
TQ = 256
TK = 2048

C_AQ, C_AK, C_AV = 0, 256, 512
C_MQKV, C_MO = 768, 1536
C_RQKV, C_RG = 1792, 2560
C_CQ, C_CKV, C_TAIL = 2816, 3072, 3200
C_AQS, C_AKS, C_TAILS = 3328, 3584, 3840
W_ALL = 3968
TAIL_GATE0 = QK_ROPE


def _cparams(sem, flags=None):
    return pltpu.CompilerParams(dimension_semantics=sem, vmem_limit_bytes=VMEM_LIMIT, flags=flags)


def _rms(x):
    return x * lax.rsqrt(jnp.mean(x * x, axis=-1, keepdims=True) + EPS)


def _head_mean_matrix(width):
    r = lax.broadcasted_iota(jnp.int32, (width, width), 0) // HEAD_DIM
    c = lax.broadcasted_iota(jnp.int32, (width, width), 1) // HEAD_DIM
    return jnp.where(r == c, 1.0 / HEAD_DIM, 0.0).astype(F32)


def _rms_heads(x):
    ms = jnp.dot(x * x, _head_mean_matrix(x.shape[-1]), precision=HIGHEST, preferred_element_type=F32)
    return x * lax.rsqrt(ms + EPS)


def _sigmoid(x):
    return 1.0 / (1.0 + jnp.exp(-x))


def _silu(x):
    return x * _sigmoid(x)


def _log_sigmoid(x):
    return jnp.minimum(x, 0.0) - jnp.log1p(jnp.exp(-jnp.abs(x)))


def _bdot(a, b):
    return jnp.dot(a.astype(BF16), b.astype(BF16), preferred_element_type=F32)


def _mod_kernel(c_ref, w_ref, b_ref, o_ref):
    o_ref[0] = _bdot(_silu(c_ref[...]), w_ref[0]) + b_ref[0]


def _modulation(cond, w_mod, b_mod):
    depth, _, n = w_mod.shape
    tn = 1536
    return pl.pallas_call(
        _mod_kernel,
        grid=(depth, n // tn),
        in_specs=[pl.BlockSpec((8, D_MODEL), lambda l, j: (0, 0)),
                  pl.BlockSpec((1, D_MODEL, tn), lambda l, j: (l, 0, j)),
                  pl.BlockSpec((1, 1, tn), lambda l, j: (l, 0, j))],
        out_specs=pl.BlockSpec((1, 8, tn), lambda l, j: (l, 0, j)),
        out_shape=jax.ShapeDtypeStruct((depth, 8, n), F32),
        compiler_params=_cparams(("parallel", "parallel")),
        name="adaln_mod",
    )(cond, w_mod, b_mod.reshape(depth, 1, n))


def _in_kernel(xp_ref, xs_ref, mod_ref, npre_ref, w_ref, cos_ref, sin_ref, cosq_ref, sinq_ref,
               qg_ref, wuq_ref, wuqs_ref, kvg_ref, wukv_ref,
               aq1_ref, aq2_ref, ak1t_ref, ak2t_ref, avh_ref, ak_ref, av_ref,
               mqkv_ref, mo_ref, rqkv_ref, rg_ref, tail_ref,
               qmla_ref, ckv_ref, kmlat_ref, vmla_ref, *, npt):
    is_ctx = pl.program_id(0) < npt
    x = jnp.where(is_ctx, xp_ref[...], xs_ref[...])
    m = mod_ref[0]
    h = (_rms(x) * npre_ref[...] * (1.0 + m[1:2]) + m[0:1]).astype(BF16)

    def proj(c0, width):
        return jnp.dot(h, w_ref[:, c0:c0 + width], preferred_element_type=F32)

    cos = jnp.where(is_ctx, 1.0, cos_ref[...])
    sin = jnp.where(is_ctx, 0.0, sin_ref[...])
    aq = (proj(C_AQ, 256) * cos + proj(C_AQS, 256) * sin) * (DIFF_HALF ** -0.5)
    ak = proj(C_AK, 256) * cos + proj(C_AKS, 256) * sin
    av = proj(C_AV, 256)
    ak_ref[...] = ak
    av_ref[...] = av
    ak_t = ak.T.astype(BF16)
    for hd in range(N_HEADS):
        lo = hd * HEAD_DIM
        aq1_ref[hd] = aq[:, lo:lo + DIFF_HALF].astype(BF16)
        aq2_ref[hd] = aq[:, lo + DIFF_HALF:lo + HEAD_DIM].astype(BF16)
        ak1t_ref[hd] = ak_t[lo:lo + DIFF_HALF, :]
        ak2t_ref[hd] = ak_t[lo + DIFF_HALF:lo + HEAD_DIM, :]
        avh_ref[hd] = av[:, lo:lo + HEAD_DIM].astype(BF16)

    mqkv_ref[...] = proj(C_MQKV, 768).astype(BF16)
    mo_ref[...] = proj(C_MO, 256)
    rqkv_ref[...] = proj(C_RQKV, 768).astype(BF16)
    rg_ref[...] = proj(C_RG, 256)

    tail = proj(C_TAIL, LANES)
    tail_ref[...] = tail
    kr_t = (tail * cos[:, :LANES] + proj(C_TAILS, LANES) * sin[:, :LANES]).T[:QK_ROPE, :].astype(BF16)

    cqn = (_rms(proj(C_CQ, Q_LORA)) * qg_ref[...]).astype(BF16)
    q = jnp.dot(cqn, wuq_ref[...], preferred_element_type=F32)
    qs = jnp.dot(cqn, wuqs_ref[...], preferred_element_type=F32)
    cosq = jnp.where(is_ctx, 1.0, cosq_ref[...])
    sinq = jnp.where(is_ctx, 0.0, sinq_ref[...])
    qmla = (q * cosq + qs * sinq) * ((QK_NOPE + QK_ROPE) ** -0.5)
    for hd in range(N_HEADS):
        lo = hd * (QK_NOPE + QK_ROPE)
        qmla_ref[hd] = qmla[:, lo:lo + QK_NOPE + QK_ROPE].astype(BF16)
    ckvn = _rms(proj(C_CKV, KV_LORA)) * kvg_ref[...]
    ckv_ref[...] = ckvn
    kv = jnp.dot(ckvn.astype(BF16), wukv_ref[...], preferred_element_type=F32)
    kv_t = kv.T.astype(BF16)
    per = QK_NOPE + V_HEAD
    for hd in range(N_HEADS):
        kmlat_ref[hd, 0:QK_NOPE, :] = kv_t[hd * per:hd * per + QK_NOPE, :]
        kmlat_ref[hd, QK_NOPE:QK_NOPE + QK_ROPE, :] = kr_t
        vmla_ref[hd] = kv[:, hd * per + QK_NOPE:(hd + 1) * per].astype(BF16)


def _mod_row(i, npt, tps):
    return jnp.where(i < npt, 0, 1 + (i - npt) // tps)


def _pair_specs(npt, nst, tm, width):
    return [pl.BlockSpec((tm, width), lambda i, *_: (jnp.minimum(i, npt - 1), 0)),
            pl.BlockSpec((tm, width), lambda i, *_: (jnp.clip(i - npt, 0, nst - 1), 0))]


def _input_stage(xp, xs, mod, npre, w_all, tabs, qg, wuq, wuqs, kvg, wukv, dims):
    t_all = xp.shape[0] + xs.shape[0]
    npt, nst, tps = dims["tp"] // TM, xs.shape[0] // TM, dims["ns"] // TM
    row = lambda i: (i, 0)
    tab = lambda w: pl.BlockSpec((TM, w), lambda i: (jnp.maximum(i - npt, 0) % tps, 0))
    hrow = lambda i: (0, i, 0)
    const2 = lambda i: (0, 0)
    tok = lambda w: pl.BlockSpec((TM, w), row)
    headed = lambda w: pl.BlockSpec((N_HEADS, TM, w), hrow)
    headed_t = lambda d: pl.BlockSpec((N_HEADS, d, TM), lambda i: (0, 0, i))
    full = lambda a: pl.BlockSpec(a.shape, const2)
    cos, sin, cosq, sinq = tabs
    out_shapes = [
        (headed(DIFF_HALF), (N_HEADS, t_all, DIFF_HALF), BF16),
        (headed(DIFF_HALF), (N_HEADS, t_all, DIFF_HALF), BF16),
        (headed_t(DIFF_HALF), (N_HEADS, DIFF_HALF, t_all), BF16),
        (headed_t(DIFF_HALF), (N_HEADS, DIFF_HALF, t_all), BF16),
        (headed(HEAD_DIM), (N_HEADS, t_all, HEAD_DIM), BF16),
        (tok(256), (t_all, 256), F32),
        (tok(256), (t_all, 256), F32),
        (tok(768), (t_all, 768), BF16),
        (tok(256), (t_all, 256), F32),
        (tok(768), (t_all, 768), BF16),
        (tok(256), (t_all, 256), F32),
        (tok(LANES), (t_all, LANES), F32),
        (headed(QK_NOPE + QK_ROPE), (N_HEADS, t_all, QK_NOPE + QK_ROPE), BF16),
        (tok(KV_LORA), (t_all, KV_LORA), F32),
        (headed_t(QK_NOPE + QK_ROPE), (N_HEADS, QK_NOPE + QK_ROPE, t_all), BF16),
        (headed(V_HEAD), (N_HEADS, t_all, V_HEAD), BF16),
    ]
    return pl.pallas_call(
        functools.partial(_in_kernel, npt=npt),
        grid=(t_all // TM,),
        in_specs=_pair_specs(npt, nst, TM, D_MODEL) + [
            pl.BlockSpec((1, 6, D_MODEL), lambda i: (_mod_row(i, npt, tps), 0, 0)),
            full(npre), full(w_all), tab(256), tab(256), tab(Q_MLA), tab(Q_MLA),
            full(qg), full(wuq), full(wuqs), full(kvg), full(wukv)],
        out_specs=[s for s, _, _ in out_shapes],
        out_shape=[jax.ShapeDtypeStruct(shp, dt) for _, shp, dt in out_shapes],
        compiler_params=_cparams(("parallel",)),
        name="input_stage",
    )(xp, xs, mod, npre, w_all, cos, sin, cosq, sinq, qg, wuq, wuqs, kvg, wukv)


def _attn_kernel(*refs, n_soft, n_new, n_ctx, lam_init):
    refs = list(refs)
    q_refs, kt_refs, v_ref = refs[:n_soft], refs[n_soft:2 * n_soft], refs[2 * n_soft]
    pos = 2 * n_soft + 1
    if n_ctx:
        ckt_refs, cv_ref = refs[pos:pos + n_soft], refs[pos + n_soft]
        pos += n_soft + 1
    if n_soft == 2:
        lam_ref, g_ref = refs[pos:pos + 2]
        pos += 2
    o_ref = refs[pos]
    tq = o_ref.shape[0]
    nchain = n_soft * N_HEADS

    def chunk(state, kt_of, v_of):
        ms, ls, accs = state
        new_m, new_l, new_acc = list(ms), list(ls), list(accs)
        for hd in range(N_HEADS):
            v = v_of(hd)
            for j in range(n_soft):
                c = j * N_HEADS + hd
                s = jnp.dot(q_refs[j][hd], kt_of(j, hd), preferred_element_type=F32)
                m_new = jnp.maximum(ms[c], jnp.broadcast_to(jnp.max(s, axis=-1, keepdims=True), (tq, LANES)))
                p = jnp.exp(s - jnp.tile(m_new, (1, s.shape[1] // LANES)))
                alpha = jnp.exp(ms[c] - m_new)
                new_l[c] = alpha * ls[c] + jnp.broadcast_to(jnp.sum(p, axis=-1, keepdims=True), (tq, LANES))
                new_acc[c] = alpha[:, :V_HEAD] * accs[c] + jnp.dot(p.astype(BF16), v, preferred_element_type=F32)
                new_m[c] = m_new
        return tuple(new_m), tuple(new_l), tuple(new_acc)

    state = (tuple(jnp.full((tq, LANES), -jnp.inf, F32) for _ in range(nchain)),
             tuple(jnp.zeros((tq, LANES), F32) for _ in range(nchain)),
             tuple(jnp.zeros((tq, V_HEAD), F32) for _ in range(nchain)))
    if n_ctx:
        cstep = min(TK, n_ctx)
        for i in range(n_ctx // cstep):
            state = chunk(state, lambda j, hd, i=i: ckt_refs[j][hd, :, i * cstep:(i + 1) * cstep],
                          lambda hd, i=i: cv_ref[hd, i * cstep:(i + 1) * cstep, :])
    step = min(TK, n_new)

    def body(i, st):
        start = pl.multiple_of(i * step, step)
        return chunk(st, lambda j, hd: kt_refs[j][hd, :, pl.ds(start, step)],
                     lambda hd: v_ref[hd, pl.ds(start, step), :])

    _, ls, accs = lax.fori_loop(0, n_new // step, body, state)

    if n_soft == 2:
        lv = lam_ref[...]
        lam = (jnp.exp(jnp.sum(lv[0:1] * lv[1:2], axis=-1, keepdims=True))
               - jnp.exp(jnp.sum(lv[2:3] * lv[3:4], axis=-1, keepdims=True)) + lam_init)
    for hd in range(N_HEADS):
        out = accs[hd] / ls[hd][:, :V_HEAD]
        if n_soft == 2:
            a = out - lam * (accs[N_HEADS + hd] / ls[N_HEADS + hd][:, :V_HEAD])
            out = _rms(a) * g_ref[...] * (1.0 - lam_init)
        o_ref[:, hd * V_HEAD:(hd + 1) * V_HEAD] = out


def _attention(name, qs, kts, v, ctx_kts, ctx_v, extras, row0, nb, n, n_ctx, lam_init=0.0):
    tq = min(TQ, n)
    nqt = n // tq
    n_soft = len(qs)
    qmap = lambda b, i: (0, row0 // tq + b * nqt + i, 0)
    in_specs = [pl.BlockSpec((N_HEADS, tq, a.shape[-1]), qmap) for a in qs]
    in_specs += [pl.BlockSpec((N_HEADS, a.shape[1], n), lambda b, i: (0, 0, row0 // n + b)) for a in kts]
    in_specs += [pl.BlockSpec((N_HEADS, n, V_HEAD), lambda b, i: (0, row0 // n + b, 0))]
    args = list(qs) + list(kts) + [v]
    if n_ctx:
        in_specs += [pl.BlockSpec((N_HEADS, a.shape[1], n_ctx), lambda b, i: (0, 0, b)) for a in ctx_kts]
        in_specs += [pl.BlockSpec((N_HEADS, n_ctx, V_HEAD), lambda b, i: (0, b, 0))]
        args += list(ctx_kts) + [ctx_v]
    in_specs += [pl.BlockSpec(a.shape, lambda b, i: (0, 0)) for a in extras]
    args += list(extras)
    return pl.pallas_call(
        functools.partial(_attn_kernel, n_soft=n_soft, n_new=n, n_ctx=n_ctx, lam_init=lam_init),
        grid=(nb, nqt),
        in_specs=in_specs,
        out_specs=pl.BlockSpec((tq, N_HEADS * V_HEAD), lambda b, i: (b * nqt + i, 0)),
        out_shape=jax.ShapeDtypeStruct((nb * n, N_HEADS * V_HEAD), F32),
        compiler_params=_cparams(("parallel", "parallel")),
        name=name,
    )(*args)


def _tri(lower):
    r = lax.broadcasted_iota(jnp.int32, (CHUNK, CHUNK), 0)
    c = lax.broadcasted_iota(jnp.int32, (CHUNK, CHUNK), 1)
    return (c <= r) if lower else (c >= r)


def _nt(a, b):
    return lax.dot_general(a, b, (((1,), (1,)), ((), ())), preferred_element_type=F32)


def _tn(a, b):
    return lax.dot_general(a, b, (((0,), (0,)), ((), ())), preferred_element_type=F32)


def _mlstm_dir(d, qkv_ref, tail_ref, bias_ref, c_s, n_s, m_s, h_ref):
    g = tail_ref[...] + bias_ref[...]
    ls = _log_sigmoid(g)
    g_t, ls_t = g.T, ls.T
    lower = _tri(True).astype(F32)
    upper = _tri(False).astype(F32)
    left, right = (lower, upper) if d == 0 else (upper, lower)
    cum_col = jnp.dot(left, ls, precision=HIGHEST, preferred_element_type=F32)
    cum_row = jnp.dot(ls_t, right, precision=HIGHEST, preferred_element_type=F32)
    valid = _tri(d == 0)
    last = CHUNK - 1 if d == 0 else 0
    qkv = qkv_ref[...]
    for hd in range(N_HEADS):
        ci = TAIL_GATE0 + 4 * (2 * d) + hd
        cf = ci + 4
        li_row, li_col = g_t[ci:ci + 1, :], g[:, ci:ci + 1]
        b_col, b_row = cum_col[:, cf:cf + 1], cum_row[cf:cf + 1, :]
        m_prev = m_s[d:d + 1, hd:hd + 1]
        log_d = jnp.where(valid, b_col - b_row + li_row, -jnp.inf)
        log_inter = b_col + m_prev
        m_t = jnp.maximum(log_inter, jnp.max(log_d, axis=1, keepdims=True))
        lo = hd * HEAD_DIM
        qh = qkv[:, lo:lo + HEAD_DIM]
        kh = qkv[:, 256 + lo:256 + lo + HEAD_DIM]
        vh = qkv[:, 512 + lo:512 + lo + HEAD_DIM]
        k_scale = HEAD_DIM ** -0.5
        w = _nt(qh, kh) * k_scale * jnp.exp(log_d - m_t)
        w_inter = jnp.exp(log_inter - m_t)
        c_mat = c_s[d, hd]
        n_vec = n_s[d, hd:hd + 1, :]
        num = (jnp.dot(w.astype(BF16), vh, preferred_element_type=F32)
               + w_inter * jnp.dot(qh, c_mat.astype(BF16), preferred_element_type=F32))
        den = jnp.sum(w, axis=1, keepdims=True) + w_inter * jnp.sum(qh.astype(F32) * n_vec, axis=1, keepdims=True)
        h_ref[:, lo:lo + HEAD_DIM] = num / jnp.maximum(jnp.abs(den), jnp.exp(-m_t))
        m_new = m_t[last:last + 1, :]
        b_last = b_col[last:last + 1, :]
        w_end = jnp.exp(b_last - b_col + li_col - m_new)
        decay = jnp.exp(b_last + m_prev - m_new)
        kw = kh.astype(F32) * k_scale * w_end
        c_s[d, hd] = decay * c_mat + _tn(kw.astype(BF16), vh)
        n_s[d, hd:hd + 1, :] = decay * n_vec + jnp.sum(kw, axis=0, keepdims=True)
        m_s[d:d + 1, hd:hd + 1] = m_new


def _mlstm_kernel(tab_ref, qkvf_ref, tailf_ref, qkvb_ref, tailb_ref, bias_ref, c0_ref, n0_ref, m0_ref,
                  hf_ref, hb_ref, cn_ref, nn_ref, mn_ref, c_s, n_s, m_s):
    i = pl.program_id(0)

    @pl.when(tab_ref[3, i] == 1)
    def _():
        c_s[...] = c0_ref[0]
        n_s[...] = n0_ref[0]
        m_s[...] = m0_ref[0]

    _mlstm_dir(0, qkvf_ref, tailf_ref, bias_ref, c_s, n_s, m_s, hf_ref)
    _mlstm_dir(1, qkvb_ref, tailb_ref, bias_ref, c_s, n_s, m_s, hb_ref)

    @pl.when(tab_ref[4, i] == 1)
    def _():
        cn_ref[0] = c_s[...]
        nn_ref[0] = n_s[...]
        mn_ref[0] = m_s[...]


def _ret_dir(d, qkv_ref, lg_ref, s_s, o_ref):
    t = lax.broadcasted_iota(jnp.int32, (CHUNK, CHUNK), 0)
    s = lax.broadcasted_iota(jnp.int32, (CHUNK, CHUNK), 1)
    lag = (t - s) if d == 0 else (s - t)
    lag_f = jnp.maximum(lag, 0).astype(F32)
    pos_c = lax.broadcasted_iota(jnp.int32, (CHUNK, 1), 0)
    pos_c = (pos_c if d == 0 else CHUNK - 1 - pos_c).astype(F32)
    qkv = qkv_ref[...]
    for hd in range(N_HEADS):
        lg = _log_sigmoid(lg_ref[d:d + 1, hd:hd + 1])
        intra = jnp.where(lag >= 0, jnp.exp(lag_f * lg), 0.0)
        inter = jnp.exp((pos_c + 1.0) * lg)
        tail = jnp.exp((CHUNK - 1.0 - pos_c) * lg)
        chunk_decay = jnp.exp(CHUNK * lg)
        lo = hd * HEAD_DIM
        qh = qkv[:, lo:lo + HEAD_DIM]
        kh = qkv[:, 256 + lo:256 + lo + HEAD_DIM]
        vh = qkv[:, 512 + lo:512 + lo + HEAD_DIM]
        k_scale = HEAD_DIM ** -0.5
        a = _nt(qh, kh) * k_scale * intra
        st = s_s[d, hd]
        o_ref[:, lo:lo + HEAD_DIM] = (jnp.dot(a.astype(BF16), vh, preferred_element_type=F32)
                                      + inter * jnp.dot(qh, st.astype(BF16), preferred_element_type=F32))
        kw = kh.astype(F32) * k_scale * tail
        s_s[d, hd] = chunk_decay * st + _tn(kw.astype(BF16), vh)


def _ret_kernel(tab_ref, qkvf_ref, qkvb_ref, lg_ref, s0_ref, of_ref, ob_ref, sn_ref, s_s):
    i = pl.program_id(0)

    @pl.when(tab_ref[3, i] == 1)
    def _():
        s_s[...] = s0_ref[0]

    _ret_dir(0, qkvf_ref, lg_ref, s_s, of_ref)
    _ret_dir(1, qkvb_ref, lg_ref, s_s, ob_ref)

    @pl.when(tab_ref[4, i] == 1)
    def _():
        sn_ref[0] = s_s[...]


def _scan_table(dims):
    rows = []
    seq = 0
    for base, nb, n in ((0, dims["bp"], dims["np"]), (dims["tp"], dims["bs"], dims["ns"])):
        nch = n // CHUNK
        for b in range(nb):
            blk0 = (base + b * n) // CHUNK
            for c in range(nch):
                rows.append((blk0 + c, blk0 + nch - 1 - c, seq, int(c == 0), int(c == nch - 1)))
            seq += 1
    return jnp.asarray(np.array(rows, dtype=np.int32).T)


def _mlstm_scan(tab, mqkv, tail, bias_row, c0, n0, m0):
    t_all = mqkv.shape[0]
    nseq = c0.shape[0]
    fwd = lambda i, tab: (tab[0, i], 0)
    bwd = lambda i, tab: (tab[1, i], 0)
    st = lambda nd: (lambda i, tab: (tab[2, i],) + (0,) * nd)
    grid_spec = pltpu.PrefetchScalarGridSpec(
        num_scalar_prefetch=1,
        grid=(tab.shape[1],),
        in_specs=[pl.BlockSpec((CHUNK, 768), fwd), pl.BlockSpec((CHUNK, LANES), fwd),
                  pl.BlockSpec((CHUNK, 768), bwd), pl.BlockSpec((CHUNK, LANES), bwd),
                  pl.BlockSpec((1, LANES), lambda i, tab: (0, 0)),
                  pl.BlockSpec((1,) + c0.shape[1:], st(4)),
                  pl.BlockSpec((1,) + n0.shape[1:], st(3)),
                  pl.BlockSpec((1,) + m0.shape[1:], st(2))],
        out_specs=[pl.BlockSpec((CHUNK, 256), fwd), pl.BlockSpec((CHUNK, 256), bwd),
                   pl.BlockSpec((1,) + c0.shape[1:], st(4)),
                   pl.BlockSpec((1,) + n0.shape[1:], st(3)),
                   pl.BlockSpec((1,) + m0.shape[1:], st(2))],
        scratch_shapes=[pltpu.VMEM(c0.shape[1:], F32), pltpu.VMEM(n0.shape[1:], F32),
                        pltpu.VMEM(m0.shape[1:], F32)],
    )
    return pl.pallas_call(
        _mlstm_kernel,
        grid_spec=grid_spec,
        out_shape=[jax.ShapeDtypeStruct((t_all, 256), F32), jax.ShapeDtypeStruct((t_all, 256), F32),
                   jax.ShapeDtypeStruct(c0.shape, F32), jax.ShapeDtypeStruct(n0.shape, F32),
                   jax.ShapeDtypeStruct(m0.shape, F32)],
        compiler_params=_cparams(("arbitrary",)),
        name="mlstm_scan",
    )(tab, mqkv, tail, mqkv, tail, bias_row, c0, n0, m0)


def _ret_scan(tab, rqkv, decay_logit, s0):
    t_all = rqkv.shape[0]
    fwd = lambda i, tab: (tab[0, i], 0)
    bwd = lambda i, tab: (tab[1, i], 0)
    st = lambda i, tab: (tab[2, i], 0, 0, 0, 0)
    grid_spec = pltpu.PrefetchScalarGridSpec(
        num_scalar_prefetch=1,
        grid=(tab.shape[1],),
        in_specs=[pl.BlockSpec((CHUNK, 768), fwd), pl.BlockSpec((CHUNK, 768), bwd),
                  pl.BlockSpec(decay_logit.shape, lambda i, tab: (0, 0)),
                  pl.BlockSpec((1,) + s0.shape[1:], st)],
        out_specs=[pl.BlockSpec((CHUNK, 256), fwd), pl.BlockSpec((CHUNK, 256), bwd),
                   pl.BlockSpec((1,) + s0.shape[1:], st)],
        scratch_shapes=[pltpu.VMEM(s0.shape[1:], F32)],
    )
    return pl.pallas_call(
        _ret_kernel,
        grid_spec=grid_spec,
        out_shape=[jax.ShapeDtypeStruct((t_all, 256), F32), jax.ShapeDtypeStruct((t_all, 256), F32),
                   jax.ShapeDtypeStruct(s0.shape, F32)],
        compiler_params=_cparams(("arbitrary",)),
        name="retention_scan",
    )(tab, rqkv, rqkv, decay_logit, s0)


def _proj_kernel(x_ref, w_ref, o_ref):
    o_ref[...] = _bdot(x_ref[...], w_ref[...]).astype(o_ref.dtype)


def _project(x, w, dtype):
    return pl.pallas_call(
        _proj_kernel,
        out_shape=jax.ShapeDtypeStruct((x.shape[0], w.shape[1]), dtype),
        name="ctx_kv_proj",
    )(x, w)


def _first_argmax_mask(cur, axis, size):
    io = lax.broadcasted_iota(jnp.int32, cur.shape, axis)
    mx = jnp.max(cur, axis=axis, keepdims=True)
    ix = jnp.min(jnp.where(cur == mx, io, size), axis=axis, keepdims=True)
    return io == ix


def _route(scores_t, bias_col):
    tm = scores_t.shape[1]
    per = N_EXPERTS // N_GROUPS
    sel = scores_t + bias_col
    s3 = sel.reshape(N_GROUPS, per, tm)
    hit1 = _first_argmax_mask(s3, 1, per)
    m1 = jnp.max(s3, axis=1, keepdims=True)
    m2 = jnp.max(jnp.where(hit1, -jnp.inf, s3), axis=1, keepdims=True)
    cur = m1 + m2
    gsel = None
    for _ in range(TOPK_GROUPS):
        hit = _first_argmax_mask(cur, 0, N_GROUPS)
        gsel = hit if gsel is None else jnp.logical_or(gsel, hit)
        cur = jnp.where(hit, -jnp.inf, cur)
    cur = jnp.where(gsel, s3, -jnp.inf).reshape(N_EXPERTS, tm)
    chosen = None
    for _ in range(TOP_K):
        hit = _first_argmax_mask(cur, 0, N_EXPERTS)
        chosen = hit if chosen is None else jnp.logical_or(chosen, hit)
        cur = jnp.where(hit, -jnp.inf, cur)
    w = jnp.where(chosen, scores_t, 0.0)
    return w / jnp.sum(w, axis=0, keepdims=True) * ROUTED_SCALE


def _out_kernel(oap_ref, oas_ref, odp_ref, ods_ref, hf_ref, hb_ref, mo_ref, of_ref, ob_ref, rg_ref,
                xp_ref, xs_ref, mod_ref, mg_ref, npost_ref, npre_ref, wout_ref, wrt_ref, rb_ref,
                x1_ref, h2_ref, wd_ref, *, npt):
    i = pl.program_id(0)
    is_p = i < npt
    a = jnp.where(is_p, oap_ref[...], oas_ref[...])
    dd = jnp.where(is_p, odp_ref[...], ods_ref[...])
    b = _rms_heads(hf_ref[...] + hb_ref[...]) * mg_ref[...] * _sigmoid(mo_ref[...])
    r = _rms_heads(of_ref[...] + ob_ref[...]) * _silu(rg_ref[...])
    mix = (jnp.dot(a.astype(BF16), wout_ref[0:256, :], preferred_element_type=F32)
           + jnp.dot(b.astype(BF16), wout_ref[256:512, :], preferred_element_type=F32)
           + jnp.dot(r.astype(BF16), wout_ref[512:768, :], preferred_element_type=F32)
           + jnp.dot(dd.astype(BF16), wout_ref[768:1024, :], preferred_element_type=F32))
    m = mod_ref[0]
    x1 = jnp.where(is_p, xp_ref[...], xs_ref[...]) + m[2:3] * (_rms(mix) * npost_ref[...])
    x1_ref[...] = x1
    h2 = (_rms(x1) * npre_ref[...] * (1.0 + m[4:5]) + m[3:4]).astype(BF16)
    h2_ref[...] = h2
    logits_t = lax.dot_general(wrt_ref[...], h2, (((1,), (1,)), ((), ())), preferred_element_type=F32)
    w_t = _route(_sigmoid(logits_t), rb_ref[...])
    w_t = jnp.concatenate([w_t, jnp.zeros((LANES - N_EXPERTS, w_t.shape[1]), F32)], axis=0)
    wd_ref[...] = w_t.T


def _output_stage(oa, od, hf, hb, mo, of, ob, rg, xp, xs, mod, mg, npost, npre, wout, wrt, rb, dims):
    t_all = xp.shape[0] + xs.shape[0]
    npt, nst, tps = dims["tp"] // TM, xs.shape[0] // TM, dims["ns"] // TM
    row = lambda i: (i, 0)
    const2 = lambda i: (0, 0)
    tok = lambda w: pl.BlockSpec((TM, w), row)
    full = lambda a: pl.BlockSpec(a.shape, const2)
    hp = pl.BlockSpec((TM, 256), lambda i: (jnp.minimum(i, npt - 1), 0))
    hs = pl.BlockSpec((TM, 256), lambda i: (jnp.clip(i - npt, 0, nst - 1), 0))
    return pl.pallas_call(
        functools.partial(_out_kernel, npt=npt),
        grid=(t_all // TM,),
        in_specs=[hp, hs, hp, hs, tok(256), tok(256), tok(256), tok(256), tok(256), tok(256)]
        + _pair_specs(npt, nst, TM, D_MODEL)
        + [pl.BlockSpec((1, 6, D_MODEL), lambda i: (_mod_row(i, npt, tps), 0, 0)),
           full(mg), full(npost), full(npre), full(wout), full(wrt), full(rb)],
        out_specs=[tok(D_MODEL), tok(D_MODEL), tok(LANES)],
        out_shape=[jax.ShapeDtypeStruct((t_all, D_MODEL), F32),
                   jax.ShapeDtypeStruct((t_all, D_MODEL), BF16),
                   jax.ShapeDtypeStruct((t_all, LANES), F32)],
        compiler_params=_cparams(("parallel",)),
        name="output_stage",
    )(oa[0], oa[1], od[0], od[1], hf, hb, mo, of, ob, rg, xp, xs, mod, mg, npost, npre, wout, wrt, rb)


def _swiglu(h, wg, wu, wd):
    act = _silu(_bdot(h, wg)) * _bdot(h, wu)
    return _bdot(act, wd)


def _moe_kernel(h_ref, wd_ref, x1_ref, mod_ref, npost_ref, wg_ref, wu_ref, wdn_ref, sg_ref, su_ref, sd_ref,
                op_ref, os_ref, acc, *, npt):
    i = pl.program_id(0)
    e = pl.program_id(1)
    h = h_ref[...]

    @pl.when(e == 0)
    def _():
        acc[...] = _swiglu(h, sg_ref[...], su_ref[...], sd_ref[...])

    lane = lax.broadcasted_iota(jnp.int32, wd_ref.shape, 1)
    wcol = jnp.sum(jnp.where(lane == e, wd_ref[...], 0.0), axis=1, keepdims=True)
    acc[...] += _swiglu(h, wg_ref[0], wu_ref[0], wdn_ref[0]) * wcol

    def finish(o_ref):
        m = mod_ref[0]
        o_ref[...] = x1_ref[...] + m[5:6] * (_rms(acc[...]) * npost_ref[...])

    last = e == pl.num_programs(1) - 1
    pl.when(jnp.logical_and(last, i < npt))(lambda: finish(op_ref))
    pl.when(jnp.logical_and(last, i >= npt))(lambda: finish(os_ref))


def _moe_stage(h2, wd, x1, mod, npost, wg, wu, wdn, sg, su, sd, dims):
    t_all = h2.shape[0]
    tm = TM_MOE if (dims["tp"] % TM_MOE == 0 and dims["ns"] % TM_MOE == 0) else TM
    npt, tps = dims["tp"] // tm, dims["ns"] // tm
    nst = t_all // tm - npt
    row = lambda i, e: (i, 0)
    const2 = lambda i, e: (0, 0)
    exp3 = lambda i, e: (e, 0, 0)
    full = lambda a: pl.BlockSpec(a.shape, const2)
    return pl.pallas_call(
        functools.partial(_moe_kernel, npt=npt),
        grid=(t_all // tm, N_EXPERTS),
        in_specs=[pl.BlockSpec((tm, D_MODEL), row), pl.BlockSpec((tm, LANES), row),
                  pl.BlockSpec((tm, D_MODEL), row),
                  pl.BlockSpec((1, 6, D_MODEL), lambda i, e: (_mod_row(i, npt, tps), 0, 0)),
                  full(npost),
                  pl.BlockSpec((1, D_MODEL, D_EXPERT), exp3), pl.BlockSpec((1, D_MODEL, D_EXPERT), exp3),
                  pl.BlockSpec((1, D_EXPERT, D_MODEL), exp3),
                  full(sg), full(su), full(sd)],
        out_specs=_pair_specs(npt, nst, tm, D_MODEL),
        out_shape=[jax.ShapeDtypeStruct((npt * tm, D_MODEL), F32), jax.ShapeDtypeStruct((nst * tm, D_MODEL), F32)],
        scratch_shapes=[pltpu.VMEM((tm, D_MODEL), F32)],
        compiler_params=_cparams(("arbitrary", "arbitrary")),
        name="moe_experts",
    )(h2, wd, x1, mod, npost, wg, wu, wdn, sg, su, sd)


_PERM32 = np.concatenate([np.arange(8, 16), np.arange(0, 8), np.arange(24, 32), np.arange(16, 24)])


def _rope_tables(dims):
    n = dims["ns"]
    pos = jnp.arange(n)
    quarter = QK_ROPE // 4
    inv = 1.0 / (ROPE_THETA ** (jnp.arange(quarter, dtype=F32) / quarter))
    ang_r = (pos // GRID_W).astype(F32)[:, None] * inv[None, :]
    ang_c = (pos % GRID_W).astype(F32)[:, None] * inv[None, :]
    cos32 = jnp.concatenate([jnp.cos(ang_r)] * 2 + [jnp.cos(ang_c)] * 2, axis=-1)
    sin32 = jnp.concatenate([-jnp.sin(ang_r), jnp.sin(ang_r), -jnp.sin(ang_c), jnp.sin(ang_c)], axis=-1)
    cos256, sin256 = jnp.tile(cos32, (1, 8)), jnp.tile(sin32, (1, 8))
    one, zero = jnp.ones((n, QK_NOPE), F32), jnp.zeros((n, QK_NOPE), F32)
    cosq = jnp.concatenate([one, cos32] * N_HEADS, axis=-1)
    sinq = jnp.concatenate([zero, sin32] * N_HEADS, axis=-1)
    return cos256, sin256, cosq, sinq


def _split_w_in(w_in_l):
    sizes = (256, 256, 256, 256, 256, 256, 256, 16, 256, 256, 256, 256, Q_LORA, KV_LORA, QK_ROPE)
    idx = np.cumsum(sizes)[:-1]
    return jnp.split(w_in_l, [int(v) for v in idx], axis=-1)


def _layer_weights(w_in_l, w_uq_l):
    (a_q, a_k, a_v, m_q, m_k, m_v, m_o, m_g, r_q, r_k, r_v, r_g, d_cq, d_ckv, d_kr) = _split_w_in(w_in_l)
    perm256 = np.concatenate([_PERM32 + 32 * j for j in range(8)])
    pad = lambda w, n: jnp.pad(w, ((0, 0), (0, n - w.shape[1])))
    tail = pad(jnp.concatenate([d_kr, m_g], axis=-1), LANES)
    tail_s = pad(d_kr[:, _PERM32], LANES)
    w_all = jnp.concatenate([a_q, a_k, a_v, m_q, m_k, m_v, m_o, r_q, r_k, r_v, r_g, d_cq, d_ckv, tail,
                             a_q[:, perm256], a_k[:, perm256], tail_s], axis=-1).astype(BF16)
    per = QK_NOPE + QK_ROPE
    permq = np.concatenate([np.concatenate([np.arange(QK_NOPE), QK_NOPE + _PERM32]) + per * j
                            for j in range(N_HEADS)])
    return w_all, w_uq_l.astype(BF16), w_uq_l[:, permq].astype(BF16)


def kernel(x_prompt, x_sample, cache_diff_k, cache_diff_v, state_mlstm_C, state_mlstm_n, state_mlstm_m, state_ret_S, cache_mla_ckv, cache_mla_krope, c, c_ctx, w_mod, b_mod, norm_pre, norm_post, w_in, w_out, diff_lambda, diff_norm, mlstm_gate_bias, mlstm_norm, ret_decay_logit, mla_q_norm, mla_w_uq, mla_kv_norm, mla_w_ukv, moe_w_router, moe_router_bias, moe_w_gate, moe_w_up, moe_w_down, shared_w_gate, shared_w_up, shared_w_down):
    bp, n_p, _ = x_prompt.shape
    bs, n_s, _ = x_sample.shape
    depth = w_in.shape[0]
    past = cache_diff_k.shape[2]
    dims = dict(bp=bp, np=n_p, bs=bs, ns=n_s, tp=bp * n_p, past=past)
    tp, ts = bp * n_p, bs * n_s
    assert n_p % TM == 0 and n_s % TM == 0 and n_p % CHUNK == 0 and n_s % min(TK, n_s) == 0
    assert past % min(TK, past) == 0 and tp % n_s == 0 and bs + 1 <= 8 and n_s % GRID_W == 0

    xp, xs = x_prompt.reshape(tp, D_MODEL), x_sample.reshape(ts, D_MODEL)
    cond = jnp.zeros((8, D_MODEL), F32).at[0].set(c_ctx).at[1:1 + bs].set(c)
    mod_all = _modulation(cond, w_mod, b_mod).reshape(depth, 8, 6, D_MODEL)
    tabs = _rope_tables(dims)
    scan_tab = _scan_table(dims)

    outs = [[] for _ in range(8)]
    for l in range(depth):
        lam_init = 0.8 - 0.6 * math.exp(-0.3 * l)
        mod = mod_all[l]
        w_all, wuq, wuqs = _layer_weights(w_in[l], mla_w_uq[l])
        wukv = mla_w_ukv[l].astype(BF16)
        (aq1, aq2, ak1t, ak2t, avh, ak, av, mqkv, mo, rqkv, rg, tail, qmla, ckv, kmlat, vmla) = _input_stage(
            xp, xs, mod, norm_pre[l, 0:1], w_all, tabs, mla_q_norm[l][None], wuq, wuqs,
            mla_kv_norm[l][None], wukv, dims)

        lamv, dg = diff_lambda[l], diff_norm[l][None]
        ckt = jnp.transpose(cache_diff_k[:, l], (2, 3, 0, 1)).reshape(N_HEADS, HEAD_DIM, bs * past).astype(BF16)
        cv = jnp.transpose(cache_diff_v[:, l], (2, 0, 1, 3)).reshape(N_HEADS, bs * past, HEAD_DIM).astype(BF16)
        oa_p = _attention("diff_attn_ctx", [aq1, aq2], [ak1t, ak2t], avh, None, None, [lamv, dg],
                          0, bp, n_p, 0, lam_init)
        oa_s = _attention("diff_attn_latent", [aq1, aq2], [ak1t, ak2t], avh,
                          [ckt[:, :DIFF_HALF], ckt[:, DIFF_HALF:]], cv, [lamv, dg], tp, bs, n_s, past, lam_init)

        ckv_ctx = cache_mla_ckv[:, l].reshape(bs * past, KV_LORA)
        kvc = _project(ckv_ctx, wukv, BF16).reshape(bs * past, N_HEADS, QK_NOPE + V_HEAD)
        krc = jnp.broadcast_to(cache_mla_krope[:, l].reshape(bs * past, 1, QK_ROPE).astype(BF16),
                               (bs * past, N_HEADS, QK_ROPE))
        kct = jnp.transpose(jnp.concatenate([kvc[..., :QK_NOPE], krc], axis=-1), (1, 2, 0))
        vc = jnp.transpose(kvc[..., QK_NOPE:], (1, 0, 2))
        od_p = _attention("mla_attn_ctx", [qmla], [kmlat], vmla, None, None, [], 0, bp, n_p, 0)
        od_s = _attention("mla_attn_latent", [qmla], [kmlat], vmla, [kct], vc, [], tp, bs, n_s, past)

        zeros = lambda a: jnp.zeros((bp,) + a.shape[1:], F32)
        st_c, st_n, st_m, st_s = state_mlstm_C[:, l], state_mlstm_n[:, l], state_mlstm_m[:, l], state_ret_S[:, l]
        bias_row = jnp.zeros((1, LANES), F32).at[0, TAIL_GATE0:TAIL_GATE0 + 16].set(mlstm_gate_bias[l].reshape(16))
        hf, hb, c_n, n_n, m_n = _mlstm_scan(scan_tab, mqkv, tail, bias_row,
                                            jnp.concatenate([zeros(st_c), st_c]),
                                            jnp.concatenate([zeros(st_n), st_n]),
                                            jnp.concatenate([zeros(st_m), st_m]))
        of, ob, s_n = _ret_scan(scan_tab, rqkv, ret_decay_logit[l], jnp.concatenate([zeros(st_s), st_s]))

        x1, h2, wd = _output_stage((oa_p, oa_s), (od_p, od_s), hf, hb, mo, of, ob, rg, xp, xs, mod,
                                   mlstm_norm[l][None], norm_post[l, 0:1], norm_pre[l, 1:2],
                                   w_out[l].astype(BF16), moe_w_router[l].T.astype(BF16),
                                   moe_router_bias[l][:, None], dims)
        xp, xs = _moe_stage(h2, wd, x1, mod, norm_post[l, 1:2], moe_w_gate[l], moe_w_up[l], moe_w_down[l],
                            shared_w_gate[l], shared_w_up[l], shared_w_down[l], dims)

        outs[0].append(ak[:tp].reshape(bp, n_p, N_HEADS, HEAD_DIM))
        outs[1].append(av[:tp].reshape(bp, n_p, N_HEADS, HEAD_DIM))
        outs[2].append(c_n[:bp])
        outs[3].append(n_n[:bp])
        outs[4].append(m_n[:bp])
        outs[5].append(s_n[:bp])
        outs[6].append(ckv[:tp].reshape(bp, n_p, KV_LORA))
        outs[7].append(tail[:tp, :QK_ROPE].reshape(bp, n_p, QK_ROPE))

    return (xp.reshape(bp, n_p, D_MODEL), xs.reshape(bs, n_s, D_MODEL)) + tuple(
        jnp.stack(o, axis=1) for o in outs)
```

```python
import functools
import math

import numpy as np
import jax
import jax.numpy as jnp
from jax import lax
from jax.experimental import pallas as pl
from jax.experimental.pallas import tpu as pltpu

F32 = jnp.float32
BF16 = jnp.bfloat16
HIGHEST = lax.Precision.HIGHEST

D_MODEL = 1024
GRID_W = 64
GROUP_WIDTH = 256
HEAD_DIM = 64
N_HEADS = 4
DIFF_HALF = 32
ROPE_THETA = 10000.0
Q_LORA = 256
KV_LORA = 128
QK_NOPE = 64
QK_ROPE = 32
V_HEAD = 64
N_EXPERTS = 64
TOP_K = 8
N_GROUPS = 8
TOPK_GROUPS = 4
D_EXPERT = 256
ROUTED_SCALE = 2.5
CHUNK = 128
EPS = 1e-6
Q_MLA = N_HEADS * (QK_NOPE + QK_ROPE)
KV_MLA = N_HEADS * (QK_NOPE + V_HEAD)

LANES = 128
VMEM_LIMIT = 56 * 1024 * 1024

TM = 256
TM_MOE = 1024
TQ = 256
TK = 2048

C_AQ, C_AK, C_AV = 0, 256, 512
C_MQKV, C_MO = 768, 1536
C_RQKV, C_RG = 1792, 2560
C_CQ, C_CKV, C_TAIL = 2816, 3072, 3200
C_AQS, C_AKS, C_TAILS = 3328, 3584, 3840
W_ALL = 3968
TAIL_GATE0 = QK_ROPE


def _cparams(sem, flags=None):
    return pltpu.CompilerParams(dimension_semantics=sem, vmem_limit_bytes=VMEM_LIMIT, flags=flags)


def _rms(x):
    return x * lax.rsqrt(jnp.mean(x * x, axis=-1, keepdims=True) + EPS)


def _head_mean_matrix(width):
    r = lax.broadcasted_iota(jnp.int32, (width, width), 0) // HEAD_DIM
    c = lax.broadcasted_iota(jnp.int32, (width, width), 1) // HEAD_DIM
    return jnp.where(r == c, 1.0 / HEAD_DIM, 0.0).astype(F32)


def _rms_heads(x):
    ms = jnp.dot(x * x, _head_mean_matrix(x.shape[-1]), precision=HIGHEST, preferred_element_type=F32)
    return x * lax.rsqrt(ms + EPS)


def _sigmoid(x):
    return 1.0 / (1.0 + jnp.exp(-x))


def _silu(x):
    return x * _sigmoid(x)


def _log_sigmoid(x):
    return jnp.minimum(x, 0.0) - jnp.log1p(jnp.exp(-jnp.abs(x)))


def _bdot(a, b):
    return jnp.dot(a.astype(BF16), b.astype(BF16), preferred_element_type=F32)


def _mod_kernel(c_ref, w_ref, b_ref, o_ref):
    o_ref[0] = _bdot(_silu(c_ref[...]), w_ref[0]) + b_ref[0]


def _modulation(cond, w_mod, b_mod):
    depth, _, n = w_mod.shape
    tn = 1536
    return pl.pallas_call(
        _mod_kernel,
        grid=(depth, n // tn),
        in_specs=[pl.BlockSpec((8, D_MODEL), lambda l, j: (0, 0)),
                  pl.BlockSpec((1, D_MODEL, tn), lambda l, j: (l, 0, j)),
                  pl.BlockSpec((1, 1, tn), lambda l, j: (l, 0, j))],
        out_specs=pl.BlockSpec((1, 8, tn), lambda l, j: (l, 0, j)),
        out_shape=jax.ShapeDtypeStruct((depth, 8, n), F32),
        compiler_params=_cparams(("parallel", "parallel")),
        name="adaln_mod",
    )(cond, w_mod, b_mod.reshape(depth, 1, n))


def _in_kernel(xp_ref, xs_ref, mod_ref, npre_ref, w_ref, cos_ref, sin_ref, cosq_ref, sinq_ref,
               qg_ref, wuq_ref, wuqs_ref, kvg_ref, wukv_ref,
               aq1_ref, aq2_ref, ak1t_ref, ak2t_ref, avh_ref, ak_ref, av_ref,
               mqkv_ref, mo_ref, rqkv_ref, rg_ref, tail_ref,
               qmla_ref, ckv_ref, kmlat_ref, vmla_ref, *, npt):
    is_ctx = pl.program_id(0) < npt
    x = jnp.where(is_ctx, xp_ref[...], xs_ref[...])
    m = mod_ref[0]
    h = (_rms(x) * npre_ref[...] * (1.0 + m[1:2]) + m[0:1]).astype(BF16)

    def proj(c0, width):
        return jnp.dot(h, w_ref[:, c0:c0 + width], preferred_element_type=F32)

    cos = jnp.where(is_ctx, 1.0, cos_ref[...])
    sin = jnp.where(is_ctx, 0.0, sin_ref[...])
    aq = (proj(C_AQ, 256) * cos + proj(C_AQS, 256) * sin) * (DIFF_HALF ** -0.5)
    ak = proj(C_AK, 256) * cos + proj(C_AKS, 256) * sin
    av = proj(C_AV, 256)
    ak_ref[...] = ak
    av_ref[...] = av
    ak_t = ak.T.astype(BF16)
    for hd in range(N_HEADS):
        lo = hd * HEAD_DIM
        aq1_ref[hd] = aq[:, lo:lo + DIFF_HALF].astype(BF16)
        aq2_ref[hd] = aq[:, lo + DIFF_HALF:lo + HEAD_DIM].astype(BF16)
        ak1t_ref[hd] = ak_t[lo:lo + DIFF_HALF, :]
        ak2t_ref[hd] = ak_t[lo + DIFF_HALF:lo + HEAD_DIM, :]
        avh_ref[hd] = av[:, lo:lo + HEAD_DIM].astype(BF16)

    mqkv_ref[...] = proj(C_MQKV, 768).astype(BF16)
    mo_ref[...] = proj(C_MO, 256)
    rqkv_ref[...] = proj(C_RQKV, 768).astype(BF16)
    rg_ref[...] = proj(C_RG, 256)

    tail = proj(C_TAIL, LANES)
    tail_ref[...] = tail
    kr_t = (tail * cos[:, :LANES] + proj(C_TAILS, LANES) * sin[:, :LANES]).T[:QK_ROPE, :].astype(BF16)

    cqn = (_rms(proj(C_CQ, Q_LORA)) * qg_ref[...]).astype(BF16)
    q = jnp.dot(cqn, wuq_ref[...], preferred_element_type=F32)
    qs = jnp.dot(cqn, wuqs_ref[...], preferred_element_type=F32)
    cosq = jnp.where(is_ctx, 1.0, cosq_ref[...])
    sinq = jnp.where(is_ctx, 0.0, sinq_ref[...])
    qmla = (q * cosq + qs * sinq) * ((QK_NOPE + QK_ROPE) ** -0.5)
    for hd in range(N_HEADS):
        lo = hd * (QK_NOPE + QK_ROPE)
        qmla_ref[hd] = qmla[:, lo:lo + QK_NOPE + QK_ROPE].astype(BF16)
    ckvn = _rms(proj(C_CKV, KV_LORA)) * kvg_ref[...]
    ckv_ref[...] = ckvn
    kv = jnp.dot(ckvn.astype(BF16), wukv_ref[...], preferred_element_type=F32)
    kv_t = kv.T.astype(BF16)
    per = QK_NOPE + V_HEAD
    for hd in range(N_HEADS):
        kmlat_ref[hd, 0:QK_NOPE, :] = kv_t[hd * per:hd * per + QK_NOPE, :]
        kmlat_ref[hd, QK_NOPE:QK_NOPE + QK_ROPE, :] = kr_t
        vmla_ref[hd] = kv[:, hd * per + QK_NOPE:(hd + 1) * per].astype(BF16)


def _mod_row(i, npt, tps):
    return jnp.where(i < npt, 0, 1 + (i - npt) // tps)


def _pair_specs(npt, nst, tm, width):
    return [pl.BlockSpec((tm, width), lambda i, *_: (jnp.minimum(i, npt - 1), 0)),
            pl.BlockSpec((tm, width), lambda i, *_: (jnp.clip(i - npt, 0, nst - 1), 0))]


def _input_stage(xp, xs, mod, npre, w_all, tabs, qg, wuq, wuqs, kvg, wukv, dims):
    t_all = xp.shape[0] + xs.shape[0]
    npt, nst, tps = dims["tp"] // TM, xs.shape[0] // TM, dims["ns"] // TM
    row = lambda i: (i, 0)
    tab = lambda w: pl.BlockSpec((TM, w), lambda i: (jnp.maximum(i - npt, 0) % tps, 0))
    hrow = lambda i: (0, i, 0)
    const2 = lambda i: (0, 0)
    tok = lambda w: pl.BlockSpec((TM, w), row)
    headed = lambda w: pl.BlockSpec((N_HEADS, TM, w), hrow)
    headed_t = lambda d: pl.BlockSpec((N_HEADS, d, TM), lambda i: (0, 0, i))
    full = lambda a: pl.BlockSpec(a.shape, const2)
    cos, sin, cosq, sinq = tabs
    out_shapes = [
        (headed(DIFF_HALF), (N_HEADS, t_all, DIFF_HALF), BF16),
        (headed(DIFF_HALF), (N_HEADS, t_all, DIFF_HALF), BF16),
        (headed_t(DIFF_HALF), (N_HEADS, DIFF_HALF, t_all), BF16),
        (headed_t(DIFF_HALF), (N_HEADS, DIFF_HALF, t_all), BF16),
        (headed(HEAD_DIM), (N_HEADS, t_all, HEAD_DIM), BF16),
        (tok(256), (t_all, 256), F32),
        (tok(256), (t_all, 256), F32),
        (tok(768), (t_all, 768), BF16),
        (tok(256), (t_all, 256), F32),
        (tok(768), (t_all, 768), BF16),
        (tok(256), (t_all, 256), F32),
        (tok(LANES), (t_all, LANES), F32),
        (headed(QK_NOPE + QK_ROPE), (N_HEADS, t_all, QK_NOPE + QK_ROPE), BF16),
        (tok(KV_LORA), (t_all, KV_LORA), F32),
        (headed_t(QK_NOPE + QK_ROPE), (N_HEADS, QK_NOPE + QK_ROPE, t_all), BF16),
        (headed(V_HEAD), (N_HEADS, t_all, V_HEAD), BF16),
    ]
    return pl.pallas_call(
        functools.partial(_in_kernel, npt=npt),
        grid=(t_all // TM,),
        in_specs=_pair_specs(npt, nst, TM, D_MODEL) + [
            pl.BlockSpec((1, 6, D_MODEL), lambda i: (_mod_row(i, npt, tps), 0, 0)),
            full(npre), full(w_all), tab(256), tab(256), tab(Q_MLA), tab(Q_MLA),
            full(qg), full(wuq), full(wuqs), full(kvg), full(wukv)],
        out_specs=[s for s, _, _ in out_shapes],
        out_shape=[jax.ShapeDtypeStruct(shp, dt) for _, shp, dt in out_shapes],
        compiler_params=_cparams(("parallel",)),
        name="input_stage",
    )(xp, xs, mod, npre, w_all, cos, sin, cosq, sinq, qg, wuq, wuqs, kvg, wukv)


def _attn_kernel(*refs, n_soft, n_new, n_ctx, lam_init):
    refs = list(refs)
    q_refs, kt_refs, v_ref = refs[:n_soft], refs[n_soft:2 * n_soft], refs[2 * n_soft]
    pos = 2 * n_soft + 1
    if n_ctx:
        ckt_refs, cv_ref = refs[pos:pos + n_soft], refs[pos + n_soft]
        pos += n_soft + 1
    if n_soft == 2:
        lam_ref, g_ref = refs[pos:pos + 2]
        pos += 2
    o_ref = refs[pos]
    tq = o_ref.shape[0]
    nchain = n_soft * N_HEADS

    def chunk(state, kt_of, v_of):
        ms, ls, accs = state
        new_m, new_l, new_acc = list(ms), list(ls), list(accs)
        order = [(j, hd) for hd in range(N_HEADS) for j in range(n_soft)]
        ss = [jnp.dot(q_refs[j][hd], kt_of(j, hd), preferred_element_type=F32) for j, hd in order]
        ps, alphas = [], []
        for (j, hd), s in zip(order, ss):
            c = j * N_HEADS + hd
            m_new = jnp.maximum(ms[c], jnp.broadcast_to(jnp.max(s, axis=-1, keepdims=True), (tq, LANES)))
            p = jnp.exp(s - jnp.tile(m_new, (1, s.shape[1] // LANES)))
            alpha = jnp.exp(ms[c] - m_new)
            new_l[c] = alpha * ls[c] + jnp.broadcast_to(jnp.sum(p, axis=-1, keepdims=True), (tq, LANES))
            new_m[c] = m_new
            ps.append(p.astype(BF16))
            alphas.append(alpha)
        for (j, hd), p, alpha in zip(order, ps, alphas):
            c = j * N_HEADS + hd
            new_acc[c] = alpha[:, :V_HEAD] * accs[c] + jnp.dot(p, v_of(hd), preferred_element_type=F32)
        return tuple(new_m), tuple(new_l), tuple(new_acc)

    state = (tuple(jnp.full((tq, LANES), -jnp.inf, F32) for _ in range(nchain)),
             tuple(jnp.zeros((tq, LANES), F32) for _ in range(nchain)),
             tuple(jnp.zeros((tq, V_HEAD), F32) for _ in range(nchain)))
    if n_ctx:
        cstep = min(TK, n_ctx)
        for i in range(n_ctx // cstep):
            state = chunk(state, lambda j, hd, i=i: ckt_refs[j][hd, :, i * cstep:(i + 1) * cstep],
                          lambda hd, i=i: cv_ref[hd, i * cstep:(i + 1) * cstep, :])
    step = min(TK, n_new)

    def body(i, st):
        start = pl.multiple_of(i * step, step)
        return chunk(st, lambda j, hd: kt_refs[j][hd, :, pl.ds(start, step)],
                     lambda hd: v_ref[hd, pl.ds(start, step), :])

    _, ls, accs = lax.fori_loop(0, n_new // step, body, state)

    if n_soft == 2:
        lv = lam_ref[...]
        lam = (jnp.exp(jnp.sum(lv[0:1] * lv[1:2], axis=-1, keepdims=True))
               - jnp.exp(jnp.sum(lv[2:3] * lv[3:4], axis=-1, keepdims=True)) + lam_init)
    for hd in range(N_HEADS):
        out = accs[hd] / ls[hd][:, :V_HEAD]
        if n_soft == 2:
            a = out - lam * (accs[N_HEADS + hd] / ls[N_HEADS + hd][:, :V_HEAD])
            out = _rms(a) * g_ref[...] * (1.0 - lam_init)
        o_ref[:, hd * V_HEAD:(hd + 1) * V_HEAD] = out


def _attention(name, qs, kts, v, ctx_kts, ctx_v, extras, row0, nb, n, n_ctx, lam_init=0.0):
    tq = min(TQ, n)
    nqt = n // tq
    n_soft = len(qs)
    qmap = lambda b, i: (0, row0 // tq + b * nqt + i, 0)
    in_specs = [pl.BlockSpec((N_HEADS, tq, a.shape[-1]), qmap) for a in qs]
    in_specs += [pl.BlockSpec((N_HEADS, a.shape[1], n), lambda b, i: (0, 0, row0 // n + b)) for a in kts]
    in_specs += [pl.BlockSpec((N_HEADS, n, V_HEAD), lambda b, i: (0, row0 // n + b, 0))]
    args = list(qs) + list(kts) + [v]
    if n_ctx:
        in_specs += [pl.BlockSpec((N_HEADS, a.shape[1], n_ctx), lambda b, i: (0, 0, b)) for a in ctx_kts]
        in_specs += [pl.BlockSpec((N_HEADS, n_ctx, V_HEAD), lambda b, i: (0, b, 0))]
        args += list(ctx_kts) + [ctx_v]
    in_specs += [pl.BlockSpec(a.shape, lambda b, i: (0, 0)) for a in extras]
    args += list(extras)
    return pl.pallas_call(
        functools.partial(_attn_kernel, n_soft=n_soft, n_new=n, n_ctx=n_ctx, lam_init=lam_init),
        grid=(nb, nqt),
        in_specs=in_specs,
        out_specs=pl.BlockSpec((tq, N_HEADS * V_HEAD), lambda b, i: (b * nqt + i, 0)),
        out_shape=jax.ShapeDtypeStruct((nb * n, N_HEADS * V_HEAD), F32),
        compiler_params=_cparams(("parallel", "parallel")),
        name=name,
    )(*args)


def _tri(lower):
    r = lax.broadcasted_iota(jnp.int32, (CHUNK, CHUNK), 0)
    c = lax.broadcasted_iota(jnp.int32, (CHUNK, CHUNK), 1)
    return (c <= r) if lower else (c >= r)


def _nt(a, b):
    return lax.dot_general(a, b, (((1,), (1,)), ((), ())), preferred_element_type=F32)


def _tn(a, b):
    return lax.dot_general(a, b, (((0,), (0,)), ((), ())), preferred_element_type=F32)


def _head_slices(qkv, hd):
    lo = hd * HEAD_DIM
    return qkv[:, lo:lo + HEAD_DIM], qkv[:, 256 + lo:256 + lo + HEAD_DIM], qkv[:, 512 + lo:512 + lo + HEAD_DIM]


def _row_dot(row, mat, transpose_mat):
    row8 = jnp.broadcast_to(row, (8, row.shape[1])).astype(BF16)
    dims = (((1,), (1,)), ((), ())) if transpose_mat else (((1,), (0,)), ((), ()))
    return lax.dot_general(row8, mat, dims, preferred_element_type=F32)[0:1]


def _load_state_t(dst, src):
    for d in range(2):
        for hd in range(N_HEADS):
            dst[d, hd] = src[d, hd].T


def _mlstm_step(dirs, bias_ref, ct_s, n_s, m_s):
    k_scale = HEAD_DIM ** -0.5
    lower = _tri(True).astype(F32)
    upper = _tri(False).astype(F32)
    chains = []
    for d, qkv_ref, tail_ref, _ in dirs:
        g = tail_ref[...] + bias_ref[...]
        ls = _log_sigmoid(g)
        g_t, ls_t = g.T, ls.T
        left, right = (lower, upper) if d == 0 else (upper, lower)
        cum_col = jnp.dot(left, ls, precision=HIGHEST, preferred_element_type=F32)
        cum_row = jnp.dot(ls_t, right, precision=HIGHEST, preferred_element_type=F32)
        qkv = qkv_ref[...]
        for hd in range(N_HEADS):
            ci = TAIL_GATE0 + 4 * (2 * d) + hd
            cf = ci + 4
            q, k, v = _head_slices(qkv, hd)
            chains.append(dict(
                d=d, hd=hd, q=q, k=k, v=v, li_row=g_t[ci:ci + 1, :], b_row=cum_row[cf:cf + 1, :],
                c_col=g[:, ci:ci + 1] - cum_col[:, cf:cf + 1],
                m_prev=m_s[d:d + 1, hd:hd + 1], ct=ct_s[d, hd], n=n_s[d, hd:hd + 1, :]))
    for c in chains:
        c["s"] = _nt(c["k"], c["q"])
        c["v_t"] = c["v"].T
    for c in chains:
        valid = _tri(c["d"] != 0)
        log_d = jnp.where(valid, c["b_row"] + c["c_col"], -jnp.inf)
        log_inter = c["b_row"] + c["m_prev"]
        c["m_t"] = jnp.maximum(log_inter, jnp.max(log_d, axis=0, keepdims=True))
        c["w"] = c["s"] * k_scale * jnp.exp(log_d - c["m_t"])
        c["w_inter"] = jnp.exp(log_inter - c["m_t"])
    for c in chains:
        num = (jnp.dot(c["v_t"], c["w"].astype(BF16), preferred_element_type=F32)
               + c["w_inter"] * _nt(c["ct"].astype(BF16), c["q"]))
        den = jnp.sum(c["w"], axis=0, keepdims=True) + c["w_inter"] * _row_dot(c["n"], c["q"], True)
        c["h"] = num / jnp.maximum(jnp.abs(den), jnp.exp(-c["m_t"]))
    for c in chains:
        last = CHUNK - 1 if c["d"] == 0 else 0
        c["m_new"] = c["m_t"][:, last:last + 1]
        b_last = c["b_row"][:, last:last + 1]
        w_end_row = jnp.exp(b_last - c["b_row"] + c["li_row"] - c["m_new"])
        w_end_col = jnp.exp(c["c_col"] + (b_last - c["m_new"]))
        decay = jnp.exp(b_last + c["m_prev"] - c["m_new"])
        kw = (c["k"].astype(F32) * (w_end_col * k_scale)).astype(BF16)
        c["ct_new"] = decay * c["ct"] + jnp.dot(c["v_t"], kw, preferred_element_type=F32)
        c["n_new"] = decay * c["n"] + _row_dot(w_end_row, c["k"], False) * k_scale
    for j, (_, _, _, h_ref) in enumerate(dirs):
        h_ref[...] = jnp.concatenate([c["h"] for c in chains[j * N_HEADS:(j + 1) * N_HEADS]], axis=0).T
    for c in chains:
        d, hd = c["d"], c["hd"]
        ct_s[d, hd] = c["ct_new"]
        n_s[d, hd:hd + 1, :] = c["n_new"]
        m_s[d:d + 1, hd:hd + 1] = c["m_new"]


def _mlstm_kernel(tab_ref, qkvf_ref, tailf_ref, qkvb_ref, tailb_ref, bias_ref, c0_ref, n0_ref, m0_ref,
                  hf_ref, hb_ref, cn_ref, nn_ref, mn_ref, ct_s, n_s, m_s):
    i = pl.program_id(0)

    @pl.when(tab_ref[3, i] == 1)
    def _():
        _load_state_t(ct_s, c0_ref.at[0])
        n_s[...] = n0_ref[0]
        m_s[...] = m0_ref[0]

    _mlstm_step([(0, qkvf_ref, tailf_ref, hf_ref), (1, qkvb_ref, tailb_ref, hb_ref)], bias_ref, ct_s, n_s, m_s)

    @pl.when(tab_ref[4, i] == 1)
    def _():
        _load_state_t(cn_ref.at[0], ct_s)
        nn_ref[0] = n_s[...]
        mn_ref[0] = m_s[...]


def _ret_step(dirs, lg_ref, st_s):
    k_scale = HEAD_DIM ** -0.5
    s = lax.broadcasted_iota(jnp.int32, (CHUNK, CHUNK), 0)
    t = lax.broadcasted_iota(jnp.int32, (CHUNK, CHUNK), 1)
    lane = lax.broadcasted_iota(jnp.int32, (1, CHUNK), 1)
    chains = []
    for d, qkv_ref, _ in dirs:
        lag = (t - s) if d == 0 else (s - t)
        pos = (lane if d == 0 else CHUNK - 1 - lane).astype(F32)
        qkv = qkv_ref[...]
        for hd in range(N_HEADS):
            q, k, v = _head_slices(qkv, hd)
            chains.append(dict(d=d, hd=hd, q=q, k=k, v=v, lag=lag, pos=pos, st=st_s[d, hd],
                               lg=_log_sigmoid(lg_ref[d:d + 1, hd:hd + 1])))
    for c in chains:
        c["s"] = _nt(c["k"], c["q"])
        c["v_t"] = c["v"].T
    for c in chains:
        intra = jnp.where(c["lag"] >= 0, jnp.exp(jnp.maximum(c["lag"], 0).astype(F32) * c["lg"]), 0.0)
        c["a"] = (c["s"] * k_scale * intra).astype(BF16)
    for c in chains:
        inter = jnp.exp((c["pos"] + 1.0) * c["lg"])
        c["o"] = jnp.dot(c["v_t"], c["a"], preferred_element_type=F32) + inter * _nt(c["st"].astype(BF16), c["q"])
    row = lax.broadcasted_iota(jnp.int32, (CHUNK, HEAD_DIM), 0)
    for c in chains:
        src_pos = (row if c["d"] == 0 else CHUNK - 1 - row).astype(F32)
        tail = jnp.exp((CHUNK - 1.0 - src_pos) * c["lg"]) * k_scale
        kw = (c["k"].astype(F32) * tail).astype(BF16)
        c["st_new"] = jnp.exp(CHUNK * c["lg"]) * c["st"] + jnp.dot(c["v_t"], kw, preferred_element_type=F32)
    for j, (_, _, o_ref) in enumerate(dirs):
        o_ref[...] = jnp.concatenate([c["o"] for c in chains[j * N_HEADS:(j + 1) * N_HEADS]], axis=0).T
    for c in chains:
        st_s[c["d"], c["hd"]] = c["st_new"]


def _ret_kernel(tab_ref, qkvf_ref, qkvb_ref, lg_ref, s0_ref, of_ref, ob_ref, sn_ref, st_s):
    i = pl.program_id(0)

    @pl.when(tab_ref[3, i] == 1)
    def _():
        _load_state_t(st_s, s0_ref.at[0])

    _ret_step([(0, qkvf_ref, of_ref), (1, qkvb_ref, ob_ref)], lg_ref, st_s)

    @pl.when(tab_ref[4, i] == 1)
    def _():
        _load_state_t(sn_ref.at[0], st_s)


def _scan_table(dims):
    rows = []
    seq = 0
    for base, nb, n in ((0, dims["bp"], dims["np"]), (dims["tp"], dims["bs"], dims["ns"])):
        nch = n // CHUNK
        for b in range(nb):
            blk0 = (base + b * n) // CHUNK
            for c in range(nch):
                rows.append((blk0 + c, blk0 + nch - 1 - c, seq, int(c == 0), int(c == nch - 1)))
            seq += 1
    return jnp.asarray(np.array(rows, dtype=np.int32).T)


def _mlstm_scan(tab, mqkv, tail, bias_row, c0, n0, m0):
    t_all = mqkv.shape[0]
    nseq = c0.shape[0]
    fwd = lambda i, tab: (tab[0, i], 0)
    bwd = lambda i, tab: (tab[1, i], 0)
    st = lambda nd: (lambda i, tab: (tab[2, i],) + (0,) * nd)
    grid_spec = pltpu.PrefetchScalarGridSpec(
        num_scalar_prefetch=1,
        grid=(tab.shape[1],),
        in_specs=[pl.BlockSpec((CHUNK, 768), fwd), pl.BlockSpec((CHUNK, LANES), fwd),
                  pl.BlockSpec((CHUNK, 768), bwd), pl.BlockSpec((CHUNK, LANES), bwd),
                  pl.BlockSpec((1, LANES), lambda i, tab: (0, 0)),
                  pl.BlockSpec((1,) + c0.shape[1:], st(4)),
                  pl.BlockSpec((1,) + n0.shape[1:], st(3)),
                  pl.BlockSpec((1,) + m0.shape[1:], st(2))],
        out_specs=[pl.BlockSpec((CHUNK, 256), fwd), pl.BlockSpec((CHUNK, 256), bwd),
                   pl.BlockSpec((1,) + c0.shape[1:], st(4)),
                   pl.BlockSpec((1,) + n0.shape[1:], st(3)),
                   pl.BlockSpec((1,) + m0.shape[1:], st(2))],
        scratch_shapes=[pltpu.VMEM(c0.shape[1:], F32), pltpu.VMEM(n0.shape[1:], F32),
                        pltpu.VMEM(m0.shape[1:], F32)],
    )
    return pl.pallas_call(
        _mlstm_kernel,
        grid_spec=grid_spec,
        out_shape=[jax.ShapeDtypeStruct((t_all, 256), F32), jax.ShapeDtypeStruct((t_all, 256), F32),
                   jax.ShapeDtypeStruct(c0.shape, F32), jax.ShapeDtypeStruct(n0.shape, F32),
                   jax.ShapeDtypeStruct(m0.shape, F32)],
        compiler_params=_cparams(("arbitrary",)),
        name="mlstm_scan",
    )(tab, mqkv, tail, mqkv, tail, bias_row, c0, n0, m0)


def _ret_scan(tab, rqkv, decay_logit, s0):
    t_all = rqkv.shape[0]
    fwd = lambda i, tab: (tab[0, i], 0)
    bwd = lambda i, tab: (tab[1, i], 0)
    st = lambda i, tab: (tab[2, i], 0, 0, 0, 0)
    grid_spec = pltpu.PrefetchScalarGridSpec(
        num_scalar_prefetch=1,
        grid=(tab.shape[1],),
        in_specs=[pl.BlockSpec((CHUNK, 768), fwd), pl.BlockSpec((CHUNK, 768), bwd),
                  pl.BlockSpec(decay_logit.shape, lambda i, tab: (0, 0)),
                  pl.BlockSpec((1,) + s0.shape[1:], st)],
        out_specs=[pl.BlockSpec((CHUNK, 256), fwd), pl.BlockSpec((CHUNK, 256), bwd),
                   pl.BlockSpec((1,) + s0.shape[1:], st)],
        scratch_shapes=[pltpu.VMEM(s0.shape[1:], F32)],
    )
    return pl.pallas_call(
        _ret_kernel,
        grid_spec=grid_spec,
        out_shape=[jax.ShapeDtypeStruct((t_all, 256), F32), jax.ShapeDtypeStruct((t_all, 256), F32),
                   jax.ShapeDtypeStruct(s0.shape, F32)],
        compiler_params=_cparams(("arbitrary",)),
        name="retention_scan",
    )(tab, rqkv, rqkv, decay_logit, s0)


def _proj_kernel(x_ref, w_ref, o_ref):
    o_ref[...] = _bdot(x_ref[...], w_ref[...]).astype(o_ref.dtype)


def _project(x, w, dtype):
    return pl.pallas_call(
        _proj_kernel,
        out_shape=jax.ShapeDtypeStruct((x.shape[0], w.shape[1]), dtype),
        name="ctx_kv_proj",
    )(x, w)


def _first_argmax_mask(cur, axis, size):
    io = lax.broadcasted_iota(jnp.int32, cur.shape, axis)
    mx = jnp.max(cur, axis=axis, keepdims=True)
    ix = jnp.min(jnp.where(cur == mx, io, size), axis=axis, keepdims=True)
    return io == ix


def _route(scores_t, bias_col):
    tm = scores_t.shape[1]
    per = N_EXPERTS // N_GROUPS
    sel = scores_t + bias_col
    s3 = sel.reshape(N_GROUPS, per, tm)
    hit1 = _first_argmax_mask(s3, 1, per)
    m1 = jnp.max(s3, axis=1, keepdims=True)
    m2 = jnp.max(jnp.where(hit1, -jnp.inf, s3), axis=1, keepdims=True)
    cur = m1 + m2
    gsel = None
    for _ in range(TOPK_GROUPS):
        hit = _first_argmax_mask(cur, 0, N_GROUPS)
        gsel = hit if gsel is None else jnp.logical_or(gsel, hit)
        cur = jnp.where(hit, -jnp.inf, cur)
    cur = jnp.where(gsel, s3, -jnp.inf).reshape(N_EXPERTS, tm)
    chosen = None
    for _ in range(TOP_K):
        hit = _first_argmax_mask(cur, 0, N_EXPERTS)
        chosen = hit if chosen is None else jnp.logical_or(chosen, hit)
        cur = jnp.where(hit, -jnp.inf, cur)
    w = jnp.where(chosen, scores_t, 0.0)
    return w / jnp.sum(w, axis=0, keepdims=True) * ROUTED_SCALE


def _out_kernel(oap_ref, oas_ref, odp_ref, ods_ref, hf_ref, hb_ref, mo_ref, of_ref, ob_ref, rg_ref,
                xp_ref, xs_ref, mod_ref, mg_ref, npost_ref, npre_ref, wout_ref, wrt_ref, rb_ref,
                x1_ref, h2_ref, wd_ref, *, npt):
    i = pl.program_id(0)
    is_p = i < npt
    a = jnp.where(is_p, oap_ref[...], oas_ref[...])
    dd = jnp.where(is_p, odp_ref[...], ods_ref[...])
    b = _rms_heads(hf_ref[...] + hb_ref[...]) * mg_ref[...] * _sigmoid(mo_ref[...])
    r = _rms_heads(of_ref[...] + ob_ref[...]) * _silu(rg_ref[...])
    mix = (jnp.dot(a.astype(BF16), wout_ref[0:256, :], preferred_element_type=F32)
           + jnp.dot(b.astype(BF16), wout_ref[256:512, :], preferred_element_type=F32)
           + jnp.dot(r.astype(BF16), wout_ref[512:768, :], preferred_element_type=F32)
           + jnp.dot(dd.astype(BF16), wout_ref[768:1024, :], preferred_element_type=F32))
    m = mod_ref[0]
    x1 = jnp.where(is_p, xp_ref[...], xs_ref[...]) + m[2:3] * (_rms(mix) * npost_ref[...])
    x1_ref[...] = x1
    h2 = (_rms(x1) * npre_ref[...] * (1.0 + m[4:5]) + m[3:4]).astype(BF16)
    h2_ref[...] = h2
    logits_t = lax.dot_general(wrt_ref[...], h2, (((1,), (1,)), ((), ())), preferred_element_type=F32)
    w_t = _route(_sigmoid(logits_t), rb_ref[...])
    w_t = jnp.concatenate([w_t, jnp.zeros((LANES - N_EXPERTS, w_t.shape[1]), F32)], axis=0)
    wd_ref[...] = w_t.T


def _output_stage(oa, od, hf, hb, mo, of, ob, rg, xp, xs, mod, mg, npost, npre, wout, wrt, rb, dims):
    t_all = xp.shape[0] + xs.shape[0]
    npt, nst, tps = dims["tp"] // TM, xs.shape[0] // TM, dims["ns"] // TM
    row = lambda i: (i, 0)
    const2 = lambda i: (0, 0)
    tok = lambda w: pl.BlockSpec((TM, w), row)
    full = lambda a: pl.BlockSpec(a.shape, const2)
    hp = pl.BlockSpec((TM, 256), lambda i: (jnp.minimum(i, npt - 1), 0))
    hs = pl.BlockSpec((TM, 256), lambda i: (jnp.clip(i - npt, 0, nst - 1), 0))
    return pl.pallas_call(
        functools.partial(_out_kernel, npt=npt),
        grid=(t_all // TM,),
        in_specs=[hp, hs, hp, hs, tok(256), tok(256), tok(256), tok(256), tok(256), tok(256)]
        + _pair_specs(npt, nst, TM, D_MODEL)
        + [pl.BlockSpec((1, 6, D_MODEL), lambda i: (_mod_row(i, npt, tps), 0, 0)),
           full(mg), full(npost), full(npre), full(wout), full(wrt), full(rb)],
        out_specs=[tok(D_MODEL), tok(D_MODEL), tok(LANES)],
        out_shape=[jax.ShapeDtypeStruct((t_all, D_MODEL), F32),
                   jax.ShapeDtypeStruct((t_all, D_MODEL), BF16),
                   jax.ShapeDtypeStruct((t_all, LANES), F32)],
        compiler_params=_cparams(("parallel",)),
        name="output_stage",
    )(oa[0], oa[1], od[0], od[1], hf, hb, mo, of, ob, rg, xp, xs, mod, mg, npost, npre, wout, wrt, rb)


def _swiglu(h, wg, wu, wd):
    act = _silu(_bdot(h, wg)) * _bdot(h, wu)
    return _bdot(act, wd)


def _moe_kernel(h_ref, wd_ref, x1_ref, mod_ref, npost_ref, wg_ref, wu_ref, wdn_ref, sg_ref, su_ref, sd_ref,
                op_ref, os_ref, acc, *, npt):
    i = pl.program_id(0)
    e = pl.program_id(1)
    h = h_ref[...]

    @pl.when(e == 0)
    def _():
        acc[...] = _swiglu(h, sg_ref[...], su_ref[...], sd_ref[...])

    lane = lax.broadcasted_iota(jnp.int32, wd_ref.shape, 1)
    wcol = jnp.sum(jnp.where(lane == e, wd_ref[...], 0.0), axis=1, keepdims=True)
    acc[...] += _swiglu(h, wg_ref[0, 0], wu_ref[0, 0], wdn_ref[0, 0]) * wcol

    def finish(o_ref):
        m = mod_ref[0]
        o_ref[...] = x1_ref[...] + m[5:6] * (_rms(acc[...]) * npost_ref[...])

    last = e == pl.num_programs(1) - 1
    pl.when(jnp.logical_and(last, i < npt))(lambda: finish(op_ref))
    pl.when(jnp.logical_and(last, i >= npt))(lambda: finish(os_ref))


def _moe_stage(layer, h2, wd, x1, mod, npost, wg, wu, wdn, sg, su, sd, dims):
    t_all = h2.shape[0]
    tm = TM_MOE if (dims["tp"] % TM_MOE == 0 and dims["ns"] % TM_MOE == 0) else TM
    npt, tps = dims["tp"] // tm, dims["ns"] // tm
    nst = t_all // tm - npt
    row = lambda i, e: (i, 0)
    const2 = lambda i, e: (0, 0)
    exp3 = lambda i, e: (layer, e, 0, 0)
    full = lambda a: pl.BlockSpec(a.shape, const2)
    return pl.pallas_call(
        functools.partial(_moe_kernel, npt=npt),
        grid=(t_all // tm, N_EXPERTS),
        in_specs=[pl.BlockSpec((tm, D_MODEL), row), pl.BlockSpec((tm, LANES), row),
                  pl.BlockSpec((tm, D_MODEL), row),
                  pl.BlockSpec((1, 6, D_MODEL), lambda i, e: (_mod_row(i, npt, tps), 0, 0)),
                  full(npost),
                  pl.BlockSpec((1, 1, D_MODEL, D_EXPERT), exp3), pl.BlockSpec((1, 1, D_MODEL, D_EXPERT), exp3),
                  pl.BlockSpec((1, 1, D_EXPERT, D_MODEL), exp3),
                  full(sg), full(su), full(sd)],
        out_specs=_pair_specs(npt, nst, tm, D_MODEL),
        out_shape=[jax.ShapeDtypeStruct((npt * tm, D_MODEL), F32), jax.ShapeDtypeStruct((nst * tm, D_MODEL), F32)],
        scratch_shapes=[pltpu.VMEM((tm, D_MODEL), F32)],
        compiler_params=_cparams(("arbitrary", "arbitrary")),
        name="moe_experts",
    )(h2, wd, x1, mod, npost, wg, wu, wdn, sg, su, sd)


_PERM32 = np.concatenate([np.arange(8, 16), np.arange(0, 8), np.arange(24, 32), np.arange(16, 24)])


def _rope_tables(dims):
    n = dims["ns"]
    pos = jnp.arange(n)
    quarter = QK_ROPE // 4
    inv = 1.0 / (ROPE_THETA ** (jnp.arange(quarter, dtype=F32) / quarter))
    ang_r = (pos // GRID_W).astype(F32)[:, None] * inv[None, :]
    ang_c = (pos % GRID_W).astype(F32)[:, None] * inv[None, :]
    cos32 = jnp.concatenate([jnp.cos(ang_r)] * 2 + [jnp.cos(ang_c)] * 2, axis=-1)
    sin32 = jnp.concatenate([-jnp.sin(ang_r), jnp.sin(ang_r), -jnp.sin(ang_c), jnp.sin(ang_c)], axis=-1)
    cos256, sin256 = jnp.tile(cos32, (1, 8)), jnp.tile(sin32, (1, 8))
    one, zero = jnp.ones((n, QK_NOPE), F32), jnp.zeros((n, QK_NOPE), F32)
    cosq = jnp.concatenate([one, cos32] * N_HEADS, axis=-1)
    sinq = jnp.concatenate([zero, sin32] * N_HEADS, axis=-1)
    return cos256, sin256, cosq, sinq


def _split_w_in(w_in_l):
    sizes = (256, 256, 256, 256, 256, 256, 256, 16, 256, 256, 256, 256, Q_LORA, KV_LORA, QK_ROPE)
    idx = np.cumsum(sizes)[:-1]
    return jnp.split(w_in_l, [int(v) for v in idx], axis=-1)


def _layer_weights(w_in_l, w_uq_l):
    (a_q, a_k, a_v, m_q, m_k, m_v, m_o, m_g, r_q, r_k, r_v, r_g, d_cq, d_ckv, d_kr) = _split_w_in(w_in_l)
    perm256 = np.concatenate([_PERM32 + 32 * j for j in range(8)])
    pad = lambda w, n: jnp.pad(w, ((0, 0), (0, n - w.shape[1])))
    tail = pad(jnp.concatenate([d_kr, m_g], axis=-1), LANES)
    tail_s = pad(d_kr[:, _PERM32], LANES)
    w_all = jnp.concatenate([a_q, a_k, a_v, m_q, m_k, m_v, m_o, r_q, r_k, r_v, r_g, d_cq, d_ckv, tail,
                             a_q[:, perm256], a_k[:, perm256], tail_s], axis=-1).astype(BF16)
    per = QK_NOPE + QK_ROPE
    permq = np.concatenate([np.concatenate([np.arange(QK_NOPE), QK_NOPE + _PERM32]) + per * j
                            for j in range(N_HEADS)])
    return w_all, w_uq_l.astype(BF16), w_uq_l[:, permq].astype(BF16)


def kernel(x_prompt, x_sample, cache_diff_k, cache_diff_v, state_mlstm_C, state_mlstm_n, state_mlstm_m, state_ret_S, cache_mla_ckv, cache_mla_krope, c, c_ctx, w_mod, b_mod, norm_pre, norm_post, w_in, w_out, diff_lambda, diff_norm, mlstm_gate_bias, mlstm_norm, ret_decay_logit, mla_q_norm, mla_w_uq, mla_kv_norm, mla_w_ukv, moe_w_router, moe_router_bias, moe_w_gate, moe_w_up, moe_w_down, shared_w_gate, shared_w_up, shared_w_down):
    bp, n_p, _ = x_prompt.shape
    bs, n_s, _ = x_sample.shape
    depth = w_in.shape[0]
    past = cache_diff_k.shape[2]
    dims = dict(bp=bp, np=n_p, bs=bs, ns=n_s, tp=bp * n_p, past=past)
    tp, ts = bp * n_p, bs * n_s
    assert n_p % TM == 0 and n_s % TM == 0 and n_p % CHUNK == 0 and n_s % min(TK, n_s) == 0
    assert past % min(TK, past) == 0 and tp % n_s == 0 and bs + 1 <= 8 and n_s % GRID_W == 0

    xp, xs = x_prompt.reshape(tp, D_MODEL), x_sample.reshape(ts, D_MODEL)
    cond = jnp.zeros((8, D_MODEL), F32).at[0].set(c_ctx).at[1:1 + bs].set(c)
    mod_all = _modulation(cond, w_mod, b_mod).reshape(depth, 8, 6, D_MODEL)
    tabs = _rope_tables(dims)
    scan_tab = _scan_table(dims)

    outs = [[] for _ in range(8)]
    for l in range(depth):
        lam_init = 0.8 - 0.6 * math.exp(-0.3 * l)
        mod = mod_all[l]
        w_all, wuq, wuqs = _layer_weights(w_in[l], mla_w_uq[l])
        wukv = mla_w_ukv[l].astype(BF16)
        (aq1, aq2, ak1t, ak2t, avh, ak, av, mqkv, mo, rqkv, rg, tail, qmla, ckv, kmlat, vmla) = _input_stage(
            xp, xs, mod, norm_pre[l, 0:1], w_all, tabs, mla_q_norm[l][None], wuq, wuqs,
            mla_kv_norm[l][None], wukv, dims)

        lamv, dg = diff_lambda[l], diff_norm[l][None]
        ckt = jnp.transpose(cache_diff_k[:, l], (2, 3, 0, 1)).reshape(N_HEADS, HEAD_DIM, bs * past).astype(BF16)
        cv = jnp.transpose(cache_diff_v[:, l], (2, 0, 1, 3)).reshape(N_HEADS, bs * past, HEAD_DIM).astype(BF16)
        oa_p = _attention("diff_attn_ctx", [aq1, aq2], [ak1t, ak2t], avh, None, None, [lamv, dg],
                          0, bp, n_p, 0, lam_init)
        oa_s = _attention("diff_attn_latent", [aq1, aq2], [ak1t, ak2t], avh,
                          [ckt[:, :DIFF_HALF], ckt[:, DIFF_HALF:]], cv, [lamv, dg], tp, bs, n_s, past, lam_init)

        ckv_ctx = cache_mla_ckv[:, l].reshape(bs * past, KV_LORA)
        kvc = _project(ckv_ctx, wukv, BF16).reshape(bs * past, N_HEADS, QK_NOPE + V_HEAD)
        krc = jnp.broadcast_to(cache_mla_krope[:, l].reshape(bs * past, 1, QK_ROPE).astype(BF16),
                               (bs * past, N_HEADS, QK_ROPE))
        kct = jnp.transpose(jnp.concatenate([kvc[..., :QK_NOPE], krc], axis=-1), (1, 2, 0))
        vc = jnp.transpose(kvc[..., QK_NOPE:], (1, 0, 2))
        od_p = _attention("mla_attn_ctx", [qmla], [kmlat], vmla, None, None, [], 0, bp, n_p, 0)
        od_s = _attention("mla_attn_latent", [qmla], [kmlat], vmla, [kct], vc, [], tp, bs, n_s, past)

        zeros = lambda a: jnp.zeros((bp,) + a.shape[1:], F32)
        st_c, st_n, st_m, st_s = state_mlstm_C[:, l], state_mlstm_n[:, l], state_mlstm_m[:, l], state_ret_S[:, l]
        bias_row = jnp.zeros((1, LANES), F32).at[0, TAIL_GATE0:TAIL_GATE0 + 16].set(mlstm_gate_bias[l].reshape(16))
        hf, hb, c_n, n_n, m_n = _mlstm_scan(scan_tab, mqkv, tail, bias_row,
                                            jnp.concatenate([zeros(st_c), st_c]),
                                            jnp.concatenate([zeros(st_n), st_n]),
                                            jnp.concatenate([zeros(st_m), st_m]))
        of, ob, s_n = _ret_scan(scan_tab, rqkv, ret_decay_logit[l], jnp.concatenate([zeros(st_s), st_s]))

        x1, h2, wd = _output_stage((oa_p, oa_s), (od_p, od_s), hf, hb, mo, of, ob, rg, xp, xs, mod,
                                   mlstm_norm[l][None], norm_post[l, 0:1], norm_pre[l, 1:2],
                                   w_out[l].astype(BF16), moe_w_router[l].T.astype(BF16),
                                   moe_router_bias[l][:, None], dims)
        xp, xs = _moe_stage(l, h2, wd, x1, mod, norm_post[l, 1:2], moe_w_gate, moe_w_up, moe_w_down,
                            shared_w_gate[l], shared_w_up[l], shared_w_down[l], dims)

        outs[0].append(ak[:tp].reshape(bp, n_p, N_HEADS, HEAD_DIM))
        outs[1].append(av[:tp].reshape(bp, n_p, N_HEADS, HEAD_DIM))
        outs[2].append(c_n[:bp])
        outs[3].append(n_n[:bp])
        outs[4].append(m_n[:bp])
        outs[5].append(s_n[:bp])
        outs[6].append(ckv[:tp].reshape(bp, n_p, KV_LORA))
        outs[7].append(tail[:tp, :QK_ROPE].reshape(bp, n_p, QK_ROPE))

    return (xp.reshape(bp, n_p, D_MODEL), xs.reshape(bs, n_s, D_MODEL)) + tuple(
        jnp.stack(o, axis=1) for o in outs)
```

```python
import functools
import math

import numpy as np
import jax
import jax.numpy as jnp
from jax import lax
from jax.experimental import pallas as pl
from jax.experimental.pallas import tpu as pltpu
from jax.experimental.pallas import tpu_sc as plsc

F32 = jnp.float32
BF16 = jnp.bfloat16
HIGHEST = lax.Precision.HIGHEST

D_MODEL = 1024
GRID_W = 64
GROUP_WIDTH = 256
HEAD_DIM = 64
N_HEADS = 4
DIFF_HALF = 32
ROPE_THETA = 10000.0
Q_LORA = 256
KV_LORA = 128
QK_NOPE = 64
QK_ROPE = 32
V_HEAD = 64
N_EXPERTS = 64
TOP_K = 8
N_GROUPS = 8
TOPK_GROUPS = 4
D_EXPERT = 256
ROUTED_SCALE = 2.5
CHUNK = 128
EPS = 1e-6
Q_MLA = N_HEADS * (QK_NOPE + QK_ROPE)
KV_MLA = N_HEADS * (QK_NOPE + V_HEAD)

LANES = 128
VMEM_LIMIT = 56 * 1024 * 1024

TM = 256
TM_MOE = 1024
TQ = 256
TK = 2048

C_AQ, C_AK, C_AV = 0, 256, 512
C_MQKV, C_MO = 768, 1536
C_RQKV, C_RG = 1792, 2560
C_CQ, C_CKV, C_TAIL = 2816, 3072, 3200
C_AQS, C_AKS, C_TAILS = 3328, 3584, 3840
W_ALL = 3968
TAIL_GATE0 = QK_ROPE


def _cparams(sem, flags=None):
    return pltpu.CompilerParams(dimension_semantics=sem, vmem_limit_bytes=VMEM_LIMIT, flags=flags)


def _rms(x):
    return x * lax.rsqrt(jnp.mean(x * x, axis=-1, keepdims=True) + EPS)


def _head_mean_matrix(width):
    r = lax.broadcasted_iota(jnp.int32, (width, width), 0) // HEAD_DIM
    c = lax.broadcasted_iota(jnp.int32, (width, width), 1) // HEAD_DIM
    return jnp.where(r == c, 1.0 / HEAD_DIM, 0.0).astype(F32)


def _rms_heads(x):
    ms = jnp.dot(x * x, _head_mean_matrix(x.shape[-1]), precision=HIGHEST, preferred_element_type=F32)
    return x * lax.rsqrt(ms + EPS)


def _sigmoid(x):
    return 1.0 / (1.0 + jnp.exp(-x))


def _silu(x):
    return x * _sigmoid(x)


def _log_sigmoid(x):
    return jnp.minimum(x, 0.0) - jnp.log1p(jnp.exp(-jnp.abs(x)))


def _bdot(a, b):
    return jnp.dot(a.astype(BF16), b.astype(BF16), preferred_element_type=F32)


def _mod_kernel(c_ref, w_ref, b_ref, o_ref):
    o_ref[0] = _bdot(_silu(c_ref[...]), w_ref[0]) + b_ref[0]


def _modulation(cond, w_mod, b_mod):
    depth, _, n = w_mod.shape
    tn = 1536
    return pl.pallas_call(
        _mod_kernel,
        grid=(depth, n // tn),
        in_specs=[pl.BlockSpec((8, D_MODEL), lambda l, j: (0, 0)),
                  pl.BlockSpec((1, D_MODEL, tn), lambda l, j: (l, 0, j)),
                  pl.BlockSpec((1, 1, tn), lambda l, j: (l, 0, j))],
        out_specs=pl.BlockSpec((1, 8, tn), lambda l, j: (l, 0, j)),
        out_shape=jax.ShapeDtypeStruct((depth, 8, n), F32),
        compiler_params=_cparams(("parallel", "parallel")),
        name="adaln_mod",
    )(cond, w_mod, b_mod.reshape(depth, 1, n))


def _in_kernel(xp_ref, xs_ref, mod_ref, npre_ref, w_ref, cos_ref, sin_ref, cosq_ref, sinq_ref,
               qg_ref, wuq_ref, wuqs_ref, kvg_ref, wukv_ref,
               aq1_ref, aq2_ref, ak1t_ref, ak2t_ref, avh_ref, ak_ref, av_ref,
               mqkv_ref, mo_ref, rqkv_ref, rg_ref, tail_ref,
               qmla_ref, ckv_ref, kmlat_ref, vmla_ref, *, npt):
    is_ctx = pl.program_id(0) < npt
    x = jnp.where(is_ctx, xp_ref[...], xs_ref[...])
    m = mod_ref[0]
    h = (_rms(x) * npre_ref[...] * (1.0 + m[1:2]) + m[0:1]).astype(BF16)

    def proj(c0, width):
        return jnp.dot(h, w_ref[:, c0:c0 + width], preferred_element_type=F32)

    cos = jnp.where(is_ctx, 1.0, cos_ref[...])
    sin = jnp.where(is_ctx, 0.0, sin_ref[...])
    aq = (proj(C_AQ, 256) * cos + proj(C_AQS, 256) * sin) * (DIFF_HALF ** -0.5)
    ak = proj(C_AK, 256) * cos + proj(C_AKS, 256) * sin
    av = proj(C_AV, 256)
    ak_ref[...] = ak
    av_ref[...] = av
    ak_t = ak.T.astype(BF16)
    for hd in range(N_HEADS):
        lo = hd * HEAD_DIM
        aq1_ref[hd] = aq[:, lo:lo + DIFF_HALF].astype(BF16)
        aq2_ref[hd] = aq[:, lo + DIFF_HALF:lo + HEAD_DIM].astype(BF16)
        ak1t_ref[hd] = ak_t[lo:lo + DIFF_HALF, :]
        ak2t_ref[hd] = ak_t[lo + DIFF_HALF:lo + HEAD_DIM, :]
        avh_ref[hd] = av[:, lo:lo + HEAD_DIM].astype(BF16)

    mqkv_ref[...] = proj(C_MQKV, 768).astype(BF16)
    mo_ref[...] = proj(C_MO, 256)
    rqkv_ref[...] = proj(C_RQKV, 768).astype(BF16)
    rg_ref[...] = proj(C_RG, 256)

    tail = proj(C_TAIL, LANES)
    tail_ref[...] = tail
    kr_t = (tail * cos[:, :LANES] + proj(C_TAILS, LANES) * sin[:, :LANES]).T[:QK_ROPE, :].astype(BF16)

    cqn = (_rms(proj(C_CQ, Q_LORA)) * qg_ref[...]).astype(BF16)
    q = jnp.dot(cqn, wuq_ref[...], preferred_element_type=F32)
    qs = jnp.dot(cqn, wuqs_ref[...], preferred_element_type=F32)
    cosq = jnp.where(is_ctx, 1.0, cosq_ref[...])
    sinq = jnp.where(is_ctx, 0.0, sinq_ref[...])
    qmla = (q * cosq + qs * sinq) * ((QK_NOPE + QK_ROPE) ** -0.5)
    for hd in range(N_HEADS):
        lo = hd * (QK_NOPE + QK_ROPE)
        qmla_ref[hd] = qmla[:, lo:lo + QK_NOPE + QK_ROPE].astype(BF16)
    ckvn = _rms(proj(C_CKV, KV_LORA)) * kvg_ref[...]
    ckv_ref[...] = ckvn
    kv = jnp.dot(ckvn.astype(BF16), wukv_ref[...], preferred_element_type=F32)
    kv_t = kv.T.astype(BF16)
    per = QK_NOPE + V_HEAD
    for hd in range(N_HEADS):
        kmlat_ref[hd, 0:QK_NOPE, :] = kv_t[hd * per:hd * per + QK_NOPE, :]
        kmlat_ref[hd, QK_NOPE:QK_NOPE + QK_ROPE, :] = kr_t
        vmla_ref[hd] = kv[:, hd * per + QK_NOPE:(hd + 1) * per].astype(BF16)


def _mod_row(i, npt, tps):
    return jnp.where(i < npt, 0, 1 + (i - npt) // tps)


def _pair_specs(npt, nst, tm, width):
    return [pl.BlockSpec((tm, width), lambda i, *_: (jnp.minimum(i, npt - 1), 0)),
            pl.BlockSpec((tm, width), lambda i, *_: (jnp.clip(i - npt, 0, nst - 1), 0))]


def _input_stage(xp, xs, mod, npre, w_all, tabs, qg, wuq, wuqs, kvg, wukv, dims):
    t_all = xp.shape[0] + xs.shape[0]
    npt, nst, tps = dims["tp"] // TM, xs.shape[0] // TM, dims["ns"] // TM
    row = lambda i: (i, 0)
    tab = lambda w: pl.BlockSpec((TM, w), lambda i: (jnp.maximum(i - npt, 0) % tps, 0))
    hrow = lambda i: (0, i, 0)
    const2 = lambda i: (0, 0)
    tok = lambda w: pl.BlockSpec((TM, w), row)
    headed = lambda w: pl.BlockSpec((N_HEADS, TM, w), hrow)
    headed_t = lambda d: pl.BlockSpec((N_HEADS, d, TM), lambda i: (0, 0, i))
    full = lambda a: pl.BlockSpec(a.shape, const2)
    cos, sin, cosq, sinq = tabs
    out_shapes = [
        (headed(DIFF_HALF), (N_HEADS, t_all, DIFF_HALF), BF16),
        (headed(DIFF_HALF), (N_HEADS, t_all, DIFF_HALF), BF16),
        (headed_t(DIFF_HALF), (N_HEADS, DIFF_HALF, t_all), BF16),
        (headed_t(DIFF_HALF), (N_HEADS, DIFF_HALF, t_all), BF16),
        (headed(HEAD_DIM), (N_HEADS, t_all, HEAD_DIM), BF16),
        (tok(256), (t_all, 256), F32),
        (tok(256), (t_all, 256), F32),
        (tok(768), (t_all, 768), BF16),
        (tok(256), (t_all, 256), F32),
        (tok(768), (t_all, 768), BF16),
        (tok(256), (t_all, 256), F32),
        (tok(LANES), (t_all, LANES), F32),
        (headed(QK_NOPE + QK_ROPE), (N_HEADS, t_all, QK_NOPE + QK_ROPE), BF16),
        (tok(KV_LORA), (t_all, KV_LORA), F32),
        (headed_t(QK_NOPE + QK_ROPE), (N_HEADS, QK_NOPE + QK_ROPE, t_all), BF16),
        (headed(V_HEAD), (N_HEADS, t_all, V_HEAD), BF16),
    ]
    return pl.pallas_call(
        functools.partial(_in_kernel, npt=npt),
        grid=(t_all // TM,),
        in_specs=_pair_specs(npt, nst, TM, D_MODEL) + [
            pl.BlockSpec((1, 6, D_MODEL), lambda i: (_mod_row(i, npt, tps), 0, 0)),
            full(npre), full(w_all), tab(256), tab(256), tab(Q_MLA), tab(Q_MLA),
            full(qg), full(wuq), full(wuqs), full(kvg), full(wukv)],
        out_specs=[s for s, _, _ in out_shapes],
        out_shape=[jax.ShapeDtypeStruct(shp, dt) for _, shp, dt in out_shapes],
        compiler_params=_cparams(("parallel",)),
        name="input_stage",
    )(xp, xs, mod, npre, w_all, cos, sin, cosq, sinq, qg, wuq, wuqs, kvg, wukv)


def _attn_kernel(*refs, n_soft, n_new, n_ctx, lam_init):
    refs = list(refs)
    q_refs, kt_refs, v_ref = refs[:n_soft], refs[n_soft:2 * n_soft], refs[2 * n_soft]
    pos = 2 * n_soft + 1
    if n_ctx:
        ckt_refs, cv_ref = refs[pos:pos + n_soft], refs[pos + n_soft]
        pos += n_soft + 1
    if n_soft == 2:
        lam_ref, g_ref = refs[pos:pos + 2]
        pos += 2
    o_ref = refs[pos]
    tq = o_ref.shape[0]
    nchain = n_soft * N_HEADS

    def chunk(state, kt_of, v_of):
        ms, ls, accs = state
        new_m, new_l, new_acc = list(ms), list(ls), list(accs)
        order = [(j, hd) for hd in range(N_HEADS) for j in range(n_soft)]
        ss = [jnp.dot(q_refs[j][hd], kt_of(j, hd), preferred_element_type=F32) for j, hd in order]
        ps, alphas = [], []
        for (j, hd), s in zip(order, ss):
            c = j * N_HEADS + hd
            m_new = jnp.maximum(ms[c], jnp.broadcast_to(jnp.max(s, axis=-1, keepdims=True), (tq, LANES)))
            p = jnp.exp(s - jnp.tile(m_new, (1, s.shape[1] // LANES)))
            alpha = jnp.exp(ms[c] - m_new)
            new_l[c] = alpha * ls[c] + jnp.broadcast_to(jnp.sum(p, axis=-1, keepdims=True), (tq, LANES))
            new_m[c] = m_new
            ps.append(p.astype(BF16))
            alphas.append(alpha)
        for (j, hd), p, alpha in zip(order, ps, alphas):
            c = j * N_HEADS + hd
            new_acc[c] = alpha[:, :V_HEAD] * accs[c] + jnp.dot(p, v_of(hd), preferred_element_type=F32)
        return tuple(new_m), tuple(new_l), tuple(new_acc)

    state = (tuple(jnp.full((tq, LANES), -jnp.inf, F32) for _ in range(nchain)),
             tuple(jnp.zeros((tq, LANES), F32) for _ in range(nchain)),
             tuple(jnp.zeros((tq, V_HEAD), F32) for _ in range(nchain)))
    if n_ctx:
        cstep = min(TK, n_ctx)
        for i in range(n_ctx // cstep):
            state = chunk(state, lambda j, hd, i=i: ckt_refs[j][hd, :, i * cstep:(i + 1) * cstep],
                          lambda hd, i=i: cv_ref[hd, i * cstep:(i + 1) * cstep, :])
    step = min(TK, n_new)

    def body(i, st):
        start = pl.multiple_of(i * step, step)
        return chunk(st, lambda j, hd: kt_refs[j][hd, :, pl.ds(start, step)],
                     lambda hd: v_ref[hd, pl.ds(start, step), :])

    _, ls, accs = lax.fori_loop(0, n_new // step, body, state)

    if n_soft == 2:
        lv = lam_ref[...]
        lam = (jnp.exp(jnp.sum(lv[0:1] * lv[1:2], axis=-1, keepdims=True))
               - jnp.exp(jnp.sum(lv[2:3] * lv[3:4], axis=-1, keepdims=True)) + lam_init)
    for hd in range(N_HEADS):
        out = accs[hd] / ls[hd][:, :V_HEAD]
        if n_soft == 2:
            a = out - lam * (accs[N_HEADS + hd] / ls[N_HEADS + hd][:, :V_HEAD])
            out = _rms(a) * g_ref[...] * (1.0 - lam_init)
        o_ref[:, hd * V_HEAD:(hd + 1) * V_HEAD] = out


def _attention(name, qs, kts, v, ctx_kts, ctx_v, extras, row0, nb, n, n_ctx, lam_init=0.0):
    tq = min(TQ, n)
    nqt = n // tq
    n_soft = len(qs)
    qmap = lambda b, i: (0, row0 // tq + b * nqt + i, 0)
    in_specs = [pl.BlockSpec((N_HEADS, tq, a.shape[-1]), qmap) for a in qs]
    in_specs += [pl.BlockSpec((N_HEADS, a.shape[1], n), lambda b, i: (0, 0, row0 // n + b)) for a in kts]
    in_specs += [pl.BlockSpec((N_HEADS, n, V_HEAD), lambda b, i: (0, row0 // n + b, 0))]
    args = list(qs) + list(kts) + [v]
    if n_ctx:
        in_specs += [pl.BlockSpec((N_HEADS, a.shape[1], n_ctx), lambda b, i: (0, 0, b)) for a in ctx_kts]
        in_specs += [pl.BlockSpec((N_HEADS, n_ctx, V_HEAD), lambda b, i: (0, b, 0))]
        args += list(ctx_kts) + [ctx_v]
    in_specs += [pl.BlockSpec(a.shape, lambda b, i: (0, 0)) for a in extras]
    args += list(extras)
    return pl.pallas_call(
        functools.partial(_attn_kernel, n_soft=n_soft, n_new=n, n_ctx=n_ctx, lam_init=lam_init),
        grid=(nb, nqt),
        in_specs=in_specs,
        out_specs=pl.BlockSpec((tq, N_HEADS * V_HEAD), lambda b, i: (b * nqt + i, 0)),
        out_shape=jax.ShapeDtypeStruct((nb * n, N_HEADS * V_HEAD), F32),
        compiler_params=_cparams(("parallel", "parallel")),
        name=name,
    )(*args)


def _tri(lower):
    r = lax.broadcasted_iota(jnp.int32, (CHUNK, CHUNK), 0)
    c = lax.broadcasted_iota(jnp.int32, (CHUNK, CHUNK), 1)
    return (c <= r) if lower else (c >= r)


def _nt(a, b):
    return lax.dot_general(a, b, (((1,), (1,)), ((), ())), preferred_element_type=F32)


def _tn(a, b):
    return lax.dot_general(a, b, (((0,), (0,)), ((), ())), preferred_element_type=F32)


def _head_slices(qkv, hd):
    lo = hd * HEAD_DIM
    return qkv[:, lo:lo + HEAD_DIM], qkv[:, 256 + lo:256 + lo + HEAD_DIM], qkv[:, 512 + lo:512 + lo + HEAD_DIM]


def _row_dot(row, mat, transpose_mat):
    row8 = jnp.broadcast_to(row, (8, row.shape[1])).astype(BF16)
    dims = (((1,), (1,)), ((), ())) if transpose_mat else (((1,), (0,)), ((), ()))
    return lax.dot_general(row8, mat, dims, preferred_element_type=F32)[0:1]


def _load_state_t(dst, src):
    for d in range(2):
        for hd in range(N_HEADS):
            dst[d, hd] = src[d, hd].T


def _mlstm_step(dirs, bias_ref, ct_s, n_s, m_s):
    k_scale = HEAD_DIM ** -0.5
    lower = _tri(True).astype(F32)
    upper = _tri(False).astype(F32)
    chains = []
    for d, qkv_ref, tail_ref, _ in dirs:
        g = tail_ref[...] + bias_ref[...]
        ls = _log_sigmoid(g)
        g_t, ls_t = g.T, ls.T
        left, right = (lower, upper) if d == 0 else (upper, lower)
        cum_col = jnp.dot(left, ls, precision=HIGHEST, preferred_element_type=F32)
        cum_row = jnp.dot(ls_t, right, precision=HIGHEST, preferred_element_type=F32)
        qkv = qkv_ref[...]
        for hd in range(N_HEADS):
            ci = TAIL_GATE0 + 4 * (2 * d) + hd
            cf = ci + 4
            q, k, v = _head_slices(qkv, hd)
            chains.append(dict(
                d=d, hd=hd, q=q, k=k, v=v, li_row=g_t[ci:ci + 1, :], b_row=cum_row[cf:cf + 1, :],
                c_col=g[:, ci:ci + 1] - cum_col[:, cf:cf + 1],
                m_prev=m_s[d:d + 1, hd:hd + 1], ct=ct_s[d, hd], n=n_s[d, hd:hd + 1, :]))
    for c in chains:
        c["s"] = _nt(c["k"], c["q"])
        c["v_t"] = c["v"].T
    for c in chains:
        valid = _tri(c["d"] != 0)
        log_d = jnp.where(valid, c["b_row"] + c["c_col"], -jnp.inf)
        log_inter = c["b_row"] + c["m_prev"]
        c["m_t"] = jnp.maximum(log_inter, jnp.max(log_d, axis=0, keepdims=True))
        c["w"] = c["s"] * k_scale * jnp.exp(log_d - c["m_t"])
        c["w_inter"] = jnp.exp(log_inter - c["m_t"])
    for c in chains:
        num = (jnp.dot(c["v_t"], c["w"].astype(BF16), preferred_element_type=F32)
               + c["w_inter"] * _nt(c["ct"].astype(BF16), c["q"]))
        den = jnp.sum(c["w"], axis=0, keepdims=True) + c["w_inter"] * _row_dot(c["n"], c["q"], True)
        c["h"] = num / jnp.maximum(jnp.abs(den), jnp.exp(-c["m_t"]))
    for c in chains:
        last = CHUNK - 1 if c["d"] == 0 else 0
        c["m_new"] = c["m_t"][:, last:last + 1]
        b_last = c["b_row"][:, last:last + 1]
        w_end_row = jnp.exp(b_last - c["b_row"] + c["li_row"] - c["m_new"])
        w_end_col = jnp.exp(c["c_col"] + (b_last - c["m_new"]))
        decay = jnp.exp(b_last + c["m_prev"] - c["m_new"])
        kw = (c["k"].astype(F32) * (w_end_col * k_scale)).astype(BF16)
        c["ct_new"] = decay * c["ct"] + jnp.dot(c["v_t"], kw, preferred_element_type=F32)
        c["n_new"] = decay * c["n"] + _row_dot(w_end_row, c["k"], False) * k_scale
    for j, (_, _, _, h_ref) in enumerate(dirs):
        h_ref[...] = jnp.concatenate([c["h"] for c in chains[j * N_HEADS:(j + 1) * N_HEADS]], axis=0).T
    for c in chains:
        d, hd = c["d"], c["hd"]
        ct_s[d, hd] = c["ct_new"]
        n_s[d, hd:hd + 1, :] = c["n_new"]
        m_s[d:d + 1, hd:hd + 1] = c["m_new"]


def _mlstm_kernel(tab_ref, qkvf_ref, tailf_ref, qkvb_ref, tailb_ref, bias_ref, c0_ref, n0_ref, m0_ref,
                  hf_ref, hb_ref, cn_ref, nn_ref, mn_ref, ct_s, n_s, m_s):
    i = pl.program_id(0)

    @pl.when(tab_ref[3, i] == 1)
    def _():
        _load_state_t(ct_s, c0_ref.at[0])
        n_s[...] = n0_ref[0]
        m_s[...] = m0_ref[0]

    _mlstm_step([(0, qkvf_ref, tailf_ref, hf_ref), (1, qkvb_ref, tailb_ref, hb_ref)], bias_ref, ct_s, n_s, m_s)

    @pl.when(tab_ref[4, i] == 1)
    def _():
        _load_state_t(cn_ref.at[0], ct_s)
        nn_ref[0] = n_s[...]
        mn_ref[0] = m_s[...]


def _ret_step(dirs, lg_ref, st_s):
    k_scale = HEAD_DIM ** -0.5
    s = lax.broadcasted_iota(jnp.int32, (CHUNK, CHUNK), 0)
    t = lax.broadcasted_iota(jnp.int32, (CHUNK, CHUNK), 1)
    lane = lax.broadcasted_iota(jnp.int32, (1, CHUNK), 1)
    chains = []
    for d, qkv_ref, _ in dirs:
        lag = (t - s) if d == 0 else (s - t)
        pos = (lane if d == 0 else CHUNK - 1 - lane).astype(F32)
        qkv = qkv_ref[...]
        for hd in range(N_HEADS):
            q, k, v = _head_slices(qkv, hd)
            chains.append(dict(d=d, hd=hd, q=q, k=k, v=v, lag=lag, pos=pos, st=st_s[d, hd],
                               lg=_log_sigmoid(lg_ref[d:d + 1, hd:hd + 1])))
    for c in chains:
        c["s"] = _nt(c["k"], c["q"])
        c["v_t"] = c["v"].T
    for c in chains:
        intra = jnp.where(c["lag"] >= 0, jnp.exp(jnp.maximum(c["lag"], 0).astype(F32) * c["lg"]), 0.0)
        c["a"] = (c["s"] * k_scale * intra).astype(BF16)
    for c in chains:
        inter = jnp.exp((c["pos"] + 1.0) * c["lg"])
        c["o"] = jnp.dot(c["v_t"], c["a"], preferred_element_type=F32) + inter * _nt(c["st"].astype(BF16), c["q"])
    row = lax.broadcasted_iota(jnp.int32, (CHUNK, HEAD_DIM), 0)
    for c in chains:
        src_pos = (row if c["d"] == 0 else CHUNK - 1 - row).astype(F32)
        tail = jnp.exp((CHUNK - 1.0 - src_pos) * c["lg"]) * k_scale
        kw = (c["k"].astype(F32) * tail).astype(BF16)
        c["st_new"] = jnp.exp(CHUNK * c["lg"]) * c["st"] + jnp.dot(c["v_t"], kw, preferred_element_type=F32)
    for j, (_, _, o_ref) in enumerate(dirs):
        o_ref[...] = jnp.concatenate([c["o"] for c in chains[j * N_HEADS:(j + 1) * N_HEADS]], axis=0).T
    for c in chains:
        st_s[c["d"], c["hd"]] = c["st_new"]


def _ret_kernel(tab_ref, qkvf_ref, qkvb_ref, lg_ref, s0_ref, of_ref, ob_ref, sn_ref, st_s):
    i = pl.program_id(0)

    @pl.when(tab_ref[3, i] == 1)
    def _():
        _load_state_t(st_s, s0_ref.at[0])

    _ret_step([(0, qkvf_ref, of_ref), (1, qkvb_ref, ob_ref)], lg_ref, st_s)

    @pl.when(tab_ref[4, i] == 1)
    def _():
        _load_state_t(sn_ref.at[0], st_s)


def _scan_table(dims):
    rows = []
    seq = 0
    for base, nb, n in ((0, dims["bp"], dims["np"]), (dims["tp"], dims["bs"], dims["ns"])):
        nch = n // CHUNK
        for b in range(nb):
            blk0 = (base + b * n) // CHUNK
            for c in range(nch):
                rows.append((blk0 + c, blk0 + nch - 1 - c, seq, int(c == 0), int(c == nch - 1)))
            seq += 1
    return jnp.asarray(np.array(rows, dtype=np.int32).T)


def _mlstm_scan(tab, mqkv, tail, bias_row, c0, n0, m0):
    t_all = mqkv.shape[0]
    nseq = c0.shape[0]
    fwd = lambda i, tab: (tab[0, i], 0)
    bwd = lambda i, tab: (tab[1, i], 0)
    st = lambda nd: (lambda i, tab: (tab[2, i],) + (0,) * nd)
    grid_spec = pltpu.PrefetchScalarGridSpec(
        num_scalar_prefetch=1,
        grid=(tab.shape[1],),
        in_specs=[pl.BlockSpec((CHUNK, 768), fwd), pl.BlockSpec((CHUNK, LANES), fwd),
                  pl.BlockSpec((CHUNK, 768), bwd), pl.BlockSpec((CHUNK, LANES), bwd),
                  pl.BlockSpec((1, LANES), lambda i, tab: (0, 0)),
                  pl.BlockSpec((1,) + c0.shape[1:], st(4)),
                  pl.BlockSpec((1,) + n0.shape[1:], st(3)),
                  pl.BlockSpec((1,) + m0.shape[1:], st(2))],
        out_specs=[pl.BlockSpec((CHUNK, 256), fwd), pl.BlockSpec((CHUNK, 256), bwd),
                   pl.BlockSpec((1,) + c0.shape[1:], st(4)),
                   pl.BlockSpec((1,) + n0.shape[1:], st(3)),
                   pl.BlockSpec((1,) + m0.shape[1:], st(2))],
        scratch_shapes=[pltpu.VMEM(c0.shape[1:], F32), pltpu.VMEM(n0.shape[1:], F32),
                        pltpu.VMEM(m0.shape[1:], F32)],
    )
    return pl.pallas_call(
        _mlstm_kernel,
        grid_spec=grid_spec,
        out_shape=[jax.ShapeDtypeStruct((t_all, 256), F32), jax.ShapeDtypeStruct((t_all, 256), F32),
                   jax.ShapeDtypeStruct(c0.shape, F32), jax.ShapeDtypeStruct(n0.shape, F32),
                   jax.ShapeDtypeStruct(m0.shape, F32)],
        compiler_params=_cparams(("arbitrary",)),
        name="mlstm_scan",
    )(tab, mqkv, tail, mqkv, tail, bias_row, c0, n0, m0)


def _ret_scan(tab, rqkv, decay_logit, s0):
    t_all = rqkv.shape[0]
    fwd = lambda i, tab: (tab[0, i], 0)
    bwd = lambda i, tab: (tab[1, i], 0)
    st = lambda i, tab: (tab[2, i], 0, 0, 0, 0)
    grid_spec = pltpu.PrefetchScalarGridSpec(
        num_scalar_prefetch=1,
        grid=(tab.shape[1],),
        in_specs=[pl.BlockSpec((CHUNK, 768), fwd), pl.BlockSpec((CHUNK, 768), bwd),
                  pl.BlockSpec(decay_logit.shape, lambda i, tab: (0, 0)),
                  pl.BlockSpec((1,) + s0.shape[1:], st)],
        out_specs=[pl.BlockSpec((CHUNK, 256), fwd), pl.BlockSpec((CHUNK, 256), bwd),
                   pl.BlockSpec((1,) + s0.shape[1:], st)],
        scratch_shapes=[pltpu.VMEM(s0.shape[1:], F32)],
    )
    return pl.pallas_call(
        _ret_kernel,
        grid_spec=grid_spec,
        out_shape=[jax.ShapeDtypeStruct((t_all, 256), F32), jax.ShapeDtypeStruct((t_all, 256), F32),
                   jax.ShapeDtypeStruct(s0.shape, F32)],
        compiler_params=_cparams(("arbitrary",)),
        name="retention_scan",
    )(tab, rqkv, rqkv, decay_logit, s0)


def _proj_kernel(x_ref, w_ref, o_ref):
    o_ref[...] = _bdot(x_ref[...], w_ref[...]).astype(o_ref.dtype)


def _project(x, w, dtype):
    return pl.pallas_call(
        _proj_kernel,
        out_shape=jax.ShapeDtypeStruct((x.shape[0], w.shape[1]), dtype),
        name="ctx_kv_proj",
    )(x, w)


def _first_argmax_mask(cur, axis, size):
    io = lax.broadcasted_iota(jnp.int32, cur.shape, axis)
    mx = jnp.max(cur, axis=axis, keepdims=True)
    ix = jnp.min(jnp.where(cur == mx, io, size), axis=axis, keepdims=True)
    return io == ix


def _route(scores_t, bias_col):
    tm = scores_t.shape[1]
    per = N_EXPERTS // N_GROUPS
    sel = scores_t + bias_col
    s3 = sel.reshape(N_GROUPS, per, tm)
    hit1 = _first_argmax_mask(s3, 1, per)
    m1 = jnp.max(s3, axis=1, keepdims=True)
    m2 = jnp.max(jnp.where(hit1, -jnp.inf, s3), axis=1, keepdims=True)
    cur = m1 + m2
    gsel = None
    for _ in range(TOPK_GROUPS):
        hit = _first_argmax_mask(cur, 0, N_GROUPS)
        gsel = hit if gsel is None else jnp.logical_or(gsel, hit)
        cur = jnp.where(hit, -jnp.inf, cur)
    cur = jnp.where(gsel, s3, -jnp.inf).reshape(N_EXPERTS, tm)
    chosen, hits = None, []
    for _ in range(TOP_K):
        hit = _first_argmax_mask(cur, 0, N_EXPERTS)
        hits.append(hit)
        chosen = hit if chosen is None else jnp.logical_or(chosen, hit)
        cur = jnp.where(hit, -jnp.inf, cur)
    w = jnp.where(chosen, scores_t, 0.0)
    return w / jnp.sum(w, axis=0, keepdims=True) * ROUTED_SCALE, chosen, hits


def _out_kernel(oap_ref, oas_ref, odp_ref, ods_ref, hf_ref, hb_ref, mo_ref, of_ref, ob_ref, rg_ref,
                xp_ref, xs_ref, mod_ref, mg_ref, npost_ref, npre_ref, wout_ref, wrt_ref, rb_ref,
                x1_ref, h2_ref, hpk_ref, eidx_ref, rank_ref, wk_ref, cnt_ref, count_s, *, npt):
    i = pl.program_id(0)
    is_p = i < npt

    @pl.when(i == 0)
    def _():
        count_s[...] = jnp.zeros(count_s.shape, F32)

    a = jnp.where(is_p, oap_ref[...], oas_ref[...])
    dd = jnp.where(is_p, odp_ref[...], ods_ref[...])
    b = _rms_heads(hf_ref[...] + hb_ref[...]) * mg_ref[...] * _sigmoid(mo_ref[...])
    r = _rms_heads(of_ref[...] + ob_ref[...]) * _silu(rg_ref[...])
    mix = (jnp.dot(a.astype(BF16), wout_ref[0:256, :], preferred_element_type=F32)
           + jnp.dot(b.astype(BF16), wout_ref[256:512, :], preferred_element_type=F32)
           + jnp.dot(r.astype(BF16), wout_ref[512:768, :], preferred_element_type=F32)
           + jnp.dot(dd.astype(BF16), wout_ref[768:1024, :], preferred_element_type=F32))
    m = mod_ref[0]
    x1 = jnp.where(is_p, xp_ref[...], xs_ref[...]) + m[2:3] * (_rms(mix) * npost_ref[...])
    x1_ref[...] = x1
    h2 = (_rms(x1) * npre_ref[...] * (1.0 + m[4:5]) + m[3:4]).astype(BF16)
    h2_ref[...] = h2
    bits = pltpu.bitcast(h2.astype(F32), jnp.uint32)
    half = D_MODEL // 2
    hpk_ref[...] = (bits[:, :half] >> 16) | (bits[:, half:] & jnp.uint32(0xFFFF0000))

    logits_t = lax.dot_general(wrt_ref[...], h2, (((1,), (1,)), ((), ())), preferred_element_type=F32)
    w_t, chosen, hits = _route(_sigmoid(logits_t), rb_ref[...])
    tm = w_t.shape[1]
    src = lax.broadcasted_iota(jnp.int32, (tm, tm), 0)
    dst = lax.broadcasted_iota(jnp.int32, (tm, tm), 1)
    before = jnp.where(src < dst, 1.0, 0.0).astype(BF16)
    picked = jnp.where(chosen, 1.0, 0.0)
    rank = jnp.dot(picked.astype(BF16), before, preferred_element_type=F32) + count_s[...]
    count_s[...] += jnp.sum(picked, axis=1, keepdims=True)
    cnt_ref[...] = jnp.broadcast_to(count_s[...], cnt_ref.shape)
    e_io = lax.broadcasted_iota(jnp.int32, w_t.shape, 0)
    pick = lambda hit, val: jnp.sum(jnp.where(hit, val, jnp.zeros_like(val)), axis=0, keepdims=True)
    eidx_ref[...] = jnp.concatenate([pick(h, e_io) for h in hits], axis=0)
    rank_ref[...] = jnp.concatenate([pick(h, rank) for h in hits], axis=0).astype(jnp.int32)
    wk = jnp.concatenate([pick(h, w_t) for h in hits] + [jnp.zeros((LANES - TOP_K, tm), F32)], axis=0)
    wk_ref[...] = wk.T


def _output_stage(oa, od, hf, hb, mo, of, ob, rg, xp, xs, mod, mg, npost, npre, wout, wrt, rb, dims):
    t_all = xp.shape[0] + xs.shape[0]
    npt, nst, tps = dims["tp"] // TM, xs.shape[0] // TM, dims["ns"] // TM
    row = lambda i: (i, 0)
    const2 = lambda i: (0, 0)
    tok = lambda w: pl.BlockSpec((TM, w), row)
    full = lambda a: pl.BlockSpec(a.shape, const2)
    hp = pl.BlockSpec((TM, 256), lambda i: (jnp.minimum(i, npt - 1), 0))
    hs = pl.BlockSpec((TM, 256), lambda i: (jnp.clip(i - npt, 0, nst - 1), 0))
    return pl.pallas_call(
        functools.partial(_out_kernel, npt=npt),
        grid=(t_all // TM,),
        in_specs=[hp, hs, hp, hs, tok(256), tok(256), tok(256), tok(256), tok(256), tok(256)]
        + _pair_specs(npt, nst, TM, D_MODEL)
        + [pl.BlockSpec((1, 6, D_MODEL), lambda i: (_mod_row(i, npt, tps), 0, 0)),
           full(mg), full(npost), full(npre), full(wout), full(wrt), full(rb)],
        out_specs=[tok(D_MODEL), tok(D_MODEL), tok(D_MODEL // 2),
                   pl.BlockSpec((TOP_K, TM), lambda i: (0, i)), pl.BlockSpec((TOP_K, TM), lambda i: (0, i)),
                   tok(LANES), pl.BlockSpec((N_EXPERTS, LANES), const2)],
        out_shape=[jax.ShapeDtypeStruct((t_all, D_MODEL), F32),
                   jax.ShapeDtypeStruct((t_all, D_MODEL), BF16),
                   jax.ShapeDtypeStruct((t_all, D_MODEL // 2), jnp.uint32),
                   jax.ShapeDtypeStruct((TOP_K, t_all), jnp.int32),
                   jax.ShapeDtypeStruct((TOP_K, t_all), jnp.int32),
                   jax.ShapeDtypeStruct((t_all, LANES), F32),
                   jax.ShapeDtypeStruct((N_EXPERTS, LANES), F32)],
        scratch_shapes=[pltpu.VMEM((N_EXPERTS, 1), F32)],
        compiler_params=_cparams(("arbitrary",)),
        name="output_stage",
    )(oa[0], oa[1], od[0], od[1], hf, hb, mo, of, ob, rg, xp, xs, mod, mg, npost, npre, wout, wrt, rb)


SC_CORES, SC_SUBCORES = 2, 16
SC_WORKERS = SC_CORES * SC_SUBCORES
SLOT_BLOCK = 256
DISPATCH_ROWS = 64
COMBINE_ROWS = 32


def _sc_mesh():
    return plsc.VectorSubcoreMesh(core_axis_name="core", subcore_axis_name="subcore")


def _sc_worker():
    return lax.axis_index("subcore") * SC_CORES + lax.axis_index("core")


def _sc_dispatch(hpk, dest, n_slots):
    t_all, width = hpk.shape
    per_worker = t_all // SC_WORKERS

    @functools.partial(
        pl.kernel, mesh=_sc_mesh(), out_type=jax.ShapeDtypeStruct((n_slots, width), hpk.dtype),
        scratch_types=[pltpu.VMEM((DISPATCH_ROWS, width), hpk.dtype)]
        + [pltpu.VMEM((DISPATCH_ROWS,), jnp.int32)] * TOP_K + [pltpu.SemaphoreType.DMA])
    def dispatch(x_hbm, d_hbm, o_hbm, rows_v, *rest):
        idx, sem = rest[:TOP_K], rest[TOP_K]

        @pl.loop(0, per_worker // DISPATCH_ROWS)
        def _(j):
            base = _sc_worker() * per_worker + j * DISPATCH_ROWS
            pltpu.sync_copy(x_hbm.at[pl.ds(base, DISPATCH_ROWS)], rows_v)
            for k in range(TOP_K):
                pltpu.sync_copy(d_hbm.at[pl.ds(k * t_all + base, DISPATCH_ROWS)], idx[k])
            copies = [pltpu.async_copy(rows_v, o_hbm.at[idx[k]], sem) for k in range(TOP_K)]
            for c in copies:
                c.wait()

    return dispatch(hpk, dest)


def _sc_combine(yb, dest, t_all):
    width = yb.shape[1]
    per_worker = t_all // SC_WORKERS

    @functools.partial(
        pl.kernel, mesh=_sc_mesh(), out_type=jax.ShapeDtypeStruct((TOP_K, t_all, width), yb.dtype),
        scratch_types=[pltpu.VMEM((COMBINE_ROWS, width), yb.dtype)] * 2
        + [pltpu.VMEM((COMBINE_ROWS,), jnp.int32)] * TOP_K + [pltpu.SemaphoreType.DMA] * 2)
    def combine(y_hbm, d_hbm, o_hbm, rows_a, rows_b, *rest):
        idx, sems = rest[:TOP_K], rest[TOP_K:]
        bufs = (rows_a, rows_b)

        @pl.loop(0, per_worker // COMBINE_ROWS)
        def _(j):
            base = _sc_worker() * per_worker + j * COMBINE_ROWS
            for k in range(TOP_K):
                pltpu.sync_copy(d_hbm.at[pl.ds(k * t_all + base, COMBINE_ROWS)], idx[k])
            pending = pltpu.async_copy(y_hbm.at[idx[0]], bufs[0], sems[0])
            for k in range(TOP_K):
                pending.wait()
                if k + 1 < TOP_K:
                    pending = pltpu.async_copy(y_hbm.at[idx[k + 1]], bufs[(k + 1) % 2], sems[(k + 1) % 2])
                pltpu.sync_copy(bufs[k % 2], o_hbm.at[k, pl.ds(base, COMBINE_ROWS)])

    return combine(yb, dest)


def _unpack_rows(words):
    lo = pltpu.bitcast(words << 16, F32).astype(BF16)
    hi = pltpu.bitcast(words & jnp.uint32(0xFFFF0000), F32).astype(BF16)
    return lo, hi


def _expert_kernel(be_ref, used_ref, xs_ref, wg_ref, wu_ref, wd_ref, y_ref, wg_s, wu_s, wd_s):
    b = pl.program_id(0)
    fresh = jnp.logical_or(b == 0, be_ref[b] != be_ref[jnp.maximum(b - 1, 0)])

    @pl.when(jnp.logical_and(fresh, b < used_ref[0]))
    def _():
        wg_s[...] = wg_ref[0, 0].astype(BF16)
        wu_s[...] = wu_ref[0, 0].astype(BF16)
        wd_s[...] = wd_ref[0, 0].astype(BF16)

    @pl.when(b < used_ref[0])
    def _():
        lo, hi = _unpack_rows(xs_ref[...])
        half = D_MODEL // 2
        gate = (jnp.dot(lo, wg_s[:half, :], preferred_element_type=F32)
                + jnp.dot(hi, wg_s[half:, :], preferred_element_type=F32))
        up = (jnp.dot(lo, wu_s[:half, :], preferred_element_type=F32)
              + jnp.dot(hi, wu_s[half:, :], preferred_element_type=F32))
        y_ref[...] = jnp.dot((_silu(gate) * up).astype(BF16), wd_s[...], preferred_element_type=F32)


def _expert_blocks(layer, xs, block_expert, blocks_used, wg, wu, wdn):
    n_slots = xs.shape[0]
    wspec = lambda shape: pl.BlockSpec((1, 1) + shape, lambda b, be, used: (layer, be[b], 0, 0))
    grid_spec = pltpu.PrefetchScalarGridSpec(
        num_scalar_prefetch=2,
        grid=(n_slots // SLOT_BLOCK,),
        in_specs=[pl.BlockSpec((SLOT_BLOCK, D_MODEL // 2), lambda b, be, used: (b, 0)),
                  wspec((D_MODEL, D_EXPERT)), wspec((D_MODEL, D_EXPERT)), wspec((D_EXPERT, D_MODEL))],
        out_specs=pl.BlockSpec((SLOT_BLOCK, D_MODEL), lambda b, be, used: (b, 0)),
        scratch_shapes=[pltpu.VMEM((D_MODEL, D_EXPERT), BF16), pltpu.VMEM((D_MODEL, D_EXPERT), BF16),
                        pltpu.VMEM((D_EXPERT, D_MODEL), BF16)])
    return pl.pallas_call(
        _expert_kernel, grid_spec=grid_spec,
        out_shape=jax.ShapeDtypeStruct((n_slots, D_MODEL), F32),
        compiler_params=_cparams(("arbitrary",)),
        name="moe_experts",
    )(block_expert, blocks_used, xs, wg, wu, wdn)


def _moe_out_kernel(g_ref, wk_ref, h_ref, x1_ref, mod_ref, npost_ref, sg_ref, su_ref, sd_ref,
                    op_ref, os_ref, *, npt):
    h = h_ref[...]
    act = _silu(jnp.dot(h, sg_ref[...], preferred_element_type=F32)) * jnp.dot(h, su_ref[...],
                                                                              preferred_element_type=F32)
    moe = jnp.dot(act.astype(BF16), sd_ref[...], preferred_element_type=F32)
    wk = wk_ref[...]
    for k in range(TOP_K):
        moe = moe + g_ref[k] * wk[:, k:k + 1]
    m = mod_ref[0]
    res = x1_ref[...] + m[5:6] * (_rms(moe) * npost_ref[...])
    i = pl.program_id(0)

    @pl.when(i < npt)
    def _():
        op_ref[...] = res

    @pl.when(i >= npt)
    def _():
        os_ref[...] = res


def _moe_out_stage(g, wk, h2, x1, mod, npost, sg, su, sd, dims):
    t_all = h2.shape[0]
    npt, tps = dims["tp"] // TM, dims["ns"] // TM
    nst = t_all // TM - npt
    row = lambda i: (i, 0)
    full = lambda a: pl.BlockSpec(a.shape, lambda i: (0, 0))
    return pl.pallas_call(
        functools.partial(_moe_out_kernel, npt=npt),
        grid=(t_all // TM,),
        in_specs=[pl.BlockSpec((TOP_K, TM, D_MODEL), lambda i: (0, i, 0)), pl.BlockSpec((TM, LANES), row),
                  pl.BlockSpec((TM, D_MODEL), row), pl.BlockSpec((TM, D_MODEL), row),
                  pl.BlockSpec((1, 6, D_MODEL), lambda i: (_mod_row(i, npt, tps), 0, 0)),
                  full(npost), full(sg), full(su), full(sd)],
        out_specs=_pair_specs(npt, nst, TM, D_MODEL),
        out_shape=[jax.ShapeDtypeStruct((npt * TM, D_MODEL), F32), jax.ShapeDtypeStruct((nst * TM, D_MODEL), F32)],
        compiler_params=_cparams(("arbitrary",)),
        name="moe_combine",
    )(g, wk, h2, x1, mod, npost, sg, su, sd)


def _moe_stage(layer, h2, hpk, eidx, rank, wk, counts, x1, mod, npost, wg, wu, wdn, sg, su, sd, dims):
    t_all = h2.shape[0]
    n_blocks = -(-(t_all * TOP_K + N_EXPERTS * (SLOT_BLOCK - 1)) // SLOT_BLOCK)
    cnt = counts[:, 0].astype(jnp.int32)
    padded = (cnt + SLOT_BLOCK - 1) // SLOT_BLOCK * SLOT_BLOCK
    pad_end = jnp.cumsum(padded)
    pad_start = pad_end - padded
    experts = jnp.arange(N_EXPERTS, dtype=jnp.int32)[:, None, None]
    dest = rank + jnp.sum(jnp.where(eidx[None] == experts, pad_start[:, None, None], 0), axis=0)
    dest = dest.reshape(TOP_K * t_all)
    block_expert = jnp.minimum(
        jnp.searchsorted(pad_end, jnp.arange(n_blocks, dtype=jnp.int32) * SLOT_BLOCK, side="right"),
        N_EXPERTS - 1).astype(jnp.int32)
    blocks_used = (pad_end[-1:] // SLOT_BLOCK).astype(jnp.int32)

    xs = _sc_dispatch(hpk, dest, n_blocks * SLOT_BLOCK)
    yb = _expert_blocks(layer, xs, block_expert, blocks_used, wg, wu, wdn)
    g = _sc_combine(yb, dest, t_all)
    return _moe_out_stage(g, wk, h2, x1, mod, npost, sg, su, sd, dims)


_PERM32 = np.concatenate([np.arange(8, 16), np.arange(0, 8), np.arange(24, 32), np.arange(16, 24)])


def _rope_tables(dims):
    n = dims["ns"]
    pos = jnp.arange(n)
    quarter = QK_ROPE // 4
    inv = 1.0 / (ROPE_THETA ** (jnp.arange(quarter, dtype=F32) / quarter))
    ang_r = (pos // GRID_W).astype(F32)[:, None] * inv[None, :]
    ang_c = (pos % GRID_W).astype(F32)[:, None] * inv[None, :]
    cos32 = jnp.concatenate([jnp.cos(ang_r)] * 2 + [jnp.cos(ang_c)] * 2, axis=-1)
    sin32 = jnp.concatenate([-jnp.sin(ang_r), jnp.sin(ang_r), -jnp.sin(ang_c), jnp.sin(ang_c)], axis=-1)
    cos256, sin256 = jnp.tile(cos32, (1, 8)), jnp.tile(sin32, (1, 8))
    one, zero = jnp.ones((n, QK_NOPE), F32), jnp.zeros((n, QK_NOPE), F32)
    cosq = jnp.concatenate([one, cos32] * N_HEADS, axis=-1)
    sinq = jnp.concatenate([zero, sin32] * N_HEADS, axis=-1)
    return cos256, sin256, cosq, sinq


def _split_w_in(w_in_l):
    sizes = (256, 256, 256, 256, 256, 256, 256, 16, 256, 256, 256, 256, Q_LORA, KV_LORA, QK_ROPE)
    idx = np.cumsum(sizes)[:-1]
    return jnp.split(w_in_l, [int(v) for v in idx], axis=-1)


def _layer_weights(w_in_l, w_uq_l):
    (a_q, a_k, a_v, m_q, m_k, m_v, m_o, m_g, r_q, r_k, r_v, r_g, d_cq, d_ckv, d_kr) = _split_w_in(w_in_l)
    perm256 = np.concatenate([_PERM32 + 32 * j for j in range(8)])
    pad = lambda w, n: jnp.pad(w, ((0, 0), (0, n - w.shape[1])))
    tail = pad(jnp.concatenate([d_kr, m_g], axis=-1), LANES)
    tail_s = pad(d_kr[:, _PERM32], LANES)
    w_all = jnp.concatenate([a_q, a_k, a_v, m_q, m_k, m_v, m_o, r_q, r_k, r_v, r_g, d_cq, d_ckv, tail,
                             a_q[:, perm256], a_k[:, perm256], tail_s], axis=-1).astype(BF16)
    per = QK_NOPE + QK_ROPE
    permq = np.concatenate([np.concatenate([np.arange(QK_NOPE), QK_NOPE + _PERM32]) + per * j
                            for j in range(N_HEADS)])
    return w_all, w_uq_l.astype(BF16), w_uq_l[:, permq].astype(BF16)


def kernel(x_prompt, x_sample, cache_diff_k, cache_diff_v, state_mlstm_C, state_mlstm_n, state_mlstm_m, state_ret_S, cache_mla_ckv, cache_mla_krope, c, c_ctx, w_mod, b_mod, norm_pre, norm_post, w_in, w_out, diff_lambda, diff_norm, mlstm_gate_bias, mlstm_norm, ret_decay_logit, mla_q_norm, mla_w_uq, mla_kv_norm, mla_w_ukv, moe_w_router, moe_router_bias, moe_w_gate, moe_w_up, moe_w_down, shared_w_gate, shared_w_up, shared_w_down):
    bp, n_p, _ = x_prompt.shape
    bs, n_s, _ = x_sample.shape
    depth = w_in.shape[0]
    past = cache_diff_k.shape[2]
    dims = dict(bp=bp, np=n_p, bs=bs, ns=n_s, tp=bp * n_p, past=past)
    tp, ts = bp * n_p, bs * n_s
    assert n_p % TM == 0 and n_s % TM == 0 and n_p % CHUNK == 0 and n_s % min(TK, n_s) == 0
    assert past % min(TK, past) == 0 and tp % n_s == 0 and bs + 1 <= 8 and n_s % GRID_W == 0

    xp, xs = x_prompt.reshape(tp, D_MODEL), x_sample.reshape(ts, D_MODEL)
    cond = jnp.zeros((8, D_MODEL), F32).at[0].set(c_ctx).at[1:1 + bs].set(c)
    mod_all = _modulation(cond, w_mod, b_mod).reshape(depth, 8, 6, D_MODEL)
    tabs = _rope_tables(dims)
    scan_tab = _scan_table(dims)

    outs = [[] for _ in range(8)]
    for l in range(depth):
        lam_init = 0.8 - 0.6 * math.exp(-0.3 * l)
        mod = mod_all[l]
        w_all, wuq, wuqs = _layer_weights(w_in[l], mla_w_uq[l])
        wukv = mla_w_ukv[l].astype(BF16)
        (aq1, aq2, ak1t, ak2t, avh, ak, av, mqkv, mo, rqkv, rg, tail, qmla, ckv, kmlat, vmla) = _input_stage(
            xp, xs, mod, norm_pre[l, 0:1], w_all, tabs, mla_q_norm[l][None], wuq, wuqs,
            mla_kv_norm[l][None], wukv, dims)

        lamv, dg = diff_lambda[l], diff_norm[l][None]
        ckt = jnp.transpose(cache_diff_k[:, l], (2, 3, 0, 1)).reshape(N_HEADS, HEAD_DIM, bs * past).astype(BF16)
        cv = jnp.transpose(cache_diff_v[:, l], (2, 0, 1, 3)).reshape(N_HEADS, bs * past, HEAD_DIM).astype(BF16)
        oa_p = _attention("diff_attn_ctx", [aq1, aq2], [ak1t, ak2t], avh, None, None, [lamv, dg],
                          0, bp, n_p, 0, lam_init)
        oa_s = _attention("diff_attn_latent", [aq1, aq2], [ak1t, ak2t], avh,
                          [ckt[:, :DIFF_HALF], ckt[:, DIFF_HALF:]], cv, [lamv, dg], tp, bs, n_s, past, lam_init)

        ckv_ctx = cache_mla_ckv[:, l].reshape(bs * past, KV_LORA)
        kvc = _project(ckv_ctx, wukv, BF16).reshape(bs * past, N_HEADS, QK_NOPE + V_HEAD)
        krc = jnp.broadcast_to(cache_mla_krope[:, l].reshape(bs * past, 1, QK_ROPE).astype(BF16),
                               (bs * past, N_HEADS, QK_ROPE))
        kct = jnp.transpose(jnp.concatenate([kvc[..., :QK_NOPE], krc], axis=-1), (1, 2, 0))
        vc = jnp.transpose(kvc[..., QK_NOPE:], (1, 0, 2))
        od_p = _attention("mla_attn_ctx", [qmla], [kmlat], vmla, None, None, [], 0, bp, n_p, 0)
        od_s = _attention("mla_attn_latent", [qmla], [kmlat], vmla, [kct], vc, [], tp, bs, n_s, past)

        zeros = lambda a: jnp.zeros((bp,) + a.shape[1:], F32)
        st_c, st_n, st_m, st_s = state_mlstm_C[:, l], state_mlstm_n[:, l], state_mlstm_m[:, l], state_ret_S[:, l]
        bias_row = jnp.zeros((1, LANES), F32).at[0, TAIL_GATE0:TAIL_GATE0 + 16].set(mlstm_gate_bias[l].reshape(16))
        hf, hb, c_n, n_n, m_n = _mlstm_scan(scan_tab, mqkv, tail, bias_row,
                                            jnp.concatenate([zeros(st_c), st_c]),
                                            jnp.concatenate([zeros(st_n), st_n]),
                                            jnp.concatenate([zeros(st_m), st_m]))
        of, ob, s_n = _ret_scan(scan_tab, rqkv, ret_decay_logit[l], jnp.concatenate([zeros(st_s), st_s]))

        x1, h2, hpk, eidx, rank, wk, counts = _output_stage(
            (oa_p, oa_s), (od_p, od_s), hf, hb, mo, of, ob, rg, xp, xs, mod,
            mlstm_norm[l][None], norm_post[l, 0:1], norm_pre[l, 1:2],
            w_out[l].astype(BF16), moe_w_router[l].T.astype(BF16), moe_router_bias[l][:, None], dims)
        xp, xs = _moe_stage(l, h2, hpk, eidx, rank, wk, counts, x1, mod, norm_post[l, 1:2],
                            moe_w_gate, moe_w_up, moe_w_down, shared_w_gate[l].astype(BF16),
                            shared_w_up[l].astype(BF16), shared_w_down[l].astype(BF16), dims)

        outs[0].append(ak[:tp].reshape(bp, n_p, N_HEADS, HEAD_DIM))
        outs[1].append(av[:tp].reshape(bp, n_p, N_HEADS, HEAD_DIM))
        outs[2].append(c_n[:bp])
        outs[3].append(n_n[:bp])
        outs[4].append(m_n[:bp])
        outs[5].append(s_n[:bp])
        outs[6].append(ckv[:tp].reshape(bp, n_p, KV_LORA))
        outs[7].append(tail[:tp, :QK_ROPE].reshape(bp, n_p, QK_ROPE))

    return (xp.reshape(bp, n_p, D_MODEL), xs.reshape(bs, n_s, D_MODEL)) + tuple(
        jnp.stack(o, axis=1) for o in outs)
```

```python
import functools
import math

import numpy as np
import jax
import jax.numpy as jnp
from jax import lax
from jax.experimental import pallas as pl
from jax.experimental.pallas import tpu as pltpu
from jax.experimental.pallas import tpu_sc as plsc

F32 = jnp.float32
BF16 = jnp.bfloat16
HIGHEST = lax.Precision.HIGHEST

D_MODEL = 1024
GRID_W = 64
GROUP_WIDTH = 256
HEAD_DIM = 64
N_HEADS = 4
DIFF_HALF = 32
ROPE_THETA = 10000.0
Q_LORA = 256
KV_LORA = 128
QK_NOPE = 64
QK_ROPE = 32
V_HEAD = 64
N_EXPERTS = 64
TOP_K = 8
N_GROUPS = 8
TOPK_GROUPS = 4
D_EXPERT = 256
ROUTED_SCALE = 2.5
CHUNK = 128
EPS = 1e-6
Q_MLA = N_HEADS * (QK_NOPE + QK_ROPE)
KV_MLA = N_HEADS * (QK_NOPE + V_HEAD)

LANES = 128
VMEM_LIMIT = 56 * 1024 * 1024

TM = 256
TM_MOE = 1024
TQ = 256
TK = 2048

C_AQ, C_AK, C_AV = 0, 256, 512
C_MQKV, C_MO = 768, 1536
C_RQKV, C_RG = 1792, 2560
C_CQ, C_CKV, C_TAIL = 2816, 3072, 3200
C_AQS, C_AKS, C_TAILS = 3328, 3584, 3840
W_ALL = 3968
TAIL_GATE0 = QK_ROPE


def _cparams(sem, flags=None):
    return pltpu.CompilerParams(dimension_semantics=sem, vmem_limit_bytes=VMEM_LIMIT, flags=flags)


def _rms(x):
    return x * lax.rsqrt(jnp.mean(x * x, axis=-1, keepdims=True) + EPS)


def _head_mean_matrix(width):
    r = lax.broadcasted_iota(jnp.int32, (width, width), 0) // HEAD_DIM
    c = lax.broadcasted_iota(jnp.int32, (width, width), 1) // HEAD_DIM
    return jnp.where(r == c, 1.0 / HEAD_DIM, 0.0).astype(F32)


def _rms_heads(x):
    ms = jnp.dot(x * x, _head_mean_matrix(x.shape[-1]), precision=HIGHEST, preferred_element_type=F32)
    return x * lax.rsqrt(ms + EPS)


def _sigmoid(x):
    return 1.0 / (1.0 + jnp.exp(-x))


def _silu(x):
    return x * _sigmoid(x)


def _log_sigmoid(x):
    return jnp.minimum(x, 0.0) - jnp.log1p(jnp.exp(-jnp.abs(x)))


def _bdot(a, b):
    return jnp.dot(a.astype(BF16), b.astype(BF16), preferred_element_type=F32)


def _mod_kernel(c_ref, w_ref, b_ref, o_ref):
    o_ref[0] = _bdot(_silu(c_ref[...]), w_ref[0]) + b_ref[0]


def _modulation(cond, w_mod, b_mod):
    depth, _, n = w_mod.shape
    tn = 1536
    return pl.pallas_call(
        _mod_kernel,
        grid=(depth, n // tn),
        in_specs=[pl.BlockSpec((8, D_MODEL), lambda l, j: (0, 0)),
                  pl.BlockSpec((1, D_MODEL, tn), lambda l, j: (l, 0, j)),
                  pl.BlockSpec((1, 1, tn), lambda l, j: (l, 0, j))],
        out_specs=pl.BlockSpec((1, 8, tn), lambda l, j: (l, 0, j)),
        out_shape=jax.ShapeDtypeStruct((depth, 8, n), F32),
        compiler_params=_cparams(("parallel", "parallel")),
        name="adaln_mod",
    )(cond, w_mod, b_mod.reshape(depth, 1, n))


def _in_kernel(xp_ref, xs_ref, mod_ref, npre_ref, w_ref, cos_ref, sin_ref, cosq_ref, sinq_ref,
               qg_ref, wuq_ref, wuqs_ref, kvg_ref, wukv_ref,
               aq1_ref, aq2_ref, ak1t_ref, ak2t_ref, avh_ref, ak_ref, av_ref,
               mqkv_ref, mo_ref, rqkv_ref, rg_ref, tail_ref,
               qmla_ref, ckv_ref, kmlat_ref, vmla_ref, *, npt):
    is_ctx = pl.program_id(0) < npt
    x = jnp.where(is_ctx, xp_ref[...], xs_ref[...])
    m = mod_ref[0]
    h = (_rms(x) * npre_ref[...] * (1.0 + m[1:2]) + m[0:1]).astype(BF16)

    def proj(c0, width):
        return jnp.dot(h, w_ref[:, c0:c0 + width], preferred_element_type=F32)

    cos = jnp.where(is_ctx, 1.0, cos_ref[...])
    sin = jnp.where(is_ctx, 0.0, sin_ref[...])
    aq = (proj(C_AQ, 256) * cos + proj(C_AQS, 256) * sin) * (DIFF_HALF ** -0.5)
    ak = proj(C_AK, 256) * cos + proj(C_AKS, 256) * sin
    av = proj(C_AV, 256)
    ak_ref[...] = ak
    av_ref[...] = av
    ak_t = ak.T.astype(BF16)
    for hd in range(N_HEADS):
        lo = hd * HEAD_DIM
        aq1_ref[hd] = aq[:, lo:lo + DIFF_HALF].astype(BF16)
        aq2_ref[hd] = aq[:, lo + DIFF_HALF:lo + HEAD_DIM].astype(BF16)
        ak1t_ref[hd] = ak_t[lo:lo + DIFF_HALF, :]
        ak2t_ref[hd] = ak_t[lo + DIFF_HALF:lo + HEAD_DIM, :]
        avh_ref[hd] = av[:, lo:lo + HEAD_DIM].astype(BF16)

    mqkv_ref[...] = proj(C_MQKV, 768).astype(BF16)
    mo_ref[...] = proj(C_MO, 256)
    rqkv_ref[...] = proj(C_RQKV, 768).astype(BF16)
    rg_ref[...] = proj(C_RG, 256)

    tail = proj(C_TAIL, LANES)
    tail_ref[...] = tail
    kr_t = (tail * cos[:, :LANES] + proj(C_TAILS, LANES) * sin[:, :LANES]).T[:QK_ROPE, :].astype(BF16)

    cqn = (_rms(proj(C_CQ, Q_LORA)) * qg_ref[...]).astype(BF16)
    q = jnp.dot(cqn, wuq_ref[...], preferred_element_type=F32)
    qs = jnp.dot(cqn, wuqs_ref[...], preferred_element_type=F32)
    cosq = jnp.where(is_ctx, 1.0, cosq_ref[...])
    sinq = jnp.where(is_ctx, 0.0, sinq_ref[...])
    qmla = (q * cosq + qs * sinq) * ((QK_NOPE + QK_ROPE) ** -0.5)
    for hd in range(N_HEADS):
        lo = hd * (QK_NOPE + QK_ROPE)
        qmla_ref[hd] = qmla[:, lo:lo + QK_NOPE + QK_ROPE].astype(BF16)
    ckvn = _rms(proj(C_CKV, KV_LORA)) * kvg_ref[...]
    ckv_ref[...] = ckvn
    kv = jnp.dot(ckvn.astype(BF16), wukv_ref[...], preferred_element_type=F32)
    kv_t = kv.T.astype(BF16)
    per = QK_NOPE + V_HEAD
    for hd in range(N_HEADS):
        kmlat_ref[hd, 0:QK_NOPE, :] = kv_t[hd * per:hd * per + QK_NOPE, :]
        kmlat_ref[hd, QK_NOPE:QK_NOPE + QK_ROPE, :] = kr_t
        vmla_ref[hd] = kv[:, hd * per + QK_NOPE:(hd + 1) * per].astype(BF16)


def _mod_row(i, npt, tps):
    return jnp.where(i < npt, 0, 1 + (i - npt) // tps)


def _pair_specs(npt, nst, tm, width):
    return [pl.BlockSpec((tm, width), lambda i, *_: (jnp.minimum(i, npt - 1), 0)),
            pl.BlockSpec((tm, width), lambda i, *_: (jnp.clip(i - npt, 0, nst - 1), 0))]


def _input_stage(xp, xs, mod, npre, w_all, tabs, qg, wuq, wuqs, kvg, wukv, dims):
    t_all = xp.shape[0] + xs.shape[0]
    npt, nst, tps = dims["tp"] // TM, xs.shape[0] // TM, dims["ns"] // TM
    row = lambda i: (i, 0)
    tab = lambda w: pl.BlockSpec((TM, w), lambda i: (jnp.maximum(i - npt, 0) % tps, 0))
    hrow = lambda i: (0, i, 0)
    const2 = lambda i: (0, 0)
    tok = lambda w: pl.BlockSpec((TM, w), row)
    headed = lambda w: pl.BlockSpec((N_HEADS, TM, w), hrow)
    headed_t = lambda d: pl.BlockSpec((N_HEADS, d, TM), lambda i: (0, 0, i))
    full = lambda a: pl.BlockSpec(a.shape, const2)
    cos, sin, cosq, sinq = tabs
    out_shapes = [
        (headed(DIFF_HALF), (N_HEADS, t_all, DIFF_HALF), BF16),
        (headed(DIFF_HALF), (N_HEADS, t_all, DIFF_HALF), BF16),
        (headed_t(DIFF_HALF), (N_HEADS, DIFF_HALF, t_all), BF16),
        (headed_t(DIFF_HALF), (N_HEADS, DIFF_HALF, t_all), BF16),
        (headed(HEAD_DIM), (N_HEADS, t_all, HEAD_DIM), BF16),
        (tok(256), (t_all, 256), F32),
        (tok(256), (t_all, 256), F32),
        (tok(768), (t_all, 768), BF16),
        (tok(256), (t_all, 256), F32),
        (tok(768), (t_all, 768), BF16),
        (tok(256), (t_all, 256), F32),
        (tok(LANES), (t_all, LANES), F32),
        (headed(QK_NOPE + QK_ROPE), (N_HEADS, t_all, QK_NOPE + QK_ROPE), BF16),
        (tok(KV_LORA), (t_all, KV_LORA), F32),
        (headed_t(QK_NOPE + QK_ROPE), (N_HEADS, QK_NOPE + QK_ROPE, t_all), BF16),
        (headed(V_HEAD), (N_HEADS, t_all, V_HEAD), BF16),
    ]
    return pl.pallas_call(
        functools.partial(_in_kernel, npt=npt),
        grid=(t_all // TM,),
        in_specs=_pair_specs(npt, nst, TM, D_MODEL) + [
            pl.BlockSpec((1, 6, D_MODEL), lambda i: (_mod_row(i, npt, tps), 0, 0)),
            full(npre), full(w_all), tab(256), tab(256), tab(Q_MLA), tab(Q_MLA),
            full(qg), full(wuq), full(wuqs), full(kvg), full(wukv)],
        out_specs=[s for s, _, _ in out_shapes],
        out_shape=[jax.ShapeDtypeStruct(shp, dt) for _, shp, dt in out_shapes],
        compiler_params=_cparams(("parallel",)),
        name="input_stage",
    )(xp, xs, mod, npre, w_all, cos, sin, cosq, sinq, qg, wuq, wuqs, kvg, wukv)


def _attn_kernel(*refs, n_soft, n_new, n_ctx, lam_init):
    refs = list(refs)
    q_refs, kt_refs, v_ref = refs[:n_soft], refs[n_soft:2 * n_soft], refs[2 * n_soft]
    pos = 2 * n_soft + 1
    if n_ctx:
        ckt_refs, cv_ref = refs[pos:pos + n_soft], refs[pos + n_soft]
        pos += n_soft + 1
    if n_soft == 2:
        lam_ref, g_ref = refs[pos:pos + 2]
        pos += 2
    o_ref = refs[pos]
    tq = o_ref.shape[0]
    nchain = n_soft * N_HEADS

    def chunk(state, kt_of, v_of):
        ms, ls, accs = state
        new_m, new_l, new_acc = list(ms), list(ls), list(accs)
        order = [(j, hd) for hd in range(N_HEADS) for j in range(n_soft)]
        ss = [jnp.dot(q_refs[j][hd], kt_of(j, hd), preferred_element_type=F32) for j, hd in order]
        ps, alphas = [], []
        for (j, hd), s in zip(order, ss):
            c = j * N_HEADS + hd
            m_new = jnp.maximum(ms[c], jnp.broadcast_to(jnp.max(s, axis=-1, keepdims=True), (tq, LANES)))
            p = jnp.exp(s - jnp.tile(m_new, (1, s.shape[1] // LANES)))
            alpha = jnp.exp(ms[c] - m_new)
            new_l[c] = alpha * ls[c] + jnp.broadcast_to(jnp.sum(p, axis=-1, keepdims=True), (tq, LANES))
            new_m[c] = m_new
            ps.append(p.astype(BF16))
            alphas.append(alpha)
        for (j, hd), p, alpha in zip(order, ps, alphas):
            c = j * N_HEADS + hd
            new_acc[c] = alpha[:, :V_HEAD] * accs[c] + jnp.dot(p, v_of(hd), preferred_element_type=F32)
        return tuple(new_m), tuple(new_l), tuple(new_acc)

    state = (tuple(jnp.full((tq, LANES), -jnp.inf, F32) for _ in range(nchain)),
             tuple(jnp.zeros((tq, LANES), F32) for _ in range(nchain)),
             tuple(jnp.zeros((tq, V_HEAD), F32) for _ in range(nchain)))
    if n_ctx:
        cstep = min(TK, n_ctx)
        for i in range(n_ctx // cstep):
            state = chunk(state, lambda j, hd, i=i: ckt_refs[j][hd, :, i * cstep:(i + 1) * cstep],
                          lambda hd, i=i: cv_ref[hd, i * cstep:(i + 1) * cstep, :])
    step = min(TK, n_new)

    def body(i, st):
        start = pl.multiple_of(i * step, step)
        return chunk(st, lambda j, hd: kt_refs[j][hd, :, pl.ds(start, step)],
                     lambda hd: v_ref[hd, pl.ds(start, step), :])

    _, ls, accs = lax.fori_loop(0, n_new // step, body, state)

    if n_soft == 2:
        lv = lam_ref[...]
        lam = (jnp.exp(jnp.sum(lv[0:1] * lv[1:2], axis=-1, keepdims=True))
               - jnp.exp(jnp.sum(lv[2:3] * lv[3:4], axis=-1, keepdims=True)) + lam_init)
    for hd in range(N_HEADS):
        out = accs[hd] / ls[hd][:, :V_HEAD]
        if n_soft == 2:
            a = out - lam * (accs[N_HEADS + hd] / ls[N_HEADS + hd][:, :V_HEAD])
            out = _rms(a) * g_ref[...] * (1.0 - lam_init)
        o_ref[:, hd * V_HEAD:(hd + 1) * V_HEAD] = out


def _attention(name, qs, kts, v, ctx_kts, ctx_v, extras, row0, nb, n, n_ctx, lam_init=0.0):
    tq = min(TQ, n)
    nqt = n // tq
    n_soft = len(qs)
    qmap = lambda b, i: (0, row0 // tq + b * nqt + i, 0)
    in_specs = [pl.BlockSpec((N_HEADS, tq, a.shape[-1]), qmap) for a in qs]
    in_specs += [pl.BlockSpec((N_HEADS, a.shape[1], n), lambda b, i: (0, 0, row0 // n + b)) for a in kts]
    in_specs += [pl.BlockSpec((N_HEADS, n, V_HEAD), lambda b, i: (0, row0 // n + b, 0))]
    args = list(qs) + list(kts) + [v]
    if n_ctx:
        in_specs += [pl.BlockSpec((N_HEADS, a.shape[1], n_ctx), lambda b, i: (0, 0, b)) for a in ctx_kts]
        in_specs += [pl.BlockSpec((N_HEADS, n_ctx, V_HEAD), lambda b, i: (0, b, 0))]
        args += list(ctx_kts) + [ctx_v]
    in_specs += [pl.BlockSpec(a.shape, lambda b, i: (0, 0)) for a in extras]
    args += list(extras)
    return pl.pallas_call(
        functools.partial(_attn_kernel, n_soft=n_soft, n_new=n, n_ctx=n_ctx, lam_init=lam_init),
        grid=(nb, nqt),
        in_specs=in_specs,
        out_specs=pl.BlockSpec((tq, N_HEADS * V_HEAD), lambda b, i: (b * nqt + i, 0)),
        out_shape=jax.ShapeDtypeStruct((nb * n, N_HEADS * V_HEAD), F32),
        compiler_params=_cparams(("parallel", "parallel")),
        name=name,
    )(*args)


def _tri(lower):
    r = lax.broadcasted_iota(jnp.int32, (CHUNK, CHUNK), 0)
    c = lax.broadcasted_iota(jnp.int32, (CHUNK, CHUNK), 1)
    return (c <= r) if lower else (c >= r)


def _nt(a, b):
    return lax.dot_general(a, b, (((1,), (1,)), ((), ())), preferred_element_type=F32)


def _tn(a, b):
    return lax.dot_general(a, b, (((0,), (0,)), ((), ())), preferred_element_type=F32)


def _head_slices(qkv, hd):
    lo = hd * HEAD_DIM
    return qkv[:, lo:lo + HEAD_DIM], qkv[:, 256 + lo:256 + lo + HEAD_DIM], qkv[:, 512 + lo:512 + lo + HEAD_DIM]


def _row_dot(row, mat, transpose_mat):
    row8 = jnp.broadcast_to(row, (8, row.shape[1])).astype(BF16)
    dims = (((1,), (1,)), ((), ())) if transpose_mat else (((1,), (0,)), ((), ()))
    return lax.dot_general(row8, mat, dims, preferred_element_type=F32)[0:1]


def _load_state_t(dst, src):
    for d in range(2):
        for hd in range(N_HEADS):
            dst[d, hd] = src[d, hd].T


def _mlstm_step(dirs, bias_ref, ct_s, n_s, m_s):
    k_scale = HEAD_DIM ** -0.5
    lower = _tri(True).astype(F32)
    upper = _tri(False).astype(F32)
    chains = []
    for d, qkv_ref, tail_ref, _ in dirs:
        g = tail_ref[...] + bias_ref[...]
        ls = _log_sigmoid(g)
        g_t, ls_t = g.T, ls.T
        left, right = (lower, upper) if d == 0 else (upper, lower)
        cum_col = jnp.dot(left, ls, precision=HIGHEST, preferred_element_type=F32)
        cum_row = jnp.dot(ls_t, right, precision=HIGHEST, preferred_element_type=F32)
        qkv = qkv_ref[...]
        for hd in range(N_HEADS):
            ci = TAIL_GATE0 + 4 * (2 * d) + hd
            cf = ci + 4
            q, k, v = _head_slices(qkv, hd)
            chains.append(dict(
                d=d, hd=hd, q=q, k=k, v=v, li_row=g_t[ci:ci + 1, :], b_row=cum_row[cf:cf + 1, :],
                c_col=g[:, ci:ci + 1] - cum_col[:, cf:cf + 1],
                m_prev=m_s[d:d + 1, hd:hd + 1], ct=ct_s[d, hd], n=n_s[d, hd:hd + 1, :]))
    for c in chains:
        c["s"] = _nt(c["k"], c["q"])
        c["v_t"] = c["v"].T
    for c in chains:
        valid = _tri(c["d"] != 0)
        log_d = jnp.where(valid, c["b_row"] + c["c_col"], -jnp.inf)
        log_inter = c["b_row"] + c["m_prev"]
        c["m_t"] = jnp.maximum(log_inter, jnp.max(log_d, axis=0, keepdims=True))
        c["w"] = c["s"] * k_scale * jnp.exp(log_d - c["m_t"])
        c["w_inter"] = jnp.exp(log_inter - c["m_t"])
    for c in chains:
        num = (jnp.dot(c["v_t"], c["w"].astype(BF16), preferred_element_type=F32)
               + c["w_inter"] * _nt(c["ct"].astype(BF16), c["q"]))
        den = jnp.sum(c["w"], axis=0, keepdims=True) + c["w_inter"] * _row_dot(c["n"], c["q"], True)
        c["h"] = num / jnp.maximum(jnp.abs(den), jnp.exp(-c["m_t"]))
    for c in chains:
        last = CHUNK - 1 if c["d"] == 0 else 0
        c["m_new"] = c["m_t"][:, last:last + 1]
        b_last = c["b_row"][:, last:last + 1]
        w_end_row = jnp.exp(b_last - c["b_row"] + c["li_row"] - c["m_new"])
        w_end_col = jnp.exp(c["c_col"] + (b_last - c["m_new"]))
        decay = jnp.exp(b_last + c["m_prev"] - c["m_new"])
        kw = (c["k"].astype(F32) * (w_end_col * k_scale)).astype(BF16)
        c["ct_new"] = decay * c["ct"] + jnp.dot(c["v_t"], kw, preferred_element_type=F32)
        c["n_new"] = decay * c["n"] + _row_dot(w_end_row, c["k"], False) * k_scale
    for j, (_, _, _, h_ref) in enumerate(dirs):
        h_ref[...] = jnp.concatenate([c["h"] for c in chains[j * N_HEADS:(j + 1) * N_HEADS]], axis=0).T
    for c in chains:
        d, hd = c["d"], c["hd"]
        ct_s[d, hd] = c["ct_new"]
        n_s[d, hd:hd + 1, :] = c["n_new"]
        m_s[d:d + 1, hd:hd + 1] = c["m_new"]


def _mlstm_kernel(tab_ref, qkvf_ref, tailf_ref, qkvb_ref, tailb_ref, bias_ref, c0_ref, n0_ref, m0_ref,
                  hf_ref, hb_ref, cn_ref, nn_ref, mn_ref, ct_s, n_s, m_s):
    i = pl.program_id(0)

    @pl.when(tab_ref[3, i] == 1)
    def _():
        _load_state_t(ct_s, c0_ref.at[0])
        n_s[...] = n0_ref[0]
        m_s[...] = m0_ref[0]

    _mlstm_step([(0, qkvf_ref, tailf_ref, hf_ref), (1, qkvb_ref, tailb_ref, hb_ref)], bias_ref, ct_s, n_s, m_s)

    @pl.when(tab_ref[4, i] == 1)
    def _():
        _load_state_t(cn_ref.at[0], ct_s)
        nn_ref[0] = n_s[...]
        mn_ref[0] = m_s[...]


def _ret_step(dirs, lg_ref, st_s):
    k_scale = HEAD_DIM ** -0.5
    s = lax.broadcasted_iota(jnp.int32, (CHUNK, CHUNK), 0)
    t = lax.broadcasted_iota(jnp.int32, (CHUNK, CHUNK), 1)
    lane = lax.broadcasted_iota(jnp.int32, (1, CHUNK), 1)
    chains = []
    for d, qkv_ref, _ in dirs:
        lag = (t - s) if d == 0 else (s - t)
        pos = (lane if d == 0 else CHUNK - 1 - lane).astype(F32)
        qkv = qkv_ref[...]
        for hd in range(N_HEADS):
            q, k, v = _head_slices(qkv, hd)
            chains.append(dict(d=d, hd=hd, q=q, k=k, v=v, lag=lag, pos=pos, st=st_s[d, hd],
                               lg=_log_sigmoid(lg_ref[d:d + 1, hd:hd + 1])))
    for c in chains:
        c["s"] = _nt(c["k"], c["q"])
        c["v_t"] = c["v"].T
    for c in chains:
        intra = jnp.where(c["lag"] >= 0, jnp.exp(jnp.maximum(c["lag"], 0).astype(F32) * c["lg"]), 0.0)
        c["a"] = (c["s"] * k_scale * intra).astype(BF16)
    for c in chains:
        inter = jnp.exp((c["pos"] + 1.0) * c["lg"])
        c["o"] = jnp.dot(c["v_t"], c["a"], preferred_element_type=F32) + inter * _nt(c["st"].astype(BF16), c["q"])
    row = lax.broadcasted_iota(jnp.int32, (CHUNK, HEAD_DIM), 0)
    for c in chains:
        src_pos = (row if c["d"] == 0 else CHUNK - 1 - row).astype(F32)
        tail = jnp.exp((CHUNK - 1.0 - src_pos) * c["lg"]) * k_scale
        kw = (c["k"].astype(F32) * tail).astype(BF16)
        c["st_new"] = jnp.exp(CHUNK * c["lg"]) * c["st"] + jnp.dot(c["v_t"], kw, preferred_element_type=F32)
    for j, (_, _, o_ref) in enumerate(dirs):
        o_ref[...] = jnp.concatenate([c["o"] for c in chains[j * N_HEADS:(j + 1) * N_HEADS]], axis=0).T
    for c in chains:
        st_s[c["d"], c["hd"]] = c["st_new"]


def _ret_kernel(tab_ref, qkvf_ref, qkvb_ref, lg_ref, s0_ref, of_ref, ob_ref, sn_ref, st_s):
    i = pl.program_id(0)

    @pl.when(tab_ref[3, i] == 1)
    def _():
        _load_state_t(st_s, s0_ref.at[0])

    _ret_step([(0, qkvf_ref, of_ref), (1, qkvb_ref, ob_ref)], lg_ref, st_s)

    @pl.when(tab_ref[4, i] == 1)
    def _():
        _load_state_t(sn_ref.at[0], st_s)


def _scan_table(dims):
    rows = []
    seq = 0
    for base, nb, n in ((0, dims["bp"], dims["np"]), (dims["tp"], dims["bs"], dims["ns"])):
        nch = n // CHUNK
        for b in range(nb):
            blk0 = (base + b * n) // CHUNK
            for c in range(nch):
                rows.append((blk0 + c, blk0 + nch - 1 - c, seq, int(c == 0), int(c == nch - 1)))
            seq += 1
    return jnp.asarray(np.array(rows, dtype=np.int32).T)


def _mlstm_scan(tab, mqkv, tail, bias_row, c0, n0, m0):
    t_all = mqkv.shape[0]
    nseq = c0.shape[0]
    fwd = lambda i, tab: (tab[0, i], 0)
    bwd = lambda i, tab: (tab[1, i], 0)
    st = lambda nd: (lambda i, tab: (tab[2, i],) + (0,) * nd)
    grid_spec = pltpu.PrefetchScalarGridSpec(
        num_scalar_prefetch=1,
        grid=(tab.shape[1],),
        in_specs=[pl.BlockSpec((CHUNK, 768), fwd), pl.BlockSpec((CHUNK, LANES), fwd),
                  pl.BlockSpec((CHUNK, 768), bwd), pl.BlockSpec((CHUNK, LANES), bwd),
                  pl.BlockSpec((1, LANES), lambda i, tab: (0, 0)),
                  pl.BlockSpec((1,) + c0.shape[1:], st(4)),
                  pl.BlockSpec((1,) + n0.shape[1:], st(3)),
                  pl.BlockSpec((1,) + m0.shape[1:], st(2))],
        out_specs=[pl.BlockSpec((CHUNK, 256), fwd), pl.BlockSpec((CHUNK, 256), bwd),
                   pl.BlockSpec((1,) + c0.shape[1:], st(4)),
                   pl.BlockSpec((1,) + n0.shape[1:], st(3)),
                   pl.BlockSpec((1,) + m0.shape[1:], st(2))],
        scratch_shapes=[pltpu.VMEM(c0.shape[1:], F32), pltpu.VMEM(n0.shape[1:], F32),
                        pltpu.VMEM(m0.shape[1:], F32)],
    )
    return pl.pallas_call(
        _mlstm_kernel,
        grid_spec=grid_spec,
        out_shape=[jax.ShapeDtypeStruct((t_all, 256), F32), jax.ShapeDtypeStruct((t_all, 256), F32),
                   jax.ShapeDtypeStruct(c0.shape, F32), jax.ShapeDtypeStruct(n0.shape, F32),
                   jax.ShapeDtypeStruct(m0.shape, F32)],
        compiler_params=_cparams(("arbitrary",)),
        name="mlstm_scan",
    )(tab, mqkv, tail, mqkv, tail, bias_row, c0, n0, m0)


def _ret_scan(tab, rqkv, decay_logit, s0):
    t_all = rqkv.shape[0]
    fwd = lambda i, tab: (tab[0, i], 0)
    bwd = lambda i, tab: (tab[1, i], 0)
    st = lambda i, tab: (tab[2, i], 0, 0, 0, 0)
    grid_spec = pltpu.PrefetchScalarGridSpec(
        num_scalar_prefetch=1,
        grid=(tab.shape[1],),
        in_specs=[pl.BlockSpec((CHUNK, 768), fwd), pl.BlockSpec((CHUNK, 768), bwd),
                  pl.BlockSpec(decay_logit.shape, lambda i, tab: (0, 0)),
                  pl.BlockSpec((1,) + s0.shape[1:], st)],
        out_specs=[pl.BlockSpec((CHUNK, 256), fwd), pl.BlockSpec((CHUNK, 256), bwd),
                   pl.BlockSpec((1,) + s0.shape[1:], st)],
        scratch_shapes=[pltpu.VMEM(s0.shape[1:], F32)],
    )
    return pl.pallas_call(
        _ret_kernel,
        grid_spec=grid_spec,
        out_shape=[jax.ShapeDtypeStruct((t_all, 256), F32), jax.ShapeDtypeStruct((t_all, 256), F32),
                   jax.ShapeDtypeStruct(s0.shape, F32)],
        compiler_params=_cparams(("arbitrary",)),
        name="retention_scan",
    )(tab, rqkv, rqkv, decay_logit, s0)


def _proj_kernel(x_ref, w_ref, o_ref):
    o_ref[...] = _bdot(x_ref[...], w_ref[...]).astype(o_ref.dtype)


def _project(x, w, dtype):
    return pl.pallas_call(
        _proj_kernel,
        out_shape=jax.ShapeDtypeStruct((x.shape[0], w.shape[1]), dtype),
        name="ctx_kv_proj",
    )(x, w)


def _first_argmax_mask(cur, axis, size):
    io = lax.broadcasted_iota(jnp.int32, cur.shape, axis)
    mx = jnp.max(cur, axis=axis, keepdims=True)
    ix = jnp.min(jnp.where(cur == mx, io, size), axis=axis, keepdims=True)
    return io == ix


def _route(scores_t, bias_col):
    tm = scores_t.shape[1]
    per = N_EXPERTS // N_GROUPS
    sel = scores_t + bias_col
    s3 = sel.reshape(N_GROUPS, per, tm)
    hit1 = _first_argmax_mask(s3, 1, per)
    m1 = jnp.max(s3, axis=1, keepdims=True)
    m2 = jnp.max(jnp.where(hit1, -jnp.inf, s3), axis=1, keepdims=True)
    cur = m1 + m2
    gsel = None
    for _ in range(TOPK_GROUPS):
        hit = _first_argmax_mask(cur, 0, N_GROUPS)
        gsel = hit if gsel is None else jnp.logical_or(gsel, hit)
        cur = jnp.where(hit, -jnp.inf, cur)
    cur = jnp.where(gsel, s3, -jnp.inf).reshape(N_EXPERTS, tm)
    chosen, hits = None, []
    for _ in range(TOP_K):
        hit = _first_argmax_mask(cur, 0, N_EXPERTS)
        hits.append(hit)
        chosen = hit if chosen is None else jnp.logical_or(chosen, hit)
        cur = jnp.where(hit, -jnp.inf, cur)
    w = jnp.where(chosen, scores_t, 0.0)
    return w / jnp.sum(w, axis=0, keepdims=True) * ROUTED_SCALE, chosen, hits


def _out_kernel(oap_ref, oas_ref, odp_ref, ods_ref, hf_ref, hb_ref, mo_ref, of_ref, ob_ref, rg_ref,
                xp_ref, xs_ref, mod_ref, mg_ref, npost_ref, npre_ref, wout_ref, wrt_ref, rb_ref,
                x1_ref, h2_ref, hpk_ref, eidx_ref, rank_ref, wk_ref, cnt_ref, count_s, *, npt):
    i = pl.program_id(0)
    is_p = i < npt

    @pl.when(i == 0)
    def _():
        count_s[...] = jnp.zeros(count_s.shape, F32)

    a = jnp.where(is_p, oap_ref[...], oas_ref[...])
    dd = jnp.where(is_p, odp_ref[...], ods_ref[...])
    b = _rms_heads(hf_ref[...] + hb_ref[...]) * mg_ref[...] * _sigmoid(mo_ref[...])
    r = _rms_heads(of_ref[...] + ob_ref[...]) * _silu(rg_ref[...])
    mix = (jnp.dot(a.astype(BF16), wout_ref[0:256, :], preferred_element_type=F32)
           + jnp.dot(b.astype(BF16), wout_ref[256:512, :], preferred_element_type=F32)
           + jnp.dot(r.astype(BF16), wout_ref[512:768, :], preferred_element_type=F32)
           + jnp.dot(dd.astype(BF16), wout_ref[768:1024, :], preferred_element_type=F32))
    m = mod_ref[0]
    x1 = jnp.where(is_p, xp_ref[...], xs_ref[...]) + m[2:3] * (_rms(mix) * npost_ref[...])
    x1_ref[...] = x1
    h2 = (_rms(x1) * npre_ref[...] * (1.0 + m[4:5]) + m[3:4]).astype(BF16)
    h2_ref[...] = h2
    bits = pltpu.bitcast(h2.astype(F32), jnp.uint32)
    half = D_MODEL // 2
    hpk_ref[...] = (bits[:, :half] >> 16) | (bits[:, half:] & jnp.uint32(0xFFFF0000))

    logits_t = lax.dot_general(wrt_ref[...], h2, (((1,), (1,)), ((), ())), preferred_element_type=F32)
    w_t, chosen, hits = _route(_sigmoid(logits_t), rb_ref[...])
    tm = w_t.shape[1]
    src = lax.broadcasted_iota(jnp.int32, (tm, tm), 0)
    dst = lax.broadcasted_iota(jnp.int32, (tm, tm), 1)
    before = jnp.where(src < dst, 1.0, 0.0).astype(BF16)
    picked = jnp.where(chosen, 1.0, 0.0)
    rank = jnp.dot(picked.astype(BF16), before, preferred_element_type=F32) + count_s[...]
    count_s[...] += jnp.sum(picked, axis=1, keepdims=True)
    cnt_ref[...] = jnp.broadcast_to(count_s[...], cnt_ref.shape)
    e_io = lax.broadcasted_iota(jnp.int32, w_t.shape, 0)
    pick = lambda hit, val: jnp.sum(jnp.where(hit, val, jnp.zeros_like(val)), axis=0, keepdims=True)
    eidx_ref[...] = jnp.concatenate([pick(h, e_io) for h in hits], axis=0)
    rank_ref[...] = jnp.concatenate([pick(h, rank) for h in hits], axis=0).astype(jnp.int32)
    wk = jnp.concatenate([pick(h, w_t) for h in hits] + [jnp.zeros((LANES - TOP_K, tm), F32)], axis=0)
    wk_ref[...] = wk.T


def _output_stage(oa, od, hf, hb, mo, of, ob, rg, xp, xs, mod, mg, npost, npre, wout, wrt, rb, dims):
    t_all = xp.shape[0] + xs.shape[0]
    npt, nst, tps = dims["tp"] // TM, xs.shape[0] // TM, dims["ns"] // TM
    row = lambda i: (i, 0)
    const2 = lambda i: (0, 0)
    tok = lambda w: pl.BlockSpec((TM, w), row)
    full = lambda a: pl.BlockSpec(a.shape, const2)
    hp = pl.BlockSpec((TM, 256), lambda i: (jnp.minimum(i, npt - 1), 0))
    hs = pl.BlockSpec((TM, 256), lambda i: (jnp.clip(i - npt, 0, nst - 1), 0))
    return pl.pallas_call(
        functools.partial(_out_kernel, npt=npt),
        grid=(t_all // TM,),
        in_specs=[hp, hs, hp, hs, tok(256), tok(256), tok(256), tok(256), tok(256), tok(256)]
        + _pair_specs(npt, nst, TM, D_MODEL)
        + [pl.BlockSpec((1, 6, D_MODEL), lambda i: (_mod_row(i, npt, tps), 0, 0)),
           full(mg), full(npost), full(npre), full(wout), full(wrt), full(rb)],
        out_specs=[tok(D_MODEL), tok(D_MODEL), tok(D_MODEL // 2),
                   pl.BlockSpec((TOP_K, TM), lambda i: (0, i)), pl.BlockSpec((TOP_K, TM), lambda i: (0, i)),
                   tok(LANES), pl.BlockSpec((N_EXPERTS, LANES), const2)],
        out_shape=[jax.ShapeDtypeStruct((t_all, D_MODEL), F32),
                   jax.ShapeDtypeStruct((t_all, D_MODEL), BF16),
                   jax.ShapeDtypeStruct((t_all, D_MODEL // 2), jnp.uint32),
                   jax.ShapeDtypeStruct((TOP_K, t_all), jnp.int32),
                   jax.ShapeDtypeStruct((TOP_K, t_all), jnp.int32),
                   jax.ShapeDtypeStruct((t_all, LANES), F32),
                   jax.ShapeDtypeStruct((N_EXPERTS, LANES), F32)],
        scratch_shapes=[pltpu.VMEM((N_EXPERTS, 1), F32)],
        compiler_params=_cparams(("arbitrary",)),
        name="output_stage",
    )(oa[0], oa[1], od[0], od[1], hf, hb, mo, of, ob, rg, xp, xs, mod, mg, npost, npre, wout, wrt, rb)


SC_CORES, SC_SUBCORES = 2, 16
SC_WORKERS = SC_CORES * SC_SUBCORES
SLOT_BLOCK = 256
DISPATCH_ROWS = 64
COMBINE_ROWS = 32


def _sc_mesh():
    return plsc.VectorSubcoreMesh(core_axis_name="core", subcore_axis_name="subcore")


def _sc_worker():
    return lax.axis_index("subcore") * SC_CORES + lax.axis_index("core")


def _sc_dispatch(hpk, dest, n_slots):
    t_all, width = hpk.shape
    per_worker = t_all // SC_WORKERS

    @functools.partial(
        pl.kernel, mesh=_sc_mesh(), out_type=jax.ShapeDtypeStruct((n_slots, width), hpk.dtype),
        scratch_types=[pltpu.VMEM((DISPATCH_ROWS, width), hpk.dtype)]
        + [pltpu.VMEM((DISPATCH_ROWS,), jnp.int32)] * TOP_K + [pltpu.SemaphoreType.DMA])
    def dispatch(x_hbm, d_hbm, o_hbm, rows_v, *rest):
        idx, sem = rest[:TOP_K], rest[TOP_K]

        @pl.loop(0, per_worker // DISPATCH_ROWS)
        def _(j):
            base = _sc_worker() * per_worker + j * DISPATCH_ROWS
            pltpu.sync_copy(x_hbm.at[pl.ds(base, DISPATCH_ROWS)], rows_v)
            for k in range(TOP_K):
                pltpu.sync_copy(d_hbm.at[pl.ds(k * t_all + base, DISPATCH_ROWS)], idx[k])
            copies = [pltpu.async_copy(rows_v, o_hbm.at[idx[k]], sem) for k in range(TOP_K)]
            for c in copies:
                c.wait()

    return dispatch(hpk, dest)


def _sc_combine(yb, dest, t_all):
    width = yb.shape[1]
    per_worker = t_all // SC_WORKERS

    @functools.partial(
        pl.kernel, mesh=_sc_mesh(), out_type=jax.ShapeDtypeStruct((TOP_K, t_all, width), yb.dtype),
        scratch_types=[pltpu.VMEM((COMBINE_ROWS, width), yb.dtype)] * 2
        + [pltpu.VMEM((COMBINE_ROWS,), jnp.int32)] * TOP_K + [pltpu.SemaphoreType.DMA] * 2)
    def combine(y_hbm, d_hbm, o_hbm, rows_a, rows_b, *rest):
        idx, sems = rest[:TOP_K], rest[TOP_K:]
        bufs = (rows_a, rows_b)

        @pl.loop(0, per_worker // COMBINE_ROWS)
        def _(j):
            base = _sc_worker() * per_worker + j * COMBINE_ROWS
            for k in range(TOP_K):
                pltpu.sync_copy(d_hbm.at[pl.ds(k * t_all + base, COMBINE_ROWS)], idx[k])
            pending = pltpu.async_copy(y_hbm.at[idx[0]], bufs[0], sems[0])
            for k in range(TOP_K):
                pending.wait()
                if k + 1 < TOP_K:
                    pending = pltpu.async_copy(y_hbm.at[idx[k + 1]], bufs[(k + 1) % 2], sems[(k + 1) % 2])
                pltpu.sync_copy(bufs[k % 2], o_hbm.at[k, pl.ds(base, COMBINE_ROWS)])

    return combine(yb, dest)


def _unpack_rows(words):
    lo = pltpu.bitcast(words << 16, F32).astype(BF16)
    hi = pltpu.bitcast(words & jnp.uint32(0xFFFF0000), F32).astype(BF16)
    return lo, hi


def _expert_kernel(be_ref, used_ref, xs_ref, wg_ref, wu_ref, wd_ref, y_ref, wg_s, wu_s, wd_s):
    b = pl.program_id(0)
    fresh = jnp.logical_or(b == 0, be_ref[b] != be_ref[jnp.maximum(b - 1, 0)])

    @pl.when(jnp.logical_and(fresh, b < used_ref[0]))
    def _():
        wg_s[...] = wg_ref[0, 0].astype(BF16)
        wu_s[...] = wu_ref[0, 0].astype(BF16)
        wd_s[...] = wd_ref[0, 0].astype(BF16)

    @pl.when(b < used_ref[0])
    def _():
        lo, hi = _unpack_rows(xs_ref[...])
        half = D_MODEL // 2
        gate = (jnp.dot(lo, wg_s[:half, :], preferred_element_type=F32)
                + jnp.dot(hi, wg_s[half:, :], preferred_element_type=F32))
        up = (jnp.dot(lo, wu_s[:half, :], preferred_element_type=F32)
              + jnp.dot(hi, wu_s[half:, :], preferred_element_type=F32))
        y_ref[...] = jnp.dot((_silu(gate) * up).astype(BF16), wd_s[...], preferred_element_type=F32)


def _expert_blocks(layer, xs, block_expert, blocks_used, wg, wu, wdn):
    n_slots = xs.shape[0]
    wspec = lambda shape: pl.BlockSpec((1, 1) + shape, lambda b, be, used: (layer, be[b], 0, 0))
    grid_spec = pltpu.PrefetchScalarGridSpec(
        num_scalar_prefetch=2,
        grid=(n_slots // SLOT_BLOCK,),
        in_specs=[pl.BlockSpec((SLOT_BLOCK, D_MODEL // 2), lambda b, be, used: (b, 0)),
                  wspec((D_MODEL, D_EXPERT)), wspec((D_MODEL, D_EXPERT)), wspec((D_EXPERT, D_MODEL))],
        out_specs=pl.BlockSpec((SLOT_BLOCK, D_MODEL), lambda b, be, used: (b, 0)),
        scratch_shapes=[pltpu.VMEM((D_MODEL, D_EXPERT), BF16), pltpu.VMEM((D_MODEL, D_EXPERT), BF16),
                        pltpu.VMEM((D_EXPERT, D_MODEL), BF16)])
    return pl.pallas_call(
        _expert_kernel, grid_spec=grid_spec,
        out_shape=jax.ShapeDtypeStruct((n_slots, D_MODEL), F32),
        compiler_params=_cparams(("arbitrary",)),
        name="moe_experts",
    )(block_expert, blocks_used, xs, wg, wu, wdn)


def _moe_out_kernel(g_ref, wk_ref, h_ref, x1_ref, mod_ref, npost_ref, sg_ref, su_ref, sd_ref,
                    op_ref, os_ref, *, npt):
    h = h_ref[...]
    act = _silu(jnp.dot(h, sg_ref[...], preferred_element_type=F32)) * jnp.dot(h, su_ref[...],
                                                                              preferred_element_type=F32)
    moe = jnp.dot(act.astype(BF16), sd_ref[...], preferred_element_type=F32)
    wk = wk_ref[...]
    for k in range(TOP_K):
        moe = moe + g_ref[k] * wk[:, k:k + 1]
    m = mod_ref[0]
    res = x1_ref[...] + m[5:6] * (_rms(moe) * npost_ref[...])
    i = pl.program_id(0)

    @pl.when(i < npt)
    def _():
        op_ref[...] = res

    @pl.when(i >= npt)
    def _():
        os_ref[...] = res


def _moe_out_stage(g, wk, h2, x1, mod, npost, sg, su, sd, dims):
    t_all = h2.shape[0]
    npt, tps = dims["tp"] // TM, dims["ns"] // TM
    nst = t_all // TM - npt
    row = lambda i: (i, 0)
    full = lambda a: pl.BlockSpec(a.shape, lambda i: (0, 0))
    return pl.pallas_call(
        functools.partial(_moe_out_kernel, npt=npt),
        grid=(t_all // TM,),
        in_specs=[pl.BlockSpec((TOP_K, TM, D_MODEL), lambda i: (0, i, 0)), pl.BlockSpec((TM, LANES), row),
                  pl.BlockSpec((TM, D_MODEL), row), pl.BlockSpec((TM, D_MODEL), row),
                  pl.BlockSpec((1, 6, D_MODEL), lambda i: (_mod_row(i, npt, tps), 0, 0)),
                  full(npost), full(sg), full(su), full(sd)],
        out_specs=_pair_specs(npt, nst, TM, D_MODEL),
        out_shape=[jax.ShapeDtypeStruct((npt * TM, D_MODEL), F32), jax.ShapeDtypeStruct((nst * TM, D_MODEL), F32)],
        compiler_params=_cparams(("arbitrary",)),
        name="moe_combine",
    )(g, wk, h2, x1, mod, npost, sg, su, sd)


def _moe_stage(layer, h2, hpk, eidx, rank, wk, counts, x1, mod, npost, wg, wu, wdn, sg, su, sd, dims):
    t_all = h2.shape[0]
    n_blocks = -(-(t_all * TOP_K + N_EXPERTS * (SLOT_BLOCK - 1)) // SLOT_BLOCK)
    cnt = counts[:, 0].astype(jnp.int32)
    padded = (cnt + SLOT_BLOCK - 1) // SLOT_BLOCK * SLOT_BLOCK
    pad_end = jnp.cumsum(padded)
    pad_start = pad_end - padded
    experts = jnp.arange(N_EXPERTS, dtype=jnp.int32)[:, None, None]
    dest = rank + jnp.sum(jnp.where(eidx[None] == experts, pad_start[:, None, None], 0), axis=0)
    dest = dest.reshape(TOP_K * t_all)
    block_start = jnp.arange(n_blocks, dtype=jnp.int32) * SLOT_BLOCK
    block_expert = jnp.minimum(jnp.sum((pad_end[None, :] <= block_start[:, None]).astype(jnp.int32), axis=1),
                               N_EXPERTS - 1)
    blocks_used = (pad_end[-1:] // SLOT_BLOCK).astype(jnp.int32)

    xs = _sc_dispatch(hpk, dest, n_blocks * SLOT_BLOCK)
    yb = _expert_blocks(layer, xs, block_expert, blocks_used, wg, wu, wdn)
    g = _sc_combine(yb, dest, t_all)
    return _moe_out_stage(g, wk, h2, x1, mod, npost, sg, su, sd, dims)


_PERM32 = np.concatenate([np.arange(8, 16), np.arange(0, 8), np.arange(24, 32), np.arange(16, 24)])


def _rope_tables(dims):
    n = dims["ns"]
    pos = jnp.arange(n)
    quarter = QK_ROPE // 4
    inv = 1.0 / (ROPE_THETA ** (jnp.arange(quarter, dtype=F32) / quarter))
    ang_r = (pos // GRID_W).astype(F32)[:, None] * inv[None, :]
    ang_c = (pos % GRID_W).astype(F32)[:, None] * inv[None, :]
    cos32 = jnp.concatenate([jnp.cos(ang_r)] * 2 + [jnp.cos(ang_c)] * 2, axis=-1)
    sin32 = jnp.concatenate([-jnp.sin(ang_r), jnp.sin(ang_r), -jnp.sin(ang_c), jnp.sin(ang_c)], axis=-1)
    cos256, sin256 = jnp.tile(cos32, (1, 8)), jnp.tile(sin32, (1, 8))
    one, zero = jnp.ones((n, QK_NOPE), F32), jnp.zeros((n, QK_NOPE), F32)
    cosq = jnp.concatenate([one, cos32] * N_HEADS, axis=-1)
    sinq = jnp.concatenate([zero, sin32] * N_HEADS, axis=-1)
    return cos256, sin256, cosq, sinq


def _split_w_in(w_in_l):
    sizes = (256, 256, 256, 256, 256, 256, 256, 16, 256, 256, 256, 256, Q_LORA, KV_LORA, QK_ROPE)
    idx = np.cumsum(sizes)[:-1]
    return jnp.split(w_in_l, [int(v) for v in idx], axis=-1)


def _layer_weights(w_in_l, w_uq_l):
    (a_q, a_k, a_v, m_q, m_k, m_v, m_o, m_g, r_q, r_k, r_v, r_g, d_cq, d_ckv, d_kr) = _split_w_in(w_in_l)
    perm256 = np.concatenate([_PERM32 + 32 * j for j in range(8)])
    pad = lambda w, n: jnp.pad(w, ((0, 0), (0, n - w.shape[1])))
    tail = pad(jnp.concatenate([d_kr, m_g], axis=-1), LANES)
    tail_s = pad(d_kr[:, _PERM32], LANES)
    w_all = jnp.concatenate([a_q, a_k, a_v, m_q, m_k, m_v, m_o, r_q, r_k, r_v, r_g, d_cq, d_ckv, tail,
                             a_q[:, perm256], a_k[:, perm256], tail_s], axis=-1).astype(BF16)
    per = QK_NOPE + QK_ROPE
    permq = np.concatenate([np.concatenate([np.arange(QK_NOPE), QK_NOPE + _PERM32]) + per * j
                            for j in range(N_HEADS)])
    return w_all, w_uq_l.astype(BF16), w_uq_l[:, permq].astype(BF16)


def kernel(x_prompt, x_sample, cache_diff_k, cache_diff_v, state_mlstm_C, state_mlstm_n, state_mlstm_m, state_ret_S, cache_mla_ckv, cache_mla_krope, c, c_ctx, w_mod, b_mod, norm_pre, norm_post, w_in, w_out, diff_lambda, diff_norm, mlstm_gate_bias, mlstm_norm, ret_decay_logit, mla_q_norm, mla_w_uq, mla_kv_norm, mla_w_ukv, moe_w_router, moe_router_bias, moe_w_gate, moe_w_up, moe_w_down, shared_w_gate, shared_w_up, shared_w_down):
    bp, n_p, _ = x_prompt.shape
    bs, n_s, _ = x_sample.shape
    depth = w_in.shape[0]
    past = cache_diff_k.shape[2]
    dims = dict(bp=bp, np=n_p, bs=bs, ns=n_s, tp=bp * n_p, past=past)
    tp, ts = bp * n_p, bs * n_s
    assert n_p % TM == 0 and n_s % TM == 0 and n_p % CHUNK == 0 and n_s % min(TK, n_s) == 0
    assert past % min(TK, past) == 0 and tp % n_s == 0 and bs + 1 <= 8 and n_s % GRID_W == 0

    xp, xs = x_prompt.reshape(tp, D_MODEL), x_sample.reshape(ts, D_MODEL)
    cond = jnp.zeros((8, D_MODEL), F32).at[0].set(c_ctx).at[1:1 + bs].set(c)
    mod_all = _modulation(cond, w_mod, b_mod).reshape(depth, 8, 6, D_MODEL)
    tabs = _rope_tables(dims)
    scan_tab = _scan_table(dims)

    outs = [[] for _ in range(8)]
    for l in range(depth):
        lam_init = 0.8 - 0.6 * math.exp(-0.3 * l)
        mod = mod_all[l]
        w_all, wuq, wuqs = _layer_weights(w_in[l], mla_w_uq[l])
        wukv = mla_w_ukv[l].astype(BF16)
        (aq1, aq2, ak1t, ak2t, avh, ak, av, mqkv, mo, rqkv, rg, tail, qmla, ckv, kmlat, vmla) = _input_stage(
            xp, xs, mod, norm_pre[l, 0:1], w_all, tabs, mla_q_norm[l][None], wuq, wuqs,
            mla_kv_norm[l][None], wukv, dims)

        lamv, dg = diff_lambda[l], diff_norm[l][None]
        ckt = jnp.transpose(cache_diff_k[:, l], (2, 3, 0, 1)).reshape(N_HEADS, HEAD_DIM, bs * past).astype(BF16)
        cv = jnp.transpose(cache_diff_v[:, l], (2, 0, 1, 3)).reshape(N_HEADS, bs * past, HEAD_DIM).astype(BF16)
        oa_p = _attention("diff_attn_ctx", [aq1, aq2], [ak1t, ak2t], avh, None, None, [lamv, dg],
                          0, bp, n_p, 0, lam_init)
        oa_s = _attention("diff_attn_latent", [aq1, aq2], [ak1t, ak2t], avh,
                          [ckt[:, :DIFF_HALF], ckt[:, DIFF_HALF:]], cv, [lamv, dg], tp, bs, n_s, past, lam_init)

        ckv_ctx = cache_mla_ckv[:, l].reshape(bs * past, KV_LORA)
        kvc = _project(ckv_ctx, wukv, BF16).reshape(bs * past, N_HEADS, QK_NOPE + V_HEAD)
        krc = jnp.broadcast_to(cache_mla_krope[:, l].reshape(bs * past, 1, QK_ROPE).astype(BF16),
                               (bs * past, N_HEADS, QK_ROPE))
        kct = jnp.transpose(jnp.concatenate([kvc[..., :QK_NOPE], krc], axis=-1), (1, 2, 0))
        vc = jnp.transpose(kvc[..., QK_NOPE:], (1, 0, 2))
        od_p = _attention("mla_attn_ctx", [qmla], [kmlat], vmla, None, None, [], 0, bp, n_p, 0)
        od_s = _attention("mla_attn_latent", [qmla], [kmlat], vmla, [kct], vc, [], tp, bs, n_s, past)

        zeros = lambda a: jnp.zeros((bp,) + a.shape[1:], F32)
        st_c, st_n, st_m, st_s = state_mlstm_C[:, l], state_mlstm_n[:, l], state_mlstm_m[:, l], state_ret_S[:, l]
        bias_row = jnp.zeros((1, LANES), F32).at[0, TAIL_GATE0:TAIL_GATE0 + 16].set(mlstm_gate_bias[l].reshape(16))
        hf, hb, c_n, n_n, m_n = _mlstm_scan(scan_tab, mqkv, tail, bias_row,
                                            jnp.concatenate([zeros(st_c), st_c]),
                                            jnp.concatenate([zeros(st_n), st_n]),
                                            jnp.concatenate([zeros(st_m), st_m]))
        of, ob, s_n = _ret_scan(scan_tab, rqkv, ret_decay_logit[l], jnp.concatenate([zeros(st_s), st_s]))

        x1, h2, hpk, eidx, rank, wk, counts = _output_stage(
            (oa_p, oa_s), (od_p, od_s), hf, hb, mo, of, ob, rg, xp, xs, mod,
            mlstm_norm[l][None], norm_post[l, 0:1], norm_pre[l, 1:2],
            w_out[l].astype(BF16), moe_w_router[l].T.astype(BF16), moe_router_bias[l][:, None], dims)
        xp, xs = _moe_stage(l, h2, hpk, eidx, rank, wk, counts, x1, mod, norm_post[l, 1:2],
                            moe_w_gate, moe_w_up, moe_w_down, shared_w_gate[l].astype(BF16),
                            shared_w_up[l].astype(BF16), shared_w_down[l].astype(BF16), dims)

        outs[0].append(ak[:tp].reshape(bp, n_p, N_HEADS, HEAD_DIM))
        outs[1].append(av[:tp].reshape(bp, n_p, N_HEADS, HEAD_DIM))
        outs[2].append(c_n[:bp])
        outs[3].append(n_n[:bp])
        outs[4].append(m_n[:bp])
        outs[5].append(s_n[:bp])
        outs[6].append(ckv[:tp].reshape(bp, n_p, KV_LORA))
        outs[7].append(tail[:tp, :QK_ROPE].reshape(bp, n_p, QK_ROPE))

    return (xp.reshape(bp, n_p, D_MODEL), xs.reshape(bs, n_s, D_MODEL)) + tuple(
        jnp.stack(o, axis=1) for o in outs)
```

```python
import functools
import math

import numpy as np
import jax
import jax.numpy as jnp
from jax import lax
from jax.experimental import pallas as pl
from jax.experimental.pallas import tpu as pltpu
from jax.experimental.pallas import tpu_sc as plsc

F32 = jnp.float32
BF16 = jnp.bfloat16
HIGHEST = lax.Precision.HIGHEST

D_MODEL = 1024
GRID_W = 64
GROUP_WIDTH = 256
HEAD_DIM = 64
N_HEADS = 4
DIFF_HALF = 32
ROPE_THETA = 10000.0
Q_LORA = 256
KV_LORA = 128
QK_NOPE = 64
QK_ROPE = 32
V_HEAD = 64
N_EXPERTS = 64
TOP_K = 8
N_GROUPS = 8
TOPK_GROUPS = 4
D_EXPERT = 256
ROUTED_SCALE = 2.5
CHUNK = 128
EPS = 1e-6
Q_MLA = N_HEADS * (QK_NOPE + QK_ROPE)
KV_MLA = N_HEADS * (QK_NOPE + V_HEAD)

LANES = 128
VMEM_LIMIT = 56 * 1024 * 1024

TM = 256
TQ = 256
TK = 2048

C_AQ, C_AK, C_AV = 0, 256, 512
C_MQKV, C_MO = 768, 1536
C_RQKV, C_RG = 1792, 2560
C_CQ, C_CKV, C_TAIL = 2816, 3072, 3200
C_AQS, C_AKS, C_TAILS = 3328, 3584, 3840
W_ALL = 3968
TAIL_GATE0 = QK_ROPE


def _cparams(sem, flags=None):
    return pltpu.CompilerParams(dimension_semantics=sem, vmem_limit_bytes=VMEM_LIMIT, flags=flags)


def _rms(x):
    return x * lax.rsqrt(jnp.mean(x * x, axis=-1, keepdims=True) + EPS)


def _head_mean_matrix(width):
    r = lax.broadcasted_iota(jnp.int32, (width, width), 0) // HEAD_DIM
    c = lax.broadcasted_iota(jnp.int32, (width, width), 1) // HEAD_DIM
    return jnp.where(r == c, 1.0 / HEAD_DIM, 0.0).astype(F32)


def _rms_heads(x):
    ms = jnp.dot(x * x, _head_mean_matrix(x.shape[-1]), precision=HIGHEST, preferred_element_type=F32)
    return x * lax.rsqrt(ms + EPS)


def _sigmoid(x):
    return 1.0 / (1.0 + jnp.exp(-x))


def _silu(x):
    return x * _sigmoid(x)


def _log_sigmoid(x):
    return jnp.minimum(x, 0.0) - jnp.log1p(jnp.exp(-jnp.abs(x)))


def _bdot(a, b):
    return jnp.dot(a.astype(BF16), b.astype(BF16), preferred_element_type=F32)


def _mod_kernel(c_ref, w_ref, b_ref, o_ref):
    o_ref[0] = _bdot(_silu(c_ref[...]), w_ref[0]) + b_ref[0]


def _modulation(cond, w_mod, b_mod):
    depth, _, n = w_mod.shape
    tn = 1536
    return pl.pallas_call(
        _mod_kernel,
        grid=(depth, n // tn),
        in_specs=[pl.BlockSpec((8, D_MODEL), lambda l, j: (0, 0)),
                  pl.BlockSpec((1, D_MODEL, tn), lambda l, j: (l, 0, j)),
                  pl.BlockSpec((1, 1, tn), lambda l, j: (l, 0, j))],
        out_specs=pl.BlockSpec((1, 8, tn), lambda l, j: (l, 0, j)),
        out_shape=jax.ShapeDtypeStruct((depth, 8, n), F32),
        compiler_params=_cparams(("parallel", "parallel")),
        name="adaln_mod",
    )(cond, w_mod, b_mod.reshape(depth, 1, n))


def _in_kernel(xp_ref, xs_ref, mod_ref, npre_ref, w_ref, cos_ref, sin_ref, cosq_ref, sinq_ref,
               qg_ref, wuq_ref, wuqs_ref, kvg_ref, wukv_ref,
               aq1_ref, aq2_ref, ak1t_ref, ak2t_ref, avh_ref, ak_ref, av_ref,
               mqkv_ref, mo_ref, rqkv_ref, rg_ref, tail_ref,
               qmla_ref, ckv_ref, kmlat_ref, vmla_ref, *, npt):
    is_ctx = pl.program_id(0) < npt
    x = jnp.where(is_ctx, xp_ref[...], xs_ref[...])
    m = mod_ref[0]
    h = (_rms(x) * npre_ref[...] * (1.0 + m[1:2]) + m[0:1]).astype(BF16)

    def proj(c0, width):
        return jnp.dot(h, w_ref[:, c0:c0 + width], preferred_element_type=F32)

    cos = jnp.where(is_ctx, 1.0, cos_ref[...])
    sin = jnp.where(is_ctx, 0.0, sin_ref[...])
    aq = (proj(C_AQ, 256) * cos + proj(C_AQS, 256) * sin) * (DIFF_HALF ** -0.5)
    ak = proj(C_AK, 256) * cos + proj(C_AKS, 256) * sin
    av = proj(C_AV, 256)
    ak_ref[...] = ak
    av_ref[...] = av
    ak_t = ak.T.astype(BF16)
    for hd in range(N_HEADS):
        lo = hd * HEAD_DIM
        aq1_ref[hd] = aq[:, lo:lo + DIFF_HALF].astype(BF16)
        aq2_ref[hd] = aq[:, lo + DIFF_HALF:lo + HEAD_DIM].astype(BF16)
        ak1t_ref[hd] = ak_t[lo:lo + DIFF_HALF, :]
        ak2t_ref[hd] = ak_t[lo + DIFF_HALF:lo + HEAD_DIM, :]
        avh_ref[hd] = av[:, lo:lo + HEAD_DIM].astype(BF16)

    mqkv_ref[...] = proj(C_MQKV, 768).astype(BF16)
    mo_ref[...] = proj(C_MO, 256)
    rqkv_ref[...] = proj(C_RQKV, 768).astype(BF16)
    rg_ref[...] = proj(C_RG, 256)

    tail = proj(C_TAIL, LANES)
    tail_ref[...] = tail
    kr_t = (tail * cos[:, :LANES] + proj(C_TAILS, LANES) * sin[:, :LANES]).T[:QK_ROPE, :].astype(BF16)

    cqn = (_rms(proj(C_CQ, Q_LORA)) * qg_ref[...]).astype(BF16)
    q = jnp.dot(cqn, wuq_ref[...], preferred_element_type=F32)
    qs = jnp.dot(cqn, wuqs_ref[...], preferred_element_type=F32)
    cosq = jnp.where(is_ctx, 1.0, cosq_ref[...])
    sinq = jnp.where(is_ctx, 0.0, sinq_ref[...])
    qmla = (q * cosq + qs * sinq) * ((QK_NOPE + QK_ROPE) ** -0.5)
    for hd in range(N_HEADS):
        lo = hd * (QK_NOPE + QK_ROPE)
        qmla_ref[hd] = qmla[:, lo:lo + QK_NOPE + QK_ROPE].astype(BF16)
    ckvn = _rms(proj(C_CKV, KV_LORA)) * kvg_ref[...]
    ckv_ref[...] = ckvn
    kv = jnp.dot(ckvn.astype(BF16), wukv_ref[...], preferred_element_type=F32)
    kv_t = kv.T.astype(BF16)
    per = QK_NOPE + V_HEAD
    for hd in range(N_HEADS):
        kmlat_ref[hd, 0:QK_NOPE, :] = kv_t[hd * per:hd * per + QK_NOPE, :]
        kmlat_ref[hd, QK_NOPE:QK_NOPE + QK_ROPE, :] = kr_t
        vmla_ref[hd] = kv[:, hd * per + QK_NOPE:(hd + 1) * per].astype(BF16)


def _mod_row(i, npt, tps):
    return jnp.where(i < npt, 0, 1 + (i - npt) // tps)


def _pair_specs(npt, nst, tm, width):
    return [pl.BlockSpec((tm, width), lambda i, *_: (jnp.minimum(i, npt - 1), 0)),
            pl.BlockSpec((tm, width), lambda i, *_: (jnp.clip(i - npt, 0, nst - 1), 0))]


def _input_stage(xp, xs, mod, npre, w_all, tabs, qg, wuq, wuqs, kvg, wukv, dims):
    t_all = xp.shape[0] + xs.shape[0]
    npt, nst, tps = dims["tp"] // TM, xs.shape[0] // TM, dims["ns"] // TM
    row = lambda i: (i, 0)
    tab = lambda w: pl.BlockSpec((TM, w), lambda i: (jnp.maximum(i - npt, 0) % tps, 0))
    hrow = lambda i: (0, i, 0)
    const2 = lambda i: (0, 0)
    tok = lambda w: pl.BlockSpec((TM, w), row)
    headed = lambda w: pl.BlockSpec((N_HEADS, TM, w), hrow)
    headed_t = lambda d: pl.BlockSpec((N_HEADS, d, TM), lambda i: (0, 0, i))
    full = lambda a: pl.BlockSpec(a.shape, const2)
    cos, sin, cosq, sinq = tabs
    out_shapes = [
        (headed(DIFF_HALF), (N_HEADS, t_all, DIFF_HALF), BF16),
        (headed(DIFF_HALF), (N_HEADS, t_all, DIFF_HALF), BF16),
        (headed_t(DIFF_HALF), (N_HEADS, DIFF_HALF, t_all), BF16),
        (headed_t(DIFF_HALF), (N_HEADS, DIFF_HALF, t_all), BF16),
        (headed(HEAD_DIM), (N_HEADS, t_all, HEAD_DIM), BF16),
        (tok(256), (t_all, 256), F32),
        (tok(256), (t_all, 256), F32),
        (tok(768), (t_all, 768), BF16),
        (tok(256), (t_all, 256), F32),
        (tok(768), (t_all, 768), BF16),
        (tok(256), (t_all, 256), F32),
        (tok(LANES), (t_all, LANES), F32),
        (headed(QK_NOPE + QK_ROPE), (N_HEADS, t_all, QK_NOPE + QK_ROPE), BF16),
        (tok(KV_LORA), (t_all, KV_LORA), F32),
        (headed_t(QK_NOPE + QK_ROPE), (N_HEADS, QK_NOPE + QK_ROPE, t_all), BF16),
        (headed(V_HEAD), (N_HEADS, t_all, V_HEAD), BF16),
    ]
    return pl.pallas_call(
        functools.partial(_in_kernel, npt=npt),
        grid=(t_all // TM,),
        in_specs=_pair_specs(npt, nst, TM, D_MODEL) + [
            pl.BlockSpec((1, 6, D_MODEL), lambda i: (_mod_row(i, npt, tps), 0, 0)),
            full(npre), full(w_all), tab(256), tab(256), tab(Q_MLA), tab(Q_MLA),
            full(qg), full(wuq), full(wuqs), full(kvg), full(wukv)],
        out_specs=[s for s, _, _ in out_shapes],
        out_shape=[jax.ShapeDtypeStruct(shp, dt) for _, shp, dt in out_shapes],
        compiler_params=_cparams(("parallel",)),
        name="input_stage",
    )(xp, xs, mod, npre, w_all, cos, sin, cosq, sinq, qg, wuq, wuqs, kvg, wukv)


def _attn_kernel(*refs, n_soft, n_new, n_ctx, lam_init):
    refs = list(refs)
    q_refs, kt_refs, v_ref = refs[:n_soft], refs[n_soft:2 * n_soft], refs[2 * n_soft]
    pos = 2 * n_soft + 1
    if n_ctx:
        ckt_refs, cv_ref = refs[pos:pos + n_soft], refs[pos + n_soft]
        pos += n_soft + 1
    if n_soft == 2:
        lam_ref, g_ref = refs[pos:pos + 2]
        pos += 2
    o_ref = refs[pos]
    tq = o_ref.shape[0]
    nchain = n_soft * N_HEADS

    def chunk(state, kt_of, v_of):
        ms, ls, accs = state
        new_m, new_l, new_acc = list(ms), list(ls), list(accs)
        order = [(j, hd) for hd in range(N_HEADS) for j in range(n_soft)]
        ss = [jnp.dot(q_refs[j][hd], kt_of(j, hd), preferred_element_type=F32) for j, hd in order]
        ps, alphas = [], []
        for (j, hd), s in zip(order, ss):
            c = j * N_HEADS + hd
            m_new = jnp.maximum(ms[c], jnp.broadcast_to(jnp.max(s, axis=-1, keepdims=True), (tq, LANES)))
            p = jnp.exp(s - jnp.tile(m_new, (1, s.shape[1] // LANES)))
            alpha = jnp.exp(ms[c] - m_new)
            new_l[c] = alpha * ls[c] + jnp.broadcast_to(jnp.sum(p, axis=-1, keepdims=True), (tq, LANES))
            new_m[c] = m_new
            ps.append(p.astype(BF16))
            alphas.append(alpha)
        for (j, hd), p, alpha in zip(order, ps, alphas):
            c = j * N_HEADS + hd
            new_acc[c] = alpha[:, :V_HEAD] * accs[c] + jnp.dot(p, v_of(hd), preferred_element_type=F32)
        return tuple(new_m), tuple(new_l), tuple(new_acc)

    state = (tuple(jnp.full((tq, LANES), -jnp.inf, F32) for _ in range(nchain)),
             tuple(jnp.zeros((tq, LANES), F32) for _ in range(nchain)),
             tuple(jnp.zeros((tq, V_HEAD), F32) for _ in range(nchain)))
    if n_ctx:
        cstep = min(TK, n_ctx)
        for i in range(n_ctx // cstep):
            state = chunk(state, lambda j, hd, i=i: ckt_refs[j][hd, :, i * cstep:(i + 1) * cstep],
                          lambda hd, i=i: cv_ref[hd, i * cstep:(i + 1) * cstep, :])
    step = min(TK, n_new)

    def body(i, st):
        start = pl.multiple_of(i * step, step)
        return chunk(st, lambda j, hd: kt_refs[j][hd, :, pl.ds(start, step)],
                     lambda hd: v_ref[hd, pl.ds(start, step), :])

    _, ls, accs = lax.fori_loop(0, n_new // step, body, state)

    if n_soft == 2:
        lv = lam_ref[...]
        lam = (jnp.exp(jnp.sum(lv[0:1] * lv[1:2], axis=-1, keepdims=True))
               - jnp.exp(jnp.sum(lv[2:3] * lv[3:4], axis=-1, keepdims=True)) + lam_init)
    for hd in range(N_HEADS):
        out = accs[hd] / ls[hd][:, :V_HEAD]
        if n_soft == 2:
            a = out - lam * (accs[N_HEADS + hd] / ls[N_HEADS + hd][:, :V_HEAD])
            out = _rms(a) * g_ref[...] * (1.0 - lam_init)
        o_ref[:, hd * V_HEAD:(hd + 1) * V_HEAD] = out


def _attention(name, qs, kts, v, ctx_kts, ctx_v, extras, row0, nb, n, n_ctx, lam_init=0.0):
    tq = min(TQ, n)
    nqt = n // tq
    n_soft = len(qs)
    qmap = lambda b, i: (0, row0 // tq + b * nqt + i, 0)
    in_specs = [pl.BlockSpec((N_HEADS, tq, a.shape[-1]), qmap) for a in qs]
    in_specs += [pl.BlockSpec((N_HEADS, a.shape[1], n), lambda b, i: (0, 0, row0 // n + b)) for a in kts]
    in_specs += [pl.BlockSpec((N_HEADS, n, V_HEAD), lambda b, i: (0, row0 // n + b, 0))]
    args = list(qs) + list(kts) + [v]
    if n_ctx:
        in_specs += [pl.BlockSpec((N_HEADS, a.shape[1], n_ctx), lambda b, i: (0, 0, b)) for a in ctx_kts]
        in_specs += [pl.BlockSpec((N_HEADS, n_ctx, V_HEAD), lambda b, i: (0, b, 0))]
        args += list(ctx_kts) + [ctx_v]
    in_specs += [pl.BlockSpec(a.shape, lambda b, i: (0, 0)) for a in extras]
    args += list(extras)
    return pl.pallas_call(
        functools.partial(_attn_kernel, n_soft=n_soft, n_new=n, n_ctx=n_ctx, lam_init=lam_init),
        grid=(nb, nqt),
        in_specs=in_specs,
        out_specs=pl.BlockSpec((tq, N_HEADS * V_HEAD), lambda b, i: (b * nqt + i, 0)),
        out_shape=jax.ShapeDtypeStruct((nb * n, N_HEADS * V_HEAD), F32),
        compiler_params=_cparams(("parallel", "parallel")),
        name=name,
    )(*args)


def _tri(lower):
    r = lax.broadcasted_iota(jnp.int32, (CHUNK, CHUNK), 0)
    c = lax.broadcasted_iota(jnp.int32, (CHUNK, CHUNK), 1)
    return (c <= r) if lower else (c >= r)


def _nt(a, b):
    return lax.dot_general(a, b, (((1,), (1,)), ((), ())), preferred_element_type=F32)


def _tn(a, b):
    return lax.dot_general(a, b, (((0,), (0,)), ((), ())), preferred_element_type=F32)


def _head_slices(qkv, hd):
    lo = hd * HEAD_DIM
    return qkv[:, lo:lo + HEAD_DIM], qkv[:, 256 + lo:256 + lo + HEAD_DIM], qkv[:, 512 + lo:512 + lo + HEAD_DIM]


def _row_dot(row, mat, transpose_mat):
    row8 = jnp.broadcast_to(row, (8, row.shape[1])).astype(BF16)
    dims = (((1,), (1,)), ((), ())) if transpose_mat else (((1,), (0,)), ((), ()))
    return lax.dot_general(row8, mat, dims, preferred_element_type=F32)[0:1]


def _load_state_t(dst, src):
    for d in range(2):
        for hd in range(N_HEADS):
            dst[d, hd] = src[d, hd].T


def _mlstm_step(dirs, bias_ref, ct_s, n_s, m_s):
    k_scale = HEAD_DIM ** -0.5
    lower = _tri(True).astype(F32)
    upper = _tri(False).astype(F32)
    chains = []
    for d, qkv_ref, tail_ref, _ in dirs:
        g = tail_ref[...] + bias_ref[...]
        ls = _log_sigmoid(g)
        g_t, ls_t = g.T, ls.T
        left, right = (lower, upper) if d == 0 else (upper, lower)
        cum_col = jnp.dot(left, ls, precision=HIGHEST, preferred_element_type=F32)
        cum_row = jnp.dot(ls_t, right, precision=HIGHEST, preferred_element_type=F32)
        qkv = qkv_ref[...]
        for hd in range(N_HEADS):
            ci = TAIL_GATE0 + 4 * (2 * d) + hd
            cf = ci + 4
            q, k, v = _head_slices(qkv, hd)
            chains.append(dict(
                d=d, hd=hd, q=q, k=k, v=v, li_row=g_t[ci:ci + 1, :], b_row=cum_row[cf:cf + 1, :],
                c_col=g[:, ci:ci + 1] - cum_col[:, cf:cf + 1],
                m_prev=m_s[d:d + 1, hd:hd + 1], ct=ct_s[d, hd], n=n_s[d, hd:hd + 1, :]))
    for c in chains:
        c["s"] = _nt(c["k"], c["q"])
        c["v_t"] = c["v"].T
    for c in chains:
        valid = _tri(c["d"] != 0)
        log_d = jnp.where(valid, c["b_row"] + c["c_col"], -jnp.inf)
        log_inter = c["b_row"] + c["m_prev"]
        c["m_t"] = jnp.maximum(log_inter, jnp.max(log_d, axis=0, keepdims=True))
        c["w"] = c["s"] * k_scale * jnp.exp(log_d - c["m_t"])
        c["w_inter"] = jnp.exp(log_inter - c["m_t"])
    for c in chains:
        num = (jnp.dot(c["v_t"], c["w"].astype(BF16), preferred_element_type=F32)
               + c["w_inter"] * _nt(c["ct"].astype(BF16), c["q"]))
        den = jnp.sum(c["w"], axis=0, keepdims=True) + c["w_inter"] * _row_dot(c["n"], c["q"], True)
        c["h"] = num / jnp.maximum(jnp.abs(den), jnp.exp(-c["m_t"]))
    for c in chains:
        last = CHUNK - 1 if c["d"] == 0 else 0
        c["m_new"] = c["m_t"][:, last:last + 1]
        b_last = c["b_row"][:, last:last + 1]
        w_end_row = jnp.exp(b_last - c["b_row"] + c["li_row"] - c["m_new"])
        w_end_col = jnp.exp(c["c_col"] + (b_last - c["m_new"]))
        decay = jnp.exp(b_last + c["m_prev"] - c["m_new"])
        kw = (c["k"].astype(F32) * (w_end_col * k_scale)).astype(BF16)
        c["ct_new"] = decay * c["ct"] + jnp.dot(c["v_t"], kw, preferred_element_type=F32)
        c["n_new"] = decay * c["n"] + _row_dot(w_end_row, c["k"], False) * k_scale
    for j, (_, _, _, h_ref) in enumerate(dirs):
        h_ref[...] = jnp.concatenate([c["h"] for c in chains[j * N_HEADS:(j + 1) * N_HEADS]], axis=0).T
    for c in chains:
        d, hd = c["d"], c["hd"]
        ct_s[d, hd] = c["ct_new"]
        n_s[d, hd:hd + 1, :] = c["n_new"]
        m_s[d:d + 1, hd:hd + 1] = c["m_new"]


def _mlstm_kernel(tab_ref, qkvf_ref, tailf_ref, qkvb_ref, tailb_ref, bias_ref, c0_ref, n0_ref, m0_ref,
                  hf_ref, hb_ref, cn_ref, nn_ref, mn_ref, ct_s, n_s, m_s):
    i = pl.program_id(0)

    @pl.when(tab_ref[3, i] == 1)
    def _():
        _load_state_t(ct_s, c0_ref.at[0])
        n_s[...] = n0_ref[0]
        m_s[...] = m0_ref[0]

    _mlstm_step([(0, qkvf_ref, tailf_ref, hf_ref), (1, qkvb_ref, tailb_ref, hb_ref)], bias_ref, ct_s, n_s, m_s)

    @pl.when(tab_ref[4, i] == 1)
    def _():
        _load_state_t(cn_ref.at[0], ct_s)
        nn_ref[0] = n_s[...]
        mn_ref[0] = m_s[...]


def _ret_step(dirs, lg_ref, st_s):
    k_scale = HEAD_DIM ** -0.5
    s = lax.broadcasted_iota(jnp.int32, (CHUNK, CHUNK), 0)
    t = lax.broadcasted_iota(jnp.int32, (CHUNK, CHUNK), 1)
    lane = lax.broadcasted_iota(jnp.int32, (1, CHUNK), 1)
    chains = []
    for d, qkv_ref, _ in dirs:
        lag = (t - s) if d == 0 else (s - t)
        pos = (lane if d == 0 else CHUNK - 1 - lane).astype(F32)
        qkv = qkv_ref[...]
        for hd in range(N_HEADS):
            q, k, v = _head_slices(qkv, hd)
            chains.append(dict(d=d, hd=hd, q=q, k=k, v=v, lag=lag, pos=pos, st=st_s[d, hd],
                               lg=_log_sigmoid(lg_ref[d:d + 1, hd:hd + 1])))
    for c in chains:
        c["s"] = _nt(c["k"], c["q"])
        c["v_t"] = c["v"].T
    for c in chains:
        intra = jnp.where(c["lag"] >= 0, jnp.exp(jnp.maximum(c["lag"], 0).astype(F32) * c["lg"]), 0.0)
        c["a"] = (c["s"] * k_scale * intra).astype(BF16)
    for c in chains:
        inter = jnp.exp((c["pos"] + 1.0) * c["lg"])
        c["o"] = jnp.dot(c["v_t"], c["a"], preferred_element_type=F32) + inter * _nt(c["st"].astype(BF16), c["q"])
    row = lax.broadcasted_iota(jnp.int32, (CHUNK, HEAD_DIM), 0)
    for c in chains:
        src_pos = (row if c["d"] == 0 else CHUNK - 1 - row).astype(F32)
        tail = jnp.exp((CHUNK - 1.0 - src_pos) * c["lg"]) * k_scale
        kw = (c["k"].astype(F32) * tail).astype(BF16)
        c["st_new"] = jnp.exp(CHUNK * c["lg"]) * c["st"] + jnp.dot(c["v_t"], kw, preferred_element_type=F32)
    for j, (_, _, o_ref) in enumerate(dirs):
        o_ref[...] = jnp.concatenate([c["o"] for c in chains[j * N_HEADS:(j + 1) * N_HEADS]], axis=0).T
    for c in chains:
        st_s[c["d"], c["hd"]] = c["st_new"]


def _ret_kernel(tab_ref, qkvf_ref, qkvb_ref, lg_ref, s0_ref, of_ref, ob_ref, sn_ref, st_s):
    i = pl.program_id(0)

    @pl.when(tab_ref[3, i] == 1)
    def _():
        _load_state_t(st_s, s0_ref.at[0])

    _ret_step([(0, qkvf_ref, of_ref), (1, qkvb_ref, ob_ref)], lg_ref, st_s)

    @pl.when(tab_ref[4, i] == 1)
    def _():
        _load_state_t(sn_ref.at[0], st_s)


def _scan_table(dims):
    rows = []
    seq = 0
    for base, nb, n in ((0, dims["bp"], dims["np"]), (dims["tp"], dims["bs"], dims["ns"])):
        nch = n // CHUNK
        for b in range(nb):
            blk0 = (base + b * n) // CHUNK
            for c in range(nch):
                rows.append((blk0 + c, blk0 + nch - 1 - c, seq, int(c == 0), int(c == nch - 1)))
            seq += 1
    return jnp.asarray(np.array(rows, dtype=np.int32).T)


def _mlstm_scan(tab, mqkv, tail, bias_row, c0, n0, m0):
    t_all = mqkv.shape[0]
    nseq = c0.shape[0]
    fwd = lambda i, tab: (tab[0, i], 0)
    bwd = lambda i, tab: (tab[1, i], 0)
    st = lambda nd: (lambda i, tab: (tab[2, i],) + (0,) * nd)
    grid_spec = pltpu.PrefetchScalarGridSpec(
        num_scalar_prefetch=1,
        grid=(tab.shape[1],),
        in_specs=[pl.BlockSpec((CHUNK, 768), fwd), pl.BlockSpec((CHUNK, LANES), fwd),
                  pl.BlockSpec((CHUNK, 768), bwd), pl.BlockSpec((CHUNK, LANES), bwd),
                  pl.BlockSpec((1, LANES), lambda i, tab: (0, 0)),
                  pl.BlockSpec((1,) + c0.shape[1:], st(4)),
                  pl.BlockSpec((1,) + n0.shape[1:], st(3)),
                  pl.BlockSpec((1,) + m0.shape[1:], st(2))],
        out_specs=[pl.BlockSpec((CHUNK, 256), fwd), pl.BlockSpec((CHUNK, 256), bwd),
                   pl.BlockSpec((1,) + c0.shape[1:], st(4)),
                   pl.BlockSpec((1,) + n0.shape[1:], st(3)),
                   pl.BlockSpec((1,) + m0.shape[1:], st(2))],
        scratch_shapes=[pltpu.VMEM(c0.shape[1:], F32), pltpu.VMEM(n0.shape[1:], F32),
                        pltpu.VMEM(m0.shape[1:], F32)],
    )
    return pl.pallas_call(
        _mlstm_kernel,
        grid_spec=grid_spec,
        out_shape=[jax.ShapeDtypeStruct((t_all, 256), F32), jax.ShapeDtypeStruct((t_all, 256), F32),
                   jax.ShapeDtypeStruct(c0.shape, F32), jax.ShapeDtypeStruct(n0.shape, F32),
                   jax.ShapeDtypeStruct(m0.shape, F32)],
        compiler_params=_cparams(("arbitrary",)),
        name="mlstm_scan",
    )(tab, mqkv, tail, mqkv, tail, bias_row, c0, n0, m0)


def _ret_scan(tab, rqkv, decay_logit, s0):
    t_all = rqkv.shape[0]
    fwd = lambda i, tab: (tab[0, i], 0)
    bwd = lambda i, tab: (tab[1, i], 0)
    st = lambda i, tab: (tab[2, i], 0, 0, 0, 0)
    grid_spec = pltpu.PrefetchScalarGridSpec(
        num_scalar_prefetch=1,
        grid=(tab.shape[1],),
        in_specs=[pl.BlockSpec((CHUNK, 768), fwd), pl.BlockSpec((CHUNK, 768), bwd),
                  pl.BlockSpec(decay_logit.shape, lambda i, tab: (0, 0)),
                  pl.BlockSpec((1,) + s0.shape[1:], st)],
        out_specs=[pl.BlockSpec((CHUNK, 256), fwd), pl.BlockSpec((CHUNK, 256), bwd),
                   pl.BlockSpec((1,) + s0.shape[1:], st)],
        scratch_shapes=[pltpu.VMEM(s0.shape[1:], F32)],
    )
    return pl.pallas_call(
        _ret_kernel,
        grid_spec=grid_spec,
        out_shape=[jax.ShapeDtypeStruct((t_all, 256), F32), jax.ShapeDtypeStruct((t_all, 256), F32),
                   jax.ShapeDtypeStruct(s0.shape, F32)],
        compiler_params=_cparams(("arbitrary",)),
        name="retention_scan",
    )(tab, rqkv, rqkv, decay_logit, s0)


def _proj_kernel(x_ref, w_ref, o_ref):
    o_ref[...] = _bdot(x_ref[...], w_ref[...]).astype(o_ref.dtype)


def _project(x, w, dtype):
    return pl.pallas_call(
        _proj_kernel,
        out_shape=jax.ShapeDtypeStruct((x.shape[0], w.shape[1]), dtype),
        name="ctx_kv_proj",
    )(x, w)


def _first_argmax_mask(cur, axis, size):
    io = lax.broadcasted_iota(jnp.int32, cur.shape, axis)
    mx = jnp.max(cur, axis=axis, keepdims=True)
    ix = jnp.min(jnp.where(cur == mx, io, size), axis=axis, keepdims=True)
    return io == ix


def _route(scores_t, bias_col):
    tm = scores_t.shape[1]
    per = N_EXPERTS // N_GROUPS
    sel = scores_t + bias_col
    s3 = sel.reshape(N_GROUPS, per, tm)
    hit1 = _first_argmax_mask(s3, 1, per)
    m1 = jnp.max(s3, axis=1, keepdims=True)
    m2 = jnp.max(jnp.where(hit1, -jnp.inf, s3), axis=1, keepdims=True)
    cur = m1 + m2
    gsel = None
    for _ in range(TOPK_GROUPS):
        hit = _first_argmax_mask(cur, 0, N_GROUPS)
        gsel = hit if gsel is None else jnp.logical_or(gsel, hit)
        cur = jnp.where(hit, -jnp.inf, cur)
    cur = jnp.where(gsel, s3, -jnp.inf).reshape(N_EXPERTS, tm)
    chosen, hits = None, []
    for _ in range(TOP_K):
        hit = _first_argmax_mask(cur, 0, N_EXPERTS)
        hits.append(hit)
        chosen = hit if chosen is None else jnp.logical_or(chosen, hit)
        cur = jnp.where(hit, -jnp.inf, cur)
    w = jnp.where(chosen, scores_t, 0.0)
    return w / jnp.sum(w, axis=0, keepdims=True) * ROUTED_SCALE, chosen, hits


def _out_kernel(oap_ref, oas_ref, odp_ref, ods_ref, hf_ref, hb_ref, mo_ref, of_ref, ob_ref, rg_ref,
                xp_ref, xs_ref, mod_ref, mg_ref, npost_ref, npre_ref, wout_ref, wrt_ref, rb_ref,
                x1_ref, h2_ref, hpk_ref, eidx_ref, rank_ref, wk_ref, cnt_ref, count_s, *, npt):
    i = pl.program_id(0)
    is_p = i < npt

    @pl.when(i == 0)
    def _():
        count_s[...] = jnp.zeros(count_s.shape, F32)

    a = jnp.where(is_p, oap_ref[...], oas_ref[...])
    dd = jnp.where(is_p, odp_ref[...], ods_ref[...])
    b = _rms_heads(hf_ref[...] + hb_ref[...]) * mg_ref[...] * _sigmoid(mo_ref[...])
    r = _rms_heads(of_ref[...] + ob_ref[...]) * _silu(rg_ref[...])
    mix = (jnp.dot(a.astype(BF16), wout_ref[0:256, :], preferred_element_type=F32)
           + jnp.dot(b.astype(BF16), wout_ref[256:512, :], preferred_element_type=F32)
           + jnp.dot(r.astype(BF16), wout_ref[512:768, :], preferred_element_type=F32)
           + jnp.dot(dd.astype(BF16), wout_ref[768:1024, :], preferred_element_type=F32))
    m = mod_ref[0]
    x1 = jnp.where(is_p, xp_ref[...], xs_ref[...]) + m[2:3] * (_rms(mix) * npost_ref[...])
    x1_ref[...] = x1
    h2 = (_rms(x1) * npre_ref[...] * (1.0 + m[4:5]) + m[3:4]).astype(BF16)
    h2_ref[...] = h2
    bits = pltpu.bitcast(h2.astype(F32), jnp.uint32)
    half = D_MODEL // 2
    hpk_ref[...] = (bits[:, :half] >> 16) | (bits[:, half:] & jnp.uint32(0xFFFF0000))

    logits_t = lax.dot_general(wrt_ref[...], h2, (((1,), (1,)), ((), ())), preferred_element_type=F32)
    w_t, chosen, hits = _route(_sigmoid(logits_t), rb_ref[...])
    tm = w_t.shape[1]
    src = lax.broadcasted_iota(jnp.int32, (tm, tm), 0)
    dst = lax.broadcasted_iota(jnp.int32, (tm, tm), 1)
    before = jnp.where(src < dst, 1.0, 0.0).astype(BF16)
    picked = jnp.where(chosen, 1.0, 0.0)
    rank = jnp.dot(picked.astype(BF16), before, preferred_element_type=F32) + count_s[...]
    count_s[...] += jnp.sum(picked, axis=1, keepdims=True)
    cnt_ref[...] = jnp.broadcast_to(count_s[...], cnt_ref.shape)
    e_io = lax.broadcasted_iota(jnp.int32, w_t.shape, 0)
    pick = lambda hit, val: jnp.sum(jnp.where(hit, val, jnp.zeros_like(val)), axis=0, keepdims=True)
    eidx_ref[...] = jnp.concatenate([pick(h, e_io) for h in hits], axis=0)
    rank_ref[...] = jnp.concatenate([pick(h, rank) for h in hits], axis=0).astype(jnp.int32)
    wk = jnp.concatenate([pick(h, w_t) for h in hits] + [jnp.zeros((LANES - TOP_K, tm), F32)], axis=0)
    wk_ref[...] = wk.T


def _output_stage(oa, od, hf, hb, mo, of, ob, rg, xp, xs, mod, mg, npost, npre, wout, wrt, rb, dims):
    t_all = xp.shape[0] + xs.shape[0]
    npt, nst, tps = dims["tp"] // TM, xs.shape[0] // TM, dims["ns"] // TM
    row = lambda i: (i, 0)
    const2 = lambda i: (0, 0)
    tok = lambda w: pl.BlockSpec((TM, w), row)
    full = lambda a: pl.BlockSpec(a.shape, const2)
    hp = pl.BlockSpec((TM, 256), lambda i: (jnp.minimum(i, npt - 1), 0))
    hs = pl.BlockSpec((TM, 256), lambda i: (jnp.clip(i - npt, 0, nst - 1), 0))
    return pl.pallas_call(
        functools.partial(_out_kernel, npt=npt),
        grid=(t_all // TM,),
        in_specs=[hp, hs, hp, hs, tok(256), tok(256), tok(256), tok(256), tok(256), tok(256)]
        + _pair_specs(npt, nst, TM, D_MODEL)
        + [pl.BlockSpec((1, 6, D_MODEL), lambda i: (_mod_row(i, npt, tps), 0, 0)),
           full(mg), full(npost), full(npre), full(wout), full(wrt), full(rb)],
        out_specs=[tok(D_MODEL), tok(D_MODEL), tok(D_MODEL // 2),
                   pl.BlockSpec((TOP_K, TM), lambda i: (0, i)), pl.BlockSpec((TOP_K, TM), lambda i: (0, i)),
                   tok(LANES), pl.BlockSpec((N_EXPERTS, LANES), const2)],
        out_shape=[jax.ShapeDtypeStruct((t_all, D_MODEL), F32),
                   jax.ShapeDtypeStruct((t_all, D_MODEL), BF16),
                   jax.ShapeDtypeStruct((t_all, D_MODEL // 2), jnp.uint32),
                   jax.ShapeDtypeStruct((TOP_K, t_all), jnp.int32),
                   jax.ShapeDtypeStruct((TOP_K, t_all), jnp.int32),
                   jax.ShapeDtypeStruct((t_all, LANES), F32),
                   jax.ShapeDtypeStruct((N_EXPERTS, LANES), F32)],
        scratch_shapes=[pltpu.VMEM((N_EXPERTS, 1), F32)],
        compiler_params=_cparams(("arbitrary",)),
        name="output_stage",
    )(oa[0], oa[1], od[0], od[1], hf, hb, mo, of, ob, rg, xp, xs, mod, mg, npost, npre, wout, wrt, rb)


SC_CORES, SC_SUBCORES = 2, 16
SC_WORKERS = SC_CORES * SC_SUBCORES
SLOT_BLOCK = 512
DISPATCH_ROWS = 64
COMBINE_ROWS = 32


def _sc_mesh():
    return plsc.VectorSubcoreMesh(core_axis_name="core", subcore_axis_name="subcore")


def _sc_worker():
    return lax.axis_index("subcore") * SC_CORES + lax.axis_index("core")


def _sc_dispatch(hpk, dest, n_slots):
    t_all, width = hpk.shape
    per_worker = t_all // SC_WORKERS

    @functools.partial(
        pl.kernel, mesh=_sc_mesh(), out_type=jax.ShapeDtypeStruct((n_slots, width), hpk.dtype),
        scratch_types=[pltpu.VMEM((DISPATCH_ROWS, width), hpk.dtype)]
        + [pltpu.VMEM((DISPATCH_ROWS,), jnp.int32)] * TOP_K + [pltpu.SemaphoreType.DMA])
    def dispatch(x_hbm, d_hbm, o_hbm, rows_v, *rest):
        idx, sem = rest[:TOP_K], rest[TOP_K]

        @pl.loop(0, per_worker // DISPATCH_ROWS)
        def _(j):
            base = _sc_worker() * per_worker + j * DISPATCH_ROWS
            pltpu.sync_copy(x_hbm.at[pl.ds(base, DISPATCH_ROWS)], rows_v)
            for k in range(TOP_K):
                pltpu.sync_copy(d_hbm.at[pl.ds(k * t_all + base, DISPATCH_ROWS)], idx[k])
            copies = [pltpu.async_copy(rows_v, o_hbm.at[idx[k]], sem) for k in range(TOP_K)]
            for c in copies:
                c.wait()

    return dispatch(hpk, dest)


def _sc_combine(yb, dest, t_all):
    width = yb.shape[1]
    per_worker = t_all // SC_WORKERS

    @functools.partial(
        pl.kernel, mesh=_sc_mesh(), out_type=jax.ShapeDtypeStruct((TOP_K, t_all, width), yb.dtype),
        scratch_types=[pltpu.VMEM((COMBINE_ROWS, width), yb.dtype)] * 2
        + [pltpu.VMEM((COMBINE_ROWS,), jnp.int32)] * TOP_K + [pltpu.SemaphoreType.DMA] * 2)
    def combine(y_hbm, d_hbm, o_hbm, rows_a, rows_b, *rest):
        idx, sems = rest[:TOP_K], rest[TOP_K:]
        bufs = (rows_a, rows_b)

        @pl.loop(0, per_worker // COMBINE_ROWS)
        def _(j):
            base = _sc_worker() * per_worker + j * COMBINE_ROWS
            for k in range(TOP_K):
                pltpu.sync_copy(d_hbm.at[pl.ds(k * t_all + base, COMBINE_ROWS)], idx[k])
            pending = pltpu.async_copy(y_hbm.at[idx[0]], bufs[0], sems[0])
            for k in range(TOP_K):
                pending.wait()
                if k + 1 < TOP_K:
                    pending = pltpu.async_copy(y_hbm.at[idx[k + 1]], bufs[(k + 1) % 2], sems[(k + 1) % 2])
                pltpu.sync_copy(bufs[k % 2], o_hbm.at[k, pl.ds(base, COMBINE_ROWS)])

    return combine(yb, dest)


def _unpack_rows(words):
    lo = pltpu.bitcast(words << 16, F32).astype(BF16)
    hi = pltpu.bitcast(words & jnp.uint32(0xFFFF0000), F32).astype(BF16)
    return lo, hi


def _expert_kernel(be_ref, used_ref, xs_ref, wg_ref, wu_ref, wd_ref, y_ref, wg_s, wu_s, wd_s):
    b = pl.program_id(0)
    fresh = jnp.logical_or(b == 0, be_ref[b] != be_ref[jnp.maximum(b - 1, 0)])

    @pl.when(jnp.logical_and(fresh, b < used_ref[0]))
    def _():
        wg_s[...] = wg_ref[0, 0].astype(BF16)
        wu_s[...] = wu_ref[0, 0].astype(BF16)
        wd_s[...] = wd_ref[0, 0].astype(BF16)

    @pl.when(b < used_ref[0])
    def _():
        lo, hi = _unpack_rows(xs_ref[...])
        half = D_MODEL // 2
        gate = (jnp.dot(lo, wg_s[:half, :], preferred_element_type=F32)
                + jnp.dot(hi, wg_s[half:, :], preferred_element_type=F32))
        up = (jnp.dot(lo, wu_s[:half, :], preferred_element_type=F32)
              + jnp.dot(hi, wu_s[half:, :], preferred_element_type=F32))
        y_ref[...] = jnp.dot((_silu(gate) * up).astype(BF16), wd_s[...], preferred_element_type=F32)


def _expert_blocks(layer, xs, block_expert, blocks_used, wg, wu, wdn):
    n_slots = xs.shape[0]
    wspec = lambda shape: pl.BlockSpec((1, 1) + shape, lambda b, be, used: (layer, be[b], 0, 0))
    grid_spec = pltpu.PrefetchScalarGridSpec(
        num_scalar_prefetch=2,
        grid=(n_slots // SLOT_BLOCK,),
        in_specs=[pl.BlockSpec((SLOT_BLOCK, D_MODEL // 2), lambda b, be, used: (b, 0)),
                  wspec((D_MODEL, D_EXPERT)), wspec((D_MODEL, D_EXPERT)), wspec((D_EXPERT, D_MODEL))],
        out_specs=pl.BlockSpec((SLOT_BLOCK, D_MODEL), lambda b, be, used: (b, 0)),
        scratch_shapes=[pltpu.VMEM((D_MODEL, D_EXPERT), BF16), pltpu.VMEM((D_MODEL, D_EXPERT), BF16),
                        pltpu.VMEM((D_EXPERT, D_MODEL), BF16)])
    return pl.pallas_call(
        _expert_kernel, grid_spec=grid_spec,
        out_shape=jax.ShapeDtypeStruct((n_slots, D_MODEL), F32),
        compiler_params=_cparams(("arbitrary",)),
        name="moe_experts",
    )(block_expert, blocks_used, xs, wg, wu, wdn)


def _moe_out_kernel(g_ref, wk_ref, h_ref, x1_ref, mod_ref, npost_ref, sg_ref, su_ref, sd_ref,
                    op_ref, os_ref, *, npt):
    h = h_ref[...]
    act = _silu(jnp.dot(h, sg_ref[...], preferred_element_type=F32)) * jnp.dot(h, su_ref[...],
                                                                              preferred_element_type=F32)
    moe = jnp.dot(act.astype(BF16), sd_ref[...], preferred_element_type=F32)
    wk = wk_ref[...]
    for k in range(TOP_K):
        moe = moe + g_ref[k] * wk[:, k:k + 1]
    m = mod_ref[0]
    res = x1_ref[...] + m[5:6] * (_rms(moe) * npost_ref[...])
    i = pl.program_id(0)

    @pl.when(i < npt)
    def _():
        op_ref[...] = res

    @pl.when(i >= npt)
    def _():
        os_ref[...] = res


def _moe_out_stage(g, wk, h2, x1, mod, npost, sg, su, sd, dims):
    t_all = h2.shape[0]
    npt, tps = dims["tp"] // TM, dims["ns"] // TM
    nst = t_all // TM - npt
    row = lambda i: (i, 0)
    full = lambda a: pl.BlockSpec(a.shape, lambda i: (0, 0))
    return pl.pallas_call(
        functools.partial(_moe_out_kernel, npt=npt),
        grid=(t_all // TM,),
        in_specs=[pl.BlockSpec((TOP_K, TM, D_MODEL), lambda i: (0, i, 0)), pl.BlockSpec((TM, LANES), row),
                  pl.BlockSpec((TM, D_MODEL), row), pl.BlockSpec((TM, D_MODEL), row),
                  pl.BlockSpec((1, 6, D_MODEL), lambda i: (_mod_row(i, npt, tps), 0, 0)),
                  full(npost), full(sg), full(su), full(sd)],
        out_specs=_pair_specs(npt, nst, TM, D_MODEL),
        out_shape=[jax.ShapeDtypeStruct((npt * TM, D_MODEL), F32), jax.ShapeDtypeStruct((nst * TM, D_MODEL), F32)],
        compiler_params=_cparams(("arbitrary",)),
        name="moe_combine",
    )(g, wk, h2, x1, mod, npost, sg, su, sd)


def _moe_stage(layer, h2, hpk, eidx, rank, wk, counts, x1, mod, npost, wg, wu, wdn, sg, su, sd, dims):
    t_all = h2.shape[0]
    n_blocks = -(-(t_all * TOP_K + N_EXPERTS * (SLOT_BLOCK - 1)) // SLOT_BLOCK)
    cnt = counts[:, 0].astype(jnp.int32)
    padded = (cnt + SLOT_BLOCK - 1) // SLOT_BLOCK * SLOT_BLOCK
    pad_end = jnp.cumsum(padded)
    pad_start = pad_end - padded
    experts = jnp.arange(N_EXPERTS, dtype=jnp.int32)[:, None, None]
    dest = rank + jnp.sum(jnp.where(eidx[None] == experts, pad_start[:, None, None], 0), axis=0)
    dest = dest.reshape(TOP_K * t_all)
    block_start = jnp.arange(n_blocks, dtype=jnp.int32) * SLOT_BLOCK
    block_expert = jnp.minimum(jnp.sum((pad_end[None, :] <= block_start[:, None]).astype(jnp.int32), axis=1),
                               N_EXPERTS - 1)
    blocks_used = (pad_end[-1:] // SLOT_BLOCK).astype(jnp.int32)

    xs = _sc_dispatch(hpk, dest, n_blocks * SLOT_BLOCK)
    yb = _expert_blocks(layer, xs, block_expert, blocks_used, wg, wu, wdn)
    g = _sc_combine(yb, dest, t_all)
    return _moe_out_stage(g, wk, h2, x1, mod, npost, sg, su, sd, dims)


_PERM32 = np.concatenate([np.arange(8, 16), np.arange(0, 8), np.arange(24, 32), np.arange(16, 24)])


def _rope_tables(dims):
    n = dims["ns"]
    pos = jnp.arange(n)
    quarter = QK_ROPE // 4
    inv = 1.0 / (ROPE_THETA ** (jnp.arange(quarter, dtype=F32) / quarter))
    ang_r = (pos // GRID_W).astype(F32)[:, None] * inv[None, :]
    ang_c = (pos % GRID_W).astype(F32)[:, None] * inv[None, :]
    cos32 = jnp.concatenate([jnp.cos(ang_r)] * 2 + [jnp.cos(ang_c)] * 2, axis=-1)
    sin32 = jnp.concatenate([-jnp.sin(ang_r), jnp.sin(ang_r), -jnp.sin(ang_c), jnp.sin(ang_c)], axis=-1)
    cos256, sin256 = jnp.tile(cos32, (1, 8)), jnp.tile(sin32, (1, 8))
    one, zero = jnp.ones((n, QK_NOPE), F32), jnp.zeros((n, QK_NOPE), F32)
    cosq = jnp.concatenate([one, cos32] * N_HEADS, axis=-1)
    sinq = jnp.concatenate([zero, sin32] * N_HEADS, axis=-1)
    return cos256, sin256, cosq, sinq


def _split_w_in(w_in_l):
    sizes = (256, 256, 256, 256, 256, 256, 256, 16, 256, 256, 256, 256, Q_LORA, KV_LORA, QK_ROPE)
    idx = np.cumsum(sizes)[:-1]
    return jnp.split(w_in_l, [int(v) for v in idx], axis=-1)


def _layer_weights(w_in_l, w_uq_l):
    (a_q, a_k, a_v, m_q, m_k, m_v, m_o, m_g, r_q, r_k, r_v, r_g, d_cq, d_ckv, d_kr) = _split_w_in(w_in_l)
    perm256 = np.concatenate([_PERM32 + 32 * j for j in range(8)])
    pad = lambda w, n: jnp.pad(w, ((0, 0), (0, n - w.shape[1])))
    tail = pad(jnp.concatenate([d_kr, m_g], axis=-1), LANES)
    tail_s = pad(d_kr[:, _PERM32], LANES)
    w_all = jnp.concatenate([a_q, a_k, a_v, m_q, m_k, m_v, m_o, r_q, r_k, r_v, r_g, d_cq, d_ckv, tail,
                             a_q[:, perm256], a_k[:, perm256], tail_s], axis=-1).astype(BF16)
    per = QK_NOPE + QK_ROPE
    permq = np.concatenate([np.concatenate([np.arange(QK_NOPE), QK_NOPE + _PERM32]) + per * j
                            for j in range(N_HEADS)])
    return w_all, w_uq_l.astype(BF16), w_uq_l[:, permq].astype(BF16)


def kernel(x_prompt, x_sample, cache_diff_k, cache_diff_v, state_mlstm_C, state_mlstm_n, state_mlstm_m, state_ret_S, cache_mla_ckv, cache_mla_krope, c, c_ctx, w_mod, b_mod, norm_pre, norm_post, w_in, w_out, diff_lambda, diff_norm, mlstm_gate_bias, mlstm_norm, ret_decay_logit, mla_q_norm, mla_w_uq, mla_kv_norm, mla_w_ukv, moe_w_router, moe_router_bias, moe_w_gate, moe_w_up, moe_w_down, shared_w_gate, shared_w_up, shared_w_down):
    bp, n_p, _ = x_prompt.shape
    bs, n_s, _ = x_sample.shape
    depth = w_in.shape[0]
    past = cache_diff_k.shape[2]
    dims = dict(bp=bp, np=n_p, bs=bs, ns=n_s, tp=bp * n_p, past=past)
    tp, ts = bp * n_p, bs * n_s
    assert n_p % TM == 0 and n_s % TM == 0 and n_p % CHUNK == 0 and n_s % min(TK, n_s) == 0
    assert past % min(TK, past) == 0 and tp % n_s == 0 and bs + 1 <= 8 and n_s % GRID_W == 0

    xp, xs = x_prompt.reshape(tp, D_MODEL), x_sample.reshape(ts, D_MODEL)
    cond = jnp.zeros((8, D_MODEL), F32).at[0].set(c_ctx).at[1:1 + bs].set(c)
    mod_all = _modulation(cond, w_mod, b_mod).reshape(depth, 8, 6, D_MODEL)
    tabs = _rope_tables(dims)
    scan_tab = _scan_table(dims)

    outs = [[] for _ in range(8)]
    for l in range(depth):
        lam_init = 0.8 - 0.6 * math.exp(-0.3 * l)
        mod = mod_all[l]
        w_all, wuq, wuqs = _layer_weights(w_in[l], mla_w_uq[l])
        wukv = mla_w_ukv[l].astype(BF16)
        (aq1, aq2, ak1t, ak2t, avh, ak, av, mqkv, mo, rqkv, rg, tail, qmla, ckv, kmlat, vmla) = _input_stage(
            xp, xs, mod, norm_pre[l, 0:1], w_all, tabs, mla_q_norm[l][None], wuq, wuqs,
            mla_kv_norm[l][None], wukv, dims)

        lamv, dg = diff_lambda[l], diff_norm[l][None]
        ckt = jnp.transpose(cache_diff_k[:, l], (2, 3, 0, 1)).reshape(N_HEADS, HEAD_DIM, bs * past).astype(BF16)
        cv = jnp.transpose(cache_diff_v[:, l], (2, 0, 1, 3)).reshape(N_HEADS, bs * past, HEAD_DIM).astype(BF16)
        oa_p = _attention("diff_attn_ctx", [aq1, aq2], [ak1t, ak2t], avh, None, None, [lamv, dg],
                          0, bp, n_p, 0, lam_init)
        oa_s = _attention("diff_attn_latent", [aq1, aq2], [ak1t, ak2t], avh,
                          [ckt[:, :DIFF_HALF], ckt[:, DIFF_HALF:]], cv, [lamv, dg], tp, bs, n_s, past, lam_init)

        ckv_ctx = cache_mla_ckv[:, l].reshape(bs * past, KV_LORA)
        kvc = _project(ckv_ctx, wukv, BF16).reshape(bs * past, N_HEADS, QK_NOPE + V_HEAD)
        krc = jnp.broadcast_to(cache_mla_krope[:, l].reshape(bs * past, 1, QK_ROPE).astype(BF16),
                               (bs * past, N_HEADS, QK_ROPE))
        kct = jnp.transpose(jnp.concatenate([kvc[..., :QK_NOPE], krc], axis=-1), (1, 2, 0))
        vc = jnp.transpose(kvc[..., QK_NOPE:], (1, 0, 2))
        od_p = _attention("mla_attn_ctx", [qmla], [kmlat], vmla, None, None, [], 0, bp, n_p, 0)
        od_s = _attention("mla_attn_latent", [qmla], [kmlat], vmla, [kct], vc, [], tp, bs, n_s, past)

        zeros = lambda a: jnp.zeros((bp,) + a.shape[1:], F32)
        st_c, st_n, st_m, st_s = state_mlstm_C[:, l], state_mlstm_n[:, l], state_mlstm_m[:, l], state_ret_S[:, l]
        bias_row = jnp.zeros((1, LANES), F32).at[0, TAIL_GATE0:TAIL_GATE0 + 16].set(mlstm_gate_bias[l].reshape(16))
        hf, hb, c_n, n_n, m_n = _mlstm_scan(scan_tab, mqkv, tail, bias_row,
                                            jnp.concatenate([zeros(st_c), st_c]),
                                            jnp.concatenate([zeros(st_n), st_n]),
                                            jnp.concatenate([zeros(st_m), st_m]))
        of, ob, s_n = _ret_scan(scan_tab, rqkv, ret_decay_logit[l], jnp.concatenate([zeros(st_s), st_s]))

        x1, h2, hpk, eidx, rank, wk, counts = _output_stage(
            (oa_p, oa_s), (od_p, od_s), hf, hb, mo, of, ob, rg, xp, xs, mod,
            mlstm_norm[l][None], norm_post[l, 0:1], norm_pre[l, 1:2],
            w_out[l].astype(BF16), moe_w_router[l].T.astype(BF16), moe_router_bias[l][:, None], dims)
        xp, xs = _moe_stage(l, h2, hpk, eidx, rank, wk, counts, x1, mod, norm_post[l, 1:2],
                            moe_w_gate, moe_w_up, moe_w_down, shared_w_gate[l].astype(BF16),
                            shared_w_up[l].astype(BF16), shared_w_down[l].astype(BF16), dims)

        outs[0].append(ak[:tp].reshape(bp, n_p, N_HEADS, HEAD_DIM))
        outs[1].append(av[:tp].reshape(bp, n_p, N_HEADS, HEAD_DIM))
        outs[2].append(c_n[:bp])
        outs[3].append(n_n[:bp])
        outs[4].append(m_n[:bp])
        outs[5].append(s_n[:bp])
        outs[6].append(ckv[:tp].reshape(bp, n_p, KV_LORA))
        outs[7].append(tail[:tp, :QK_ROPE].reshape(bp, n_p, QK_ROPE))

    return (xp.reshape(bp, n_p, D_MODEL), xs.reshape(bs, n_s, D_MODEL)) + tuple(
        jnp.stack(o, axis=1) for o in outs)
```

```python
import functools
import math

import numpy as np
import jax
import jax.numpy as jnp
from jax import lax
from jax.experimental import pallas as pl
from jax.experimental.pallas import tpu as pltpu
from jax.experimental.pallas import tpu_sc as plsc

F32 = jnp.float32
BF16 = jnp.bfloat16
HIGHEST = lax.Precision.HIGHEST

D_MODEL = 1024
GRID_W = 64
GROUP_WIDTH = 256
HEAD_DIM = 64
N_HEADS = 4
DIFF_HALF = 32
ROPE_THETA = 10000.0
Q_LORA = 256
KV_LORA = 128
QK_NOPE = 64
QK_ROPE = 32
V_HEAD = 64
N_EXPERTS = 64
TOP_K = 8
N_GROUPS = 8
TOPK_GROUPS = 4
D_EXPERT = 256
ROUTED_SCALE = 2.5
CHUNK = 128
EPS = 1e-6
Q_MLA = N_HEADS * (QK_NOPE + QK_ROPE)
KV_MLA = N_HEADS * (QK_NOPE + V_HEAD)

LANES = 128
VMEM_LIMIT = 56 * 1024 * 1024

TM = 256
TQ = 256
TK = 2048

C_AQ, C_AK, C_AV = 0, 256, 512
C_MQKV, C_MO = 768, 1536
C_RQKV, C_RG = 1792, 2560
C_CQ, C_CKV, C_TAIL = 2816, 3072, 3200
C_AQS, C_AKS, C_TAILS = 3328, 3584, 3840
W_ALL = 3968
TAIL_GATE0 = QK_ROPE


def _cparams(sem, flags=None):
    return pltpu.CompilerParams(dimension_semantics=sem, vmem_limit_bytes=VMEM_LIMIT, flags=flags)


def _rms(x):
    return x * lax.rsqrt(jnp.mean(x * x, axis=-1, keepdims=True) + EPS)


def _head_mean_matrix(width):
    r = lax.broadcasted_iota(jnp.int32, (width, width), 0) // HEAD_DIM
    c = lax.broadcasted_iota(jnp.int32, (width, width), 1) // HEAD_DIM
    return jnp.where(r == c, 1.0 / HEAD_DIM, 0.0).astype(F32)


def _rms_heads(x):
    ms = jnp.dot(x * x, _head_mean_matrix(x.shape[-1]), precision=HIGHEST, preferred_element_type=F32)
    return x * lax.rsqrt(ms + EPS)


def _sigmoid(x):
    return 1.0 / (1.0 + jnp.exp(-x))


def _silu(x):
    return x * _sigmoid(x)


def _log_sigmoid(x):
    return jnp.minimum(x, 0.0) - jnp.log1p(jnp.exp(-jnp.abs(x)))


def _bdot(a, b):
    return jnp.dot(a.astype(BF16), b.astype(BF16), preferred_element_type=F32)


def _mod_kernel(c_ref, w_ref, b_ref, o_ref):
    o_ref[0] = _bdot(_silu(c_ref[...]), w_ref[0]) + b_ref[0]


def _modulation(cond, w_mod, b_mod):
    depth, _, n = w_mod.shape
    tn = 1536
    return pl.pallas_call(
        _mod_kernel,
        grid=(depth, n // tn),
        in_specs=[pl.BlockSpec((8, D_MODEL), lambda l, j: (0, 0)),
                  pl.BlockSpec((1, D_MODEL, tn), lambda l, j: (l, 0, j)),
                  pl.BlockSpec((1, 1, tn), lambda l, j: (l, 0, j))],
        out_specs=pl.BlockSpec((1, 8, tn), lambda l, j: (l, 0, j)),
        out_shape=jax.ShapeDtypeStruct((depth, 8, n), F32),
        compiler_params=_cparams(("parallel", "parallel")),
        name="adaln_mod",
    )(cond, w_mod, b_mod.reshape(depth, 1, n))


def _in_kernel(x_ref, mod_ref, npre_ref, w_ref, cos_ref, sin_ref, cosq_ref, sinq_ref,
               qg_ref, wuq_ref, wuqs_ref, kvg_ref, wukv_ref,
               aq1_ref, aq2_ref, ak1t_ref, ak2t_ref, avh_ref, ak_ref, av_ref,
               mqkv_ref, mo_ref, rqkv_ref, rg_ref, tail_ref,
               qmla_ref, ckv_ref, kmlat_ref, vmla_ref, *, latent):
    m = mod_ref[0]
    h = (_rms(x_ref[...]) * npre_ref[...] * (1.0 + m[1:2]) + m[0:1]).astype(BF16)

    def proj(c0, width):
        return jnp.dot(h, w_ref[:, c0:c0 + width], preferred_element_type=F32)

    def rotated(c0, c0_swapped, width, cos, sin):
        return proj(c0, width) * cos + proj(c0_swapped, width) * sin if latent else proj(c0, width)

    cos = cos_ref[...]
    sin = sin_ref[...]
    aq = rotated(C_AQ, C_AQS, 256, cos, sin) * (DIFF_HALF ** -0.5)
    ak = rotated(C_AK, C_AKS, 256, cos, sin)
    av = proj(C_AV, 256)
    ak_ref[...] = ak
    av_ref[...] = av
    ak_t = ak.T.astype(BF16)
    for hd in range(N_HEADS):
        lo = hd * HEAD_DIM
        aq1_ref[hd] = aq[:, lo:lo + DIFF_HALF].astype(BF16)
        aq2_ref[hd] = aq[:, lo + DIFF_HALF:lo + HEAD_DIM].astype(BF16)
        ak1t_ref[hd] = ak_t[lo:lo + DIFF_HALF, :]
        ak2t_ref[hd] = ak_t[lo + DIFF_HALF:lo + HEAD_DIM, :]
        avh_ref[hd] = av[:, lo:lo + HEAD_DIM].astype(BF16)

    mqkv_ref[...] = proj(C_MQKV, 768).astype(BF16)
    mo_ref[...] = proj(C_MO, 256)
    rqkv_ref[...] = proj(C_RQKV, 768).astype(BF16)
    rg_ref[...] = proj(C_RG, 256)

    tail = proj(C_TAIL, LANES)
    tail_ref[...] = tail
    kr = tail * cos[:, :LANES] + proj(C_TAILS, LANES) * sin[:, :LANES] if latent else tail
    kr_t = kr.T[:QK_ROPE, :].astype(BF16)

    cqn = (_rms(proj(C_CQ, Q_LORA)) * qg_ref[...]).astype(BF16)
    qmla = jnp.dot(cqn, wuq_ref[...], preferred_element_type=F32)
    if latent:
        qmla = qmla * cosq_ref[...] + jnp.dot(cqn, wuqs_ref[...], preferred_element_type=F32) * sinq_ref[...]
    qmla = qmla * ((QK_NOPE + QK_ROPE) ** -0.5)
    for hd in range(N_HEADS):
        lo = hd * (QK_NOPE + QK_ROPE)
        qmla_ref[hd] = qmla[:, lo:lo + QK_NOPE + QK_ROPE].astype(BF16)
    ckvn = _rms(proj(C_CKV, KV_LORA)) * kvg_ref[...]
    ckv_ref[...] = ckvn
    kv = jnp.dot(ckvn.astype(BF16), wukv_ref[...], preferred_element_type=F32)
    kv_t = kv.T.astype(BF16)
    per = QK_NOPE + V_HEAD
    for hd in range(N_HEADS):
        kmlat_ref[hd, 0:QK_NOPE, :] = kv_t[hd * per:hd * per + QK_NOPE, :]
        kmlat_ref[hd, QK_NOPE:QK_NOPE + QK_ROPE, :] = kr_t
        vmla_ref[hd] = kv[:, hd * per + QK_NOPE:(hd + 1) * per].astype(BF16)


def _mod_spec(st):
    tps = st["n"] // TM
    return pl.BlockSpec((1, 6, D_MODEL), lambda i, *_: (1 + i // tps if st["latent"] else 0, 0, 0))


def _input_stage(x, mod, npre, w_all, tabs, qg, wuq, wuqs, kvg, wukv, st):
    t_all = x.shape[0]
    tps = st["n"] // TM
    row = lambda i: (i, 0)
    tab = lambda w: pl.BlockSpec((TM, w), lambda i: (i % tps if st["latent"] else 0, 0))
    hrow = lambda i: (0, i, 0)
    const2 = lambda i: (0, 0)
    tok = lambda w: pl.BlockSpec((TM, w), row)
    headed = lambda w: pl.BlockSpec((N_HEADS, TM, w), hrow)
    headed_t = lambda d: pl.BlockSpec((N_HEADS, d, TM), lambda i: (0, 0, i))
    full = lambda a: pl.BlockSpec(a.shape, const2)
    cos, sin, cosq, sinq = tabs
    out_shapes = [
        (headed(DIFF_HALF), (N_HEADS, t_all, DIFF_HALF), BF16),
        (headed(DIFF_HALF), (N_HEADS, t_all, DIFF_HALF), BF16),
        (headed_t(DIFF_HALF), (N_HEADS, DIFF_HALF, t_all), BF16),
        (headed_t(DIFF_HALF), (N_HEADS, DIFF_HALF, t_all), BF16),
        (headed(HEAD_DIM), (N_HEADS, t_all, HEAD_DIM), BF16),
        (tok(256), (t_all, 256), F32),
        (tok(256), (t_all, 256), F32),
        (tok(768), (t_all, 768), BF16),
        (tok(256), (t_all, 256), F32),
        (tok(768), (t_all, 768), BF16),
        (tok(256), (t_all, 256), F32),
        (tok(LANES), (t_all, LANES), F32),
        (headed(QK_NOPE + QK_ROPE), (N_HEADS, t_all, QK_NOPE + QK_ROPE), BF16),
        (tok(KV_LORA), (t_all, KV_LORA), F32),
        (headed_t(QK_NOPE + QK_ROPE), (N_HEADS, QK_NOPE + QK_ROPE, t_all), BF16),
        (headed(V_HEAD), (N_HEADS, t_all, V_HEAD), BF16),
    ]
    return pl.pallas_call(
        functools.partial(_in_kernel, latent=st["latent"]),
        grid=(t_all // TM,),
        in_specs=[tok(D_MODEL), _mod_spec(st),
                  full(npre), full(w_all), tab(256), tab(256), tab(Q_MLA), tab(Q_MLA),
                  full(qg), full(wuq), full(wuqs), full(kvg), full(wukv)],
        out_specs=[s for s, _, _ in out_shapes],
        out_shape=[jax.ShapeDtypeStruct(shp, dt) for _, shp, dt in out_shapes],
        compiler_params=_cparams(("parallel",)),
        name="input_stage_" + st["name"],
    )(x, mod, npre, w_all, cos, sin, cosq, sinq, qg, wuq, wuqs, kvg, wukv)


def _attn_kernel(*refs, n_soft, n_new, n_ctx, lam_init):
    refs = list(refs)
    q_refs, kt_refs, v_ref = refs[:n_soft], refs[n_soft:2 * n_soft], refs[2 * n_soft]
    pos = 2 * n_soft + 1
    if n_ctx:
        ckt_refs, cv_ref = refs[pos:pos + n_soft], refs[pos + n_soft]
        pos += n_soft + 1
    if n_soft == 2:
        lam_ref, g_ref = refs[pos:pos + 2]
        pos += 2
    o_ref = refs[pos]
    tq = o_ref.shape[0]
    nchain = n_soft * N_HEADS

    def chunk(state, kt_of, v_of):
        ms, ls, accs = state
        new_m, new_l, new_acc = list(ms), list(ls), list(accs)
        order = [(j, hd) for hd in range(N_HEADS) for j in range(n_soft)]
        ss = [jnp.dot(q_refs[j][hd], kt_of(j, hd), preferred_element_type=F32) for j, hd in order]
        ps, alphas = [], []
        for (j, hd), s in zip(order, ss):
            c = j * N_HEADS + hd
            m_new = jnp.maximum(ms[c], jnp.broadcast_to(jnp.max(s, axis=-1, keepdims=True), (tq, LANES)))
            p = jnp.exp(s - jnp.tile(m_new, (1, s.shape[1] // LANES)))
            alpha = jnp.exp(ms[c] - m_new)
            new_l[c] = alpha * ls[c] + jnp.broadcast_to(jnp.sum(p, axis=-1, keepdims=True), (tq, LANES))
            new_m[c] = m_new
            ps.append(p.astype(BF16))
            alphas.append(alpha)
        for (j, hd), p, alpha in zip(order, ps, alphas):
            c = j * N_HEADS + hd
            new_acc[c] = alpha[:, :V_HEAD] * accs[c] + jnp.dot(p, v_of(hd), preferred_element_type=F32)
        return tuple(new_m), tuple(new_l), tuple(new_acc)

    state = (tuple(jnp.full((tq, LANES), -jnp.inf, F32) for _ in range(nchain)),
             tuple(jnp.zeros((tq, LANES), F32) for _ in range(nchain)),
             tuple(jnp.zeros((tq, V_HEAD), F32) for _ in range(nchain)))
    if n_ctx:
        cstep = min(TK, n_ctx)
        for i in range(n_ctx // cstep):
            state = chunk(state, lambda j, hd, i=i: ckt_refs[j][hd, :, i * cstep:(i + 1) * cstep],
                          lambda hd, i=i: cv_ref[hd, i * cstep:(i + 1) * cstep, :])
    step = min(TK, n_new)

    def body(i, st):
        start = pl.multiple_of(i * step, step)
        return chunk(st, lambda j, hd: kt_refs[j][hd, :, pl.ds(start, step)],
                     lambda hd: v_ref[hd, pl.ds(start, step), :])

    _, ls, accs = lax.fori_loop(0, n_new // step, body, state)

    if n_soft == 2:
        lv = lam_ref[...]
        lam = (jnp.exp(jnp.sum(lv[0:1] * lv[1:2], axis=-1, keepdims=True))
               - jnp.exp(jnp.sum(lv[2:3] * lv[3:4], axis=-1, keepdims=True)) + lam_init)
    for hd in range(N_HEADS):
        out = accs[hd] / ls[hd][:, :V_HEAD]
        if n_soft == 2:
            a = out - lam * (accs[N_HEADS + hd] / ls[N_HEADS + hd][:, :V_HEAD])
            out = _rms(a) * g_ref[...] * (1.0 - lam_init)
        o_ref[:, hd * V_HEAD:(hd + 1) * V_HEAD] = out


def _attention(name, qs, kts, v, ctx_kts, ctx_v, extras, row0, nb, n, n_ctx, lam_init=0.0):
    tq = min(TQ, n)
    nqt = n // tq
    n_soft = len(qs)
    qmap = lambda b, i: (0, row0 // tq + b * nqt + i, 0)
    in_specs = [pl.BlockSpec((N_HEADS, tq, a.shape[-1]), qmap) for a in qs]
    in_specs += [pl.BlockSpec((N_HEADS, a.shape[1], n), lambda b, i: (0, 0, row0 // n + b)) for a in kts]
    in_specs += [pl.BlockSpec((N_HEADS, n, V_HEAD), lambda b, i: (0, row0 // n + b, 0))]
    args = list(qs) + list(kts) + [v]
    if n_ctx:
        in_specs += [pl.BlockSpec((N_HEADS, a.shape[1], n_ctx), lambda b, i: (0, 0, b)) for a in ctx_kts]
        in_specs += [pl.BlockSpec((N_HEADS, n_ctx, V_HEAD), lambda b, i: (0, b, 0))]
        args += list(ctx_kts) + [ctx_v]
    in_specs += [pl.BlockSpec(a.shape, lambda b, i: (0, 0)) for a in extras]
    args += list(extras)
    return pl.pallas_call(
        functools.partial(_attn_kernel, n_soft=n_soft, n_new=n, n_ctx=n_ctx, lam_init=lam_init),
        grid=(nb, nqt),
        in_specs=in_specs,
        out_specs=pl.BlockSpec((tq, N_HEADS * V_HEAD), lambda b, i: (b * nqt + i, 0)),
        out_shape=jax.ShapeDtypeStruct((nb * n, N_HEADS * V_HEAD), F32),
        compiler_params=_cparams(("parallel", "parallel")),
        name=name,
    )(*args)


def _tri(lower):
    r = lax.broadcasted_iota(jnp.int32, (CHUNK, CHUNK), 0)
    c = lax.broadcasted_iota(jnp.int32, (CHUNK, CHUNK), 1)
    return (c <= r) if lower else (c >= r)


def _nt(a, b):
    return lax.dot_general(a, b, (((1,), (1,)), ((), ())), preferred_element_type=F32)


def _tn(a, b):
    return lax.dot_general(a, b, (((0,), (0,)), ((), ())), preferred_element_type=F32)


def _head_slices(qkv, hd):
    lo = hd * HEAD_DIM
    return qkv[:, lo:lo + HEAD_DIM], qkv[:, 256 + lo:256 + lo + HEAD_DIM], qkv[:, 512 + lo:512 + lo + HEAD_DIM]


def _row_dot(row, mat, transpose_mat):
    row8 = jnp.broadcast_to(row, (8, row.shape[1])).astype(BF16)
    dims = (((1,), (1,)), ((), ())) if transpose_mat else (((1,), (0,)), ((), ()))
    return lax.dot_general(row8, mat, dims, preferred_element_type=F32)[0:1]


def _load_state_t(dst, src):
    for d in range(2):
        for hd in range(N_HEADS):
            dst[d, hd] = src[d, hd].T


def _mlstm_step(dirs, bias_ref, ct_s, n_s, m_s):
    k_scale = HEAD_DIM ** -0.5
    lower = _tri(True).astype(F32)
    upper = _tri(False).astype(F32)
    chains = []
    for d, qkv_ref, tail_ref, _ in dirs:
        g = tail_ref[...] + bias_ref[...]
        ls = _log_sigmoid(g)
        g_t, ls_t = g.T, ls.T
        left, right = (lower, upper) if d == 0 else (upper, lower)
        cum_col = jnp.dot(left, ls, precision=HIGHEST, preferred_element_type=F32)
        cum_row = jnp.dot(ls_t, right, precision=HIGHEST, preferred_element_type=F32)
        qkv = qkv_ref[...]
        for hd in range(N_HEADS):
            ci = TAIL_GATE0 + 4 * (2 * d) + hd
            cf = ci + 4
            q, k, v = _head_slices(qkv, hd)
            chains.append(dict(
                d=d, hd=hd, q=q, k=k, v=v, li_row=g_t[ci:ci + 1, :], b_row=cum_row[cf:cf + 1, :],
                c_col=g[:, ci:ci + 1] - cum_col[:, cf:cf + 1],
                m_prev=m_s[d:d + 1, hd:hd + 1], ct=ct_s[d, hd], n=n_s[d, hd:hd + 1, :]))
    for c in chains:
        c["s"] = _nt(c["k"], c["q"])
        c["v_t"] = c["v"].T
    for c in chains:
        valid = _tri(c["d"] != 0)
        log_d = jnp.where(valid, c["b_row"] + c["c_col"], -jnp.inf)
        log_inter = c["b_row"] + c["m_prev"]
        c["m_t"] = jnp.maximum(log_inter, jnp.max(log_d, axis=0, keepdims=True))
        c["w"] = c["s"] * k_scale * jnp.exp(log_d - c["m_t"])
        c["w_inter"] = jnp.exp(log_inter - c["m_t"])
    for c in chains:
        num = (jnp.dot(c["v_t"], c["w"].astype(BF16), preferred_element_type=F32)
               + c["w_inter"] * _nt(c["ct"].astype(BF16), c["q"]))
        den = jnp.sum(c["w"], axis=0, keepdims=True) + c["w_inter"] * _row_dot(c["n"], c["q"], True)
        c["h"] = num / jnp.maximum(jnp.abs(den), jnp.exp(-c["m_t"]))
    for c in chains:
        last = CHUNK - 1 if c["d"] == 0 else 0
        c["m_new"] = c["m_t"][:, last:last + 1]
        b_last = c["b_row"][:, last:last + 1]
        w_end_row = jnp.exp(b_last - c["b_row"] + c["li_row"] - c["m_new"])
        w_end_col = jnp.exp(c["c_col"] + (b_last - c["m_new"]))
        decay = jnp.exp(b_last + c["m_prev"] - c["m_new"])
        kw = (c["k"].astype(F32) * (w_end_col * k_scale)).astype(BF16)
        c["ct_new"] = decay * c["ct"] + jnp.dot(c["v_t"], kw, preferred_element_type=F32)
        c["n_new"] = decay * c["n"] + _row_dot(w_end_row, c["k"], False) * k_scale
    for j, (_, _, _, h_ref) in enumerate(dirs):
        h_ref[...] = jnp.concatenate([c["h"] for c in chains[j * N_HEADS:(j + 1) * N_HEADS]], axis=0).T
    for c in chains:
        d, hd = c["d"], c["hd"]
        ct_s[d, hd] = c["ct_new"]
        n_s[d, hd:hd + 1, :] = c["n_new"]
        m_s[d:d + 1, hd:hd + 1] = c["m_new"]


def _mlstm_kernel(tab_ref, qkvf_ref, tailf_ref, qkvb_ref, tailb_ref, bias_ref, c0_ref, n0_ref, m0_ref,
                  hf_ref, hb_ref, cn_ref, nn_ref, mn_ref, ct_s, n_s, m_s):
    i = pl.program_id(0)

    @pl.when(tab_ref[3, i] == 1)
    def _():
        _load_state_t(ct_s, c0_ref.at[0])
        n_s[...] = n0_ref[0]
        m_s[...] = m0_ref[0]

    _mlstm_step([(0, qkvf_ref, tailf_ref, hf_ref), (1, qkvb_ref, tailb_ref, hb_ref)], bias_ref, ct_s, n_s, m_s)

    @pl.when(tab_ref[4, i] == 1)
    def _():
        _load_state_t(cn_ref.at[0], ct_s)
        nn_ref[0] = n_s[...]
        mn_ref[0] = m_s[...]


def _ret_step(dirs, lg_ref, st_s):
    k_scale = HEAD_DIM ** -0.5
    s = lax.broadcasted_iota(jnp.int32, (CHUNK, CHUNK), 0)
    t = lax.broadcasted_iota(jnp.int32, (CHUNK, CHUNK), 1)
    lane = lax.broadcasted_iota(jnp.int32, (1, CHUNK), 1)
    chains = []
    for d, qkv_ref, _ in dirs:
        lag = (t - s) if d == 0 else (s - t)
        pos = (lane if d == 0 else CHUNK - 1 - lane).astype(F32)
        qkv = qkv_ref[...]
        for hd in range(N_HEADS):
            q, k, v = _head_slices(qkv, hd)
            chains.append(dict(d=d, hd=hd, q=q, k=k, v=v, lag=lag, pos=pos, st=st_s[d, hd],
                               lg=_log_sigmoid(lg_ref[d:d + 1, hd:hd + 1])))
    for c in chains:
        c["s"] = _nt(c["k"], c["q"])
        c["v_t"] = c["v"].T
    for c in chains:
        intra = jnp.where(c["lag"] >= 0, jnp.exp(jnp.maximum(c["lag"], 0).astype(F32) * c["lg"]), 0.0)
        c["a"] = (c["s"] * k_scale * intra).astype(BF16)
    for c in chains:
        inter = jnp.exp((c["pos"] + 1.0) * c["lg"])
        c["o"] = jnp.dot(c["v_t"], c["a"], preferred_element_type=F32) + inter * _nt(c["st"].astype(BF16), c["q"])
    row = lax.broadcasted_iota(jnp.int32, (CHUNK, HEAD_DIM), 0)
    for c in chains:
        src_pos = (row if c["d"] == 0 else CHUNK - 1 - row).astype(F32)
        tail = jnp.exp((CHUNK - 1.0 - src_pos) * c["lg"]) * k_scale
        kw = (c["k"].astype(F32) * tail).astype(BF16)
        c["st_new"] = jnp.exp(CHUNK * c["lg"]) * c["st"] + jnp.dot(c["v_t"], kw, preferred_element_type=F32)
    for j, (_, _, o_ref) in enumerate(dirs):
        o_ref[...] = jnp.concatenate([c["o"] for c in chains[j * N_HEADS:(j + 1) * N_HEADS]], axis=0).T
    for c in chains:
        st_s[c["d"], c["hd"]] = c["st_new"]


def _ret_kernel(tab_ref, qkvf_ref, qkvb_ref, lg_ref, s0_ref, of_ref, ob_ref, sn_ref, st_s):
    i = pl.program_id(0)

    @pl.when(tab_ref[3, i] == 1)
    def _():
        _load_state_t(st_s, s0_ref.at[0])

    _ret_step([(0, qkvf_ref, of_ref), (1, qkvb_ref, ob_ref)], lg_ref, st_s)

    @pl.when(tab_ref[4, i] == 1)
    def _():
        _load_state_t(sn_ref.at[0], st_s)


def _scan_table(st):
    nch = st["n"] // CHUNK
    rows = [(b * nch + c, b * nch + nch - 1 - c, b, int(c == 0), int(c == nch - 1))
            for b in range(st["nb"]) for c in range(nch)]
    return jnp.asarray(np.array(rows, dtype=np.int32).T)


def _mlstm_scan(tab, mqkv, tail, bias_row, c0, n0, m0):
    t_all = mqkv.shape[0]
    nseq = c0.shape[0]
    fwd = lambda i, tab: (tab[0, i], 0)
    bwd = lambda i, tab: (tab[1, i], 0)
    st = lambda nd: (lambda i, tab: (tab[2, i],) + (0,) * nd)
    grid_spec = pltpu.PrefetchScalarGridSpec(
        num_scalar_prefetch=1,
        grid=(tab.shape[1],),
        in_specs=[pl.BlockSpec((CHUNK, 768), fwd), pl.BlockSpec((CHUNK, LANES), fwd),
                  pl.BlockSpec((CHUNK, 768), bwd), pl.BlockSpec((CHUNK, LANES), bwd),
                  pl.BlockSpec((1, LANES), lambda i, tab: (0, 0)),
                  pl.BlockSpec((1,) + c0.shape[1:], st(4)),
                  pl.BlockSpec((1,) + n0.shape[1:], st(3)),
                  pl.BlockSpec((1,) + m0.shape[1:], st(2))],
        out_specs=[pl.BlockSpec((CHUNK, 256), fwd), pl.BlockSpec((CHUNK, 256), bwd),
                   pl.BlockSpec((1,) + c0.shape[1:], st(4)),
                   pl.BlockSpec((1,) + n0.shape[1:], st(3)),
                   pl.BlockSpec((1,) + m0.shape[1:], st(2))],
        scratch_shapes=[pltpu.VMEM(c0.shape[1:], F32), pltpu.VMEM(n0.shape[1:], F32),
                        pltpu.VMEM(m0.shape[1:], F32)],
    )
    return pl.pallas_call(
        _mlstm_kernel,
        grid_spec=grid_spec,
        out_shape=[jax.ShapeDtypeStruct((t_all, 256), F32), jax.ShapeDtypeStruct((t_all, 256), F32),
                   jax.ShapeDtypeStruct(c0.shape, F32), jax.ShapeDtypeStruct(n0.shape, F32),
                   jax.ShapeDtypeStruct(m0.shape, F32)],
        compiler_params=_cparams(("arbitrary",)),
        name="mlstm_scan",
    )(tab, mqkv, tail, mqkv, tail, bias_row, c0, n0, m0)


def _ret_scan(tab, rqkv, decay_logit, s0):
    t_all = rqkv.shape[0]
    fwd = lambda i, tab: (tab[0, i], 0)
    bwd = lambda i, tab: (tab[1, i], 0)
    st = lambda i, tab: (tab[2, i], 0, 0, 0, 0)
    grid_spec = pltpu.PrefetchScalarGridSpec(
        num_scalar_prefetch=1,
        grid=(tab.shape[1],),
        in_specs=[pl.BlockSpec((CHUNK, 768), fwd), pl.BlockSpec((CHUNK, 768), bwd),
                  pl.BlockSpec(decay_logit.shape, lambda i, tab: (0, 0)),
                  pl.BlockSpec((1,) + s0.shape[1:], st)],
        out_specs=[pl.BlockSpec((CHUNK, 256), fwd), pl.BlockSpec((CHUNK, 256), bwd),
                   pl.BlockSpec((1,) + s0.shape[1:], st)],
        scratch_shapes=[pltpu.VMEM(s0.shape[1:], F32)],
    )
    return pl.pallas_call(
        _ret_kernel,
        grid_spec=grid_spec,
        out_shape=[jax.ShapeDtypeStruct((t_all, 256), F32), jax.ShapeDtypeStruct((t_all, 256), F32),
                   jax.ShapeDtypeStruct(s0.shape, F32)],
        compiler_params=_cparams(("arbitrary",)),
        name="retention_scan",
    )(tab, rqkv, rqkv, decay_logit, s0)


def _proj_kernel(x_ref, w_ref, o_ref):
    o_ref[...] = _bdot(x_ref[...], w_ref[...]).astype(o_ref.dtype)


def _project(x, w, dtype):
    return pl.pallas_call(
        _proj_kernel,
        out_shape=jax.ShapeDtypeStruct((x.shape[0], w.shape[1]), dtype),
        name="ctx_kv_proj",
    )(x, w)


def _first_argmax_mask(cur, axis, size):
    io = lax.broadcasted_iota(jnp.int32, cur.shape, axis)
    mx = jnp.max(cur, axis=axis, keepdims=True)
    ix = jnp.min(jnp.where(cur == mx, io, size), axis=axis, keepdims=True)
    return io == ix


def _route(scores_t, bias_col):
    tm = scores_t.shape[1]
    per = N_EXPERTS // N_GROUPS
    sel = scores_t + bias_col
    s3 = sel.reshape(N_GROUPS, per, tm)
    hit1 = _first_argmax_mask(s3, 1, per)
    m1 = jnp.max(s3, axis=1, keepdims=True)
    m2 = jnp.max(jnp.where(hit1, -jnp.inf, s3), axis=1, keepdims=True)
    cur = m1 + m2
    gsel = None
    for _ in range(TOPK_GROUPS):
        hit = _first_argmax_mask(cur, 0, N_GROUPS)
        gsel = hit if gsel is None else jnp.logical_or(gsel, hit)
        cur = jnp.where(hit, -jnp.inf, cur)
    cur = jnp.where(gsel, s3, -jnp.inf).reshape(N_EXPERTS, tm)
    chosen, hits = None, []
    for _ in range(TOP_K):
        hit = _first_argmax_mask(cur, 0, N_EXPERTS)
        hits.append(hit)
        chosen = hit if chosen is None else jnp.logical_or(chosen, hit)
        cur = jnp.where(hit, -jnp.inf, cur)
    w = jnp.where(chosen, scores_t, 0.0)
    return w / jnp.sum(w, axis=0, keepdims=True) * ROUTED_SCALE, chosen, hits


def _out_kernel(oa_ref, od_ref, hf_ref, hb_ref, mo_ref, of_ref, ob_ref, rg_ref,
                x_ref, mod_ref, mg_ref, npost_ref, npre_ref, wout_ref, wrt_ref, rb_ref,
                x1_ref, h2_ref, hpk_ref, eidx_ref, rank_ref, wk_ref, cnt_ref, count_s):
    @pl.when(pl.program_id(0) == 0)
    def _():
        count_s[...] = jnp.zeros(count_s.shape, F32)

    a = oa_ref[...]
    dd = od_ref[...]
    b = _rms_heads(hf_ref[...] + hb_ref[...]) * mg_ref[...] * _sigmoid(mo_ref[...])
    r = _rms_heads(of_ref[...] + ob_ref[...]) * _silu(rg_ref[...])
    mix = (jnp.dot(a.astype(BF16), wout_ref[0:256, :], preferred_element_type=F32)
           + jnp.dot(b.astype(BF16), wout_ref[256:512, :], preferred_element_type=F32)
           + jnp.dot(r.astype(BF16), wout_ref[512:768, :], preferred_element_type=F32)
           + jnp.dot(dd.astype(BF16), wout_ref[768:1024, :], preferred_element_type=F32))
    m = mod_ref[0]
    x1 = x_ref[...] + m[2:3] * (_rms(mix) * npost_ref[...])
    x1_ref[...] = x1
    h2 = (_rms(x1) * npre_ref[...] * (1.0 + m[4:5]) + m[3:4]).astype(BF16)
    h2_ref[...] = h2
    bits = pltpu.bitcast(h2.astype(F32), jnp.uint32)
    half = D_MODEL // 2
    hpk_ref[...] = (bits[:, :half] >> 16) | (bits[:, half:] & jnp.uint32(0xFFFF0000))

    logits_t = lax.dot_general(wrt_ref[...], h2, (((1,), (1,)), ((), ())), preferred_element_type=F32)
    w_t, chosen, hits = _route(_sigmoid(logits_t), rb_ref[...])
    tm = w_t.shape[1]
    src = lax.broadcasted_iota(jnp.int32, (tm, tm), 0)
    dst = lax.broadcasted_iota(jnp.int32, (tm, tm), 1)
    before = jnp.where(src < dst, 1.0, 0.0).astype(BF16)
    picked = jnp.where(chosen, 1.0, 0.0)
    rank = jnp.dot(picked.astype(BF16), before, preferred_element_type=F32) + count_s[...]
    count_s[...] += jnp.sum(picked, axis=1, keepdims=True)
    cnt_ref[...] = jnp.broadcast_to(count_s[...], cnt_ref.shape)
    e_io = lax.broadcasted_iota(jnp.int32, w_t.shape, 0)
    pick = lambda hit, val: jnp.sum(jnp.where(hit, val, jnp.zeros_like(val)), axis=0, keepdims=True)
    eidx_ref[...] = jnp.concatenate([pick(h, e_io) for h in hits], axis=0)
    rank_ref[...] = jnp.concatenate([pick(h, rank) for h in hits], axis=0).astype(jnp.int32)
    wk = jnp.concatenate([pick(h, w_t) for h in hits] + [jnp.zeros((LANES - TOP_K, tm), F32)], axis=0)
    wk_ref[...] = wk.T


def _output_stage(oa, od, hf, hb, mo, of, ob, rg, x, mod, mg, npost, npre, wout, wrt, rb, st):
    t_all = x.shape[0]
    row = lambda i: (i, 0)
    const2 = lambda i: (0, 0)
    tok = lambda w: pl.BlockSpec((TM, w), row)
    full = lambda a: pl.BlockSpec(a.shape, const2)
    return pl.pallas_call(
        _out_kernel,
        grid=(t_all // TM,),
        in_specs=[tok(256)] * 8 + [tok(D_MODEL), _mod_spec(st),
                                   full(mg), full(npost), full(npre), full(wout), full(wrt), full(rb)],
        out_specs=[tok(D_MODEL), tok(D_MODEL), tok(D_MODEL // 2),
                   pl.BlockSpec((TOP_K, TM), lambda i: (0, i)), pl.BlockSpec((TOP_K, TM), lambda i: (0, i)),
                   tok(LANES), pl.BlockSpec((N_EXPERTS, LANES), const2)],
        out_shape=[jax.ShapeDtypeStruct((t_all, D_MODEL), F32),
                   jax.ShapeDtypeStruct((t_all, D_MODEL), BF16),
                   jax.ShapeDtypeStruct((t_all, D_MODEL // 2), jnp.uint32),
                   jax.ShapeDtypeStruct((TOP_K, t_all), jnp.int32),
                   jax.ShapeDtypeStruct((TOP_K, t_all), jnp.int32),
                   jax.ShapeDtypeStruct((t_all, LANES), F32),
                   jax.ShapeDtypeStruct((N_EXPERTS, LANES), F32)],
        scratch_shapes=[pltpu.VMEM((N_EXPERTS, 1), F32)],
        compiler_params=_cparams(("arbitrary",)),
        name="output_stage_" + st["name"],
    )(oa, od, hf, hb, mo, of, ob, rg, x, mod, mg, npost, npre, wout, wrt, rb)


SC_CORES, SC_SUBCORES = 2, 16
SC_WORKERS = SC_CORES * SC_SUBCORES
SLOT_BLOCK = 512
DISPATCH_ROWS = 64
COMBINE_ROWS = 32


def _sc_mesh():
    return plsc.VectorSubcoreMesh(core_axis_name="core", subcore_axis_name="subcore")


def _sc_worker():
    return lax.axis_index("subcore") * SC_CORES + lax.axis_index("core")


def _sc_dispatch(hpk, dest, n_slots):
    t_all, width = hpk.shape
    per_worker = t_all // SC_WORKERS
    assert t_all % (SC_WORKERS * DISPATCH_ROWS) == 0

    @functools.partial(
        pl.kernel, mesh=_sc_mesh(), out_type=jax.ShapeDtypeStruct((n_slots, width), hpk.dtype),
        scratch_types=[pltpu.VMEM((DISPATCH_ROWS, width), hpk.dtype)]
        + [pltpu.VMEM((DISPATCH_ROWS,), jnp.int32)] * TOP_K + [pltpu.SemaphoreType.DMA])
    def dispatch(x_hbm, d_hbm, o_hbm, rows_v, *rest):
        idx, sem = rest[:TOP_K], rest[TOP_K]

        @pl.loop(0, per_worker // DISPATCH_ROWS)
        def _(j):
            base = _sc_worker() * per_worker + j * DISPATCH_ROWS
            pltpu.sync_copy(x_hbm.at[pl.ds(base, DISPATCH_ROWS)], rows_v)
            for k in range(TOP_K):
                pltpu.sync_copy(d_hbm.at[pl.ds(k * t_all + base, DISPATCH_ROWS)], idx[k])
            copies = [pltpu.async_copy(rows_v, o_hbm.at[idx[k]], sem) for k in range(TOP_K)]
            for c in copies:
                c.wait()

    return dispatch(hpk, dest)


def _sc_combine(yb, dest, t_all):
    width = yb.shape[1]
    per_worker = t_all // SC_WORKERS
    assert t_all % (SC_WORKERS * COMBINE_ROWS) == 0

    @functools.partial(
        pl.kernel, mesh=_sc_mesh(), out_type=jax.ShapeDtypeStruct((TOP_K, t_all, width), yb.dtype),
        scratch_types=[pltpu.VMEM((COMBINE_ROWS, width), yb.dtype)] * 2
        + [pltpu.VMEM((COMBINE_ROWS,), jnp.int32)] * TOP_K + [pltpu.SemaphoreType.DMA] * 2)
    def combine(y_hbm, d_hbm, o_hbm, rows_a, rows_b, *rest):
        idx, sems = rest[:TOP_K], rest[TOP_K:]
        bufs = (rows_a, rows_b)

        @pl.loop(0, per_worker // COMBINE_ROWS)
        def _(j):
            base = _sc_worker() * per_worker + j * COMBINE_ROWS
            for k in range(TOP_K):
                pltpu.sync_copy(d_hbm.at[pl.ds(k * t_all + base, COMBINE_ROWS)], idx[k])
            pending = pltpu.async_copy(y_hbm.at[idx[0]], bufs[0], sems[0])
            for k in range(TOP_K):
                pending.wait()
                if k + 1 < TOP_K:
                    pending = pltpu.async_copy(y_hbm.at[idx[k + 1]], bufs[(k + 1) % 2], sems[(k + 1) % 2])
                pltpu.sync_copy(bufs[k % 2], o_hbm.at[k, pl.ds(base, COMBINE_ROWS)])

    return combine(yb, dest)


def _unpack_rows(words):
    lo = pltpu.bitcast(words << 16, F32).astype(BF16)
    hi = pltpu.bitcast(words & jnp.uint32(0xFFFF0000), F32).astype(BF16)
    return lo, hi


def _expert_kernel(be_ref, used_ref, xs_ref, wg_ref, wu_ref, wd_ref, y_ref, wg_s, wu_s, wd_s):
    b = pl.program_id(0)
    fresh = jnp.logical_or(b == 0, be_ref[b] != be_ref[jnp.maximum(b - 1, 0)])

    @pl.when(jnp.logical_and(fresh, b < used_ref[0]))
    def _():
        wg_s[...] = wg_ref[0, 0].astype(BF16)
        wu_s[...] = wu_ref[0, 0].astype(BF16)
        wd_s[...] = wd_ref[0, 0].astype(BF16)

    @pl.when(b < used_ref[0])
    def _():
        lo, hi = _unpack_rows(xs_ref[...])
        half = D_MODEL // 2
        gate = (jnp.dot(lo, wg_s[:half, :], preferred_element_type=F32)
                + jnp.dot(hi, wg_s[half:, :], preferred_element_type=F32))
        up = (jnp.dot(lo, wu_s[:half, :], preferred_element_type=F32)
              + jnp.dot(hi, wu_s[half:, :], preferred_element_type=F32))
        y_ref[...] = jnp.dot((_silu(gate) * up).astype(BF16), wd_s[...], preferred_element_type=F32)


def _expert_blocks(layer, xs, block_expert, blocks_used, wg, wu, wdn):
    n_slots = xs.shape[0]
    wspec = lambda shape: pl.BlockSpec((1, 1) + shape, lambda b, be, used: (layer, be[b], 0, 0))
    grid_spec = pltpu.PrefetchScalarGridSpec(
        num_scalar_prefetch=2,
        grid=(n_slots // SLOT_BLOCK,),
        in_specs=[pl.BlockSpec((SLOT_BLOCK, D_MODEL // 2), lambda b, be, used: (b, 0)),
                  wspec((D_MODEL, D_EXPERT)), wspec((D_MODEL, D_EXPERT)), wspec((D_EXPERT, D_MODEL))],
        out_specs=pl.BlockSpec((SLOT_BLOCK, D_MODEL), lambda b, be, used: (b, 0)),
        scratch_shapes=[pltpu.VMEM((D_MODEL, D_EXPERT), BF16), pltpu.VMEM((D_MODEL, D_EXPERT), BF16),
                        pltpu.VMEM((D_EXPERT, D_MODEL), BF16)])
    return pl.pallas_call(
        _expert_kernel, grid_spec=grid_spec,
        out_shape=jax.ShapeDtypeStruct((n_slots, D_MODEL), F32),
        compiler_params=_cparams(("arbitrary",)),
        name="moe_experts",
    )(block_expert, blocks_used, xs, wg, wu, wdn)


def _moe_out_kernel(g_ref, wk_ref, h_ref, x1_ref, mod_ref, npost_ref, sg_ref, su_ref, sd_ref, o_ref):
    h = h_ref[...]
    act = _silu(jnp.dot(h, sg_ref[...], preferred_element_type=F32)) * jnp.dot(h, su_ref[...],
                                                                              preferred_element_type=F32)
    moe = jnp.dot(act.astype(BF16), sd_ref[...], preferred_element_type=F32)
    wk = wk_ref[...]
    for k in range(TOP_K):
        moe = moe + g_ref[k] * wk[:, k:k + 1]
    m = mod_ref[0]
    o_ref[...] = x1_ref[...] + m[5:6] * (_rms(moe) * npost_ref[...])


def _moe_out_stage(g, wk, h2, x1, mod, npost, sg, su, sd, st):
    t_all = h2.shape[0]
    row = lambda i: (i, 0)
    full = lambda a: pl.BlockSpec(a.shape, lambda i: (0, 0))
    return pl.pallas_call(
        _moe_out_kernel,
        grid=(t_all // TM,),
        in_specs=[pl.BlockSpec((TOP_K, TM, D_MODEL), lambda i: (0, i, 0)), pl.BlockSpec((TM, LANES), row),
                  pl.BlockSpec((TM, D_MODEL), row), pl.BlockSpec((TM, D_MODEL), row), _mod_spec(st),
                  full(npost), full(sg), full(su), full(sd)],
        out_specs=pl.BlockSpec((TM, D_MODEL), row),
        out_shape=jax.ShapeDtypeStruct((t_all, D_MODEL), F32),
        compiler_params=_cparams(("parallel",)),
        name="moe_combine_" + st["name"],
    )(g, wk, h2, x1, mod, npost, sg, su, sd)


def _moe_stage(layer, h2, hpk, eidx, rank, wk, counts, x1, mod, npost, wg, wu, wdn, sg, su, sd, st):
    t_all = h2.shape[0]
    n_blocks = -(-(t_all * TOP_K + N_EXPERTS * (SLOT_BLOCK - 1)) // SLOT_BLOCK)
    cnt = counts[:, 0].astype(jnp.int32)
    padded = (cnt + SLOT_BLOCK - 1) // SLOT_BLOCK * SLOT_BLOCK
    pad_end = jnp.cumsum(padded)
    pad_start = pad_end - padded
    experts = jnp.arange(N_EXPERTS, dtype=jnp.int32)[:, None, None]
    dest = rank + jnp.sum(jnp.where(eidx[None] == experts, pad_start[:, None, None], 0), axis=0)
    dest = dest.reshape(TOP_K * t_all)
    block_start = jnp.arange(n_blocks, dtype=jnp.int32) * SLOT_BLOCK
    block_expert = jnp.minimum(jnp.sum((pad_end[None, :] <= block_start[:, None]).astype(jnp.int32), axis=1),
                               N_EXPERTS - 1)
    blocks_used = (pad_end[-1:] // SLOT_BLOCK).astype(jnp.int32)

    xs = _sc_dispatch(hpk, dest, n_blocks * SLOT_BLOCK)
    yb = _expert_blocks(layer, xs, block_expert, blocks_used, wg, wu, wdn)
    g = _sc_combine(yb, dest, t_all)
    return _moe_out_stage(g, wk, h2, x1, mod, npost, sg, su, sd, st)


_PERM32 = np.concatenate([np.arange(8, 16), np.arange(0, 8), np.arange(24, 32), np.arange(16, 24)])


def _rope_tables(n):
    pos = jnp.arange(n)
    quarter = QK_ROPE // 4
    inv = 1.0 / (ROPE_THETA ** (jnp.arange(quarter, dtype=F32) / quarter))
    ang_r = (pos // GRID_W).astype(F32)[:, None] * inv[None, :]
    ang_c = (pos % GRID_W).astype(F32)[:, None] * inv[None, :]
    cos32 = jnp.concatenate([jnp.cos(ang_r)] * 2 + [jnp.cos(ang_c)] * 2, axis=-1)
    sin32 = jnp.concatenate([-jnp.sin(ang_r), jnp.sin(ang_r), -jnp.sin(ang_c), jnp.sin(ang_c)], axis=-1)
    cos256, sin256 = jnp.tile(cos32, (1, 8)), jnp.tile(sin32, (1, 8))
    one, zero = jnp.ones((n, QK_NOPE), F32), jnp.zeros((n, QK_NOPE), F32)
    cosq = jnp.concatenate([one, cos32] * N_HEADS, axis=-1)
    sinq = jnp.concatenate([zero, sin32] * N_HEADS, axis=-1)
    return cos256, sin256, cosq, sinq


def _split_w_in(w_in_l):
    sizes = (256, 256, 256, 256, 256, 256, 256, 16, 256, 256, 256, 256, Q_LORA, KV_LORA, QK_ROPE)
    idx = np.cumsum(sizes)[:-1]
    return jnp.split(w_in_l, [int(v) for v in idx], axis=-1)


def _layer_weights(w_in_l, w_uq_l):
    (a_q, a_k, a_v, m_q, m_k, m_v, m_o, m_g, r_q, r_k, r_v, r_g, d_cq, d_ckv, d_kr) = _split_w_in(w_in_l)
    perm256 = np.concatenate([_PERM32 + 32 * j for j in range(8)])
    pad = lambda w, n: jnp.pad(w, ((0, 0), (0, n - w.shape[1])))
    tail = pad(jnp.concatenate([d_kr, m_g], axis=-1), LANES)
    tail_s = pad(d_kr[:, _PERM32], LANES)
    w_all = jnp.concatenate([a_q, a_k, a_v, m_q, m_k, m_v, m_o, r_q, r_k, r_v, r_g, d_cq, d_ckv, tail,
                             a_q[:, perm256], a_k[:, perm256], tail_s], axis=-1).astype(BF16)
    per = QK_NOPE + QK_ROPE
    permq = np.concatenate([np.concatenate([np.arange(QK_NOPE), QK_NOPE + _PERM32]) + per * j
                            for j in range(N_HEADS)])
    return w_all, w_uq_l.astype(BF16), w_uq_l[:, permq].astype(BF16)


def kernel(x_prompt, x_sample, cache_diff_k, cache_diff_v, state_mlstm_C, state_mlstm_n, state_mlstm_m, state_ret_S, cache_mla_ckv, cache_mla_krope, c, c_ctx, w_mod, b_mod, norm_pre, norm_post, w_in, w_out, diff_lambda, diff_norm, mlstm_gate_bias, mlstm_norm, ret_decay_logit, mla_q_norm, mla_w_uq, mla_kv_norm, mla_w_ukv, moe_w_router, moe_router_bias, moe_w_gate, moe_w_up, moe_w_down, shared_w_gate, shared_w_up, shared_w_down):
    bp, n_p, _ = x_prompt.shape
    bs, n_s, _ = x_sample.shape
    depth = w_in.shape[0]
    past = cache_diff_k.shape[2]
    assert n_p % TM == 0 and n_s % TM == 0 and n_p % CHUNK == 0 and n_s % min(TK, n_s) == 0
    assert past % min(TK, past) == 0 and bs + 1 <= 8 and n_s % GRID_W == 0

    streams = (dict(name="ctx", nb=bp, n=n_p, latent=False), dict(name="latent", nb=bs, n=n_s, latent=True))
    xs_by_stream = [x_prompt.reshape(bp * n_p, D_MODEL), x_sample.reshape(bs * n_s, D_MODEL)]
    cond = jnp.zeros((8, D_MODEL), F32).at[0].set(c_ctx).at[1:1 + bs].set(c)
    mod_all = _modulation(cond, w_mod, b_mod).reshape(depth, 8, 6, D_MODEL)
    tabs = _rope_tables(n_s)
    scan_tabs = [_scan_table(st) for st in streams]

    outs = [[] for _ in range(8)]
    for l in range(depth):
        lam_init = 0.8 - 0.6 * math.exp(-0.3 * l)
        mod = mod_all[l]
        w_all, wuq, wuqs = _layer_weights(w_in[l], mla_w_uq[l])
        wukv = mla_w_ukv[l].astype(BF16)
        wout, wrt, rbias = w_out[l].astype(BF16), moe_w_router[l].T.astype(BF16), moe_router_bias[l][:, None]
        shared = [w[l].astype(BF16) for w in (shared_w_gate, shared_w_up, shared_w_down)]
        lamv, dg = diff_lambda[l], diff_norm[l][None]
        bias_row = jnp.zeros((1, LANES), F32).at[0, TAIL_GATE0:TAIL_GATE0 + 16].set(mlstm_gate_bias[l].reshape(16))

        ckt = jnp.transpose(cache_diff_k[:, l], (2, 3, 0, 1)).reshape(N_HEADS, HEAD_DIM, bs * past).astype(BF16)
        cv = jnp.transpose(cache_diff_v[:, l], (2, 0, 1, 3)).reshape(N_HEADS, bs * past, HEAD_DIM).astype(BF16)
        kvc = _project(cache_mla_ckv[:, l].reshape(bs * past, KV_LORA), wukv, BF16)
        kvc = kvc.reshape(bs * past, N_HEADS, QK_NOPE + V_HEAD)
        krc = jnp.broadcast_to(cache_mla_krope[:, l].reshape(bs * past, 1, QK_ROPE).astype(BF16),
                               (bs * past, N_HEADS, QK_ROPE))
        kct = jnp.transpose(jnp.concatenate([kvc[..., :QK_NOPE], krc], axis=-1), (1, 2, 0))
        vc = jnp.transpose(kvc[..., QK_NOPE:], (1, 0, 2))

        for si, st in enumerate(streams):
            x, nb, n = xs_by_stream[si], st["nb"], st["n"]
            (aq1, aq2, ak1t, ak2t, avh, ak, av, mqkv, mo, rqkv, rg, tail, qmla, ckv, kmlat, vmla) = _input_stage(
                x, mod, norm_pre[l, 0:1], w_all, tabs, mla_q_norm[l][None], wuq, wuqs,
                mla_kv_norm[l][None], wukv, st)
            if st["latent"]:
                oa = _attention("diff_attn_latent", [aq1, aq2], [ak1t, ak2t], avh,
                                [ckt[:, :DIFF_HALF], ckt[:, DIFF_HALF:]], cv, [lamv, dg], 0, nb, n, past, lam_init)
                od = _attention("mla_attn_latent", [qmla], [kmlat], vmla, [kct], vc, [], 0, nb, n, past)
                states = [s[:, l] for s in (state_mlstm_C, state_mlstm_n, state_mlstm_m, state_ret_S)]
            else:
                oa = _attention("diff_attn_ctx", [aq1, aq2], [ak1t, ak2t], avh, None, None, [lamv, dg],
                                0, nb, n, 0, lam_init)
                od = _attention("mla_attn_ctx", [qmla], [kmlat], vmla, None, None, [], 0, nb, n, 0)
                states = [jnp.zeros((nb,) + s.shape[2:], F32)
                          for s in (state_mlstm_C, state_mlstm_n, state_mlstm_m, state_ret_S)]
            hf, hb, c_n, n_n, m_n = _mlstm_scan(scan_tabs[si], mqkv, tail, bias_row, *states[:3])
            of, ob, s_n = _ret_scan(scan_tabs[si], rqkv, ret_decay_logit[l], states[3])

            x1, h2, hpk, eidx, rank, wk, counts = _output_stage(
                oa, od, hf, hb, mo, of, ob, rg, x, mod, mlstm_norm[l][None], norm_post[l, 0:1],
                norm_pre[l, 1:2], wout, wrt, rbias, st)
            xs_by_stream[si] = _moe_stage(l, h2, hpk, eidx, rank, wk, counts, x1, mod, norm_post[l, 1:2],
                                          moe_w_gate, moe_w_up, moe_w_down, *shared, st)
            if not st["latent"]:
                new = (ak.reshape(nb, n, N_HEADS, HEAD_DIM), av.reshape(nb, n, N_HEADS, HEAD_DIM), c_n, n_n, m_n,
                       s_n, ckv.reshape(nb, n, KV_LORA), tail[:, :QK_ROPE].reshape(nb, n, QK_ROPE))
                for o, v in zip(outs, new):
                    o.append(v)

    return (xs_by_stream[0].reshape(bp, n_p, D_MODEL), xs_by_stream[1].reshape(bs, n_s, D_MODEL)) + tuple(
        jnp.stack(o, axis=1) for o in outs)
```

```python
import functools
import math

import numpy as np
import jax
import jax.numpy as jnp
from jax import lax
from jax.experimental import pallas as pl
from jax.experimental.pallas import tpu as pltpu
from jax.experimental.pallas import tpu_sc as plsc

F32 = jnp.float32
BF16 = jnp.bfloat16
HIGHEST = lax.Precision.HIGHEST

D_MODEL = 1024
GRID_W = 64
GROUP_WIDTH = 256
HEAD_DIM = 64
N_HEADS = 4
DIFF_HALF = 32
ROPE_THETA = 10000.0
Q_LORA = 256
KV_LORA = 128
QK_NOPE = 64
QK_ROPE = 32
V_HEAD = 64
N_EXPERTS = 64
TOP_K = 8
N_GROUPS = 8
TOPK_GROUPS = 4
D_EXPERT = 256
ROUTED_SCALE = 2.5
CHUNK = 128
EPS = 1e-6
Q_MLA = N_HEADS * (QK_NOPE + QK_ROPE)
KV_MLA = N_HEADS * (QK_NOPE + V_HEAD)

LANES = 128
VMEM_LIMIT = 56 * 1024 * 1024

TM = 256
TQ = 256
TK = 2048

C_AQ, C_AK, C_AV = 0, 256, 512
C_MQKV, C_MO = 768, 1536
C_RQKV, C_RG = 1792, 2560
C_CQ, C_CKV, C_TAIL = 2816, 3072, 3200
C_AQS, C_AKS, C_TAILS = 3328, 3584, 3840
W_ALL = 3968
TAIL_GATE0 = QK_ROPE


def _cparams(sem, flags=None):
    return pltpu.CompilerParams(dimension_semantics=sem, vmem_limit_bytes=VMEM_LIMIT, flags=flags)


def _rms(x):
    return x * lax.rsqrt(jnp.mean(x * x, axis=-1, keepdims=True) + EPS)


def _head_mean_matrix(width):
    r = lax.broadcasted_iota(jnp.int32, (width, width), 0) // HEAD_DIM
    c = lax.broadcasted_iota(jnp.int32, (width, width), 1) // HEAD_DIM
    return jnp.where(r == c, 1.0 / HEAD_DIM, 0.0).astype(F32)


def _rms_heads(x):
    ms = jnp.dot(x * x, _head_mean_matrix(x.shape[-1]), precision=HIGHEST, preferred_element_type=F32)
    return x * lax.rsqrt(ms + EPS)


def _sigmoid(x):
    return 1.0 / (1.0 + jnp.exp(-x))


def _silu(x):
    return x * _sigmoid(x)


def _log_sigmoid(x):
    return jnp.minimum(x, 0.0) - jnp.log1p(jnp.exp(-jnp.abs(x)))


def _bdot(a, b):
    return jnp.dot(a.astype(BF16), b.astype(BF16), preferred_element_type=F32)


def _mod_kernel(c_ref, w_ref, b_ref, o_ref):
    o_ref[0] = _bdot(_silu(c_ref[...]), w_ref[0]) + b_ref[0]


def _modulation(cond, w_mod, b_mod):
    depth, _, n = w_mod.shape
    tn = 1536
    return pl.pallas_call(
        _mod_kernel,
        grid=(depth, n // tn),
        in_specs=[pl.BlockSpec((8, D_MODEL), lambda l, j: (0, 0)),
                  pl.BlockSpec((1, D_MODEL, tn), lambda l, j: (l, 0, j)),
                  pl.BlockSpec((1, 1, tn), lambda l, j: (l, 0, j))],
        out_specs=pl.BlockSpec((1, 8, tn), lambda l, j: (l, 0, j)),
        out_shape=jax.ShapeDtypeStruct((depth, 8, n), F32),
        compiler_params=_cparams(("parallel", "parallel")),
        name="adaln_mod",
    )(cond, w_mod, b_mod.reshape(depth, 1, n))


def _in_kernel(x_ref, mod_ref, npre_ref, w_ref, cos_ref, sin_ref, cosq_ref, sinq_ref,
               qg_ref, wuq_ref, wuqs_ref, kvg_ref, wukv_ref,
               aq1_ref, aq2_ref, ak1t_ref, ak2t_ref, avh_ref, ak_ref, av_ref,
               mqkv_ref, mo_ref, rqkv_ref, rg_ref, tail_ref,
               qmla_ref, ckv_ref, kmlat_ref, vmla_ref, *, latent):
    m = mod_ref[0]
    h = (_rms(x_ref[...]) * npre_ref[...] * (1.0 + m[1:2]) + m[0:1]).astype(BF16)

    def proj(c0, width):
        return jnp.dot(h, w_ref[:, c0:c0 + width], preferred_element_type=F32)

    def rotated(c0, c0_swapped, width, cos, sin):
        return proj(c0, width) * cos + proj(c0_swapped, width) * sin if latent else proj(c0, width)

    cos = cos_ref[...]
    sin = sin_ref[...]
    aq = rotated(C_AQ, C_AQS, 256, cos, sin) * (DIFF_HALF ** -0.5)
    ak = rotated(C_AK, C_AKS, 256, cos, sin)
    av = proj(C_AV, 256)
    ak_ref[...] = ak
    av_ref[...] = av
    ak_t = ak.T.astype(BF16)
    for hd in range(N_HEADS):
        lo = hd * HEAD_DIM
        aq1_ref[hd] = aq[:, lo:lo + DIFF_HALF].astype(BF16)
        aq2_ref[hd] = aq[:, lo + DIFF_HALF:lo + HEAD_DIM].astype(BF16)
        ak1t_ref[hd] = ak_t[lo:lo + DIFF_HALF, :]
        ak2t_ref[hd] = ak_t[lo + DIFF_HALF:lo + HEAD_DIM, :]
        avh_ref[hd] = av[:, lo:lo + HEAD_DIM].astype(BF16)

    mqkv_ref[...] = proj(C_MQKV, 768).astype(BF16)
    mo_ref[...] = proj(C_MO, 256)
    rqkv_ref[...] = proj(C_RQKV, 768).astype(BF16)
    rg_ref[...] = proj(C_RG, 256)

    tail = proj(C_TAIL, LANES)
    tail_ref[...] = tail
    kr = tail * cos[:, :LANES] + proj(C_TAILS, LANES) * sin[:, :LANES] if latent else tail
    kr_t = kr.T[:QK_ROPE, :].astype(BF16)

    cqn = (_rms(proj(C_CQ, Q_LORA)) * qg_ref[...]).astype(BF16)
    qmla = jnp.dot(cqn, wuq_ref[...], preferred_element_type=F32)
    if latent:
        qmla = qmla * cosq_ref[...] + jnp.dot(cqn, wuqs_ref[...], preferred_element_type=F32) * sinq_ref[...]
    qmla = qmla * ((QK_NOPE + QK_ROPE) ** -0.5)
    for hd in range(N_HEADS):
        lo = hd * (QK_NOPE + QK_ROPE)
        qmla_ref[hd] = qmla[:, lo:lo + QK_NOPE + QK_ROPE].astype(BF16)
    ckvn = _rms(proj(C_CKV, KV_LORA)) * kvg_ref[...]
    ckv_ref[...] = ckvn
    kv = jnp.dot(ckvn.astype(BF16), wukv_ref[...], preferred_element_type=F32)
    kv_t = kv.T.astype(BF16)
    per = QK_NOPE + V_HEAD
    for hd in range(N_HEADS):
        kmlat_ref[hd, 0:QK_NOPE, :] = kv_t[hd * per:hd * per + QK_NOPE, :]
        kmlat_ref[hd, QK_NOPE:QK_NOPE + QK_ROPE, :] = kr_t
        vmla_ref[hd] = kv[:, hd * per + QK_NOPE:(hd + 1) * per].astype(BF16)


def _mod_spec(st):
    tps = st["n"] // TM
    return pl.BlockSpec((1, 6, D_MODEL), lambda i, *_: (1 + i // tps if st["latent"] else 0, 0, 0))


def _input_stage(x, mod, npre, w_all, tabs, qg, wuq, wuqs, kvg, wukv, st):
    t_all = x.shape[0]
    tps = st["n"] // TM
    row = lambda i: (i, 0)
    tab = lambda w: pl.BlockSpec((TM, w), lambda i: (i % tps if st["latent"] else 0, 0))
    hrow = lambda i: (0, i, 0)
    const2 = lambda i: (0, 0)
    tok = lambda w: pl.BlockSpec((TM, w), row)
    headed = lambda w: pl.BlockSpec((N_HEADS, TM, w), hrow)
    headed_t = lambda d: pl.BlockSpec((N_HEADS, d, TM), lambda i: (0, 0, i))
    full = lambda a: pl.BlockSpec(a.shape, const2)
    cos, sin, cosq, sinq = tabs
    out_shapes = [
        (headed(DIFF_HALF), (N_HEADS, t_all, DIFF_HALF), BF16),
        (headed(DIFF_HALF), (N_HEADS, t_all, DIFF_HALF), BF16),
        (headed_t(DIFF_HALF), (N_HEADS, DIFF_HALF, t_all), BF16),
        (headed_t(DIFF_HALF), (N_HEADS, DIFF_HALF, t_all), BF16),
        (headed(HEAD_DIM), (N_HEADS, t_all, HEAD_DIM), BF16),
        (tok(256), (t_all, 256), F32),
        (tok(256), (t_all, 256), F32),
        (tok(768), (t_all, 768), BF16),
        (tok(256), (t_all, 256), F32),
        (tok(768), (t_all, 768), BF16),
        (tok(256), (t_all, 256), F32),
        (tok(LANES), (t_all, LANES), F32),
        (headed(QK_NOPE + QK_ROPE), (N_HEADS, t_all, QK_NOPE + QK_ROPE), BF16),
        (tok(KV_LORA), (t_all, KV_LORA), F32),
        (headed_t(QK_NOPE + QK_ROPE), (N_HEADS, QK_NOPE + QK_ROPE, t_all), BF16),
        (headed(V_HEAD), (N_HEADS, t_all, V_HEAD), BF16),
    ]
    return pl.pallas_call(
        functools.partial(_in_kernel, latent=st["latent"]),
        grid=(t_all // TM,),
        in_specs=[tok(D_MODEL), _mod_spec(st),
                  full(npre), full(w_all), tab(256), tab(256), tab(Q_MLA), tab(Q_MLA),
                  full(qg), full(wuq), full(wuqs), full(kvg), full(wukv)],
        out_specs=[s for s, _, _ in out_shapes],
        out_shape=[jax.ShapeDtypeStruct(shp, dt) for _, shp, dt in out_shapes],
        compiler_params=_cparams(("parallel",)),
        name="input_stage_" + st["name"],
    )(x, mod, npre, w_all, cos, sin, cosq, sinq, qg, wuq, wuqs, kvg, wukv)


def _attn_kernel(*refs, n_soft, n_new, n_ctx, lam_init):
    refs = list(refs)
    q_refs, kt_refs, v_ref = refs[:n_soft], refs[n_soft:2 * n_soft], refs[2 * n_soft]
    pos = 2 * n_soft + 1
    if n_ctx:
        ckt_refs, cv_ref = refs[pos:pos + n_soft], refs[pos + n_soft]
        pos += n_soft + 1
    if n_soft == 2:
        lam_ref, g_ref = refs[pos:pos + 2]
        pos += 2
    o_ref = refs[pos]
    tq = o_ref.shape[0]
    nchain = n_soft * N_HEADS

    def chunk(state, kt_of, v_of):
        ms, ls, accs = state
        new_m, new_l, new_acc = list(ms), list(ls), list(accs)
        order = [(j, hd) for hd in range(N_HEADS) for j in range(n_soft)]
        ss = [jnp.dot(q_refs[j][hd], kt_of(j, hd), preferred_element_type=F32) for j, hd in order]
        ps, alphas = [], []
        for (j, hd), s in zip(order, ss):
            c = j * N_HEADS + hd
            m_new = jnp.maximum(ms[c], jnp.broadcast_to(jnp.max(s, axis=-1, keepdims=True), (tq, LANES)))
            p = jnp.exp(s - jnp.tile(m_new, (1, s.shape[1] // LANES)))
            alpha = jnp.exp(ms[c] - m_new)
            new_l[c] = alpha * ls[c] + jnp.broadcast_to(jnp.sum(p, axis=-1, keepdims=True), (tq, LANES))
            new_m[c] = m_new
            ps.append(p.astype(BF16))
            alphas.append(alpha)
        for (j, hd), p, alpha in zip(order, ps, alphas):
            c = j * N_HEADS + hd
            new_acc[c] = alpha[:, :V_HEAD] * accs[c] + jnp.dot(p, v_of(hd), preferred_element_type=F32)
        return tuple(new_m), tuple(new_l), tuple(new_acc)

    state = (tuple(jnp.full((tq, LANES), -jnp.inf, F32) for _ in range(nchain)),
             tuple(jnp.zeros((tq, LANES), F32) for _ in range(nchain)),
             tuple(jnp.zeros((tq, V_HEAD), F32) for _ in range(nchain)))
    if n_ctx:
        cstep = min(TK, n_ctx)
        for i in range(n_ctx // cstep):
            state = chunk(state, lambda j, hd, i=i: ckt_refs[j][hd, :, i * cstep:(i + 1) * cstep],
                          lambda hd, i=i: cv_ref[hd, i * cstep:(i + 1) * cstep, :])
    step = min(TK, n_new)

    def body(i, st):
        start = pl.multiple_of(i * step, step)
        return chunk(st, lambda j, hd: kt_refs[j][hd, :, pl.ds(start, step)],
                     lambda hd: v_ref[hd, pl.ds(start, step), :])

    _, ls, accs = lax.fori_loop(0, n_new // step, body, state)

    if n_soft == 2:
        lv = lam_ref[...]
        lam = (jnp.exp(jnp.sum(lv[0:1] * lv[1:2], axis=-1, keepdims=True))
               - jnp.exp(jnp.sum(lv[2:3] * lv[3:4], axis=-1, keepdims=True)) + lam_init)
    for hd in range(N_HEADS):
        out = accs[hd] / ls[hd][:, :V_HEAD]
        if n_soft == 2:
            a = out - lam * (accs[N_HEADS + hd] / ls[N_HEADS + hd][:, :V_HEAD])
            out = _rms(a) * g_ref[...] * (1.0 - lam_init)
        o_ref[:, hd * V_HEAD:(hd + 1) * V_HEAD] = out


def _attention(name, qs, kts, v, ctx_kts, ctx_v, extras, row0, nb, n, n_ctx, lam_init=0.0):
    tq = min(TQ, n)
    nqt = n // tq
    n_soft = len(qs)
    qmap = lambda b, i: (0, row0 // tq + b * nqt + i, 0)
    in_specs = [pl.BlockSpec((N_HEADS, tq, a.shape[-1]), qmap) for a in qs]
    in_specs += [pl.BlockSpec((N_HEADS, a.shape[1], n), lambda b, i: (0, 0, row0 // n + b)) for a in kts]
    in_specs += [pl.BlockSpec((N_HEADS, n, V_HEAD), lambda b, i: (0, row0 // n + b, 0))]
    args = list(qs) + list(kts) + [v]
    if n_ctx:
        in_specs += [pl.BlockSpec((N_HEADS, a.shape[1], n_ctx), lambda b, i: (0, 0, b)) for a in ctx_kts]
        in_specs += [pl.BlockSpec((N_HEADS, n_ctx, V_HEAD), lambda b, i: (0, b, 0))]
        args += list(ctx_kts) + [ctx_v]
    in_specs += [pl.BlockSpec(a.shape, lambda b, i: (0, 0)) for a in extras]
    args += list(extras)
    return pl.pallas_call(
        functools.partial(_attn_kernel, n_soft=n_soft, n_new=n, n_ctx=n_ctx, lam_init=lam_init),
        grid=(nb, nqt),
        in_specs=in_specs,
        out_specs=pl.BlockSpec((tq, N_HEADS * V_HEAD), lambda b, i: (b * nqt + i, 0)),
        out_shape=jax.ShapeDtypeStruct((nb * n, N_HEADS * V_HEAD), F32),
        compiler_params=_cparams(("parallel", "parallel")),
        name=name,
    )(*args)


def _tri(lower):
    r = lax.broadcasted_iota(jnp.int32, (CHUNK, CHUNK), 0)
    c = lax.broadcasted_iota(jnp.int32, (CHUNK, CHUNK), 1)
    return (c <= r) if lower else (c >= r)


def _nt(a, b):
    return lax.dot_general(a, b, (((1,), (1,)), ((), ())), preferred_element_type=F32)


def _tn(a, b):
    return lax.dot_general(a, b, (((0,), (0,)), ((), ())), preferred_element_type=F32)


def _head_slices(qkv, hd):
    lo = hd * HEAD_DIM
    return qkv[:, lo:lo + HEAD_DIM], qkv[:, 256 + lo:256 + lo + HEAD_DIM], qkv[:, 512 + lo:512 + lo + HEAD_DIM]


def _row_dot(row, mat, transpose_mat):
    row8 = jnp.broadcast_to(row, (8, row.shape[1])).astype(BF16)
    dims = (((1,), (1,)), ((), ())) if transpose_mat else (((1,), (0,)), ((), ()))
    return lax.dot_general(row8, mat, dims, preferred_element_type=F32)[0:1]


def _load_state_t(dst, src):
    for d in range(2):
        for hd in range(N_HEADS):
            dst[d, hd] = src[d, hd].T


def _mlstm_step(dirs, bias_ref, ct_s, n_s, m_s):
    k_scale = HEAD_DIM ** -0.5
    lower = _tri(True).astype(F32)
    upper = _tri(False).astype(F32)
    chains = []
    for d, qkv_ref, tail_ref, _ in dirs:
        g = tail_ref[...] + bias_ref[...]
        ls = _log_sigmoid(g)
        g_t, ls_t = g.T, ls.T
        left, right = (lower, upper) if d == 0 else (upper, lower)
        cum_col = jnp.dot(left, ls, precision=HIGHEST, preferred_element_type=F32)
        cum_row = jnp.dot(ls_t, right, precision=HIGHEST, preferred_element_type=F32)
        qkv = qkv_ref[...]
        for hd in range(N_HEADS):
            ci = TAIL_GATE0 + 4 * (2 * d) + hd
            cf = ci + 4
            q, k, v = _head_slices(qkv, hd)
            chains.append(dict(
                d=d, hd=hd, q=q, k=k, v=v, li_row=g_t[ci:ci + 1, :], b_row=cum_row[cf:cf + 1, :],
                c_col=g[:, ci:ci + 1] - cum_col[:, cf:cf + 1],
                m_prev=m_s[d:d + 1, hd:hd + 1], ct=ct_s[d, hd], n=n_s[d, hd:hd + 1, :]))
    for c in chains:
        c["s"] = _nt(c["k"], c["q"])
        c["v_t"] = c["v"].T
    for c in chains:
        valid = _tri(c["d"] != 0)
        log_d = jnp.where(valid, c["b_row"] + c["c_col"], -jnp.inf)
        log_inter = c["b_row"] + c["m_prev"]
        c["m_t"] = jnp.maximum(log_inter, jnp.max(log_d, axis=0, keepdims=True))
        c["w"] = c["s"] * k_scale * jnp.exp(log_d - c["m_t"])
        c["w_inter"] = jnp.exp(log_inter - c["m_t"])
    for c in chains:
        num = (jnp.dot(c["v_t"], c["w"].astype(BF16), preferred_element_type=F32)
               + c["w_inter"] * _nt(c["ct"].astype(BF16), c["q"]))
        den = jnp.sum(c["w"], axis=0, keepdims=True) + c["w_inter"] * _row_dot(c["n"], c["q"], True)
        c["h"] = num / jnp.maximum(jnp.abs(den), jnp.exp(-c["m_t"]))
    for c in chains:
        last = CHUNK - 1 if c["d"] == 0 else 0
        c["m_new"] = c["m_t"][:, last:last + 1]
        b_last = c["b_row"][:, last:last + 1]
        w_end_row = jnp.exp(b_last - c["b_row"] + c["li_row"] - c["m_new"])
        w_end_col = jnp.exp(c["c_col"] + (b_last - c["m_new"]))
        decay = jnp.exp(b_last + c["m_prev"] - c["m_new"])
        kw = (c["k"].astype(F32) * (w_end_col * k_scale)).astype(BF16)
        c["ct_new"] = decay * c["ct"] + jnp.dot(c["v_t"], kw, preferred_element_type=F32)
        c["n_new"] = decay * c["n"] + _row_dot(w_end_row, c["k"], False) * k_scale
    for j, (_, _, _, h_ref) in enumerate(dirs):
        h_ref[...] = jnp.concatenate([c["h"] for c in chains[j * N_HEADS:(j + 1) * N_HEADS]], axis=0).T
    for c in chains:
        d, hd = c["d"], c["hd"]
        ct_s[d, hd] = c["ct_new"]
        n_s[d, hd:hd + 1, :] = c["n_new"]
        m_s[d:d + 1, hd:hd + 1] = c["m_new"]


def _mlstm_kernel(tab_ref, qkvf_ref, tailf_ref, qkvb_ref, tailb_ref, bias_ref, c0_ref, n0_ref, m0_ref,
                  hf_ref, hb_ref, cn_ref, nn_ref, mn_ref, ct_s, n_s, m_s):
    i = pl.program_id(0)

    @pl.when(tab_ref[3, i] == 1)
    def _():
        _load_state_t(ct_s, c0_ref.at[0])
        n_s[...] = n0_ref[0]
        m_s[...] = m0_ref[0]

    _mlstm_step([(0, qkvf_ref, tailf_ref, hf_ref), (1, qkvb_ref, tailb_ref, hb_ref)], bias_ref, ct_s, n_s, m_s)

    @pl.when(tab_ref[4, i] == 1)
    def _():
        _load_state_t(cn_ref.at[0], ct_s)
        nn_ref[0] = n_s[...]
        mn_ref[0] = m_s[...]


def _ret_step(dirs, lg_ref, st_s):
    k_scale = HEAD_DIM ** -0.5
    s = lax.broadcasted_iota(jnp.int32, (CHUNK, CHUNK), 0)
    t = lax.broadcasted_iota(jnp.int32, (CHUNK, CHUNK), 1)
    lane = lax.broadcasted_iota(jnp.int32, (1, CHUNK), 1)
    chains = []
    for d, qkv_ref, _ in dirs:
        lag = (t - s) if d == 0 else (s - t)
        pos = (lane if d == 0 else CHUNK - 1 - lane).astype(F32)
        qkv = qkv_ref[...]
        for hd in range(N_HEADS):
            q, k, v = _head_slices(qkv, hd)
            chains.append(dict(d=d, hd=hd, q=q, k=k, v=v, lag=lag, pos=pos, st=st_s[d, hd],
                               lg=_log_sigmoid(lg_ref[d:d + 1, hd:hd + 1])))
    for c in chains:
        c["s"] = _nt(c["k"], c["q"])
        c["v_t"] = c["v"].T
    for c in chains:
        intra = jnp.where(c["lag"] >= 0, jnp.exp(jnp.maximum(c["lag"], 0).astype(F32) * c["lg"]), 0.0)
        c["a"] = (c["s"] * k_scale * intra).astype(BF16)
    for c in chains:
        inter = jnp.exp((c["pos"] + 1.0) * c["lg"])
        c["o"] = jnp.dot(c["v_t"], c["a"], preferred_element_type=F32) + inter * _nt(c["st"].astype(BF16), c["q"])
    row = lax.broadcasted_iota(jnp.int32, (CHUNK, HEAD_DIM), 0)
    for c in chains:
        src_pos = (row if c["d"] == 0 else CHUNK - 1 - row).astype(F32)
        tail = jnp.exp((CHUNK - 1.0 - src_pos) * c["lg"]) * k_scale
        kw = (c["k"].astype(F32) * tail).astype(BF16)
        c["st_new"] = jnp.exp(CHUNK * c["lg"]) * c["st"] + jnp.dot(c["v_t"], kw, preferred_element_type=F32)
    for j, (_, _, o_ref) in enumerate(dirs):
        o_ref[...] = jnp.concatenate([c["o"] for c in chains[j * N_HEADS:(j + 1) * N_HEADS]], axis=0).T
    for c in chains:
        st_s[c["d"], c["hd"]] = c["st_new"]


def _ret_kernel(tab_ref, qkvf_ref, qkvb_ref, lg_ref, s0_ref, of_ref, ob_ref, sn_ref, st_s):
    i = pl.program_id(0)

    @pl.when(tab_ref[3, i] == 1)
    def _():
        _load_state_t(st_s, s0_ref.at[0])

    _ret_step([(0, qkvf_ref, of_ref), (1, qkvb_ref, ob_ref)], lg_ref, st_s)

    @pl.when(tab_ref[4, i] == 1)
    def _():
        _load_state_t(sn_ref.at[0], st_s)


def _scan_table(st):
    nch = st["n"] // CHUNK
    rows = [(b * nch + c, b * nch + nch - 1 - c, b, int(c == 0), int(c == nch - 1))
            for b in range(st["nb"]) for c in range(nch)]
    return jnp.asarray(np.array(rows, dtype=np.int32).T)


def _mlstm_scan(tab, mqkv, tail, bias_row, c0, n0, m0):
    t_all = mqkv.shape[0]
    nseq = c0.shape[0]
    fwd = lambda i, tab: (tab[0, i], 0)
    bwd = lambda i, tab: (tab[1, i], 0)
    st = lambda nd: (lambda i, tab: (tab[2, i],) + (0,) * nd)
    grid_spec = pltpu.PrefetchScalarGridSpec(
        num_scalar_prefetch=1,
        grid=(tab.shape[1],),
        in_specs=[pl.BlockSpec((CHUNK, 768), fwd), pl.BlockSpec((CHUNK, LANES), fwd),
                  pl.BlockSpec((CHUNK, 768), bwd), pl.BlockSpec((CHUNK, LANES), bwd),
                  pl.BlockSpec((1, LANES), lambda i, tab: (0, 0)),
                  pl.BlockSpec((1,) + c0.shape[1:], st(4)),
                  pl.BlockSpec((1,) + n0.shape[1:], st(3)),
                  pl.BlockSpec((1,) + m0.shape[1:], st(2))],
        out_specs=[pl.BlockSpec((CHUNK, 256), fwd), pl.BlockSpec((CHUNK, 256), bwd),
                   pl.BlockSpec((1,) + c0.shape[1:], st(4)),
                   pl.BlockSpec((1,) + n0.shape[1:], st(3)),
                   pl.BlockSpec((1,) + m0.shape[1:], st(2))],
        scratch_shapes=[pltpu.VMEM(c0.shape[1:], F32), pltpu.VMEM(n0.shape[1:], F32),
                        pltpu.VMEM(m0.shape[1:], F32)],
    )
    return pl.pallas_call(
        _mlstm_kernel,
        grid_spec=grid_spec,
        out_shape=[jax.ShapeDtypeStruct((t_all, 256), F32), jax.ShapeDtypeStruct((t_all, 256), F32),
                   jax.ShapeDtypeStruct(c0.shape, F32), jax.ShapeDtypeStruct(n0.shape, F32),
                   jax.ShapeDtypeStruct(m0.shape, F32)],
        compiler_params=_cparams(("arbitrary",)),
        name="mlstm_scan",
    )(tab, mqkv, tail, mqkv, tail, bias_row, c0, n0, m0)


def _ret_scan(tab, rqkv, decay_logit, s0):
    t_all = rqkv.shape[0]
    fwd = lambda i, tab: (tab[0, i], 0)
    bwd = lambda i, tab: (tab[1, i], 0)
    st = lambda i, tab: (tab[2, i], 0, 0, 0, 0)
    grid_spec = pltpu.PrefetchScalarGridSpec(
        num_scalar_prefetch=1,
        grid=(tab.shape[1],),
        in_specs=[pl.BlockSpec((CHUNK, 768), fwd), pl.BlockSpec((CHUNK, 768), bwd),
                  pl.BlockSpec(decay_logit.shape, lambda i, tab: (0, 0)),
                  pl.BlockSpec((1,) + s0.shape[1:], st)],
        out_specs=[pl.BlockSpec((CHUNK, 256), fwd), pl.BlockSpec((CHUNK, 256), bwd),
                   pl.BlockSpec((1,) + s0.shape[1:], st)],
        scratch_shapes=[pltpu.VMEM(s0.shape[1:], F32)],
    )
    return pl.pallas_call(
        _ret_kernel,
        grid_spec=grid_spec,
        out_shape=[jax.ShapeDtypeStruct((t_all, 256), F32), jax.ShapeDtypeStruct((t_all, 256), F32),
                   jax.ShapeDtypeStruct(s0.shape, F32)],
        compiler_params=_cparams(("arbitrary",)),
        name="retention_scan",
    )(tab, rqkv, rqkv, decay_logit, s0)


def _proj_kernel(x_ref, w_ref, o_ref):
    o_ref[...] = _bdot(x_ref[...], w_ref[...]).astype(o_ref.dtype)


def _project(x, w, dtype):
    return pl.pallas_call(
        _proj_kernel,
        out_shape=jax.ShapeDtypeStruct((x.shape[0], w.shape[1]), dtype),
        name="ctx_kv_proj",
    )(x, w)


def _first_argmax_mask(cur, axis, size):
    io = lax.broadcasted_iota(jnp.int32, cur.shape, axis)
    mx = jnp.max(cur, axis=axis, keepdims=True)
    ix = jnp.min(jnp.where(cur == mx, io, size), axis=axis, keepdims=True)
    return io == ix


def _route(scores_t, bias_col):
    tm = scores_t.shape[1]
    per = N_EXPERTS // N_GROUPS
    sel = scores_t + bias_col
    s3 = sel.reshape(N_GROUPS, per, tm)
    hit1 = _first_argmax_mask(s3, 1, per)
    m1 = jnp.max(s3, axis=1, keepdims=True)
    m2 = jnp.max(jnp.where(hit1, -jnp.inf, s3), axis=1, keepdims=True)
    cur = m1 + m2
    gsel = None
    for _ in range(TOPK_GROUPS):
        hit = _first_argmax_mask(cur, 0, N_GROUPS)
        gsel = hit if gsel is None else jnp.logical_or(gsel, hit)
        cur = jnp.where(hit, -jnp.inf, cur)
    cur = jnp.where(gsel, s3, -jnp.inf).reshape(N_EXPERTS, tm)
    chosen, hits = None, []
    for _ in range(TOP_K):
        hit = _first_argmax_mask(cur, 0, N_EXPERTS)
        hits.append(hit)
        chosen = hit if chosen is None else jnp.logical_or(chosen, hit)
        cur = jnp.where(hit, -jnp.inf, cur)
    w = jnp.where(chosen, scores_t, 0.0)
    return w / jnp.sum(w, axis=0, keepdims=True) * ROUTED_SCALE, chosen, hits


def _out_kernel(oa_ref, od_ref, hf_ref, hb_ref, mo_ref, of_ref, ob_ref, rg_ref,
                x_ref, mod_ref, mg_ref, npost_ref, npre_ref, wout_ref, wrt_ref, rb_ref,
                x1_ref, h2_ref, hpk_ref, eidx_ref, rank_ref, wk_ref, cnt_ref, count_s):
    @pl.when(pl.program_id(0) == 0)
    def _():
        count_s[...] = jnp.zeros(count_s.shape, F32)

    a = oa_ref[...]
    dd = od_ref[...]
    b = _rms_heads(hf_ref[...] + hb_ref[...]) * mg_ref[...] * _sigmoid(mo_ref[...])
    r = _rms_heads(of_ref[...] + ob_ref[...]) * _silu(rg_ref[...])
    mix = (jnp.dot(a.astype(BF16), wout_ref[0:256, :], preferred_element_type=F32)
           + jnp.dot(b.astype(BF16), wout_ref[256:512, :], preferred_element_type=F32)
           + jnp.dot(r.astype(BF16), wout_ref[512:768, :], preferred_element_type=F32)
           + jnp.dot(dd.astype(BF16), wout_ref[768:1024, :], preferred_element_type=F32))
    m = mod_ref[0]
    x1 = x_ref[...] + m[2:3] * (_rms(mix) * npost_ref[...])
    x1_ref[...] = x1
    h2 = (_rms(x1) * npre_ref[...] * (1.0 + m[4:5]) + m[3:4]).astype(BF16)
    h2_ref[...] = h2
    hpk_ref[...] = _pack_rows(h2)

    logits_t = lax.dot_general(wrt_ref[...], h2, (((1,), (1,)), ((), ())), preferred_element_type=F32)
    w_t, chosen, hits = _route(_sigmoid(logits_t), rb_ref[...])
    tm = w_t.shape[1]
    src = lax.broadcasted_iota(jnp.int32, (tm, tm), 0)
    dst = lax.broadcasted_iota(jnp.int32, (tm, tm), 1)
    before = jnp.where(src < dst, 1.0, 0.0).astype(BF16)
    picked = jnp.where(chosen, 1.0, 0.0)
    rank = jnp.dot(picked.astype(BF16), before, preferred_element_type=F32) + count_s[...]
    count_s[...] += jnp.sum(picked, axis=1, keepdims=True)
    cnt_ref[...] = jnp.broadcast_to(count_s[...], cnt_ref.shape)
    e_io = lax.broadcasted_iota(jnp.int32, w_t.shape, 0)
    pick = lambda hit, val: jnp.sum(jnp.where(hit, val, jnp.zeros_like(val)), axis=0, keepdims=True)
    eidx_ref[...] = jnp.concatenate([pick(h, e_io) for h in hits], axis=0)
    rank_ref[...] = jnp.concatenate([pick(h, rank) for h in hits], axis=0).astype(jnp.int32)
    wk = jnp.concatenate([pick(h, w_t) for h in hits] + [jnp.zeros((LANES - TOP_K, tm), F32)], axis=0)
    wk_ref[...] = wk.T


def _output_stage(oa, od, hf, hb, mo, of, ob, rg, x, mod, mg, npost, npre, wout, wrt, rb, st):
    t_all = x.shape[0]
    row = lambda i: (i, 0)
    const2 = lambda i: (0, 0)
    tok = lambda w: pl.BlockSpec((TM, w), row)
    full = lambda a: pl.BlockSpec(a.shape, const2)
    return pl.pallas_call(
        _out_kernel,
        grid=(t_all // TM,),
        in_specs=[tok(256)] * 8 + [tok(D_MODEL), _mod_spec(st),
                                   full(mg), full(npost), full(npre), full(wout), full(wrt), full(rb)],
        out_specs=[tok(D_MODEL), tok(D_MODEL), tok(D_MODEL // 2),
                   pl.BlockSpec((TOP_K, TM), lambda i: (0, i)), pl.BlockSpec((TOP_K, TM), lambda i: (0, i)),
                   tok(LANES), pl.BlockSpec((N_EXPERTS, LANES), const2)],
        out_shape=[jax.ShapeDtypeStruct((t_all, D_MODEL), F32),
                   jax.ShapeDtypeStruct((t_all, D_MODEL), BF16),
                   jax.ShapeDtypeStruct((t_all, D_MODEL // 2), jnp.uint32),
                   jax.ShapeDtypeStruct((TOP_K, t_all), jnp.int32),
                   jax.ShapeDtypeStruct((TOP_K, t_all), jnp.int32),
                   jax.ShapeDtypeStruct((t_all, LANES), F32),
                   jax.ShapeDtypeStruct((N_EXPERTS, LANES), F32)],
        scratch_shapes=[pltpu.VMEM((N_EXPERTS, 1), F32)],
        compiler_params=_cparams(("arbitrary",)),
        name="output_stage_" + st["name"],
    )(oa, od, hf, hb, mo, of, ob, rg, x, mod, mg, npost, npre, wout, wrt, rb)


SC_CORES, SC_SUBCORES = 2, 16
SC_WORKERS = SC_CORES * SC_SUBCORES
SLOT_BLOCK = 512
DISPATCH_ROWS = 64
COMBINE_ROWS = 64


def _sc_mesh():
    return plsc.VectorSubcoreMesh(core_axis_name="core", subcore_axis_name="subcore")


def _sc_worker():
    return lax.axis_index("subcore") * SC_CORES + lax.axis_index("core")


def _sc_dispatch(hpk, dest, n_slots):
    t_all, width = hpk.shape
    per_worker = t_all // SC_WORKERS
    assert t_all % (SC_WORKERS * DISPATCH_ROWS) == 0

    @functools.partial(
        pl.kernel, mesh=_sc_mesh(), out_type=jax.ShapeDtypeStruct((n_slots, width), hpk.dtype),
        scratch_types=[pltpu.VMEM((DISPATCH_ROWS, width), hpk.dtype)]
        + [pltpu.VMEM((DISPATCH_ROWS,), jnp.int32)] * TOP_K + [pltpu.SemaphoreType.DMA])
    def dispatch(x_hbm, d_hbm, o_hbm, rows_v, *rest):
        idx, sem = rest[:TOP_K], rest[TOP_K]

        @pl.loop(0, per_worker // DISPATCH_ROWS)
        def _(j):
            base = _sc_worker() * per_worker + j * DISPATCH_ROWS
            pltpu.sync_copy(x_hbm.at[pl.ds(base, DISPATCH_ROWS)], rows_v)
            for k in range(TOP_K):
                pltpu.sync_copy(d_hbm.at[pl.ds(k * t_all + base, DISPATCH_ROWS)], idx[k])
            copies = [pltpu.async_copy(rows_v, o_hbm.at[idx[k]], sem) for k in range(TOP_K)]
            for c in copies:
                c.wait()

    return dispatch(hpk, dest)


def _sc_combine(yb, dest, t_all):
    width = yb.shape[1]
    per_worker = t_all // SC_WORKERS
    assert t_all % (SC_WORKERS * COMBINE_ROWS) == 0

    @functools.partial(
        pl.kernel, mesh=_sc_mesh(), out_type=jax.ShapeDtypeStruct((TOP_K, t_all, width), yb.dtype),
        scratch_types=[pltpu.VMEM((COMBINE_ROWS, width), yb.dtype)] * 2
        + [pltpu.VMEM((COMBINE_ROWS,), jnp.int32)] * TOP_K + [pltpu.SemaphoreType.DMA] * 2)
    def combine(y_hbm, d_hbm, o_hbm, rows_a, rows_b, *rest):
        idx, sems = rest[:TOP_K], rest[TOP_K:]
        bufs = (rows_a, rows_b)

        @pl.loop(0, per_worker // COMBINE_ROWS)
        def _(j):
            base = _sc_worker() * per_worker + j * COMBINE_ROWS
            for k in range(TOP_K):
                pltpu.sync_copy(d_hbm.at[pl.ds(k * t_all + base, COMBINE_ROWS)], idx[k])
            pending = pltpu.async_copy(y_hbm.at[idx[0]], bufs[0], sems[0])
            for k in range(TOP_K):
                pending.wait()
                if k + 1 < TOP_K:
                    pending = pltpu.async_copy(y_hbm.at[idx[k + 1]], bufs[(k + 1) % 2], sems[(k + 1) % 2])
                pltpu.sync_copy(bufs[k % 2], o_hbm.at[k, pl.ds(base, COMBINE_ROWS)])

    return combine(yb, dest)


def _pack_rows(x):
    bits = pltpu.bitcast(x.astype(BF16).astype(F32), jnp.uint32)
    w = x.shape[-1] // 2
    return (bits[..., :w] >> 16) | (bits[..., w:] & jnp.uint32(0xFFFF0000))


def _unpack_rows(words):
    return pltpu.bitcast(words << 16, F32), pltpu.bitcast(words & jnp.uint32(0xFFFF0000), F32)


def _expert_kernel(be_ref, used_ref, xs_ref, wg_ref, wu_ref, wd_ref, y_ref, wg_s, wu_s, wd_s):
    b = pl.program_id(0)
    fresh = jnp.logical_or(b == 0, be_ref[b] != be_ref[jnp.maximum(b - 1, 0)])

    @pl.when(jnp.logical_and(fresh, b < used_ref[0]))
    def _():
        wg_s[...] = wg_ref[0, 0].astype(BF16)
        wu_s[...] = wu_ref[0, 0].astype(BF16)
        wd_s[...] = wd_ref[0, 0].astype(BF16)

    @pl.when(b < used_ref[0])
    def _():
        lo, hi = (v.astype(BF16) for v in _unpack_rows(xs_ref[...]))
        half = D_MODEL // 2
        gate = (jnp.dot(lo, wg_s[:half, :], preferred_element_type=F32)
                + jnp.dot(hi, wg_s[half:, :], preferred_element_type=F32))
        up = (jnp.dot(lo, wu_s[:half, :], preferred_element_type=F32)
              + jnp.dot(hi, wu_s[half:, :], preferred_element_type=F32))
        y = jnp.dot((_silu(gate) * up).astype(BF16), wd_s[...], preferred_element_type=F32)
        y_ref[...] = _pack_rows(y)


def _expert_blocks(layer, xs, block_expert, blocks_used, wg, wu, wdn):
    n_slots = xs.shape[0]
    wspec = lambda shape: pl.BlockSpec((1, 1) + shape, lambda b, be, used: (layer, be[b], 0, 0))
    rows = lambda b, be, used: (jnp.minimum(b, used[0] - 1), 0)
    grid_spec = pltpu.PrefetchScalarGridSpec(
        num_scalar_prefetch=2,
        grid=(n_slots // SLOT_BLOCK,),
        in_specs=[pl.BlockSpec((SLOT_BLOCK, D_MODEL // 2), rows),
                  wspec((D_MODEL, D_EXPERT)), wspec((D_MODEL, D_EXPERT)), wspec((D_EXPERT, D_MODEL))],
        out_specs=pl.BlockSpec((SLOT_BLOCK, D_MODEL // 2), rows),
        scratch_shapes=[pltpu.VMEM((D_MODEL, D_EXPERT), BF16), pltpu.VMEM((D_MODEL, D_EXPERT), BF16),
                        pltpu.VMEM((D_EXPERT, D_MODEL), BF16)])
    return pl.pallas_call(
        _expert_kernel, grid_spec=grid_spec,
        out_shape=jax.ShapeDtypeStruct((n_slots, D_MODEL // 2), jnp.uint32),
        compiler_params=_cparams(("arbitrary",)),
        name="moe_experts",
    )(block_expert, blocks_used, xs, wg, wu, wdn)


def _moe_out_kernel(g_ref, wk_ref, h_ref, x1_ref, mod_ref, npost_ref, sg_ref, su_ref, sd_ref, o_ref):
    h = h_ref[...]
    act = _silu(jnp.dot(h, sg_ref[...], preferred_element_type=F32)) * jnp.dot(h, su_ref[...],
                                                                              preferred_element_type=F32)
    moe = jnp.dot(act.astype(BF16), sd_ref[...], preferred_element_type=F32)
    wk = wk_ref[...]
    lo, hi = 0.0, 0.0
    for k in range(TOP_K):
        g_lo, g_hi = _unpack_rows(g_ref[k])
        lo = lo + g_lo * wk[:, k:k + 1]
        hi = hi + g_hi * wk[:, k:k + 1]
    moe = moe + jnp.concatenate([lo, hi], axis=-1)
    m = mod_ref[0]
    o_ref[...] = x1_ref[...] + m[5:6] * (_rms(moe) * npost_ref[...])


def _moe_out_stage(g, wk, h2, x1, mod, npost, sg, su, sd, st):
    t_all = h2.shape[0]
    row = lambda i: (i, 0)
    full = lambda a: pl.BlockSpec(a.shape, lambda i: (0, 0))
    return pl.pallas_call(
        _moe_out_kernel,
        grid=(t_all // TM,),
        in_specs=[pl.BlockSpec((TOP_K, TM, D_MODEL // 2), lambda i: (0, i, 0)), pl.BlockSpec((TM, LANES), row),
                  pl.BlockSpec((TM, D_MODEL), row), pl.BlockSpec((TM, D_MODEL), row), _mod_spec(st),
                  full(npost), full(sg), full(su), full(sd)],
        out_specs=pl.BlockSpec((TM, D_MODEL), row),
        out_shape=jax.ShapeDtypeStruct((t_all, D_MODEL), F32),
        compiler_params=_cparams(("parallel",)),
        name="moe_combine_" + st["name"],
    )(g, wk, h2, x1, mod, npost, sg, su, sd)


def _moe_stage(layer, h2, hpk, eidx, rank, wk, counts, x1, mod, npost, wg, wu, wdn, sg, su, sd, st):
    t_all = h2.shape[0]
    n_blocks = -(-(t_all * TOP_K + N_EXPERTS * (SLOT_BLOCK - 1)) // SLOT_BLOCK)
    cnt = counts[:, 0].astype(jnp.int32)
    padded = (cnt + SLOT_BLOCK - 1) // SLOT_BLOCK * SLOT_BLOCK
    pad_end = jnp.cumsum(padded)
    pad_start = pad_end - padded
    experts = jnp.arange(N_EXPERTS, dtype=jnp.int32)[:, None, None]
    dest = rank + jnp.sum(jnp.where(eidx[None] == experts, pad_start[:, None, None], 0), axis=0)
    dest = dest.reshape(TOP_K * t_all)
    block_start = jnp.arange(n_blocks, dtype=jnp.int32) * SLOT_BLOCK
    block_expert = jnp.minimum(jnp.sum((pad_end[None, :] <= block_start[:, None]).astype(jnp.int32), axis=1),
                               N_EXPERTS - 1)
    blocks_used = (pad_end[-1:] // SLOT_BLOCK).astype(jnp.int32)

    xs = _sc_dispatch(hpk, dest, n_blocks * SLOT_BLOCK)
    yb = _expert_blocks(layer, xs, block_expert, blocks_used, wg, wu, wdn)
    g = _sc_combine(yb, dest, t_all)
    return _moe_out_stage(g, wk, h2, x1, mod, npost, sg, su, sd, st)


_PERM32 = np.concatenate([np.arange(8, 16), np.arange(0, 8), np.arange(24, 32), np.arange(16, 24)])


def _rope_tables(n):
    pos = jnp.arange(n)
    quarter = QK_ROPE // 4
    inv = 1.0 / (ROPE_THETA ** (jnp.arange(quarter, dtype=F32) / quarter))
    ang_r = (pos // GRID_W).astype(F32)[:, None] * inv[None, :]
    ang_c = (pos % GRID_W).astype(F32)[:, None] * inv[None, :]
    cos32 = jnp.concatenate([jnp.cos(ang_r)] * 2 + [jnp.cos(ang_c)] * 2, axis=-1)
    sin32 = jnp.concatenate([-jnp.sin(ang_r), jnp.sin(ang_r), -jnp.sin(ang_c), jnp.sin(ang_c)], axis=-1)
    cos256, sin256 = jnp.tile(cos32, (1, 8)), jnp.tile(sin32, (1, 8))
    one, zero = jnp.ones((n, QK_NOPE), F32), jnp.zeros((n, QK_NOPE), F32)
    cosq = jnp.concatenate([one, cos32] * N_HEADS, axis=-1)
    sinq = jnp.concatenate([zero, sin32] * N_HEADS, axis=-1)
    return cos256, sin256, cosq, sinq


def _split_w_in(w_in_l):
    sizes = (256, 256, 256, 256, 256, 256, 256, 16, 256, 256, 256, 256, Q_LORA, KV_LORA, QK_ROPE)
    idx = np.cumsum(sizes)[:-1]
    return jnp.split(w_in_l, [int(v) for v in idx], axis=-1)


def _layer_weights(w_in_l, w_uq_l):
    (a_q, a_k, a_v, m_q, m_k, m_v, m_o, m_g, r_q, r_k, r_v, r_g, d_cq, d_ckv, d_kr) = _split_w_in(w_in_l)
    perm256 = np.concatenate([_PERM32 + 32 * j for j in range(8)])
    pad = lambda w, n: jnp.pad(w, ((0, 0), (0, n - w.shape[1])))
    tail = pad(jnp.concatenate([d_kr, m_g], axis=-1), LANES)
    tail_s = pad(d_kr[:, _PERM32], LANES)
    w_all = jnp.concatenate([a_q, a_k, a_v, m_q, m_k, m_v, m_o, r_q, r_k, r_v, r_g, d_cq, d_ckv, tail,
                             a_q[:, perm256], a_k[:, perm256], tail_s], axis=-1).astype(BF16)
    per = QK_NOPE + QK_ROPE
    permq = np.concatenate([np.concatenate([np.arange(QK_NOPE), QK_NOPE + _PERM32]) + per * j
                            for j in range(N_HEADS)])
    return w_all, w_uq_l.astype(BF16), w_uq_l[:, permq].astype(BF16)


def kernel(x_prompt, x_sample, cache_diff_k, cache_diff_v, state_mlstm_C, state_mlstm_n, state_mlstm_m, state_ret_S, cache_mla_ckv, cache_mla_krope, c, c_ctx, w_mod, b_mod, norm_pre, norm_post, w_in, w_out, diff_lambda, diff_norm, mlstm_gate_bias, mlstm_norm, ret_decay_logit, mla_q_norm, mla_w_uq, mla_kv_norm, mla_w_ukv, moe_w_router, moe_router_bias, moe_w_gate, moe_w_up, moe_w_down, shared_w_gate, shared_w_up, shared_w_down):
    bp, n_p, _ = x_prompt.shape
    bs, n_s, _ = x_sample.shape
    depth = w_in.shape[0]
    past = cache_diff_k.shape[2]
    assert n_p % TM == 0 and n_s % TM == 0 and n_p % CHUNK == 0 and n_s % min(TK, n_s) == 0
    assert past % min(TK, past) == 0 and bs + 1 <= 8 and n_s % GRID_W == 0

    streams = (dict(name="ctx", nb=bp, n=n_p, latent=False), dict(name="latent", nb=bs, n=n_s, latent=True))
    xs_by_stream = [x_prompt.reshape(bp * n_p, D_MODEL), x_sample.reshape(bs * n_s, D_MODEL)]
    cond = jnp.zeros((8, D_MODEL), F32).at[0].set(c_ctx).at[1:1 + bs].set(c)
    mod_all = _modulation(cond, w_mod, b_mod).reshape(depth, 8, 6, D_MODEL)
    tabs = _rope_tables(n_s)
    scan_tabs = [_scan_table(st) for st in streams]

    outs = [[] for _ in range(8)]
    for l in range(depth):
        lam_init = 0.8 - 0.6 * math.exp(-0.3 * l)
        mod = mod_all[l]
        w_all, wuq, wuqs = _layer_weights(w_in[l], mla_w_uq[l])
        wukv = mla_w_ukv[l].astype(BF16)
        wout, wrt, rbias = w_out[l].astype(BF16), moe_w_router[l].T.astype(BF16), moe_router_bias[l][:, None]
        shared = [w[l].astype(BF16) for w in (shared_w_gate, shared_w_up, shared_w_down)]
        lamv, dg = diff_lambda[l], diff_norm[l][None]
        bias_row = jnp.zeros((1, LANES), F32).at[0, TAIL_GATE0:TAIL_GATE0 + 16].set(mlstm_gate_bias[l].reshape(16))

        ckt = jnp.transpose(cache_diff_k[:, l], (2, 3, 0, 1)).reshape(N_HEADS, HEAD_DIM, bs * past).astype(BF16)
        cv = jnp.transpose(cache_diff_v[:, l], (2, 0, 1, 3)).reshape(N_HEADS, bs * past, HEAD_DIM).astype(BF16)
        kvc = _project(cache_mla_ckv[:, l].reshape(bs * past, KV_LORA), wukv, BF16)
        kvc = kvc.reshape(bs * past, N_HEADS, QK_NOPE + V_HEAD)
        krc = jnp.broadcast_to(cache_mla_krope[:, l].reshape(bs * past, 1, QK_ROPE).astype(BF16),
                               (bs * past, N_HEADS, QK_ROPE))
        kct = jnp.transpose(jnp.concatenate([kvc[..., :QK_NOPE], krc], axis=-1), (1, 2, 0))
        vc = jnp.transpose(kvc[..., QK_NOPE:], (1, 0, 2))

        for si, st in enumerate(streams):
            x, nb, n = xs_by_stream[si], st["nb"], st["n"]
            (aq1, aq2, ak1t, ak2t, avh, ak, av, mqkv, mo, rqkv, rg, tail, qmla, ckv, kmlat, vmla) = _input_stage(
                x, mod, norm_pre[l, 0:1], w_all, tabs, mla_q_norm[l][None], wuq, wuqs,
                mla_kv_norm[l][None], wukv, st)
            if st["latent"]:
                oa = _attention("diff_attn_latent", [aq1, aq2], [ak1t, ak2t], avh,
                                [ckt[:, :DIFF_HALF], ckt[:, DIFF_HALF:]], cv, [lamv, dg], 0, nb, n, past, lam_init)
                od = _attention("mla_attn_latent", [qmla], [kmlat], vmla, [kct], vc, [], 0, nb, n, past)
                states = [s[:, l] for s in (state_mlstm_C, state_mlstm_n, state_mlstm_m, state_ret_S)]
            else:
                oa = _attention("diff_attn_ctx", [aq1, aq2], [ak1t, ak2t], avh, None, None, [lamv, dg],
                                0, nb, n, 0, lam_init)
                od = _attention("mla_attn_ctx", [qmla], [kmlat], vmla, None, None, [], 0, nb, n, 0)
                states = [jnp.zeros((nb,) + s.shape[2:], F32)
                          for s in (state_mlstm_C, state_mlstm_n, state_mlstm_m, state_ret_S)]
            hf, hb, c_n, n_n, m_n = _mlstm_scan(scan_tabs[si], mqkv, tail, bias_row, *states[:3])
            of, ob, s_n = _ret_scan(scan_tabs[si], rqkv, ret_decay_logit[l], states[3])

            x1, h2, hpk, eidx, rank, wk, counts = _output_stage(
                oa, od, hf, hb, mo, of, ob, rg, x, mod, mlstm_norm[l][None], norm_post[l, 0:1],
                norm_pre[l, 1:2], wout, wrt, rbias, st)
            xs_by_stream[si] = _moe_stage(l, h2, hpk, eidx, rank, wk, counts, x1, mod, norm_post[l, 1:2],
                                          moe_w_gate, moe_w_up, moe_w_down, *shared, st)
            if not st["latent"]:
                new = (ak.reshape(nb, n, N_HEADS, HEAD_DIM), av.reshape(nb, n, N_HEADS, HEAD_DIM), c_n, n_n, m_n,
                       s_n, ckv.reshape(nb, n, KV_LORA), tail[:, :QK_ROPE].reshape(nb, n, QK_ROPE))
                for o, v in zip(outs, new):
                    o.append(v)

    return (xs_by_stream[0].reshape(bp, n_p, D_MODEL), xs_by_stream[1].reshape(bs, n_s, D_MODEL)) + tuple(
        jnp.stack(o, axis=1) for o in outs)
```

```python
import functools
import math

import numpy as np
import jax
import jax.numpy as jnp
from jax import lax
from jax.experimental import pallas as pl
from jax.experimental.pallas import tpu as pltpu
from jax.experimental.pallas import tpu_sc as plsc

F32 = jnp.float32
BF16 = jnp.bfloat16
HIGHEST = lax.Precision.HIGHEST

D_MODEL = 1024
GRID_W = 64
GROUP_WIDTH = 256
HEAD_DIM = 64
N_HEADS = 4
DIFF_HALF = 32
ROPE_THETA = 10000.0
Q_LORA = 256
KV_LORA = 128
QK_NOPE = 64
QK_ROPE = 32
V_HEAD = 64
N_EXPERTS = 64
TOP_K = 8
N_GROUPS = 8
TOPK_GROUPS = 4
D_EXPERT = 256
ROUTED_SCALE = 2.5
CHUNK = 128
EPS = 1e-6
Q_MLA = N_HEADS * (QK_NOPE + QK_ROPE)
KV_MLA = N_HEADS * (QK_NOPE + V_HEAD)

LANES = 128
VMEM_LIMIT = 56 * 1024 * 1024

TM = 256
TQ = 256
TK = 4096

C_AQ, C_AK, C_AV = 0, 256, 512
C_MQKV, C_MO = 768, 1536
C_RQKV, C_RG = 1792, 2560
C_CQ, C_CKV, C_TAIL = 2816, 3072, 3200
C_AQS, C_AKS, C_TAILS = 3328, 3584, 3840
W_ALL = 3968
TAIL_GATE0 = QK_ROPE
VT_ONES = 16
VT_ROWS = V_HEAD + VT_ONES


def _cparams(sem, flags=None):
    return pltpu.CompilerParams(dimension_semantics=sem, vmem_limit_bytes=VMEM_LIMIT, flags=flags)


def _rms(x):
    return x * lax.rsqrt(jnp.mean(x * x, axis=-1, keepdims=True) + EPS)


def _head_mean_matrix(width):
    r = lax.broadcasted_iota(jnp.int32, (width, width), 0) // HEAD_DIM
    c = lax.broadcasted_iota(jnp.int32, (width, width), 1) // HEAD_DIM
    return jnp.where(r == c, 1.0 / HEAD_DIM, 0.0).astype(F32)


def _rms_heads(x):
    ms = jnp.dot(x * x, _head_mean_matrix(x.shape[-1]), precision=HIGHEST, preferred_element_type=F32)
    return x * lax.rsqrt(ms + EPS)


def _sigmoid(x):
    return 1.0 / (1.0 + jnp.exp(-x))


def _silu(x):
    return x * _sigmoid(x)


def _log_sigmoid(x):
    return jnp.minimum(x, 0.0) - jnp.log1p(jnp.exp(-jnp.abs(x)))


def _bdot(a, b):
    return jnp.dot(a.astype(BF16), b.astype(BF16), preferred_element_type=F32)


def _mod_kernel(c_ref, w_ref, b_ref, o_ref):
    o_ref[0] = _bdot(_silu(c_ref[...]), w_ref[0]) + b_ref[0]


def _modulation(cond, w_mod, b_mod):
    depth, _, n = w_mod.shape
    tn = 1536
    return pl.pallas_call(
        _mod_kernel,
        grid=(depth, n // tn),
        in_specs=[pl.BlockSpec((8, D_MODEL), lambda l, j: (0, 0)),
                  pl.BlockSpec((1, D_MODEL, tn), lambda l, j: (l, 0, j)),
                  pl.BlockSpec((1, 1, tn), lambda l, j: (l, 0, j))],
        out_specs=pl.BlockSpec((1, 8, tn), lambda l, j: (l, 0, j)),
        out_shape=jax.ShapeDtypeStruct((depth, 8, n), F32),
        compiler_params=_cparams(("parallel", "parallel")),
        name="adaln_mod",
    )(cond, w_mod, b_mod.reshape(depth, 1, n))


def _in_kernel(x_ref, mod_ref, npre_ref, w_ref, cos_ref, sin_ref, cosq_ref, sinq_ref,
               qg_ref, wuq_ref, wuqs_ref, kvg_ref, wukv_ref,
               aq1_ref, aq2_ref, ak1t_ref, ak2t_ref, avh_ref, ak_ref, av_ref,
               mqkv_ref, mo_ref, rqkv_ref, rg_ref, tail_ref,
               qmla_ref, ckv_ref, kmlat_ref, vmla_ref, *, latent):
    m = mod_ref[0]
    h = (_rms(x_ref[...]) * npre_ref[...] * (1.0 + m[1:2]) + m[0:1]).astype(BF16)

    def proj(c0, width):
        return jnp.dot(h, w_ref[:, c0:c0 + width], preferred_element_type=F32)

    def rotated(c0, c0_swapped, width, cos, sin):
        return proj(c0, width) * cos + proj(c0_swapped, width) * sin if latent else proj(c0, width)

    cos = cos_ref[...]
    sin = sin_ref[...]
    aq = rotated(C_AQ, C_AQS, 256, cos, sin) * (DIFF_HALF ** -0.5)
    ak = rotated(C_AK, C_AKS, 256, cos, sin)
    av = proj(C_AV, 256)
    ak_ref[...] = ak
    av_ref[...] = av
    aq_t, av_t = aq.T.astype(BF16), av.T.astype(BF16)
    ones = jnp.ones((VT_ONES, aq_t.shape[1]), BF16)
    for hd in range(N_HEADS):
        lo = hd * HEAD_DIM
        aq1_ref[hd] = aq_t[lo:lo + DIFF_HALF, :]
        aq2_ref[hd] = aq_t[lo + DIFF_HALF:lo + HEAD_DIM, :]
        ak1t_ref[hd] = ak[:, lo:lo + DIFF_HALF].astype(BF16)
        ak2t_ref[hd] = ak[:, lo + DIFF_HALF:lo + HEAD_DIM].astype(BF16)
        avh_ref[hd, 0:HEAD_DIM, :] = av_t[lo:lo + HEAD_DIM, :]
        avh_ref[hd, HEAD_DIM:VT_ROWS, :] = ones

    mqkv_ref[...] = proj(C_MQKV, 768).astype(BF16)
    mo_ref[...] = proj(C_MO, 256)
    rqkv_ref[...] = proj(C_RQKV, 768).astype(BF16)
    rg_ref[...] = proj(C_RG, 256)

    tail = proj(C_TAIL, LANES)
    tail_ref[...] = tail
    kr = tail * cos[:, :LANES] + proj(C_TAILS, LANES) * sin[:, :LANES] if latent else tail
    kr = kr[:, :QK_ROPE].astype(BF16)

    cqn = (_rms(proj(C_CQ, Q_LORA)) * qg_ref[...]).astype(BF16)
    qmla = jnp.dot(cqn, wuq_ref[...], preferred_element_type=F32)
    if latent:
        qmla = qmla * cosq_ref[...] + jnp.dot(cqn, wuqs_ref[...], preferred_element_type=F32) * sinq_ref[...]
    qmla_t = (qmla * ((QK_NOPE + QK_ROPE) ** -0.5)).T.astype(BF16)
    for hd in range(N_HEADS):
        lo = hd * (QK_NOPE + QK_ROPE)
        qmla_ref[hd] = qmla_t[lo:lo + QK_NOPE + QK_ROPE, :]
    ckvn = _rms(proj(C_CKV, KV_LORA)) * kvg_ref[...]
    ckv_ref[...] = ckvn
    kv = jnp.dot(ckvn.astype(BF16), wukv_ref[...], preferred_element_type=F32)
    kv_t = kv.T.astype(BF16)
    per = QK_NOPE + V_HEAD
    for hd in range(N_HEADS):
        kmlat_ref[hd, :, 0:QK_NOPE] = kv[:, hd * per:hd * per + QK_NOPE].astype(BF16)
        kmlat_ref[hd, :, QK_NOPE:QK_NOPE + QK_ROPE] = kr
        vmla_ref[hd, 0:V_HEAD, :] = kv_t[hd * per + QK_NOPE:(hd + 1) * per, :]
        vmla_ref[hd, V_HEAD:VT_ROWS, :] = ones


def _mod_spec(st):
    tps = st["n"] // TM
    return pl.BlockSpec((1, 6, D_MODEL), lambda i, *_: (1 + i // tps if st["latent"] else 0, 0, 0))


def _input_stage(x, mod, npre, w_all, tabs, qg, wuq, wuqs, kvg, wukv, st):
    t_all = x.shape[0]
    tps = st["n"] // TM
    row = lambda i: (i, 0)
    tab = lambda w: pl.BlockSpec((TM, w), lambda i: (i % tps if st["latent"] else 0, 0))
    hrow = lambda i: (0, i, 0)
    const2 = lambda i: (0, 0)
    tok = lambda w: pl.BlockSpec((TM, w), row)
    headed = lambda w: pl.BlockSpec((N_HEADS, TM, w), hrow)
    headed_t = lambda d: pl.BlockSpec((N_HEADS, d, TM), lambda i: (0, 0, i))
    full = lambda a: pl.BlockSpec(a.shape, const2)
    cos, sin, cosq, sinq = tabs
    out_shapes = [
        (headed_t(DIFF_HALF), (N_HEADS, DIFF_HALF, t_all), BF16),
        (headed_t(DIFF_HALF), (N_HEADS, DIFF_HALF, t_all), BF16),
        (headed(DIFF_HALF), (N_HEADS, t_all, DIFF_HALF), BF16),
        (headed(DIFF_HALF), (N_HEADS, t_all, DIFF_HALF), BF16),
        (headed_t(VT_ROWS), (N_HEADS, VT_ROWS, t_all), BF16),
        (tok(256), (t_all, 256), F32),
        (tok(256), (t_all, 256), F32),
        (tok(768), (t_all, 768), BF16),
        (tok(256), (t_all, 256), F32),
        (tok(768), (t_all, 768), BF16),
        (tok(256), (t_all, 256), F32),
        (tok(LANES), (t_all, LANES), F32),
        (headed_t(QK_NOPE + QK_ROPE), (N_HEADS, QK_NOPE + QK_ROPE, t_all), BF16),
        (tok(KV_LORA), (t_all, KV_LORA), F32),
        (headed(QK_NOPE + QK_ROPE), (N_HEADS, t_all, QK_NOPE + QK_ROPE), BF16),
        (headed_t(VT_ROWS), (N_HEADS, VT_ROWS, t_all), BF16),
    ]
    return pl.pallas_call(
        functools.partial(_in_kernel, latent=st["latent"]),
        grid=(t_all // TM,),
        in_specs=[tok(D_MODEL), _mod_spec(st),
                  full(npre), full(w_all), tab(256), tab(256), tab(Q_MLA), tab(Q_MLA),
                  full(qg), full(wuq), full(wuqs), full(kvg), full(wukv)],
        out_specs=[s for s, _, _ in out_shapes],
        out_shape=[jax.ShapeDtypeStruct(shp, dt) for _, shp, dt in out_shapes],
        compiler_params=_cparams(("parallel",)),
        name="input_stage_" + st["name"],
    )(x, mod, npre, w_all, cos, sin, cosq, sinq, qg, wuq, wuqs, kvg, wukv)


def _attn_kernel(*refs, n_soft, n_new, n_ctx, lam_init):
    refs = list(refs)
    q_refs, kt_refs, v_ref = refs[:n_soft], refs[n_soft:2 * n_soft], refs[2 * n_soft]
    pos = 2 * n_soft + 1
    if n_ctx:
        ckt_refs, cv_ref = refs[pos:pos + n_soft], refs[pos + n_soft]
        pos += n_soft + 1
    if n_soft == 2:
        lam_ref, g_ref = refs[pos:pos + 2]
        pos += 2
    o_ref = refs[pos]
    tq = o_ref.shape[0]
    nchain = n_soft * N_HEADS

    def chunk(state, kt_of, v_of):
        ms, ls, accs = state
        new_m, new_l, new_acc = list(ms), list(ls), list(accs)
        order = [(j, hd) for hd in range(N_HEADS) for j in range(n_soft)]
        ss = [jnp.dot(q_refs[j][hd], kt_of(j, hd), preferred_element_type=F32) for j, hd in order]
        ps, alphas = [], []
        for (j, hd), s in zip(order, ss):
            c = j * N_HEADS + hd
            m_new = jnp.maximum(ms[c], jnp.broadcast_to(jnp.max(s, axis=-1, keepdims=True), (tq, LANES)))
            p = jnp.exp(s - jnp.tile(m_new, (1, s.shape[1] // LANES)))
            alpha = jnp.exp(ms[c] - m_new)
            new_l[c] = alpha * ls[c] + jnp.broadcast_to(jnp.sum(p, axis=-1, keepdims=True), (tq, LANES))
            new_m[c] = m_new
            ps.append(p.astype(BF16))
            alphas.append(alpha)
        for (j, hd), p, alpha in zip(order, ps, alphas):
            c = j * N_HEADS + hd
            new_acc[c] = alpha[:, :V_HEAD] * accs[c] + jnp.dot(p, v_of(hd), preferred_element_type=F32)
        return tuple(new_m), tuple(new_l), tuple(new_acc)

    state = (tuple(jnp.full((tq, LANES), -jnp.inf, F32) for _ in range(nchain)),
             tuple(jnp.zeros((tq, LANES), F32) for _ in range(nchain)),
             tuple(jnp.zeros((tq, V_HEAD), F32) for _ in range(nchain)))
    if n_ctx:
        cstep = min(TK, n_ctx)
        for i in range(n_ctx // cstep):
            state = chunk(state, lambda j, hd, i=i: ckt_refs[j][hd, :, i * cstep:(i + 1) * cstep],
                          lambda hd, i=i: cv_ref[hd, i * cstep:(i + 1) * cstep, :])
    step = min(TK, n_new)

    def body(i, st):
        start = pl.multiple_of(i * step, step)
        return chunk(st, lambda j, hd: kt_refs[j][hd, :, pl.ds(start, step)],
                     lambda hd: v_ref[hd, pl.ds(start, step), :])

    _, ls, accs = lax.fori_loop(0, n_new // step, body, state)

    if n_soft == 2:
        lv = lam_ref[...]
        lam = (jnp.exp(jnp.sum(lv[0:1] * lv[1:2], axis=-1, keepdims=True))
               - jnp.exp(jnp.sum(lv[2:3] * lv[3:4], axis=-1, keepdims=True)) + lam_init)
    for hd in range(N_HEADS):
        out = accs[hd] / ls[hd][:, :V_HEAD]
        if n_soft == 2:
            a = out - lam * (accs[N_HEADS + hd] / ls[N_HEADS + hd][:, :V_HEAD])
            out = _rms(a) * g_ref[...] * (1.0 - lam_init)
        o_ref[:, hd * V_HEAD:(hd + 1) * V_HEAD] = out


def _attn_t_kernel(*refs, n_soft, n_new, n_ctx, lam_init):
    refs = list(refs)
    qt_refs, k_refs, vt_ref = refs[:n_soft], refs[n_soft:2 * n_soft], refs[2 * n_soft]
    pos = 2 * n_soft + 1
    if n_ctx:
        ck_refs, cvt_ref = refs[pos:pos + n_soft], refs[pos + n_soft]
        pos += n_soft + 1
    if n_soft == 2:
        lam_ref, g_ref = refs[pos:pos + 2]
        pos += 2
    o_ref = refs[pos]
    tq = o_ref.shape[0]
    nchain = n_soft * N_HEADS
    order = [(j, hd) for hd in range(N_HEADS) for j in range(n_soft)]

    def chunk(state, k_of, vt_of):
        ms, accs = state
        new_m, new_acc = list(ms), list(accs)
        ss = [jnp.dot(k_of(j, hd), qt_refs[j][hd], preferred_element_type=F32) for j, hd in order]
        ps, alphas = [], []
        for (j, hd), s in zip(order, ss):
            c = j * N_HEADS + hd
            s3 = s.reshape(s.shape[0] // 8, 8, tq)
            top = jnp.max(jnp.max(s3, axis=0), axis=0, keepdims=True)
            m_new = jnp.maximum(ms[c], jnp.broadcast_to(top, (8, tq)))
            ps.append(jnp.exp(s3 - m_new[None]).reshape(s.shape).astype(BF16))
            alphas.append(jnp.exp(ms[c] - m_new))
            new_m[c] = m_new
        for (j, hd), p, alpha in zip(order, ps, alphas):
            c = j * N_HEADS + hd
            scaled = (accs[c].reshape(VT_ROWS // 8, 8, tq) * alpha[None]).reshape(VT_ROWS, tq)
            new_acc[c] = scaled + jnp.dot(vt_of(hd), p, preferred_element_type=F32)
        return tuple(new_m), tuple(new_acc)

    state = (tuple(jnp.full((8, tq), -jnp.inf, F32) for _ in range(nchain)),
             tuple(jnp.zeros((VT_ROWS, tq), F32) for _ in range(nchain)))
    if n_ctx:
        cstep = min(TK, n_ctx)
        for i in range(n_ctx // cstep):
            state = chunk(state, lambda j, hd, i=i: ck_refs[j][hd, i * cstep:(i + 1) * cstep, :],
                          lambda hd, i=i: cvt_ref[hd, :, i * cstep:(i + 1) * cstep])
    step = min(TK, n_new)

    def body(i, st):
        start = pl.multiple_of(i * step, step)
        return chunk(st, lambda j, hd: k_refs[j][hd, pl.ds(start, step), :],
                     lambda hd: vt_ref[hd, :, pl.ds(start, step)])

    _, accs = lax.fori_loop(0, n_new // step, body, state)

    def normalised(c):
        num, den = accs[c][:V_HEAD], accs[c][V_HEAD:V_HEAD + 8]
        return (num.reshape(V_HEAD // 8, 8, tq) / den[None]).reshape(V_HEAD, tq)

    if n_soft == 2:
        lv = lam_ref[...]
        lam = (jnp.exp(jnp.sum(lv[0:1] * lv[1:2], axis=-1, keepdims=True))
               - jnp.exp(jnp.sum(lv[2:3] * lv[3:4], axis=-1, keepdims=True)) + lam_init)
    outs = []
    for hd in range(N_HEADS):
        out = normalised(hd)
        if n_soft == 2:
            a = out - lam * normalised(N_HEADS + hd)
            out = a * lax.rsqrt(jnp.mean(a * a, axis=0, keepdims=True) + EPS) * g_ref[...] * (1.0 - lam_init)
        outs.append(out)
    o_ref[...] = jnp.concatenate(outs, axis=0).T


def _attention_t(name, qts, ks, vt, ctx_ks, ctx_vt, extras, nb, n, n_ctx, lam_init=0.0):
    tq = min(TQ, n)
    nqt = n // tq
    n_soft = len(qts)
    in_specs = [pl.BlockSpec((N_HEADS, a.shape[1], tq), lambda b, i: (0, 0, b * nqt + i)) for a in qts]
    in_specs += [pl.BlockSpec((N_HEADS, n, a.shape[-1]), lambda b, i: (0, b, 0)) for a in ks]
    in_specs += [pl.BlockSpec((N_HEADS, VT_ROWS, n), lambda b, i: (0, 0, b))]
    args = list(qts) + list(ks) + [vt]
    if n_ctx:
        in_specs += [pl.BlockSpec((N_HEADS, n_ctx, a.shape[-1]), lambda b, i: (0, b, 0)) for a in ctx_ks]
        in_specs += [pl.BlockSpec((N_HEADS, VT_ROWS, n_ctx), lambda b, i: (0, 0, b))]
        args += list(ctx_ks) + [ctx_vt]
    in_specs += [pl.BlockSpec(a.shape, lambda b, i: (0, 0)) for a in extras]
    args += list(extras)
    return pl.pallas_call(
        functools.partial(_attn_t_kernel, n_soft=n_soft, n_new=n, n_ctx=n_ctx, lam_init=lam_init),
        grid=(nb, nqt),
        in_specs=in_specs,
        out_specs=pl.BlockSpec((tq, N_HEADS * V_HEAD), lambda b, i: (b * nqt + i, 0)),
        out_shape=jax.ShapeDtypeStruct((nb * n, N_HEADS * V_HEAD), F32),
        compiler_params=_cparams(("parallel", "parallel")),
        name=name,
    )(*args)


def _attention(name, qs, kts, v, ctx_kts, ctx_v, extras, row0, nb, n, n_ctx, lam_init=0.0):
    tq = min(TQ, n)
    nqt = n // tq
    n_soft = len(qs)
    qmap = lambda b, i: (0, row0 // tq + b * nqt + i, 0)
    in_specs = [pl.BlockSpec((N_HEADS, tq, a.shape[-1]), qmap) for a in qs]
    in_specs += [pl.BlockSpec((N_HEADS, a.shape[1], n), lambda b, i: (0, 0, row0 // n + b)) for a in kts]
    in_specs += [pl.BlockSpec((N_HEADS, n, V_HEAD), lambda b, i: (0, row0 // n + b, 0))]
    args = list(qs) + list(kts) + [v]
    if n_ctx:
        in_specs += [pl.BlockSpec((N_HEADS, a.shape[1], n_ctx), lambda b, i: (0, 0, b)) for a in ctx_kts]
        in_specs += [pl.BlockSpec((N_HEADS, n_ctx, V_HEAD), lambda b, i: (0, b, 0))]
        args += list(ctx_kts) + [ctx_v]
    in_specs += [pl.BlockSpec(a.shape, lambda b, i: (0, 0)) for a in extras]
    args += list(extras)
    return pl.pallas_call(
        functools.partial(_attn_kernel, n_soft=n_soft, n_new=n, n_ctx=n_ctx, lam_init=lam_init),
        grid=(nb, nqt),
        in_specs=in_specs,
        out_specs=pl.BlockSpec((tq, N_HEADS * V_HEAD), lambda b, i: (b * nqt + i, 0)),
        out_shape=jax.ShapeDtypeStruct((nb * n, N_HEADS * V_HEAD), F32),
        compiler_params=_cparams(("parallel", "parallel")),
        name=name,
    )(*args)


def _tri(lower):
    r = lax.broadcasted_iota(jnp.int32, (CHUNK, CHUNK), 0)
    c = lax.broadcasted_iota(jnp.int32, (CHUNK, CHUNK), 1)
    return (c <= r) if lower else (c >= r)


def _nt(a, b):
    return lax.dot_general(a, b, (((1,), (1,)), ((), ())), preferred_element_type=F32)


def _tn(a, b):
    return lax.dot_general(a, b, (((0,), (0,)), ((), ())), preferred_element_type=F32)


def _head_slices(qkv, hd):
    lo = hd * HEAD_DIM
    return qkv[:, lo:lo + HEAD_DIM], qkv[:, 256 + lo:256 + lo + HEAD_DIM], qkv[:, 512 + lo:512 + lo + HEAD_DIM]


def _row_dot(row, mat, transpose_mat):
    row8 = jnp.broadcast_to(row, (8, row.shape[1])).astype(BF16)
    dims = (((1,), (1,)), ((), ())) if transpose_mat else (((1,), (0,)), ((), ()))
    return lax.dot_general(row8, mat, dims, preferred_element_type=F32)[0:1]


def _load_state_t(dst, src):
    for d in range(2):
        for hd in range(N_HEADS):
            dst[d, hd] = src[d, hd].T


def _mlstm_step(dirs, bias_ref, ct_s, n_s, m_s):
    k_scale = HEAD_DIM ** -0.5
    lower = _tri(True).astype(F32)
    upper = _tri(False).astype(F32)
    chains = []
    for d, qkv_ref, tail_ref, _ in dirs:
        g = tail_ref[...] + bias_ref[...]
        ls = _log_sigmoid(g)
        g_t, ls_t = g.T, ls.T
        left, right = (lower, upper) if d == 0 else (upper, lower)
        cum_col = jnp.dot(left, ls, precision=HIGHEST, preferred_element_type=F32)
        cum_row = jnp.dot(ls_t, right, precision=HIGHEST, preferred_element_type=F32)
        qkv = qkv_ref[...]
        for hd in range(N_HEADS):
            ci = TAIL_GATE0 + 4 * (2 * d) + hd
            cf = ci + 4
            q, k, v = _head_slices(qkv, hd)
            chains.append(dict(
                d=d, hd=hd, q=q, k=k, v=v, li_row=g_t[ci:ci + 1, :], b_row=cum_row[cf:cf + 1, :],
                c_col=g[:, ci:ci + 1] - cum_col[:, cf:cf + 1],
                m_prev=m_s[d:d + 1, hd:hd + 1], ct=ct_s[d, hd], n=n_s[d, hd:hd + 1, :]))
    for c in chains:
        c["s"] = _nt(c["k"], c["q"])
        c["v_t"] = c["v"].T
    for c in chains:
        valid = _tri(c["d"] != 0)
        log_d = jnp.where(valid, c["b_row"] + c["c_col"], -jnp.inf)
        log_inter = c["b_row"] + c["m_prev"]
        c["m_t"] = jnp.maximum(log_inter, jnp.max(log_d, axis=0, keepdims=True))
        c["w"] = c["s"] * k_scale * jnp.exp(log_d - c["m_t"])
        c["w_inter"] = jnp.exp(log_inter - c["m_t"])
    for c in chains:
        num = (jnp.dot(c["v_t"], c["w"].astype(BF16), preferred_element_type=F32)
               + c["w_inter"] * _nt(c["ct"].astype(BF16), c["q"]))
        den = jnp.sum(c["w"], axis=0, keepdims=True) + c["w_inter"] * _row_dot(c["n"], c["q"], True)
        c["h"] = num / jnp.maximum(jnp.abs(den), jnp.exp(-c["m_t"]))
    for c in chains:
        last = CHUNK - 1 if c["d"] == 0 else 0
        c["m_new"] = c["m_t"][:, last:last + 1]
        b_last = c["b_row"][:, last:last + 1]
        w_end_row = jnp.exp(b_last - c["b_row"] + c["li_row"] - c["m_new"])
        w_end_col = jnp.exp(c["c_col"] + (b_last - c["m_new"]))
        decay = jnp.exp(b_last + c["m_prev"] - c["m_new"])
        kw = (c["k"].astype(F32) * (w_end_col * k_scale)).astype(BF16)
        c["ct_new"] = decay * c["ct"] + jnp.dot(c["v_t"], kw, preferred_element_type=F32)
        c["n_new"] = decay * c["n"] + _row_dot(w_end_row, c["k"], False) * k_scale
    for j, (_, _, _, h_ref) in enumerate(dirs):
        h_ref[...] = jnp.concatenate([c["h"] for c in chains[j * N_HEADS:(j + 1) * N_HEADS]], axis=0).T
    for c in chains:
        d, hd = c["d"], c["hd"]
        ct_s[d, hd] = c["ct_new"]
        n_s[d, hd:hd + 1, :] = c["n_new"]
        m_s[d:d + 1, hd:hd + 1] = c["m_new"]


def _mlstm_kernel(tab_ref, qkvf_ref, tailf_ref, qkvb_ref, tailb_ref, bias_ref, c0_ref, n0_ref, m0_ref,
                  hf_ref, hb_ref, cn_ref, nn_ref, mn_ref, ct_s, n_s, m_s):
    i = pl.program_id(0)

    @pl.when(tab_ref[3, i] == 1)
    def _():
        _load_state_t(ct_s, c0_ref.at[0])
        n_s[...] = n0_ref[0]
        m_s[...] = m0_ref[0]

    _mlstm_step([(0, qkvf_ref, tailf_ref, hf_ref), (1, qkvb_ref, tailb_ref, hb_ref)], bias_ref, ct_s, n_s, m_s)

    @pl.when(tab_ref[4, i] == 1)
    def _():
        _load_state_t(cn_ref.at[0], ct_s)
        nn_ref[0] = n_s[...]
        mn_ref[0] = m_s[...]


def _ret_step(dirs, lg_ref, st_s):
    k_scale = HEAD_DIM ** -0.5
    s = lax.broadcasted_iota(jnp.int32, (CHUNK, CHUNK), 0)
    t = lax.broadcasted_iota(jnp.int32, (CHUNK, CHUNK), 1)
    lane = lax.broadcasted_iota(jnp.int32, (1, CHUNK), 1)
    chains = []
    for d, qkv_ref, _ in dirs:
        lag = (t - s) if d == 0 else (s - t)
        pos = (lane if d == 0 else CHUNK - 1 - lane).astype(F32)
        qkv = qkv_ref[...]
        for hd in range(N_HEADS):
            q, k, v = _head_slices(qkv, hd)
            chains.append(dict(d=d, hd=hd, q=q, k=k, v=v, lag=lag, pos=pos, st=st_s[d, hd],
                               lg=_log_sigmoid(lg_ref[d:d + 1, hd:hd + 1])))
    for c in chains:
        c["s"] = _nt(c["k"], c["q"])
        c["v_t"] = c["v"].T
    for c in chains:
        intra = jnp.where(c["lag"] >= 0, jnp.exp(jnp.maximum(c["lag"], 0).astype(F32) * c["lg"]), 0.0)
        c["a"] = (c["s"] * k_scale * intra).astype(BF16)
    for c in chains:
        inter = jnp.exp((c["pos"] + 1.0) * c["lg"])
        c["o"] = jnp.dot(c["v_t"], c["a"], preferred_element_type=F32) + inter * _nt(c["st"].astype(BF16), c["q"])
    row = lax.broadcasted_iota(jnp.int32, (CHUNK, HEAD_DIM), 0)
    for c in chains:
        src_pos = (row if c["d"] == 0 else CHUNK - 1 - row).astype(F32)
        tail = jnp.exp((CHUNK - 1.0 - src_pos) * c["lg"]) * k_scale
        kw = (c["k"].astype(F32) * tail).astype(BF16)
        c["st_new"] = jnp.exp(CHUNK * c["lg"]) * c["st"] + jnp.dot(c["v_t"], kw, preferred_element_type=F32)
    for j, (_, _, o_ref) in enumerate(dirs):
        o_ref[...] = jnp.concatenate([c["o"] for c in chains[j * N_HEADS:(j + 1) * N_HEADS]], axis=0).T
    for c in chains:
        st_s[c["d"], c["hd"]] = c["st_new"]


def _ret_kernel(tab_ref, qkvf_ref, qkvb_ref, lg_ref, s0_ref, of_ref, ob_ref, sn_ref, st_s):
    i = pl.program_id(0)

    @pl.when(tab_ref[3, i] == 1)
    def _():
        _load_state_t(st_s, s0_ref.at[0])

    _ret_step([(0, qkvf_ref, of_ref), (1, qkvb_ref, ob_ref)], lg_ref, st_s)

    @pl.when(tab_ref[4, i] == 1)
    def _():
        _load_state_t(sn_ref.at[0], st_s)


def _scan_table(st):
    nch = st["n"] // CHUNK
    rows = [(b * nch + c, b * nch + nch - 1 - c, b, int(c == 0), int(c == nch - 1))
            for b in range(st["nb"]) for c in range(nch)]
    return jnp.asarray(np.array(rows, dtype=np.int32).T)


def _mlstm_scan(tab, mqkv, tail, bias_row, c0, n0, m0):
    t_all = mqkv.shape[0]
    nseq = c0.shape[0]
    fwd = lambda i, tab: (tab[0, i], 0)
    bwd = lambda i, tab: (tab[1, i], 0)
    st = lambda nd: (lambda i, tab: (tab[2, i],) + (0,) * nd)
    grid_spec = pltpu.PrefetchScalarGridSpec(
        num_scalar_prefetch=1,
        grid=(tab.shape[1],),
        in_specs=[pl.BlockSpec((CHUNK, 768), fwd), pl.BlockSpec((CHUNK, LANES), fwd),
                  pl.BlockSpec((CHUNK, 768), bwd), pl.BlockSpec((CHUNK, LANES), bwd),
                  pl.BlockSpec((1, LANES), lambda i, tab: (0, 0)),
                  pl.BlockSpec((1,) + c0.shape[1:], st(4)),
                  pl.BlockSpec((1,) + n0.shape[1:], st(3)),
                  pl.BlockSpec((1,) + m0.shape[1:], st(2))],
        out_specs=[pl.BlockSpec((CHUNK, 256), fwd), pl.BlockSpec((CHUNK, 256), bwd),
                   pl.BlockSpec((1,) + c0.shape[1:], st(4)),
                   pl.BlockSpec((1,) + n0.shape[1:], st(3)),
                   pl.BlockSpec((1,) + m0.shape[1:], st(2))],
        scratch_shapes=[pltpu.VMEM(c0.shape[1:], F32), pltpu.VMEM(n0.shape[1:], F32),
                        pltpu.VMEM(m0.shape[1:], F32)],
    )
    return pl.pallas_call(
        _mlstm_kernel,
        grid_spec=grid_spec,
        out_shape=[jax.ShapeDtypeStruct((t_all, 256), F32), jax.ShapeDtypeStruct((t_all, 256), F32),
                   jax.ShapeDtypeStruct(c0.shape, F32), jax.ShapeDtypeStruct(n0.shape, F32),
                   jax.ShapeDtypeStruct(m0.shape, F32)],
        compiler_params=_cparams(("arbitrary",)),
        name="mlstm_scan",
    )(tab, mqkv, tail, mqkv, tail, bias_row, c0, n0, m0)


def _ret_scan(tab, rqkv, decay_logit, s0):
    t_all = rqkv.shape[0]
    fwd = lambda i, tab: (tab[0, i], 0)
    bwd = lambda i, tab: (tab[1, i], 0)
    st = lambda i, tab: (tab[2, i], 0, 0, 0, 0)
    grid_spec = pltpu.PrefetchScalarGridSpec(
        num_scalar_prefetch=1,
        grid=(tab.shape[1],),
        in_specs=[pl.BlockSpec((CHUNK, 768), fwd), pl.BlockSpec((CHUNK, 768), bwd),
                  pl.BlockSpec(decay_logit.shape, lambda i, tab: (0, 0)),
                  pl.BlockSpec((1,) + s0.shape[1:], st)],
        out_specs=[pl.BlockSpec((CHUNK, 256), fwd), pl.BlockSpec((CHUNK, 256), bwd),
                   pl.BlockSpec((1,) + s0.shape[1:], st)],
        scratch_shapes=[pltpu.VMEM(s0.shape[1:], F32)],
    )
    return pl.pallas_call(
        _ret_kernel,
        grid_spec=grid_spec,
        out_shape=[jax.ShapeDtypeStruct((t_all, 256), F32), jax.ShapeDtypeStruct((t_all, 256), F32),
                   jax.ShapeDtypeStruct(s0.shape, F32)],
        compiler_params=_cparams(("arbitrary",)),
        name="retention_scan",
    )(tab, rqkv, rqkv, decay_logit, s0)


def _proj_kernel(x_ref, w_ref, o_ref):
    o_ref[...] = _bdot(x_ref[...], w_ref[...]).astype(o_ref.dtype)


def _project(x, w, dtype):
    return pl.pallas_call(
        _proj_kernel,
        out_shape=jax.ShapeDtypeStruct((x.shape[0], w.shape[1]), dtype),
        name="ctx_kv_proj",
    )(x, w)


def _first_argmax_mask(cur, axis, size):
    io = lax.broadcasted_iota(jnp.int32, cur.shape, axis)
    mx = jnp.max(cur, axis=axis, keepdims=True)
    ix = jnp.min(jnp.where(cur == mx, io, size), axis=axis, keepdims=True)
    return io == ix


def _route(scores_t, bias_col):
    tm = scores_t.shape[1]
    per = N_EXPERTS // N_GROUPS
    sel = scores_t + bias_col
    s3 = sel.reshape(N_GROUPS, per, tm)
    hit1 = _first_argmax_mask(s3, 1, per)
    m1 = jnp.max(s3, axis=1, keepdims=True)
    m2 = jnp.max(jnp.where(hit1, -jnp.inf, s3), axis=1, keepdims=True)
    cur = m1 + m2
    gsel = None
    for _ in range(TOPK_GROUPS):
        hit = _first_argmax_mask(cur, 0, N_GROUPS)
        gsel = hit if gsel is None else jnp.logical_or(gsel, hit)
        cur = jnp.where(hit, -jnp.inf, cur)
    cur = jnp.where(gsel, s3, -jnp.inf).reshape(N_EXPERTS, tm)
    chosen, hits = None, []
    for _ in range(TOP_K):
        hit = _first_argmax_mask(cur, 0, N_EXPERTS)
        hits.append(hit)
        chosen = hit if chosen is None else jnp.logical_or(chosen, hit)
        cur = jnp.where(hit, -jnp.inf, cur)
    w = jnp.where(chosen, scores_t, 0.0)
    return w / jnp.sum(w, axis=0, keepdims=True) * ROUTED_SCALE, chosen, hits


def _out_kernel(oa_ref, od_ref, hf_ref, hb_ref, mo_ref, of_ref, ob_ref, rg_ref,
                x_ref, mod_ref, mg_ref, npost_ref, npre_ref, wout_ref, wrt_ref, rb_ref,
                x1_ref, h2_ref, hpk_ref, eidx_ref, rank_ref, wk_ref, cnt_ref, count_s):
    @pl.when(pl.program_id(0) == 0)
    def _():
        count_s[...] = jnp.zeros(count_s.shape, F32)

    a = oa_ref[...]
    dd = od_ref[...]
    b = _rms_heads(hf_ref[...] + hb_ref[...]) * mg_ref[...] * _sigmoid(mo_ref[...])
    r = _rms_heads(of_ref[...] + ob_ref[...]) * _silu(rg_ref[...])
    mix = (jnp.dot(a.astype(BF16), wout_ref[0:256, :], preferred_element_type=F32)
           + jnp.dot(b.astype(BF16), wout_ref[256:512, :], preferred_element_type=F32)
           + jnp.dot(r.astype(BF16), wout_ref[512:768, :], preferred_element_type=F32)
           + jnp.dot(dd.astype(BF16), wout_ref[768:1024, :], preferred_element_type=F32))
    m = mod_ref[0]
    x1 = x_ref[...] + m[2:3] * (_rms(mix) * npost_ref[...])
    x1_ref[...] = x1
    h2 = (_rms(x1) * npre_ref[...] * (1.0 + m[4:5]) + m[3:4]).astype(BF16)
    h2_ref[...] = h2
    hpk_ref[...] = _pack_rows(h2)

    logits_t = lax.dot_general(wrt_ref[...], h2, (((1,), (1,)), ((), ())), preferred_element_type=F32)
    w_t, chosen, hits = _route(_sigmoid(logits_t), rb_ref[...])
    tm = w_t.shape[1]
    src = lax.broadcasted_iota(jnp.int32, (tm, tm), 0)
    dst = lax.broadcasted_iota(jnp.int32, (tm, tm), 1)
    before = jnp.where(src < dst, 1.0, 0.0).astype(BF16)
    picked = jnp.where(chosen, 1.0, 0.0)
    rank = jnp.dot(picked.astype(BF16), before, preferred_element_type=F32) + count_s[...]
    count_s[...] += jnp.sum(picked, axis=1, keepdims=True)
    cnt_ref[...] = jnp.broadcast_to(count_s[...], cnt_ref.shape)
    e_io = lax.broadcasted_iota(jnp.int32, w_t.shape, 0)
    pick = lambda hit, val: jnp.sum(jnp.where(hit, val, jnp.zeros_like(val)), axis=0, keepdims=True)
    eidx_ref[...] = jnp.concatenate([pick(h, e_io) for h in hits], axis=0)
    rank_ref[...] = jnp.concatenate([pick(h, rank) for h in hits], axis=0).astype(jnp.int32)
    wk = jnp.concatenate([pick(h, w_t) for h in hits] + [jnp.zeros((LANES - TOP_K, tm), F32)], axis=0)
    wk_ref[...] = wk.T


def _output_stage(oa, od, hf, hb, mo, of, ob, rg, x, mod, mg, npost, npre, wout, wrt, rb, st):
    t_all = x.shape[0]
    row = lambda i: (i, 0)
    const2 = lambda i: (0, 0)
    tok = lambda w: pl.BlockSpec((TM, w), row)
    full = lambda a: pl.BlockSpec(a.shape, const2)
    return pl.pallas_call(
        _out_kernel,
        grid=(t_all // TM,),
        in_specs=[tok(256)] * 8 + [tok(D_MODEL), _mod_spec(st),
                                   full(mg), full(npost), full(npre), full(wout), full(wrt), full(rb)],
        out_specs=[tok(D_MODEL), tok(D_MODEL), tok(D_MODEL // 2),
                   pl.BlockSpec((TOP_K, TM), lambda i: (0, i)), pl.BlockSpec((TOP_K, TM), lambda i: (0, i)),
                   tok(LANES), pl.BlockSpec((N_EXPERTS, LANES), const2)],
        out_shape=[jax.ShapeDtypeStruct((t_all, D_MODEL), F32),
                   jax.ShapeDtypeStruct((t_all, D_MODEL), BF16),
                   jax.ShapeDtypeStruct((t_all, D_MODEL // 2), jnp.uint32),
                   jax.ShapeDtypeStruct((TOP_K, t_all), jnp.int32),
                   jax.ShapeDtypeStruct((TOP_K, t_all), jnp.int32),
                   jax.ShapeDtypeStruct((t_all, LANES), F32),
                   jax.ShapeDtypeStruct((N_EXPERTS, LANES), F32)],
        scratch_shapes=[pltpu.VMEM((N_EXPERTS, 1), F32)],
        compiler_params=_cparams(("arbitrary",)),
        name="output_stage_" + st["name"],
    )(oa, od, hf, hb, mo, of, ob, rg, x, mod, mg, npost, npre, wout, wrt, rb)


SC_CORES, SC_SUBCORES = 2, 16
SC_WORKERS = SC_CORES * SC_SUBCORES
SLOT_BLOCK = 512
DISPATCH_ROWS = 64
COMBINE_ROWS = 64


def _sc_mesh():
    return plsc.VectorSubcoreMesh(core_axis_name="core", subcore_axis_name="subcore")


def _sc_worker():
    return lax.axis_index("subcore") * SC_CORES + lax.axis_index("core")


def _sc_dispatch(hpk, dest, n_slots):
    t_all, width = hpk.shape
    per_worker = t_all // SC_WORKERS
    assert t_all % (SC_WORKERS * DISPATCH_ROWS) == 0

    @functools.partial(
        pl.kernel, mesh=_sc_mesh(), out_type=jax.ShapeDtypeStruct((n_slots, width), hpk.dtype),
        scratch_types=[pltpu.VMEM((DISPATCH_ROWS, width), hpk.dtype)]
        + [pltpu.VMEM((DISPATCH_ROWS,), jnp.int32)] * TOP_K + [pltpu.SemaphoreType.DMA])
    def dispatch(x_hbm, d_hbm, o_hbm, rows_v, *rest):
        idx, sem = rest[:TOP_K], rest[TOP_K]

        @pl.loop(0, per_worker // DISPATCH_ROWS)
        def _(j):
            base = _sc_worker() * per_worker + j * DISPATCH_ROWS
            pltpu.sync_copy(x_hbm.at[pl.ds(base, DISPATCH_ROWS)], rows_v)
            for k in range(TOP_K):
                pltpu.sync_copy(d_hbm.at[pl.ds(k * t_all + base, DISPATCH_ROWS)], idx[k])
            copies = [pltpu.async_copy(rows_v, o_hbm.at[idx[k]], sem) for k in range(TOP_K)]
            for c in copies:
                c.wait()

    return dispatch(hpk, dest)


def _sc_combine(yb, dest, t_all):
    width = yb.shape[1]
    per_worker = t_all // SC_WORKERS
    assert t_all % (SC_WORKERS * COMBINE_ROWS) == 0

    @functools.partial(
        pl.kernel, mesh=_sc_mesh(), out_type=jax.ShapeDtypeStruct((TOP_K, t_all, width), yb.dtype),
        scratch_types=[pltpu.VMEM((COMBINE_ROWS, width), yb.dtype)] * 2
        + [pltpu.VMEM((COMBINE_ROWS,), jnp.int32)] * TOP_K + [pltpu.SemaphoreType.DMA] * 2)
    def combine(y_hbm, d_hbm, o_hbm, rows_a, rows_b, *rest):
        idx, sems = rest[:TOP_K], rest[TOP_K:]
        bufs = (rows_a, rows_b)

        @pl.loop(0, per_worker // COMBINE_ROWS)
        def _(j):
            base = _sc_worker() * per_worker + j * COMBINE_ROWS
            for k in range(TOP_K):
                pltpu.sync_copy(d_hbm.at[pl.ds(k * t_all + base, COMBINE_ROWS)], idx[k])
            pending = pltpu.async_copy(y_hbm.at[idx[0]], bufs[0], sems[0])
            for k in range(TOP_K):
                pending.wait()
                if k + 1 < TOP_K:
                    pending = pltpu.async_copy(y_hbm.at[idx[k + 1]], bufs[(k + 1) % 2], sems[(k + 1) % 2])
                pltpu.sync_copy(bufs[k % 2], o_hbm.at[k, pl.ds(base, COMBINE_ROWS)])

    return combine(yb, dest)


def _pack_rows(x):
    bits = pltpu.bitcast(x.astype(BF16).astype(F32), jnp.uint32)
    w = x.shape[-1] // 2
    return (bits[..., :w] >> 16) | (bits[..., w:] & jnp.uint32(0xFFFF0000))


def _unpack_rows(words):
    return pltpu.bitcast(words << 16, F32), pltpu.bitcast(words & jnp.uint32(0xFFFF0000), F32)


def _expert_kernel(be_ref, used_ref, xs_ref, wg_ref, wu_ref, wd_ref, y_ref, wg_s, wu_s, wd_s):
    b = pl.program_id(0)
    fresh = jnp.logical_or(b == 0, be_ref[b] != be_ref[jnp.maximum(b - 1, 0)])

    @pl.when(jnp.logical_and(fresh, b < used_ref[0]))
    def _():
        wg_s[...] = wg_ref[0, 0].astype(BF16)
        wu_s[...] = wu_ref[0, 0].astype(BF16)
        wd_s[...] = wd_ref[0, 0].astype(BF16)

    @pl.when(b < used_ref[0])
    def _():
        lo, hi = (v.astype(BF16) for v in _unpack_rows(xs_ref[...]))
        half = D_MODEL // 2
        gate = (jnp.dot(lo, wg_s[:half, :], preferred_element_type=F32)
                + jnp.dot(hi, wg_s[half:, :], preferred_element_type=F32))
        up = (jnp.dot(lo, wu_s[:half, :], preferred_element_type=F32)
              + jnp.dot(hi, wu_s[half:, :], preferred_element_type=F32))
        y = jnp.dot((_silu(gate) * up).astype(BF16), wd_s[...], preferred_element_type=F32)
        y_ref[...] = _pack_rows(y)


def _expert_blocks(layer, xs, block_expert, blocks_used, wg, wu, wdn):
    n_slots = xs.shape[0]
    wspec = lambda shape: pl.BlockSpec((1, 1) + shape, lambda b, be, used: (layer, be[b], 0, 0))
    rows = lambda b, be, used: (jnp.minimum(b, used[0] - 1), 0)
    grid_spec = pltpu.PrefetchScalarGridSpec(
        num_scalar_prefetch=2,
        grid=(n_slots // SLOT_BLOCK,),
        in_specs=[pl.BlockSpec((SLOT_BLOCK, D_MODEL // 2), rows),
                  wspec((D_MODEL, D_EXPERT)), wspec((D_MODEL, D_EXPERT)), wspec((D_EXPERT, D_MODEL))],
        out_specs=pl.BlockSpec((SLOT_BLOCK, D_MODEL // 2), rows),
        scratch_shapes=[pltpu.VMEM((D_MODEL, D_EXPERT), BF16), pltpu.VMEM((D_MODEL, D_EXPERT), BF16),
                        pltpu.VMEM((D_EXPERT, D_MODEL), BF16)])
    return pl.pallas_call(
        _expert_kernel, grid_spec=grid_spec,
        out_shape=jax.ShapeDtypeStruct((n_slots, D_MODEL // 2), jnp.uint32),
        compiler_params=_cparams(("arbitrary",)),
        name="moe_experts",
    )(block_expert, blocks_used, xs, wg, wu, wdn)


def _moe_out_kernel(g_ref, wk_ref, h_ref, x1_ref, mod_ref, npost_ref, sg_ref, su_ref, sd_ref, o_ref):
    h = h_ref[...]
    act = _silu(jnp.dot(h, sg_ref[...], preferred_element_type=F32)) * jnp.dot(h, su_ref[...],
                                                                              preferred_element_type=F32)
    moe = jnp.dot(act.astype(BF16), sd_ref[...], preferred_element_type=F32)
    wk = wk_ref[...]
    lo, hi = 0.0, 0.0
    for k in range(TOP_K):
        g_lo, g_hi = _unpack_rows(g_ref[k])
        lo = lo + g_lo * wk[:, k:k + 1]
        hi = hi + g_hi * wk[:, k:k + 1]
    moe = moe + jnp.concatenate([lo, hi], axis=-1)
    m = mod_ref[0]
    o_ref[...] = x1_ref[...] + m[5:6] * (_rms(moe) * npost_ref[...])


def _moe_out_stage(g, wk, h2, x1, mod, npost, sg, su, sd, st):
    t_all = h2.shape[0]
    row = lambda i: (i, 0)
    full = lambda a: pl.BlockSpec(a.shape, lambda i: (0, 0))
    return pl.pallas_call(
        _moe_out_kernel,
        grid=(t_all // TM,),
        in_specs=[pl.BlockSpec((TOP_K, TM, D_MODEL // 2), lambda i: (0, i, 0)), pl.BlockSpec((TM, LANES), row),
                  pl.BlockSpec((TM, D_MODEL), row), pl.BlockSpec((TM, D_MODEL), row), _mod_spec(st),
                  full(npost), full(sg), full(su), full(sd)],
        out_specs=pl.BlockSpec((TM, D_MODEL), row),
        out_shape=jax.ShapeDtypeStruct((t_all, D_MODEL), F32),
        compiler_params=_cparams(("parallel",)),
        name="moe_combine_" + st["name"],
    )(g, wk, h2, x1, mod, npost, sg, su, sd)


def _moe_stage(layer, h2, hpk, eidx, rank, wk, counts, x1, mod, npost, wg, wu, wdn, sg, su, sd, st):
    t_all = h2.shape[0]
    n_blocks = -(-(t_all * TOP_K + N_EXPERTS * (SLOT_BLOCK - 1)) // SLOT_BLOCK)
    cnt = counts[:, 0].astype(jnp.int32)
    padded = (cnt + SLOT_BLOCK - 1) // SLOT_BLOCK * SLOT_BLOCK
    pad_end = jnp.cumsum(padded)
    pad_start = pad_end - padded
    experts = jnp.arange(N_EXPERTS, dtype=jnp.int32)[:, None, None]
    dest = rank + jnp.sum(jnp.where(eidx[None] == experts, pad_start[:, None, None], 0), axis=0)
    dest = dest.reshape(TOP_K * t_all)
    block_start = jnp.arange(n_blocks, dtype=jnp.int32) * SLOT_BLOCK
    block_expert = jnp.minimum(jnp.sum((pad_end[None, :] <= block_start[:, None]).astype(jnp.int32), axis=1),
                               N_EXPERTS - 1)
    blocks_used = (pad_end[-1:] // SLOT_BLOCK).astype(jnp.int32)

    xs = _sc_dispatch(hpk, dest, n_blocks * SLOT_BLOCK)
    yb = _expert_blocks(layer, xs, block_expert, blocks_used, wg, wu, wdn)
    g = _sc_combine(yb, dest, t_all)
    return _moe_out_stage(g, wk, h2, x1, mod, npost, sg, su, sd, st)


_PERM32 = np.concatenate([np.arange(8, 16), np.arange(0, 8), np.arange(24, 32), np.arange(16, 24)])


def _rope_tables(n):
    pos = jnp.arange(n)
    quarter = QK_ROPE // 4
    inv = 1.0 / (ROPE_THETA ** (jnp.arange(quarter, dtype=F32) / quarter))
    ang_r = (pos // GRID_W).astype(F32)[:, None] * inv[None, :]
    ang_c = (pos % GRID_W).astype(F32)[:, None] * inv[None, :]
    cos32 = jnp.concatenate([jnp.cos(ang_r)] * 2 + [jnp.cos(ang_c)] * 2, axis=-1)
    sin32 = jnp.concatenate([-jnp.sin(ang_r), jnp.sin(ang_r), -jnp.sin(ang_c), jnp.sin(ang_c)], axis=-1)
    cos256, sin256 = jnp.tile(cos32, (1, 8)), jnp.tile(sin32, (1, 8))
    one, zero = jnp.ones((n, QK_NOPE), F32), jnp.zeros((n, QK_NOPE), F32)
    cosq = jnp.concatenate([one, cos32] * N_HEADS, axis=-1)
    sinq = jnp.concatenate([zero, sin32] * N_HEADS, axis=-1)
    return cos256, sin256, cosq, sinq


def _split_w_in(w_in_l):
    sizes = (256, 256, 256, 256, 256, 256, 256, 16, 256, 256, 256, 256, Q_LORA, KV_LORA, QK_ROPE)
    idx = np.cumsum(sizes)[:-1]
    return jnp.split(w_in_l, [int(v) for v in idx], axis=-1)


def _layer_weights(w_in_l, w_uq_l):
    (a_q, a_k, a_v, m_q, m_k, m_v, m_o, m_g, r_q, r_k, r_v, r_g, d_cq, d_ckv, d_kr) = _split_w_in(w_in_l)
    perm256 = np.concatenate([_PERM32 + 32 * j for j in range(8)])
    pad = lambda w, n: jnp.pad(w, ((0, 0), (0, n - w.shape[1])))
    tail = pad(jnp.concatenate([d_kr, m_g], axis=-1), LANES)
    tail_s = pad(d_kr[:, _PERM32], LANES)
    w_all = jnp.concatenate([a_q, a_k, a_v, m_q, m_k, m_v, m_o, r_q, r_k, r_v, r_g, d_cq, d_ckv, tail,
                             a_q[:, perm256], a_k[:, perm256], tail_s], axis=-1).astype(BF16)
    per = QK_NOPE + QK_ROPE
    permq = np.concatenate([np.concatenate([np.arange(QK_NOPE), QK_NOPE + _PERM32]) + per * j
                            for j in range(N_HEADS)])
    return w_all, w_uq_l.astype(BF16), w_uq_l[:, permq].astype(BF16)


def kernel(x_prompt, x_sample, cache_diff_k, cache_diff_v, state_mlstm_C, state_mlstm_n, state_mlstm_m, state_ret_S, cache_mla_ckv, cache_mla_krope, c, c_ctx, w_mod, b_mod, norm_pre, norm_post, w_in, w_out, diff_lambda, diff_norm, mlstm_gate_bias, mlstm_norm, ret_decay_logit, mla_q_norm, mla_w_uq, mla_kv_norm, mla_w_ukv, moe_w_router, moe_router_bias, moe_w_gate, moe_w_up, moe_w_down, shared_w_gate, shared_w_up, shared_w_down):
    bp, n_p, _ = x_prompt.shape
    bs, n_s, _ = x_sample.shape
    depth = w_in.shape[0]
    past = cache_diff_k.shape[2]
    assert n_p % TM == 0 and n_s % TM == 0 and n_p % CHUNK == 0 and n_s % min(TK, n_s) == 0
    assert past % min(TK, past) == 0 and bs + 1 <= 8 and n_s % GRID_W == 0

    streams = (dict(name="ctx", nb=bp, n=n_p, latent=False), dict(name="latent", nb=bs, n=n_s, latent=True))
    xs_by_stream = [x_prompt.reshape(bp * n_p, D_MODEL), x_sample.reshape(bs * n_s, D_MODEL)]
    cond = jnp.zeros((8, D_MODEL), F32).at[0].set(c_ctx).at[1:1 + bs].set(c)
    mod_all = _modulation(cond, w_mod, b_mod).reshape(depth, 8, 6, D_MODEL)
    tabs = _rope_tables(n_s)
    scan_tabs = [_scan_table(st) for st in streams]

    outs = [[] for _ in range(8)]
    for l in range(depth):
        lam_init = 0.8 - 0.6 * math.exp(-0.3 * l)
        mod = mod_all[l]
        w_all, wuq, wuqs = _layer_weights(w_in[l], mla_w_uq[l])
        wukv = mla_w_ukv[l].astype(BF16)
        wout, wrt, rbias = w_out[l].astype(BF16), moe_w_router[l].T.astype(BF16), moe_router_bias[l][:, None]
        shared = [w[l].astype(BF16) for w in (shared_w_gate, shared_w_up, shared_w_down)]
        lamv, dg = diff_lambda[l], diff_norm[l][:, None]
        bias_row = jnp.zeros((1, LANES), F32).at[0, TAIL_GATE0:TAIL_GATE0 + 16].set(mlstm_gate_bias[l].reshape(16))

        with_ones = lambda vt: jnp.concatenate([vt, jnp.ones((N_HEADS, VT_ONES, bs * past), BF16)], axis=1)
        ck = jnp.transpose(cache_diff_k[:, l], (2, 0, 1, 3)).reshape(N_HEADS, bs * past, HEAD_DIM).astype(BF16)
        cvt = with_ones(jnp.transpose(cache_diff_v[:, l], (2, 3, 0, 1)).reshape(N_HEADS, HEAD_DIM, bs * past)
                        .astype(BF16))
        kvc = _project(cache_mla_ckv[:, l].reshape(bs * past, KV_LORA), wukv, BF16)
        kvc = kvc.reshape(bs * past, N_HEADS, QK_NOPE + V_HEAD)
        krc = jnp.broadcast_to(cache_mla_krope[:, l].reshape(bs * past, 1, QK_ROPE).astype(BF16),
                               (bs * past, N_HEADS, QK_ROPE))
        kc = jnp.transpose(jnp.concatenate([kvc[..., :QK_NOPE], krc], axis=-1), (1, 0, 2))
        vct = with_ones(jnp.transpose(kvc[..., QK_NOPE:], (1, 2, 0)))

        for si, st in enumerate(streams):
            x, nb, n = xs_by_stream[si], st["nb"], st["n"]
            (aq1t, aq2t, ak1, ak2, avt, ak, av, mqkv, mo, rqkv, rg, tail, qmlat, ckv, kmla, vmlat) = _input_stage(
                x, mod, norm_pre[l, 0:1], w_all, tabs, mla_q_norm[l][None], wuq, wuqs,
                mla_kv_norm[l][None], wukv, st)
            if st["latent"]:
                oa = _attention_t("diff_attn_latent", [aq1t, aq2t], [ak1, ak2], avt,
                                  [ck[..., :DIFF_HALF], ck[..., DIFF_HALF:]], cvt, [lamv, dg], nb, n, past, lam_init)
                od = _attention_t("mla_attn_latent", [qmlat], [kmla], vmlat, [kc], vct, [], nb, n, past)
                states = [s[:, l] for s in (state_mlstm_C, state_mlstm_n, state_mlstm_m, state_ret_S)]
            else:
                oa = _attention_t("diff_attn_ctx", [aq1t, aq2t], [ak1, ak2], avt, None, None, [lamv, dg],
                                  nb, n, 0, lam_init)
                od = _attention_t("mla_attn_ctx", [qmlat], [kmla], vmlat, None, None, [], nb, n, 0)
                states = [jnp.zeros((nb,) + s.shape[2:], F32)
                          for s in (state_mlstm_C, state_mlstm_n, state_mlstm_m, state_ret_S)]
            hf, hb, c_n, n_n, m_n = _mlstm_scan(scan_tabs[si], mqkv, tail, bias_row, *states[:3])
            of, ob, s_n = _ret_scan(scan_tabs[si], rqkv, ret_decay_logit[l], states[3])

            x1, h2, hpk, eidx, rank, wk, counts = _output_stage(
                oa, od, hf, hb, mo, of, ob, rg, x, mod, mlstm_norm[l][None], norm_post[l, 0:1],
                norm_pre[l, 1:2], wout, wrt, rbias, st)
            xs_by_stream[si] = _moe_stage(l, h2, hpk, eidx, rank, wk, counts, x1, mod, norm_post[l, 1:2],
                                          moe_w_gate, moe_w_up, moe_w_down, *shared, st)
            if not st["latent"]:
                new = (ak.reshape(nb, n, N_HEADS, HEAD_DIM), av.reshape(nb, n, N_HEADS, HEAD_DIM), c_n, n_n, m_n,
                       s_n, ckv.reshape(nb, n, KV_LORA), tail[:, :QK_ROPE].reshape(nb, n, QK_ROPE))
                for o, v in zip(outs, new):
                    o.append(v)

    return (xs_by_stream[0].reshape(bp, n_p, D_MODEL), xs_by_stream[1].reshape(bs, n_s, D_MODEL)) + tuple(
        jnp.stack(o, axis=1) for o in outs)
```

```python
import functools
import math

import numpy as np
import jax
import jax.numpy as jnp
from jax import lax
from jax.experimental import pallas as pl
from jax.experimental.pallas import tpu as pltpu
from jax.experimental.pallas import tpu_sc as plsc

F32 = jnp.float32
BF16 = jnp.bfloat16
HIGHEST = lax.Precision.HIGHEST

D_MODEL = 1024
GRID_W = 64
GROUP_WIDTH = 256
HEAD_DIM = 64
N_HEADS = 4
DIFF_HALF = 32
ROPE_THETA = 10000.0
Q_LORA = 256
KV_LORA = 128
QK_NOPE = 64
QK_ROPE = 32
V_HEAD = 64
N_EXPERTS = 64
TOP_K = 8
N_GROUPS = 8
TOPK_GROUPS = 4
D_EXPERT = 256
ROUTED_SCALE = 2.5
CHUNK = 128
EPS = 1e-6
Q_MLA = N_HEADS * (QK_NOPE + QK_ROPE)
KV_MLA = N_HEADS * (QK_NOPE + V_HEAD)

LANES = 128
VMEM_LIMIT = 56 * 1024 * 1024

TM = 256
TQ = 256
TK = 4096

C_AQ, C_AK, C_AV = 0, 256, 512
C_MQKV, C_MO = 768, 1536
C_RQKV, C_RG = 1792, 2560
C_CQ, C_CKV, C_TAIL = 2816, 3072, 3200
C_AQS, C_AKS, C_TAILS = 3328, 3584, 3840
W_ALL = 3968
TAIL_GATE0 = QK_ROPE
VT_ONES = 16
VT_ROWS = V_HEAD + VT_ONES


def _cparams(sem, flags=None):
    return pltpu.CompilerParams(dimension_semantics=sem, vmem_limit_bytes=VMEM_LIMIT, flags=flags)


def _rms(x):
    return x * lax.rsqrt(jnp.mean(x * x, axis=-1, keepdims=True) + EPS)


def _head_mean_matrix(width):
    r = lax.broadcasted_iota(jnp.int32, (width, width), 0) // HEAD_DIM
    c = lax.broadcasted_iota(jnp.int32, (width, width), 1) // HEAD_DIM
    return jnp.where(r == c, 1.0 / HEAD_DIM, 0.0).astype(F32)


def _rms_heads(x):
    ms = jnp.dot(x * x, _head_mean_matrix(x.shape[-1]), precision=HIGHEST, preferred_element_type=F32)
    return x * lax.rsqrt(ms + EPS)


def _sigmoid(x):
    return 1.0 / (1.0 + jnp.exp(-x))


def _silu(x):
    return x * _sigmoid(x)


def _log_sigmoid(x):
    return jnp.minimum(x, 0.0) - jnp.log1p(jnp.exp(-jnp.abs(x)))


def _bdot(a, b):
    return jnp.dot(a.astype(BF16), b.astype(BF16), preferred_element_type=F32)


def _mod_kernel(c_ref, w_ref, b_ref, o_ref):
    o_ref[0] = _bdot(_silu(c_ref[...]), w_ref[0]) + b_ref[0]


def _modulation(cond, w_mod, b_mod):
    depth, _, n = w_mod.shape
    tn = 1536
    return pl.pallas_call(
        _mod_kernel,
        grid=(depth, n // tn),
        in_specs=[pl.BlockSpec((8, D_MODEL), lambda l, j: (0, 0)),
                  pl.BlockSpec((1, D_MODEL, tn), lambda l, j: (l, 0, j)),
                  pl.BlockSpec((1, 1, tn), lambda l, j: (l, 0, j))],
        out_specs=pl.BlockSpec((1, 8, tn), lambda l, j: (l, 0, j)),
        out_shape=jax.ShapeDtypeStruct((depth, 8, n), F32),
        compiler_params=_cparams(("parallel", "parallel")),
        name="adaln_mod",
    )(cond, w_mod, b_mod.reshape(depth, 1, n))


def _in_kernel(x_ref, mod_ref, npre_ref, w_ref, cos_ref, sin_ref, cosq_ref, sinq_ref,
               qg_ref, wuq_ref, wuqs_ref, kvg_ref, wukv_ref,
               aq1_ref, aq2_ref, ak1t_ref, ak2t_ref, avh_ref, ak_ref, av_ref,
               mqkv_ref, mo_ref, rqkv_ref, rg_ref, tail_ref,
               qmla_ref, ckv_ref, kmlat_ref, vmla_ref, *, latent):
    m = mod_ref[0]
    h = (_rms(x_ref[...]) * npre_ref[...] * (1.0 + m[1:2]) + m[0:1]).astype(BF16)

    def proj(c0, width):
        return jnp.dot(h, w_ref[:, c0:c0 + width], preferred_element_type=F32)

    def rotated(c0, c0_swapped, width, cos, sin):
        return proj(c0, width) * cos + proj(c0_swapped, width) * sin if latent else proj(c0, width)

    cos = cos_ref[...]
    sin = sin_ref[...]
    aq = rotated(C_AQ, C_AQS, 256, cos, sin) * (DIFF_HALF ** -0.5)
    ak = rotated(C_AK, C_AKS, 256, cos, sin)
    av = proj(C_AV, 256)
    ak_ref[...] = ak
    av_ref[...] = av
    aq_t, av_t = aq.T.astype(BF16), av.T.astype(BF16)
    ones = jnp.ones((VT_ONES, aq_t.shape[1]), BF16)
    for hd in range(N_HEADS):
        lo = hd * HEAD_DIM
        aq1_ref[hd] = aq_t[lo:lo + DIFF_HALF, :]
        aq2_ref[hd] = aq_t[lo + DIFF_HALF:lo + HEAD_DIM, :]
        ak1t_ref[hd] = ak[:, lo:lo + DIFF_HALF].astype(BF16)
        ak2t_ref[hd] = ak[:, lo + DIFF_HALF:lo + HEAD_DIM].astype(BF16)
        avh_ref[hd, 0:HEAD_DIM, :] = av_t[lo:lo + HEAD_DIM, :]
        avh_ref[hd, HEAD_DIM:VT_ROWS, :] = ones

    mqkv_ref[...] = proj(C_MQKV, 768).astype(BF16)
    mo_ref[...] = proj(C_MO, 256)
    rqkv_ref[...] = proj(C_RQKV, 768).astype(BF16)
    rg_ref[...] = proj(C_RG, 256)

    tail = proj(C_TAIL, LANES)
    tail_ref[...] = tail
    kr = tail * cos[:, :LANES] + proj(C_TAILS, LANES) * sin[:, :LANES] if latent else tail
    kr = kr[:, :QK_ROPE].astype(BF16)

    cqn = (_rms(proj(C_CQ, Q_LORA)) * qg_ref[...]).astype(BF16)
    qmla = jnp.dot(cqn, wuq_ref[...], preferred_element_type=F32)
    if latent:
        qmla = qmla * cosq_ref[...] + jnp.dot(cqn, wuqs_ref[...], preferred_element_type=F32) * sinq_ref[...]
    qmla_t = (qmla * ((QK_NOPE + QK_ROPE) ** -0.5)).T.astype(BF16)
    for hd in range(N_HEADS):
        lo = hd * (QK_NOPE + QK_ROPE)
        qmla_ref[hd] = qmla_t[lo:lo + QK_NOPE + QK_ROPE, :]
    ckvn = _rms(proj(C_CKV, KV_LORA)) * kvg_ref[...]
    ckv_ref[...] = ckvn
    kv = jnp.dot(ckvn.astype(BF16), wukv_ref[...], preferred_element_type=F32)
    kv_t = kv.T.astype(BF16)
    per = QK_NOPE + V_HEAD
    for hd in range(N_HEADS):
        kmlat_ref[hd, :, 0:QK_NOPE] = kv[:, hd * per:hd * per + QK_NOPE].astype(BF16)
        kmlat_ref[hd, :, QK_NOPE:QK_NOPE + QK_ROPE] = kr
        vmla_ref[hd, 0:V_HEAD, :] = kv_t[hd * per + QK_NOPE:(hd + 1) * per, :]
        vmla_ref[hd, V_HEAD:VT_ROWS, :] = ones


def _mod_spec(st):
    tps = st["n"] // TM
    return pl.BlockSpec((1, 6, D_MODEL), lambda i, *_: (1 + i // tps if st["latent"] else 0, 0, 0))


def _input_stage(x, mod, npre, w_all, tabs, qg, wuq, wuqs, kvg, wukv, st):
    t_all = x.shape[0]
    tps = st["n"] // TM
    row = lambda i: (i, 0)
    tab = lambda w: pl.BlockSpec((TM, w), lambda i: (i % tps if st["latent"] else 0, 0))
    hrow = lambda i: (0, i, 0)
    const2 = lambda i: (0, 0)
    tok = lambda w: pl.BlockSpec((TM, w), row)
    headed = lambda w: pl.BlockSpec((N_HEADS, TM, w), hrow)
    headed_t = lambda d: pl.BlockSpec((N_HEADS, d, TM), lambda i: (0, 0, i))
    full = lambda a: pl.BlockSpec(a.shape, const2)
    cos, sin, cosq, sinq = tabs
    out_shapes = [
        (headed_t(DIFF_HALF), (N_HEADS, DIFF_HALF, t_all), BF16),
        (headed_t(DIFF_HALF), (N_HEADS, DIFF_HALF, t_all), BF16),
        (headed(DIFF_HALF), (N_HEADS, t_all, DIFF_HALF), BF16),
        (headed(DIFF_HALF), (N_HEADS, t_all, DIFF_HALF), BF16),
        (headed_t(VT_ROWS), (N_HEADS, VT_ROWS, t_all), BF16),
        (tok(256), (t_all, 256), F32),
        (tok(256), (t_all, 256), F32),
        (tok(768), (t_all, 768), BF16),
        (tok(256), (t_all, 256), F32),
        (tok(768), (t_all, 768), BF16),
        (tok(256), (t_all, 256), F32),
        (tok(LANES), (t_all, LANES), F32),
        (headed_t(QK_NOPE + QK_ROPE), (N_HEADS, QK_NOPE + QK_ROPE, t_all), BF16),
        (tok(KV_LORA), (t_all, KV_LORA), F32),
        (headed(QK_NOPE + QK_ROPE), (N_HEADS, t_all, QK_NOPE + QK_ROPE), BF16),
        (headed_t(VT_ROWS), (N_HEADS, VT_ROWS, t_all), BF16),
    ]
    return pl.pallas_call(
        functools.partial(_in_kernel, latent=st["latent"]),
        grid=(t_all // TM,),
        in_specs=[tok(D_MODEL), _mod_spec(st),
                  full(npre), full(w_all), tab(256), tab(256), tab(Q_MLA), tab(Q_MLA),
                  full(qg), full(wuq), full(wuqs), full(kvg), full(wukv)],
        out_specs=[s for s, _, _ in out_shapes],
        out_shape=[jax.ShapeDtypeStruct(shp, dt) for _, shp, dt in out_shapes],
        compiler_params=_cparams(("parallel",)),
        name="input_stage_" + st["name"],
    )(x, mod, npre, w_all, cos, sin, cosq, sinq, qg, wuq, wuqs, kvg, wukv)


def _attn_t_kernel(*refs, n_soft, n_new, n_ctx, lam_init):
    refs = list(refs)
    qt_refs, k_refs, vt_ref = refs[:n_soft], refs[n_soft:2 * n_soft], refs[2 * n_soft]
    pos = 2 * n_soft + 1
    if n_ctx:
        ck_refs, cvt_ref = refs[pos:pos + n_soft], refs[pos + n_soft]
        pos += n_soft + 1
    if n_soft == 2:
        lam_ref, g_ref = refs[pos:pos + 2]
        pos += 2
    o_ref = refs[pos]
    tq = o_ref.shape[0]
    nchain = n_soft * N_HEADS
    order = [(j, hd) for hd in range(N_HEADS) for j in range(n_soft)]

    def chunk(state, k_of, vt_of):
        ms, accs = state
        new_m, new_acc = list(ms), list(accs)
        ss = [jnp.dot(k_of(j, hd), qt_refs[j][hd], preferred_element_type=F32) for j, hd in order]
        ps, alphas = [], []
        for (j, hd), s in zip(order, ss):
            c = j * N_HEADS + hd
            s3 = s.reshape(s.shape[0] // 8, 8, tq)
            top = jnp.max(jnp.max(s3, axis=0), axis=0, keepdims=True)
            m_new = jnp.maximum(ms[c], jnp.broadcast_to(top, (8, tq)))
            ps.append(jnp.exp(s3 - m_new[None]).reshape(s.shape).astype(BF16))
            alphas.append(jnp.exp(ms[c] - m_new))
            new_m[c] = m_new
        for (j, hd), p, alpha in zip(order, ps, alphas):
            c = j * N_HEADS + hd
            scaled = (accs[c].reshape(VT_ROWS // 8, 8, tq) * alpha[None]).reshape(VT_ROWS, tq)
            new_acc[c] = scaled + jnp.dot(vt_of(hd), p, preferred_element_type=F32)
        return tuple(new_m), tuple(new_acc)

    state = (tuple(jnp.full((8, tq), -jnp.inf, F32) for _ in range(nchain)),
             tuple(jnp.zeros((VT_ROWS, tq), F32) for _ in range(nchain)))
    if n_ctx:
        cstep = min(TK, n_ctx)
        for i in range(n_ctx // cstep):
            state = chunk(state, lambda j, hd, i=i: ck_refs[j][hd, i * cstep:(i + 1) * cstep, :],
                          lambda hd, i=i: cvt_ref[hd, :, i * cstep:(i + 1) * cstep])
    step = min(TK, n_new)

    def body(i, st):
        start = pl.multiple_of(i * step, step)
        return chunk(st, lambda j, hd: k_refs[j][hd, pl.ds(start, step), :],
                     lambda hd: vt_ref[hd, :, pl.ds(start, step)])

    _, accs = lax.fori_loop(0, n_new // step, body, state)

    def normalised(c):
        num, den = accs[c][:V_HEAD], accs[c][V_HEAD:V_HEAD + 8]
        return (num.reshape(V_HEAD // 8, 8, tq) / den[None]).reshape(V_HEAD, tq)

    if n_soft == 2:
        lv = lam_ref[...]
        lam = (jnp.exp(jnp.sum(lv[0:1] * lv[1:2], axis=-1, keepdims=True))
               - jnp.exp(jnp.sum(lv[2:3] * lv[3:4], axis=-1, keepdims=True)) + lam_init)
    outs = []
    for hd in range(N_HEADS):
        out = normalised(hd)
        if n_soft == 2:
            a = out - lam * normalised(N_HEADS + hd)
            out = a * lax.rsqrt(jnp.mean(a * a, axis=0, keepdims=True) + EPS) * g_ref[...] * (1.0 - lam_init)
        outs.append(out)
    o_ref[...] = jnp.concatenate(outs, axis=0).T


def _attention_t(name, qts, ks, vt, ctx_ks, ctx_vt, extras, nb, n, n_ctx, lam_init=0.0):
    tq = min(TQ, n)
    nqt = n // tq
    n_soft = len(qts)
    in_specs = [pl.BlockSpec((N_HEADS, a.shape[1], tq), lambda b, i: (0, 0, b * nqt + i)) for a in qts]
    in_specs += [pl.BlockSpec((N_HEADS, n, a.shape[-1]), lambda b, i: (0, b, 0)) for a in ks]
    in_specs += [pl.BlockSpec((N_HEADS, VT_ROWS, n), lambda b, i: (0, 0, b))]
    args = list(qts) + list(ks) + [vt]
    if n_ctx:
        in_specs += [pl.BlockSpec((N_HEADS, n_ctx, a.shape[-1]), lambda b, i: (0, b, 0)) for a in ctx_ks]
        in_specs += [pl.BlockSpec((N_HEADS, VT_ROWS, n_ctx), lambda b, i: (0, 0, b))]
        args += list(ctx_ks) + [ctx_vt]
    in_specs += [pl.BlockSpec(a.shape, lambda b, i: (0, 0)) for a in extras]
    args += list(extras)
    return pl.pallas_call(
        functools.partial(_attn_t_kernel, n_soft=n_soft, n_new=n, n_ctx=n_ctx, lam_init=lam_init),
        grid=(nb, nqt),
        in_specs=in_specs,
        out_specs=pl.BlockSpec((tq, N_HEADS * V_HEAD), lambda b, i: (b * nqt + i, 0)),
        out_shape=jax.ShapeDtypeStruct((nb * n, N_HEADS * V_HEAD), F32),
        compiler_params=_cparams(("parallel", "parallel")),
        name=name,
    )(*args)


def _tri(lower):
    r = lax.broadcasted_iota(jnp.int32, (CHUNK, CHUNK), 0)
    c = lax.broadcasted_iota(jnp.int32, (CHUNK, CHUNK), 1)
    return (c <= r) if lower else (c >= r)


def _nt(a, b):
    return lax.dot_general(a, b, (((1,), (1,)), ((), ())), preferred_element_type=F32)


def _tn(a, b):
    return lax.dot_general(a, b, (((0,), (0,)), ((), ())), preferred_element_type=F32)


def _head_slices(qkv, hd):
    lo = hd * HEAD_DIM
    return qkv[:, lo:lo + HEAD_DIM], qkv[:, 256 + lo:256 + lo + HEAD_DIM], qkv[:, 512 + lo:512 + lo + HEAD_DIM]


def _row_dot(row, mat, transpose_mat):
    row8 = jnp.broadcast_to(row, (8, row.shape[1])).astype(BF16)
    dims = (((1,), (1,)), ((), ())) if transpose_mat else (((1,), (0,)), ((), ()))
    return lax.dot_general(row8, mat, dims, preferred_element_type=F32)[0:1]


def _load_state_t(dst, src):
    for d in range(2):
        for hd in range(N_HEADS):
            dst[d, hd] = src[d, hd].T


def _mlstm_step(dirs, bias_ref, ct_s, n_s, m_s):
    k_scale = HEAD_DIM ** -0.5
    lower = _tri(True).astype(F32)
    upper = _tri(False).astype(F32)
    chains = []
    for d, qkv_ref, tail_ref, _ in dirs:
        g = tail_ref[...] + bias_ref[...]
        ls = _log_sigmoid(g)
        g_t, ls_t = g.T, ls.T
        left, right = (lower, upper) if d == 0 else (upper, lower)
        cum_col = jnp.dot(left, ls, precision=HIGHEST, preferred_element_type=F32)
        cum_row = jnp.dot(ls_t, right, precision=HIGHEST, preferred_element_type=F32)
        qkv = qkv_ref[...]
        for hd in range(N_HEADS):
            ci = TAIL_GATE0 + 4 * (2 * d) + hd
            cf = ci + 4
            q, k, v = _head_slices(qkv, hd)
            chains.append(dict(
                d=d, hd=hd, q=q, k=k, v=v, li_row=g_t[ci:ci + 1, :], b_row=cum_row[cf:cf + 1, :],
                c_col=g[:, ci:ci + 1] - cum_col[:, cf:cf + 1],
                m_prev=m_s[d:d + 1, hd:hd + 1], ct=ct_s[d, hd], n=n_s[d, hd:hd + 1, :]))
    for c in chains:
        c["s"] = _nt(c["k"], c["q"])
        c["v_t"] = c["v"].T
    for c in chains:
        valid = _tri(c["d"] != 0)
        log_d = jnp.where(valid, c["b_row"] + c["c_col"], -jnp.inf)
        log_inter = c["b_row"] + c["m_prev"]
        c["m_t"] = jnp.maximum(log_inter, jnp.max(log_d, axis=0, keepdims=True))
        c["w"] = c["s"] * k_scale * jnp.exp(log_d - c["m_t"])
        c["w_inter"] = jnp.exp(log_inter - c["m_t"])
    for c in chains:
        num = (jnp.dot(c["v_t"], c["w"].astype(BF16), preferred_element_type=F32)
               + c["w_inter"] * _nt(c["ct"].astype(BF16), c["q"]))
        den = jnp.sum(c["w"], axis=0, keepdims=True) + c["w_inter"] * _row_dot(c["n"], c["q"], True)
        c["h"] = num / jnp.maximum(jnp.abs(den), jnp.exp(-c["m_t"]))
    for c in chains:
        last = CHUNK - 1 if c["d"] == 0 else 0
        c["m_new"] = c["m_t"][:, last:last + 1]
        b_last = c["b_row"][:, last:last + 1]
        w_end_row = jnp.exp(b_last - c["b_row"] + c["li_row"] - c["m_new"])
        w_end_col = jnp.exp(c["c_col"] + (b_last - c["m_new"]))
        decay = jnp.exp(b_last + c["m_prev"] - c["m_new"])
        kw = (c["k"].astype(F32) * (w_end_col * k_scale)).astype(BF16)
        c["ct_new"] = decay * c["ct"] + jnp.dot(c["v_t"], kw, preferred_element_type=F32)
        c["n_new"] = decay * c["n"] + _row_dot(w_end_row, c["k"], False) * k_scale
    for j, (_, _, _, h_ref) in enumerate(dirs):
        h_ref[...] = jnp.concatenate([c["h"] for c in chains[j * N_HEADS:(j + 1) * N_HEADS]], axis=0).T
    for c in chains:
        d, hd = c["d"], c["hd"]
        ct_s[d, hd] = c["ct_new"]
        n_s[d, hd:hd + 1, :] = c["n_new"]
        m_s[d:d + 1, hd:hd + 1] = c["m_new"]


def _mlstm_kernel(tab_ref, qkvf_ref, tailf_ref, qkvb_ref, tailb_ref, bias_ref, c0_ref, n0_ref, m0_ref,
                  hf_ref, hb_ref, cn_ref, nn_ref, mn_ref, ct_s, n_s, m_s):
    i = pl.program_id(0)

    @pl.when(tab_ref[3, i] == 1)
    def _():
        _load_state_t(ct_s, c0_ref.at[0])
        n_s[...] = n0_ref[0]
        m_s[...] = m0_ref[0]

    _mlstm_step([(0, qkvf_ref, tailf_ref, hf_ref), (1, qkvb_ref, tailb_ref, hb_ref)], bias_ref, ct_s, n_s, m_s)

    @pl.when(tab_ref[4, i] == 1)
    def _():
        _load_state_t(cn_ref.at[0], ct_s)
        nn_ref[0] = n_s[...]
        mn_ref[0] = m_s[...]


def _ret_step(dirs, lg_ref, st_s):
    k_scale = HEAD_DIM ** -0.5
    s = lax.broadcasted_iota(jnp.int32, (CHUNK, CHUNK), 0)
    t = lax.broadcasted_iota(jnp.int32, (CHUNK, CHUNK), 1)
    lane = lax.broadcasted_iota(jnp.int32, (1, CHUNK), 1)
    chains = []
    for d, qkv_ref, _ in dirs:
        lag = (t - s) if d == 0 else (s - t)
        pos = (lane if d == 0 else CHUNK - 1 - lane).astype(F32)
        qkv = qkv_ref[...]
        for hd in range(N_HEADS):
            q, k, v = _head_slices(qkv, hd)
            chains.append(dict(d=d, hd=hd, q=q, k=k, v=v, lag=lag, pos=pos, st=st_s[d, hd],
                               lg=_log_sigmoid(lg_ref[d:d + 1, hd:hd + 1])))
    for c in chains:
        c["s"] = _nt(c["k"], c["q"])
        c["v_t"] = c["v"].T
    for c in chains:
        intra = jnp.where(c["lag"] >= 0, jnp.exp(jnp.maximum(c["lag"], 0).astype(F32) * c["lg"]), 0.0)
        c["a"] = (c["s"] * k_scale * intra).astype(BF16)
    for c in chains:
        inter = jnp.exp((c["pos"] + 1.0) * c["lg"])
        c["o"] = jnp.dot(c["v_t"], c["a"], preferred_element_type=F32) + inter * _nt(c["st"].astype(BF16), c["q"])
    row = lax.broadcasted_iota(jnp.int32, (CHUNK, HEAD_DIM), 0)
    for c in chains:
        src_pos = (row if c["d"] == 0 else CHUNK - 1 - row).astype(F32)
        tail = jnp.exp((CHUNK - 1.0 - src_pos) * c["lg"]) * k_scale
        kw = (c["k"].astype(F32) * tail).astype(BF16)
        c["st_new"] = jnp.exp(CHUNK * c["lg"]) * c["st"] + jnp.dot(c["v_t"], kw, preferred_element_type=F32)
    for j, (_, _, o_ref) in enumerate(dirs):
        o_ref[...] = jnp.concatenate([c["o"] for c in chains[j * N_HEADS:(j + 1) * N_HEADS]], axis=0).T
    for c in chains:
        st_s[c["d"], c["hd"]] = c["st_new"]


def _ret_kernel(tab_ref, qkvf_ref, qkvb_ref, lg_ref, s0_ref, of_ref, ob_ref, sn_ref, st_s):
    i = pl.program_id(0)

    @pl.when(tab_ref[3, i] == 1)
    def _():
        _load_state_t(st_s, s0_ref.at[0])

    _ret_step([(0, qkvf_ref, of_ref), (1, qkvb_ref, ob_ref)], lg_ref, st_s)

    @pl.when(tab_ref[4, i] == 1)
    def _():
        _load_state_t(sn_ref.at[0], st_s)


def _scan_table(st):
    nch = st["n"] // CHUNK
    rows = [(b * nch + c, b * nch + nch - 1 - c, b, int(c == 0), int(c == nch - 1))
            for b in range(st["nb"]) for c in range(nch)]
    return jnp.asarray(np.array(rows, dtype=np.int32).T)


def _mlstm_scan(tab, mqkv, tail, bias_row, c0, n0, m0):
    t_all = mqkv.shape[0]
    nseq = c0.shape[0]
    fwd = lambda i, tab: (tab[0, i], 0)
    bwd = lambda i, tab: (tab[1, i], 0)
    st = lambda nd: (lambda i, tab: (tab[2, i],) + (0,) * nd)
    grid_spec = pltpu.PrefetchScalarGridSpec(
        num_scalar_prefetch=1,
        grid=(tab.shape[1],),
        in_specs=[pl.BlockSpec((CHUNK, 768), fwd), pl.BlockSpec((CHUNK, LANES), fwd),
                  pl.BlockSpec((CHUNK, 768), bwd), pl.BlockSpec((CHUNK, LANES), bwd),
                  pl.BlockSpec((1, LANES), lambda i, tab: (0, 0)),
                  pl.BlockSpec((1,) + c0.shape[1:], st(4)),
                  pl.BlockSpec((1,) + n0.shape[1:], st(3)),
                  pl.BlockSpec((1,) + m0.shape[1:], st(2))],
        out_specs=[pl.BlockSpec((CHUNK, 256), fwd), pl.BlockSpec((CHUNK, 256), bwd),
                   pl.BlockSpec((1,) + c0.shape[1:], st(4)),
                   pl.BlockSpec((1,) + n0.shape[1:], st(3)),
                   pl.BlockSpec((1,) + m0.shape[1:], st(2))],
        scratch_shapes=[pltpu.VMEM(c0.shape[1:], F32), pltpu.VMEM(n0.shape[1:], F32),
                        pltpu.VMEM(m0.shape[1:], F32)],
    )
    return pl.pallas_call(
        _mlstm_kernel,
        grid_spec=grid_spec,
        out_shape=[jax.ShapeDtypeStruct((t_all, 256), F32), jax.ShapeDtypeStruct((t_all, 256), F32),
                   jax.ShapeDtypeStruct(c0.shape, F32), jax.ShapeDtypeStruct(n0.shape, F32),
                   jax.ShapeDtypeStruct(m0.shape, F32)],
        compiler_params=_cparams(("arbitrary",)),
        name="mlstm_scan",
    )(tab, mqkv, tail, mqkv, tail, bias_row, c0, n0, m0)


def _ret_scan(tab, rqkv, decay_logit, s0):
    t_all = rqkv.shape[0]
    fwd = lambda i, tab: (tab[0, i], 0)
    bwd = lambda i, tab: (tab[1, i], 0)
    st = lambda i, tab: (tab[2, i], 0, 0, 0, 0)
    grid_spec = pltpu.PrefetchScalarGridSpec(
        num_scalar_prefetch=1,
        grid=(tab.shape[1],),
        in_specs=[pl.BlockSpec((CHUNK, 768), fwd), pl.BlockSpec((CHUNK, 768), bwd),
                  pl.BlockSpec(decay_logit.shape, lambda i, tab: (0, 0)),
                  pl.BlockSpec((1,) + s0.shape[1:], st)],
        out_specs=[pl.BlockSpec((CHUNK, 256), fwd), pl.BlockSpec((CHUNK, 256), bwd),
                   pl.BlockSpec((1,) + s0.shape[1:], st)],
        scratch_shapes=[pltpu.VMEM(s0.shape[1:], F32)],
    )
    return pl.pallas_call(
        _ret_kernel,
        grid_spec=grid_spec,
        out_shape=[jax.ShapeDtypeStruct((t_all, 256), F32), jax.ShapeDtypeStruct((t_all, 256), F32),
                   jax.ShapeDtypeStruct(s0.shape, F32)],
        compiler_params=_cparams(("arbitrary",)),
        name="retention_scan",
    )(tab, rqkv, rqkv, decay_logit, s0)


def _proj_kernel(x_ref, w_ref, o_ref):
    o_ref[...] = _bdot(x_ref[...], w_ref[...]).astype(o_ref.dtype)


def _project(x, w, dtype):
    return pl.pallas_call(
        _proj_kernel,
        out_shape=jax.ShapeDtypeStruct((x.shape[0], w.shape[1]), dtype),
        name="ctx_kv_proj",
    )(x, w)


def _first_argmax_mask(cur, axis, size):
    io = lax.broadcasted_iota(jnp.int32, cur.shape, axis)
    mx = jnp.max(cur, axis=axis, keepdims=True)
    ix = jnp.min(jnp.where(cur == mx, io, size), axis=axis, keepdims=True)
    return io == ix


def _route(scores_t, bias_col):
    tm = scores_t.shape[1]
    per = N_EXPERTS // N_GROUPS
    sel = scores_t + bias_col
    s3 = sel.reshape(N_GROUPS, per, tm)
    hit1 = _first_argmax_mask(s3, 1, per)
    m1 = jnp.max(s3, axis=1, keepdims=True)
    m2 = jnp.max(jnp.where(hit1, -jnp.inf, s3), axis=1, keepdims=True)
    cur = m1 + m2
    gsel = None
    for _ in range(TOPK_GROUPS):
        hit = _first_argmax_mask(cur, 0, N_GROUPS)
        gsel = hit if gsel is None else jnp.logical_or(gsel, hit)
        cur = jnp.where(hit, -jnp.inf, cur)
    cur = jnp.where(gsel, s3, -jnp.inf).reshape(N_EXPERTS, tm)
    chosen, hits = None, []
    for _ in range(TOP_K):
        hit = _first_argmax_mask(cur, 0, N_EXPERTS)
        hits.append(hit)
        chosen = hit if chosen is None else jnp.logical_or(chosen, hit)
        cur = jnp.where(hit, -jnp.inf, cur)
    w = jnp.where(chosen, scores_t, 0.0)
    return w / jnp.sum(w, axis=0, keepdims=True) * ROUTED_SCALE, chosen, hits


def _out_kernel(oa_ref, od_ref, hf_ref, hb_ref, mo_ref, of_ref, ob_ref, rg_ref,
                x_ref, mod_ref, mg_ref, npost_ref, npre_ref, wout_ref, wrt_ref, rb_ref,
                x1_ref, h2_ref, hpk_ref, eidx_ref, rank_ref, wk_ref, cnt_ref, count_s):
    @pl.when(pl.program_id(0) == 0)
    def _():
        count_s[...] = jnp.zeros(count_s.shape, F32)

    a = oa_ref[...]
    dd = od_ref[...]
    b = _rms_heads(hf_ref[...] + hb_ref[...]) * mg_ref[...] * _sigmoid(mo_ref[...])
    r = _rms_heads(of_ref[...] + ob_ref[...]) * _silu(rg_ref[...])
    mix = (jnp.dot(a.astype(BF16), wout_ref[0:256, :], preferred_element_type=F32)
           + jnp.dot(b.astype(BF16), wout_ref[256:512, :], preferred_element_type=F32)
           + jnp.dot(r.astype(BF16), wout_ref[512:768, :], preferred_element_type=F32)
           + jnp.dot(dd.astype(BF16), wout_ref[768:1024, :], preferred_element_type=F32))
    m = mod_ref[0]
    x1 = x_ref[...] + m[2:3] * (_rms(mix) * npost_ref[...])
    x1_ref[...] = x1
    h2 = (_rms(x1) * npre_ref[...] * (1.0 + m[4:5]) + m[3:4]).astype(BF16)
    h2_ref[...] = h2
    hpk_ref[...] = _pack_rows(h2)

    logits_t = lax.dot_general(wrt_ref[...], h2, (((1,), (1,)), ((), ())), preferred_element_type=F32)
    w_t, chosen, hits = _route(_sigmoid(logits_t), rb_ref[...])
    tm = w_t.shape[1]
    src = lax.broadcasted_iota(jnp.int32, (tm, tm), 0)
    dst = lax.broadcasted_iota(jnp.int32, (tm, tm), 1)
    before = jnp.where(src < dst, 1.0, 0.0).astype(BF16)
    picked = jnp.where(chosen, 1.0, 0.0)
    rank = jnp.dot(picked.astype(BF16), before, preferred_element_type=F32) + count_s[...]
    count_s[...] += jnp.sum(picked, axis=1, keepdims=True)
    cnt_ref[...] = jnp.broadcast_to(count_s[...], cnt_ref.shape)
    e_io = lax.broadcasted_iota(jnp.int32, w_t.shape, 0)
    pick = lambda hit, val: jnp.sum(jnp.where(hit, val, jnp.zeros_like(val)), axis=0, keepdims=True)
    eidx_ref[...] = jnp.concatenate([pick(h, e_io) for h in hits], axis=0)
    rank_ref[...] = jnp.concatenate([pick(h, rank) for h in hits], axis=0).astype(jnp.int32)
    wk = jnp.concatenate([pick(h, w_t) for h in hits] + [jnp.zeros((LANES - TOP_K, tm), F32)], axis=0)
    wk_ref[...] = wk.T


def _output_stage(oa, od, hf, hb, mo, of, ob, rg, x, mod, mg, npost, npre, wout, wrt, rb, st):
    t_all = x.shape[0]
    row = lambda i: (i, 0)
    const2 = lambda i: (0, 0)
    tok = lambda w: pl.BlockSpec((TM, w), row)
    full = lambda a: pl.BlockSpec(a.shape, const2)
    return pl.pallas_call(
        _out_kernel,
        grid=(t_all // TM,),
        in_specs=[tok(256)] * 8 + [tok(D_MODEL), _mod_spec(st),
                                   full(mg), full(npost), full(npre), full(wout), full(wrt), full(rb)],
        out_specs=[tok(D_MODEL), tok(D_MODEL), tok(D_MODEL // 2),
                   pl.BlockSpec((TOP_K, TM), lambda i: (0, i)), pl.BlockSpec((TOP_K, TM), lambda i: (0, i)),
                   tok(LANES), pl.BlockSpec((N_EXPERTS, LANES), const2)],
        out_shape=[jax.ShapeDtypeStruct((t_all, D_MODEL), F32),
                   jax.ShapeDtypeStruct((t_all, D_MODEL), BF16),
                   jax.ShapeDtypeStruct((t_all, D_MODEL // 2), jnp.uint32),
                   jax.ShapeDtypeStruct((TOP_K, t_all), jnp.int32),
                   jax.ShapeDtypeStruct((TOP_K, t_all), jnp.int32),
                   jax.ShapeDtypeStruct((t_all, LANES), F32),
                   jax.ShapeDtypeStruct((N_EXPERTS, LANES), F32)],
        scratch_shapes=[pltpu.VMEM((N_EXPERTS, 1), F32)],
        compiler_params=_cparams(("arbitrary",)),
        name="output_stage_" + st["name"],
    )(oa, od, hf, hb, mo, of, ob, rg, x, mod, mg, npost, npre, wout, wrt, rb)


SC_CORES, SC_SUBCORES = 2, 16
SC_WORKERS = SC_CORES * SC_SUBCORES
SLOT_BLOCK = 512
DISPATCH_ROWS = 64
COMBINE_ROWS = 64


def _sc_mesh():
    return plsc.VectorSubcoreMesh(core_axis_name="core", subcore_axis_name="subcore")


def _sc_worker():
    return lax.axis_index("subcore") * SC_CORES + lax.axis_index("core")


def _sc_dispatch(hpk, dest, n_slots):
    t_all, width = hpk.shape
    per_worker = t_all // SC_WORKERS
    assert t_all % (SC_WORKERS * DISPATCH_ROWS) == 0

    @functools.partial(
        pl.kernel, mesh=_sc_mesh(), out_type=jax.ShapeDtypeStruct((n_slots, width), hpk.dtype),
        scratch_types=[pltpu.VMEM((DISPATCH_ROWS, width), hpk.dtype)]
        + [pltpu.VMEM((DISPATCH_ROWS,), jnp.int32)] * TOP_K + [pltpu.SemaphoreType.DMA])
    def dispatch(x_hbm, d_hbm, o_hbm, rows_v, *rest):
        idx, sem = rest[:TOP_K], rest[TOP_K]

        @pl.loop(0, per_worker // DISPATCH_ROWS)
        def _(j):
            base = _sc_worker() * per_worker + j * DISPATCH_ROWS
            pltpu.sync_copy(x_hbm.at[pl.ds(base, DISPATCH_ROWS)], rows_v)
            for k in range(TOP_K):
                pltpu.sync_copy(d_hbm.at[pl.ds(k * t_all + base, DISPATCH_ROWS)], idx[k])
            copies = [pltpu.async_copy(rows_v, o_hbm.at[idx[k]], sem) for k in range(TOP_K)]
            for c in copies:
                c.wait()

    return dispatch(hpk, dest)


def _sc_combine(yb, dest, t_all):
    width = yb.shape[1]
    per_worker = t_all // SC_WORKERS
    assert t_all % (SC_WORKERS * COMBINE_ROWS) == 0

    @functools.partial(
        pl.kernel, mesh=_sc_mesh(), out_type=jax.ShapeDtypeStruct((TOP_K, t_all, width), yb.dtype),
        scratch_types=[pltpu.VMEM((COMBINE_ROWS, width), yb.dtype)] * 2
        + [pltpu.VMEM((COMBINE_ROWS,), jnp.int32)] * TOP_K + [pltpu.SemaphoreType.DMA] * 2)
    def combine(y_hbm, d_hbm, o_hbm, rows_a, rows_b, *rest):
        idx, sems = rest[:TOP_K], rest[TOP_K:]
        bufs = (rows_a, rows_b)

        @pl.loop(0, per_worker // COMBINE_ROWS)
        def _(j):
            base = _sc_worker() * per_worker + j * COMBINE_ROWS
            for k in range(TOP_K):
                pltpu.sync_copy(d_hbm.at[pl.ds(k * t_all + base, COMBINE_ROWS)], idx[k])
            pending = pltpu.async_copy(y_hbm.at[idx[0]], bufs[0], sems[0])
            for k in range(TOP_K):
                pending.wait()
                if k + 1 < TOP_K:
                    pending = pltpu.async_copy(y_hbm.at[idx[k + 1]], bufs[(k + 1) % 2], sems[(k + 1) % 2])
                pltpu.sync_copy(bufs[k % 2], o_hbm.at[k, pl.ds(base, COMBINE_ROWS)])

    return combine(yb, dest)


def _pack_rows(x):
    bits = pltpu.bitcast(x.astype(BF16).astype(F32), jnp.uint32)
    w = x.shape[-1] // 2
    return (bits[..., :w] >> 16) | (bits[..., w:] & jnp.uint32(0xFFFF0000))


def _unpack_rows(words):
    return pltpu.bitcast(words << 16, F32), pltpu.bitcast(words & jnp.uint32(0xFFFF0000), F32)


def _expert_kernel(be_ref, used_ref, xs_ref, wg_ref, wu_ref, wd_ref, y_ref):
    @pl.when(pl.program_id(0) < used_ref[0])
    def _():
        lo, hi = (v.astype(BF16) for v in _unpack_rows(xs_ref[...]))
        half = D_MODEL // 2
        gate = (jnp.dot(lo, wg_ref[0, 0, :half, :], preferred_element_type=F32)
                + jnp.dot(hi, wg_ref[0, 0, half:, :], preferred_element_type=F32))
        up = (jnp.dot(lo, wu_ref[0, 0, :half, :], preferred_element_type=F32)
              + jnp.dot(hi, wu_ref[0, 0, half:, :], preferred_element_type=F32))
        y = jnp.dot((_silu(gate) * up).astype(BF16), wd_ref[0, 0], preferred_element_type=F32)
        y_ref[...] = _pack_rows(y)


def _expert_blocks(layer, xs, block_expert, blocks_used, wg, wu, wdn):
    n_slots = xs.shape[0]
    wspec = lambda shape: pl.BlockSpec((1, 1) + shape, lambda b, be, used: (layer, be[b], 0, 0))
    rows = lambda b, be, used: (jnp.minimum(b, used[0] - 1), 0)
    grid_spec = pltpu.PrefetchScalarGridSpec(
        num_scalar_prefetch=2,
        grid=(n_slots // SLOT_BLOCK,),
        in_specs=[pl.BlockSpec((SLOT_BLOCK, D_MODEL // 2), rows),
                  wspec((D_MODEL, D_EXPERT)), wspec((D_MODEL, D_EXPERT)), wspec((D_EXPERT, D_MODEL))],
        out_specs=pl.BlockSpec((SLOT_BLOCK, D_MODEL // 2), rows))
    return pl.pallas_call(
        _expert_kernel, grid_spec=grid_spec,
        out_shape=jax.ShapeDtypeStruct((n_slots, D_MODEL // 2), jnp.uint32),
        compiler_params=_cparams(("arbitrary",)),
        name="moe_experts",
    )(block_expert, blocks_used, xs, wg, wu, wdn)


def _moe_out_kernel(g_ref, wk_ref, h_ref, x1_ref, mod_ref, npost_ref, sg_ref, su_ref, sd_ref, o_ref):
    h = h_ref[...]
    act = _silu(jnp.dot(h, sg_ref[...], preferred_element_type=F32)) * jnp.dot(h, su_ref[...],
                                                                              preferred_element_type=F32)
    moe = jnp.dot(act.astype(BF16), sd_ref[...], preferred_element_type=F32)
    wk = wk_ref[...]
    lo, hi = 0.0, 0.0
    for k in range(TOP_K):
        g_lo, g_hi = _unpack_rows(g_ref[k])
        lo = lo + g_lo * wk[:, k:k + 1]
        hi = hi + g_hi * wk[:, k:k + 1]
    moe = moe + jnp.concatenate([lo, hi], axis=-1)
    m = mod_ref[0]
    o_ref[...] = x1_ref[...] + m[5:6] * (_rms(moe) * npost_ref[...])


def _moe_out_stage(g, wk, h2, x1, mod, npost, sg, su, sd, st):
    t_all = h2.shape[0]
    row = lambda i: (i, 0)
    full = lambda a: pl.BlockSpec(a.shape, lambda i: (0, 0))
    return pl.pallas_call(
        _moe_out_kernel,
        grid=(t_all // TM,),
        in_specs=[pl.BlockSpec((TOP_K, TM, D_MODEL // 2), lambda i: (0, i, 0)), pl.BlockSpec((TM, LANES), row),
                  pl.BlockSpec((TM, D_MODEL), row), pl.BlockSpec((TM, D_MODEL), row), _mod_spec(st),
                  full(npost), full(sg), full(su), full(sd)],
        out_specs=pl.BlockSpec((TM, D_MODEL), row),
        out_shape=jax.ShapeDtypeStruct((t_all, D_MODEL), F32),
        compiler_params=_cparams(("parallel",)),
        name="moe_combine_" + st["name"],
    )(g, wk, h2, x1, mod, npost, sg, su, sd)


def _moe_stage(layer, h2, hpk, eidx, rank, wk, counts, x1, mod, npost, wg, wu, wdn, sg, su, sd, st):
    t_all = h2.shape[0]
    n_blocks = -(-(t_all * TOP_K + N_EXPERTS * (SLOT_BLOCK - 1)) // SLOT_BLOCK)
    cnt = counts[:, 0].astype(jnp.int32)
    padded = (cnt + SLOT_BLOCK - 1) // SLOT_BLOCK * SLOT_BLOCK
    pad_end = jnp.cumsum(padded)
    pad_start = pad_end - padded
    experts = jnp.arange(N_EXPERTS, dtype=jnp.int32)[:, None, None]
    dest = rank + jnp.sum(jnp.where(eidx[None] == experts, pad_start[:, None, None], 0), axis=0)
    dest = dest.reshape(TOP_K * t_all)
    block_start = jnp.arange(n_blocks, dtype=jnp.int32) * SLOT_BLOCK
    block_expert = jnp.minimum(jnp.sum((pad_end[None, :] <= block_start[:, None]).astype(jnp.int32), axis=1),
                               N_EXPERTS - 1)
    blocks_used = (pad_end[-1:] // SLOT_BLOCK).astype(jnp.int32)

    xs = _sc_dispatch(hpk, dest, n_blocks * SLOT_BLOCK)
    yb = _expert_blocks(layer, xs, block_expert, blocks_used, wg, wu, wdn)
    g = _sc_combine(yb, dest, t_all)
    return _moe_out_stage(g, wk, h2, x1, mod, npost, sg, su, sd, st)


_PERM32 = np.concatenate([np.arange(8, 16), np.arange(0, 8), np.arange(24, 32), np.arange(16, 24)])


def _rope_tables(n):
    pos = jnp.arange(n)
    quarter = QK_ROPE // 4
    inv = 1.0 / (ROPE_THETA ** (jnp.arange(quarter, dtype=F32) / quarter))
    ang_r = (pos // GRID_W).astype(F32)[:, None] * inv[None, :]
    ang_c = (pos % GRID_W).astype(F32)[:, None] * inv[None, :]
    cos32 = jnp.concatenate([jnp.cos(ang_r)] * 2 + [jnp.cos(ang_c)] * 2, axis=-1)
    sin32 = jnp.concatenate([-jnp.sin(ang_r), jnp.sin(ang_r), -jnp.sin(ang_c), jnp.sin(ang_c)], axis=-1)
    cos256, sin256 = jnp.tile(cos32, (1, 8)), jnp.tile(sin32, (1, 8))
    one, zero = jnp.ones((n, QK_NOPE), F32), jnp.zeros((n, QK_NOPE), F32)
    cosq = jnp.concatenate([one, cos32] * N_HEADS, axis=-1)
    sinq = jnp.concatenate([zero, sin32] * N_HEADS, axis=-1)
    return cos256, sin256, cosq, sinq


def _split_w_in(w_in_l):
    sizes = (256, 256, 256, 256, 256, 256, 256, 16, 256, 256, 256, 256, Q_LORA, KV_LORA, QK_ROPE)
    idx = np.cumsum(sizes)[:-1]
    return jnp.split(w_in_l, [int(v) for v in idx], axis=-1)


def _layer_weights(w_in_l, w_uq_l):
    (a_q, a_k, a_v, m_q, m_k, m_v, m_o, m_g, r_q, r_k, r_v, r_g, d_cq, d_ckv, d_kr) = _split_w_in(w_in_l)
    perm256 = np.concatenate([_PERM32 + 32 * j for j in range(8)])
    pad = lambda w, n: jnp.pad(w, ((0, 0), (0, n - w.shape[1])))
    tail = pad(jnp.concatenate([d_kr, m_g], axis=-1), LANES)
    tail_s = pad(d_kr[:, _PERM32], LANES)
    w_all = jnp.concatenate([a_q, a_k, a_v, m_q, m_k, m_v, m_o, r_q, r_k, r_v, r_g, d_cq, d_ckv, tail,
                             a_q[:, perm256], a_k[:, perm256], tail_s], axis=-1).astype(BF16)
    per = QK_NOPE + QK_ROPE
    permq = np.concatenate([np.concatenate([np.arange(QK_NOPE), QK_NOPE + _PERM32]) + per * j
                            for j in range(N_HEADS)])
    return w_all, w_uq_l.astype(BF16), w_uq_l[:, permq].astype(BF16)


def kernel(x_prompt, x_sample, cache_diff_k, cache_diff_v, state_mlstm_C, state_mlstm_n, state_mlstm_m, state_ret_S, cache_mla_ckv, cache_mla_krope, c, c_ctx, w_mod, b_mod, norm_pre, norm_post, w_in, w_out, diff_lambda, diff_norm, mlstm_gate_bias, mlstm_norm, ret_decay_logit, mla_q_norm, mla_w_uq, mla_kv_norm, mla_w_ukv, moe_w_router, moe_router_bias, moe_w_gate, moe_w_up, moe_w_down, shared_w_gate, shared_w_up, shared_w_down):
    bp, n_p, _ = x_prompt.shape
    bs, n_s, _ = x_sample.shape
    depth = w_in.shape[0]
    past = cache_diff_k.shape[2]
    assert n_p % TM == 0 and n_s % TM == 0 and n_p % CHUNK == 0 and n_s % min(TK, n_s) == 0
    assert past % min(TK, past) == 0 and bs + 1 <= 8 and n_s % GRID_W == 0

    streams = (dict(name="ctx", nb=bp, n=n_p, latent=False), dict(name="latent", nb=bs, n=n_s, latent=True))
    xs_by_stream = [x_prompt.reshape(bp * n_p, D_MODEL), x_sample.reshape(bs * n_s, D_MODEL)]
    cond = jnp.zeros((8, D_MODEL), F32).at[0].set(c_ctx).at[1:1 + bs].set(c)
    mod_all = _modulation(cond, w_mod, b_mod).reshape(depth, 8, 6, D_MODEL)
    tabs = _rope_tables(n_s)
    scan_tabs = [_scan_table(st) for st in streams]
    experts_bf16 = [w.astype(BF16) for w in (moe_w_gate, moe_w_up, moe_w_down)]

    outs = [[] for _ in range(8)]
    for l in range(depth):
        lam_init = 0.8 - 0.6 * math.exp(-0.3 * l)
        mod = mod_all[l]
        w_all, wuq, wuqs = _layer_weights(w_in[l], mla_w_uq[l])
        wukv = mla_w_ukv[l].astype(BF16)
        wout, wrt, rbias = w_out[l].astype(BF16), moe_w_router[l].T.astype(BF16), moe_router_bias[l][:, None]
        shared = [w[l].astype(BF16) for w in (shared_w_gate, shared_w_up, shared_w_down)]
        lamv, dg = diff_lambda[l], diff_norm[l][:, None]
        bias_row = jnp.zeros((1, LANES), F32).at[0, TAIL_GATE0:TAIL_GATE0 + 16].set(mlstm_gate_bias[l].reshape(16))

        with_ones = lambda vt: jnp.concatenate([vt, jnp.ones((N_HEADS, VT_ONES, bs * past), BF16)], axis=1)
        ck = jnp.transpose(cache_diff_k[:, l], (2, 0, 1, 3)).reshape(N_HEADS, bs * past, HEAD_DIM).astype(BF16)
        cvt = with_ones(jnp.transpose(cache_diff_v[:, l], (2, 3, 0, 1)).reshape(N_HEADS, HEAD_DIM, bs * past)
                        .astype(BF16))
        kvc = _project(cache_mla_ckv[:, l].reshape(bs * past, KV_LORA), wukv, BF16)
        kvc = kvc.reshape(bs * past, N_HEADS, QK_NOPE + V_HEAD)
        krc = jnp.broadcast_to(cache_mla_krope[:, l].reshape(bs * past, 1, QK_ROPE).astype(BF16),
                               (bs * past, N_HEADS, QK_ROPE))
        kc = jnp.transpose(jnp.concatenate([kvc[..., :QK_NOPE], krc], axis=-1), (1, 0, 2))
        vct = with_ones(jnp.transpose(kvc[..., QK_NOPE:], (1, 2, 0)))

        for si, st in enumerate(streams):
            x, nb, n = xs_by_stream[si], st["nb"], st["n"]
            (aq1t, aq2t, ak1, ak2, avt, ak, av, mqkv, mo, rqkv, rg, tail, qmlat, ckv, kmla, vmlat) = _input_stage(
                x, mod, norm_pre[l, 0:1], w_all, tabs, mla_q_norm[l][None], wuq, wuqs,
                mla_kv_norm[l][None], wukv, st)
            if st["latent"]:
                oa = _attention_t("diff_attn_latent", [aq1t, aq2t], [ak1, ak2], avt,
                                  [ck[..., :DIFF_HALF], ck[..., DIFF_HALF:]], cvt, [lamv, dg], nb, n, past, lam_init)
                od = _attention_t("mla_attn_latent", [qmlat], [kmla], vmlat, [kc], vct, [], nb, n, past)
                states = [s[:, l] for s in (state_mlstm_C, state_mlstm_n, state_mlstm_m, state_ret_S)]
            else:
                oa = _attention_t("diff_attn_ctx", [aq1t, aq2t], [ak1, ak2], avt, None, None, [lamv, dg],
                                  nb, n, 0, lam_init)
                od = _attention_t("mla_attn_ctx", [qmlat], [kmla], vmlat, None, None, [], nb, n, 0)
                states = [jnp.zeros((nb,) + s.shape[2:], F32)
                          for s in (state_mlstm_C, state_mlstm_n, state_mlstm_m, state_ret_S)]
            hf, hb, c_n, n_n, m_n = _mlstm_scan(scan_tabs[si], mqkv, tail, bias_row, *states[:3])
            of, ob, s_n = _ret_scan(scan_tabs[si], rqkv, ret_decay_logit[l], states[3])

            x1, h2, hpk, eidx, rank, wk, counts = _output_stage(
                oa, od, hf, hb, mo, of, ob, rg, x, mod, mlstm_norm[l][None], norm_post[l, 0:1],
                norm_pre[l, 1:2], wout, wrt, rbias, st)
            xs_by_stream[si] = _moe_stage(l, h2, hpk, eidx, rank, wk, counts, x1, mod, norm_post[l, 1:2],
                                          *experts_bf16, *shared, st)
            if not st["latent"]:
                new = (ak.reshape(nb, n, N_HEADS, HEAD_DIM), av.reshape(nb, n, N_HEADS, HEAD_DIM), c_n, n_n, m_n,
                       s_n, ckv.reshape(nb, n, KV_LORA), tail[:, :QK_ROPE].reshape(nb, n, QK_ROPE))
                for o, v in zip(outs, new):
                    o.append(v)

    return (xs_by_stream[0].reshape(bp, n_p, D_MODEL), xs_by_stream[1].reshape(bs, n_s, D_MODEL)) + tuple(
        jnp.stack(o, axis=1) for o in outs)
```

```python
import functools
import math

import numpy as np
import jax
import jax.numpy as jnp
from jax import lax
from jax.experimental import pallas as pl
from jax.experimental.pallas import tpu as pltpu
from jax.experimental.pallas import tpu_sc as plsc

F32 = jnp.float32
BF16 = jnp.bfloat16
HIGHEST = lax.Precision.HIGHEST

D_MODEL = 1024
GRID_W = 64
GROUP_WIDTH = 256
HEAD_DIM = 64
N_HEADS = 4
DIFF_HALF = 32
ROPE_THETA = 10000.0
Q_LORA = 256
KV_LORA = 128
QK_NOPE = 64
QK_ROPE = 32
V_HEAD = 64
N_EXPERTS = 64
TOP_K = 8
N_GROUPS = 8
TOPK_GROUPS = 4
D_EXPERT = 256
ROUTED_SCALE = 2.5
CHUNK = 128
EPS = 1e-6
Q_MLA = N_HEADS * (QK_NOPE + QK_ROPE)
KV_MLA = N_HEADS * (QK_NOPE + V_HEAD)

LANES = 128
VMEM_LIMIT = 56 * 1024 * 1024

TM = 256
TQ = 256
TK = 4096

C_AQ, C_AK, C_AV = 0, 256, 512
C_MQKV, C_MO = 768, 1536
C_RQKV, C_RG = 1792, 2560
C_CQ, C_CKV, C_TAIL = 2816, 3072, 3200
C_AQS, C_AKS, C_TAILS = 3328, 3584, 3840
W_ALL = 3968
TAIL_GATE0 = QK_ROPE
VT_ONES = 16
VT_ROWS = V_HEAD + VT_ONES


def _cparams(sem, flags=None):
    return pltpu.CompilerParams(dimension_semantics=sem, vmem_limit_bytes=VMEM_LIMIT, flags=flags)


def _rms(x):
    return x * lax.rsqrt(jnp.mean(x * x, axis=-1, keepdims=True) + EPS)


def _head_mean_matrix(width):
    r = lax.broadcasted_iota(jnp.int32, (width, width), 0) // HEAD_DIM
    c = lax.broadcasted_iota(jnp.int32, (width, width), 1) // HEAD_DIM
    return jnp.where(r == c, 1.0 / HEAD_DIM, 0.0).astype(F32)


def _rms_heads(x):
    ms = jnp.dot(x * x, _head_mean_matrix(x.shape[-1]), precision=HIGHEST, preferred_element_type=F32)
    return x * lax.rsqrt(ms + EPS)


def _sigmoid(x):
    return 1.0 / (1.0 + jnp.exp(-x))


def _silu(x):
    return x * _sigmoid(x)


def _log_sigmoid(x):
    return jnp.minimum(x, 0.0) - jnp.log1p(jnp.exp(-jnp.abs(x)))


def _bdot(a, b):
    return jnp.dot(a.astype(BF16), b.astype(BF16), preferred_element_type=F32)


def _mod_kernel(c_ref, w_ref, b_ref, o_ref):
    o_ref[0] = _bdot(_silu(c_ref[...]), w_ref[0]) + b_ref[0]


def _modulation(cond, w_mod, b_mod):
    depth, _, n = w_mod.shape
    tn = 1536
    return pl.pallas_call(
        _mod_kernel,
        grid=(depth, n // tn),
        in_specs=[pl.BlockSpec((8, D_MODEL), lambda l, j: (0, 0)),
                  pl.BlockSpec((1, D_MODEL, tn), lambda l, j: (l, 0, j)),
                  pl.BlockSpec((1, 1, tn), lambda l, j: (l, 0, j))],
        out_specs=pl.BlockSpec((1, 8, tn), lambda l, j: (l, 0, j)),
        out_shape=jax.ShapeDtypeStruct((depth, 8, n), F32),
        compiler_params=_cparams(("parallel", "parallel")),
        name="adaln_mod",
    )(cond, w_mod, b_mod.reshape(depth, 1, n))


def _in_kernel(x_ref, mod_ref, npre_ref, w_ref, cos_ref, sin_ref, cosq_ref, sinq_ref,
               qg_ref, wuq_ref, wuqs_ref, kvg_ref, wukv_ref,
               aq1_ref, aq2_ref, ak1t_ref, ak2t_ref, avh_ref, ak_ref, av_ref,
               mqkv_ref, mo_ref, rqkv_ref, rg_ref, tail_ref,
               qmla_ref, ckv_ref, kmlat_ref, vmla_ref, *, latent):
    m = mod_ref[0]
    h = (_rms(x_ref[...]) * npre_ref[...] * (1.0 + m[1:2]) + m[0:1]).astype(BF16)

    def proj(c0, width):
        return jnp.dot(h, w_ref[:, c0:c0 + width], preferred_element_type=F32)

    def rotated(c0, c0_swapped, width, cos, sin):
        return proj(c0, width) * cos + proj(c0_swapped, width) * sin if latent else proj(c0, width)

    cos = cos_ref[...]
    sin = sin_ref[...]
    aq = rotated(C_AQ, C_AQS, 256, cos, sin) * (DIFF_HALF ** -0.5)
    ak = rotated(C_AK, C_AKS, 256, cos, sin)
    av = proj(C_AV, 256)
    ak_ref[...] = ak
    av_ref[...] = av
    aq_t, av_t = aq.T.astype(BF16), av.T.astype(BF16)
    ones = jnp.ones((VT_ONES, aq_t.shape[1]), BF16)
    for hd in range(N_HEADS):
        lo = hd * HEAD_DIM
        aq1_ref[hd] = aq_t[lo:lo + DIFF_HALF, :]
        aq2_ref[hd] = aq_t[lo + DIFF_HALF:lo + HEAD_DIM, :]
        ak1t_ref[hd] = ak[:, lo:lo + DIFF_HALF].astype(BF16)
        ak2t_ref[hd] = ak[:, lo + DIFF_HALF:lo + HEAD_DIM].astype(BF16)
        avh_ref[hd, 0:HEAD_DIM, :] = av_t[lo:lo + HEAD_DIM, :]
        avh_ref[hd, HEAD_DIM:VT_ROWS, :] = ones

    mqkv_ref[...] = proj(C_MQKV, 768).astype(BF16)
    mo_ref[...] = proj(C_MO, 256)
    rqkv_ref[...] = proj(C_RQKV, 768).astype(BF16)
    rg_ref[...] = proj(C_RG, 256)

    tail = proj(C_TAIL, LANES)
    tail_ref[...] = tail
    kr = tail * cos[:, :LANES] + proj(C_TAILS, LANES) * sin[:, :LANES] if latent else tail
    kr = kr[:, :QK_ROPE].astype(BF16)

    cqn = (_rms(proj(C_CQ, Q_LORA)) * qg_ref[...]).astype(BF16)
    qmla = jnp.dot(cqn, wuq_ref[...], preferred_element_type=F32)
    if latent:
        qmla = qmla * cosq_ref[...] + jnp.dot(cqn, wuqs_ref[...], preferred_element_type=F32) * sinq_ref[...]
    qmla_t = (qmla * ((QK_NOPE + QK_ROPE) ** -0.5)).T.astype(BF16)
    for hd in range(N_HEADS):
        lo = hd * (QK_NOPE + QK_ROPE)
        qmla_ref[hd] = qmla_t[lo:lo + QK_NOPE + QK_ROPE, :]
    ckvn = _rms(proj(C_CKV, KV_LORA)) * kvg_ref[...]
    ckv_ref[...] = ckvn
    kv = jnp.dot(ckvn.astype(BF16), wukv_ref[...], preferred_element_type=F32)
    kv_t = kv.T.astype(BF16)
    per = QK_NOPE + V_HEAD
    for hd in range(N_HEADS):
        kmlat_ref[hd, :, 0:QK_NOPE] = kv[:, hd * per:hd * per + QK_NOPE].astype(BF16)
        kmlat_ref[hd, :, QK_NOPE:QK_NOPE + QK_ROPE] = kr
        vmla_ref[hd, 0:V_HEAD, :] = kv_t[hd * per + QK_NOPE:(hd + 1) * per, :]
        vmla_ref[hd, V_HEAD:VT_ROWS, :] = ones


def _mod_spec(st):
    tps = st["n"] // TM
    return pl.BlockSpec((1, 6, D_MODEL), lambda i, *_: (1 + i // tps if st["latent"] else 0, 0, 0))


def _input_stage(x, mod, npre, w_all, tabs, qg, wuq, wuqs, kvg, wukv, st):
    t_all = x.shape[0]
    tps = st["n"] // TM
    row = lambda i: (i, 0)
    tab = lambda w: pl.BlockSpec((TM, w), lambda i: (i % tps if st["latent"] else 0, 0))
    hrow = lambda i: (0, i, 0)
    const2 = lambda i: (0, 0)
    tok = lambda w: pl.BlockSpec((TM, w), row)
    headed = lambda w: pl.BlockSpec((N_HEADS, TM, w), hrow)
    headed_t = lambda d: pl.BlockSpec((N_HEADS, d, TM), lambda i: (0, 0, i))
    full = lambda a: pl.BlockSpec(a.shape, const2)
    cos, sin, cosq, sinq = tabs
    out_shapes = [
        (headed_t(DIFF_HALF), (N_HEADS, DIFF_HALF, t_all), BF16),
        (headed_t(DIFF_HALF), (N_HEADS, DIFF_HALF, t_all), BF16),
        (headed(DIFF_HALF), (N_HEADS, t_all, DIFF_HALF), BF16),
        (headed(DIFF_HALF), (N_HEADS, t_all, DIFF_HALF), BF16),
        (headed_t(VT_ROWS), (N_HEADS, VT_ROWS, t_all), BF16),
        (tok(256), (t_all, 256), F32),
        (tok(256), (t_all, 256), F32),
        (tok(768), (t_all, 768), BF16),
        (tok(256), (t_all, 256), F32),
        (tok(768), (t_all, 768), BF16),
        (tok(256), (t_all, 256), F32),
        (tok(LANES), (t_all, LANES), F32),
        (headed_t(QK_NOPE + QK_ROPE), (N_HEADS, QK_NOPE + QK_ROPE, t_all), BF16),
        (tok(KV_LORA), (t_all, KV_LORA), F32),
        (headed(QK_NOPE + QK_ROPE), (N_HEADS, t_all, QK_NOPE + QK_ROPE), BF16),
        (headed_t(VT_ROWS), (N_HEADS, VT_ROWS, t_all), BF16),
    ]
    return pl.pallas_call(
        functools.partial(_in_kernel, latent=st["latent"]),
        grid=(t_all // TM,),
        in_specs=[tok(D_MODEL), _mod_spec(st),
                  full(npre), full(w_all), tab(256), tab(256), tab(Q_MLA), tab(Q_MLA),
                  full(qg), full(wuq), full(wuqs), full(kvg), full(wukv)],
        out_specs=[s for s, _, _ in out_shapes],
        out_shape=[jax.ShapeDtypeStruct(shp, dt) for _, shp, dt in out_shapes],
        compiler_params=_cparams(("parallel",)),
        name="input_stage_" + st["name"],
    )(x, mod, npre, w_all, cos, sin, cosq, sinq, qg, wuq, wuqs, kvg, wukv)


def _attn_t_kernel(*refs, n_soft, n_new, n_ctx, lam_init):
    refs = list(refs)
    qt_refs, k_refs, vt_ref = refs[:n_soft], refs[n_soft:2 * n_soft], refs[2 * n_soft]
    pos = 2 * n_soft + 1
    if n_ctx:
        ck_refs, cvt_ref = refs[pos:pos + n_soft], refs[pos + n_soft]
        pos += n_soft + 1
    if n_soft == 2:
        lam_ref, g_ref = refs[pos:pos + 2]
        pos += 2
    o_ref = refs[pos]
    tq = o_ref.shape[0]
    nchain = n_soft * N_HEADS
    order = [(j, hd) for hd in range(N_HEADS) for j in range(n_soft)]

    def chunk(state, k_of, vt_of):
        ms, accs = state
        new_m, new_acc = list(ms), list(accs)
        ss = [jnp.dot(k_of(j, hd), qt_refs[j][hd], preferred_element_type=F32) for j, hd in order]
        ps, alphas = [], []
        for (j, hd), s in zip(order, ss):
            c = j * N_HEADS + hd
            s3 = s.reshape(s.shape[0] // 8, 8, tq)
            top = jnp.max(jnp.max(s3, axis=0), axis=0, keepdims=True)
            m_new = jnp.maximum(ms[c], jnp.broadcast_to(top, (8, tq)))
            ps.append(jnp.exp(s3 - m_new[None]).reshape(s.shape).astype(BF16))
            alphas.append(jnp.exp(ms[c] - m_new))
            new_m[c] = m_new
        for (j, hd), p, alpha in zip(order, ps, alphas):
            c = j * N_HEADS + hd
            scaled = (accs[c].reshape(VT_ROWS // 8, 8, tq) * alpha[None]).reshape(VT_ROWS, tq)
            new_acc[c] = scaled + jnp.dot(vt_of(hd), p, preferred_element_type=F32)
        return tuple(new_m), tuple(new_acc)

    state = (tuple(jnp.full((8, tq), -jnp.inf, F32) for _ in range(nchain)),
             tuple(jnp.zeros((VT_ROWS, tq), F32) for _ in range(nchain)))
    if n_ctx:
        cstep = min(TK, n_ctx)
        for i in range(n_ctx // cstep):
            state = chunk(state, lambda j, hd, i=i: ck_refs[j][hd, i * cstep:(i + 1) * cstep, :],
                          lambda hd, i=i: cvt_ref[hd, :, i * cstep:(i + 1) * cstep])
    step = min(TK, n_new)

    def body(i, st):
        start = pl.multiple_of(i * step, step)
        return chunk(st, lambda j, hd: k_refs[j][hd, pl.ds(start, step), :],
                     lambda hd: vt_ref[hd, :, pl.ds(start, step)])

    _, accs = lax.fori_loop(0, n_new // step, body, state)

    def normalised(c):
        num, den = accs[c][:V_HEAD], accs[c][V_HEAD:V_HEAD + 8]
        return (num.reshape(V_HEAD // 8, 8, tq) / den[None]).reshape(V_HEAD, tq)

    if n_soft == 2:
        lv = lam_ref[...]
        lam = (jnp.exp(jnp.sum(lv[0:1] * lv[1:2], axis=-1, keepdims=True))
               - jnp.exp(jnp.sum(lv[2:3] * lv[3:4], axis=-1, keepdims=True)) + lam_init)
    outs = []
    for hd in range(N_HEADS):
        out = normalised(hd)
        if n_soft == 2:
            a = out - lam * normalised(N_HEADS + hd)
            out = a * lax.rsqrt(jnp.mean(a * a, axis=0, keepdims=True) + EPS) * g_ref[...] * (1.0 - lam_init)
        outs.append(out)
    o_ref[...] = jnp.concatenate(outs, axis=0).T


def _attention_t(name, qts, ks, vt, ctx_ks, ctx_vt, extras, nb, n, n_ctx, lam_init=0.0):
    tq = min(TQ, n)
    nqt = n // tq
    n_soft = len(qts)
    in_specs = [pl.BlockSpec((N_HEADS, a.shape[1], tq), lambda b, i: (0, 0, b * nqt + i)) for a in qts]
    in_specs += [pl.BlockSpec((N_HEADS, n, a.shape[-1]), lambda b, i: (0, b, 0)) for a in ks]
    in_specs += [pl.BlockSpec((N_HEADS, VT_ROWS, n), lambda b, i: (0, 0, b))]
    args = list(qts) + list(ks) + [vt]
    if n_ctx:
        in_specs += [pl.BlockSpec((N_HEADS, n_ctx, a.shape[-1]), lambda b, i: (0, b, 0)) for a in ctx_ks]
        in_specs += [pl.BlockSpec((N_HEADS, VT_ROWS, n_ctx), lambda b, i: (0, 0, b))]
        args += list(ctx_ks) + [ctx_vt]
    in_specs += [pl.BlockSpec(a.shape, lambda b, i: (0, 0)) for a in extras]
    args += list(extras)
    return pl.pallas_call(
        functools.partial(_attn_t_kernel, n_soft=n_soft, n_new=n, n_ctx=n_ctx, lam_init=lam_init),
        grid=(nb, nqt),
        in_specs=in_specs,
        out_specs=pl.BlockSpec((tq, N_HEADS * V_HEAD), lambda b, i: (b * nqt + i, 0)),
        out_shape=jax.ShapeDtypeStruct((nb * n, N_HEADS * V_HEAD), F32),
        compiler_params=_cparams(("parallel", "parallel")),
        name=name,
    )(*args)


def _tri(lower):
    r = lax.broadcasted_iota(jnp.int32, (CHUNK, CHUNK), 0)
    c = lax.broadcasted_iota(jnp.int32, (CHUNK, CHUNK), 1)
    return (c <= r) if lower else (c >= r)


def _nt(a, b):
    return lax.dot_general(a, b, (((1,), (1,)), ((), ())), preferred_element_type=F32)


def _tn(a, b):
    return lax.dot_general(a, b, (((0,), (0,)), ((), ())), preferred_element_type=F32)


def _head_slices(qkv, hd):
    lo = hd * HEAD_DIM
    return qkv[:, lo:lo + HEAD_DIM], qkv[:, 256 + lo:256 + lo + HEAD_DIM], qkv[:, 512 + lo:512 + lo + HEAD_DIM]


def _row_dot(row, mat, transpose_mat):
    row8 = jnp.broadcast_to(row, (8, row.shape[1])).astype(BF16)
    dims = (((1,), (1,)), ((), ())) if transpose_mat else (((1,), (0,)), ((), ()))
    return lax.dot_general(row8, mat, dims, preferred_element_type=F32)[0:1]


def _load_state_t(dst, src):
    for d in range(2):
        for hd in range(N_HEADS):
            dst[d, hd] = src[d, hd].T


def _mlstm_step(dirs, bias_ref, ct_s, n_s, m_s):
    k_scale = HEAD_DIM ** -0.5
    lower = _tri(True).astype(F32)
    upper = _tri(False).astype(F32)
    chains = []
    for d, qkv_ref, tail_ref, _ in dirs:
        g = tail_ref[...] + bias_ref[...]
        ls = _log_sigmoid(g)
        g_t, ls_t = g.T, ls.T
        left, right = (lower, upper) if d == 0 else (upper, lower)
        cum_col = jnp.dot(left, ls, precision=HIGHEST, preferred_element_type=F32)
        cum_row = jnp.dot(ls_t, right, precision=HIGHEST, preferred_element_type=F32)
        qkv = qkv_ref[...]
        for hd in range(N_HEADS):
            ci = TAIL_GATE0 + 4 * (2 * d) + hd
            cf = ci + 4
            q, k, v = _head_slices(qkv, hd)
            chains.append(dict(
                d=d, hd=hd, q=q, k=k, v=v, li_row=g_t[ci:ci + 1, :], b_row=cum_row[cf:cf + 1, :],
                c_col=g[:, ci:ci + 1] - cum_col[:, cf:cf + 1],
                m_prev=m_s[d:d + 1, hd:hd + 1], ct=ct_s[d, hd], n=n_s[d, hd:hd + 1, :]))
    for c in chains:
        c["s"] = _nt(c["k"], c["q"])
        c["v_t"] = c["v"].T
    for c in chains:
        valid = _tri(c["d"] != 0)
        log_d = jnp.where(valid, c["b_row"] + c["c_col"], -jnp.inf)
        log_inter = c["b_row"] + c["m_prev"]
        c["m_t"] = jnp.maximum(log_inter, jnp.max(log_d, axis=0, keepdims=True))
        c["w"] = c["s"] * k_scale * jnp.exp(log_d - c["m_t"])
        c["w_inter"] = jnp.exp(log_inter - c["m_t"])
    for c in chains:
        num = (jnp.dot(c["v_t"], c["w"].astype(BF16), preferred_element_type=F32)
               + c["w_inter"] * _nt(c["ct"].astype(BF16), c["q"]))
        den = jnp.sum(c["w"], axis=0, keepdims=True) + c["w_inter"] * _row_dot(c["n"], c["q"], True)
        c["h"] = num / jnp.maximum(jnp.abs(den), jnp.exp(-c["m_t"]))
    for c in chains:
        last = CHUNK - 1 if c["d"] == 0 else 0
        c["m_new"] = c["m_t"][:, last:last + 1]
        b_last = c["b_row"][:, last:last + 1]
        w_end_row = jnp.exp(b_last - c["b_row"] + c["li_row"] - c["m_new"])
        w_end_col = jnp.exp(c["c_col"] + (b_last - c["m_new"]))
        decay = jnp.exp(b_last + c["m_prev"] - c["m_new"])
        kw = (c["k"].astype(F32) * (w_end_col * k_scale)).astype(BF16)
        c["ct_new"] = decay * c["ct"] + jnp.dot(c["v_t"], kw, preferred_element_type=F32)
        c["n_new"] = decay * c["n"] + _row_dot(w_end_row, c["k"], False) * k_scale
    for j, (_, _, _, h_ref) in enumerate(dirs):
        h_ref[...] = jnp.concatenate([c["h"] for c in chains[j * N_HEADS:(j + 1) * N_HEADS]], axis=0).T
    for c in chains:
        d, hd = c["d"], c["hd"]
        ct_s[d, hd] = c["ct_new"]
        n_s[d, hd:hd + 1, :] = c["n_new"]
        m_s[d:d + 1, hd:hd + 1] = c["m_new"]


def _ret_step(dirs, lg_ref, st_s):
    k_scale = HEAD_DIM ** -0.5
    s = lax.broadcasted_iota(jnp.int32, (CHUNK, CHUNK), 0)
    t = lax.broadcasted_iota(jnp.int32, (CHUNK, CHUNK), 1)
    lane = lax.broadcasted_iota(jnp.int32, (1, CHUNK), 1)
    chains = []
    for d, qkv_ref, _ in dirs:
        lag = (t - s) if d == 0 else (s - t)
        pos = (lane if d == 0 else CHUNK - 1 - lane).astype(F32)
        qkv = qkv_ref[...]
        for hd in range(N_HEADS):
            q, k, v = _head_slices(qkv, hd)
            chains.append(dict(d=d, hd=hd, q=q, k=k, v=v, lag=lag, pos=pos, st=st_s[d, hd],
                               lg=_log_sigmoid(lg_ref[d:d + 1, hd:hd + 1])))
    for c in chains:
        c["s"] = _nt(c["k"], c["q"])
        c["v_t"] = c["v"].T
    for c in chains:
        intra = jnp.where(c["lag"] >= 0, jnp.exp(jnp.maximum(c["lag"], 0).astype(F32) * c["lg"]), 0.0)
        c["a"] = (c["s"] * k_scale * intra).astype(BF16)
    for c in chains:
        inter = jnp.exp((c["pos"] + 1.0) * c["lg"])
        c["o"] = jnp.dot(c["v_t"], c["a"], preferred_element_type=F32) + inter * _nt(c["st"].astype(BF16), c["q"])
    row = lax.broadcasted_iota(jnp.int32, (CHUNK, HEAD_DIM), 0)
    for c in chains:
        src_pos = (row if c["d"] == 0 else CHUNK - 1 - row).astype(F32)
        tail = jnp.exp((CHUNK - 1.0 - src_pos) * c["lg"]) * k_scale
        kw = (c["k"].astype(F32) * tail).astype(BF16)
        c["st_new"] = jnp.exp(CHUNK * c["lg"]) * c["st"] + jnp.dot(c["v_t"], kw, preferred_element_type=F32)
    for j, (_, _, o_ref) in enumerate(dirs):
        o_ref[...] = jnp.concatenate([c["o"] for c in chains[j * N_HEADS:(j + 1) * N_HEADS]], axis=0).T
    for c in chains:
        st_s[c["d"], c["hd"]] = c["st_new"]


def _scan_kernel(tab_ref, mqf_ref, tailf_ref, mqb_ref, tailb_ref, rqf_ref, rqb_ref, bias_ref, lg_ref,
                 c0_ref, n0_ref, m0_ref, s0_ref,
                 hf_ref, hb_ref, of_ref, ob_ref, cn_ref, nn_ref, mn_ref, sn_ref, ct_s, n_s, m_s, st_s):
    i = pl.program_id(0)

    @pl.when(tab_ref[3, i] == 1)
    def _():
        _load_state_t(ct_s, c0_ref.at[0])
        _load_state_t(st_s, s0_ref.at[0])
        n_s[...] = n0_ref[0]
        m_s[...] = m0_ref[0]

    _mlstm_step([(0, mqf_ref, tailf_ref, hf_ref), (1, mqb_ref, tailb_ref, hb_ref)], bias_ref, ct_s, n_s, m_s)
    _ret_step([(0, rqf_ref, of_ref), (1, rqb_ref, ob_ref)], lg_ref, st_s)

    @pl.when(tab_ref[4, i] == 1)
    def _():
        _load_state_t(cn_ref.at[0], ct_s)
        _load_state_t(sn_ref.at[0], st_s)
        nn_ref[0] = n_s[...]
        mn_ref[0] = m_s[...]


def _scan_table(st):
    nch = st["n"] // CHUNK
    rows = [(b * nch + c, b * nch + nch - 1 - c, b, int(c == 0), int(c == nch - 1))
            for b in range(st["nb"]) for c in range(nch)]
    return jnp.asarray(np.array(rows, dtype=np.int32).T)


def _scans(tab, mqkv, tail, bias_row, rqkv, decay_logit, c0, n0, m0, s0):
    t_all = mqkv.shape[0]
    fwd = lambda i, tab: (tab[0, i], 0)
    bwd = lambda i, tab: (tab[1, i], 0)
    const = lambda i, tab: (0, 0)
    state = lambda a: pl.BlockSpec((1,) + a.shape[1:], lambda i, tab: (tab[2, i],) + (0,) * (a.ndim - 1))
    states = (c0, n0, m0, s0)
    grid_spec = pltpu.PrefetchScalarGridSpec(
        num_scalar_prefetch=1,
        grid=(tab.shape[1],),
        in_specs=[pl.BlockSpec((CHUNK, 768), fwd), pl.BlockSpec((CHUNK, LANES), fwd),
                  pl.BlockSpec((CHUNK, 768), bwd), pl.BlockSpec((CHUNK, LANES), bwd),
                  pl.BlockSpec((CHUNK, 768), fwd), pl.BlockSpec((CHUNK, 768), bwd),
                  pl.BlockSpec((1, LANES), const), pl.BlockSpec(decay_logit.shape, const)]
        + [state(a) for a in states],
        out_specs=[pl.BlockSpec((CHUNK, 256), fwd), pl.BlockSpec((CHUNK, 256), bwd),
                   pl.BlockSpec((CHUNK, 256), fwd), pl.BlockSpec((CHUNK, 256), bwd)]
        + [state(a) for a in states],
        scratch_shapes=[pltpu.VMEM(a.shape[1:], F32) for a in states],
    )
    return pl.pallas_call(
        _scan_kernel,
        grid_spec=grid_spec,
        out_shape=[jax.ShapeDtypeStruct((t_all, 256), F32)] * 4 + [jax.ShapeDtypeStruct(a.shape, F32) for a in states],
        compiler_params=_cparams(("arbitrary",)),
        name="recurrent_scans",
    )(tab, mqkv, tail, mqkv, tail, rqkv, rqkv, bias_row, decay_logit, *states)


def _proj_kernel(x_ref, w_ref, o_ref):
    o_ref[...] = _bdot(x_ref[...], w_ref[...]).astype(o_ref.dtype)


def _project(x, w, dtype):
    return pl.pallas_call(
        _proj_kernel,
        out_shape=jax.ShapeDtypeStruct((x.shape[0], w.shape[1]), dtype),
        name="ctx_kv_proj",
    )(x, w)


def _first_argmax_mask(cur, axis, size):
    io = lax.broadcasted_iota(jnp.int32, cur.shape, axis)
    mx = jnp.max(cur, axis=axis, keepdims=True)
    ix = jnp.min(jnp.where(cur == mx, io, size), axis=axis, keepdims=True)
    return io == ix


def _route(scores_t, bias_col):
    tm = scores_t.shape[1]
    per = N_EXPERTS // N_GROUPS
    sel = scores_t + bias_col
    s3 = sel.reshape(N_GROUPS, per, tm)
    hit1 = _first_argmax_mask(s3, 1, per)
    m1 = jnp.max(s3, axis=1, keepdims=True)
    m2 = jnp.max(jnp.where(hit1, -jnp.inf, s3), axis=1, keepdims=True)
    cur = m1 + m2
    gsel = None
    for _ in range(TOPK_GROUPS):
        hit = _first_argmax_mask(cur, 0, N_GROUPS)
        gsel = hit if gsel is None else jnp.logical_or(gsel, hit)
        cur = jnp.where(hit, -jnp.inf, cur)
    cur = jnp.where(gsel, s3, -jnp.inf).reshape(N_EXPERTS, tm)
    chosen, hits = None, []
    for _ in range(TOP_K):
        hit = _first_argmax_mask(cur, 0, N_EXPERTS)
        hits.append(hit)
        chosen = hit if chosen is None else jnp.logical_or(chosen, hit)
        cur = jnp.where(hit, -jnp.inf, cur)
    w = jnp.where(chosen, scores_t, 0.0)
    return w / jnp.sum(w, axis=0, keepdims=True) * ROUTED_SCALE, chosen, hits


def _out_kernel(oa_ref, od_ref, hf_ref, hb_ref, mo_ref, of_ref, ob_ref, rg_ref,
                x_ref, mod_ref, mg_ref, npost_ref, npre_ref, wout_ref, wrt_ref, rb_ref,
                x1_ref, h2_ref, hpk_ref, eidx_ref, rank_ref, wk_ref, cnt_ref, count_s):
    @pl.when(pl.program_id(0) == 0)
    def _():
        count_s[...] = jnp.zeros(count_s.shape, F32)

    a = oa_ref[...]
    dd = od_ref[...]
    b = _rms_heads(hf_ref[...] + hb_ref[...]) * mg_ref[...] * _sigmoid(mo_ref[...])
    r = _rms_heads(of_ref[...] + ob_ref[...]) * _silu(rg_ref[...])
    mix = (jnp.dot(a.astype(BF16), wout_ref[0:256, :], preferred_element_type=F32)
           + jnp.dot(b.astype(BF16), wout_ref[256:512, :], preferred_element_type=F32)
           + jnp.dot(r.astype(BF16), wout_ref[512:768, :], preferred_element_type=F32)
           + jnp.dot(dd.astype(BF16), wout_ref[768:1024, :], preferred_element_type=F32))
    m = mod_ref[0]
    x1 = x_ref[...] + m[2:3] * (_rms(mix) * npost_ref[...])
    x1_ref[...] = x1
    h2 = (_rms(x1) * npre_ref[...] * (1.0 + m[4:5]) + m[3:4]).astype(BF16)
    h2_ref[...] = h2
    hpk_ref[...] = _pack_rows(h2)

    logits_t = lax.dot_general(wrt_ref[...], h2, (((1,), (1,)), ((), ())), preferred_element_type=F32)
    w_t, chosen, hits = _route(_sigmoid(logits_t), rb_ref[...])
    tm = w_t.shape[1]
    src = lax.broadcasted_iota(jnp.int32, (tm, tm), 0)
    dst = lax.broadcasted_iota(jnp.int32, (tm, tm), 1)
    before = jnp.where(src < dst, 1.0, 0.0).astype(BF16)
    picked = jnp.where(chosen, 1.0, 0.0)
    rank = jnp.dot(picked.astype(BF16), before, preferred_element_type=F32) + count_s[...]
    count_s[...] += jnp.sum(picked, axis=1, keepdims=True)
    cnt_ref[...] = jnp.broadcast_to(count_s[...], cnt_ref.shape)
    e_io = lax.broadcasted_iota(jnp.int32, w_t.shape, 0)
    pick = lambda hit, val: jnp.sum(jnp.where(hit, val, jnp.zeros_like(val)), axis=0, keepdims=True)
    eidx_ref[...] = jnp.concatenate([pick(h, e_io) for h in hits], axis=0)
    rank_ref[...] = jnp.concatenate([pick(h, rank) for h in hits], axis=0).astype(jnp.int32)
    wk = jnp.concatenate([pick(h, w_t) for h in hits] + [jnp.zeros((LANES - TOP_K, tm), F32)], axis=0)
    wk_ref[...] = wk.T


def _output_stage(oa, od, hf, hb, mo, of, ob, rg, x, mod, mg, npost, npre, wout, wrt, rb, st):
    t_all = x.shape[0]
    row = lambda i: (i, 0)
    const2 = lambda i: (0, 0)
    tok = lambda w: pl.BlockSpec((TM, w), row)
    full = lambda a: pl.BlockSpec(a.shape, const2)
    return pl.pallas_call(
        _out_kernel,
        grid=(t_all // TM,),
        in_specs=[tok(256)] * 8 + [tok(D_MODEL), _mod_spec(st),
                                   full(mg), full(npost), full(npre), full(wout), full(wrt), full(rb)],
        out_specs=[tok(D_MODEL), tok(D_MODEL), tok(D_MODEL // 2),
                   pl.BlockSpec((TOP_K, TM), lambda i: (0, i)), pl.BlockSpec((TOP_K, TM), lambda i: (0, i)),
                   tok(LANES), pl.BlockSpec((N_EXPERTS, LANES), const2)],
        out_shape=[jax.ShapeDtypeStruct((t_all, D_MODEL), F32),
                   jax.ShapeDtypeStruct((t_all, D_MODEL), BF16),
                   jax.ShapeDtypeStruct((t_all, D_MODEL // 2), jnp.uint32),
                   jax.ShapeDtypeStruct((TOP_K, t_all), jnp.int32),
                   jax.ShapeDtypeStruct((TOP_K, t_all), jnp.int32),
                   jax.ShapeDtypeStruct((t_all, LANES), F32),
                   jax.ShapeDtypeStruct((N_EXPERTS, LANES), F32)],
        scratch_shapes=[pltpu.VMEM((N_EXPERTS, 1), F32)],
        compiler_params=_cparams(("arbitrary",)),
        name="output_stage_" + st["name"],
    )(oa, od, hf, hb, mo, of, ob, rg, x, mod, mg, npost, npre, wout, wrt, rb)


SC_CORES, SC_SUBCORES = 2, 16
SC_WORKERS = SC_CORES * SC_SUBCORES
SLOT_BLOCK = 512
DISPATCH_ROWS = 64
COMBINE_ROWS = 64


def _sc_mesh():
    return plsc.VectorSubcoreMesh(core_axis_name="core", subcore_axis_name="subcore")


def _sc_worker():
    return lax.axis_index("subcore") * SC_CORES + lax.axis_index("core")


def _sc_dispatch(hpk, dest, n_slots):
    t_all, width = hpk.shape
    per_worker = t_all // SC_WORKERS
    assert t_all % (SC_WORKERS * DISPATCH_ROWS) == 0

    @functools.partial(
        pl.kernel, mesh=_sc_mesh(), out_type=jax.ShapeDtypeStruct((n_slots, width), hpk.dtype),
        scratch_types=[pltpu.VMEM((DISPATCH_ROWS, width), hpk.dtype)]
        + [pltpu.VMEM((DISPATCH_ROWS,), jnp.int32)] * TOP_K + [pltpu.SemaphoreType.DMA])
    def dispatch(x_hbm, d_hbm, o_hbm, rows_v, *rest):
        idx, sem = rest[:TOP_K], rest[TOP_K]

        @pl.loop(0, per_worker // DISPATCH_ROWS)
        def _(j):
            base = _sc_worker() * per_worker + j * DISPATCH_ROWS
            pltpu.sync_copy(x_hbm.at[pl.ds(base, DISPATCH_ROWS)], rows_v)
            for k in range(TOP_K):
                pltpu.sync_copy(d_hbm.at[pl.ds(k * t_all + base, DISPATCH_ROWS)], idx[k])
            copies = [pltpu.async_copy(rows_v, o_hbm.at[idx[k]], sem) for k in range(TOP_K)]
            for c in copies:
                c.wait()

    return dispatch(hpk, dest)


def _sc_combine(yb, dest, t_all):
    width = yb.shape[1]
    per_worker = t_all // SC_WORKERS
    assert t_all % (SC_WORKERS * COMBINE_ROWS) == 0

    @functools.partial(
        pl.kernel, mesh=_sc_mesh(), out_type=jax.ShapeDtypeStruct((TOP_K, t_all, width), yb.dtype),
        scratch_types=[pltpu.VMEM((COMBINE_ROWS, width), yb.dtype)] * 2
        + [pltpu.VMEM((COMBINE_ROWS,), jnp.int32)] * TOP_K + [pltpu.SemaphoreType.DMA] * 2)
    def combine(y_hbm, d_hbm, o_hbm, rows_a, rows_b, *rest):
        idx, sems = rest[:TOP_K], rest[TOP_K:]
        bufs = (rows_a, rows_b)

        @pl.loop(0, per_worker // COMBINE_ROWS)
        def _(j):
            base = _sc_worker() * per_worker + j * COMBINE_ROWS
            for k in range(TOP_K):
                pltpu.sync_copy(d_hbm.at[pl.ds(k * t_all + base, COMBINE_ROWS)], idx[k])
            pending = pltpu.async_copy(y_hbm.at[idx[0]], bufs[0], sems[0])
            for k in range(TOP_K):
                pending.wait()
                if k + 1 < TOP_K:
                    pending = pltpu.async_copy(y_hbm.at[idx[k + 1]], bufs[(k + 1) % 2], sems[(k + 1) % 2])
                pltpu.sync_copy(bufs[k % 2], o_hbm.at[k, pl.ds(base, COMBINE_ROWS)])

    return combine(yb, dest)


def _pack_rows(x):
    bits = pltpu.bitcast(x.astype(BF16).astype(F32), jnp.uint32)
    w = x.shape[-1] // 2
    return (bits[..., :w] >> 16) | (bits[..., w:] & jnp.uint32(0xFFFF0000))


def _unpack_rows(words):
    return pltpu.bitcast(words << 16, F32), pltpu.bitcast(words & jnp.uint32(0xFFFF0000), F32)


def _expert_kernel(be_ref, used_ref, xs_ref, wg_ref, wu_ref, wd_ref, y_ref, wg_s, wu_s, wd_s):
    b = pl.program_id(0)
    fresh = jnp.logical_or(b == 0, be_ref[b] != be_ref[jnp.maximum(b - 1, 0)])

    @pl.when(jnp.logical_and(fresh, b < used_ref[0]))
    def _():
        wg_s[...] = wg_ref[0, 0].astype(BF16)
        wu_s[...] = wu_ref[0, 0].astype(BF16)
        wd_s[...] = wd_ref[0, 0].astype(BF16)

    @pl.when(b < used_ref[0])
    def _():
        lo, hi = (v.astype(BF16) for v in _unpack_rows(xs_ref[...]))
        half = D_MODEL // 2
        gate = (jnp.dot(lo, wg_s[:half, :], preferred_element_type=F32)
                + jnp.dot(hi, wg_s[half:, :], preferred_element_type=F32))
        up = (jnp.dot(lo, wu_s[:half, :], preferred_element_type=F32)
              + jnp.dot(hi, wu_s[half:, :], preferred_element_type=F32))
        y = jnp.dot((_silu(gate) * up).astype(BF16), wd_s[...], preferred_element_type=F32)
        y_ref[...] = _pack_rows(y)


def _expert_blocks(layer, xs, block_expert, blocks_used, wg, wu, wdn):
    n_slots = xs.shape[0]
    wspec = lambda shape: pl.BlockSpec((1, 1) + shape, lambda b, be, used: (layer, be[b], 0, 0))
    rows = lambda b, be, used: (jnp.minimum(b, used[0] - 1), 0)
    grid_spec = pltpu.PrefetchScalarGridSpec(
        num_scalar_prefetch=2,
        grid=(n_slots // SLOT_BLOCK,),
        in_specs=[pl.BlockSpec((SLOT_BLOCK, D_MODEL // 2), rows),
                  wspec((D_MODEL, D_EXPERT)), wspec((D_MODEL, D_EXPERT)), wspec((D_EXPERT, D_MODEL))],
        out_specs=pl.BlockSpec((SLOT_BLOCK, D_MODEL // 2), rows),
        scratch_shapes=[pltpu.VMEM((D_MODEL, D_EXPERT), BF16), pltpu.VMEM((D_MODEL, D_EXPERT), BF16),
                        pltpu.VMEM((D_EXPERT, D_MODEL), BF16)])
    return pl.pallas_call(
        _expert_kernel, grid_spec=grid_spec,
        out_shape=jax.ShapeDtypeStruct((n_slots, D_MODEL // 2), jnp.uint32),
        compiler_params=_cparams(("arbitrary",)),
        name="moe_experts",
    )(block_expert, blocks_used, xs, wg, wu, wdn)


def _moe_out_kernel(g_ref, wk_ref, h_ref, x1_ref, mod_ref, npost_ref, sg_ref, su_ref, sd_ref, o_ref):
    h = h_ref[...]
    act = _silu(jnp.dot(h, sg_ref[...], preferred_element_type=F32)) * jnp.dot(h, su_ref[...],
                                                                              preferred_element_type=F32)
    moe = jnp.dot(act.astype(BF16), sd_ref[...], preferred_element_type=F32)
    wk = wk_ref[...]
    lo, hi = 0.0, 0.0
    for k in range(TOP_K):
        g_lo, g_hi = _unpack_rows(g_ref[k])
        lo = lo + g_lo * wk[:, k:k + 1]
        hi = hi + g_hi * wk[:, k:k + 1]
    moe = moe + jnp.concatenate([lo, hi], axis=-1)
    m = mod_ref[0]
    o_ref[...] = x1_ref[...] + m[5:6] * (_rms(moe) * npost_ref[...])


def _moe_out_stage(g, wk, h2, x1, mod, npost, sg, su, sd, st):
    t_all = h2.shape[0]
    row = lambda i: (i, 0)
    full = lambda a: pl.BlockSpec(a.shape, lambda i: (0, 0))
    return pl.pallas_call(
        _moe_out_kernel,
        grid=(t_all // TM,),
        in_specs=[pl.BlockSpec((TOP_K, TM, D_MODEL // 2), lambda i: (0, i, 0)), pl.BlockSpec((TM, LANES), row),
                  pl.BlockSpec((TM, D_MODEL), row), pl.BlockSpec((TM, D_MODEL), row), _mod_spec(st),
                  full(npost), full(sg), full(su), full(sd)],
        out_specs=pl.BlockSpec((TM, D_MODEL), row),
        out_shape=jax.ShapeDtypeStruct((t_all, D_MODEL), F32),
        compiler_params=_cparams(("parallel",)),
        name="moe_combine_" + st["name"],
    )(g, wk, h2, x1, mod, npost, sg, su, sd)


def _moe_stage(layer, h2, hpk, eidx, rank, wk, counts, x1, mod, npost, wg, wu, wdn, sg, su, sd, st):
    t_all = h2.shape[0]
    n_blocks = -(-(t_all * TOP_K + N_EXPERTS * (SLOT_BLOCK - 1)) // SLOT_BLOCK)
    cnt = counts[:, 0].astype(jnp.int32)
    padded = (cnt + SLOT_BLOCK - 1) // SLOT_BLOCK * SLOT_BLOCK
    pad_end = jnp.cumsum(padded)
    pad_start = pad_end - padded
    experts = jnp.arange(N_EXPERTS, dtype=jnp.int32)[:, None, None]
    dest = rank + jnp.sum(jnp.where(eidx[None] == experts, pad_start[:, None, None], 0), axis=0)
    dest = dest.reshape(TOP_K * t_all)
    block_start = jnp.arange(n_blocks, dtype=jnp.int32) * SLOT_BLOCK
    block_expert = jnp.minimum(jnp.sum((pad_end[None, :] <= block_start[:, None]).astype(jnp.int32), axis=1),
                               N_EXPERTS - 1)
    blocks_used = (pad_end[-1:] // SLOT_BLOCK).astype(jnp.int32)

    xs = _sc_dispatch(hpk, dest, n_blocks * SLOT_BLOCK)
    yb = _expert_blocks(layer, xs, block_expert, blocks_used, wg, wu, wdn)
    g = _sc_combine(yb, dest, t_all)
    return _moe_out_stage(g, wk, h2, x1, mod, npost, sg, su, sd, st)


_PERM32 = np.concatenate([np.arange(8, 16), np.arange(0, 8), np.arange(24, 32), np.arange(16, 24)])


def _rope_tables(n):
    pos = jnp.arange(n)
    quarter = QK_ROPE // 4
    inv = 1.0 / (ROPE_THETA ** (jnp.arange(quarter, dtype=F32) / quarter))
    ang_r = (pos // GRID_W).astype(F32)[:, None] * inv[None, :]
    ang_c = (pos % GRID_W).astype(F32)[:, None] * inv[None, :]
    cos32 = jnp.concatenate([jnp.cos(ang_r)] * 2 + [jnp.cos(ang_c)] * 2, axis=-1)
    sin32 = jnp.concatenate([-jnp.sin(ang_r), jnp.sin(ang_r), -jnp.sin(ang_c), jnp.sin(ang_c)], axis=-1)
    cos256, sin256 = jnp.tile(cos32, (1, 8)), jnp.tile(sin32, (1, 8))
    one, zero = jnp.ones((n, QK_NOPE), F32), jnp.zeros((n, QK_NOPE), F32)
    cosq = jnp.concatenate([one, cos32] * N_HEADS, axis=-1)
    sinq = jnp.concatenate([zero, sin32] * N_HEADS, axis=-1)
    return cos256, sin256, cosq, sinq


def _split_w_in(w_in_l):
    sizes = (256, 256, 256, 256, 256, 256, 256, 16, 256, 256, 256, 256, Q_LORA, KV_LORA, QK_ROPE)
    idx = np.cumsum(sizes)[:-1]
    return jnp.split(w_in_l, [int(v) for v in idx], axis=-1)


def _layer_weights(w_in_l, w_uq_l):
    (a_q, a_k, a_v, m_q, m_k, m_v, m_o, m_g, r_q, r_k, r_v, r_g, d_cq, d_ckv, d_kr) = _split_w_in(w_in_l)
    perm256 = np.concatenate([_PERM32 + 32 * j for j in range(8)])
    pad = lambda w, n: jnp.pad(w, ((0, 0), (0, n - w.shape[1])))
    tail = pad(jnp.concatenate([d_kr, m_g], axis=-1), LANES)
    tail_s = pad(d_kr[:, _PERM32], LANES)
    w_all = jnp.concatenate([a_q, a_k, a_v, m_q, m_k, m_v, m_o, r_q, r_k, r_v, r_g, d_cq, d_ckv, tail,
                             a_q[:, perm256], a_k[:, perm256], tail_s], axis=-1).astype(BF16)
    per = QK_NOPE + QK_ROPE
    permq = np.concatenate([np.concatenate([np.arange(QK_NOPE), QK_NOPE + _PERM32]) + per * j
                            for j in range(N_HEADS)])
    return w_all, w_uq_l.astype(BF16), w_uq_l[:, permq].astype(BF16)


def kernel(x_prompt, x_sample, cache_diff_k, cache_diff_v, state_mlstm_C, state_mlstm_n, state_mlstm_m, state_ret_S, cache_mla_ckv, cache_mla_krope, c, c_ctx, w_mod, b_mod, norm_pre, norm_post, w_in, w_out, diff_lambda, diff_norm, mlstm_gate_bias, mlstm_norm, ret_decay_logit, mla_q_norm, mla_w_uq, mla_kv_norm, mla_w_ukv, moe_w_router, moe_router_bias, moe_w_gate, moe_w_up, moe_w_down, shared_w_gate, shared_w_up, shared_w_down):
    bp, n_p, _ = x_prompt.shape
    bs, n_s, _ = x_sample.shape
    depth = w_in.shape[0]
    past = cache_diff_k.shape[2]
    assert n_p % TM == 0 and n_s % TM == 0 and n_p % CHUNK == 0 and n_s % min(TK, n_s) == 0
    assert past % min(TK, past) == 0 and bs + 1 <= 8 and n_s % GRID_W == 0

    streams = (dict(name="ctx", nb=bp, n=n_p, latent=False), dict(name="latent", nb=bs, n=n_s, latent=True))
    xs_by_stream = [x_prompt.reshape(bp * n_p, D_MODEL), x_sample.reshape(bs * n_s, D_MODEL)]
    cond = jnp.zeros((8, D_MODEL), F32).at[0].set(c_ctx).at[1:1 + bs].set(c)
    mod_all = _modulation(cond, w_mod, b_mod).reshape(depth, 8, 6, D_MODEL)
    tabs = _rope_tables(n_s)
    scan_tabs = [_scan_table(st) for st in streams]

    outs = [[] for _ in range(8)]
    for l in range(depth):
        lam_init = 0.8 - 0.6 * math.exp(-0.3 * l)
        mod = mod_all[l]
        w_all, wuq, wuqs = _layer_weights(w_in[l], mla_w_uq[l])
        wukv = mla_w_ukv[l].astype(BF16)
        wout, wrt, rbias = w_out[l].astype(BF16), moe_w_router[l].T.astype(BF16), moe_router_bias[l][:, None]
        shared = [w[l].astype(BF16) for w in (shared_w_gate, shared_w_up, shared_w_down)]
        lamv, dg = diff_lambda[l], diff_norm[l][:, None]
        bias_row = jnp.zeros((1, LANES), F32).at[0, TAIL_GATE0:TAIL_GATE0 + 16].set(mlstm_gate_bias[l].reshape(16))

        with_ones = lambda vt: jnp.concatenate([vt, jnp.ones((N_HEADS, VT_ONES, bs * past), BF16)], axis=1)
        ck = jnp.transpose(cache_diff_k[:, l], (2, 0, 1, 3)).reshape(N_HEADS, bs * past, HEAD_DIM).astype(BF16)
        cvt = with_ones(jnp.transpose(cache_diff_v[:, l], (2, 3, 0, 1)).reshape(N_HEADS, HEAD_DIM, bs * past)
                        .astype(BF16))
        kvc = _project(cache_mla_ckv[:, l].reshape(bs * past, KV_LORA), wukv, BF16)
        kvc = kvc.reshape(bs * past, N_HEADS, QK_NOPE + V_HEAD)
        krc = jnp.broadcast_to(cache_mla_krope[:, l].reshape(bs * past, 1, QK_ROPE).astype(BF16),
                               (bs * past, N_HEADS, QK_ROPE))
        kc = jnp.transpose(jnp.concatenate([kvc[..., :QK_NOPE], krc], axis=-1), (1, 0, 2))
        vct = with_ones(jnp.transpose(kvc[..., QK_NOPE:], (1, 2, 0)))

        for si, st in enumerate(streams):
            x, nb, n = xs_by_stream[si], st["nb"], st["n"]
            (aq1t, aq2t, ak1, ak2, avt, ak, av, mqkv, mo, rqkv, rg, tail, qmlat, ckv, kmla, vmlat) = _input_stage(
                x, mod, norm_pre[l, 0:1], w_all, tabs, mla_q_norm[l][None], wuq, wuqs,
                mla_kv_norm[l][None], wukv, st)
            if st["latent"]:
                oa = _attention_t("diff_attn_latent", [aq1t, aq2t], [ak1, ak2], avt,
                                  [ck[..., :DIFF_HALF], ck[..., DIFF_HALF:]], cvt, [lamv, dg], nb, n, past, lam_init)
                od = _attention_t("mla_attn_latent", [qmlat], [kmla], vmlat, [kc], vct, [], nb, n, past)
                states = [s[:, l] for s in (state_mlstm_C, state_mlstm_n, state_mlstm_m, state_ret_S)]
            else:
                oa = _attention_t("diff_attn_ctx", [aq1t, aq2t], [ak1, ak2], avt, None, None, [lamv, dg],
                                  nb, n, 0, lam_init)
                od = _attention_t("mla_attn_ctx", [qmlat], [kmla], vmlat, None, None, [], nb, n, 0)
                states = [jnp.zeros((nb,) + s.shape[2:], F32)
                          for s in (state_mlstm_C, state_mlstm_n, state_mlstm_m, state_ret_S)]
            hf, hb, of, ob, c_n, n_n, m_n, s_n = _scans(scan_tabs[si], mqkv, tail, bias_row, rqkv,
                                                        ret_decay_logit[l], *states)

            x1, h2, hpk, eidx, rank, wk, counts = _output_stage(
                oa, od, hf, hb, mo, of, ob, rg, x, mod, mlstm_norm[l][None], norm_post[l, 0:1],
                norm_pre[l, 1:2], wout, wrt, rbias, st)
            xs_by_stream[si] = _moe_stage(l, h2, hpk, eidx, rank, wk, counts, x1, mod, norm_post[l, 1:2],
                                          moe_w_gate, moe_w_up, moe_w_down, *shared, st)
            if not st["latent"]:
                new = (ak.reshape(nb, n, N_HEADS, HEAD_DIM), av.reshape(nb, n, N_HEADS, HEAD_DIM), c_n, n_n, m_n,
                       s_n, ckv.reshape(nb, n, KV_LORA), tail[:, :QK_ROPE].reshape(nb, n, QK_ROPE))
                for o, v in zip(outs, new):
                    o.append(v)

    return (xs_by_stream[0].reshape(bp, n_p, D_MODEL), xs_by_stream[1].reshape(bs, n_s, D_MODEL)) + tuple(
        jnp.stack(o, axis=1) for o in outs)
```

```python
import functools
import math

import numpy as np
import jax
import jax.numpy as jnp
from jax import lax
from jax.experimental import pallas as pl
from jax.experimental.pallas import tpu as pltpu
from jax.experimental.pallas import tpu_sc as plsc

F32 = jnp.float32
BF16 = jnp.bfloat16
HIGHEST = lax.Precision.HIGHEST

D_MODEL = 1024
GRID_W = 64
GROUP_WIDTH = 256
HEAD_DIM = 64
N_HEADS = 4
DIFF_HALF = 32
ROPE_THETA = 10000.0
Q_LORA = 256
KV_LORA = 128
QK_NOPE = 64
QK_ROPE = 32
V_HEAD = 64
N_EXPERTS = 64
TOP_K = 8
N_GROUPS = 8
TOPK_GROUPS = 4
D_EXPERT = 256
ROUTED_SCALE = 2.5
CHUNK = 128
EPS = 1e-6
Q_MLA = N_HEADS * (QK_NOPE + QK_ROPE)
KV_MLA = N_HEADS * (QK_NOPE + V_HEAD)

LANES = 128
VMEM_LIMIT = 56 * 1024 * 1024

TM = 256
TM_IN = 512
TQ = 256
TK = 4096

C_AQ, C_AK, C_AV = 0, 256, 512
C_MQKV, C_MO = 768, 1536
C_RQKV, C_RG = 1792, 2560
C_CQ, C_CKV, C_TAIL = 2816, 3072, 3200
C_AQS, C_AKS, C_TAILS = 3328, 3584, 3840
W_ALL = 3968
TAIL_GATE0 = QK_ROPE
VT_ONES = 16
VT_ROWS = V_HEAD + VT_ONES


def _cparams(sem, flags=None):
    return pltpu.CompilerParams(dimension_semantics=sem, vmem_limit_bytes=VMEM_LIMIT, flags=flags)


def _rms(x):
    return x * lax.rsqrt(jnp.mean(x * x, axis=-1, keepdims=True) + EPS)


def _head_mean_matrix(width):
    r = lax.broadcasted_iota(jnp.int32, (width, width), 0) // HEAD_DIM
    c = lax.broadcasted_iota(jnp.int32, (width, width), 1) // HEAD_DIM
    return jnp.where(r == c, 1.0 / HEAD_DIM, 0.0).astype(BF16)


def _rms_heads(x):
    sq = x * x
    hi = sq.astype(BF16)
    lo = (sq - hi.astype(F32)).astype(BF16)
    g = _head_mean_matrix(x.shape[-1])
    ms = jnp.dot(hi, g, preferred_element_type=F32) + jnp.dot(lo, g, preferred_element_type=F32)
    return x * lax.rsqrt(ms + EPS)


def _sigmoid(x):
    return 1.0 / (1.0 + jnp.exp(-x))


def _silu(x):
    return x * _sigmoid(x)


def _log_sigmoid(x):
    return jnp.minimum(x, 0.0) - jnp.log1p(jnp.exp(-jnp.abs(x)))


def _bdot(a, b):
    return jnp.dot(a.astype(BF16), b.astype(BF16), preferred_element_type=F32)


def _mod_kernel(c_ref, w_ref, b_ref, o_ref):
    o_ref[0] = _bdot(_silu(c_ref[...]), w_ref[0]) + b_ref[0]


def _modulation(cond, w_mod, b_mod):
    depth, _, n = w_mod.shape
    tn = 1536
    return pl.pallas_call(
        _mod_kernel,
        grid=(depth, n // tn),
        in_specs=[pl.BlockSpec((8, D_MODEL), lambda l, j: (0, 0)),
                  pl.BlockSpec((1, D_MODEL, tn), lambda l, j: (l, 0, j)),
                  pl.BlockSpec((1, 1, tn), lambda l, j: (l, 0, j))],
        out_specs=pl.BlockSpec((1, 8, tn), lambda l, j: (l, 0, j)),
        out_shape=jax.ShapeDtypeStruct((depth, 8, n), F32),
        compiler_params=_cparams(("parallel", "parallel")),
        name="adaln_mod",
    )(cond, w_mod, b_mod.reshape(depth, 1, n))


def _in_kernel(x_ref, mod_ref, npre_ref, w_ref, cos_ref, sin_ref, cosq_ref, sinq_ref,
               qg_ref, wuq_ref, wuqs_ref, kvg_ref, wukv_ref,
               aq1_ref, aq2_ref, ak1t_ref, ak2t_ref, avh_ref, ak_ref, av_ref,
               mqkv_ref, mo_ref, rqkv_ref, rg_ref, tail_ref,
               qmla_ref, ckv_ref, kmlat_ref, vmla_ref, *, latent):
    m = mod_ref[0]
    h = (_rms(x_ref[...]) * npre_ref[...] * (1.0 + m[1:2]) + m[0:1]).astype(BF16)

    def proj(c0, width):
        return jnp.dot(h, w_ref[:, c0:c0 + width], preferred_element_type=F32)

    def rotated(c0, c0_swapped, width, cos, sin):
        return proj(c0, width) * cos + proj(c0_swapped, width) * sin if latent else proj(c0, width)

    cos = cos_ref[...]
    sin = sin_ref[...]
    aq = rotated(C_AQ, C_AQS, 256, cos, sin) * (DIFF_HALF ** -0.5)
    ak = rotated(C_AK, C_AKS, 256, cos, sin)
    av = proj(C_AV, 256)
    ak_ref[...] = ak
    av_ref[...] = av
    aq_t, av_t = aq.T.astype(BF16), av.T.astype(BF16)
    ones = jnp.ones((VT_ONES, aq_t.shape[1]), BF16)
    for hd in range(N_HEADS):
        lo = hd * HEAD_DIM
        aq1_ref[hd] = aq_t[lo:lo + DIFF_HALF, :]
        aq2_ref[hd] = aq_t[lo + DIFF_HALF:lo + HEAD_DIM, :]
        ak1t_ref[hd] = ak[:, lo:lo + DIFF_HALF].astype(BF16)
        ak2t_ref[hd] = ak[:, lo + DIFF_HALF:lo + HEAD_DIM].astype(BF16)
        avh_ref[hd, 0:HEAD_DIM, :] = av_t[lo:lo + HEAD_DIM, :]
        avh_ref[hd, HEAD_DIM:VT_ROWS, :] = ones

    mqkv_ref[...] = proj(C_MQKV, 768).astype(BF16)
    mo_ref[...] = proj(C_MO, 256)
    rqkv_ref[...] = proj(C_RQKV, 768).astype(BF16)
    rg_ref[...] = proj(C_RG, 256)

    tail = proj(C_TAIL, LANES)
    tail_ref[...] = tail
    kr = tail * cos[:, :LANES] + proj(C_TAILS, LANES) * sin[:, :LANES] if latent else tail
    kr = kr[:, :QK_ROPE].astype(BF16)

    cqn = (_rms(proj(C_CQ, Q_LORA)) * qg_ref[...]).astype(BF16)
    qmla = jnp.dot(cqn, wuq_ref[...], preferred_element_type=F32)
    if latent:
        qmla = qmla * cosq_ref[...] + jnp.dot(cqn, wuqs_ref[...], preferred_element_type=F32) * sinq_ref[...]
    qmla_t = (qmla * ((QK_NOPE + QK_ROPE) ** -0.5)).T.astype(BF16)
    for hd in range(N_HEADS):
        lo = hd * (QK_NOPE + QK_ROPE)
        qmla_ref[hd] = qmla_t[lo:lo + QK_NOPE + QK_ROPE, :]
    ckvn = _rms(proj(C_CKV, KV_LORA)) * kvg_ref[...]
    ckv_ref[...] = ckvn
    kv = jnp.dot(ckvn.astype(BF16), wukv_ref[...], preferred_element_type=F32)
    kv_t = kv.T.astype(BF16)
    per = QK_NOPE + V_HEAD
    for hd in range(N_HEADS):
        kmlat_ref[hd, :, 0:QK_NOPE] = kv[:, hd * per:hd * per + QK_NOPE].astype(BF16)
        kmlat_ref[hd, :, QK_NOPE:QK_NOPE + QK_ROPE] = kr
        vmla_ref[hd, 0:V_HEAD, :] = kv_t[hd * per + QK_NOPE:(hd + 1) * per, :]
        vmla_ref[hd, V_HEAD:VT_ROWS, :] = ones


def _mod_spec(st, tm=TM):
    tps = st["n"] // tm
    return pl.BlockSpec((1, 6, D_MODEL), lambda i, *_: (1 + i // tps if st["latent"] else 0, 0, 0))


def _input_stage(x, mod, npre, w_all, tabs, qg, wuq, wuqs, kvg, wukv, st):
    t_all = x.shape[0]
    tm = TM_IN
    tps = st["n"] // tm
    row = lambda i: (i, 0)
    tab = lambda w: pl.BlockSpec((tm, w), lambda i: (i % tps if st["latent"] else 0, 0))
    hrow = lambda i: (0, i, 0)
    const2 = lambda i: (0, 0)
    tok = lambda w: pl.BlockSpec((tm, w), row)
    headed = lambda w: pl.BlockSpec((N_HEADS, tm, w), hrow)
    headed_t = lambda d: pl.BlockSpec((N_HEADS, d, tm), lambda i: (0, 0, i))
    full = lambda a: pl.BlockSpec(a.shape, const2)
    cos, sin, cosq, sinq = tabs
    out_shapes = [
        (headed_t(DIFF_HALF), (N_HEADS, DIFF_HALF, t_all), BF16),
        (headed_t(DIFF_HALF), (N_HEADS, DIFF_HALF, t_all), BF16),
        (headed(DIFF_HALF), (N_HEADS, t_all, DIFF_HALF), BF16),
        (headed(DIFF_HALF), (N_HEADS, t_all, DIFF_HALF), BF16),
        (headed_t(VT_ROWS), (N_HEADS, VT_ROWS, t_all), BF16),
        (tok(256), (t_all, 256), F32),
        (tok(256), (t_all, 256), F32),
        (tok(768), (t_all, 768), BF16),
        (tok(256), (t_all, 256), F32),
        (tok(768), (t_all, 768), BF16),
        (tok(256), (t_all, 256), F32),
        (tok(LANES), (t_all, LANES), F32),
        (headed_t(QK_NOPE + QK_ROPE), (N_HEADS, QK_NOPE + QK_ROPE, t_all), BF16),
        (tok(KV_LORA), (t_all, KV_LORA), F32),
        (headed(QK_NOPE + QK_ROPE), (N_HEADS, t_all, QK_NOPE + QK_ROPE), BF16),
        (headed_t(VT_ROWS), (N_HEADS, VT_ROWS, t_all), BF16),
    ]
    return pl.pallas_call(
        functools.partial(_in_kernel, latent=st["latent"]),
        grid=(t_all // tm,),
        in_specs=[tok(D_MODEL), _mod_spec(st, tm),
                  full(npre), full(w_all), tab(256), tab(256), tab(Q_MLA), tab(Q_MLA),
                  full(qg), full(wuq), full(wuqs), full(kvg), full(wukv)],
        out_specs=[s for s, _, _ in out_shapes],
        out_shape=[jax.ShapeDtypeStruct(shp, dt) for _, shp, dt in out_shapes],
        compiler_params=_cparams(("parallel",)),
        name="input_stage_" + st["name"],
    )(x, mod, npre, w_all, cos, sin, cosq, sinq, qg, wuq, wuqs, kvg, wukv)


def _attn_t_kernel(*refs, n_soft, n_new, n_ctx, lam_init):
    refs = list(refs)
    qt_refs, k_refs, vt_ref = refs[:n_soft], refs[n_soft:2 * n_soft], refs[2 * n_soft]
    pos = 2 * n_soft + 1
    if n_ctx:
        ck_refs, cvt_ref = refs[pos:pos + n_soft], refs[pos + n_soft]
        pos += n_soft + 1
    if n_soft == 2:
        lam_ref, g_ref = refs[pos:pos + 2]
        pos += 2
    o_ref = refs[pos]
    tq = o_ref.shape[0]
    nchain = n_soft * N_HEADS
    order = [(j, hd) for hd in range(N_HEADS) for j in range(n_soft)]

    def chunk(state, k_of, vt_of):
        ms, accs = state
        new_m, new_acc = list(ms), list(accs)
        ss = [jnp.dot(k_of(j, hd), qt_refs[j][hd], preferred_element_type=F32) for j, hd in order]
        ps, alphas = [], []
        for (j, hd), s in zip(order, ss):
            c = j * N_HEADS + hd
            s3 = s.reshape(s.shape[0] // 8, 8, tq)
            top = jnp.max(jnp.max(s3, axis=0), axis=0, keepdims=True)
            m_new = jnp.maximum(ms[c], jnp.broadcast_to(top, (8, tq)))
            ps.append(jnp.exp(s3 - m_new[None]).reshape(s.shape).astype(BF16))
            alphas.append(jnp.exp(ms[c] - m_new))
            new_m[c] = m_new
        for (j, hd), p, alpha in zip(order, ps, alphas):
            c = j * N_HEADS + hd
            scaled = (accs[c].reshape(VT_ROWS // 8, 8, tq) * alpha[None]).reshape(VT_ROWS, tq)
            new_acc[c] = scaled + jnp.dot(vt_of(hd), p, preferred_element_type=F32)
        return tuple(new_m), tuple(new_acc)

    state = (tuple(jnp.full((8, tq), -jnp.inf, F32) for _ in range(nchain)),
             tuple(jnp.zeros((VT_ROWS, tq), F32) for _ in range(nchain)))
    if n_ctx:
        cstep = min(TK, n_ctx)
        for i in range(n_ctx // cstep):
            state = chunk(state, lambda j, hd, i=i: ck_refs[j][hd, i * cstep:(i + 1) * cstep, :],
                          lambda hd, i=i: cvt_ref[hd, :, i * cstep:(i + 1) * cstep])
    step = min(TK, n_new)

    def body(i, st):
        start = pl.multiple_of(i * step, step)
        return chunk(st, lambda j, hd: k_refs[j][hd, pl.ds(start, step), :],
                     lambda hd: vt_ref[hd, :, pl.ds(start, step)])

    _, accs = lax.fori_loop(0, n_new // step, body, state)

    def normalised(c):
        num, den = accs[c][:V_HEAD], accs[c][V_HEAD:V_HEAD + 8]
        return (num.reshape(V_HEAD // 8, 8, tq) / den[None]).reshape(V_HEAD, tq)

    if n_soft == 2:
        lv = lam_ref[...]
        lam = (jnp.exp(jnp.sum(lv[0:1] * lv[1:2], axis=-1, keepdims=True))
               - jnp.exp(jnp.sum(lv[2:3] * lv[3:4], axis=-1, keepdims=True)) + lam_init)
    outs = []
    for hd in range(N_HEADS):
        out = normalised(hd)
        if n_soft == 2:
            a = out - lam * normalised(N_HEADS + hd)
            out = a * lax.rsqrt(jnp.mean(a * a, axis=0, keepdims=True) + EPS) * g_ref[...] * (1.0 - lam_init)
        outs.append(out)
    o_ref[...] = jnp.concatenate(outs, axis=0).T


def _attention_t(name, qts, ks, vt, ctx_ks, ctx_vt, extras, nb, n, n_ctx, lam_init=0.0):
    tq = min(TQ, n)
    nqt = n // tq
    n_soft = len(qts)
    in_specs = [pl.BlockSpec((N_HEADS, a.shape[1], tq), lambda b, i: (0, 0, b * nqt + i)) for a in qts]
    in_specs += [pl.BlockSpec((N_HEADS, n, a.shape[-1]), lambda b, i: (0, b, 0)) for a in ks]
    in_specs += [pl.BlockSpec((N_HEADS, VT_ROWS, n), lambda b, i: (0, 0, b))]
    args = list(qts) + list(ks) + [vt]
    if n_ctx:
        in_specs += [pl.BlockSpec((N_HEADS, n_ctx, a.shape[-1]), lambda b, i: (0, b, 0)) for a in ctx_ks]
        in_specs += [pl.BlockSpec((N_HEADS, VT_ROWS, n_ctx), lambda b, i: (0, 0, b))]
        args += list(ctx_ks) + [ctx_vt]
    in_specs += [pl.BlockSpec(a.shape, lambda b, i: (0, 0)) for a in extras]
    args += list(extras)
    return pl.pallas_call(
        functools.partial(_attn_t_kernel, n_soft=n_soft, n_new=n, n_ctx=n_ctx, lam_init=lam_init),
        grid=(nb, nqt),
        in_specs=in_specs,
        out_specs=pl.BlockSpec((tq, N_HEADS * V_HEAD), lambda b, i: (b * nqt + i, 0)),
        out_shape=jax.ShapeDtypeStruct((nb * n, N_HEADS * V_HEAD), F32),
        compiler_params=_cparams(("parallel", "parallel")),
        name=name,
    )(*args)


def _tri(lower):
    r = lax.broadcasted_iota(jnp.int32, (CHUNK, CHUNK), 0)
    c = lax.broadcasted_iota(jnp.int32, (CHUNK, CHUNK), 1)
    return (c <= r) if lower else (c >= r)


def _nt(a, b):
    return lax.dot_general(a, b, (((1,), (1,)), ((), ())), preferred_element_type=F32)


def _tn(a, b):
    return lax.dot_general(a, b, (((0,), (0,)), ((), ())), preferred_element_type=F32)


def _head_slices(qkv, hd):
    lo = hd * HEAD_DIM
    return qkv[:, lo:lo + HEAD_DIM], qkv[:, 256 + lo:256 + lo + HEAD_DIM], qkv[:, 512 + lo:512 + lo + HEAD_DIM]


def _row_dot(row, mat, transpose_mat):
    row8 = jnp.broadcast_to(row, (8, row.shape[1])).astype(BF16)
    dims = (((1,), (1,)), ((), ())) if transpose_mat else (((1,), (0,)), ((), ()))
    return lax.dot_general(row8, mat, dims, preferred_element_type=F32)[0:1]


def _load_state_t(dst, src):
    for d in range(2):
        for hd in range(N_HEADS):
            dst[d, hd] = src[d, hd].T


def _mlstm_step(dirs, bias_ref, ct_s, n_s, m_s):
    k_scale = HEAD_DIM ** -0.5
    lower = _tri(True).astype(F32)
    upper = _tri(False).astype(F32)
    chains = []
    for d, qkv_ref, tail_ref, _ in dirs:
        g = tail_ref[...] + bias_ref[...]
        ls = _log_sigmoid(g)
        g_t, ls_t = g.T, ls.T
        left, right = (lower, upper) if d == 0 else (upper, lower)
        cum_col = jnp.dot(left, ls, precision=HIGHEST, preferred_element_type=F32)
        cum_row = jnp.dot(ls_t, right, precision=HIGHEST, preferred_element_type=F32)
        qkv = qkv_ref[...]
        for hd in range(N_HEADS):
            ci = TAIL_GATE0 + 4 * (2 * d) + hd
            cf = ci + 4
            q, k, v = _head_slices(qkv, hd)
            chains.append(dict(
                d=d, hd=hd, q=q, k=k, v=v, li_row=g_t[ci:ci + 1, :], b_row=cum_row[cf:cf + 1, :],
                c_col=g[:, ci:ci + 1] - cum_col[:, cf:cf + 1],
                m_prev=m_s[d:d + 1, hd:hd + 1], ct=ct_s[d, hd], n=n_s[d, hd:hd + 1, :]))
    for c in chains:
        c["s"] = _nt(c["k"], c["q"])
        c["v_t"] = c["v"].T
    for c in chains:
        valid = _tri(c["d"] != 0)
        log_d = jnp.where(valid, c["b_row"] + c["c_col"], -jnp.inf)
        log_inter = c["b_row"] + c["m_prev"]
        c["m_t"] = jnp.maximum(log_inter, jnp.max(log_d, axis=0, keepdims=True))
        c["w"] = c["s"] * k_scale * jnp.exp(log_d - c["m_t"])
        c["w_inter"] = jnp.exp(log_inter - c["m_t"])
    for c in chains:
        num = (jnp.dot(c["v_t"], c["w"].astype(BF16), preferred_element_type=F32)
               + c["w_inter"] * _nt(c["ct"].astype(BF16), c["q"]))
        den = jnp.sum(c["w"], axis=0, keepdims=True) + c["w_inter"] * _row_dot(c["n"], c["q"], True)
        c["h"] = num / jnp.maximum(jnp.abs(den), jnp.exp(-c["m_t"]))
    for c in chains:
        last = CHUNK - 1 if c["d"] == 0 else 0
        c["m_new"] = c["m_t"][:, last:last + 1]
        b_last = c["b_row"][:, last:last + 1]
        w_end_row = jnp.exp(b_last - c["b_row"] + c["li_row"] - c["m_new"])
        w_end_col = jnp.exp(c["c_col"] + (b_last - c["m_new"]))
        decay = jnp.exp(b_last + c["m_prev"] - c["m_new"])
        kw = (c["k"].astype(F32) * (w_end_col * k_scale)).astype(BF16)
        c["ct_new"] = decay * c["ct"] + jnp.dot(c["v_t"], kw, preferred_element_type=F32)
        c["n_new"] = decay * c["n"] + _row_dot(w_end_row, c["k"], False) * k_scale
    for j, (_, _, _, h_ref) in enumerate(dirs):
        h_ref[...] = jnp.concatenate([c["h"] for c in chains[j * N_HEADS:(j + 1) * N_HEADS]], axis=0).T
    for c in chains:
        d, hd = c["d"], c["hd"]
        ct_s[d, hd] = c["ct_new"]
        n_s[d, hd:hd + 1, :] = c["n_new"]
        m_s[d:d + 1, hd:hd + 1] = c["m_new"]


def _ret_step(dirs, lg_ref, st_s):
    k_scale = HEAD_DIM ** -0.5
    s = lax.broadcasted_iota(jnp.int32, (CHUNK, CHUNK), 0)
    t = lax.broadcasted_iota(jnp.int32, (CHUNK, CHUNK), 1)
    lane = lax.broadcasted_iota(jnp.int32, (1, CHUNK), 1)
    chains = []
    for d, qkv_ref, _ in dirs:
        lag = (t - s) if d == 0 else (s - t)
        pos = (lane if d == 0 else CHUNK - 1 - lane).astype(F32)
        qkv = qkv_ref[...]
        for hd in range(N_HEADS):
            q, k, v = _head_slices(qkv, hd)
            chains.append(dict(d=d, hd=hd, q=q, k=k, v=v, lag=lag, pos=pos, st=st_s[d, hd],
                               lg=_log_sigmoid(lg_ref[d:d + 1, hd:hd + 1])))
    for c in chains:
        c["s"] = _nt(c["k"], c["q"])
        c["v_t"] = c["v"].T
    for c in chains:
        intra = jnp.where(c["lag"] >= 0, jnp.exp(jnp.maximum(c["lag"], 0).astype(F32) * c["lg"]), 0.0)
        c["a"] = (c["s"] * k_scale * intra).astype(BF16)
    for c in chains:
        inter = jnp.exp((c["pos"] + 1.0) * c["lg"])
        c["o"] = jnp.dot(c["v_t"], c["a"], preferred_element_type=F32) + inter * _nt(c["st"].astype(BF16), c["q"])
    row = lax.broadcasted_iota(jnp.int32, (CHUNK, HEAD_DIM), 0)
    for c in chains:
        src_pos = (row if c["d"] == 0 else CHUNK - 1 - row).astype(F32)
        tail = jnp.exp((CHUNK - 1.0 - src_pos) * c["lg"]) * k_scale
        kw = (c["k"].astype(F32) * tail).astype(BF16)
        c["st_new"] = jnp.exp(CHUNK * c["lg"]) * c["st"] + jnp.dot(c["v_t"], kw, preferred_element_type=F32)
    for j, (_, _, o_ref) in enumerate(dirs):
        o_ref[...] = jnp.concatenate([c["o"] for c in chains[j * N_HEADS:(j + 1) * N_HEADS]], axis=0).T
    for c in chains:
        st_s[c["d"], c["hd"]] = c["st_new"]


def _scan_kernel(tab_ref, mqf_ref, tailf_ref, mqb_ref, tailb_ref, rqf_ref, rqb_ref, bias_ref, lg_ref,
                 c0_ref, n0_ref, m0_ref, s0_ref,
                 hf_ref, hb_ref, of_ref, ob_ref, cn_ref, nn_ref, mn_ref, sn_ref, ct_s, n_s, m_s, st_s):
    i = pl.program_id(0)

    @pl.when(tab_ref[3, i] == 1)
    def _():
        _load_state_t(ct_s, c0_ref.at[0])
        _load_state_t(st_s, s0_ref.at[0])
        n_s[...] = n0_ref[0]
        m_s[...] = m0_ref[0]

    _mlstm_step([(0, mqf_ref, tailf_ref, hf_ref), (1, mqb_ref, tailb_ref, hb_ref)], bias_ref, ct_s, n_s, m_s)
    _ret_step([(0, rqf_ref, of_ref), (1, rqb_ref, ob_ref)], lg_ref, st_s)

    @pl.when(tab_ref[4, i] == 1)
    def _():
        _load_state_t(cn_ref.at[0], ct_s)
        _load_state_t(sn_ref.at[0], st_s)
        nn_ref[0] = n_s[...]
        mn_ref[0] = m_s[...]


def _scan_table(st):
    nch = st["n"] // CHUNK
    rows = [(b * nch + c, b * nch + nch - 1 - c, b, int(c == 0), int(c == nch - 1))
            for b in range(st["nb"]) for c in range(nch)]
    return jnp.asarray(np.array(rows, dtype=np.int32).T)


def _scans(tab, mqkv, tail, bias_row, rqkv, decay_logit, c0, n0, m0, s0):
    t_all = mqkv.shape[0]
    fwd = lambda i, tab: (tab[0, i], 0)
    bwd = lambda i, tab: (tab[1, i], 0)
    const = lambda i, tab: (0, 0)
    state = lambda a: pl.BlockSpec((1,) + a.shape[1:], lambda i, tab: (tab[2, i],) + (0,) * (a.ndim - 1))
    states = (c0, n0, m0, s0)
    grid_spec = pltpu.PrefetchScalarGridSpec(
        num_scalar_prefetch=1,
        grid=(tab.shape[1],),
        in_specs=[pl.BlockSpec((CHUNK, 768), fwd), pl.BlockSpec((CHUNK, LANES), fwd),
                  pl.BlockSpec((CHUNK, 768), bwd), pl.BlockSpec((CHUNK, LANES), bwd),
                  pl.BlockSpec((CHUNK, 768), fwd), pl.BlockSpec((CHUNK, 768), bwd),
                  pl.BlockSpec((1, LANES), const), pl.BlockSpec(decay_logit.shape, const)]
        + [state(a) for a in states],
        out_specs=[pl.BlockSpec((CHUNK, 256), fwd), pl.BlockSpec((CHUNK, 256), bwd),
                   pl.BlockSpec((CHUNK, 256), fwd), pl.BlockSpec((CHUNK, 256), bwd)]
        + [state(a) for a in states],
        scratch_shapes=[pltpu.VMEM(a.shape[1:], F32) for a in states],
    )
    return pl.pallas_call(
        _scan_kernel,
        grid_spec=grid_spec,
        out_shape=[jax.ShapeDtypeStruct((t_all, 256), F32)] * 4 + [jax.ShapeDtypeStruct(a.shape, F32) for a in states],
        compiler_params=_cparams(("arbitrary",)),
        name="recurrent_scans",
    )(tab, mqkv, tail, mqkv, tail, rqkv, rqkv, bias_row, decay_logit, *states)


def _proj_kernel(x_ref, w_ref, o_ref):
    o_ref[...] = _bdot(x_ref[...], w_ref[...]).astype(o_ref.dtype)


def _project(x, w, dtype):
    return pl.pallas_call(
        _proj_kernel,
        out_shape=jax.ShapeDtypeStruct((x.shape[0], w.shape[1]), dtype),
        name="ctx_kv_proj",
    )(x, w)


def _first_argmax_mask(cur, axis, size):
    io = lax.broadcasted_iota(jnp.int32, cur.shape, axis)
    mx = jnp.max(cur, axis=axis, keepdims=True)
    ix = jnp.min(jnp.where(cur == mx, io, size), axis=axis, keepdims=True)
    return io == ix


def _route(scores_t, bias_col):
    tm = scores_t.shape[1]
    per = N_EXPERTS // N_GROUPS
    sel = scores_t + bias_col
    s3 = sel.reshape(N_GROUPS, per, tm)
    hit1 = _first_argmax_mask(s3, 1, per)
    m1 = jnp.max(s3, axis=1, keepdims=True)
    m2 = jnp.max(jnp.where(hit1, -jnp.inf, s3), axis=1, keepdims=True)
    cur = m1 + m2
    gsel = None
    for _ in range(TOPK_GROUPS):
        hit = _first_argmax_mask(cur, 0, N_GROUPS)
        gsel = hit if gsel is None else jnp.logical_or(gsel, hit)
        cur = jnp.where(hit, -jnp.inf, cur)
    cur = jnp.where(gsel, s3, -jnp.inf).reshape(N_EXPERTS, tm)
    chosen, hits = None, []
    for _ in range(TOP_K):
        hit = _first_argmax_mask(cur, 0, N_EXPERTS)
        hits.append(hit)
        chosen = hit if chosen is None else jnp.logical_or(chosen, hit)
        cur = jnp.where(hit, -jnp.inf, cur)
    w = jnp.where(chosen, scores_t, 0.0)
    return w / jnp.sum(w, axis=0, keepdims=True) * ROUTED_SCALE, chosen, hits


def _out_kernel(oa_ref, od_ref, hf_ref, hb_ref, mo_ref, of_ref, ob_ref, rg_ref,
                x_ref, mod_ref, mg_ref, npost_ref, npre_ref, wout_ref, wrt_ref, rb_ref,
                x1_ref, h2_ref, hpk_ref, eidx_ref, rank_ref, wk_ref, cnt_ref, count_s):
    @pl.when(pl.program_id(0) == 0)
    def _():
        count_s[...] = jnp.zeros(count_s.shape, F32)

    a = oa_ref[...]
    dd = od_ref[...]
    b = _rms_heads(hf_ref[...] + hb_ref[...]) * mg_ref[...] * _sigmoid(mo_ref[...])
    r = _rms_heads(of_ref[...] + ob_ref[...]) * _silu(rg_ref[...])
    mix = (jnp.dot(a.astype(BF16), wout_ref[0:256, :], preferred_element_type=F32)
           + jnp.dot(b.astype(BF16), wout_ref[256:512, :], preferred_element_type=F32)
           + jnp.dot(r.astype(BF16), wout_ref[512:768, :], preferred_element_type=F32)
           + jnp.dot(dd.astype(BF16), wout_ref[768:1024, :], preferred_element_type=F32))
    m = mod_ref[0]
    x1 = x_ref[...] + m[2:3] * (_rms(mix) * npost_ref[...])
    x1_ref[...] = x1
    h2 = (_rms(x1) * npre_ref[...] * (1.0 + m[4:5]) + m[3:4]).astype(BF16)
    h2_ref[...] = h2
    hpk_ref[...] = _pack_rows(h2)

    logits_t = lax.dot_general(wrt_ref[...], h2, (((1,), (1,)), ((), ())), preferred_element_type=F32)
    w_t, chosen, hits = _route(_sigmoid(logits_t), rb_ref[...])
    tm = w_t.shape[1]
    src = lax.broadcasted_iota(jnp.int32, (tm, tm), 0)
    dst = lax.broadcasted_iota(jnp.int32, (tm, tm), 1)
    before = jnp.where(src < dst, 1.0, 0.0).astype(BF16)
    picked = jnp.where(chosen, 1.0, 0.0)
    rank = jnp.dot(picked.astype(BF16), before, preferred_element_type=F32) + count_s[...]
    count_s[...] += jnp.sum(picked, axis=1, keepdims=True)
    cnt_ref[...] = jnp.broadcast_to(count_s[...], cnt_ref.shape)
    e_io = lax.broadcasted_iota(jnp.int32, w_t.shape, 0)
    pick = lambda hit, val: jnp.sum(jnp.where(hit, val, jnp.zeros_like(val)), axis=0, keepdims=True)
    eidx_ref[...] = jnp.concatenate([pick(h, e_io) for h in hits], axis=0)
    rank_ref[...] = jnp.concatenate([pick(h, rank) for h in hits], axis=0).astype(jnp.int32)
    wk = jnp.concatenate([pick(h, w_t) for h in hits] + [jnp.zeros((LANES - TOP_K, tm), F32)], axis=0)
    wk_ref[...] = wk.T


def _output_stage(oa, od, hf, hb, mo, of, ob, rg, x, mod, mg, npost, npre, wout, wrt, rb, st):
    t_all = x.shape[0]
    row = lambda i: (i, 0)
    const2 = lambda i: (0, 0)
    tok = lambda w: pl.BlockSpec((TM, w), row)
    full = lambda a: pl.BlockSpec(a.shape, const2)
    return pl.pallas_call(
        _out_kernel,
        grid=(t_all // TM,),
        in_specs=[tok(256)] * 8 + [tok(D_MODEL), _mod_spec(st),
                                   full(mg), full(npost), full(npre), full(wout), full(wrt), full(rb)],
        out_specs=[tok(D_MODEL), tok(D_MODEL), tok(D_MODEL // 2),
                   pl.BlockSpec((TOP_K, TM), lambda i: (0, i)), pl.BlockSpec((TOP_K, TM), lambda i: (0, i)),
                   tok(LANES), pl.BlockSpec((N_EXPERTS, LANES), const2)],
        out_shape=[jax.ShapeDtypeStruct((t_all, D_MODEL), F32),
                   jax.ShapeDtypeStruct((t_all, D_MODEL), BF16),
                   jax.ShapeDtypeStruct((t_all, D_MODEL // 2), jnp.uint32),
                   jax.ShapeDtypeStruct((TOP_K, t_all), jnp.int32),
                   jax.ShapeDtypeStruct((TOP_K, t_all), jnp.int32),
                   jax.ShapeDtypeStruct((t_all, LANES), F32),
                   jax.ShapeDtypeStruct((N_EXPERTS, LANES), F32)],
        scratch_shapes=[pltpu.VMEM((N_EXPERTS, 1), F32)],
        compiler_params=_cparams(("arbitrary",)),
        name="output_stage_" + st["name"],
    )(oa, od, hf, hb, mo, of, ob, rg, x, mod, mg, npost, npre, wout, wrt, rb)


SC_CORES, SC_SUBCORES = 2, 16
SC_WORKERS = SC_CORES * SC_SUBCORES
SLOT_BLOCK = 512
DISPATCH_ROWS = 64
COMBINE_ROWS = 64


def _sc_mesh():
    return plsc.VectorSubcoreMesh(core_axis_name="core", subcore_axis_name="subcore")


def _sc_worker():
    return lax.axis_index("subcore") * SC_CORES + lax.axis_index("core")


def _sc_dispatch(hpk, dest, n_slots):
    t_all, width = hpk.shape
    per_worker = t_all // SC_WORKERS
    assert t_all % (SC_WORKERS * DISPATCH_ROWS) == 0

    @functools.partial(
        pl.kernel, mesh=_sc_mesh(), out_type=jax.ShapeDtypeStruct((n_slots, width), hpk.dtype),
        scratch_types=[pltpu.VMEM((DISPATCH_ROWS, width), hpk.dtype)]
        + [pltpu.VMEM((DISPATCH_ROWS,), jnp.int32)] * TOP_K + [pltpu.SemaphoreType.DMA])
    def dispatch(x_hbm, d_hbm, o_hbm, rows_v, *rest):
        idx, sem = rest[:TOP_K], rest[TOP_K]

        @pl.loop(0, per_worker // DISPATCH_ROWS)
        def _(j):
            base = _sc_worker() * per_worker + j * DISPATCH_ROWS
            pltpu.sync_copy(x_hbm.at[pl.ds(base, DISPATCH_ROWS)], rows_v)
            for k in range(TOP_K):
                pltpu.sync_copy(d_hbm.at[pl.ds(k * t_all + base, DISPATCH_ROWS)], idx[k])
            copies = [pltpu.async_copy(rows_v, o_hbm.at[idx[k]], sem) for k in range(TOP_K)]
            for c in copies:
                c.wait()

    return dispatch(hpk, dest)


def _sc_combine(yb, dest, t_all):
    width = yb.shape[1]
    per_worker = t_all // SC_WORKERS
    assert t_all % (SC_WORKERS * COMBINE_ROWS) == 0

    @functools.partial(
        pl.kernel, mesh=_sc_mesh(), out_type=jax.ShapeDtypeStruct((TOP_K, t_all, width), yb.dtype),
        scratch_types=[pltpu.VMEM((COMBINE_ROWS, width), yb.dtype)] * 2
        + [pltpu.VMEM((COMBINE_ROWS,), jnp.int32)] * TOP_K + [pltpu.SemaphoreType.DMA] * 2)
    def combine(y_hbm, d_hbm, o_hbm, rows_a, rows_b, *rest):
        idx, sems = rest[:TOP_K], rest[TOP_K:]
        bufs = (rows_a, rows_b)

        @pl.loop(0, per_worker // COMBINE_ROWS)
        def _(j):
            base = _sc_worker() * per_worker + j * COMBINE_ROWS
            for k in range(TOP_K):
                pltpu.sync_copy(d_hbm.at[pl.ds(k * t_all + base, COMBINE_ROWS)], idx[k])
            pending = pltpu.async_copy(y_hbm.at[idx[0]], bufs[0], sems[0])
            for k in range(TOP_K):
                pending.wait()
                if k + 1 < TOP_K:
                    pending = pltpu.async_copy(y_hbm.at[idx[k + 1]], bufs[(k + 1) % 2], sems[(k + 1) % 2])
                pltpu.sync_copy(bufs[k % 2], o_hbm.at[k, pl.ds(base, COMBINE_ROWS)])

    return combine(yb, dest)


def _pack_rows(x):
    bits = pltpu.bitcast(x.astype(BF16).astype(F32), jnp.uint32)
    w = x.shape[-1] // 2
    return (bits[..., :w] >> 16) | (bits[..., w:] & jnp.uint32(0xFFFF0000))


def _unpack_rows(words):
    return pltpu.bitcast(words << 16, F32), pltpu.bitcast(words & jnp.uint32(0xFFFF0000), F32)


def _expert_kernel(be_ref, used_ref, xs_ref, wg_ref, wu_ref, wd_ref, y_ref, wg_s, wu_s, wd_s):
    b = pl.program_id(0)
    fresh = jnp.logical_or(b == 0, be_ref[b] != be_ref[jnp.maximum(b - 1, 0)])

    @pl.when(jnp.logical_and(fresh, b < used_ref[0]))
    def _():
        wg_s[...] = wg_ref[0, 0].astype(BF16)
        wu_s[...] = wu_ref[0, 0].astype(BF16)
        wd_s[...] = wd_ref[0, 0].astype(BF16)

    @pl.when(b < used_ref[0])
    def _():
        lo, hi = (v.astype(BF16) for v in _unpack_rows(xs_ref[...]))
        half = D_MODEL // 2
        gate = (jnp.dot(lo, wg_s[:half, :], preferred_element_type=F32)
                + jnp.dot(hi, wg_s[half:, :], preferred_element_type=F32))
        up = (jnp.dot(lo, wu_s[:half, :], preferred_element_type=F32)
              + jnp.dot(hi, wu_s[half:, :], preferred_element_type=F32))
        y = jnp.dot((_silu(gate) * up).astype(BF16), wd_s[...], preferred_element_type=F32)
        y_ref[...] = _pack_rows(y)


def _expert_blocks(layer, xs, block_expert, blocks_used, wg, wu, wdn):
    n_slots = xs.shape[0]
    wspec = lambda shape: pl.BlockSpec((1, 1) + shape, lambda b, be, used: (layer, be[b], 0, 0))
    rows = lambda b, be, used: (jnp.minimum(b, used[0] - 1), 0)
    grid_spec = pltpu.PrefetchScalarGridSpec(
        num_scalar_prefetch=2,
        grid=(n_slots // SLOT_BLOCK,),
        in_specs=[pl.BlockSpec((SLOT_BLOCK, D_MODEL // 2), rows),
                  wspec((D_MODEL, D_EXPERT)), wspec((D_MODEL, D_EXPERT)), wspec((D_EXPERT, D_MODEL))],
        out_specs=pl.BlockSpec((SLOT_BLOCK, D_MODEL // 2), rows),
        scratch_shapes=[pltpu.VMEM((D_MODEL, D_EXPERT), BF16), pltpu.VMEM((D_MODEL, D_EXPERT), BF16),
                        pltpu.VMEM((D_EXPERT, D_MODEL), BF16)])
    return pl.pallas_call(
        _expert_kernel, grid_spec=grid_spec,
        out_shape=jax.ShapeDtypeStruct((n_slots, D_MODEL // 2), jnp.uint32),
        compiler_params=_cparams(("arbitrary",)),
        name="moe_experts",
    )(block_expert, blocks_used, xs, wg, wu, wdn)


def _moe_out_kernel(g_ref, wk_ref, h_ref, x1_ref, mod_ref, npost_ref, sg_ref, su_ref, sd_ref, o_ref):
    h = h_ref[...]
    act = _silu(jnp.dot(h, sg_ref[...], preferred_element_type=F32)) * jnp.dot(h, su_ref[...],
                                                                              preferred_element_type=F32)
    moe = jnp.dot(act.astype(BF16), sd_ref[...], preferred_element_type=F32)
    wk = wk_ref[...]
    lo, hi = 0.0, 0.0
    for k in range(TOP_K):
        g_lo, g_hi = _unpack_rows(g_ref[k])
        lo = lo + g_lo * wk[:, k:k + 1]
        hi = hi + g_hi * wk[:, k:k + 1]
    moe = moe + jnp.concatenate([lo, hi], axis=-1)
    m = mod_ref[0]
    o_ref[...] = x1_ref[...] + m[5:6] * (_rms(moe) * npost_ref[...])


def _moe_out_stage(g, wk, h2, x1, mod, npost, sg, su, sd, st):
    t_all = h2.shape[0]
    row = lambda i: (i, 0)
    full = lambda a: pl.BlockSpec(a.shape, lambda i: (0, 0))
    return pl.pallas_call(
        _moe_out_kernel,
        grid=(t_all // TM,),
        in_specs=[pl.BlockSpec((TOP_K, TM, D_MODEL // 2), lambda i: (0, i, 0)), pl.BlockSpec((TM, LANES), row),
                  pl.BlockSpec((TM, D_MODEL), row), pl.BlockSpec((TM, D_MODEL), row), _mod_spec(st),
                  full(npost), full(sg), full(su), full(sd)],
        out_specs=pl.BlockSpec((TM, D_MODEL), row),
        out_shape=jax.ShapeDtypeStruct((t_all, D_MODEL), F32),
        compiler_params=_cparams(("parallel",)),
        name="moe_combine_" + st["name"],
    )(g, wk, h2, x1, mod, npost, sg, su, sd)


def _moe_stage(layer, h2, hpk, eidx, rank, wk, counts, x1, mod, npost, wg, wu, wdn, sg, su, sd, st):
    t_all = h2.shape[0]
    n_blocks = -(-(t_all * TOP_K + N_EXPERTS * (SLOT_BLOCK - 1)) // SLOT_BLOCK)
    cnt = counts[:, 0].astype(jnp.int32)
    padded = (cnt + SLOT_BLOCK - 1) // SLOT_BLOCK * SLOT_BLOCK
    pad_end = jnp.cumsum(padded)
    pad_start = pad_end - padded
    experts = jnp.arange(N_EXPERTS, dtype=jnp.int32)[:, None, None]
    dest = rank + jnp.sum(jnp.where(eidx[None] == experts, pad_start[:, None, None], 0), axis=0)
    dest = dest.reshape(TOP_K * t_all)
    block_start = jnp.arange(n_blocks, dtype=jnp.int32) * SLOT_BLOCK
    block_expert = jnp.minimum(jnp.sum((pad_end[None, :] <= block_start[:, None]).astype(jnp.int32), axis=1),
                               N_EXPERTS - 1)
    blocks_used = (pad_end[-1:] // SLOT_BLOCK).astype(jnp.int32)

    xs = _sc_dispatch(hpk, dest, n_blocks * SLOT_BLOCK)
    yb = _expert_blocks(layer, xs, block_expert, blocks_used, wg, wu, wdn)
    g = _sc_combine(yb, dest, t_all)
    return _moe_out_stage(g, wk, h2, x1, mod, npost, sg, su, sd, st)


_PERM32 = np.concatenate([np.arange(8, 16), np.arange(0, 8), np.arange(24, 32), np.arange(16, 24)])


def _rope_tables(n):
    pos = jnp.arange(n)
    quarter = QK_ROPE // 4
    inv = 1.0 / (ROPE_THETA ** (jnp.arange(quarter, dtype=F32) / quarter))
    ang_r = (pos // GRID_W).astype(F32)[:, None] * inv[None, :]
    ang_c = (pos % GRID_W).astype(F32)[:, None] * inv[None, :]
    cos32 = jnp.concatenate([jnp.cos(ang_r)] * 2 + [jnp.cos(ang_c)] * 2, axis=-1)
    sin32 = jnp.concatenate([-jnp.sin(ang_r), jnp.sin(ang_r), -jnp.sin(ang_c), jnp.sin(ang_c)], axis=-1)
    cos256, sin256 = jnp.tile(cos32, (1, 8)), jnp.tile(sin32, (1, 8))
    one, zero = jnp.ones((n, QK_NOPE), F32), jnp.zeros((n, QK_NOPE), F32)
    cosq = jnp.concatenate([one, cos32] * N_HEADS, axis=-1)
    sinq = jnp.concatenate([zero, sin32] * N_HEADS, axis=-1)
    return cos256, sin256, cosq, sinq


def _split_w_in(w_in_l):
    sizes = (256, 256, 256, 256, 256, 256, 256, 16, 256, 256, 256, 256, Q_LORA, KV_LORA, QK_ROPE)
    idx = np.cumsum(sizes)[:-1]
    return jnp.split(w_in_l, [int(v) for v in idx], axis=-1)


def _layer_weights(w_in_l, w_uq_l):
    (a_q, a_k, a_v, m_q, m_k, m_v, m_o, m_g, r_q, r_k, r_v, r_g, d_cq, d_ckv, d_kr) = _split_w_in(w_in_l)
    perm256 = np.concatenate([_PERM32 + 32 * j for j in range(8)])
    pad = lambda w, n: jnp.pad(w, ((0, 0), (0, n - w.shape[1])))
    tail = pad(jnp.concatenate([d_kr, m_g], axis=-1), LANES)
    tail_s = pad(d_kr[:, _PERM32], LANES)
    w_all = jnp.concatenate([a_q, a_k, a_v, m_q, m_k, m_v, m_o, r_q, r_k, r_v, r_g, d_cq, d_ckv, tail,
                             a_q[:, perm256], a_k[:, perm256], tail_s], axis=-1).astype(BF16)
    per = QK_NOPE + QK_ROPE
    permq = np.concatenate([np.concatenate([np.arange(QK_NOPE), QK_NOPE + _PERM32]) + per * j
                            for j in range(N_HEADS)])
    return w_all, w_uq_l.astype(BF16), w_uq_l[:, permq].astype(BF16)


def kernel(x_prompt, x_sample, cache_diff_k, cache_diff_v, state_mlstm_C, state_mlstm_n, state_mlstm_m, state_ret_S, cache_mla_ckv, cache_mla_krope, c, c_ctx, w_mod, b_mod, norm_pre, norm_post, w_in, w_out, diff_lambda, diff_norm, mlstm_gate_bias, mlstm_norm, ret_decay_logit, mla_q_norm, mla_w_uq, mla_kv_norm, mla_w_ukv, moe_w_router, moe_router_bias, moe_w_gate, moe_w_up, moe_w_down, shared_w_gate, shared_w_up, shared_w_down):
    bp, n_p, _ = x_prompt.shape
    bs, n_s, _ = x_sample.shape
    depth = w_in.shape[0]
    past = cache_diff_k.shape[2]
    assert n_p % TM == 0 and n_s % TM == 0 and n_p % CHUNK == 0 and n_s % min(TK, n_s) == 0
    assert past % min(TK, past) == 0 and bs + 1 <= 8 and n_s % GRID_W == 0

    streams = (dict(name="ctx", nb=bp, n=n_p, latent=False), dict(name="latent", nb=bs, n=n_s, latent=True))
    xs_by_stream = [x_prompt.reshape(bp * n_p, D_MODEL), x_sample.reshape(bs * n_s, D_MODEL)]
    cond = jnp.zeros((8, D_MODEL), F32).at[0].set(c_ctx).at[1:1 + bs].set(c)
    mod_all = _modulation(cond, w_mod, b_mod).reshape(depth, 8, 6, D_MODEL)
    tabs = _rope_tables(n_s)
    scan_tabs = [_scan_table(st) for st in streams]

    outs = [[] for _ in range(8)]
    for l in range(depth):
        lam_init = 0.8 - 0.6 * math.exp(-0.3 * l)
        mod = mod_all[l]
        w_all, wuq, wuqs = _layer_weights(w_in[l], mla_w_uq[l])
        wukv = mla_w_ukv[l].astype(BF16)
        wout, wrt, rbias = w_out[l].astype(BF16), moe_w_router[l].T.astype(BF16), moe_router_bias[l][:, None]
        shared = [w[l].astype(BF16) for w in (shared_w_gate, shared_w_up, shared_w_down)]
        lamv, dg = diff_lambda[l], diff_norm[l][:, None]
        bias_row = jnp.zeros((1, LANES), F32).at[0, TAIL_GATE0:TAIL_GATE0 + 16].set(mlstm_gate_bias[l].reshape(16))

        with_ones = lambda vt: jnp.concatenate([vt, jnp.ones((N_HEADS, VT_ONES, bs * past), BF16)], axis=1)
        ck = jnp.transpose(cache_diff_k[:, l], (2, 0, 1, 3)).reshape(N_HEADS, bs * past, HEAD_DIM).astype(BF16)
        cvt = with_ones(jnp.transpose(cache_diff_v[:, l], (2, 3, 0, 1)).reshape(N_HEADS, HEAD_DIM, bs * past)
                        .astype(BF16))
        kvc = _project(cache_mla_ckv[:, l].reshape(bs * past, KV_LORA), wukv, BF16)
        kvc = kvc.reshape(bs * past, N_HEADS, QK_NOPE + V_HEAD)
        krc = jnp.broadcast_to(cache_mla_krope[:, l].reshape(bs * past, 1, QK_ROPE).astype(BF16),
                               (bs * past, N_HEADS, QK_ROPE))
        kc = jnp.transpose(jnp.concatenate([kvc[..., :QK_NOPE], krc], axis=-1), (1, 0, 2))
        vct = with_ones(jnp.transpose(kvc[..., QK_NOPE:], (1, 2, 0)))

        for si, st in enumerate(streams):
            x, nb, n = xs_by_stream[si], st["nb"], st["n"]
            (aq1t, aq2t, ak1, ak2, avt, ak, av, mqkv, mo, rqkv, rg, tail, qmlat, ckv, kmla, vmlat) = _input_stage(
                x, mod, norm_pre[l, 0:1], w_all, tabs, mla_q_norm[l][None], wuq, wuqs,
                mla_kv_norm[l][None], wukv, st)
            if st["latent"]:
                oa = _attention_t("diff_attn_latent", [aq1t, aq2t], [ak1, ak2], avt,
                                  [ck[..., :DIFF_HALF], ck[..., DIFF_HALF:]], cvt, [lamv, dg], nb, n, past, lam_init)
                od = _attention_t("mla_attn_latent", [qmlat], [kmla], vmlat, [kc], vct, [], nb, n, past)
                states = [s[:, l] for s in (state_mlstm_C, state_mlstm_n, state_mlstm_m, state_ret_S)]
            else:
                oa = _attention_t("diff_attn_ctx", [aq1t, aq2t], [ak1, ak2], avt, None, None, [lamv, dg],
                                  nb, n, 0, lam_init)
                od = _attention_t("mla_attn_ctx", [qmlat], [kmla], vmlat, None, None, [], nb, n, 0)
                states = [jnp.zeros((nb,) + s.shape[2:], F32)
                          for s in (state_mlstm_C, state_mlstm_n, state_mlstm_m, state_ret_S)]
            hf, hb, of, ob, c_n, n_n, m_n, s_n = _scans(scan_tabs[si], mqkv, tail, bias_row, rqkv,
                                                        ret_decay_logit[l], *states)

            x1, h2, hpk, eidx, rank, wk, counts = _output_stage(
                oa, od, hf, hb, mo, of, ob, rg, x, mod, mlstm_norm[l][None], norm_post[l, 0:1],
                norm_pre[l, 1:2], wout, wrt, rbias, st)
            xs_by_stream[si] = _moe_stage(l, h2, hpk, eidx, rank, wk, counts, x1, mod, norm_post[l, 1:2],
                                          moe_w_gate, moe_w_up, moe_w_down, *shared, st)
            if not st["latent"]:
                new = (ak.reshape(nb, n, N_HEADS, HEAD_DIM), av.reshape(nb, n, N_HEADS, HEAD_DIM), c_n, n_n, m_n,
                       s_n, ckv.reshape(nb, n, KV_LORA), tail[:, :QK_ROPE].reshape(nb, n, QK_ROPE))
                for o, v in zip(outs, new):
                    o.append(v)

    return (xs_by_stream[0].reshape(bp, n_p, D_MODEL), xs_by_stream[1].reshape(bs, n_s, D_MODEL)) + tuple(
        jnp.stack(o, axis=1) for o in outs)
```

```python
import functools
import math

import numpy as np
import jax
import jax.numpy as jnp
from jax import lax
from jax.experimental import pallas as pl
from jax.experimental.pallas import tpu as pltpu
from jax.experimental.pallas import tpu_sc as plsc

F32 = jnp.float32
BF16 = jnp.bfloat16
HIGHEST = lax.Precision.HIGHEST

D_MODEL = 1024
GRID_W = 64
GROUP_WIDTH = 256
HEAD_DIM = 64
N_HEADS = 4
DIFF_HALF = 32
ROPE_THETA = 10000.0
Q_LORA = 256
KV_LORA = 128
QK_NOPE = 64
QK_ROPE = 32
V_HEAD = 64
N_EXPERTS = 64
TOP_K = 8
N_GROUPS = 8
TOPK_GROUPS = 4
D_EXPERT = 256
ROUTED_SCALE = 2.5
CHUNK = 128
EPS = 1e-6
Q_MLA = N_HEADS * (QK_NOPE + QK_ROPE)
KV_MLA = N_HEADS * (QK_NOPE + V_HEAD)

LANES = 128
VMEM_LIMIT = 56 * 1024 * 1024

TM = 512
TM_IN = TM
TQ = 256
TK = 4096

C_AQ, C_AK, C_AV = 0, 256, 512
C_MQKV, C_MO = 768, 1536
C_RQKV, C_RG = 1792, 2560
C_CQ, C_CKV, C_TAIL = 2816, 3072, 3200
C_AQS, C_AKS, C_TAILS = 3328, 3584, 3840
W_ALL = 3968
TAIL_GATE0 = QK_ROPE
VT_ONES = 16
VT_ROWS = V_HEAD + VT_ONES


def _cparams(sem, flags=None):
    return pltpu.CompilerParams(dimension_semantics=sem, vmem_limit_bytes=VMEM_LIMIT, flags=flags)


def _rms(x):
    return x * lax.rsqrt(jnp.mean(x * x, axis=-1, keepdims=True) + EPS)


def _head_mean_matrix(width):
    r = lax.broadcasted_iota(jnp.int32, (width, width), 0) // HEAD_DIM
    c = lax.broadcasted_iota(jnp.int32, (width, width), 1) // HEAD_DIM
    return jnp.where(r == c, 1.0 / HEAD_DIM, 0.0).astype(BF16)


def _rms_heads(x):
    sq = x * x
    hi = sq.astype(BF16)
    lo = (sq - hi.astype(F32)).astype(BF16)
    g = _head_mean_matrix(x.shape[-1])
    ms = jnp.dot(hi, g, preferred_element_type=F32) + jnp.dot(lo, g, preferred_element_type=F32)
    return x * lax.rsqrt(ms + EPS)


def _sigmoid(x):
    return 1.0 / (1.0 + jnp.exp(-x))


def _silu(x):
    return x * _sigmoid(x)


def _log_sigmoid(x):
    return jnp.minimum(x, 0.0) - jnp.log1p(jnp.exp(-jnp.abs(x)))


def _bdot(a, b):
    return jnp.dot(a.astype(BF16), b.astype(BF16), preferred_element_type=F32)


def _mod_kernel(c_ref, w_ref, b_ref, o_ref):
    o_ref[0] = _bdot(_silu(c_ref[...]), w_ref[0]) + b_ref[0]


def _modulation(cond, w_mod, b_mod):
    depth, _, n = w_mod.shape
    tn = 1536
    return pl.pallas_call(
        _mod_kernel,
        grid=(depth, n // tn),
        in_specs=[pl.BlockSpec((8, D_MODEL), lambda l, j: (0, 0)),
                  pl.BlockSpec((1, D_MODEL, tn), lambda l, j: (l, 0, j)),
                  pl.BlockSpec((1, 1, tn), lambda l, j: (l, 0, j))],
        out_specs=pl.BlockSpec((1, 8, tn), lambda l, j: (l, 0, j)),
        out_shape=jax.ShapeDtypeStruct((depth, 8, n), F32),
        compiler_params=_cparams(("parallel", "parallel")),
        name="adaln_mod",
    )(cond, w_mod, b_mod.reshape(depth, 1, n))


def _in_kernel(x_ref, mod_ref, npre_ref, w_ref, cos_ref, sin_ref, cosq_ref, sinq_ref,
               qg_ref, wuq_ref, wuqs_ref, kvg_ref, wukv_ref,
               aq1_ref, aq2_ref, ak1t_ref, ak2t_ref, avh_ref, ak_ref, av_ref,
               mqkv_ref, mo_ref, rqkv_ref, rg_ref, tail_ref,
               qmla_ref, ckv_ref, kmlat_ref, vmla_ref, *, latent):
    m = mod_ref[0]
    h = (_rms(x_ref[...]) * npre_ref[...] * (1.0 + m[1:2]) + m[0:1]).astype(BF16)

    def proj(c0, width):
        return jnp.dot(h, w_ref[:, c0:c0 + width], preferred_element_type=F32)

    def rotated(c0, c0_swapped, width, cos, sin):
        return proj(c0, width) * cos + proj(c0_swapped, width) * sin if latent else proj(c0, width)

    cos = cos_ref[...]
    sin = sin_ref[...]
    aq = rotated(C_AQ, C_AQS, 256, cos, sin) * (DIFF_HALF ** -0.5)
    ak = rotated(C_AK, C_AKS, 256, cos, sin)
    av = proj(C_AV, 256)
    ak_ref[...] = ak
    av_ref[...] = av
    aq_t, av_t = aq.T.astype(BF16), av.T.astype(BF16)
    ones = jnp.ones((VT_ONES, aq_t.shape[1]), BF16)
    for hd in range(N_HEADS):
        lo = hd * HEAD_DIM
        aq1_ref[hd] = aq_t[lo:lo + DIFF_HALF, :]
        aq2_ref[hd] = aq_t[lo + DIFF_HALF:lo + HEAD_DIM, :]
        ak1t_ref[hd] = ak[:, lo:lo + DIFF_HALF].astype(BF16)
        ak2t_ref[hd] = ak[:, lo + DIFF_HALF:lo + HEAD_DIM].astype(BF16)
        avh_ref[hd, 0:HEAD_DIM, :] = av_t[lo:lo + HEAD_DIM, :]
        avh_ref[hd, HEAD_DIM:VT_ROWS, :] = ones

    mqkv_ref[...] = proj(C_MQKV, 768).astype(BF16)
    mo_ref[...] = proj(C_MO, 256)
    rqkv_ref[...] = proj(C_RQKV, 768).astype(BF16)
    rg_ref[...] = proj(C_RG, 256)

    tail = proj(C_TAIL, LANES)
    tail_ref[...] = tail
    kr = tail * cos[:, :LANES] + proj(C_TAILS, LANES) * sin[:, :LANES] if latent else tail
    kr = kr[:, :QK_ROPE].astype(BF16)

    cqn = (_rms(proj(C_CQ, Q_LORA)) * qg_ref[...]).astype(BF16)
    qmla = jnp.dot(cqn, wuq_ref[...], preferred_element_type=F32)
    if latent:
        qmla = qmla * cosq_ref[...] + jnp.dot(cqn, wuqs_ref[...], preferred_element_type=F32) * sinq_ref[...]
    qmla_t = (qmla * ((QK_NOPE + QK_ROPE) ** -0.5)).T.astype(BF16)
    for hd in range(N_HEADS):
        lo = hd * (QK_NOPE + QK_ROPE)
        qmla_ref[hd] = qmla_t[lo:lo + QK_NOPE + QK_ROPE, :]
    ckvn = _rms(proj(C_CKV, KV_LORA)) * kvg_ref[...]
    ckv_ref[...] = ckvn
    kv = jnp.dot(ckvn.astype(BF16), wukv_ref[...], preferred_element_type=F32)
    kv_t = kv.T.astype(BF16)
    per = QK_NOPE + V_HEAD
    for hd in range(N_HEADS):
        kmlat_ref[hd, :, 0:QK_NOPE] = kv[:, hd * per:hd * per + QK_NOPE].astype(BF16)
        kmlat_ref[hd, :, QK_NOPE:QK_NOPE + QK_ROPE] = kr
        vmla_ref[hd, 0:V_HEAD, :] = kv_t[hd * per + QK_NOPE:(hd + 1) * per, :]
        vmla_ref[hd, V_HEAD:VT_ROWS, :] = ones


def _mod_spec(st, tm=TM):
    tps = st["n"] // tm
    return pl.BlockSpec((1, 6, D_MODEL), lambda i, *_: (1 + i // tps if st["latent"] else 0, 0, 0))


def _input_stage(x, mod, npre, w_all, tabs, qg, wuq, wuqs, kvg, wukv, st):
    t_all = x.shape[0]
    tm = TM_IN
    tps = st["n"] // tm
    row = lambda i: (i, 0)
    tab = lambda w: pl.BlockSpec((tm, w), lambda i: (i % tps if st["latent"] else 0, 0))
    hrow = lambda i: (0, i, 0)
    const2 = lambda i: (0, 0)
    tok = lambda w: pl.BlockSpec((tm, w), row)
    headed = lambda w: pl.BlockSpec((N_HEADS, tm, w), hrow)
    headed_t = lambda d: pl.BlockSpec((N_HEADS, d, tm), lambda i: (0, 0, i))
    full = lambda a: pl.BlockSpec(a.shape, const2)
    cos, sin, cosq, sinq = tabs
    out_shapes = [
        (headed_t(DIFF_HALF), (N_HEADS, DIFF_HALF, t_all), BF16),
        (headed_t(DIFF_HALF), (N_HEADS, DIFF_HALF, t_all), BF16),
        (headed(DIFF_HALF), (N_HEADS, t_all, DIFF_HALF), BF16),
        (headed(DIFF_HALF), (N_HEADS, t_all, DIFF_HALF), BF16),
        (headed_t(VT_ROWS), (N_HEADS, VT_ROWS, t_all), BF16),
        (tok(256), (t_all, 256), F32),
        (tok(256), (t_all, 256), F32),
        (tok(768), (t_all, 768), BF16),
        (tok(256), (t_all, 256), F32),
        (tok(768), (t_all, 768), BF16),
        (tok(256), (t_all, 256), F32),
        (tok(LANES), (t_all, LANES), F32),
        (headed_t(QK_NOPE + QK_ROPE), (N_HEADS, QK_NOPE + QK_ROPE, t_all), BF16),
        (tok(KV_LORA), (t_all, KV_LORA), F32),
        (headed(QK_NOPE + QK_ROPE), (N_HEADS, t_all, QK_NOPE + QK_ROPE), BF16),
        (headed_t(VT_ROWS), (N_HEADS, VT_ROWS, t_all), BF16),
    ]
    return pl.pallas_call(
        functools.partial(_in_kernel, latent=st["latent"]),
        grid=(t_all // tm,),
        in_specs=[tok(D_MODEL), _mod_spec(st, tm),
                  full(npre), full(w_all), tab(256), tab(256), tab(Q_MLA), tab(Q_MLA),
                  full(qg), full(wuq), full(wuqs), full(kvg), full(wukv)],
        out_specs=[s for s, _, _ in out_shapes],
        out_shape=[jax.ShapeDtypeStruct(shp, dt) for _, shp, dt in out_shapes],
        compiler_params=_cparams(("parallel",)),
        name="input_stage_" + st["name"],
    )(x, mod, npre, w_all, cos, sin, cosq, sinq, qg, wuq, wuqs, kvg, wukv)


def _attn_t_kernel(*refs, n_soft, n_new, n_ctx, lam_init):
    refs = list(refs)
    qt_refs, k_refs, vt_ref = refs[:n_soft], refs[n_soft:2 * n_soft], refs[2 * n_soft]
    pos = 2 * n_soft + 1
    if n_ctx:
        ck_refs, cvt_ref = refs[pos:pos + n_soft], refs[pos + n_soft]
        pos += n_soft + 1
    if n_soft == 2:
        lam_ref, g_ref = refs[pos:pos + 2]
        pos += 2
    o_ref = refs[pos]
    tq = o_ref.shape[0]
    nchain = n_soft * N_HEADS
    order = [(j, hd) for hd in range(N_HEADS) for j in range(n_soft)]

    def chunk(state, k_of, vt_of):
        ms, accs = state
        new_m, new_acc = list(ms), list(accs)
        ss = [jnp.dot(k_of(j, hd), qt_refs[j][hd], preferred_element_type=F32) for j, hd in order]
        ps, alphas = [], []
        for (j, hd), s in zip(order, ss):
            c = j * N_HEADS + hd
            s3 = s.reshape(s.shape[0] // 8, 8, tq)
            top = jnp.max(jnp.max(s3, axis=0), axis=0, keepdims=True)
            m_new = jnp.maximum(ms[c], jnp.broadcast_to(top, (8, tq)))
            ps.append(jnp.exp(s3 - m_new[None]).reshape(s.shape).astype(BF16))
            alphas.append(jnp.exp(ms[c] - m_new))
            new_m[c] = m_new
        for (j, hd), p, alpha in zip(order, ps, alphas):
            c = j * N_HEADS + hd
            scaled = (accs[c].reshape(VT_ROWS // 8, 8, tq) * alpha[None]).reshape(VT_ROWS, tq)
            new_acc[c] = scaled + jnp.dot(vt_of(hd), p, preferred_element_type=F32)
        return tuple(new_m), tuple(new_acc)

    state = (tuple(jnp.full((8, tq), -jnp.inf, F32) for _ in range(nchain)),
             tuple(jnp.zeros((VT_ROWS, tq), F32) for _ in range(nchain)))
    if n_ctx:
        cstep = min(TK, n_ctx)
        for i in range(n_ctx // cstep):
            state = chunk(state, lambda j, hd, i=i: ck_refs[j][hd, i * cstep:(i + 1) * cstep, :],
                          lambda hd, i=i: cvt_ref[hd, :, i * cstep:(i + 1) * cstep])
    step = min(TK, n_new)

    def body(i, st):
        start = pl.multiple_of(i * step, step)
        return chunk(st, lambda j, hd: k_refs[j][hd, pl.ds(start, step), :],
                     lambda hd: vt_ref[hd, :, pl.ds(start, step)])

    _, accs = lax.fori_loop(0, n_new // step, body, state)

    def normalised(c):
        num, den = accs[c][:V_HEAD], accs[c][V_HEAD:V_HEAD + 8]
        return (num.reshape(V_HEAD // 8, 8, tq) / den[None]).reshape(V_HEAD, tq)

    if n_soft == 2:
        lv = lam_ref[...]
        lam = (jnp.exp(jnp.sum(lv[0:1] * lv[1:2], axis=-1, keepdims=True))
               - jnp.exp(jnp.sum(lv[2:3] * lv[3:4], axis=-1, keepdims=True)) + lam_init)
    outs = []
    for hd in range(N_HEADS):
        out = normalised(hd)
        if n_soft == 2:
            a = out - lam * normalised(N_HEADS + hd)
            out = a * lax.rsqrt(jnp.mean(a * a, axis=0, keepdims=True) + EPS) * g_ref[...] * (1.0 - lam_init)
        outs.append(out)
    o_ref[...] = jnp.concatenate(outs, axis=0).T


def _attention_t(name, qts, ks, vt, ctx_ks, ctx_vt, extras, nb, n, n_ctx, lam_init=0.0):
    tq = min(TQ, n)
    nqt = n // tq
    n_soft = len(qts)
    in_specs = [pl.BlockSpec((N_HEADS, a.shape[1], tq), lambda b, i: (0, 0, b * nqt + i)) for a in qts]
    in_specs += [pl.BlockSpec((N_HEADS, n, a.shape[-1]), lambda b, i: (0, b, 0)) for a in ks]
    in_specs += [pl.BlockSpec((N_HEADS, VT_ROWS, n), lambda b, i: (0, 0, b))]
    args = list(qts) + list(ks) + [vt]
    if n_ctx:
        in_specs += [pl.BlockSpec((N_HEADS, n_ctx, a.shape[-1]), lambda b, i: (0, b, 0)) for a in ctx_ks]
        in_specs += [pl.BlockSpec((N_HEADS, VT_ROWS, n_ctx), lambda b, i: (0, 0, b))]
        args += list(ctx_ks) + [ctx_vt]
    in_specs += [pl.BlockSpec(a.shape, lambda b, i: (0, 0)) for a in extras]
    args += list(extras)
    return pl.pallas_call(
        functools.partial(_attn_t_kernel, n_soft=n_soft, n_new=n, n_ctx=n_ctx, lam_init=lam_init),
        grid=(nb, nqt),
        in_specs=in_specs,
        out_specs=pl.BlockSpec((tq, N_HEADS * V_HEAD), lambda b, i: (b * nqt + i, 0)),
        out_shape=jax.ShapeDtypeStruct((nb * n, N_HEADS * V_HEAD), F32),
        compiler_params=_cparams(("parallel", "parallel")),
        name=name,
    )(*args)


def _tri(lower):
    r = lax.broadcasted_iota(jnp.int32, (CHUNK, CHUNK), 0)
    c = lax.broadcasted_iota(jnp.int32, (CHUNK, CHUNK), 1)
    return (c <= r) if lower else (c >= r)


def _nt(a, b):
    return lax.dot_general(a, b, (((1,), (1,)), ((), ())), preferred_element_type=F32)


def _tn(a, b):
    return lax.dot_general(a, b, (((0,), (0,)), ((), ())), preferred_element_type=F32)


def _head_slices(qkv, hd):
    lo = hd * HEAD_DIM
    return qkv[:, lo:lo + HEAD_DIM], qkv[:, 256 + lo:256 + lo + HEAD_DIM], qkv[:, 512 + lo:512 + lo + HEAD_DIM]


def _row_dot(row, mat, transpose_mat):
    row8 = jnp.broadcast_to(row, (8, row.shape[1])).astype(BF16)
    dims = (((1,), (1,)), ((), ())) if transpose_mat else (((1,), (0,)), ((), ()))
    return lax.dot_general(row8, mat, dims, preferred_element_type=F32)[0:1]


def _load_state_t(dst, src):
    for d in range(2):
        for hd in range(N_HEADS):
            dst[d, hd] = src[d, hd].T


def _mlstm_step(dirs, bias_ref, ct_s, n_s, m_s):
    k_scale = HEAD_DIM ** -0.5
    lower = _tri(True).astype(F32)
    upper = _tri(False).astype(F32)
    chains = []
    for d, qkv_ref, tail_ref, _ in dirs:
        g = tail_ref[...] + bias_ref[...]
        ls = _log_sigmoid(g)
        g_t, ls_t = g.T, ls.T
        left, right = (lower, upper) if d == 0 else (upper, lower)
        cum_col = jnp.dot(left, ls, precision=HIGHEST, preferred_element_type=F32)
        cum_row = jnp.dot(ls_t, right, precision=HIGHEST, preferred_element_type=F32)
        qkv = qkv_ref[...]
        for hd in range(N_HEADS):
            ci = TAIL_GATE0 + 4 * (2 * d) + hd
            cf = ci + 4
            q, k, v = _head_slices(qkv, hd)
            chains.append(dict(
                d=d, hd=hd, q=q, k=k, v=v, li_row=g_t[ci:ci + 1, :], b_row=cum_row[cf:cf + 1, :],
                c_col=g[:, ci:ci + 1] - cum_col[:, cf:cf + 1],
                m_prev=m_s[d:d + 1, hd:hd + 1], ct=ct_s[d, hd], n=n_s[d, hd:hd + 1, :]))
    for c in chains:
        c["s"] = _nt(c["k"], c["q"])
        c["v_t"] = c["v"].T
    for c in chains:
        valid = _tri(c["d"] != 0)
        log_d = jnp.where(valid, c["b_row"] + c["c_col"], -jnp.inf)
        log_inter = c["b_row"] + c["m_prev"]
        c["m_t"] = jnp.maximum(log_inter, jnp.max(log_d, axis=0, keepdims=True))
        c["w"] = c["s"] * k_scale * jnp.exp(log_d - c["m_t"])
        c["w_inter"] = jnp.exp(log_inter - c["m_t"])
    for c in chains:
        num = (jnp.dot(c["v_t"], c["w"].astype(BF16), preferred_element_type=F32)
               + c["w_inter"] * _nt(c["ct"].astype(BF16), c["q"]))
        den = jnp.sum(c["w"], axis=0, keepdims=True) + c["w_inter"] * _row_dot(c["n"], c["q"], True)
        c["h"] = num / jnp.maximum(jnp.abs(den), jnp.exp(-c["m_t"]))
    for c in chains:
        last = CHUNK - 1 if c["d"] == 0 else 0
        c["m_new"] = c["m_t"][:, last:last + 1]
        b_last = c["b_row"][:, last:last + 1]
        w_end_row = jnp.exp(b_last - c["b_row"] + c["li_row"] - c["m_new"])
        w_end_col = jnp.exp(c["c_col"] + (b_last - c["m_new"]))
        decay = jnp.exp(b_last + c["m_prev"] - c["m_new"])
        kw = (c["k"].astype(F32) * (w_end_col * k_scale)).astype(BF16)
        c["ct_new"] = decay * c["ct"] + jnp.dot(c["v_t"], kw, preferred_element_type=F32)
        c["n_new"] = decay * c["n"] + _row_dot(w_end_row, c["k"], False) * k_scale
    for j, (_, _, _, h_ref) in enumerate(dirs):
        h_ref[...] = jnp.concatenate([c["h"] for c in chains[j * N_HEADS:(j + 1) * N_HEADS]], axis=0).T
    for c in chains:
        d, hd = c["d"], c["hd"]
        ct_s[d, hd] = c["ct_new"]
        n_s[d, hd:hd + 1, :] = c["n_new"]
        m_s[d:d + 1, hd:hd + 1] = c["m_new"]


def _ret_step(dirs, lg_ref, st_s):
    k_scale = HEAD_DIM ** -0.5
    s = lax.broadcasted_iota(jnp.int32, (CHUNK, CHUNK), 0)
    t = lax.broadcasted_iota(jnp.int32, (CHUNK, CHUNK), 1)
    lane = lax.broadcasted_iota(jnp.int32, (1, CHUNK), 1)
    chains = []
    for d, qkv_ref, _ in dirs:
        lag = (t - s) if d == 0 else (s - t)
        pos = (lane if d == 0 else CHUNK - 1 - lane).astype(F32)
        qkv = qkv_ref[...]
        for hd in range(N_HEADS):
            q, k, v = _head_slices(qkv, hd)
            chains.append(dict(d=d, hd=hd, q=q, k=k, v=v, lag=lag, pos=pos, st=st_s[d, hd],
                               lg=_log_sigmoid(lg_ref[d:d + 1, hd:hd + 1])))
    for c in chains:
        c["s"] = _nt(c["k"], c["q"])
        c["v_t"] = c["v"].T
    for c in chains:
        intra = jnp.where(c["lag"] >= 0, jnp.exp(jnp.maximum(c["lag"], 0).astype(F32) * c["lg"]), 0.0)
        c["a"] = (c["s"] * k_scale * intra).astype(BF16)
    for c in chains:
        inter = jnp.exp((c["pos"] + 1.0) * c["lg"])
        c["o"] = jnp.dot(c["v_t"], c["a"], preferred_element_type=F32) + inter * _nt(c["st"].astype(BF16), c["q"])
    row = lax.broadcasted_iota(jnp.int32, (CHUNK, HEAD_DIM), 0)
    for c in chains:
        src_pos = (row if c["d"] == 0 else CHUNK - 1 - row).astype(F32)
        tail = jnp.exp((CHUNK - 1.0 - src_pos) * c["lg"]) * k_scale
        kw = (c["k"].astype(F32) * tail).astype(BF16)
        c["st_new"] = jnp.exp(CHUNK * c["lg"]) * c["st"] + jnp.dot(c["v_t"], kw, preferred_element_type=F32)
    for j, (_, _, o_ref) in enumerate(dirs):
        o_ref[...] = jnp.concatenate([c["o"] for c in chains[j * N_HEADS:(j + 1) * N_HEADS]], axis=0).T
    for c in chains:
        st_s[c["d"], c["hd"]] = c["st_new"]


def _scan_kernel(tab_ref, mqf_ref, tailf_ref, mqb_ref, tailb_ref, rqf_ref, rqb_ref, bias_ref, lg_ref,
                 c0_ref, n0_ref, m0_ref, s0_ref,
                 hf_ref, hb_ref, of_ref, ob_ref, cn_ref, nn_ref, mn_ref, sn_ref, ct_s, n_s, m_s, st_s):
    i = pl.program_id(0)

    @pl.when(tab_ref[3, i] == 1)
    def _():
        _load_state_t(ct_s, c0_ref.at[0])
        _load_state_t(st_s, s0_ref.at[0])
        n_s[...] = n0_ref[0]
        m_s[...] = m0_ref[0]

    _mlstm_step([(0, mqf_ref, tailf_ref, hf_ref), (1, mqb_ref, tailb_ref, hb_ref)], bias_ref, ct_s, n_s, m_s)
    _ret_step([(0, rqf_ref, of_ref), (1, rqb_ref, ob_ref)], lg_ref, st_s)

    @pl.when(tab_ref[4, i] == 1)
    def _():
        _load_state_t(cn_ref.at[0], ct_s)
        _load_state_t(sn_ref.at[0], st_s)
        nn_ref[0] = n_s[...]
        mn_ref[0] = m_s[...]


def _scan_table(st):
    nch = st["n"] // CHUNK
    rows = [(b * nch + c, b * nch + nch - 1 - c, b, int(c == 0), int(c == nch - 1))
            for b in range(st["nb"]) for c in range(nch)]
    return jnp.asarray(np.array(rows, dtype=np.int32).T)


def _scans(tab, mqkv, tail, bias_row, rqkv, decay_logit, c0, n0, m0, s0):
    t_all = mqkv.shape[0]
    fwd = lambda i, tab: (tab[0, i], 0)
    bwd = lambda i, tab: (tab[1, i], 0)
    const = lambda i, tab: (0, 0)
    state = lambda a: pl.BlockSpec((1,) + a.shape[1:], lambda i, tab: (tab[2, i],) + (0,) * (a.ndim - 1))
    states = (c0, n0, m0, s0)
    grid_spec = pltpu.PrefetchScalarGridSpec(
        num_scalar_prefetch=1,
        grid=(tab.shape[1],),
        in_specs=[pl.BlockSpec((CHUNK, 768), fwd), pl.BlockSpec((CHUNK, LANES), fwd),
                  pl.BlockSpec((CHUNK, 768), bwd), pl.BlockSpec((CHUNK, LANES), bwd),
                  pl.BlockSpec((CHUNK, 768), fwd), pl.BlockSpec((CHUNK, 768), bwd),
                  pl.BlockSpec((1, LANES), const), pl.BlockSpec(decay_logit.shape, const)]
        + [state(a) for a in states],
        out_specs=[pl.BlockSpec((CHUNK, 256), fwd), pl.BlockSpec((CHUNK, 256), bwd),
                   pl.BlockSpec((CHUNK, 256), fwd), pl.BlockSpec((CHUNK, 256), bwd)]
        + [state(a) for a in states],
        scratch_shapes=[pltpu.VMEM(a.shape[1:], F32) for a in states],
    )
    return pl.pallas_call(
        _scan_kernel,
        grid_spec=grid_spec,
        out_shape=[jax.ShapeDtypeStruct((t_all, 256), F32)] * 4 + [jax.ShapeDtypeStruct(a.shape, F32) for a in states],
        compiler_params=_cparams(("arbitrary",)),
        name="recurrent_scans",
    )(tab, mqkv, tail, mqkv, tail, rqkv, rqkv, bias_row, decay_logit, *states)


def _proj_kernel(x_ref, w_ref, o_ref):
    o_ref[...] = _bdot(x_ref[...], w_ref[...]).astype(o_ref.dtype)


def _project(x, w, dtype):
    return pl.pallas_call(
        _proj_kernel,
        out_shape=jax.ShapeDtypeStruct((x.shape[0], w.shape[1]), dtype),
        name="ctx_kv_proj",
    )(x, w)


def _first_argmax_mask(cur, axis, size):
    io = lax.broadcasted_iota(jnp.int32, cur.shape, axis)
    mx = jnp.max(cur, axis=axis, keepdims=True)
    ix = jnp.min(jnp.where(cur == mx, io, size), axis=axis, keepdims=True)
    return io == ix


def _route(scores_t, bias_col):
    tm = scores_t.shape[1]
    per = N_EXPERTS // N_GROUPS
    sel = scores_t + bias_col
    s3 = sel.reshape(N_GROUPS, per, tm)
    hit1 = _first_argmax_mask(s3, 1, per)
    m1 = jnp.max(s3, axis=1, keepdims=True)
    m2 = jnp.max(jnp.where(hit1, -jnp.inf, s3), axis=1, keepdims=True)
    cur = m1 + m2
    gsel = None
    for _ in range(TOPK_GROUPS):
        hit = _first_argmax_mask(cur, 0, N_GROUPS)
        gsel = hit if gsel is None else jnp.logical_or(gsel, hit)
        cur = jnp.where(hit, -jnp.inf, cur)
    cur = jnp.where(gsel, s3, -jnp.inf).reshape(N_EXPERTS, tm)
    chosen, hits = None, []
    for _ in range(TOP_K):
        hit = _first_argmax_mask(cur, 0, N_EXPERTS)
        hits.append(hit)
        chosen = hit if chosen is None else jnp.logical_or(chosen, hit)
        cur = jnp.where(hit, -jnp.inf, cur)
    w = jnp.where(chosen, scores_t, 0.0)
    return w / jnp.sum(w, axis=0, keepdims=True) * ROUTED_SCALE, chosen, hits


def _out_kernel(oa_ref, od_ref, hf_ref, hb_ref, mo_ref, of_ref, ob_ref, rg_ref,
                x_ref, mod_ref, mg_ref, npost_ref, npre_ref, wout_ref, wrt_ref, rb_ref,
                x1_ref, h2_ref, hpk_ref, eidx_ref, rank_ref, wk_ref, cnt_ref, count_s):
    @pl.when(pl.program_id(0) == 0)
    def _():
        count_s[...] = jnp.zeros(count_s.shape, F32)

    a = oa_ref[...]
    dd = od_ref[...]
    b = _rms_heads(hf_ref[...] + hb_ref[...]) * mg_ref[...] * _sigmoid(mo_ref[...])
    r = _rms_heads(of_ref[...] + ob_ref[...]) * _silu(rg_ref[...])
    mix = (jnp.dot(a.astype(BF16), wout_ref[0:256, :], preferred_element_type=F32)
           + jnp.dot(b.astype(BF16), wout_ref[256:512, :], preferred_element_type=F32)
           + jnp.dot(r.astype(BF16), wout_ref[512:768, :], preferred_element_type=F32)
           + jnp.dot(dd.astype(BF16), wout_ref[768:1024, :], preferred_element_type=F32))
    m = mod_ref[0]
    x1 = x_ref[...] + m[2:3] * (_rms(mix) * npost_ref[...])
    x1_ref[...] = x1
    h2 = (_rms(x1) * npre_ref[...] * (1.0 + m[4:5]) + m[3:4]).astype(BF16)
    h2_ref[...] = h2
    hpk_ref[...] = _pack_rows(h2)

    logits_t = lax.dot_general(wrt_ref[...], h2, (((1,), (1,)), ((), ())), preferred_element_type=F32)
    w_t, chosen, hits = _route(_sigmoid(logits_t), rb_ref[...])
    tm = w_t.shape[1]
    src = lax.broadcasted_iota(jnp.int32, (tm, tm), 0)
    dst = lax.broadcasted_iota(jnp.int32, (tm, tm), 1)
    before = jnp.where(src < dst, 1.0, 0.0).astype(BF16)
    picked = jnp.where(chosen, 1.0, 0.0)
    rank = jnp.dot(picked.astype(BF16), before, preferred_element_type=F32) + count_s[...]
    count_s[...] += jnp.sum(picked, axis=1, keepdims=True)
    cnt_ref[...] = jnp.broadcast_to(count_s[...], cnt_ref.shape)
    e_io = lax.broadcasted_iota(jnp.int32, w_t.shape, 0)
    pick = lambda hit, val: jnp.sum(jnp.where(hit, val, jnp.zeros_like(val)), axis=0, keepdims=True)
    eidx_ref[...] = jnp.concatenate([pick(h, e_io) for h in hits], axis=0)
    rank_ref[...] = jnp.concatenate([pick(h, rank) for h in hits], axis=0).astype(jnp.int32)
    wk = jnp.concatenate([pick(h, w_t) for h in hits] + [jnp.zeros((LANES - TOP_K, tm), F32)], axis=0)
    wk_ref[...] = wk.T


def _output_stage(oa, od, hf, hb, mo, of, ob, rg, x, mod, mg, npost, npre, wout, wrt, rb, st):
    t_all = x.shape[0]
    row = lambda i: (i, 0)
    const2 = lambda i: (0, 0)
    tok = lambda w: pl.BlockSpec((TM, w), row)
    full = lambda a: pl.BlockSpec(a.shape, const2)
    return pl.pallas_call(
        _out_kernel,
        grid=(t_all // TM,),
        in_specs=[tok(256)] * 8 + [tok(D_MODEL), _mod_spec(st),
                                   full(mg), full(npost), full(npre), full(wout), full(wrt), full(rb)],
        out_specs=[tok(D_MODEL), tok(D_MODEL), tok(D_MODEL // 2),
                   pl.BlockSpec((TOP_K, TM), lambda i: (0, i)), pl.BlockSpec((TOP_K, TM), lambda i: (0, i)),
                   tok(LANES), pl.BlockSpec((N_EXPERTS, LANES), const2)],
        out_shape=[jax.ShapeDtypeStruct((t_all, D_MODEL), F32),
                   jax.ShapeDtypeStruct((t_all, D_MODEL), BF16),
                   jax.ShapeDtypeStruct((t_all, D_MODEL // 2), jnp.uint32),
                   jax.ShapeDtypeStruct((TOP_K, t_all), jnp.int32),
                   jax.ShapeDtypeStruct((TOP_K, t_all), jnp.int32),
                   jax.ShapeDtypeStruct((t_all, LANES), F32),
                   jax.ShapeDtypeStruct((N_EXPERTS, LANES), F32)],
        scratch_shapes=[pltpu.VMEM((N_EXPERTS, 1), F32)],
        compiler_params=_cparams(("arbitrary",)),
        name="output_stage_" + st["name"],
    )(oa, od, hf, hb, mo, of, ob, rg, x, mod, mg, npost, npre, wout, wrt, rb)


SC_CORES, SC_SUBCORES = 2, 16
SC_WORKERS = SC_CORES * SC_SUBCORES
SLOT_BLOCK = 512
DISPATCH_ROWS = 64
COMBINE_ROWS = 64


def _sc_mesh():
    return plsc.VectorSubcoreMesh(core_axis_name="core", subcore_axis_name="subcore")


def _sc_worker():
    return lax.axis_index("subcore") * SC_CORES + lax.axis_index("core")


def _sc_dispatch(hpk, dest, n_slots):
    t_all, width = hpk.shape
    per_worker = t_all // SC_WORKERS
    assert t_all % (SC_WORKERS * DISPATCH_ROWS) == 0

    @functools.partial(
        pl.kernel, mesh=_sc_mesh(), out_type=jax.ShapeDtypeStruct((n_slots, width), hpk.dtype),
        scratch_types=[pltpu.VMEM((DISPATCH_ROWS, width), hpk.dtype)]
        + [pltpu.VMEM((DISPATCH_ROWS,), jnp.int32)] * TOP_K + [pltpu.SemaphoreType.DMA])
    def dispatch(x_hbm, d_hbm, o_hbm, rows_v, *rest):
        idx, sem = rest[:TOP_K], rest[TOP_K]

        @pl.loop(0, per_worker // DISPATCH_ROWS)
        def _(j):
            base = _sc_worker() * per_worker + j * DISPATCH_ROWS
            pltpu.sync_copy(x_hbm.at[pl.ds(base, DISPATCH_ROWS)], rows_v)
            for k in range(TOP_K):
                pltpu.sync_copy(d_hbm.at[pl.ds(k * t_all + base, DISPATCH_ROWS)], idx[k])
            copies = [pltpu.async_copy(rows_v, o_hbm.at[idx[k]], sem) for k in range(TOP_K)]
            for c in copies:
                c.wait()

    return dispatch(hpk, dest)


def _sc_combine(yb, dest, t_all):
    width = yb.shape[1]
    per_worker = t_all // SC_WORKERS
    assert t_all % (SC_WORKERS * COMBINE_ROWS) == 0

    @functools.partial(
        pl.kernel, mesh=_sc_mesh(), out_type=jax.ShapeDtypeStruct((TOP_K, t_all, width), yb.dtype),
        scratch_types=[pltpu.VMEM((COMBINE_ROWS, width), yb.dtype)] * 2
        + [pltpu.VMEM((COMBINE_ROWS,), jnp.int32)] * TOP_K + [pltpu.SemaphoreType.DMA] * 2)
    def combine(y_hbm, d_hbm, o_hbm, rows_a, rows_b, *rest):
        idx, sems = rest[:TOP_K], rest[TOP_K:]
        bufs = (rows_a, rows_b)

        @pl.loop(0, per_worker // COMBINE_ROWS)
        def _(j):
            base = _sc_worker() * per_worker + j * COMBINE_ROWS
            for k in range(TOP_K):
                pltpu.sync_copy(d_hbm.at[pl.ds(k * t_all + base, COMBINE_ROWS)], idx[k])
            pending = pltpu.async_copy(y_hbm.at[idx[0]], bufs[0], sems[0])
            for k in range(TOP_K):
                pending.wait()
                if k + 1 < TOP_K:
                    pending = pltpu.async_copy(y_hbm.at[idx[k + 1]], bufs[(k + 1) % 2], sems[(k + 1) % 2])
                pltpu.sync_copy(bufs[k % 2], o_hbm.at[k, pl.ds(base, COMBINE_ROWS)])

    return combine(yb, dest)


def _pack_rows(x):
    bits = pltpu.bitcast(x.astype(BF16).astype(F32), jnp.uint32)
    w = x.shape[-1] // 2
    return (bits[..., :w] >> 16) | (bits[..., w:] & jnp.uint32(0xFFFF0000))


def _unpack_rows(words):
    return pltpu.bitcast(words << 16, F32), pltpu.bitcast(words & jnp.uint32(0xFFFF0000), F32)


def _expert_kernel(be_ref, used_ref, xs_ref, wg_ref, wu_ref, wd_ref, y_ref, wg_s, wu_s, wd_s):
    b = pl.program_id(0)
    fresh = jnp.logical_or(b == 0, be_ref[b] != be_ref[jnp.maximum(b - 1, 0)])

    @pl.when(jnp.logical_and(fresh, b < used_ref[0]))
    def _():
        wg_s[...] = wg_ref[0, 0].astype(BF16)
        wu_s[...] = wu_ref[0, 0].astype(BF16)
        wd_s[...] = wd_ref[0, 0].astype(BF16)

    @pl.when(b < used_ref[0])
    def _():
        lo, hi = (v.astype(BF16) for v in _unpack_rows(xs_ref[...]))
        half = D_MODEL // 2
        gate = (jnp.dot(lo, wg_s[:half, :], preferred_element_type=F32)
                + jnp.dot(hi, wg_s[half:, :], preferred_element_type=F32))
        up = (jnp.dot(lo, wu_s[:half, :], preferred_element_type=F32)
              + jnp.dot(hi, wu_s[half:, :], preferred_element_type=F32))
        y = jnp.dot((_silu(gate) * up).astype(BF16), wd_s[...], preferred_element_type=F32)
        y_ref[...] = _pack_rows(y)


def _expert_blocks(layer, xs, block_expert, blocks_used, wg, wu, wdn):
    n_slots = xs.shape[0]
    wspec = lambda shape: pl.BlockSpec((1, 1) + shape, lambda b, be, used: (layer, be[b], 0, 0))
    rows = lambda b, be, used: (jnp.minimum(b, used[0] - 1), 0)
    grid_spec = pltpu.PrefetchScalarGridSpec(
        num_scalar_prefetch=2,
        grid=(n_slots // SLOT_BLOCK,),
        in_specs=[pl.BlockSpec((SLOT_BLOCK, D_MODEL // 2), rows),
                  wspec((D_MODEL, D_EXPERT)), wspec((D_MODEL, D_EXPERT)), wspec((D_EXPERT, D_MODEL))],
        out_specs=pl.BlockSpec((SLOT_BLOCK, D_MODEL // 2), rows),
        scratch_shapes=[pltpu.VMEM((D_MODEL, D_EXPERT), BF16), pltpu.VMEM((D_MODEL, D_EXPERT), BF16),
                        pltpu.VMEM((D_EXPERT, D_MODEL), BF16)])
    return pl.pallas_call(
        _expert_kernel, grid_spec=grid_spec,
        out_shape=jax.ShapeDtypeStruct((n_slots, D_MODEL // 2), jnp.uint32),
        compiler_params=_cparams(("arbitrary",)),
        name="moe_experts",
    )(block_expert, blocks_used, xs, wg, wu, wdn)


def _moe_out_kernel(g_ref, wk_ref, h_ref, x1_ref, mod_ref, npost_ref, sg_ref, su_ref, sd_ref, o_ref):
    h = h_ref[...]
    act = _silu(jnp.dot(h, sg_ref[...], preferred_element_type=F32)) * jnp.dot(h, su_ref[...],
                                                                              preferred_element_type=F32)
    moe = jnp.dot(act.astype(BF16), sd_ref[...], preferred_element_type=F32)
    wk = wk_ref[...]
    lo, hi = 0.0, 0.0
    for k in range(TOP_K):
        g_lo, g_hi = _unpack_rows(g_ref[k])
        lo = lo + g_lo * wk[:, k:k + 1]
        hi = hi + g_hi * wk[:, k:k + 1]
    moe = moe + jnp.concatenate([lo, hi], axis=-1)
    m = mod_ref[0]
    o_ref[...] = x1_ref[...] + m[5:6] * (_rms(moe) * npost_ref[...])


def _moe_out_stage(g, wk, h2, x1, mod, npost, sg, su, sd, st):
    t_all = h2.shape[0]
    row = lambda i: (i, 0)
    full = lambda a: pl.BlockSpec(a.shape, lambda i: (0, 0))
    return pl.pallas_call(
        _moe_out_kernel,
        grid=(t_all // TM,),
        in_specs=[pl.BlockSpec((TOP_K, TM, D_MODEL // 2), lambda i: (0, i, 0)), pl.BlockSpec((TM, LANES), row),
                  pl.BlockSpec((TM, D_MODEL), row), pl.BlockSpec((TM, D_MODEL), row), _mod_spec(st),
                  full(npost), full(sg), full(su), full(sd)],
        out_specs=pl.BlockSpec((TM, D_MODEL), row),
        out_shape=jax.ShapeDtypeStruct((t_all, D_MODEL), F32),
        compiler_params=_cparams(("parallel",)),
        name="moe_combine_" + st["name"],
    )(g, wk, h2, x1, mod, npost, sg, su, sd)


def _moe_stage(layer, h2, hpk, eidx, rank, wk, counts, x1, mod, npost, wg, wu, wdn, sg, su, sd, st):
    t_all = h2.shape[0]
    n_blocks = -(-(t_all * TOP_K + N_EXPERTS * (SLOT_BLOCK - 1)) // SLOT_BLOCK)
    cnt = counts[:, 0].astype(jnp.int32)
    padded = (cnt + SLOT_BLOCK - 1) // SLOT_BLOCK * SLOT_BLOCK
    pad_end = jnp.cumsum(padded)
    pad_start = pad_end - padded
    experts = jnp.arange(N_EXPERTS, dtype=jnp.int32)[:, None, None]
    dest = rank + jnp.sum(jnp.where(eidx[None] == experts, pad_start[:, None, None], 0), axis=0)
    dest = dest.reshape(TOP_K * t_all)
    block_start = jnp.arange(n_blocks, dtype=jnp.int32) * SLOT_BLOCK
    block_expert = jnp.minimum(jnp.sum((pad_end[None, :] <= block_start[:, None]).astype(jnp.int32), axis=1),
                               N_EXPERTS - 1)
    blocks_used = (pad_end[-1:] // SLOT_BLOCK).astype(jnp.int32)

    xs = _sc_dispatch(hpk, dest, n_blocks * SLOT_BLOCK)
    yb = _expert_blocks(layer, xs, block_expert, blocks_used, wg, wu, wdn)
    g = _sc_combine(yb, dest, t_all)
    return _moe_out_stage(g, wk, h2, x1, mod, npost, sg, su, sd, st)


_PERM32 = np.concatenate([np.arange(8, 16), np.arange(0, 8), np.arange(24, 32), np.arange(16, 24)])


def _rope_tables(n):
    pos = jnp.arange(n)
    quarter = QK_ROPE // 4
    inv = 1.0 / (ROPE_THETA ** (jnp.arange(quarter, dtype=F32) / quarter))
    ang_r = (pos // GRID_W).astype(F32)[:, None] * inv[None, :]
    ang_c = (pos % GRID_W).astype(F32)[:, None] * inv[None, :]
    cos32 = jnp.concatenate([jnp.cos(ang_r)] * 2 + [jnp.cos(ang_c)] * 2, axis=-1)
    sin32 = jnp.concatenate([-jnp.sin(ang_r), jnp.sin(ang_r), -jnp.sin(ang_c), jnp.sin(ang_c)], axis=-1)
    cos256, sin256 = jnp.tile(cos32, (1, 8)), jnp.tile(sin32, (1, 8))
    one, zero = jnp.ones((n, QK_NOPE), F32), jnp.zeros((n, QK_NOPE), F32)
    cosq = jnp.concatenate([one, cos32] * N_HEADS, axis=-1)
    sinq = jnp.concatenate([zero, sin32] * N_HEADS, axis=-1)
    return cos256, sin256, cosq, sinq


def _split_w_in(w_in_l):
    sizes = (256, 256, 256, 256, 256, 256, 256, 16, 256, 256, 256, 256, Q_LORA, KV_LORA, QK_ROPE)
    idx = np.cumsum(sizes)[:-1]
    return jnp.split(w_in_l, [int(v) for v in idx], axis=-1)


def _layer_weights(w_in_l, w_uq_l):
    (a_q, a_k, a_v, m_q, m_k, m_v, m_o, m_g, r_q, r_k, r_v, r_g, d_cq, d_ckv, d_kr) = _split_w_in(w_in_l)
    perm256 = np.concatenate([_PERM32 + 32 * j for j in range(8)])
    pad = lambda w, n: jnp.pad(w, ((0, 0), (0, n - w.shape[1])))
    tail = pad(jnp.concatenate([d_kr, m_g], axis=-1), LANES)
    tail_s = pad(d_kr[:, _PERM32], LANES)
    w_all = jnp.concatenate([a_q, a_k, a_v, m_q, m_k, m_v, m_o, r_q, r_k, r_v, r_g, d_cq, d_ckv, tail,
                             a_q[:, perm256], a_k[:, perm256], tail_s], axis=-1).astype(BF16)
    per = QK_NOPE + QK_ROPE
    permq = np.concatenate([np.concatenate([np.arange(QK_NOPE), QK_NOPE + _PERM32]) + per * j
                            for j in range(N_HEADS)])
    return w_all, w_uq_l.astype(BF16), w_uq_l[:, permq].astype(BF16)


def kernel(x_prompt, x_sample, cache_diff_k, cache_diff_v, state_mlstm_C, state_mlstm_n, state_mlstm_m, state_ret_S, cache_mla_ckv, cache_mla_krope, c, c_ctx, w_mod, b_mod, norm_pre, norm_post, w_in, w_out, diff_lambda, diff_norm, mlstm_gate_bias, mlstm_norm, ret_decay_logit, mla_q_norm, mla_w_uq, mla_kv_norm, mla_w_ukv, moe_w_router, moe_router_bias, moe_w_gate, moe_w_up, moe_w_down, shared_w_gate, shared_w_up, shared_w_down):
    bp, n_p, _ = x_prompt.shape
    bs, n_s, _ = x_sample.shape
    depth = w_in.shape[0]
    past = cache_diff_k.shape[2]
    assert (bp * n_p) % TM == 0 and n_s % TM == 0 and n_p % CHUNK == 0 and n_s % min(TK, n_s) == 0
    assert past % min(TK, past) == 0 and bs + 1 <= 8 and n_s % GRID_W == 0

    streams = (dict(name="ctx", nb=bp, n=n_p, latent=False), dict(name="latent", nb=bs, n=n_s, latent=True))
    xs_by_stream = [x_prompt.reshape(bp * n_p, D_MODEL), x_sample.reshape(bs * n_s, D_MODEL)]
    cond = jnp.zeros((8, D_MODEL), F32).at[0].set(c_ctx).at[1:1 + bs].set(c)
    mod_all = _modulation(cond, w_mod, b_mod).reshape(depth, 8, 6, D_MODEL)
    tabs = _rope_tables(n_s)
    scan_tabs = [_scan_table(st) for st in streams]

    outs = [[] for _ in range(8)]
    for l in range(depth):
        lam_init = 0.8 - 0.6 * math.exp(-0.3 * l)
        mod = mod_all[l]
        w_all, wuq, wuqs = _layer_weights(w_in[l], mla_w_uq[l])
        wukv = mla_w_ukv[l].astype(BF16)
        wout, wrt, rbias = w_out[l].astype(BF16), moe_w_router[l].T.astype(BF16), moe_router_bias[l][:, None]
        shared = [w[l].astype(BF16) for w in (shared_w_gate, shared_w_up, shared_w_down)]
        lamv, dg = diff_lambda[l], diff_norm[l][:, None]
        bias_row = jnp.zeros((1, LANES), F32).at[0, TAIL_GATE0:TAIL_GATE0 + 16].set(mlstm_gate_bias[l].reshape(16))

        with_ones = lambda vt: jnp.concatenate([vt, jnp.ones((N_HEADS, VT_ONES, bs * past), BF16)], axis=1)
        ck = jnp.transpose(cache_diff_k[:, l], (2, 0, 1, 3)).reshape(N_HEADS, bs * past, HEAD_DIM).astype(BF16)
        cvt = with_ones(jnp.transpose(cache_diff_v[:, l], (2, 3, 0, 1)).reshape(N_HEADS, HEAD_DIM, bs * past)
                        .astype(BF16))
        kvc = _project(cache_mla_ckv[:, l].reshape(bs * past, KV_LORA), wukv, BF16)
        kvc = kvc.reshape(bs * past, N_HEADS, QK_NOPE + V_HEAD)
        krc = jnp.broadcast_to(cache_mla_krope[:, l].reshape(bs * past, 1, QK_ROPE).astype(BF16),
                               (bs * past, N_HEADS, QK_ROPE))
        kc = jnp.transpose(jnp.concatenate([kvc[..., :QK_NOPE], krc], axis=-1), (1, 0, 2))
        vct = with_ones(jnp.transpose(kvc[..., QK_NOPE:], (1, 2, 0)))

        for si, st in enumerate(streams):
            x, nb, n = xs_by_stream[si], st["nb"], st["n"]
            (aq1t, aq2t, ak1, ak2, avt, ak, av, mqkv, mo, rqkv, rg, tail, qmlat, ckv, kmla, vmlat) = _input_stage(
                x, mod, norm_pre[l, 0:1], w_all, tabs, mla_q_norm[l][None], wuq, wuqs,
                mla_kv_norm[l][None], wukv, st)
            if st["latent"]:
                oa = _attention_t("diff_attn_latent", [aq1t, aq2t], [ak1, ak2], avt,
                                  [ck[..., :DIFF_HALF], ck[..., DIFF_HALF:]], cvt, [lamv, dg], nb, n, past, lam_init)
                od = _attention_t("mla_attn_latent", [qmlat], [kmla], vmlat, [kc], vct, [], nb, n, past)
                states = [s[:, l] for s in (state_mlstm_C, state_mlstm_n, state_mlstm_m, state_ret_S)]
            else:
                oa = _attention_t("diff_attn_ctx", [aq1t, aq2t], [ak1, ak2], avt, None, None, [lamv, dg],
                                  nb, n, 0, lam_init)
                od = _attention_t("mla_attn_ctx", [qmlat], [kmla], vmlat, None, None, [], nb, n, 0)
                states = [jnp.zeros((nb,) + s.shape[2:], F32)
                          for s in (state_mlstm_C, state_mlstm_n, state_mlstm_m, state_ret_S)]
            hf, hb, of, ob, c_n, n_n, m_n, s_n = _scans(scan_tabs[si], mqkv, tail, bias_row, rqkv,
                                                        ret_decay_logit[l], *states)

            x1, h2, hpk, eidx, rank, wk, counts = _output_stage(
                oa, od, hf, hb, mo, of, ob, rg, x, mod, mlstm_norm[l][None], norm_post[l, 0:1],
                norm_pre[l, 1:2], wout, wrt, rbias, st)
            xs_by_stream[si] = _moe_stage(l, h2, hpk, eidx, rank, wk, counts, x1, mod, norm_post[l, 1:2],
                                          moe_w_gate, moe_w_up, moe_w_down, *shared, st)
            if not st["latent"]:
                new = (ak.reshape(nb, n, N_HEADS, HEAD_DIM), av.reshape(nb, n, N_HEADS, HEAD_DIM), c_n, n_n, m_n,
                       s_n, ckv.reshape(nb, n, KV_LORA), tail[:, :QK_ROPE].reshape(nb, n, QK_ROPE))
                for o, v in zip(outs, new):
                    o.append(v)

    return (xs_by_stream[0].reshape(bp, n_p, D_MODEL), xs_by_stream[1].reshape(bs, n_s, D_MODEL)) + tuple(
        jnp.stack(o, axis=1) for o in outs)
```

```python
import functools
import math

import numpy as np
import jax
import jax.numpy as jnp
from jax import lax
from jax.experimental import pallas as pl
from jax.experimental.pallas import tpu as pltpu
from jax.experimental.pallas import tpu_sc as plsc

F32 = jnp.float32
BF16 = jnp.bfloat16
HIGHEST = lax.Precision.HIGHEST

D_MODEL = 1024
GRID_W = 64
GROUP_WIDTH = 256
HEAD_DIM = 64
N_HEADS = 4
DIFF_HALF = 32
ROPE_THETA = 10000.0
Q_LORA = 256
KV_LORA = 128
QK_NOPE = 64
QK_ROPE = 32
V_HEAD = 64
N_EXPERTS = 64
TOP_K = 8
N_GROUPS = 8
TOPK_GROUPS = 4
D_EXPERT = 256
ROUTED_SCALE = 2.5
CHUNK = 128
EPS = 1e-6
Q_MLA = N_HEADS * (QK_NOPE + QK_ROPE)

LANES = 128
VMEM_LIMIT = 56 * 1024 * 1024

TM = 512
TQ = 256
TK = 4096

C_AQ, C_AK, C_AV = 0, 256, 512
C_MQKV, C_MO = 768, 1536
C_RQKV, C_RG = 1792, 2560
C_CQ, C_CKV, C_TAIL = 2816, 3072, 3200
C_AQS, C_AKS, C_TAILS = 3328, 3584, 3840
TAIL_GATE0 = QK_ROPE
VT_ONES = 16
VT_ROWS = V_HEAD + VT_ONES


def _cparams(sem, flags=None):
    return pltpu.CompilerParams(dimension_semantics=sem, vmem_limit_bytes=VMEM_LIMIT, flags=flags)


def _rms(x):
    return x * lax.rsqrt(jnp.mean(x * x, axis=-1, keepdims=True) + EPS)


def _head_mean_matrix(width):
    r = lax.broadcasted_iota(jnp.int32, (width, width), 0) // HEAD_DIM
    c = lax.broadcasted_iota(jnp.int32, (width, width), 1) // HEAD_DIM
    return jnp.where(r == c, 1.0 / HEAD_DIM, 0.0).astype(BF16)


def _rms_heads(x):
    sq = x * x
    hi = sq.astype(BF16)
    lo = (sq - hi.astype(F32)).astype(BF16)
    g = _head_mean_matrix(x.shape[-1])
    ms = jnp.dot(hi, g, preferred_element_type=F32) + jnp.dot(lo, g, preferred_element_type=F32)
    return x * lax.rsqrt(ms + EPS)


def _sigmoid(x):
    return 1.0 / (1.0 + jnp.exp(-x))


def _silu(x):
    return x * _sigmoid(x)


def _log_sigmoid(x):
    return jnp.minimum(x, 0.0) - jnp.log1p(jnp.exp(-jnp.abs(x)))


def _bdot(a, b):
    return jnp.dot(a.astype(BF16), b.astype(BF16), preferred_element_type=F32)


def _mod_kernel(c_ref, w_ref, b_ref, o_ref):
    o_ref[0] = _bdot(_silu(c_ref[...]), w_ref[0]) + b_ref[0]


def _modulation(cond, w_mod, b_mod):
    depth, _, n = w_mod.shape
    tn = 1536
    return pl.pallas_call(
        _mod_kernel,
        grid=(depth, n // tn),
        in_specs=[pl.BlockSpec((8, D_MODEL), lambda l, j: (0, 0)),
                  pl.BlockSpec((1, D_MODEL, tn), lambda l, j: (l, 0, j)),
                  pl.BlockSpec((1, 1, tn), lambda l, j: (l, 0, j))],
        out_specs=pl.BlockSpec((1, 8, tn), lambda l, j: (l, 0, j)),
        out_shape=jax.ShapeDtypeStruct((depth, 8, n), F32),
        compiler_params=_cparams(("parallel", "parallel")),
        name="adaln_mod",
    )(cond, w_mod, b_mod.reshape(depth, 1, n))


def _in_kernel(x_ref, mod_ref, npre_ref, w_ref, cos_ref, sin_ref, cosq_ref, sinq_ref,
               qg_ref, wuq_ref, wuqs_ref, kvg_ref, wukv_ref,
               aq1_ref, aq2_ref, ak1t_ref, ak2t_ref, avh_ref, ak_ref, av_ref,
               mqkv_ref, mo_ref, rqkv_ref, rg_ref, tail_ref,
               qmla_ref, ckv_ref, kmlat_ref, vmla_ref, *, latent):
    m = mod_ref[0]
    h = (_rms(x_ref[...]) * npre_ref[...] * (1.0 + m[1:2]) + m[0:1]).astype(BF16)

    def proj(c0, width):
        return jnp.dot(h, w_ref[:, c0:c0 + width], preferred_element_type=F32)

    def rotated(c0, c0_swapped, width, cos, sin):
        return proj(c0, width) * cos + proj(c0_swapped, width) * sin if latent else proj(c0, width)

    cos = cos_ref[...]
    sin = sin_ref[...]
    aq = rotated(C_AQ, C_AQS, GROUP_WIDTH, cos, sin) * (DIFF_HALF ** -0.5)
    ak = rotated(C_AK, C_AKS, GROUP_WIDTH, cos, sin)
    av = proj(C_AV, GROUP_WIDTH)
    ak_ref[...] = ak
    av_ref[...] = av
    aq_t, av_t = aq.T.astype(BF16), av.T.astype(BF16)
    ones = jnp.ones((VT_ONES, aq_t.shape[1]), BF16)
    for hd in range(N_HEADS):
        lo = hd * HEAD_DIM
        aq1_ref[hd] = aq_t[lo:lo + DIFF_HALF, :]
        aq2_ref[hd] = aq_t[lo + DIFF_HALF:lo + HEAD_DIM, :]
        ak1t_ref[hd] = ak[:, lo:lo + DIFF_HALF].astype(BF16)
        ak2t_ref[hd] = ak[:, lo + DIFF_HALF:lo + HEAD_DIM].astype(BF16)
        avh_ref[hd, 0:HEAD_DIM, :] = av_t[lo:lo + HEAD_DIM, :]
        avh_ref[hd, HEAD_DIM:VT_ROWS, :] = ones

    mqkv_ref[...] = proj(C_MQKV, 3 * GROUP_WIDTH).astype(BF16)
    mo_ref[...] = proj(C_MO, GROUP_WIDTH)
    rqkv_ref[...] = proj(C_RQKV, 3 * GROUP_WIDTH).astype(BF16)
    rg_ref[...] = proj(C_RG, GROUP_WIDTH)

    tail = proj(C_TAIL, LANES)
    tail_ref[...] = tail
    kr = tail * cos[:, :LANES] + proj(C_TAILS, LANES) * sin[:, :LANES] if latent else tail
    kr = kr[:, :QK_ROPE].astype(BF16)

    cqn = (_rms(proj(C_CQ, Q_LORA)) * qg_ref[...]).astype(BF16)
    qmla = jnp.dot(cqn, wuq_ref[...], preferred_element_type=F32)
    if latent:
        qmla = qmla * cosq_ref[...] + jnp.dot(cqn, wuqs_ref[...], preferred_element_type=F32) * sinq_ref[...]
    qmla_t = (qmla * ((QK_NOPE + QK_ROPE) ** -0.5)).T.astype(BF16)
    for hd in range(N_HEADS):
        lo = hd * (QK_NOPE + QK_ROPE)
        qmla_ref[hd] = qmla_t[lo:lo + QK_NOPE + QK_ROPE, :]
    ckvn = _rms(proj(C_CKV, KV_LORA)) * kvg_ref[...]
    ckv_ref[...] = ckvn
    kv = jnp.dot(ckvn.astype(BF16), wukv_ref[...], preferred_element_type=F32)
    kv_t = kv.T.astype(BF16)
    per = QK_NOPE + V_HEAD
    for hd in range(N_HEADS):
        kmlat_ref[hd, :, 0:QK_NOPE] = kv[:, hd * per:hd * per + QK_NOPE].astype(BF16)
        kmlat_ref[hd, :, QK_NOPE:QK_NOPE + QK_ROPE] = kr
        vmla_ref[hd, 0:V_HEAD, :] = kv_t[hd * per + QK_NOPE:(hd + 1) * per, :]
        vmla_ref[hd, V_HEAD:VT_ROWS, :] = ones


def _mod_spec(st, tm=TM):
    tps = st["n"] // tm
    return pl.BlockSpec((1, 6, D_MODEL), lambda i, *_: (1 + i // tps if st["latent"] else 0, 0, 0))


def _input_stage(x, mod, npre, w_all, tabs, qg, wuq, wuqs, kvg, wukv, st):
    t_all = x.shape[0]
    tm = TM
    tps = st["n"] // tm
    row = lambda i: (i, 0)
    tab = lambda w: pl.BlockSpec((tm, w), lambda i: (i % tps if st["latent"] else 0, 0))
    hrow = lambda i: (0, i, 0)
    const2 = lambda i: (0, 0)
    tok = lambda w: pl.BlockSpec((tm, w), row)
    headed = lambda w: pl.BlockSpec((N_HEADS, tm, w), hrow)
    headed_t = lambda d: pl.BlockSpec((N_HEADS, d, tm), lambda i: (0, 0, i))
    full = lambda a: pl.BlockSpec(a.shape, const2)
    cos, sin, cosq, sinq = tabs
    out_shapes = [
        (headed_t(DIFF_HALF), (N_HEADS, DIFF_HALF, t_all), BF16),
        (headed_t(DIFF_HALF), (N_HEADS, DIFF_HALF, t_all), BF16),
        (headed(DIFF_HALF), (N_HEADS, t_all, DIFF_HALF), BF16),
        (headed(DIFF_HALF), (N_HEADS, t_all, DIFF_HALF), BF16),
        (headed_t(VT_ROWS), (N_HEADS, VT_ROWS, t_all), BF16),
        (tok(GROUP_WIDTH), (t_all, GROUP_WIDTH), F32),
        (tok(GROUP_WIDTH), (t_all, GROUP_WIDTH), F32),
        (tok(3 * GROUP_WIDTH), (t_all, 3 * GROUP_WIDTH), BF16),
        (tok(GROUP_WIDTH), (t_all, GROUP_WIDTH), F32),
        (tok(3 * GROUP_WIDTH), (t_all, 3 * GROUP_WIDTH), BF16),
        (tok(GROUP_WIDTH), (t_all, GROUP_WIDTH), F32),
        (tok(LANES), (t_all, LANES), F32),
        (headed_t(QK_NOPE + QK_ROPE), (N_HEADS, QK_NOPE + QK_ROPE, t_all), BF16),
        (tok(KV_LORA), (t_all, KV_LORA), F32),
        (headed(QK_NOPE + QK_ROPE), (N_HEADS, t_all, QK_NOPE + QK_ROPE), BF16),
        (headed_t(VT_ROWS), (N_HEADS, VT_ROWS, t_all), BF16),
    ]
    return pl.pallas_call(
        functools.partial(_in_kernel, latent=st["latent"]),
        grid=(t_all // tm,),
        in_specs=[tok(D_MODEL), _mod_spec(st, tm),
                  full(npre), full(w_all), tab(GROUP_WIDTH), tab(GROUP_WIDTH), tab(Q_MLA), tab(Q_MLA),
                  full(qg), full(wuq), full(wuqs), full(kvg), full(wukv)],
        out_specs=[s for s, _, _ in out_shapes],
        out_shape=[jax.ShapeDtypeStruct(shp, dt) for _, shp, dt in out_shapes],
        compiler_params=_cparams(("parallel",)),
        name="input_stage_" + st["name"],
    )(x, mod, npre, w_all, cos, sin, cosq, sinq, qg, wuq, wuqs, kvg, wukv)


def _attn_t_kernel(*refs, n_soft, n_new, n_ctx, lam_init):
    refs = list(refs)
    qt_refs, k_refs, vt_ref = refs[:n_soft], refs[n_soft:2 * n_soft], refs[2 * n_soft]
    pos = 2 * n_soft + 1
    if n_ctx:
        ck_refs, cvt_ref = refs[pos:pos + n_soft], refs[pos + n_soft]
        pos += n_soft + 1
    if n_soft == 2:
        lam_ref, g_ref = refs[pos:pos + 2]
        pos += 2
    o_ref = refs[pos]
    tq = o_ref.shape[0]
    nchain = n_soft * N_HEADS
    order = [(j, hd) for hd in range(N_HEADS) for j in range(n_soft)]

    def chunk(state, k_of, vt_of):
        ms, accs = state
        new_m, new_acc = list(ms), list(accs)
        ss = [jnp.dot(k_of(j, hd), qt_refs[j][hd], preferred_element_type=F32) for j, hd in order]
        ps, alphas = [], []
        for (j, hd), s in zip(order, ss):
            c = j * N_HEADS + hd
            s3 = s.reshape(s.shape[0] // 8, 8, tq)
            top = jnp.max(jnp.max(s3, axis=0), axis=0, keepdims=True)
            m_new = jnp.maximum(ms[c], jnp.broadcast_to(top, (8, tq)))
            ps.append(jnp.exp(s3 - m_new[None]).reshape(s.shape).astype(BF16))
            alphas.append(jnp.exp(ms[c] - m_new))
            new_m[c] = m_new
        for (j, hd), p, alpha in zip(order, ps, alphas):
            c = j * N_HEADS + hd
            scaled = (accs[c].reshape(VT_ROWS // 8, 8, tq) * alpha[None]).reshape(VT_ROWS, tq)
            new_acc[c] = scaled + jnp.dot(vt_of(hd), p, preferred_element_type=F32)
        return tuple(new_m), tuple(new_acc)

    state = (tuple(jnp.full((8, tq), -jnp.inf, F32) for _ in range(nchain)),
             tuple(jnp.zeros((VT_ROWS, tq), F32) for _ in range(nchain)))
    if n_ctx:
        cstep = min(TK, n_ctx)
        for i in range(n_ctx // cstep):
            state = chunk(state, lambda j, hd, i=i: ck_refs[j][hd, i * cstep:(i + 1) * cstep, :],
                          lambda hd, i=i: cvt_ref[hd, :, i * cstep:(i + 1) * cstep])
    step = min(TK, n_new)

    def body(i, st):
        start = pl.multiple_of(i * step, step)
        return chunk(st, lambda j, hd: k_refs[j][hd, pl.ds(start, step), :],
                     lambda hd: vt_ref[hd, :, pl.ds(start, step)])

    _, accs = lax.fori_loop(0, n_new // step, body, state)

    def normalised(c):
        num, den = accs[c][:V_HEAD], accs[c][V_HEAD:V_HEAD + 8]
        return (num.reshape(V_HEAD // 8, 8, tq) / den[None]).reshape(V_HEAD, tq)

    if n_soft == 2:
        lv = lam_ref[...]
        lam = (jnp.exp(jnp.sum(lv[0:1] * lv[1:2], axis=-1, keepdims=True))
               - jnp.exp(jnp.sum(lv[2:3] * lv[3:4], axis=-1, keepdims=True)) + lam_init)
    outs = []
    for hd in range(N_HEADS):
        out = normalised(hd)
        if n_soft == 2:
            a = out - lam * normalised(N_HEADS + hd)
            out = a * lax.rsqrt(jnp.mean(a * a, axis=0, keepdims=True) + EPS) * g_ref[...] * (1.0 - lam_init)
        outs.append(out)
    o_ref[...] = jnp.concatenate(outs, axis=0).T


def _attention_t(name, qts, ks, vt, ctx_ks, ctx_vt, extras, nb, n, n_ctx, lam_init=0.0):
    tq = min(TQ, n)
    nqt = n // tq
    n_soft = len(qts)
    in_specs = [pl.BlockSpec((N_HEADS, a.shape[1], tq), lambda b, i: (0, 0, b * nqt + i)) for a in qts]
    in_specs += [pl.BlockSpec((N_HEADS, n, a.shape[-1]), lambda b, i: (0, b, 0)) for a in ks]
    in_specs += [pl.BlockSpec((N_HEADS, VT_ROWS, n), lambda b, i: (0, 0, b))]
    args = list(qts) + list(ks) + [vt]
    if n_ctx:
        in_specs += [pl.BlockSpec((N_HEADS, n_ctx, a.shape[-1]), lambda b, i: (0, b, 0)) for a in ctx_ks]
        in_specs += [pl.BlockSpec((N_HEADS, VT_ROWS, n_ctx), lambda b, i: (0, 0, b))]
        args += list(ctx_ks) + [ctx_vt]
    in_specs += [pl.BlockSpec(a.shape, lambda b, i: (0, 0)) for a in extras]
    args += list(extras)
    return pl.pallas_call(
        functools.partial(_attn_t_kernel, n_soft=n_soft, n_new=n, n_ctx=n_ctx, lam_init=lam_init),
        grid=(nb, nqt),
        in_specs=in_specs,
        out_specs=pl.BlockSpec((tq, N_HEADS * V_HEAD), lambda b, i: (b * nqt + i, 0)),
        out_shape=jax.ShapeDtypeStruct((nb * n, N_HEADS * V_HEAD), F32),
        compiler_params=_cparams(("parallel", "parallel")),
        name=name,
    )(*args)


def _tri(lower):
    r = lax.broadcasted_iota(jnp.int32, (CHUNK, CHUNK), 0)
    c = lax.broadcasted_iota(jnp.int32, (CHUNK, CHUNK), 1)
    return (c <= r) if lower else (c >= r)


def _nt(a, b):
    return lax.dot_general(a, b, (((1,), (1,)), ((), ())), preferred_element_type=F32)


def _tn(a, b):
    return lax.dot_general(a, b, (((0,), (0,)), ((), ())), preferred_element_type=F32)


def _head_slices(qkv, hd):
    lo = hd * HEAD_DIM
    return tuple(qkv[:, j * GROUP_WIDTH + lo:j * GROUP_WIDTH + lo + HEAD_DIM] for j in range(3))


def _row_dot(row, mat, transpose_mat):
    row8 = jnp.broadcast_to(row, (8, row.shape[1])).astype(BF16)
    dims = (((1,), (1,)), ((), ())) if transpose_mat else (((1,), (0,)), ((), ()))
    return lax.dot_general(row8, mat, dims, preferred_element_type=F32)[0:1]


def _load_state_t(dst, src):
    for d in range(2):
        for hd in range(N_HEADS):
            dst[d, hd] = src[d, hd].T


def _mlstm_step(dirs, bias_ref, ct_s, n_s, m_s):
    k_scale = HEAD_DIM ** -0.5
    lower = _tri(True).astype(F32)
    upper = _tri(False).astype(F32)
    chains = []
    for d, qkv_ref, tail_ref, _ in dirs:
        g = tail_ref[...] + bias_ref[...]
        ls = _log_sigmoid(g)
        g_t, ls_t = g.T, ls.T
        left, right = (lower, upper) if d == 0 else (upper, lower)
        cum_col = jnp.dot(left, ls, precision=HIGHEST, preferred_element_type=F32)
        cum_row = jnp.dot(ls_t, right, precision=HIGHEST, preferred_element_type=F32)
        qkv = qkv_ref[...]
        for hd in range(N_HEADS):
            ci = TAIL_GATE0 + 4 * (2 * d) + hd
            cf = ci + 4
            q, k, v = _head_slices(qkv, hd)
            chains.append(dict(
                d=d, hd=hd, q=q, k=k, v=v, li_row=g_t[ci:ci + 1, :], b_row=cum_row[cf:cf + 1, :],
                c_col=g[:, ci:ci + 1] - cum_col[:, cf:cf + 1],
                m_prev=m_s[d:d + 1, hd:hd + 1], ct=ct_s[d, hd], n=n_s[d, hd:hd + 1, :]))
    for c in chains:
        c["s"] = _nt(c["k"], c["q"])
        c["v_t"] = c["v"].T
    for c in chains:
        valid = _tri(c["d"] != 0)
        log_d = jnp.where(valid, c["b_row"] + c["c_col"], -jnp.inf)
        log_inter = c["b_row"] + c["m_prev"]
        c["m_t"] = jnp.maximum(log_inter, jnp.max(log_d, axis=0, keepdims=True))
        c["w"] = c["s"] * k_scale * jnp.exp(log_d - c["m_t"])
        c["w_inter"] = jnp.exp(log_inter - c["m_t"])
    for c in chains:
        num = (jnp.dot(c["v_t"], c["w"].astype(BF16), preferred_element_type=F32)
               + c["w_inter"] * _nt(c["ct"].astype(BF16), c["q"]))
        den = jnp.sum(c["w"], axis=0, keepdims=True) + c["w_inter"] * _row_dot(c["n"], c["q"], True)
        c["h"] = num / jnp.maximum(jnp.abs(den), jnp.exp(-c["m_t"]))
    for c in chains:
        last = CHUNK - 1 if c["d"] == 0 else 0
        c["m_new"] = c["m_t"][:, last:last + 1]
        b_last = c["b_row"][:, last:last + 1]
        w_end_row = jnp.exp(b_last - c["b_row"] + c["li_row"] - c["m_new"])
        w_end_col = jnp.exp(c["c_col"] + (b_last - c["m_new"]))
        decay = jnp.exp(b_last + c["m_prev"] - c["m_new"])
        kw = (c["k"].astype(F32) * (w_end_col * k_scale)).astype(BF16)
        c["ct_new"] = decay * c["ct"] + jnp.dot(c["v_t"], kw, preferred_element_type=F32)
        c["n_new"] = decay * c["n"] + _row_dot(w_end_row, c["k"], False) * k_scale
    for j, (_, _, _, h_ref) in enumerate(dirs):
        h_ref[...] = jnp.concatenate([c["h"] for c in chains[j * N_HEADS:(j + 1) * N_HEADS]], axis=0).T
    for c in chains:
        d, hd = c["d"], c["hd"]
        ct_s[d, hd] = c["ct_new"]
        n_s[d, hd:hd + 1, :] = c["n_new"]
        m_s[d:d + 1, hd:hd + 1] = c["m_new"]


def _ret_step(dirs, lg_ref, st_s):
    k_scale = HEAD_DIM ** -0.5
    s = lax.broadcasted_iota(jnp.int32, (CHUNK, CHUNK), 0)
    t = lax.broadcasted_iota(jnp.int32, (CHUNK, CHUNK), 1)
    lane = lax.broadcasted_iota(jnp.int32, (1, CHUNK), 1)
    chains = []
    for d, qkv_ref, _ in dirs:
        lag = (t - s) if d == 0 else (s - t)
        pos = (lane if d == 0 else CHUNK - 1 - lane).astype(F32)
        qkv = qkv_ref[...]
        for hd in range(N_HEADS):
            q, k, v = _head_slices(qkv, hd)
            chains.append(dict(d=d, hd=hd, q=q, k=k, v=v, lag=lag, pos=pos, st=st_s[d, hd],
                               lg=_log_sigmoid(lg_ref[d:d + 1, hd:hd + 1])))
    for c in chains:
        c["s"] = _nt(c["k"], c["q"])
        c["v_t"] = c["v"].T
    for c in chains:
        intra = jnp.where(c["lag"] >= 0, jnp.exp(jnp.maximum(c["lag"], 0).astype(F32) * c["lg"]), 0.0)
        c["a"] = (c["s"] * k_scale * intra).astype(BF16)
    for c in chains:
        inter = jnp.exp((c["pos"] + 1.0) * c["lg"])
        c["o"] = jnp.dot(c["v_t"], c["a"], preferred_element_type=F32) + inter * _nt(c["st"].astype(BF16), c["q"])
    row = lax.broadcasted_iota(jnp.int32, (CHUNK, HEAD_DIM), 0)
    for c in chains:
        src_pos = (row if c["d"] == 0 else CHUNK - 1 - row).astype(F32)
        tail = jnp.exp((CHUNK - 1.0 - src_pos) * c["lg"]) * k_scale
        kw = (c["k"].astype(F32) * tail).astype(BF16)
        c["st_new"] = jnp.exp(CHUNK * c["lg"]) * c["st"] + jnp.dot(c["v_t"], kw, preferred_element_type=F32)
    for j, (_, _, o_ref) in enumerate(dirs):
        o_ref[...] = jnp.concatenate([c["o"] for c in chains[j * N_HEADS:(j + 1) * N_HEADS]], axis=0).T
    for c in chains:
        st_s[c["d"], c["hd"]] = c["st_new"]


def _scan_kernel(tab_ref, mqf_ref, tailf_ref, mqb_ref, tailb_ref, rqf_ref, rqb_ref, bias_ref, lg_ref,
                 c0_ref, n0_ref, m0_ref, s0_ref,
                 hf_ref, hb_ref, of_ref, ob_ref, cn_ref, nn_ref, mn_ref, sn_ref, ct_s, n_s, m_s, st_s):
    i = pl.program_id(0)

    @pl.when(tab_ref[3, i] == 1)
    def _():
        _load_state_t(ct_s, c0_ref.at[0])
        _load_state_t(st_s, s0_ref.at[0])
        n_s[...] = n0_ref[0]
        m_s[...] = m0_ref[0]

    _mlstm_step([(0, mqf_ref, tailf_ref, hf_ref), (1, mqb_ref, tailb_ref, hb_ref)], bias_ref, ct_s, n_s, m_s)
    _ret_step([(0, rqf_ref, of_ref), (1, rqb_ref, ob_ref)], lg_ref, st_s)

    @pl.when(tab_ref[4, i] == 1)
    def _():
        _load_state_t(cn_ref.at[0], ct_s)
        _load_state_t(sn_ref.at[0], st_s)
        nn_ref[0] = n_s[...]
        mn_ref[0] = m_s[...]


def _scan_table(st):
    nch = st["n"] // CHUNK
    rows = [(b * nch + c, b * nch + nch - 1 - c, b, int(c == 0), int(c == nch - 1))
            for b in range(st["nb"]) for c in range(nch)]
    return jnp.asarray(np.array(rows, dtype=np.int32).T)


def _scans(tab, mqkv, tail, bias_row, rqkv, decay_logit, c0, n0, m0, s0):
    t_all = mqkv.shape[0]
    fwd = lambda i, tab: (tab[0, i], 0)
    bwd = lambda i, tab: (tab[1, i], 0)
    const = lambda i, tab: (0, 0)
    state = lambda a: pl.BlockSpec((1,) + a.shape[1:], lambda i, tab: (tab[2, i],) + (0,) * (a.ndim - 1))
    states = (c0, n0, m0, s0)
    grid_spec = pltpu.PrefetchScalarGridSpec(
        num_scalar_prefetch=1,
        grid=(tab.shape[1],),
        in_specs=[pl.BlockSpec((CHUNK, 3 * GROUP_WIDTH), fwd), pl.BlockSpec((CHUNK, LANES), fwd),
                  pl.BlockSpec((CHUNK, 3 * GROUP_WIDTH), bwd), pl.BlockSpec((CHUNK, LANES), bwd),
                  pl.BlockSpec((CHUNK, 3 * GROUP_WIDTH), fwd), pl.BlockSpec((CHUNK, 3 * GROUP_WIDTH), bwd),
                  pl.BlockSpec((1, LANES), const), pl.BlockSpec(decay_logit.shape, const)]
        + [state(a) for a in states],
        out_specs=[pl.BlockSpec((CHUNK, GROUP_WIDTH), fwd), pl.BlockSpec((CHUNK, GROUP_WIDTH), bwd),
                   pl.BlockSpec((CHUNK, GROUP_WIDTH), fwd), pl.BlockSpec((CHUNK, GROUP_WIDTH), bwd)]
        + [state(a) for a in states],
        scratch_shapes=[pltpu.VMEM(a.shape[1:], F32) for a in states],
    )
    return pl.pallas_call(
        _scan_kernel,
        grid_spec=grid_spec,
        out_shape=[jax.ShapeDtypeStruct((t_all, GROUP_WIDTH), F32)] * 4 + [jax.ShapeDtypeStruct(a.shape, F32) for a in states],
        compiler_params=_cparams(("arbitrary",)),
        name="recurrent_scans",
    )(tab, mqkv, tail, mqkv, tail, rqkv, rqkv, bias_row, decay_logit, *states)


def _proj_kernel(x_ref, w_ref, o_ref):
    o_ref[...] = _bdot(x_ref[...], w_ref[...]).astype(o_ref.dtype)


def _project(x, w, dtype):
    return pl.pallas_call(
        _proj_kernel,
        out_shape=jax.ShapeDtypeStruct((x.shape[0], w.shape[1]), dtype),
        name="ctx_kv_proj",
    )(x, w)


def _first_argmax_mask(cur, axis, size):
    io = lax.broadcasted_iota(jnp.int32, cur.shape, axis)
    mx = jnp.max(cur, axis=axis, keepdims=True)
    ix = jnp.min(jnp.where(cur == mx, io, size), axis=axis, keepdims=True)
    return io == ix


def _route(scores_t, bias_col):
    tm = scores_t.shape[1]
    per = N_EXPERTS // N_GROUPS
    sel = scores_t + bias_col
    s3 = sel.reshape(N_GROUPS, per, tm)
    hit1 = _first_argmax_mask(s3, 1, per)
    m1 = jnp.max(s3, axis=1, keepdims=True)
    m2 = jnp.max(jnp.where(hit1, -jnp.inf, s3), axis=1, keepdims=True)
    cur = m1 + m2
    gsel = None
    for _ in range(TOPK_GROUPS):
        hit = _first_argmax_mask(cur, 0, N_GROUPS)
        gsel = hit if gsel is None else jnp.logical_or(gsel, hit)
        cur = jnp.where(hit, -jnp.inf, cur)
    cur = jnp.where(gsel, s3, -jnp.inf).reshape(N_EXPERTS, tm)
    chosen, hits = None, []
    for _ in range(TOP_K):
        hit = _first_argmax_mask(cur, 0, N_EXPERTS)
        hits.append(hit)
        chosen = hit if chosen is None else jnp.logical_or(chosen, hit)
        cur = jnp.where(hit, -jnp.inf, cur)
    w = jnp.where(chosen, scores_t, 0.0)
    return w / jnp.sum(w, axis=0, keepdims=True) * ROUTED_SCALE, chosen, hits


def _out_kernel(oa_ref, od_ref, hf_ref, hb_ref, mo_ref, of_ref, ob_ref, rg_ref,
                x_ref, mod_ref, mg_ref, npost_ref, npre_ref, wout_ref, wrt_ref, rb_ref,
                x1_ref, h2_ref, hpk_ref, eidx_ref, rank_ref, wk_ref, cnt_ref, count_s):
    @pl.when(pl.program_id(0) == 0)
    def _():
        count_s[...] = jnp.zeros(count_s.shape, F32)

    a = oa_ref[...]
    dd = od_ref[...]
    b = _rms_heads(hf_ref[...] + hb_ref[...]) * mg_ref[...] * _sigmoid(mo_ref[...])
    r = _rms_heads(of_ref[...] + ob_ref[...]) * _silu(rg_ref[...])
    mix = sum(jnp.dot(part.astype(BF16), wout_ref[j * GROUP_WIDTH:(j + 1) * GROUP_WIDTH, :],
                      preferred_element_type=F32) for j, part in enumerate((a, b, r, dd)))
    m = mod_ref[0]
    x1 = x_ref[...] + m[2:3] * (_rms(mix) * npost_ref[...])
    x1_ref[...] = x1
    h2 = (_rms(x1) * npre_ref[...] * (1.0 + m[4:5]) + m[3:4]).astype(BF16)
    h2_ref[...] = h2
    hpk_ref[...] = _pack_rows(h2)

    logits_t = lax.dot_general(wrt_ref[...], h2, (((1,), (1,)), ((), ())), preferred_element_type=F32)
    w_t, chosen, hits = _route(_sigmoid(logits_t), rb_ref[...])
    tm = w_t.shape[1]
    src = lax.broadcasted_iota(jnp.int32, (tm, tm), 0)
    dst = lax.broadcasted_iota(jnp.int32, (tm, tm), 1)
    before = jnp.where(src < dst, 1.0, 0.0).astype(BF16)
    picked = jnp.where(chosen, 1.0, 0.0)
    rank = jnp.dot(picked.astype(BF16), before, preferred_element_type=F32) + count_s[...]
    count_s[...] += jnp.sum(picked, axis=1, keepdims=True)
    cnt_ref[...] = jnp.broadcast_to(count_s[...], cnt_ref.shape)
    e_io = lax.broadcasted_iota(jnp.int32, w_t.shape, 0)
    pick = lambda hit, val: jnp.sum(jnp.where(hit, val, jnp.zeros_like(val)), axis=0, keepdims=True)
    eidx_ref[...] = jnp.concatenate([pick(h, e_io) for h in hits], axis=0)
    rank_ref[...] = jnp.concatenate([pick(h, rank) for h in hits], axis=0).astype(jnp.int32)
    wk = jnp.concatenate([pick(h, w_t) for h in hits] + [jnp.zeros((LANES - TOP_K, tm), F32)], axis=0)
    wk_ref[...] = wk.T


def _output_stage(oa, od, hf, hb, mo, of, ob, rg, x, mod, mg, npost, npre, wout, wrt, rb, st):
    t_all = x.shape[0]
    row = lambda i: (i, 0)
    const2 = lambda i: (0, 0)
    tok = lambda w: pl.BlockSpec((TM, w), row)
    full = lambda a: pl.BlockSpec(a.shape, const2)
    return pl.pallas_call(
        _out_kernel,
        grid=(t_all // TM,),
        in_specs=[tok(GROUP_WIDTH)] * 8 + [tok(D_MODEL), _mod_spec(st),
                                   full(mg), full(npost), full(npre), full(wout), full(wrt), full(rb)],
        out_specs=[tok(D_MODEL), tok(D_MODEL), tok(D_MODEL // 2),
                   pl.BlockSpec((TOP_K, TM), lambda i: (0, i)), pl.BlockSpec((TOP_K, TM), lambda i: (0, i)),
                   tok(LANES), pl.BlockSpec((N_EXPERTS, LANES), const2)],
        out_shape=[jax.ShapeDtypeStruct((t_all, D_MODEL), F32),
                   jax.ShapeDtypeStruct((t_all, D_MODEL), BF16),
                   jax.ShapeDtypeStruct((t_all, D_MODEL // 2), jnp.uint32),
                   jax.ShapeDtypeStruct((TOP_K, t_all), jnp.int32),
                   jax.ShapeDtypeStruct((TOP_K, t_all), jnp.int32),
                   jax.ShapeDtypeStruct((t_all, LANES), F32),
                   jax.ShapeDtypeStruct((N_EXPERTS, LANES), F32)],
        scratch_shapes=[pltpu.VMEM((N_EXPERTS, 1), F32)],
        compiler_params=_cparams(("arbitrary",)),
        name="output_stage_" + st["name"],
    )(oa, od, hf, hb, mo, of, ob, rg, x, mod, mg, npost, npre, wout, wrt, rb)


SC_CORES, SC_SUBCORES = 2, 16
SC_WORKERS = SC_CORES * SC_SUBCORES
SLOT_BLOCK = 512
DISPATCH_ROWS = 64
COMBINE_ROWS = 64


def _sc_mesh():
    return plsc.VectorSubcoreMesh(core_axis_name="core", subcore_axis_name="subcore")


def _sc_worker():
    return lax.axis_index("subcore") * SC_CORES + lax.axis_index("core")


def _sc_dispatch(hpk, dest, n_slots):
    t_all, width = hpk.shape
    per_worker = t_all // SC_WORKERS
    assert t_all % (SC_WORKERS * DISPATCH_ROWS) == 0

    @functools.partial(
        pl.kernel, mesh=_sc_mesh(), out_type=jax.ShapeDtypeStruct((n_slots, width), hpk.dtype),
        scratch_types=[pltpu.VMEM((DISPATCH_ROWS, width), hpk.dtype)]
        + [pltpu.VMEM((DISPATCH_ROWS,), jnp.int32)] * TOP_K + [pltpu.SemaphoreType.DMA])
    def dispatch(x_hbm, d_hbm, o_hbm, rows_v, *rest):
        idx, sem = rest[:TOP_K], rest[TOP_K]

        @pl.loop(0, per_worker // DISPATCH_ROWS)
        def _(j):
            base = _sc_worker() * per_worker + j * DISPATCH_ROWS
            pltpu.sync_copy(x_hbm.at[pl.ds(base, DISPATCH_ROWS)], rows_v)
            for k in range(TOP_K):
                pltpu.sync_copy(d_hbm.at[pl.ds(k * t_all + base, DISPATCH_ROWS)], idx[k])
            copies = [pltpu.async_copy(rows_v, o_hbm.at[idx[k]], sem) for k in range(TOP_K)]
            for c in copies:
                c.wait()

    return dispatch(hpk, dest)


def _sc_combine(yb, dest, t_all):
    width = yb.shape[1]
    per_worker = t_all // SC_WORKERS
    assert t_all % (SC_WORKERS * COMBINE_ROWS) == 0

    @functools.partial(
        pl.kernel, mesh=_sc_mesh(), out_type=jax.ShapeDtypeStruct((TOP_K, t_all, width), yb.dtype),
        scratch_types=[pltpu.VMEM((COMBINE_ROWS, width), yb.dtype)] * 2
        + [pltpu.VMEM((COMBINE_ROWS,), jnp.int32)] * TOP_K + [pltpu.SemaphoreType.DMA] * 2)
    def combine(y_hbm, d_hbm, o_hbm, rows_a, rows_b, *rest):
        idx, sems = rest[:TOP_K], rest[TOP_K:]
        bufs = (rows_a, rows_b)

        @pl.loop(0, per_worker // COMBINE_ROWS)
        def _(j):
            base = _sc_worker() * per_worker + j * COMBINE_ROWS
            for k in range(TOP_K):
                pltpu.sync_copy(d_hbm.at[pl.ds(k * t_all + base, COMBINE_ROWS)], idx[k])
            pending = pltpu.async_copy(y_hbm.at[idx[0]], bufs[0], sems[0])
            for k in range(TOP_K):
                pending.wait()
                if k + 1 < TOP_K:
                    pending = pltpu.async_copy(y_hbm.at[idx[k + 1]], bufs[(k + 1) % 2], sems[(k + 1) % 2])
                pltpu.sync_copy(bufs[k % 2], o_hbm.at[k, pl.ds(base, COMBINE_ROWS)])

    return combine(yb, dest)


def _pack_rows(x):
    bits = pltpu.bitcast(x.astype(BF16).astype(F32), jnp.uint32)
    w = x.shape[-1] // 2
    return (bits[..., :w] >> 16) | (bits[..., w:] & jnp.uint32(0xFFFF0000))


def _unpack_rows(words):
    return pltpu.bitcast(words << 16, F32), pltpu.bitcast(words & jnp.uint32(0xFFFF0000), F32)


def _expert_kernel(be_ref, used_ref, xs_ref, wg_ref, wu_ref, wd_ref, y_ref, wg_s, wu_s, wd_s):
    b = pl.program_id(0)
    fresh = jnp.logical_or(b == 0, be_ref[b] != be_ref[jnp.maximum(b - 1, 0)])

    @pl.when(jnp.logical_and(fresh, b < used_ref[0]))
    def _():
        wg_s[...] = wg_ref[0, 0].astype(BF16)
        wu_s[...] = wu_ref[0, 0].astype(BF16)
        wd_s[...] = wd_ref[0, 0].astype(BF16)

    @pl.when(b < used_ref[0])
    def _():
        lo, hi = (v.astype(BF16) for v in _unpack_rows(xs_ref[...]))
        half = D_MODEL // 2
        gate = (jnp.dot(lo, wg_s[:half, :], preferred_element_type=F32)
                + jnp.dot(hi, wg_s[half:, :], preferred_element_type=F32))
        up = (jnp.dot(lo, wu_s[:half, :], preferred_element_type=F32)
              + jnp.dot(hi, wu_s[half:, :], preferred_element_type=F32))
        y = jnp.dot((_silu(gate) * up).astype(BF16), wd_s[...], preferred_element_type=F32)
        y_ref[...] = _pack_rows(y)


def _expert_blocks(layer, xs, block_expert, blocks_used, wg, wu, wdn):
    n_slots = xs.shape[0]
    wspec = lambda shape: pl.BlockSpec((1, 1) + shape, lambda b, be, used: (layer, be[b], 0, 0))
    rows = lambda b, be, used: (jnp.minimum(b, used[0] - 1), 0)
    grid_spec = pltpu.PrefetchScalarGridSpec(
        num_scalar_prefetch=2,
        grid=(n_slots // SLOT_BLOCK,),
        in_specs=[pl.BlockSpec((SLOT_BLOCK, D_MODEL // 2), rows),
                  wspec((D_MODEL, D_EXPERT)), wspec((D_MODEL, D_EXPERT)), wspec((D_EXPERT, D_MODEL))],
        out_specs=pl.BlockSpec((SLOT_BLOCK, D_MODEL // 2), rows),
        scratch_shapes=[pltpu.VMEM((D_MODEL, D_EXPERT), BF16), pltpu.VMEM((D_MODEL, D_EXPERT), BF16),
                        pltpu.VMEM((D_EXPERT, D_MODEL), BF16)])
    return pl.pallas_call(
        _expert_kernel, grid_spec=grid_spec,
        out_shape=jax.ShapeDtypeStruct((n_slots, D_MODEL // 2), jnp.uint32),
        compiler_params=_cparams(("arbitrary",)),
        name="moe_experts",
    )(block_expert, blocks_used, xs, wg, wu, wdn)


def _moe_out_kernel(g_ref, wk_ref, h_ref, x1_ref, mod_ref, npost_ref, sg_ref, su_ref, sd_ref, o_ref):
    h = h_ref[...]
    act = _silu(jnp.dot(h, sg_ref[...], preferred_element_type=F32)) * jnp.dot(h, su_ref[...],
                                                                              preferred_element_type=F32)
    moe = jnp.dot(act.astype(BF16), sd_ref[...], preferred_element_type=F32)
    wk = wk_ref[...]
    lo, hi = 0.0, 0.0
    for k in range(TOP_K):
        g_lo, g_hi = _unpack_rows(g_ref[k])
        lo = lo + g_lo * wk[:, k:k + 1]
        hi = hi + g_hi * wk[:, k:k + 1]
    moe = moe + jnp.concatenate([lo, hi], axis=-1)
    m = mod_ref[0]
    o_ref[...] = x1_ref[...] + m[5:6] * (_rms(moe) * npost_ref[...])


def _moe_out_stage(g, wk, h2, x1, mod, npost, sg, su, sd, st):
    t_all = h2.shape[0]
    row = lambda i: (i, 0)
    full = lambda a: pl.BlockSpec(a.shape, lambda i: (0, 0))
    return pl.pallas_call(
        _moe_out_kernel,
        grid=(t_all // TM,),
        in_specs=[pl.BlockSpec((TOP_K, TM, D_MODEL // 2), lambda i: (0, i, 0)), pl.BlockSpec((TM, LANES), row),
                  pl.BlockSpec((TM, D_MODEL), row), pl.BlockSpec((TM, D_MODEL), row), _mod_spec(st),
                  full(npost), full(sg), full(su), full(sd)],
        out_specs=pl.BlockSpec((TM, D_MODEL), row),
        out_shape=jax.ShapeDtypeStruct((t_all, D_MODEL), F32),
        compiler_params=_cparams(("parallel",)),
        name="moe_combine_" + st["name"],
    )(g, wk, h2, x1, mod, npost, sg, su, sd)


def _moe_stage(layer, h2, hpk, eidx, rank, wk, counts, x1, mod, npost, wg, wu, wdn, sg, su, sd, st):
    t_all = h2.shape[0]
    n_blocks = -(-(t_all * TOP_K + N_EXPERTS * (SLOT_BLOCK - 1)) // SLOT_BLOCK)
    cnt = counts[:, 0].astype(jnp.int32)
    padded = (cnt + SLOT_BLOCK - 1) // SLOT_BLOCK * SLOT_BLOCK
    pad_end = jnp.cumsum(padded)
    pad_start = pad_end - padded
    experts = jnp.arange(N_EXPERTS, dtype=jnp.int32)[:, None, None]
    dest = rank + jnp.sum(jnp.where(eidx[None] == experts, pad_start[:, None, None], 0), axis=0)
    dest = dest.reshape(TOP_K * t_all)
    block_start = jnp.arange(n_blocks, dtype=jnp.int32) * SLOT_BLOCK
    block_expert = jnp.minimum(jnp.sum((pad_end[None, :] <= block_start[:, None]).astype(jnp.int32), axis=1),
                               N_EXPERTS - 1)
    blocks_used = (pad_end[-1:] // SLOT_BLOCK).astype(jnp.int32)

    xs = _sc_dispatch(hpk, dest, n_blocks * SLOT_BLOCK)
    yb = _expert_blocks(layer, xs, block_expert, blocks_used, wg, wu, wdn)
    g = _sc_combine(yb, dest, t_all)
    return _moe_out_stage(g, wk, h2, x1, mod, npost, sg, su, sd, st)


_PERM32 = np.concatenate([np.arange(8, 16), np.arange(0, 8), np.arange(24, 32), np.arange(16, 24)])


def _rope_tables(n):
    pos = jnp.arange(n)
    quarter = QK_ROPE // 4
    inv = 1.0 / (ROPE_THETA ** (jnp.arange(quarter, dtype=F32) / quarter))
    ang_r = (pos // GRID_W).astype(F32)[:, None] * inv[None, :]
    ang_c = (pos % GRID_W).astype(F32)[:, None] * inv[None, :]
    cos32 = jnp.concatenate([jnp.cos(ang_r)] * 2 + [jnp.cos(ang_c)] * 2, axis=-1)
    sin32 = jnp.concatenate([-jnp.sin(ang_r), jnp.sin(ang_r), -jnp.sin(ang_c), jnp.sin(ang_c)], axis=-1)
    cos256, sin256 = jnp.tile(cos32, (1, 8)), jnp.tile(sin32, (1, 8))
    one, zero = jnp.ones((n, QK_NOPE), F32), jnp.zeros((n, QK_NOPE), F32)
    cosq = jnp.concatenate([one, cos32] * N_HEADS, axis=-1)
    sinq = jnp.concatenate([zero, sin32] * N_HEADS, axis=-1)
    return cos256, sin256, cosq, sinq


def _split_w_in(w_in_l):
    gw = GROUP_WIDTH
    sizes = (gw, gw, gw, gw, gw, gw, gw, 4 * N_HEADS, gw, gw, gw, gw, Q_LORA, KV_LORA, QK_ROPE)
    idx = np.cumsum(sizes)[:-1]
    return jnp.split(w_in_l, [int(v) for v in idx], axis=-1)


def _layer_weights(w_in_l, w_uq_l):
    (a_q, a_k, a_v, m_q, m_k, m_v, m_o, m_g, r_q, r_k, r_v, r_g, d_cq, d_ckv, d_kr) = _split_w_in(w_in_l)
    perm256 = np.concatenate([_PERM32 + 32 * j for j in range(8)])
    pad = lambda w, n: jnp.pad(w, ((0, 0), (0, n - w.shape[1])))
    tail = pad(jnp.concatenate([d_kr, m_g], axis=-1), LANES)
    tail_s = pad(d_kr[:, _PERM32], LANES)
    w_all = jnp.concatenate([a_q, a_k, a_v, m_q, m_k, m_v, m_o, r_q, r_k, r_v, r_g, d_cq, d_ckv, tail,
                             a_q[:, perm256], a_k[:, perm256], tail_s], axis=-1).astype(BF16)
    per = QK_NOPE + QK_ROPE
    permq = np.concatenate([np.concatenate([np.arange(QK_NOPE), QK_NOPE + _PERM32]) + per * j
                            for j in range(N_HEADS)])
    return w_all, w_uq_l.astype(BF16), w_uq_l[:, permq].astype(BF16)


def kernel(x_prompt, x_sample, cache_diff_k, cache_diff_v, state_mlstm_C, state_mlstm_n, state_mlstm_m, state_ret_S, cache_mla_ckv, cache_mla_krope, c, c_ctx, w_mod, b_mod, norm_pre, norm_post, w_in, w_out, diff_lambda, diff_norm, mlstm_gate_bias, mlstm_norm, ret_decay_logit, mla_q_norm, mla_w_uq, mla_kv_norm, mla_w_ukv, moe_w_router, moe_router_bias, moe_w_gate, moe_w_up, moe_w_down, shared_w_gate, shared_w_up, shared_w_down):
    bp, n_p, _ = x_prompt.shape
    bs, n_s, _ = x_sample.shape
    depth = w_in.shape[0]
    past = cache_diff_k.shape[2]
    assert (bp * n_p) % TM == 0 and n_s % TM == 0 and n_p % CHUNK == 0 and n_s % min(TK, n_s) == 0
    assert past % min(TK, past) == 0 and bs + 1 <= 8 and n_s % GRID_W == 0

    streams = (dict(name="ctx", nb=bp, n=n_p, latent=False), dict(name="latent", nb=bs, n=n_s, latent=True))
    xs_by_stream = [x_prompt.reshape(bp * n_p, D_MODEL), x_sample.reshape(bs * n_s, D_MODEL)]
    cond = jnp.zeros((8, D_MODEL), F32).at[0].set(c_ctx).at[1:1 + bs].set(c)
    mod_all = _modulation(cond, w_mod, b_mod).reshape(depth, 8, 6, D_MODEL)
    tabs = _rope_tables(n_s)
    scan_tabs = [_scan_table(st) for st in streams]

    outs = [[] for _ in range(8)]
    for l in range(depth):
        lam_init = 0.8 - 0.6 * math.exp(-0.3 * l)
        mod = mod_all[l]
        w_all, wuq, wuqs = _layer_weights(w_in[l], mla_w_uq[l])
        wukv = mla_w_ukv[l].astype(BF16)
        wout, wrt, rbias = w_out[l].astype(BF16), moe_w_router[l].T.astype(BF16), moe_router_bias[l][:, None]
        shared = [w[l].astype(BF16) for w in (shared_w_gate, shared_w_up, shared_w_down)]
        lamv, dg = diff_lambda[l], diff_norm[l][:, None]
        bias_row = jnp.zeros((1, LANES), F32).at[0, TAIL_GATE0:TAIL_GATE0 + 16].set(mlstm_gate_bias[l].reshape(16))

        with_ones = lambda vt: jnp.concatenate([vt, jnp.ones((N_HEADS, VT_ONES, bs * past), BF16)], axis=1)
        ck = jnp.transpose(cache_diff_k[:, l], (2, 0, 1, 3)).reshape(N_HEADS, bs * past, HEAD_DIM).astype(BF16)
        cvt = with_ones(jnp.transpose(cache_diff_v[:, l], (2, 3, 0, 1)).reshape(N_HEADS, HEAD_DIM, bs * past)
                        .astype(BF16))
        kvc = _project(cache_mla_ckv[:, l].reshape(bs * past, KV_LORA), wukv, BF16)
        kvc = kvc.reshape(bs * past, N_HEADS, QK_NOPE + V_HEAD)
        krc = jnp.broadcast_to(cache_mla_krope[:, l].reshape(bs * past, 1, QK_ROPE).astype(BF16),
                               (bs * past, N_HEADS, QK_ROPE))
        kc = jnp.transpose(jnp.concatenate([kvc[..., :QK_NOPE], krc], axis=-1), (1, 0, 2))
        vct = with_ones(jnp.transpose(kvc[..., QK_NOPE:], (1, 2, 0)))

        for si, st in enumerate(streams):
            x, nb, n = xs_by_stream[si], st["nb"], st["n"]
            (aq1t, aq2t, ak1, ak2, avt, ak, av, mqkv, mo, rqkv, rg, tail, qmlat, ckv, kmla, vmlat) = _input_stage(
                x, mod, norm_pre[l, 0:1], w_all, tabs, mla_q_norm[l][None], wuq, wuqs,
                mla_kv_norm[l][None], wukv, st)
            if st["latent"]:
                oa = _attention_t("diff_attn_latent", [aq1t, aq2t], [ak1, ak2], avt,
                                  [ck[..., :DIFF_HALF], ck[..., DIFF_HALF:]], cvt, [lamv, dg], nb, n, past, lam_init)
                od = _attention_t("mla_attn_latent", [qmlat], [kmla], vmlat, [kc], vct, [], nb, n, past)
                states = [s[:, l] for s in (state_mlstm_C, state_mlstm_n, state_mlstm_m, state_ret_S)]
            else:
                oa = _attention_t("diff_attn_ctx", [aq1t, aq2t], [ak1, ak2], avt, None, None, [lamv, dg],
                                  nb, n, 0, lam_init)
                od = _attention_t("mla_attn_ctx", [qmlat], [kmla], vmlat, None, None, [], nb, n, 0)
                states = [jnp.zeros((nb,) + s.shape[2:], F32)
                          for s in (state_mlstm_C, state_mlstm_n, state_mlstm_m, state_ret_S)]
            hf, hb, of, ob, c_n, n_n, m_n, s_n = _scans(scan_tabs[si], mqkv, tail, bias_row, rqkv,
                                                        ret_decay_logit[l], *states)

            x1, h2, hpk, eidx, rank, wk, counts = _output_stage(
                oa, od, hf, hb, mo, of, ob, rg, x, mod, mlstm_norm[l][None], norm_post[l, 0:1],
                norm_pre[l, 1:2], wout, wrt, rbias, st)
            xs_by_stream[si] = _moe_stage(l, h2, hpk, eidx, rank, wk, counts, x1, mod, norm_post[l, 1:2],
                                          moe_w_gate, moe_w_up, moe_w_down, *shared, st)
            if not st["latent"]:
                new = (ak.reshape(nb, n, N_HEADS, HEAD_DIM), av.reshape(nb, n, N_HEADS, HEAD_DIM), c_n, n_n, m_n,
                       s_n, ckv.reshape(nb, n, KV_LORA), tail[:, :QK_ROPE].reshape(nb, n, QK_ROPE))
                for o, v in zip(outs, new):
                    o.append(v)

    return (xs_by_stream[0].reshape(bp, n_p, D_MODEL), xs_by_stream[1].reshape(bs, n_s, D_MODEL)) + tuple(
        jnp.stack(o, axis=1) for o in outs)
```

```python
import functools
import math

import numpy as np
import jax
import jax.numpy as jnp
from jax import lax
from jax.experimental import pallas as pl
from jax.experimental.pallas import tpu as pltpu
from jax.experimental.pallas import tpu_sc as plsc

F32 = jnp.float32
BF16 = jnp.bfloat16
HIGHEST = lax.Precision.HIGHEST

D_MODEL = 1024
GRID_W = 64
GROUP_WIDTH = 256
HEAD_DIM = 64
N_HEADS = 4
DIFF_HALF = 32
ROPE_THETA = 10000.0
Q_LORA = 256
KV_LORA = 128
QK_NOPE = 64
QK_ROPE = 32
V_HEAD = 64
N_EXPERTS = 64
TOP_K = 8
N_GROUPS = 8
TOPK_GROUPS = 4
D_EXPERT = 256
ROUTED_SCALE = 2.5
CHUNK = 128
EPS = 1e-6
Q_MLA = N_HEADS * (QK_NOPE + QK_ROPE)

LANES = 128
VMEM_LIMIT = 56 * 1024 * 1024

TM = 512
TQ = 256
TK = 4096

C_AQ, C_AK, C_AV = 0, 256, 512
C_MQKV, C_MO = 768, 1536
C_RQKV, C_RG = 1792, 2560
C_CQ, C_CKV, C_TAIL = 2816, 3072, 3200
C_AQS, C_AKS, C_TAILS = 3328, 3584, 3840
TAIL_GATE0 = QK_ROPE
QKV_SLOTS = 3 * N_HEADS * LANES
VT_ONES = 16
VT_ROWS = V_HEAD + VT_ONES


def _cparams(sem, flags=None):
    return pltpu.CompilerParams(dimension_semantics=sem, vmem_limit_bytes=VMEM_LIMIT, flags=flags)


def _rms(x):
    return x * lax.rsqrt(jnp.mean(x * x, axis=-1, keepdims=True) + EPS)


def _head_mean_matrix(width):
    r = lax.broadcasted_iota(jnp.int32, (width, width), 0) // HEAD_DIM
    c = lax.broadcasted_iota(jnp.int32, (width, width), 1) // HEAD_DIM
    return jnp.where(r == c, 1.0 / HEAD_DIM, 0.0).astype(BF16)


def _rms_heads(x):
    sq = x * x
    hi = sq.astype(BF16)
    lo = (sq - hi.astype(F32)).astype(BF16)
    g = _head_mean_matrix(x.shape[-1])
    ms = jnp.dot(hi, g, preferred_element_type=F32) + jnp.dot(lo, g, preferred_element_type=F32)
    return x * lax.rsqrt(ms + EPS)


def _sigmoid(x):
    return 1.0 / (1.0 + jnp.exp(-x))


def _silu(x):
    return x * _sigmoid(x)


def _log_sigmoid(x):
    return jnp.minimum(x, 0.0) - jnp.log1p(jnp.exp(-jnp.abs(x)))


def _bdot(a, b):
    return jnp.dot(a.astype(BF16), b.astype(BF16), preferred_element_type=F32)


def _mod_kernel(c_ref, w_ref, b_ref, o_ref):
    o_ref[0] = _bdot(_silu(c_ref[...]), w_ref[0]) + b_ref[0]


def _modulation(cond, w_mod, b_mod):
    depth, _, n = w_mod.shape
    tn = 1536
    return pl.pallas_call(
        _mod_kernel,
        grid=(depth, n // tn),
        in_specs=[pl.BlockSpec((8, D_MODEL), lambda l, j: (0, 0)),
                  pl.BlockSpec((1, D_MODEL, tn), lambda l, j: (l, 0, j)),
                  pl.BlockSpec((1, 1, tn), lambda l, j: (l, 0, j))],
        out_specs=pl.BlockSpec((1, 8, tn), lambda l, j: (l, 0, j)),
        out_shape=jax.ShapeDtypeStruct((depth, 8, n), F32),
        compiler_params=_cparams(("parallel", "parallel")),
        name="adaln_mod",
    )(cond, w_mod, b_mod.reshape(depth, 1, n))


def _in_kernel(x_ref, mod_ref, npre_ref, w_ref, cos_ref, sin_ref, cosq_ref, sinq_ref,
               qg_ref, wuq_ref, wuqs_ref, kvg_ref, wukv_ref,
               aq1_ref, aq2_ref, ak1t_ref, ak2t_ref, avh_ref, ak_ref, av_ref,
               mqkv_ref, mo_ref, rqkv_ref, rg_ref, tail_ref,
               qmla_ref, ckv_ref, kmlat_ref, vmla_ref, *, latent):
    m = mod_ref[0]
    h = (_rms(x_ref[...]) * npre_ref[...] * (1.0 + m[1:2]) + m[0:1]).astype(BF16)

    def proj(c0, width):
        return jnp.dot(h, w_ref[:, c0:c0 + width], preferred_element_type=F32)

    def rotated(c0, c0_swapped, width, cos, sin):
        return proj(c0, width) * cos + proj(c0_swapped, width) * sin if latent else proj(c0, width)

    cos = cos_ref[...]
    sin = sin_ref[...]
    aq = rotated(C_AQ, C_AQS, GROUP_WIDTH, cos, sin) * (DIFF_HALF ** -0.5)
    ak = rotated(C_AK, C_AKS, GROUP_WIDTH, cos, sin)
    av = proj(C_AV, GROUP_WIDTH)
    ak_ref[...] = ak
    av_ref[...] = av
    aq_t, av_t = aq.T.astype(BF16), av.T.astype(BF16)
    ones = jnp.ones((VT_ONES, aq_t.shape[1]), BF16)
    for hd in range(N_HEADS):
        lo = hd * HEAD_DIM
        aq1_ref[hd] = aq_t[lo:lo + DIFF_HALF, :]
        aq2_ref[hd] = aq_t[lo + DIFF_HALF:lo + HEAD_DIM, :]
        ak1t_ref[hd] = ak[:, lo:lo + DIFF_HALF].astype(BF16)
        ak2t_ref[hd] = ak[:, lo + DIFF_HALF:lo + HEAD_DIM].astype(BF16)
        avh_ref[hd, 0:HEAD_DIM, :] = av_t[lo:lo + HEAD_DIM, :]
        avh_ref[hd, HEAD_DIM:VT_ROWS, :] = ones

    def head_slots(x):
        x = x.astype(BF16)
        zero = jnp.zeros((x.shape[0], LANES - HEAD_DIM), BF16)
        parts = []
        for j in range(3 * N_HEADS):
            parts += [x[:, j * HEAD_DIM:(j + 1) * HEAD_DIM], zero]
        return jnp.concatenate(parts, axis=-1)

    mqkv_ref[...] = head_slots(proj(C_MQKV, 3 * GROUP_WIDTH))
    mo_ref[...] = proj(C_MO, GROUP_WIDTH)
    rqkv_ref[...] = head_slots(proj(C_RQKV, 3 * GROUP_WIDTH))
    rg_ref[...] = proj(C_RG, GROUP_WIDTH)

    tail = proj(C_TAIL, LANES)
    tail_ref[...] = tail
    kr = tail * cos[:, :LANES] + proj(C_TAILS, LANES) * sin[:, :LANES] if latent else tail
    kr = kr[:, :QK_ROPE].astype(BF16)

    cqn = (_rms(proj(C_CQ, Q_LORA)) * qg_ref[...]).astype(BF16)
    qmla = jnp.dot(cqn, wuq_ref[...], preferred_element_type=F32)
    if latent:
        qmla = qmla * cosq_ref[...] + jnp.dot(cqn, wuqs_ref[...], preferred_element_type=F32) * sinq_ref[...]
    qmla_t = (qmla * ((QK_NOPE + QK_ROPE) ** -0.5)).T.astype(BF16)
    for hd in range(N_HEADS):
        lo = hd * (QK_NOPE + QK_ROPE)
        qmla_ref[hd] = qmla_t[lo:lo + QK_NOPE + QK_ROPE, :]
    ckvn = _rms(proj(C_CKV, KV_LORA)) * kvg_ref[...]
    ckv_ref[...] = ckvn
    kv = jnp.dot(ckvn.astype(BF16), wukv_ref[...], preferred_element_type=F32)
    kv_t = kv.T.astype(BF16)
    per = QK_NOPE + V_HEAD
    for hd in range(N_HEADS):
        kmlat_ref[hd, :, 0:QK_NOPE] = kv[:, hd * per:hd * per + QK_NOPE].astype(BF16)
        kmlat_ref[hd, :, QK_NOPE:QK_NOPE + QK_ROPE] = kr
        vmla_ref[hd, 0:V_HEAD, :] = kv_t[hd * per + QK_NOPE:(hd + 1) * per, :]
        vmla_ref[hd, V_HEAD:VT_ROWS, :] = ones


def _mod_spec(st, tm=TM):
    tps = st["n"] // tm
    return pl.BlockSpec((1, 6, D_MODEL), lambda i, *_: (1 + i // tps if st["latent"] else 0, 0, 0))


def _input_stage(x, mod, npre, w_all, tabs, qg, wuq, wuqs, kvg, wukv, st):
    t_all = x.shape[0]
    tm = TM
    tps = st["n"] // tm
    row = lambda i: (i, 0)
    tab = lambda w: pl.BlockSpec((tm, w), lambda i: (i % tps if st["latent"] else 0, 0))
    hrow = lambda i: (0, i, 0)
    const2 = lambda i: (0, 0)
    tok = lambda w: pl.BlockSpec((tm, w), row)
    headed = lambda w: pl.BlockSpec((N_HEADS, tm, w), hrow)
    headed_t = lambda d: pl.BlockSpec((N_HEADS, d, tm), lambda i: (0, 0, i))
    full = lambda a: pl.BlockSpec(a.shape, const2)
    cos, sin, cosq, sinq = tabs
    out_shapes = [
        (headed_t(DIFF_HALF), (N_HEADS, DIFF_HALF, t_all), BF16),
        (headed_t(DIFF_HALF), (N_HEADS, DIFF_HALF, t_all), BF16),
        (headed(DIFF_HALF), (N_HEADS, t_all, DIFF_HALF), BF16),
        (headed(DIFF_HALF), (N_HEADS, t_all, DIFF_HALF), BF16),
        (headed_t(VT_ROWS), (N_HEADS, VT_ROWS, t_all), BF16),
        (tok(GROUP_WIDTH), (t_all, GROUP_WIDTH), F32),
        (tok(GROUP_WIDTH), (t_all, GROUP_WIDTH), F32),
        (tok(QKV_SLOTS), (t_all, QKV_SLOTS), BF16),
        (tok(GROUP_WIDTH), (t_all, GROUP_WIDTH), F32),
        (tok(QKV_SLOTS), (t_all, QKV_SLOTS), BF16),
        (tok(GROUP_WIDTH), (t_all, GROUP_WIDTH), F32),
        (tok(LANES), (t_all, LANES), F32),
        (headed_t(QK_NOPE + QK_ROPE), (N_HEADS, QK_NOPE + QK_ROPE, t_all), BF16),
        (tok(KV_LORA), (t_all, KV_LORA), F32),
        (headed(QK_NOPE + QK_ROPE), (N_HEADS, t_all, QK_NOPE + QK_ROPE), BF16),
        (headed_t(VT_ROWS), (N_HEADS, VT_ROWS, t_all), BF16),
    ]
    return pl.pallas_call(
        functools.partial(_in_kernel, latent=st["latent"]),
        grid=(t_all // tm,),
        in_specs=[tok(D_MODEL), _mod_spec(st, tm),
                  full(npre), full(w_all), tab(GROUP_WIDTH), tab(GROUP_WIDTH), tab(Q_MLA), tab(Q_MLA),
                  full(qg), full(wuq), full(wuqs), full(kvg), full(wukv)],
        out_specs=[s for s, _, _ in out_shapes],
        out_shape=[jax.ShapeDtypeStruct(shp, dt) for _, shp, dt in out_shapes],
        compiler_params=_cparams(("parallel",)),
        name="input_stage_" + st["name"],
    )(x, mod, npre, w_all, cos, sin, cosq, sinq, qg, wuq, wuqs, kvg, wukv)


def _attn_t_kernel(*refs, n_soft, n_new, n_ctx, lam_init):
    refs = list(refs)
    qt_refs, k_refs, vt_ref = refs[:n_soft], refs[n_soft:2 * n_soft], refs[2 * n_soft]
    pos = 2 * n_soft + 1
    if n_ctx:
        ck_refs, cvt_ref = refs[pos:pos + n_soft], refs[pos + n_soft]
        pos += n_soft + 1
    if n_soft == 2:
        lam_ref, g_ref = refs[pos:pos + 2]
        pos += 2
    o_ref = refs[pos]
    tq = o_ref.shape[0]
    nchain = n_soft * N_HEADS
    order = [(j, hd) for hd in range(N_HEADS) for j in range(n_soft)]

    def chunk(state, k_of, vt_of):
        ms, accs = state
        new_m, new_acc = list(ms), list(accs)
        ss = [jnp.dot(k_of(j, hd), qt_refs[j][hd], preferred_element_type=F32) for j, hd in order]
        ps, alphas = [], []
        for (j, hd), s in zip(order, ss):
            c = j * N_HEADS + hd
            s3 = s.reshape(s.shape[0] // 8, 8, tq)
            top = jnp.max(jnp.max(s3, axis=0), axis=0, keepdims=True)
            m_new = jnp.maximum(ms[c], jnp.broadcast_to(top, (8, tq)))
            ps.append(jnp.exp(s3 - m_new[None]).reshape(s.shape).astype(BF16))
            alphas.append(jnp.exp(ms[c] - m_new))
            new_m[c] = m_new
        for (j, hd), p, alpha in zip(order, ps, alphas):
            c = j * N_HEADS + hd
            scaled = (accs[c].reshape(VT_ROWS // 8, 8, tq) * alpha[None]).reshape(VT_ROWS, tq)
            new_acc[c] = scaled + jnp.dot(vt_of(hd), p, preferred_element_type=F32)
        return tuple(new_m), tuple(new_acc)

    state = (tuple(jnp.full((8, tq), -jnp.inf, F32) for _ in range(nchain)),
             tuple(jnp.zeros((VT_ROWS, tq), F32) for _ in range(nchain)))
    if n_ctx:
        cstep = min(TK, n_ctx)
        for i in range(n_ctx // cstep):
            state = chunk(state, lambda j, hd, i=i: ck_refs[j][hd, i * cstep:(i + 1) * cstep, :],
                          lambda hd, i=i: cvt_ref[hd, :, i * cstep:(i + 1) * cstep])
    step = min(TK, n_new)

    def body(i, st):
        start = pl.multiple_of(i * step, step)
        return chunk(st, lambda j, hd: k_refs[j][hd, pl.ds(start, step), :],
                     lambda hd: vt_ref[hd, :, pl.ds(start, step)])

    _, accs = lax.fori_loop(0, n_new // step, body, state)

    def normalised(c):
        num, den = accs[c][:V_HEAD], accs[c][V_HEAD:V_HEAD + 8]
        return (num.reshape(V_HEAD // 8, 8, tq) / den[None]).reshape(V_HEAD, tq)

    if n_soft == 2:
        lv = lam_ref[...]
        lam = (jnp.exp(jnp.sum(lv[0:1] * lv[1:2], axis=-1, keepdims=True))
               - jnp.exp(jnp.sum(lv[2:3] * lv[3:4], axis=-1, keepdims=True)) + lam_init)
    outs = []
    for hd in range(N_HEADS):
        out = normalised(hd)
        if n_soft == 2:
            a = out - lam * normalised(N_HEADS + hd)
            out = a * lax.rsqrt(jnp.mean(a * a, axis=0, keepdims=True) + EPS) * g_ref[...] * (1.0 - lam_init)
        outs.append(out)
    o_ref[...] = jnp.concatenate(outs, axis=0).T


def _attention_t(name, qts, ks, vt, ctx_ks, ctx_vt, extras, nb, n, n_ctx, lam_init=0.0):
    tq = min(TQ, n)
    nqt = n // tq
    n_soft = len(qts)
    in_specs = [pl.BlockSpec((N_HEADS, a.shape[1], tq), lambda b, i: (0, 0, b * nqt + i)) for a in qts]
    in_specs += [pl.BlockSpec((N_HEADS, n, a.shape[-1]), lambda b, i: (0, b, 0)) for a in ks]
    in_specs += [pl.BlockSpec((N_HEADS, VT_ROWS, n), lambda b, i: (0, 0, b))]
    args = list(qts) + list(ks) + [vt]
    if n_ctx:
        in_specs += [pl.BlockSpec((N_HEADS, n_ctx, a.shape[-1]), lambda b, i: (0, b, 0)) for a in ctx_ks]
        in_specs += [pl.BlockSpec((N_HEADS, VT_ROWS, n_ctx), lambda b, i: (0, 0, b))]
        args += list(ctx_ks) + [ctx_vt]
    in_specs += [pl.BlockSpec(a.shape, lambda b, i: (0, 0)) for a in extras]
    args += list(extras)
    return pl.pallas_call(
        functools.partial(_attn_t_kernel, n_soft=n_soft, n_new=n, n_ctx=n_ctx, lam_init=lam_init),
        grid=(nb, nqt),
        in_specs=in_specs,
        out_specs=pl.BlockSpec((tq, N_HEADS * V_HEAD), lambda b, i: (b * nqt + i, 0)),
        out_shape=jax.ShapeDtypeStruct((nb * n, N_HEADS * V_HEAD), F32),
        compiler_params=_cparams(("parallel", "parallel")),
        name=name,
    )(*args)


def _tri(lower):
    r = lax.broadcasted_iota(jnp.int32, (CHUNK, CHUNK), 0)
    c = lax.broadcasted_iota(jnp.int32, (CHUNK, CHUNK), 1)
    return (c <= r) if lower else (c >= r)


def _nt(a, b):
    return lax.dot_general(a, b, (((1,), (1,)), ((), ())), preferred_element_type=F32)


def _tn(a, b):
    return lax.dot_general(a, b, (((0,), (0,)), ((), ())), preferred_element_type=F32)


def _head_slices(qkv, hd):
    return tuple(qkv[:, (j * N_HEADS + hd) * LANES:(j * N_HEADS + hd) * LANES + HEAD_DIM] for j in range(3))


def _row_dot(row, mat, transpose_mat):
    row8 = jnp.broadcast_to(row, (8, row.shape[1])).astype(BF16)
    dims = (((1,), (1,)), ((), ())) if transpose_mat else (((1,), (0,)), ((), ()))
    return lax.dot_general(row8, mat, dims, preferred_element_type=F32)[0:1]


def _load_state_t(dst, src):
    for d in range(2):
        for hd in range(N_HEADS):
            dst[d, hd] = src[d, hd].T


def _mlstm_step(dirs, bias_ref):
    k_scale = HEAD_DIM ** -0.5
    lower = _tri(True).astype(F32)
    upper = _tri(False).astype(F32)
    chains = []
    for d, qkv_ref, tail_ref, _, ct_s, n_s, m_s in dirs:
        g = tail_ref[...] + bias_ref[...]
        ls = _log_sigmoid(g)
        g_t, ls_t = g.T, ls.T
        left, right = (lower, upper) if d == 0 else (upper, lower)
        cum_col = jnp.dot(left, ls, precision=HIGHEST, preferred_element_type=F32)
        cum_row = jnp.dot(ls_t, right, precision=HIGHEST, preferred_element_type=F32)
        qkv = qkv_ref[...]
        for hd in range(N_HEADS):
            ci = TAIL_GATE0 + 4 * (2 * d) + hd
            cf = ci + 4
            q, k, v = _head_slices(qkv, hd)
            chains.append(dict(
                d=d, hd=hd, q=q, k=k, v=v, li_row=g_t[ci:ci + 1, :], b_row=cum_row[cf:cf + 1, :],
                c_col=g[:, ci:ci + 1] - cum_col[:, cf:cf + 1],
                m_prev=m_s[d:d + 1, hd:hd + 1], ct=ct_s[d, hd], n=n_s[d, hd:hd + 1, :],
                refs=(ct_s, n_s, m_s)))
    for c in chains:
        c["s"] = _nt(c["k"], c["q"])
        c["v_t"] = c["v"].T
    for c in chains:
        valid = _tri(c["d"] != 0)
        log_d = jnp.where(valid, c["b_row"] + c["c_col"], -jnp.inf)
        log_inter = c["b_row"] + c["m_prev"]
        c["m_t"] = jnp.maximum(log_inter, jnp.max(log_d, axis=0, keepdims=True))
        c["w"] = c["s"] * k_scale * jnp.exp(log_d - c["m_t"])
        c["w_inter"] = jnp.exp(log_inter - c["m_t"])
    for c in chains:
        num = (jnp.dot(c["v_t"], c["w"].astype(BF16), preferred_element_type=F32)
               + c["w_inter"] * _nt(c["ct"].astype(BF16), c["q"]))
        den = jnp.sum(c["w"], axis=0, keepdims=True) + c["w_inter"] * _row_dot(c["n"], c["q"], True)
        c["h"] = num / jnp.maximum(jnp.abs(den), jnp.exp(-c["m_t"]))
    for c in chains:
        last = CHUNK - 1 if c["d"] == 0 else 0
        c["m_new"] = c["m_t"][:, last:last + 1]
        b_last = c["b_row"][:, last:last + 1]
        w_end_row = jnp.exp(b_last - c["b_row"] + c["li_row"] - c["m_new"])
        w_end_col = jnp.exp(c["c_col"] + (b_last - c["m_new"]))
        decay = jnp.exp(b_last + c["m_prev"] - c["m_new"])
        kw = (c["k"].astype(F32) * (w_end_col * k_scale)).astype(BF16)
        c["ct_new"] = decay * c["ct"] + jnp.dot(c["v_t"], kw, preferred_element_type=F32)
        c["n_new"] = decay * c["n"] + _row_dot(w_end_row, c["k"], False) * k_scale
    for j, entry in enumerate(dirs):
        entry[3][...] = jnp.concatenate([c["h"] for c in chains[j * N_HEADS:(j + 1) * N_HEADS]], axis=0).T
    for c in chains:
        d, hd = c["d"], c["hd"]
        ct_s, n_s, m_s = c["refs"]
        ct_s[d, hd] = c["ct_new"]
        n_s[d, hd:hd + 1, :] = c["n_new"]
        m_s[d:d + 1, hd:hd + 1] = c["m_new"]


def _ret_step(dirs, lg_ref):
    k_scale = HEAD_DIM ** -0.5
    s = lax.broadcasted_iota(jnp.int32, (CHUNK, CHUNK), 0)
    t = lax.broadcasted_iota(jnp.int32, (CHUNK, CHUNK), 1)
    lane = lax.broadcasted_iota(jnp.int32, (1, CHUNK), 1)
    chains = []
    for d, qkv_ref, _, st_s in dirs:
        lag = (t - s) if d == 0 else (s - t)
        pos = (lane if d == 0 else CHUNK - 1 - lane).astype(F32)
        qkv = qkv_ref[...]
        for hd in range(N_HEADS):
            q, k, v = _head_slices(qkv, hd)
            chains.append(dict(d=d, hd=hd, q=q, k=k, v=v, lag=lag, pos=pos, st=st_s[d, hd], ref=st_s,
                               lg=_log_sigmoid(lg_ref[d:d + 1, hd:hd + 1])))
    for c in chains:
        c["s"] = _nt(c["k"], c["q"])
        c["v_t"] = c["v"].T
    for c in chains:
        intra = jnp.where(c["lag"] >= 0, jnp.exp(jnp.maximum(c["lag"], 0).astype(F32) * c["lg"]), 0.0)
        c["a"] = (c["s"] * k_scale * intra).astype(BF16)
    for c in chains:
        inter = jnp.exp((c["pos"] + 1.0) * c["lg"])
        c["o"] = jnp.dot(c["v_t"], c["a"], preferred_element_type=F32) + inter * _nt(c["st"].astype(BF16), c["q"])
    row = lax.broadcasted_iota(jnp.int32, (CHUNK, HEAD_DIM), 0)
    for c in chains:
        src_pos = (row if c["d"] == 0 else CHUNK - 1 - row).astype(F32)
        tail = jnp.exp((CHUNK - 1.0 - src_pos) * c["lg"]) * k_scale
        kw = (c["k"].astype(F32) * tail).astype(BF16)
        c["st_new"] = jnp.exp(CHUNK * c["lg"]) * c["st"] + jnp.dot(c["v_t"], kw, preferred_element_type=F32)
    for j, entry in enumerate(dirs):
        entry[2][...] = jnp.concatenate([c["o"] for c in chains[j * N_HEADS:(j + 1) * N_HEADS]], axis=0).T
    for c in chains:
        c["ref"][c["d"], c["hd"]] = c["st_new"]


def _scan_kernel(tab_ref, mqf_ref, tailf_ref, mqb_ref, tailb_ref, rqf_ref, rqb_ref, bias_ref, lg_ref,
                 c0_ref, n0_ref, m0_ref, s0_ref,
                 hf_ref, hb_ref, of_ref, ob_ref, cn_ref, nn_ref, mn_ref, sn_ref, ct_s, n_s, m_s, st_s):
    i = pl.program_id(0)
    seqs = range(ct_s.shape[0])

    @pl.when(tab_ref[3, i] == 1)
    def _():
        for p in seqs:
            _load_state_t(ct_s.at[p], c0_ref.at[p])
            _load_state_t(st_s.at[p], s0_ref.at[p])
        n_s[...] = n0_ref[...]
        m_s[...] = m0_ref[...]

    mdirs, rdirs = [], []
    for p in seqs:
        state = (ct_s.at[p], n_s.at[p], m_s.at[p])
        mdirs += [(0, mqf_ref.at[p], tailf_ref.at[p], hf_ref.at[p]) + state,
                  (1, mqb_ref.at[p], tailb_ref.at[p], hb_ref.at[p]) + state]
        rdirs += [(0, rqf_ref.at[p], of_ref.at[p], st_s.at[p]), (1, rqb_ref.at[p], ob_ref.at[p], st_s.at[p])]
    _mlstm_step(mdirs, bias_ref)
    _ret_step(rdirs, lg_ref)

    @pl.when(tab_ref[4, i] == 1)
    def _():
        for p in seqs:
            _load_state_t(cn_ref.at[p], ct_s.at[p])
            _load_state_t(sn_ref.at[p], st_s.at[p])
        nn_ref[...] = n_s[...]
        mn_ref[...] = m_s[...]


def _scan_table(st):
    nch = st["n"] // CHUNK
    rows = [(c, nch - 1 - c, g, int(c == 0), int(c == nch - 1))
            for g in range(st["nb"] // _scan_group(st)) for c in range(nch)]
    return jnp.asarray(np.array(rows, dtype=np.int32).T)


def _scan_group(st):
    return 2 if st["nb"] % 2 == 0 else 1


def _scans(tab, mqkv, tail, bias_row, rqkv, decay_logit, c0, n0, m0, s0, st):
    nb, n, grp = st["nb"], st["n"], _scan_group(st)
    seq3 = lambda a: a.reshape(nb, n, a.shape[-1])
    fwd = lambda i, tab: (tab[2, i], tab[0, i], 0)
    bwd = lambda i, tab: (tab[2, i], tab[1, i], 0)
    const = lambda i, tab: (0, 0)
    rows = lambda width, where: pl.BlockSpec((grp, CHUNK, width), where)
    state = lambda a: pl.BlockSpec((grp,) + a.shape[1:], lambda i, tab: (tab[2, i],) + (0,) * (a.ndim - 1))
    states = (c0, n0, m0, s0)
    grid_spec = pltpu.PrefetchScalarGridSpec(
        num_scalar_prefetch=1,
        grid=(tab.shape[1],),
        in_specs=[rows(QKV_SLOTS, fwd), rows(LANES, fwd), rows(QKV_SLOTS, bwd), rows(LANES, bwd),
                  rows(QKV_SLOTS, fwd), rows(QKV_SLOTS, bwd),
                  pl.BlockSpec((1, LANES), const), pl.BlockSpec(decay_logit.shape, const)]
        + [state(a) for a in states],
        out_specs=[rows(GROUP_WIDTH, fwd), rows(GROUP_WIDTH, bwd), rows(GROUP_WIDTH, fwd), rows(GROUP_WIDTH, bwd)]
        + [state(a) for a in states],
        scratch_shapes=[pltpu.VMEM((grp,) + a.shape[1:], F32) for a in states],
    )
    outs = pl.pallas_call(
        _scan_kernel,
        grid_spec=grid_spec,
        out_shape=[jax.ShapeDtypeStruct((nb, n, GROUP_WIDTH), F32)] * 4
        + [jax.ShapeDtypeStruct(a.shape, F32) for a in states],
        compiler_params=_cparams(("arbitrary",)),
        name="recurrent_scans",
    )(tab, seq3(mqkv), seq3(tail), seq3(mqkv), seq3(tail), seq3(rqkv), seq3(rqkv), bias_row, decay_logit, *states)
    return [o.reshape(nb * n, GROUP_WIDTH) for o in outs[:4]] + list(outs[4:])


def _proj_kernel(x_ref, w_ref, o_ref):
    o_ref[...] = _bdot(x_ref[...], w_ref[...]).astype(o_ref.dtype)


def _project(x, w, dtype):
    return pl.pallas_call(
        _proj_kernel,
        out_shape=jax.ShapeDtypeStruct((x.shape[0], w.shape[1]), dtype),
        name="ctx_kv_proj",
    )(x, w)


def _first_argmax_mask(cur, axis, size):
    io = lax.broadcasted_iota(jnp.int32, cur.shape, axis)
    mx = jnp.max(cur, axis=axis, keepdims=True)
    ix = jnp.min(jnp.where(cur == mx, io, size), axis=axis, keepdims=True)
    return io == ix


def _route(scores_t, bias_col):
    tm = scores_t.shape[1]
    per = N_EXPERTS // N_GROUPS
    sel = scores_t + bias_col
    s3 = sel.reshape(N_GROUPS, per, tm)
    hit1 = _first_argmax_mask(s3, 1, per)
    m1 = jnp.max(s3, axis=1, keepdims=True)
    m2 = jnp.max(jnp.where(hit1, -jnp.inf, s3), axis=1, keepdims=True)
    cur = m1 + m2
    gsel = None
    for _ in range(TOPK_GROUPS):
        hit = _first_argmax_mask(cur, 0, N_GROUPS)
        gsel = hit if gsel is None else jnp.logical_or(gsel, hit)
        cur = jnp.where(hit, -jnp.inf, cur)
    cur = jnp.where(gsel, s3, -jnp.inf).reshape(N_EXPERTS, tm)
    chosen, hits = None, []
    for _ in range(TOP_K):
        hit = _first_argmax_mask(cur, 0, N_EXPERTS)
        hits.append(hit)
        chosen = hit if chosen is None else jnp.logical_or(chosen, hit)
        cur = jnp.where(hit, -jnp.inf, cur)
    w = jnp.where(chosen, scores_t, 0.0)
    return w / jnp.sum(w, axis=0, keepdims=True) * ROUTED_SCALE, chosen, hits


def _out_kernel(oa_ref, od_ref, hf_ref, hb_ref, mo_ref, of_ref, ob_ref, rg_ref,
                x_ref, mod_ref, mg_ref, npost_ref, npre_ref, wout_ref, wrt_ref, rb_ref,
                x1_ref, h2_ref, hpk_ref, eidx_ref, rank_ref, wk_ref, cnt_ref, count_s):
    @pl.when(pl.program_id(0) == 0)
    def _():
        count_s[...] = jnp.zeros(count_s.shape, F32)

    a = oa_ref[...]
    dd = od_ref[...]
    b = _rms_heads(hf_ref[...] + hb_ref[...]) * mg_ref[...] * _sigmoid(mo_ref[...])
    r = _rms_heads(of_ref[...] + ob_ref[...]) * _silu(rg_ref[...])
    mix = sum(jnp.dot(part.astype(BF16), wout_ref[j * GROUP_WIDTH:(j + 1) * GROUP_WIDTH, :],
                      preferred_element_type=F32) for j, part in enumerate((a, b, r, dd)))
    m = mod_ref[0]
    x1 = x_ref[...] + m[2:3] * (_rms(mix) * npost_ref[...])
    x1_ref[...] = x1
    h2 = (_rms(x1) * npre_ref[...] * (1.0 + m[4:5]) + m[3:4]).astype(BF16)
    h2_ref[...] = h2
    hpk_ref[...] = _pack_rows(h2)

    logits_t = lax.dot_general(wrt_ref[...], h2, (((1,), (1,)), ((), ())), preferred_element_type=F32)
    w_t, chosen, hits = _route(_sigmoid(logits_t), rb_ref[...])
    tm = w_t.shape[1]
    src = lax.broadcasted_iota(jnp.int32, (tm, tm), 0)
    dst = lax.broadcasted_iota(jnp.int32, (tm, tm), 1)
    before = jnp.where(src < dst, 1.0, 0.0).astype(BF16)
    picked = jnp.where(chosen, 1.0, 0.0)
    rank = jnp.dot(picked.astype(BF16), before, preferred_element_type=F32) + count_s[...]
    count_s[...] += jnp.sum(picked, axis=1, keepdims=True)
    cnt_ref[...] = jnp.broadcast_to(count_s[...], cnt_ref.shape)
    e_io = lax.broadcasted_iota(jnp.int32, w_t.shape, 0)
    pick = lambda hit, val: jnp.sum(jnp.where(hit, val, jnp.zeros_like(val)), axis=0, keepdims=True)
    eidx_ref[...] = jnp.concatenate([pick(h, e_io) for h in hits], axis=0)
    rank_ref[...] = jnp.concatenate([pick(h, rank) for h in hits], axis=0).astype(jnp.int32)
    wk = jnp.concatenate([pick(h, w_t) for h in hits] + [jnp.zeros((LANES - TOP_K, tm), F32)], axis=0)
    wk_ref[...] = wk.T


def _output_stage(oa, od, hf, hb, mo, of, ob, rg, x, mod, mg, npost, npre, wout, wrt, rb, st):
    t_all = x.shape[0]
    row = lambda i: (i, 0)
    const2 = lambda i: (0, 0)
    tok = lambda w: pl.BlockSpec((TM, w), row)
    full = lambda a: pl.BlockSpec(a.shape, const2)
    return pl.pallas_call(
        _out_kernel,
        grid=(t_all // TM,),
        in_specs=[tok(GROUP_WIDTH)] * 8 + [tok(D_MODEL), _mod_spec(st),
                                   full(mg), full(npost), full(npre), full(wout), full(wrt), full(rb)],
        out_specs=[tok(D_MODEL), tok(D_MODEL), tok(D_MODEL // 2),
                   pl.BlockSpec((TOP_K, TM), lambda i: (0, i)), pl.BlockSpec((TOP_K, TM), lambda i: (0, i)),
                   tok(LANES), pl.BlockSpec((N_EXPERTS, LANES), const2)],
        out_shape=[jax.ShapeDtypeStruct((t_all, D_MODEL), F32),
                   jax.ShapeDtypeStruct((t_all, D_MODEL), BF16),
                   jax.ShapeDtypeStruct((t_all, D_MODEL // 2), jnp.uint32),
                   jax.ShapeDtypeStruct((TOP_K, t_all), jnp.int32),
                   jax.ShapeDtypeStruct((TOP_K, t_all), jnp.int32),
                   jax.ShapeDtypeStruct((t_all, LANES), F32),
                   jax.ShapeDtypeStruct((N_EXPERTS, LANES), F32)],
        scratch_shapes=[pltpu.VMEM((N_EXPERTS, 1), F32)],
        compiler_params=_cparams(("arbitrary",)),
        name="output_stage_" + st["name"],
    )(oa, od, hf, hb, mo, of, ob, rg, x, mod, mg, npost, npre, wout, wrt, rb)


SC_CORES, SC_SUBCORES = 2, 16
SC_WORKERS = SC_CORES * SC_SUBCORES
SLOT_BLOCK = 512
DISPATCH_ROWS = 64
COMBINE_ROWS = 64


def _sc_mesh():
    return plsc.VectorSubcoreMesh(core_axis_name="core", subcore_axis_name="subcore")


def _sc_worker():
    return lax.axis_index("subcore") * SC_CORES + lax.axis_index("core")


def _sc_dispatch(hpk, dest, n_slots):
    t_all, width = hpk.shape
    per_worker = t_all // SC_WORKERS
    assert t_all % (SC_WORKERS * DISPATCH_ROWS) == 0

    @functools.partial(
        pl.kernel, mesh=_sc_mesh(), out_type=jax.ShapeDtypeStruct((n_slots, width), hpk.dtype),
        scratch_types=[pltpu.VMEM((DISPATCH_ROWS, width), hpk.dtype)]
        + [pltpu.VMEM((DISPATCH_ROWS,), jnp.int32)] * TOP_K + [pltpu.SemaphoreType.DMA])
    def dispatch(x_hbm, d_hbm, o_hbm, rows_v, *rest):
        idx, sem = rest[:TOP_K], rest[TOP_K]

        @pl.loop(0, per_worker // DISPATCH_ROWS)
        def _(j):
            base = _sc_worker() * per_worker + j * DISPATCH_ROWS
            pltpu.sync_copy(x_hbm.at[pl.ds(base, DISPATCH_ROWS)], rows_v)
            for k in range(TOP_K):
                pltpu.sync_copy(d_hbm.at[pl.ds(k * t_all + base, DISPATCH_ROWS)], idx[k])
            copies = [pltpu.async_copy(rows_v, o_hbm.at[idx[k]], sem) for k in range(TOP_K)]
            for c in copies:
                c.wait()

    return dispatch(hpk, dest)


def _sc_combine(yb, dest, t_all):
    width = yb.shape[1]
    per_worker = t_all // SC_WORKERS
    assert t_all % (SC_WORKERS * COMBINE_ROWS) == 0

    @functools.partial(
        pl.kernel, mesh=_sc_mesh(), out_type=jax.ShapeDtypeStruct((TOP_K, t_all, width), yb.dtype),
        scratch_types=[pltpu.VMEM((COMBINE_ROWS, width), yb.dtype)] * 2
        + [pltpu.VMEM((COMBINE_ROWS,), jnp.int32)] * TOP_K + [pltpu.SemaphoreType.DMA] * 2)
    def combine(y_hbm, d_hbm, o_hbm, rows_a, rows_b, *rest):
        idx, sems = rest[:TOP_K], rest[TOP_K:]
        bufs = (rows_a, rows_b)

        @pl.loop(0, per_worker // COMBINE_ROWS)
        def _(j):
            base = _sc_worker() * per_worker + j * COMBINE_ROWS
            for k in range(TOP_K):
                pltpu.sync_copy(d_hbm.at[pl.ds(k * t_all + base, COMBINE_ROWS)], idx[k])
            pending = pltpu.async_copy(y_hbm.at[idx[0]], bufs[0], sems[0])
            for k in range(TOP_K):
                pending.wait()
                if k + 1 < TOP_K:
                    pending = pltpu.async_copy(y_hbm.at[idx[k + 1]], bufs[(k + 1) % 2], sems[(k + 1) % 2])
                pltpu.sync_copy(bufs[k % 2], o_hbm.at[k, pl.ds(base, COMBINE_ROWS)])

    return combine(yb, dest)


def _pack_rows(x):
    bits = pltpu.bitcast(x.astype(BF16).astype(F32), jnp.uint32)
    w = x.shape[-1] // 2
    return (bits[..., :w] >> 16) | (bits[..., w:] & jnp.uint32(0xFFFF0000))


def _unpack_rows(words):
    return pltpu.bitcast(words << 16, F32), pltpu.bitcast(words & jnp.uint32(0xFFFF0000), F32)


def _expert_kernel(be_ref, used_ref, xs_ref, wg_ref, wu_ref, wd_ref, y_ref, wg_s, wu_s, wd_s):
    b = pl.program_id(0)
    fresh = jnp.logical_or(b == 0, be_ref[b] != be_ref[jnp.maximum(b - 1, 0)])

    @pl.when(jnp.logical_and(fresh, b < used_ref[0]))
    def _():
        wg_s[...] = wg_ref[0, 0].astype(BF16)
        wu_s[...] = wu_ref[0, 0].astype(BF16)
        wd_s[...] = wd_ref[0, 0].astype(BF16)

    @pl.when(b < used_ref[0])
    def _():
        lo, hi = (v.astype(BF16) for v in _unpack_rows(xs_ref[...]))
        half = D_MODEL // 2
        gate = (jnp.dot(lo, wg_s[:half, :], preferred_element_type=F32)
                + jnp.dot(hi, wg_s[half:, :], preferred_element_type=F32))
        up = (jnp.dot(lo, wu_s[:half, :], preferred_element_type=F32)
              + jnp.dot(hi, wu_s[half:, :], preferred_element_type=F32))
        y = jnp.dot((_silu(gate) * up).astype(BF16), wd_s[...], preferred_element_type=F32)
        y_ref[...] = _pack_rows(y)


def _expert_blocks(layer, xs, block_expert, blocks_used, wg, wu, wdn):
    n_slots = xs.shape[0]
    wspec = lambda shape: pl.BlockSpec((1, 1) + shape, lambda b, be, used: (layer, be[b], 0, 0))
    rows = lambda b, be, used: (jnp.minimum(b, used[0] - 1), 0)
    grid_spec = pltpu.PrefetchScalarGridSpec(
        num_scalar_prefetch=2,
        grid=(n_slots // SLOT_BLOCK,),
        in_specs=[pl.BlockSpec((SLOT_BLOCK, D_MODEL // 2), rows),
                  wspec((D_MODEL, D_EXPERT)), wspec((D_MODEL, D_EXPERT)), wspec((D_EXPERT, D_MODEL))],
        out_specs=pl.BlockSpec((SLOT_BLOCK, D_MODEL // 2), rows),
        scratch_shapes=[pltpu.VMEM((D_MODEL, D_EXPERT), BF16), pltpu.VMEM((D_MODEL, D_EXPERT), BF16),
                        pltpu.VMEM((D_EXPERT, D_MODEL), BF16)])
    return pl.pallas_call(
        _expert_kernel, grid_spec=grid_spec,
        out_shape=jax.ShapeDtypeStruct((n_slots, D_MODEL // 2), jnp.uint32),
        compiler_params=_cparams(("arbitrary",)),
        name="moe_experts",
    )(block_expert, blocks_used, xs, wg, wu, wdn)


def _moe_out_kernel(g_ref, wk_ref, h_ref, x1_ref, mod_ref, npost_ref, sg_ref, su_ref, sd_ref, o_ref):
    h = h_ref[...]
    act = _silu(jnp.dot(h, sg_ref[...], preferred_element_type=F32)) * jnp.dot(h, su_ref[...],
                                                                              preferred_element_type=F32)
    moe = jnp.dot(act.astype(BF16), sd_ref[...], preferred_element_type=F32)
    wk = wk_ref[...]
    lo, hi = 0.0, 0.0
    for k in range(TOP_K):
        g_lo, g_hi = _unpack_rows(g_ref[k])
        lo = lo + g_lo * wk[:, k:k + 1]
        hi = hi + g_hi * wk[:, k:k + 1]
    moe = moe + jnp.concatenate([lo, hi], axis=-1)
    m = mod_ref[0]
    o_ref[...] = x1_ref[...] + m[5:6] * (_rms(moe) * npost_ref[...])


def _moe_out_stage(g, wk, h2, x1, mod, npost, sg, su, sd, st):
    t_all = h2.shape[0]
    row = lambda i: (i, 0)
    full = lambda a: pl.BlockSpec(a.shape, lambda i: (0, 0))
    return pl.pallas_call(
        _moe_out_kernel,
        grid=(t_all // TM,),
        in_specs=[pl.BlockSpec((TOP_K, TM, D_MODEL // 2), lambda i: (0, i, 0)), pl.BlockSpec((TM, LANES), row),
                  pl.BlockSpec((TM, D_MODEL), row), pl.BlockSpec((TM, D_MODEL), row), _mod_spec(st),
                  full(npost), full(sg), full(su), full(sd)],
        out_specs=pl.BlockSpec((TM, D_MODEL), row),
        out_shape=jax.ShapeDtypeStruct((t_all, D_MODEL), F32),
        compiler_params=_cparams(("parallel",)),
        name="moe_combine_" + st["name"],
    )(g, wk, h2, x1, mod, npost, sg, su, sd)


def _moe_stage(layer, h2, hpk, eidx, rank, wk, counts, x1, mod, npost, wg, wu, wdn, sg, su, sd, st):
    t_all = h2.shape[0]
    n_blocks = -(-(t_all * TOP_K + N_EXPERTS * (SLOT_BLOCK - 1)) // SLOT_BLOCK)
    cnt = counts[:, 0].astype(jnp.int32)
    padded = (cnt + SLOT_BLOCK - 1) // SLOT_BLOCK * SLOT_BLOCK
    pad_end = jnp.cumsum(padded)
    pad_start = pad_end - padded
    experts = jnp.arange(N_EXPERTS, dtype=jnp.int32)[:, None, None]
    dest = rank + jnp.sum(jnp.where(eidx[None] == experts, pad_start[:, None, None], 0), axis=0)
    dest = dest.reshape(TOP_K * t_all)
    block_start = jnp.arange(n_blocks, dtype=jnp.int32) * SLOT_BLOCK
    block_expert = jnp.minimum(jnp.sum((pad_end[None, :] <= block_start[:, None]).astype(jnp.int32), axis=1),
                               N_EXPERTS - 1)
    blocks_used = (pad_end[-1:] // SLOT_BLOCK).astype(jnp.int32)

    xs = _sc_dispatch(hpk, dest, n_blocks * SLOT_BLOCK)
    yb = _expert_blocks(layer, xs, block_expert, blocks_used, wg, wu, wdn)
    g = _sc_combine(yb, dest, t_all)
    return _moe_out_stage(g, wk, h2, x1, mod, npost, sg, su, sd, st)


_PERM32 = np.concatenate([np.arange(8, 16), np.arange(0, 8), np.arange(24, 32), np.arange(16, 24)])


def _rope_tables(n):
    pos = jnp.arange(n)
    quarter = QK_ROPE // 4
    inv = 1.0 / (ROPE_THETA ** (jnp.arange(quarter, dtype=F32) / quarter))
    ang_r = (pos // GRID_W).astype(F32)[:, None] * inv[None, :]
    ang_c = (pos % GRID_W).astype(F32)[:, None] * inv[None, :]
    cos32 = jnp.concatenate([jnp.cos(ang_r)] * 2 + [jnp.cos(ang_c)] * 2, axis=-1)
    sin32 = jnp.concatenate([-jnp.sin(ang_r), jnp.sin(ang_r), -jnp.sin(ang_c), jnp.sin(ang_c)], axis=-1)
    cos256, sin256 = jnp.tile(cos32, (1, 8)), jnp.tile(sin32, (1, 8))
    one, zero = jnp.ones((n, QK_NOPE), F32), jnp.zeros((n, QK_NOPE), F32)
    cosq = jnp.concatenate([one, cos32] * N_HEADS, axis=-1)
    sinq = jnp.concatenate([zero, sin32] * N_HEADS, axis=-1)
    return cos256, sin256, cosq, sinq


def _split_w_in(w_in_l):
    gw = GROUP_WIDTH
    sizes = (gw, gw, gw, gw, gw, gw, gw, 4 * N_HEADS, gw, gw, gw, gw, Q_LORA, KV_LORA, QK_ROPE)
    idx = np.cumsum(sizes)[:-1]
    return jnp.split(w_in_l, [int(v) for v in idx], axis=-1)


def _layer_weights(w_in_l, w_uq_l):
    (a_q, a_k, a_v, m_q, m_k, m_v, m_o, m_g, r_q, r_k, r_v, r_g, d_cq, d_ckv, d_kr) = _split_w_in(w_in_l)
    perm256 = np.concatenate([_PERM32 + 32 * j for j in range(8)])
    pad = lambda w, n: jnp.pad(w, ((0, 0), (0, n - w.shape[1])))
    tail = pad(jnp.concatenate([d_kr, m_g], axis=-1), LANES)
    tail_s = pad(d_kr[:, _PERM32], LANES)
    w_all = jnp.concatenate([a_q, a_k, a_v, m_q, m_k, m_v, m_o, r_q, r_k, r_v, r_g, d_cq, d_ckv, tail,
                             a_q[:, perm256], a_k[:, perm256], tail_s], axis=-1).astype(BF16)
    per = QK_NOPE + QK_ROPE
    permq = np.concatenate([np.concatenate([np.arange(QK_NOPE), QK_NOPE + _PERM32]) + per * j
                            for j in range(N_HEADS)])
    return w_all, w_uq_l.astype(BF16), w_uq_l[:, permq].astype(BF16)


def kernel(x_prompt, x_sample, cache_diff_k, cache_diff_v, state_mlstm_C, state_mlstm_n, state_mlstm_m, state_ret_S, cache_mla_ckv, cache_mla_krope, c, c_ctx, w_mod, b_mod, norm_pre, norm_post, w_in, w_out, diff_lambda, diff_norm, mlstm_gate_bias, mlstm_norm, ret_decay_logit, mla_q_norm, mla_w_uq, mla_kv_norm, mla_w_ukv, moe_w_router, moe_router_bias, moe_w_gate, moe_w_up, moe_w_down, shared_w_gate, shared_w_up, shared_w_down):
    bp, n_p, _ = x_prompt.shape
    bs, n_s, _ = x_sample.shape
    depth = w_in.shape[0]
    past = cache_diff_k.shape[2]
    assert (bp * n_p) % TM == 0 and n_s % TM == 0 and n_p % CHUNK == 0 and n_s % min(TK, n_s) == 0
    assert past % min(TK, past) == 0 and bs + 1 <= 8 and n_s % GRID_W == 0

    streams = (dict(name="ctx", nb=bp, n=n_p, latent=False), dict(name="latent", nb=bs, n=n_s, latent=True))
    xs_by_stream = [x_prompt.reshape(bp * n_p, D_MODEL), x_sample.reshape(bs * n_s, D_MODEL)]
    cond = jnp.zeros((8, D_MODEL), F32).at[0].set(c_ctx).at[1:1 + bs].set(c)
    mod_all = _modulation(cond, w_mod, b_mod).reshape(depth, 8, 6, D_MODEL)
    tabs = _rope_tables(n_s)
    scan_tabs = [_scan_table(st) for st in streams]

    outs = [[] for _ in range(8)]
    for l in range(depth):
        lam_init = 0.8 - 0.6 * math.exp(-0.3 * l)
        mod = mod_all[l]
        w_all, wuq, wuqs = _layer_weights(w_in[l], mla_w_uq[l])
        wukv = mla_w_ukv[l].astype(BF16)
        wout, wrt, rbias = w_out[l].astype(BF16), moe_w_router[l].T.astype(BF16), moe_router_bias[l][:, None]
        shared = [w[l].astype(BF16) for w in (shared_w_gate, shared_w_up, shared_w_down)]
        lamv, dg = diff_lambda[l], diff_norm[l][:, None]
        bias_row = jnp.zeros((1, LANES), F32).at[0, TAIL_GATE0:TAIL_GATE0 + 16].set(mlstm_gate_bias[l].reshape(16))

        with_ones = lambda vt: jnp.concatenate([vt, jnp.ones((N_HEADS, VT_ONES, bs * past), BF16)], axis=1)
        ck = jnp.transpose(cache_diff_k[:, l], (2, 0, 1, 3)).reshape(N_HEADS, bs * past, HEAD_DIM).astype(BF16)
        cvt = with_ones(jnp.transpose(cache_diff_v[:, l], (2, 3, 0, 1)).reshape(N_HEADS, HEAD_DIM, bs * past)
                        .astype(BF16))
        kvc = _project(cache_mla_ckv[:, l].reshape(bs * past, KV_LORA), wukv, BF16)
        kvc = kvc.reshape(bs * past, N_HEADS, QK_NOPE + V_HEAD)
        krc = jnp.broadcast_to(cache_mla_krope[:, l].reshape(bs * past, 1, QK_ROPE).astype(BF16),
                               (bs * past, N_HEADS, QK_ROPE))
        kc = jnp.transpose(jnp.concatenate([kvc[..., :QK_NOPE], krc], axis=-1), (1, 0, 2))
        vct = with_ones(jnp.transpose(kvc[..., QK_NOPE:], (1, 2, 0)))

        for si, st in enumerate(streams):
            x, nb, n = xs_by_stream[si], st["nb"], st["n"]
            (aq1t, aq2t, ak1, ak2, avt, ak, av, mqkv, mo, rqkv, rg, tail, qmlat, ckv, kmla, vmlat) = _input_stage(
                x, mod, norm_pre[l, 0:1], w_all, tabs, mla_q_norm[l][None], wuq, wuqs,
                mla_kv_norm[l][None], wukv, st)
            if st["latent"]:
                oa = _attention_t("diff_attn_latent", [aq1t, aq2t], [ak1, ak2], avt,
                                  [ck[..., :DIFF_HALF], ck[..., DIFF_HALF:]], cvt, [lamv, dg], nb, n, past, lam_init)
                od = _attention_t("mla_attn_latent", [qmlat], [kmla], vmlat, [kc], vct, [], nb, n, past)
                states = [s[:, l] for s in (state_mlstm_C, state_mlstm_n, state_mlstm_m, state_ret_S)]
            else:
                oa = _attention_t("diff_attn_ctx", [aq1t, aq2t], [ak1, ak2], avt, None, None, [lamv, dg],
                                  nb, n, 0, lam_init)
                od = _attention_t("mla_attn_ctx", [qmlat], [kmla], vmlat, None, None, [], nb, n, 0)
                states = [jnp.zeros((nb,) + s.shape[2:], F32)
                          for s in (state_mlstm_C, state_mlstm_n, state_mlstm_m, state_ret_S)]
            hf, hb, of, ob, c_n, n_n, m_n, s_n = _scans(scan_tabs[si], mqkv, tail, bias_row, rqkv,
                                                        ret_decay_logit[l], *states, st)

            x1, h2, hpk, eidx, rank, wk, counts = _output_stage(
                oa, od, hf, hb, mo, of, ob, rg, x, mod, mlstm_norm[l][None], norm_post[l, 0:1],
                norm_pre[l, 1:2], wout, wrt, rbias, st)
            xs_by_stream[si] = _moe_stage(l, h2, hpk, eidx, rank, wk, counts, x1, mod, norm_post[l, 1:2],
                                          moe_w_gate, moe_w_up, moe_w_down, *shared, st)
            if not st["latent"]:
                new = (ak.reshape(nb, n, N_HEADS, HEAD_DIM), av.reshape(nb, n, N_HEADS, HEAD_DIM), c_n, n_n, m_n,
                       s_n, ckv.reshape(nb, n, KV_LORA), tail[:, :QK_ROPE].reshape(nb, n, QK_ROPE))
                for o, v in zip(outs, new):
                    o.append(v)

    return (xs_by_stream[0].reshape(bp, n_p, D_MODEL), xs_by_stream[1].reshape(bs, n_s, D_MODEL)) + tuple(
        jnp.stack(o, axis=1) for o in outs)
```

```python
import functools
import math

import numpy as np
import jax
import jax.numpy as jnp
from jax import lax
from jax.experimental import pallas as pl
from jax.experimental.pallas import tpu as pltpu
from jax.experimental.pallas import tpu_sc as plsc

F32 = jnp.float32
BF16 = jnp.bfloat16
HIGHEST = lax.Precision.HIGHEST

D_MODEL = 1024
GRID_W = 64
GROUP_WIDTH = 256
HEAD_DIM = 64
N_HEADS = 4
DIFF_HALF = 32
ROPE_THETA = 10000.0
Q_LORA = 256
KV_LORA = 128
QK_NOPE = 64
QK_ROPE = 32
V_HEAD = 64
N_EXPERTS = 64
TOP_K = 8
N_GROUPS = 8
TOPK_GROUPS = 4
D_EXPERT = 256
ROUTED_SCALE = 2.5
CHUNK = 128
EPS = 1e-6
Q_MLA = N_HEADS * (QK_NOPE + QK_ROPE)

LANES = 128
VMEM_LIMIT = 56 * 1024 * 1024

TM = 512
TQ = 256
TK = 4096

C_AQ, C_AK, C_AV = 0, 256, 512
C_MQKV, C_MO = 768, 1536
C_RQKV, C_RG = 1792, 2560
C_CQ, C_CKV, C_TAIL = 2816, 3072, 3200
C_AQS, C_AKS, C_TAILS = 3328, 3584, 3840
TAIL_GATE0 = QK_ROPE
QKV_SLOTS = 3 * N_HEADS * LANES
VT_ONES = 16
VT_ROWS = V_HEAD + VT_ONES


def _cparams(sem, flags=None):
    return pltpu.CompilerParams(dimension_semantics=sem, vmem_limit_bytes=VMEM_LIMIT, flags=flags)


def _rms(x):
    return x * lax.rsqrt(jnp.mean(x * x, axis=-1, keepdims=True) + EPS)


def _head_mean_matrix(width):
    r = lax.broadcasted_iota(jnp.int32, (width, width), 0) // HEAD_DIM
    c = lax.broadcasted_iota(jnp.int32, (width, width), 1) // HEAD_DIM
    return jnp.where(r == c, 1.0 / HEAD_DIM, 0.0).astype(BF16)


def _rms_heads(x):
    sq = x * x
    hi = sq.astype(BF16)
    lo = (sq - hi.astype(F32)).astype(BF16)
    g = _head_mean_matrix(x.shape[-1])
    ms = jnp.dot(hi, g, preferred_element_type=F32) + jnp.dot(lo, g, preferred_element_type=F32)
    return x * lax.rsqrt(ms + EPS)


def _sigmoid(x):
    return 1.0 / (1.0 + jnp.exp(-x))


def _silu(x):
    return x * _sigmoid(x)


def _log_sigmoid(x):
    return jnp.minimum(x, 0.0) - jnp.log1p(jnp.exp(-jnp.abs(x)))


def _bdot(a, b):
    return jnp.dot(a.astype(BF16), b.astype(BF16), preferred_element_type=F32)


def _mod_kernel(c_ref, w_ref, b_ref, o_ref):
    o_ref[0] = _bdot(_silu(c_ref[...]), w_ref[0]) + b_ref[0]


def _modulation(cond, w_mod, b_mod):
    depth, _, n = w_mod.shape
    tn = 1536
    return pl.pallas_call(
        _mod_kernel,
        grid=(depth, n // tn),
        in_specs=[pl.BlockSpec((8, D_MODEL), lambda l, j: (0, 0)),
                  pl.BlockSpec((1, D_MODEL, tn), lambda l, j: (l, 0, j)),
                  pl.BlockSpec((1, 1, tn), lambda l, j: (l, 0, j))],
        out_specs=pl.BlockSpec((1, 8, tn), lambda l, j: (l, 0, j)),
        out_shape=jax.ShapeDtypeStruct((depth, 8, n), F32),
        compiler_params=_cparams(("parallel", "parallel")),
        name="adaln_mod",
    )(cond, w_mod, b_mod.reshape(depth, 1, n))


def _in_kernel(x_ref, mod_ref, npre_ref, w_ref, cos_ref, sin_ref, cosq_ref, sinq_ref,
               qg_ref, wuq_ref, wuqs_ref, kvg_ref, wukv_ref,
               aq1_ref, aq2_ref, ak1t_ref, ak2t_ref, avh_ref, ak_ref, av_ref,
               mqkv_ref, mo_ref, rqkv_ref, rg_ref, tail_ref,
               qmla_ref, ckv_ref, kmlat_ref, vmla_ref, *, latent):
    m = mod_ref[0]
    h = (_rms(x_ref[...]) * npre_ref[...] * (1.0 + m[1:2]) + m[0:1]).astype(BF16)

    def proj(c0, width):
        return jnp.dot(h, w_ref[:, c0:c0 + width], preferred_element_type=F32)

    def rotated(c0, c0_swapped, width, cos, sin):
        return proj(c0, width) * cos + proj(c0_swapped, width) * sin if latent else proj(c0, width)

    cos = cos_ref[...]
    sin = sin_ref[...]
    aq = rotated(C_AQ, C_AQS, GROUP_WIDTH, cos, sin) * (DIFF_HALF ** -0.5)
    ak = rotated(C_AK, C_AKS, GROUP_WIDTH, cos, sin)
    av = proj(C_AV, GROUP_WIDTH)
    ak_ref[...] = ak
    av_ref[...] = av
    aq_t, av_t = aq.T.astype(BF16), av.T.astype(BF16)
    ones = jnp.ones((VT_ONES, aq_t.shape[1]), BF16)
    for hd in range(N_HEADS):
        lo = hd * HEAD_DIM
        aq1_ref[hd] = aq_t[lo:lo + DIFF_HALF, :]
        aq2_ref[hd] = aq_t[lo + DIFF_HALF:lo + HEAD_DIM, :]
        ak1t_ref[hd] = ak[:, lo:lo + DIFF_HALF].astype(BF16)
        ak2t_ref[hd] = ak[:, lo + DIFF_HALF:lo + HEAD_DIM].astype(BF16)
        avh_ref[hd, 0:HEAD_DIM, :] = av_t[lo:lo + HEAD_DIM, :]
        avh_ref[hd, HEAD_DIM:VT_ROWS, :] = ones

    def head_slots(x):
        x = x.astype(BF16)
        zero = jnp.zeros((x.shape[0], LANES - HEAD_DIM), BF16)
        parts = []
        for j in range(3 * N_HEADS):
            parts += [x[:, j * HEAD_DIM:(j + 1) * HEAD_DIM], zero]
        return jnp.concatenate(parts, axis=-1)

    mqkv_ref[...] = head_slots(proj(C_MQKV, 3 * GROUP_WIDTH))
    mo_ref[...] = proj(C_MO, GROUP_WIDTH)
    rqkv_ref[...] = head_slots(proj(C_RQKV, 3 * GROUP_WIDTH))
    rg_ref[...] = proj(C_RG, GROUP_WIDTH)

    tail = proj(C_TAIL, LANES)
    tail_ref[...] = tail
    kr = tail * cos[:, :LANES] + proj(C_TAILS, LANES) * sin[:, :LANES] if latent else tail
    kr = kr[:, :QK_ROPE].astype(BF16)

    cqn = (_rms(proj(C_CQ, Q_LORA)) * qg_ref[...]).astype(BF16)
    qmla = jnp.dot(cqn, wuq_ref[...], preferred_element_type=F32)
    if latent:
        qmla = qmla * cosq_ref[...] + jnp.dot(cqn, wuqs_ref[...], preferred_element_type=F32) * sinq_ref[...]
    qmla_t = (qmla * ((QK_NOPE + QK_ROPE) ** -0.5)).T.astype(BF16)
    for hd in range(N_HEADS):
        lo = hd * (QK_NOPE + QK_ROPE)
        qmla_ref[hd] = qmla_t[lo:lo + QK_NOPE + QK_ROPE, :]
    ckvn = _rms(proj(C_CKV, KV_LORA)) * kvg_ref[...]
    ckv_ref[...] = ckvn
    kv = jnp.dot(ckvn.astype(BF16), wukv_ref[...], preferred_element_type=F32)
    kv_t = kv.T.astype(BF16)
    per = QK_NOPE + V_HEAD
    for hd in range(N_HEADS):
        kmlat_ref[hd, :, 0:QK_NOPE] = kv[:, hd * per:hd * per + QK_NOPE].astype(BF16)
        kmlat_ref[hd, :, QK_NOPE:QK_NOPE + QK_ROPE] = kr
        vmla_ref[hd, 0:V_HEAD, :] = kv_t[hd * per + QK_NOPE:(hd + 1) * per, :]
        vmla_ref[hd, V_HEAD:VT_ROWS, :] = ones


def _mod_spec(st, tm=TM):
    tps = st["n"] // tm
    return pl.BlockSpec((1, 6, D_MODEL), lambda i, *_: (1 + i // tps if st["latent"] else 0, 0, 0))


def _input_stage(x, mod, npre, w_all, tabs, qg, wuq, wuqs, kvg, wukv, st):
    t_all = x.shape[0]
    tm = TM
    tps = st["n"] // tm
    row = lambda i: (i, 0)
    tab = lambda w: pl.BlockSpec((tm, w), lambda i: (i % tps if st["latent"] else 0, 0))
    hrow = lambda i: (0, i, 0)
    const2 = lambda i: (0, 0)
    tok = lambda w: pl.BlockSpec((tm, w), row)
    headed = lambda w: pl.BlockSpec((N_HEADS, tm, w), hrow)
    headed_t = lambda d: pl.BlockSpec((N_HEADS, d, tm), lambda i: (0, 0, i))
    full = lambda a: pl.BlockSpec(a.shape, const2)
    cos, sin, cosq, sinq = tabs
    out_shapes = [
        (headed_t(DIFF_HALF), (N_HEADS, DIFF_HALF, t_all), BF16),
        (headed_t(DIFF_HALF), (N_HEADS, DIFF_HALF, t_all), BF16),
        (headed(DIFF_HALF), (N_HEADS, t_all, DIFF_HALF), BF16),
        (headed(DIFF_HALF), (N_HEADS, t_all, DIFF_HALF), BF16),
        (headed_t(VT_ROWS), (N_HEADS, VT_ROWS, t_all), BF16),
        (tok(GROUP_WIDTH), (t_all, GROUP_WIDTH), F32),
        (tok(GROUP_WIDTH), (t_all, GROUP_WIDTH), F32),
        (tok(QKV_SLOTS), (t_all, QKV_SLOTS), BF16),
        (tok(GROUP_WIDTH), (t_all, GROUP_WIDTH), F32),
        (tok(QKV_SLOTS), (t_all, QKV_SLOTS), BF16),
        (tok(GROUP_WIDTH), (t_all, GROUP_WIDTH), F32),
        (tok(LANES), (t_all, LANES), F32),
        (headed_t(QK_NOPE + QK_ROPE), (N_HEADS, QK_NOPE + QK_ROPE, t_all), BF16),
        (tok(KV_LORA), (t_all, KV_LORA), F32),
        (headed(QK_NOPE + QK_ROPE), (N_HEADS, t_all, QK_NOPE + QK_ROPE), BF16),
        (headed_t(VT_ROWS), (N_HEADS, VT_ROWS, t_all), BF16),
    ]
    return pl.pallas_call(
        functools.partial(_in_kernel, latent=st["latent"]),
        grid=(t_all // tm,),
        in_specs=[tok(D_MODEL), _mod_spec(st, tm),
                  full(npre), full(w_all), tab(GROUP_WIDTH), tab(GROUP_WIDTH), tab(Q_MLA), tab(Q_MLA),
                  full(qg), full(wuq), full(wuqs), full(kvg), full(wukv)],
        out_specs=[s for s, _, _ in out_shapes],
        out_shape=[jax.ShapeDtypeStruct(shp, dt) for _, shp, dt in out_shapes],
        compiler_params=_cparams(("parallel",)),
        name="input_stage_" + st["name"],
    )(x, mod, npre, w_all, cos, sin, cosq, sinq, qg, wuq, wuqs, kvg, wukv)


def _attn_t_kernel(*refs, n_soft, n_new, n_ctx, lam_init):
    refs = list(refs)
    qt_refs, k_refs, vt_ref = refs[:n_soft], refs[n_soft:2 * n_soft], refs[2 * n_soft]
    pos = 2 * n_soft + 1
    if n_ctx:
        ck_refs, cvt_ref = refs[pos:pos + n_soft], refs[pos + n_soft]
        pos += n_soft + 1
    if n_soft == 2:
        lam_ref, g_ref = refs[pos:pos + 2]
        pos += 2
    o_ref = refs[pos]
    tq = o_ref.shape[0]
    nchain = n_soft * N_HEADS
    order = [(j, hd) for hd in range(N_HEADS) for j in range(n_soft)]

    def chunk(state, k_of, vt_of):
        ms, accs = state
        new_m, new_acc = list(ms), list(accs)
        ss = [jnp.dot(k_of(j, hd), qt_refs[j][hd], preferred_element_type=F32) for j, hd in order]
        ps, alphas = [], []
        for (j, hd), s in zip(order, ss):
            c = j * N_HEADS + hd
            s3 = s.reshape(s.shape[0] // 8, 8, tq)
            top = jnp.max(jnp.max(s3, axis=0), axis=0, keepdims=True)
            m_new = jnp.maximum(ms[c], jnp.broadcast_to(top, (8, tq)))
            ps.append(jnp.exp(s3 - m_new[None]).reshape(s.shape).astype(BF16))
            alphas.append(jnp.exp(ms[c] - m_new))
            new_m[c] = m_new
        for (j, hd), p, alpha in zip(order, ps, alphas):
            c = j * N_HEADS + hd
            scaled = (accs[c].reshape(VT_ROWS // 8, 8, tq) * alpha[None]).reshape(VT_ROWS, tq)
            new_acc[c] = scaled + jnp.dot(vt_of(hd), p, preferred_element_type=F32)
        return tuple(new_m), tuple(new_acc)

    state = (tuple(jnp.full((8, tq), -jnp.inf, F32) for _ in range(nchain)),
             tuple(jnp.zeros((VT_ROWS, tq), F32) for _ in range(nchain)))
    if n_ctx:
        cstep = min(TK, n_ctx)
        for i in range(n_ctx // cstep):
            state = chunk(state, lambda j, hd, i=i: ck_refs[j][hd, i * cstep:(i + 1) * cstep, :],
                          lambda hd, i=i: cvt_ref[hd, :, i * cstep:(i + 1) * cstep])
    step = min(TK, n_new)

    def body(i, st):
        start = pl.multiple_of(i * step, step)
        return chunk(st, lambda j, hd: k_refs[j][hd, pl.ds(start, step), :],
                     lambda hd: vt_ref[hd, :, pl.ds(start, step)])

    _, accs = lax.fori_loop(0, n_new // step, body, state)

    def normalised(c):
        num, den = accs[c][:V_HEAD], accs[c][V_HEAD:V_HEAD + 8]
        return (num.reshape(V_HEAD // 8, 8, tq) / den[None]).reshape(V_HEAD, tq)

    if n_soft == 2:
        lv = lam_ref[...]
        lam = (jnp.exp(jnp.sum(lv[0:1] * lv[1:2], axis=-1, keepdims=True))
               - jnp.exp(jnp.sum(lv[2:3] * lv[3:4], axis=-1, keepdims=True)) + lam_init)
    outs = []
    for hd in range(N_HEADS):
        out = normalised(hd)
        if n_soft == 2:
            a = out - lam * normalised(N_HEADS + hd)
            out = a * lax.rsqrt(jnp.mean(a * a, axis=0, keepdims=True) + EPS) * g_ref[...] * (1.0 - lam_init)
        outs.append(out)
    o_ref[...] = jnp.concatenate(outs, axis=0).T


def _attention_t(name, qts, ks, vt, ctx_ks, ctx_vt, extras, nb, n, n_ctx, lam_init=0.0):
    tq = min(TQ, n)
    nqt = n // tq
    n_soft = len(qts)
    in_specs = [pl.BlockSpec((N_HEADS, a.shape[1], tq), lambda b, i: (0, 0, b * nqt + i)) for a in qts]
    in_specs += [pl.BlockSpec((N_HEADS, n, a.shape[-1]), lambda b, i: (0, b, 0)) for a in ks]
    in_specs += [pl.BlockSpec((N_HEADS, VT_ROWS, n), lambda b, i: (0, 0, b))]
    args = list(qts) + list(ks) + [vt]
    if n_ctx:
        in_specs += [pl.BlockSpec((N_HEADS, n_ctx, a.shape[-1]), lambda b, i: (0, b, 0)) for a in ctx_ks]
        in_specs += [pl.BlockSpec((N_HEADS, VT_ROWS, n_ctx), lambda b, i: (0, 0, b))]
        args += list(ctx_ks) + [ctx_vt]
    in_specs += [pl.BlockSpec(a.shape, lambda b, i: (0, 0)) for a in extras]
    args += list(extras)
    return pl.pallas_call(
        functools.partial(_attn_t_kernel, n_soft=n_soft, n_new=n, n_ctx=n_ctx, lam_init=lam_init),
        grid=(nb, nqt),
        in_specs=in_specs,
        out_specs=pl.BlockSpec((tq, N_HEADS * V_HEAD), lambda b, i: (b * nqt + i, 0)),
        out_shape=jax.ShapeDtypeStruct((nb * n, N_HEADS * V_HEAD), F32),
        compiler_params=_cparams(("parallel", "parallel")),
        name=name,
    )(*args)


def _tri(lower):
    r = lax.broadcasted_iota(jnp.int32, (CHUNK, CHUNK), 0)
    c = lax.broadcasted_iota(jnp.int32, (CHUNK, CHUNK), 1)
    return (c <= r) if lower else (c >= r)


def _nt(a, b):
    return lax.dot_general(a, b, (((1,), (1,)), ((), ())), preferred_element_type=F32)


def _tn(a, b):
    return lax.dot_general(a, b, (((0,), (0,)), ((), ())), preferred_element_type=F32)


def _head_slices(qkv, hd):
    return tuple(qkv[:, (j * N_HEADS + hd) * LANES:(j * N_HEADS + hd) * LANES + HEAD_DIM] for j in range(3))


def _row_dot(row, mat, transpose_mat):
    row8 = jnp.broadcast_to(row, (8, row.shape[1])).astype(BF16)
    dims = (((1,), (1,)), ((), ())) if transpose_mat else (((1,), (0,)), ((), ()))
    return lax.dot_general(row8, mat, dims, preferred_element_type=F32)[0:1]


def _load_state_t(dst, src):
    for d in range(2):
        for hd in range(N_HEADS):
            dst[d, hd] = src[d, hd].T


def _mlstm_step(dirs, bias_ref):
    k_scale = HEAD_DIM ** -0.5
    lower = _tri(True).astype(F32)
    upper = _tri(False).astype(F32)
    chains = []
    for d, qkv_ref, tail_ref, _, ct_s, n_s, m_s in dirs:
        g = tail_ref[...] + bias_ref[...]
        ls = _log_sigmoid(g)
        g_t, ls_t = g.T, ls.T
        left, right = (lower, upper) if d == 0 else (upper, lower)
        cum_col = jnp.dot(left, ls, precision=HIGHEST, preferred_element_type=F32)
        cum_row = jnp.dot(ls_t, right, precision=HIGHEST, preferred_element_type=F32)
        qkv = qkv_ref[...]
        for hd in range(N_HEADS):
            ci = TAIL_GATE0 + 4 * (2 * d) + hd
            cf = ci + 4
            q, k, v = _head_slices(qkv, hd)
            chains.append(dict(
                d=d, hd=hd, q=q, k=k, v=v, li_row=g_t[ci:ci + 1, :], b_row=cum_row[cf:cf + 1, :],
                c_col=g[:, ci:ci + 1] - cum_col[:, cf:cf + 1],
                m_prev=m_s[d:d + 1, hd:hd + 1], ct=ct_s[d, hd], n=n_s[d, hd:hd + 1, :],
                refs=(ct_s, n_s, m_s)))
    for c in chains:
        c["s"] = _nt(c["k"], c["q"])
        c["v_t"] = c["v"].T
    for c in chains:
        valid = _tri(c["d"] != 0)
        log_d = jnp.where(valid, c["b_row"] + c["c_col"], -jnp.inf)
        log_inter = c["b_row"] + c["m_prev"]
        c["m_t"] = jnp.maximum(log_inter, jnp.max(log_d, axis=0, keepdims=True))
        c["w"] = c["s"] * k_scale * jnp.exp(log_d - c["m_t"])
        c["w_inter"] = jnp.exp(log_inter - c["m_t"])
    for c in chains:
        num = (jnp.dot(c["v_t"], c["w"].astype(BF16), preferred_element_type=F32)
               + c["w_inter"] * _nt(c["ct"].astype(BF16), c["q"]))
        den = jnp.sum(c["w"], axis=0, keepdims=True) + c["w_inter"] * _row_dot(c["n"], c["q"], True)
        c["h"] = num / jnp.maximum(jnp.abs(den), jnp.exp(-c["m_t"]))
    for c in chains:
        last = CHUNK - 1 if c["d"] == 0 else 0
        c["m_new"] = c["m_t"][:, last:last + 1]
        b_last = c["b_row"][:, last:last + 1]
        w_end_row = jnp.exp(b_last - c["b_row"] + c["li_row"] - c["m_new"])
        w_end_col = jnp.exp(c["c_col"] + (b_last - c["m_new"]))
        decay = jnp.exp(b_last + c["m_prev"] - c["m_new"])
        kw = (c["k"].astype(F32) * (w_end_col * k_scale)).astype(BF16)
        c["ct_new"] = decay * c["ct"] + jnp.dot(c["v_t"], kw, preferred_element_type=F32)
        c["n_new"] = decay * c["n"] + _row_dot(w_end_row, c["k"], False) * k_scale
    for j, entry in enumerate(dirs):
        entry[3][...] = jnp.concatenate([c["h"] for c in chains[j * N_HEADS:(j + 1) * N_HEADS]], axis=0).T
    for c in chains:
        d, hd = c["d"], c["hd"]
        ct_s, n_s, m_s = c["refs"]
        ct_s[d, hd] = c["ct_new"]
        n_s[d, hd:hd + 1, :] = c["n_new"]
        m_s[d:d + 1, hd:hd + 1] = c["m_new"]


def _ret_step(dirs, lg_ref):
    k_scale = HEAD_DIM ** -0.5
    s = lax.broadcasted_iota(jnp.int32, (CHUNK, CHUNK), 0)
    t = lax.broadcasted_iota(jnp.int32, (CHUNK, CHUNK), 1)
    lane = lax.broadcasted_iota(jnp.int32, (1, CHUNK), 1)
    chains = []
    for d, qkv_ref, _, st_s in dirs:
        lag = (t - s) if d == 0 else (s - t)
        pos = (lane if d == 0 else CHUNK - 1 - lane).astype(F32)
        qkv = qkv_ref[...]
        for hd in range(N_HEADS):
            q, k, v = _head_slices(qkv, hd)
            chains.append(dict(d=d, hd=hd, q=q, k=k, v=v, lag=lag, pos=pos, st=st_s[d, hd], ref=st_s,
                               lg=_log_sigmoid(lg_ref[d:d + 1, hd:hd + 1])))
    for c in chains:
        c["s"] = _nt(c["k"], c["q"])
        c["v_t"] = c["v"].T
    for c in chains:
        intra = jnp.where(c["lag"] >= 0, jnp.exp(jnp.maximum(c["lag"], 0).astype(F32) * c["lg"]), 0.0)
        c["a"] = (c["s"] * k_scale * intra).astype(BF16)
    for c in chains:
        inter = jnp.exp((c["pos"] + 1.0) * c["lg"])
        c["o"] = jnp.dot(c["v_t"], c["a"], preferred_element_type=F32) + inter * _nt(c["st"].astype(BF16), c["q"])
    row = lax.broadcasted_iota(jnp.int32, (CHUNK, HEAD_DIM), 0)
    for c in chains:
        src_pos = (row if c["d"] == 0 else CHUNK - 1 - row).astype(F32)
        tail = jnp.exp((CHUNK - 1.0 - src_pos) * c["lg"]) * k_scale
        kw = (c["k"].astype(F32) * tail).astype(BF16)
        c["st_new"] = jnp.exp(CHUNK * c["lg"]) * c["st"] + jnp.dot(c["v_t"], kw, preferred_element_type=F32)
    for j, entry in enumerate(dirs):
        entry[2][...] = jnp.concatenate([c["o"] for c in chains[j * N_HEADS:(j + 1) * N_HEADS]], axis=0).T
    for c in chains:
        c["ref"][c["d"], c["hd"]] = c["st_new"]


def _scan_kernel(tab_ref, mqf_ref, tailf_ref, mqb_ref, tailb_ref, rqf_ref, rqb_ref, bias_ref, lg_ref,
                 c0_ref, n0_ref, m0_ref, s0_ref,
                 hf_ref, hb_ref, of_ref, ob_ref, cn_ref, nn_ref, mn_ref, sn_ref, ct_s, n_s, m_s, st_s):
    i = pl.program_id(0)
    seqs = range(ct_s.shape[0])

    @pl.when(tab_ref[3, i] == 1)
    def _():
        for p in seqs:
            _load_state_t(ct_s.at[p], c0_ref.at[p])
            _load_state_t(st_s.at[p], s0_ref.at[p])
        n_s[...] = n0_ref[...]
        m_s[...] = m0_ref[...]

    mdirs, rdirs = [], []
    for p in seqs:
        state = (ct_s.at[p], n_s.at[p], m_s.at[p])
        mdirs += [(0, mqf_ref.at[p], tailf_ref.at[p], hf_ref.at[p]) + state,
                  (1, mqb_ref.at[p], tailb_ref.at[p], hb_ref.at[p]) + state]
        rdirs += [(0, rqf_ref.at[p], of_ref.at[p], st_s.at[p]), (1, rqb_ref.at[p], ob_ref.at[p], st_s.at[p])]
    _mlstm_step(mdirs, bias_ref)
    _ret_step(rdirs, lg_ref)

    @pl.when(tab_ref[4, i] == 1)
    def _():
        for p in seqs:
            _load_state_t(cn_ref.at[p], ct_s.at[p])
            _load_state_t(sn_ref.at[p], st_s.at[p])
        nn_ref[...] = n_s[...]
        mn_ref[...] = m_s[...]


def _scan_table(st):
    nch = st["n"] // CHUNK
    rows = [(c, nch - 1 - c, g, int(c == 0), int(c == nch - 1))
            for g in range(st["nb"] // _scan_group(st)) for c in range(nch)]
    return jnp.asarray(np.array(rows, dtype=np.int32).T)


def _scan_group(st):
    return next(g for g in (4, 2, 1) if st["nb"] % g == 0)


def _scans(tab, mqkv, tail, bias_row, rqkv, decay_logit, c0, n0, m0, s0, st):
    nb, n, grp = st["nb"], st["n"], _scan_group(st)
    seq3 = lambda a: a.reshape(nb, n, a.shape[-1])
    fwd = lambda i, tab: (tab[2, i], tab[0, i], 0)
    bwd = lambda i, tab: (tab[2, i], tab[1, i], 0)
    const = lambda i, tab: (0, 0)
    rows = lambda width, where: pl.BlockSpec((grp, CHUNK, width), where)
    state = lambda a: pl.BlockSpec((grp,) + a.shape[1:], lambda i, tab: (tab[2, i],) + (0,) * (a.ndim - 1))
    states = (c0, n0, m0, s0)
    grid_spec = pltpu.PrefetchScalarGridSpec(
        num_scalar_prefetch=1,
        grid=(tab.shape[1],),
        in_specs=[rows(QKV_SLOTS, fwd), rows(LANES, fwd), rows(QKV_SLOTS, bwd), rows(LANES, bwd),
                  rows(QKV_SLOTS, fwd), rows(QKV_SLOTS, bwd),
                  pl.BlockSpec((1, LANES), const), pl.BlockSpec(decay_logit.shape, const)]
        + [state(a) for a in states],
        out_specs=[rows(GROUP_WIDTH, fwd), rows(GROUP_WIDTH, bwd), rows(GROUP_WIDTH, fwd), rows(GROUP_WIDTH, bwd)]
        + [state(a) for a in states],
        scratch_shapes=[pltpu.VMEM((grp,) + a.shape[1:], F32) for a in states],
    )
    outs = pl.pallas_call(
        _scan_kernel,
        grid_spec=grid_spec,
        out_shape=[jax.ShapeDtypeStruct((nb, n, GROUP_WIDTH), F32)] * 4
        + [jax.ShapeDtypeStruct(a.shape, F32) for a in states],
        compiler_params=_cparams(("arbitrary",)),
        name="recurrent_scans",
    )(tab, seq3(mqkv), seq3(tail), seq3(mqkv), seq3(tail), seq3(rqkv), seq3(rqkv), bias_row, decay_logit, *states)
    return [o.reshape(nb * n, GROUP_WIDTH) for o in outs[:4]] + list(outs[4:])


def _proj_kernel(x_ref, w_ref, o_ref):
    o_ref[...] = _bdot(x_ref[...], w_ref[...]).astype(o_ref.dtype)


def _project(x, w, dtype):
    return pl.pallas_call(
        _proj_kernel,
        out_shape=jax.ShapeDtypeStruct((x.shape[0], w.shape[1]), dtype),
        name="ctx_kv_proj",
    )(x, w)


def _first_argmax_mask(cur, axis, size):
    io = lax.broadcasted_iota(jnp.int32, cur.shape, axis)
    mx = jnp.max(cur, axis=axis, keepdims=True)
    ix = jnp.min(jnp.where(cur == mx, io, size), axis=axis, keepdims=True)
    return io == ix


def _route(scores_t, bias_col):
    tm = scores_t.shape[1]
    per = N_EXPERTS // N_GROUPS
    sel = scores_t + bias_col
    s3 = sel.reshape(N_GROUPS, per, tm)
    hit1 = _first_argmax_mask(s3, 1, per)
    m1 = jnp.max(s3, axis=1, keepdims=True)
    m2 = jnp.max(jnp.where(hit1, -jnp.inf, s3), axis=1, keepdims=True)
    cur = m1 + m2
    gsel = None
    for _ in range(TOPK_GROUPS):
        hit = _first_argmax_mask(cur, 0, N_GROUPS)
        gsel = hit if gsel is None else jnp.logical_or(gsel, hit)
        cur = jnp.where(hit, -jnp.inf, cur)
    cur = jnp.where(gsel, s3, -jnp.inf).reshape(N_EXPERTS, tm)
    chosen, hits = None, []
    for _ in range(TOP_K):
        hit = _first_argmax_mask(cur, 0, N_EXPERTS)
        hits.append(hit)
        chosen = hit if chosen is None else jnp.logical_or(chosen, hit)
        cur = jnp.where(hit, -jnp.inf, cur)
    w = jnp.where(chosen, scores_t, 0.0)
    return w / jnp.sum(w, axis=0, keepdims=True) * ROUTED_SCALE, chosen, hits


def _out_kernel(oa_ref, od_ref, hf_ref, hb_ref, mo_ref, of_ref, ob_ref, rg_ref,
                x_ref, mod_ref, mg_ref, npost_ref, npre_ref, wout_ref, wrt_ref, rb_ref,
                x1_ref, h2_ref, hpk_ref, eidx_ref, rank_ref, wk_ref, cnt_ref, count_s):
    @pl.when(pl.program_id(0) == 0)
    def _():
        count_s[...] = jnp.zeros(count_s.shape, F32)

    a = oa_ref[...]
    dd = od_ref[...]
    b = _rms_heads(hf_ref[...] + hb_ref[...]) * mg_ref[...] * _sigmoid(mo_ref[...])
    r = _rms_heads(of_ref[...] + ob_ref[...]) * _silu(rg_ref[...])
    mix = sum(jnp.dot(part.astype(BF16), wout_ref[j * GROUP_WIDTH:(j + 1) * GROUP_WIDTH, :],
                      preferred_element_type=F32) for j, part in enumerate((a, b, r, dd)))
    m = mod_ref[0]
    x1 = x_ref[...] + m[2:3] * (_rms(mix) * npost_ref[...])
    x1_ref[...] = x1
    h2 = (_rms(x1) * npre_ref[...] * (1.0 + m[4:5]) + m[3:4]).astype(BF16)
    h2_ref[...] = h2
    hpk_ref[...] = _pack_rows(h2)

    logits_t = lax.dot_general(wrt_ref[...], h2, (((1,), (1,)), ((), ())), preferred_element_type=F32)
    w_t, chosen, hits = _route(_sigmoid(logits_t), rb_ref[...])
    tm = w_t.shape[1]
    src = lax.broadcasted_iota(jnp.int32, (tm, tm), 0)
    dst = lax.broadcasted_iota(jnp.int32, (tm, tm), 1)
    before = jnp.where(src < dst, 1.0, 0.0).astype(BF16)
    picked = jnp.where(chosen, 1.0, 0.0)
    rank = jnp.dot(picked.astype(BF16), before, preferred_element_type=F32) + count_s[...]
    count_s[...] += jnp.sum(picked, axis=1, keepdims=True)
    cnt_ref[...] = jnp.broadcast_to(count_s[...], cnt_ref.shape)
    e_io = lax.broadcasted_iota(jnp.int32, w_t.shape, 0)
    pick = lambda hit, val: jnp.sum(jnp.where(hit, val, jnp.zeros_like(val)), axis=0, keepdims=True)
    eidx_ref[...] = jnp.concatenate([pick(h, e_io) for h in hits], axis=0)
    rank_ref[...] = jnp.concatenate([pick(h, rank) for h in hits], axis=0).astype(jnp.int32)
    wk = jnp.concatenate([pick(h, w_t) for h in hits] + [jnp.zeros((LANES - TOP_K, tm), F32)], axis=0)
    wk_ref[...] = wk.T


def _output_stage(oa, od, hf, hb, mo, of, ob, rg, x, mod, mg, npost, npre, wout, wrt, rb, st):
    t_all = x.shape[0]
    row = lambda i: (i, 0)
    const2 = lambda i: (0, 0)
    tok = lambda w: pl.BlockSpec((TM, w), row)
    full = lambda a: pl.BlockSpec(a.shape, const2)
    return pl.pallas_call(
        _out_kernel,
        grid=(t_all // TM,),
        in_specs=[tok(GROUP_WIDTH)] * 8 + [tok(D_MODEL), _mod_spec(st),
                                   full(mg), full(npost), full(npre), full(wout), full(wrt), full(rb)],
        out_specs=[tok(D_MODEL), tok(D_MODEL), tok(D_MODEL // 2),
                   pl.BlockSpec((TOP_K, TM), lambda i: (0, i)), pl.BlockSpec((TOP_K, TM), lambda i: (0, i)),
                   tok(LANES), pl.BlockSpec((N_EXPERTS, LANES), const2)],
        out_shape=[jax.ShapeDtypeStruct((t_all, D_MODEL), F32),
                   jax.ShapeDtypeStruct((t_all, D_MODEL), BF16),
                   jax.ShapeDtypeStruct((t_all, D_MODEL // 2), jnp.uint32),
                   jax.ShapeDtypeStruct((TOP_K, t_all), jnp.int32),
                   jax.ShapeDtypeStruct((TOP_K, t_all), jnp.int32),
                   jax.ShapeDtypeStruct((t_all, LANES), F32),
                   jax.ShapeDtypeStruct((N_EXPERTS, LANES), F32)],
        scratch_shapes=[pltpu.VMEM((N_EXPERTS, 1), F32)],
        compiler_params=_cparams(("arbitrary",)),
        name="output_stage_" + st["name"],
    )(oa, od, hf, hb, mo, of, ob, rg, x, mod, mg, npost, npre, wout, wrt, rb)


SC_CORES, SC_SUBCORES = 2, 16
SC_WORKERS = SC_CORES * SC_SUBCORES
SLOT_BLOCK = 512
DISPATCH_ROWS = 64
COMBINE_ROWS = 64


def _sc_mesh():
    return plsc.VectorSubcoreMesh(core_axis_name="core", subcore_axis_name="subcore")


def _sc_worker():
    return lax.axis_index("subcore") * SC_CORES + lax.axis_index("core")


def _sc_dispatch(hpk, dest, n_slots):
    t_all, width = hpk.shape
    per_worker = t_all // SC_WORKERS
    assert t_all % (SC_WORKERS * DISPATCH_ROWS) == 0

    @functools.partial(
        pl.kernel, mesh=_sc_mesh(), out_type=jax.ShapeDtypeStruct((n_slots, width), hpk.dtype),
        scratch_types=[pltpu.VMEM((DISPATCH_ROWS, width), hpk.dtype)]
        + [pltpu.VMEM((DISPATCH_ROWS,), jnp.int32)] * TOP_K + [pltpu.SemaphoreType.DMA])
    def dispatch(x_hbm, d_hbm, o_hbm, rows_v, *rest):
        idx, sem = rest[:TOP_K], rest[TOP_K]

        @pl.loop(0, per_worker // DISPATCH_ROWS)
        def _(j):
            base = _sc_worker() * per_worker + j * DISPATCH_ROWS
            pltpu.sync_copy(x_hbm.at[pl.ds(base, DISPATCH_ROWS)], rows_v)
            for k in range(TOP_K):
                pltpu.sync_copy(d_hbm.at[pl.ds(k * t_all + base, DISPATCH_ROWS)], idx[k])
            copies = [pltpu.async_copy(rows_v, o_hbm.at[idx[k]], sem) for k in range(TOP_K)]
            for c in copies:
                c.wait()

    return dispatch(hpk, dest)


def _sc_combine(yb, dest, t_all):
    width = yb.shape[1]
    per_worker = t_all // SC_WORKERS
    assert t_all % (SC_WORKERS * COMBINE_ROWS) == 0

    @functools.partial(
        pl.kernel, mesh=_sc_mesh(), out_type=jax.ShapeDtypeStruct((TOP_K, t_all, width), yb.dtype),
        scratch_types=[pltpu.VMEM((COMBINE_ROWS, width), yb.dtype)] * 2
        + [pltpu.VMEM((COMBINE_ROWS,), jnp.int32)] * TOP_K + [pltpu.SemaphoreType.DMA] * 2)
    def combine(y_hbm, d_hbm, o_hbm, rows_a, rows_b, *rest):
        idx, sems = rest[:TOP_K], rest[TOP_K:]
        bufs = (rows_a, rows_b)

        @pl.loop(0, per_worker // COMBINE_ROWS)
        def _(j):
            base = _sc_worker() * per_worker + j * COMBINE_ROWS
            for k in range(TOP_K):
                pltpu.sync_copy(d_hbm.at[pl.ds(k * t_all + base, COMBINE_ROWS)], idx[k])
            pending = pltpu.async_copy(y_hbm.at[idx[0]], bufs[0], sems[0])
            for k in range(TOP_K):
                pending.wait()
                if k + 1 < TOP_K:
                    pending = pltpu.async_copy(y_hbm.at[idx[k + 1]], bufs[(k + 1) % 2], sems[(k + 1) % 2])
                pltpu.sync_copy(bufs[k % 2], o_hbm.at[k, pl.ds(base, COMBINE_ROWS)])

    return combine(yb, dest)


def _pack_rows(x):
    bits = pltpu.bitcast(x.astype(BF16).astype(F32), jnp.uint32)
    w = x.shape[-1] // 2
    return (bits[..., :w] >> 16) | (bits[..., w:] & jnp.uint32(0xFFFF0000))


def _unpack_rows(words):
    return pltpu.bitcast(words << 16, F32), pltpu.bitcast(words & jnp.uint32(0xFFFF0000), F32)


def _expert_kernel(be_ref, used_ref, xs_ref, wg_ref, wu_ref, wd_ref, y_ref, wg_s, wu_s, wd_s):
    b = pl.program_id(0)
    fresh = jnp.logical_or(b == 0, be_ref[b] != be_ref[jnp.maximum(b - 1, 0)])

    @pl.when(jnp.logical_and(fresh, b < used_ref[0]))
    def _():
        wg_s[...] = wg_ref[0, 0].astype(BF16)
        wu_s[...] = wu_ref[0, 0].astype(BF16)
        wd_s[...] = wd_ref[0, 0].astype(BF16)

    @pl.when(b < used_ref[0])
    def _():
        lo, hi = (v.astype(BF16) for v in _unpack_rows(xs_ref[...]))
        half = D_MODEL // 2
        gate = (jnp.dot(lo, wg_s[:half, :], preferred_element_type=F32)
                + jnp.dot(hi, wg_s[half:, :], preferred_element_type=F32))
        up = (jnp.dot(lo, wu_s[:half, :], preferred_element_type=F32)
              + jnp.dot(hi, wu_s[half:, :], preferred_element_type=F32))
        y = jnp.dot((_silu(gate) * up).astype(BF16), wd_s[...], preferred_element_type=F32)
        y_ref[...] = _pack_rows(y)


def _expert_blocks(layer, xs, block_expert, blocks_used, wg, wu, wdn):
    n_slots = xs.shape[0]
    wspec = lambda shape: pl.BlockSpec((1, 1) + shape, lambda b, be, used: (layer, be[b], 0, 0))
    rows = lambda b, be, used: (jnp.minimum(b, used[0] - 1), 0)
    grid_spec = pltpu.PrefetchScalarGridSpec(
        num_scalar_prefetch=2,
        grid=(n_slots // SLOT_BLOCK,),
        in_specs=[pl.BlockSpec((SLOT_BLOCK, D_MODEL // 2), rows),
                  wspec((D_MODEL, D_EXPERT)), wspec((D_MODEL, D_EXPERT)), wspec((D_EXPERT, D_MODEL))],
        out_specs=pl.BlockSpec((SLOT_BLOCK, D_MODEL // 2), rows),
        scratch_shapes=[pltpu.VMEM((D_MODEL, D_EXPERT), BF16), pltpu.VMEM((D_MODEL, D_EXPERT), BF16),
                        pltpu.VMEM((D_EXPERT, D_MODEL), BF16)])
    return pl.pallas_call(
        _expert_kernel, grid_spec=grid_spec,
        out_shape=jax.ShapeDtypeStruct((n_slots, D_MODEL // 2), jnp.uint32),
        compiler_params=_cparams(("arbitrary",)),
        name="moe_experts",
    )(block_expert, blocks_used, xs, wg, wu, wdn)


def _moe_out_kernel(g_ref, wk_ref, h_ref, x1_ref, mod_ref, npost_ref, sg_ref, su_ref, sd_ref, o_ref):
    h = h_ref[...]
    act = _silu(jnp.dot(h, sg_ref[...], preferred_element_type=F32)) * jnp.dot(h, su_ref[...],
                                                                              preferred_element_type=F32)
    moe = jnp.dot(act.astype(BF16), sd_ref[...], preferred_element_type=F32)
    wk = wk_ref[...]
    lo, hi = 0.0, 0.0
    for k in range(TOP_K):
        g_lo, g_hi = _unpack_rows(g_ref[k])
        lo = lo + g_lo * wk[:, k:k + 1]
        hi = hi + g_hi * wk[:, k:k + 1]
    moe = moe + jnp.concatenate([lo, hi], axis=-1)
    m = mod_ref[0]
    o_ref[...] = x1_ref[...] + m[5:6] * (_rms(moe) * npost_ref[...])


def _moe_out_stage(g, wk, h2, x1, mod, npost, sg, su, sd, st):
    t_all = h2.shape[0]
    row = lambda i: (i, 0)
    full = lambda a: pl.BlockSpec(a.shape, lambda i: (0, 0))
    return pl.pallas_call(
        _moe_out_kernel,
        grid=(t_all // TM,),
        in_specs=[pl.BlockSpec((TOP_K, TM, D_MODEL // 2), lambda i: (0, i, 0)), pl.BlockSpec((TM, LANES), row),
                  pl.BlockSpec((TM, D_MODEL), row), pl.BlockSpec((TM, D_MODEL), row), _mod_spec(st),
                  full(npost), full(sg), full(su), full(sd)],
        out_specs=pl.BlockSpec((TM, D_MODEL), row),
        out_shape=jax.ShapeDtypeStruct((t_all, D_MODEL), F32),
        compiler_params=_cparams(("parallel",)),
        name="moe_combine_" + st["name"],
    )(g, wk, h2, x1, mod, npost, sg, su, sd)


def _moe_stage(layer, h2, hpk, eidx, rank, wk, counts, x1, mod, npost, wg, wu, wdn, sg, su, sd, st):
    t_all = h2.shape[0]
    n_blocks = -(-(t_all * TOP_K + N_EXPERTS * (SLOT_BLOCK - 1)) // SLOT_BLOCK)
    cnt = counts[:, 0].astype(jnp.int32)
    padded = (cnt + SLOT_BLOCK - 1) // SLOT_BLOCK * SLOT_BLOCK
    pad_end = jnp.cumsum(padded)
    pad_start = pad_end - padded
    experts = jnp.arange(N_EXPERTS, dtype=jnp.int32)[:, None, None]
    dest = rank + jnp.sum(jnp.where(eidx[None] == experts, pad_start[:, None, None], 0), axis=0)
    dest = dest.reshape(TOP_K * t_all)
    block_start = jnp.arange(n_blocks, dtype=jnp.int32) * SLOT_BLOCK
    block_expert = jnp.minimum(jnp.sum((pad_end[None, :] <= block_start[:, None]).astype(jnp.int32), axis=1),
                               N_EXPERTS - 1)
    blocks_used = (pad_end[-1:] // SLOT_BLOCK).astype(jnp.int32)

    xs = _sc_dispatch(hpk, dest, n_blocks * SLOT_BLOCK)
    yb = _expert_blocks(layer, xs, block_expert, blocks_used, wg, wu, wdn)
    g = _sc_combine(yb, dest, t_all)
    return _moe_out_stage(g, wk, h2, x1, mod, npost, sg, su, sd, st)


_PERM32 = np.concatenate([np.arange(8, 16), np.arange(0, 8), np.arange(24, 32), np.arange(16, 24)])


def _rope_tables(n):
    pos = jnp.arange(n)
    quarter = QK_ROPE // 4
    inv = 1.0 / (ROPE_THETA ** (jnp.arange(quarter, dtype=F32) / quarter))
    ang_r = (pos // GRID_W).astype(F32)[:, None] * inv[None, :]
    ang_c = (pos % GRID_W).astype(F32)[:, None] * inv[None, :]
    cos32 = jnp.concatenate([jnp.cos(ang_r)] * 2 + [jnp.cos(ang_c)] * 2, axis=-1)
    sin32 = jnp.concatenate([-jnp.sin(ang_r), jnp.sin(ang_r), -jnp.sin(ang_c), jnp.sin(ang_c)], axis=-1)
    cos256, sin256 = jnp.tile(cos32, (1, 8)), jnp.tile(sin32, (1, 8))
    one, zero = jnp.ones((n, QK_NOPE), F32), jnp.zeros((n, QK_NOPE), F32)
    cosq = jnp.concatenate([one, cos32] * N_HEADS, axis=-1)
    sinq = jnp.concatenate([zero, sin32] * N_HEADS, axis=-1)
    return cos256, sin256, cosq, sinq


def _split_w_in(w_in_l):
    gw = GROUP_WIDTH
    sizes = (gw, gw, gw, gw, gw, gw, gw, 4 * N_HEADS, gw, gw, gw, gw, Q_LORA, KV_LORA, QK_ROPE)
    idx = np.cumsum(sizes)[:-1]
    return jnp.split(w_in_l, [int(v) for v in idx], axis=-1)


def _layer_weights(w_in_l, w_uq_l):
    (a_q, a_k, a_v, m_q, m_k, m_v, m_o, m_g, r_q, r_k, r_v, r_g, d_cq, d_ckv, d_kr) = _split_w_in(w_in_l)
    perm256 = np.concatenate([_PERM32 + 32 * j for j in range(8)])
    pad = lambda w, n: jnp.pad(w, ((0, 0), (0, n - w.shape[1])))
    tail = pad(jnp.concatenate([d_kr, m_g], axis=-1), LANES)
    tail_s = pad(d_kr[:, _PERM32], LANES)
    w_all = jnp.concatenate([a_q, a_k, a_v, m_q, m_k, m_v, m_o, r_q, r_k, r_v, r_g, d_cq, d_ckv, tail,
                             a_q[:, perm256], a_k[:, perm256], tail_s], axis=-1).astype(BF16)
    per = QK_NOPE + QK_ROPE
    permq = np.concatenate([np.concatenate([np.arange(QK_NOPE), QK_NOPE + _PERM32]) + per * j
                            for j in range(N_HEADS)])
    return w_all, w_uq_l.astype(BF16), w_uq_l[:, permq].astype(BF16)


def kernel(x_prompt, x_sample, cache_diff_k, cache_diff_v, state_mlstm_C, state_mlstm_n, state_mlstm_m, state_ret_S, cache_mla_ckv, cache_mla_krope, c, c_ctx, w_mod, b_mod, norm_pre, norm_post, w_in, w_out, diff_lambda, diff_norm, mlstm_gate_bias, mlstm_norm, ret_decay_logit, mla_q_norm, mla_w_uq, mla_kv_norm, mla_w_ukv, moe_w_router, moe_router_bias, moe_w_gate, moe_w_up, moe_w_down, shared_w_gate, shared_w_up, shared_w_down):
    bp, n_p, _ = x_prompt.shape
    bs, n_s, _ = x_sample.shape
    depth = w_in.shape[0]
    past = cache_diff_k.shape[2]
    assert (bp * n_p) % TM == 0 and n_s % TM == 0 and n_p % CHUNK == 0 and n_s % min(TK, n_s) == 0
    assert past % min(TK, past) == 0 and bs + 1 <= 8 and n_s % GRID_W == 0

    streams = (dict(name="ctx", nb=bp, n=n_p, latent=False), dict(name="latent", nb=bs, n=n_s, latent=True))
    xs_by_stream = [x_prompt.reshape(bp * n_p, D_MODEL), x_sample.reshape(bs * n_s, D_MODEL)]
    cond = jnp.zeros((8, D_MODEL), F32).at[0].set(c_ctx).at[1:1 + bs].set(c)
    mod_all = _modulation(cond, w_mod, b_mod).reshape(depth, 8, 6, D_MODEL)
    tabs = _rope_tables(n_s)
    scan_tabs = [_scan_table(st) for st in streams]

    outs = [[] for _ in range(8)]
    for l in range(depth):
        lam_init = 0.8 - 0.6 * math.exp(-0.3 * l)
        mod = mod_all[l]
        w_all, wuq, wuqs = _layer_weights(w_in[l], mla_w_uq[l])
        wukv = mla_w_ukv[l].astype(BF16)
        wout, wrt, rbias = w_out[l].astype(BF16), moe_w_router[l].T.astype(BF16), moe_router_bias[l][:, None]
        shared = [w[l].astype(BF16) for w in (shared_w_gate, shared_w_up, shared_w_down)]
        lamv, dg = diff_lambda[l], diff_norm[l][:, None]
        bias_row = jnp.zeros((1, LANES), F32).at[0, TAIL_GATE0:TAIL_GATE0 + 16].set(mlstm_gate_bias[l].reshape(16))

        with_ones = lambda vt: jnp.concatenate([vt, jnp.ones((N_HEADS, VT_ONES, bs * past), BF16)], axis=1)
        ck = jnp.transpose(cache_diff_k[:, l], (2, 0, 1, 3)).reshape(N_HEADS, bs * past, HEAD_DIM).astype(BF16)
        cvt = with_ones(jnp.transpose(cache_diff_v[:, l], (2, 3, 0, 1)).reshape(N_HEADS, HEAD_DIM, bs * past)
                        .astype(BF16))
        kvc = _project(cache_mla_ckv[:, l].reshape(bs * past, KV_LORA), wukv, BF16)
        kvc = kvc.reshape(bs * past, N_HEADS, QK_NOPE + V_HEAD)
        krc = jnp.broadcast_to(cache_mla_krope[:, l].reshape(bs * past, 1, QK_ROPE).astype(BF16),
                               (bs * past, N_HEADS, QK_ROPE))
        kc = jnp.transpose(jnp.concatenate([kvc[..., :QK_NOPE], krc], axis=-1), (1, 0, 2))
        vct = with_ones(jnp.transpose(kvc[..., QK_NOPE:], (1, 2, 0)))

        for si, st in enumerate(streams):
            x, nb, n = xs_by_stream[si], st["nb"], st["n"]
            (aq1t, aq2t, ak1, ak2, avt, ak, av, mqkv, mo, rqkv, rg, tail, qmlat, ckv, kmla, vmlat) = _input_stage(
                x, mod, norm_pre[l, 0:1], w_all, tabs, mla_q_norm[l][None], wuq, wuqs,
                mla_kv_norm[l][None], wukv, st)
            if st["latent"]:
                oa = _attention_t("diff_attn_latent", [aq1t, aq2t], [ak1, ak2], avt,
                                  [ck[..., :DIFF_HALF], ck[..., DIFF_HALF:]], cvt, [lamv, dg], nb, n, past, lam_init)
                od = _attention_t("mla_attn_latent", [qmlat], [kmla], vmlat, [kc], vct, [], nb, n, past)
                states = [s[:, l] for s in (state_mlstm_C, state_mlstm_n, state_mlstm_m, state_ret_S)]
            else:
                oa = _attention_t("diff_attn_ctx", [aq1t, aq2t], [ak1, ak2], avt, None, None, [lamv, dg],
                                  nb, n, 0, lam_init)
                od = _attention_t("mla_attn_ctx", [qmlat], [kmla], vmlat, None, None, [], nb, n, 0)
                states = [jnp.zeros((nb,) + s.shape[2:], F32)
                          for s in (state_mlstm_C, state_mlstm_n, state_mlstm_m, state_ret_S)]
            hf, hb, of, ob, c_n, n_n, m_n, s_n = _scans(scan_tabs[si], mqkv, tail, bias_row, rqkv,
                                                        ret_decay_logit[l], *states, st)

            x1, h2, hpk, eidx, rank, wk, counts = _output_stage(
                oa, od, hf, hb, mo, of, ob, rg, x, mod, mlstm_norm[l][None], norm_post[l, 0:1],
                norm_pre[l, 1:2], wout, wrt, rbias, st)
            xs_by_stream[si] = _moe_stage(l, h2, hpk, eidx, rank, wk, counts, x1, mod, norm_post[l, 1:2],
                                          moe_w_gate, moe_w_up, moe_w_down, *shared, st)
            if not st["latent"]:
                new = (ak.reshape(nb, n, N_HEADS, HEAD_DIM), av.reshape(nb, n, N_HEADS, HEAD_DIM), c_n, n_n, m_n,
                       s_n, ckv.reshape(nb, n, KV_LORA), tail[:, :QK_ROPE].reshape(nb, n, QK_ROPE))
                for o, v in zip(outs, new):
                    o.append(v)

    return (xs_by_stream[0].reshape(bp, n_p, D_MODEL), xs_by_stream[1].reshape(bs, n_s, D_MODEL)) + tuple(
        jnp.stack(o, axis=1) for o in outs)
```

```python
import functools
import math

import numpy as np
import jax
import jax.numpy as jnp
from jax import lax
from jax.experimental import pallas as pl
from jax.experimental.pallas import tpu as pltpu
from jax.experimental.pallas import tpu_sc as plsc

F32 = jnp.float32
BF16 = jnp.bfloat16
HIGHEST = lax.Precision.HIGHEST

D_MODEL = 1024
GRID_W = 64
GROUP_WIDTH = 256
HEAD_DIM = 64
N_HEADS = 4
DIFF_HALF = 32
ROPE_THETA = 10000.0
Q_LORA = 256
KV_LORA = 128
QK_NOPE = 64
QK_ROPE = 32
V_HEAD = 64
N_EXPERTS = 64
TOP_K = 8
N_GROUPS = 8
TOPK_GROUPS = 4
D_EXPERT = 256
ROUTED_SCALE = 2.5
CHUNK = 128
EPS = 1e-6
Q_MLA = N_HEADS * (QK_NOPE + QK_ROPE)

LANES = 128
VMEM_LIMIT = 56 * 1024 * 1024

TM = 512
TQ = 256
TK = 4096

C_AQ, C_AK, C_AV, C_MQKV, C_MO = (0, 0), (0, 256), (0, 512), (0, 768), (0, 1536)
C_RQKV, C_RG, C_CQ, C_CKV = (1, 0), (1, 768), (1, 1024), (1, 1280)
C_TAIL, C_AQS, C_AKS, C_TAILS = (2, 0), (2, 128), (2, 384), (2, 640)
W_BEFORE_GATES, W_AFTER_GATES = 1792, 1408
TAIL_GATE0 = QK_ROPE
QKV_SLOTS = 3 * N_HEADS * LANES
VT_ONES = 16
VT_ROWS = V_HEAD + VT_ONES


def _cparams(sem, flags=None):
    return pltpu.CompilerParams(dimension_semantics=sem, vmem_limit_bytes=VMEM_LIMIT, flags=flags)


def _rms(x):
    return x * lax.rsqrt(jnp.mean(x * x, axis=-1, keepdims=True) + EPS)


def _head_mean_matrix(width):
    r = lax.broadcasted_iota(jnp.int32, (width, width), 0) // HEAD_DIM
    c = lax.broadcasted_iota(jnp.int32, (width, width), 1) // HEAD_DIM
    return jnp.where(r == c, 1.0 / HEAD_DIM, 0.0).astype(BF16)


def _rms_heads(x):
    sq = x * x
    hi = sq.astype(BF16)
    lo = (sq - hi.astype(F32)).astype(BF16)
    g = _head_mean_matrix(x.shape[-1])
    ms = jnp.dot(hi, g, preferred_element_type=F32) + jnp.dot(lo, g, preferred_element_type=F32)
    return x * lax.rsqrt(ms + EPS)


def _sigmoid(x):
    return 1.0 / (1.0 + jnp.exp(-x))


def _silu(x):
    return x * _sigmoid(x)


def _log_sigmoid(x):
    return jnp.minimum(x, 0.0) - jnp.log1p(jnp.exp(-jnp.abs(x)))


def _bdot(a, b):
    return jnp.dot(a.astype(BF16), b.astype(BF16), preferred_element_type=F32)


def _mod_kernel(c_ref, w_ref, b_ref, o_ref):
    o_ref[0] = _bdot(_silu(c_ref[...]), w_ref[0]) + b_ref[0]


def _modulation(cond, w_mod, b_mod):
    depth, _, n = w_mod.shape
    tn = 1536
    return pl.pallas_call(
        _mod_kernel,
        grid=(depth, n // tn),
        in_specs=[pl.BlockSpec((8, D_MODEL), lambda l, j: (0, 0)),
                  pl.BlockSpec((1, D_MODEL, tn), lambda l, j: (l, 0, j)),
                  pl.BlockSpec((1, 1, tn), lambda l, j: (l, 0, j))],
        out_specs=pl.BlockSpec((1, 8, tn), lambda l, j: (l, 0, j)),
        out_shape=jax.ShapeDtypeStruct((depth, 8, n), F32),
        compiler_params=_cparams(("parallel", "parallel")),
        name="adaln_mod",
    )(cond, w_mod, b_mod.reshape(depth, 1, n))


def _in_kernel(x_ref, mod_ref, npre_ref, wa_ref, wb_ref, wc_ref, cos_ref, sin_ref, cosq_ref, sinq_ref,
               qg_ref, wuq_ref, wuqs_ref, kvg_ref, wukv_ref,
               aq1_ref, aq2_ref, ak1t_ref, ak2t_ref, avh_ref, ak_ref, av_ref,
               mqkv_ref, mo_ref, rqkv_ref, rg_ref, tail_ref,
               qmla_ref, ckv_ref, kmlat_ref, vmla_ref, *, latent):
    m = mod_ref[0]
    h = (_rms(x_ref[...]) * npre_ref[...] * (1.0 + m[1:2]) + m[0:1]).astype(BF16)

    def proj(col, width):
        w_ref = (wa_ref, wb_ref, wc_ref)[col[0]]
        return jnp.dot(h, w_ref[:, col[1]:col[1] + width], preferred_element_type=F32)

    def rotated(c0, c0_swapped, width, cos, sin):
        return proj(c0, width) * cos + proj(c0_swapped, width) * sin if latent else proj(c0, width)

    cos = cos_ref[...]
    sin = sin_ref[...]
    aq = rotated(C_AQ, C_AQS, GROUP_WIDTH, cos, sin) * (DIFF_HALF ** -0.5)
    ak = rotated(C_AK, C_AKS, GROUP_WIDTH, cos, sin)
    av = proj(C_AV, GROUP_WIDTH)
    ak_ref[...] = ak
    av_ref[...] = av
    aq_t, av_t = aq.T.astype(BF16), av.T.astype(BF16)
    ones = jnp.ones((VT_ONES, aq_t.shape[1]), BF16)
    for hd in range(N_HEADS):
        lo = hd * HEAD_DIM
        aq1_ref[hd] = aq_t[lo:lo + DIFF_HALF, :]
        aq2_ref[hd] = aq_t[lo + DIFF_HALF:lo + HEAD_DIM, :]
        ak1t_ref[hd] = ak[:, lo:lo + DIFF_HALF].astype(BF16)
        ak2t_ref[hd] = ak[:, lo + DIFF_HALF:lo + HEAD_DIM].astype(BF16)
        avh_ref[hd, 0:HEAD_DIM, :] = av_t[lo:lo + HEAD_DIM, :]
        avh_ref[hd, HEAD_DIM:VT_ROWS, :] = ones

    def head_slots(x):
        x = x.astype(BF16)
        zero = jnp.zeros((x.shape[0], LANES - HEAD_DIM), BF16)
        parts = []
        for j in range(3 * N_HEADS):
            parts += [x[:, j * HEAD_DIM:(j + 1) * HEAD_DIM], zero]
        return jnp.concatenate(parts, axis=-1)

    mqkv_ref[...] = head_slots(proj(C_MQKV, 3 * GROUP_WIDTH))
    mo_ref[...] = proj(C_MO, GROUP_WIDTH)
    rqkv_ref[...] = head_slots(proj(C_RQKV, 3 * GROUP_WIDTH))
    rg_ref[...] = proj(C_RG, GROUP_WIDTH)

    tail = proj(C_TAIL, LANES)
    tail_ref[...] = tail
    kr = tail * cos[:, :LANES] + proj(C_TAILS, LANES) * sin[:, :LANES] if latent else tail
    kr = kr[:, :QK_ROPE].astype(BF16)

    cqn = (_rms(proj(C_CQ, Q_LORA)) * qg_ref[...]).astype(BF16)
    qmla = jnp.dot(cqn, wuq_ref[...], preferred_element_type=F32)
    if latent:
        qmla = qmla * cosq_ref[...] + jnp.dot(cqn, wuqs_ref[...], preferred_element_type=F32) * sinq_ref[...]
    qmla_t = (qmla * ((QK_NOPE + QK_ROPE) ** -0.5)).T.astype(BF16)
    for hd in range(N_HEADS):
        lo = hd * (QK_NOPE + QK_ROPE)
        qmla_ref[hd] = qmla_t[lo:lo + QK_NOPE + QK_ROPE, :]
    ckvn = _rms(proj(C_CKV, KV_LORA)) * kvg_ref[...]
    ckv_ref[...] = ckvn
    kv = jnp.dot(ckvn.astype(BF16), wukv_ref[...], preferred_element_type=F32)
    kv_t = kv.T.astype(BF16)
    per = QK_NOPE + V_HEAD
    for hd in range(N_HEADS):
        kmlat_ref[hd, :, 0:QK_NOPE] = kv[:, hd * per:hd * per + QK_NOPE].astype(BF16)
        kmlat_ref[hd, :, QK_NOPE:QK_NOPE + QK_ROPE] = kr
        vmla_ref[hd, 0:V_HEAD, :] = kv_t[hd * per + QK_NOPE:(hd + 1) * per, :]
        vmla_ref[hd, V_HEAD:VT_ROWS, :] = ones


def _mod_spec(st, tm=TM):
    tps = st["n"] // tm
    return pl.BlockSpec((1, 6, D_MODEL), lambda i, *_: (1 + i // tps if st["latent"] else 0, 0, 0))


def _input_stage(x, mod, npre, w_all, tabs, qg, wuq, wuqs, kvg, wukv, st):
    t_all = x.shape[0]
    tm = TM
    tps = st["n"] // tm
    row = lambda i: (i, 0)
    tab = lambda w: pl.BlockSpec((tm, w), lambda i: (i % tps if st["latent"] else 0, 0))
    hrow = lambda i: (0, i, 0)
    const2 = lambda i: (0, 0)
    tok = lambda w: pl.BlockSpec((tm, w), row)
    headed = lambda w: pl.BlockSpec((N_HEADS, tm, w), hrow)
    headed_t = lambda d: pl.BlockSpec((N_HEADS, d, tm), lambda i: (0, 0, i))
    full = lambda a: pl.BlockSpec(a.shape, const2)
    cos, sin, cosq, sinq = tabs
    out_shapes = [
        (headed_t(DIFF_HALF), (N_HEADS, DIFF_HALF, t_all), BF16),
        (headed_t(DIFF_HALF), (N_HEADS, DIFF_HALF, t_all), BF16),
        (headed(DIFF_HALF), (N_HEADS, t_all, DIFF_HALF), BF16),
        (headed(DIFF_HALF), (N_HEADS, t_all, DIFF_HALF), BF16),
        (headed_t(VT_ROWS), (N_HEADS, VT_ROWS, t_all), BF16),
        (tok(GROUP_WIDTH), (t_all, GROUP_WIDTH), F32),
        (tok(GROUP_WIDTH), (t_all, GROUP_WIDTH), F32),
        (tok(QKV_SLOTS), (t_all, QKV_SLOTS), BF16),
        (tok(GROUP_WIDTH), (t_all, GROUP_WIDTH), F32),
        (tok(QKV_SLOTS), (t_all, QKV_SLOTS), BF16),
        (tok(GROUP_WIDTH), (t_all, GROUP_WIDTH), F32),
        (tok(LANES), (t_all, LANES), F32),
        (headed_t(QK_NOPE + QK_ROPE), (N_HEADS, QK_NOPE + QK_ROPE, t_all), BF16),
        (tok(KV_LORA), (t_all, KV_LORA), F32),
        (headed(QK_NOPE + QK_ROPE), (N_HEADS, t_all, QK_NOPE + QK_ROPE), BF16),
        (headed_t(VT_ROWS), (N_HEADS, VT_ROWS, t_all), BF16),
    ]
    return pl.pallas_call(
        functools.partial(_in_kernel, latent=st["latent"]),
        grid=(t_all // tm,),
        in_specs=[tok(D_MODEL), _mod_spec(st, tm),
                  full(npre), *[full(w) for w in w_all], tab(GROUP_WIDTH), tab(GROUP_WIDTH), tab(Q_MLA), tab(Q_MLA),
                  full(qg), full(wuq), full(wuqs), full(kvg), full(wukv)],
        out_specs=[s for s, _, _ in out_shapes],
        out_shape=[jax.ShapeDtypeStruct(shp, dt) for _, shp, dt in out_shapes],
        compiler_params=_cparams(("parallel",)),
        name="input_stage_" + st["name"],
    )(x, mod, npre, *w_all, cos, sin, cosq, sinq, qg, wuq, wuqs, kvg, wukv)


def _attn_t_kernel(*refs, n_soft, n_new, n_ctx, lam_init):
    refs = list(refs)
    qt_refs, k_refs, vt_ref = refs[:n_soft], refs[n_soft:2 * n_soft], refs[2 * n_soft]
    pos = 2 * n_soft + 1
    if n_ctx:
        ck_refs, cvt_ref = refs[pos:pos + n_soft], refs[pos + n_soft]
        pos += n_soft + 1
    if n_soft == 2:
        lam_ref, g_ref = refs[pos:pos + 2]
        pos += 2
    o_ref = refs[pos]
    tq = o_ref.shape[0]
    nchain = n_soft * N_HEADS
    order = [(j, hd) for hd in range(N_HEADS) for j in range(n_soft)]

    def chunk(state, k_of, vt_of):
        ms, accs = state
        new_m, new_acc = list(ms), list(accs)
        ss = [jnp.dot(k_of(j, hd), qt_refs[j][hd], preferred_element_type=F32) for j, hd in order]
        ps, alphas = [], []
        for (j, hd), s in zip(order, ss):
            c = j * N_HEADS + hd
            s3 = s.reshape(s.shape[0] // 8, 8, tq)
            top = jnp.max(jnp.max(s3, axis=0), axis=0, keepdims=True)
            m_new = jnp.maximum(ms[c], jnp.broadcast_to(top, (8, tq)))
            ps.append(jnp.exp(s3 - m_new[None]).reshape(s.shape).astype(BF16))
            alphas.append(jnp.exp(ms[c] - m_new))
            new_m[c] = m_new
        for (j, hd), p, alpha in zip(order, ps, alphas):
            c = j * N_HEADS + hd
            scaled = (accs[c].reshape(VT_ROWS // 8, 8, tq) * alpha[None]).reshape(VT_ROWS, tq)
            new_acc[c] = scaled + jnp.dot(vt_of(hd), p, preferred_element_type=F32)
        return tuple(new_m), tuple(new_acc)

    state = (tuple(jnp.full((8, tq), -jnp.inf, F32) for _ in range(nchain)),
             tuple(jnp.zeros((VT_ROWS, tq), F32) for _ in range(nchain)))
    if n_ctx:
        cstep = min(TK, n_ctx)
        for i in range(n_ctx // cstep):
            state = chunk(state, lambda j, hd, i=i: ck_refs[j][hd, i * cstep:(i + 1) * cstep, :],
                          lambda hd, i=i: cvt_ref[hd, :, i * cstep:(i + 1) * cstep])
    step = min(TK, n_new)

    def body(i, st):
        start = pl.multiple_of(i * step, step)
        return chunk(st, lambda j, hd: k_refs[j][hd, pl.ds(start, step), :],
                     lambda hd: vt_ref[hd, :, pl.ds(start, step)])

    _, accs = lax.fori_loop(0, n_new // step, body, state)

    def normalised(c):
        num, den = accs[c][:V_HEAD], accs[c][V_HEAD:V_HEAD + 8]
        return (num.reshape(V_HEAD // 8, 8, tq) / den[None]).reshape(V_HEAD, tq)

    if n_soft == 2:
        lv = lam_ref[...]
        lam = (jnp.exp(jnp.sum(lv[0:1] * lv[1:2], axis=-1, keepdims=True))
               - jnp.exp(jnp.sum(lv[2:3] * lv[3:4], axis=-1, keepdims=True)) + lam_init)
    outs = []
    for hd in range(N_HEADS):
        out = normalised(hd)
        if n_soft == 2:
            a = out - lam * normalised(N_HEADS + hd)
            out = a * lax.rsqrt(jnp.mean(a * a, axis=0, keepdims=True) + EPS) * g_ref[...] * (1.0 - lam_init)
        outs.append(out)
    o_ref[...] = jnp.concatenate(outs, axis=0).T


def _attention_t(name, qts, ks, vt, ctx_ks, ctx_vt, extras, nb, n, n_ctx, lam_init=0.0):
    tq = min(TQ, n)
    nqt = n // tq
    n_soft = len(qts)
    in_specs = [pl.BlockSpec((N_HEADS, a.shape[1], tq), lambda b, i: (0, 0, b * nqt + i)) for a in qts]
    in_specs += [pl.BlockSpec((N_HEADS, n, a.shape[-1]), lambda b, i: (0, b, 0)) for a in ks]
    in_specs += [pl.BlockSpec((N_HEADS, VT_ROWS, n), lambda b, i: (0, 0, b))]
    args = list(qts) + list(ks) + [vt]
    if n_ctx:
        in_specs += [pl.BlockSpec((N_HEADS, n_ctx, a.shape[-1]), lambda b, i: (0, b, 0)) for a in ctx_ks]
        in_specs += [pl.BlockSpec((N_HEADS, VT_ROWS, n_ctx), lambda b, i: (0, 0, b))]
        args += list(ctx_ks) + [ctx_vt]
    in_specs += [pl.BlockSpec(a.shape, lambda b, i: (0, 0)) for a in extras]
    args += list(extras)
    return pl.pallas_call(
        functools.partial(_attn_t_kernel, n_soft=n_soft, n_new=n, n_ctx=n_ctx, lam_init=lam_init),
        grid=(nb, nqt),
        in_specs=in_specs,
        out_specs=pl.BlockSpec((tq, N_HEADS * V_HEAD), lambda b, i: (b * nqt + i, 0)),
        out_shape=jax.ShapeDtypeStruct((nb * n, N_HEADS * V_HEAD), F32),
        compiler_params=_cparams(("parallel", "parallel")),
        name=name,
    )(*args)


def _tri(lower):
    r = lax.broadcasted_iota(jnp.int32, (CHUNK, CHUNK), 0)
    c = lax.broadcasted_iota(jnp.int32, (CHUNK, CHUNK), 1)
    return (c <= r) if lower else (c >= r)


def _nt(a, b):
    return lax.dot_general(a, b, (((1,), (1,)), ((), ())), preferred_element_type=F32)


def _tn(a, b):
    return lax.dot_general(a, b, (((0,), (0,)), ((), ())), preferred_element_type=F32)


def _head_slices(qkv, hd):
    return tuple(qkv[:, (j * N_HEADS + hd) * LANES:(j * N_HEADS + hd) * LANES + HEAD_DIM] for j in range(3))


def _row_dot(row, mat, transpose_mat):
    row8 = jnp.broadcast_to(row, (8, row.shape[1])).astype(BF16)
    dims = (((1,), (1,)), ((), ())) if transpose_mat else (((1,), (0,)), ((), ()))
    return lax.dot_general(row8, mat, dims, preferred_element_type=F32)[0:1]


def _load_state_t(dst, src):
    for d in range(2):
        for hd in range(N_HEADS):
            dst[d, hd] = src[d, hd].T


def _mlstm_step(dirs, bias_ref):
    k_scale = HEAD_DIM ** -0.5
    lower = _tri(True).astype(F32)
    upper = _tri(False).astype(F32)
    chains = []
    for d, qkv_ref, tail_ref, _, ct_s, n_s, m_s in dirs:
        g = tail_ref[...] + bias_ref[...]
        ls = _log_sigmoid(g)
        g_t, ls_t = g.T, ls.T
        left, right = (lower, upper) if d == 0 else (upper, lower)
        cum_col = jnp.dot(left, ls, precision=HIGHEST, preferred_element_type=F32)
        cum_row = jnp.dot(ls_t, right, precision=HIGHEST, preferred_element_type=F32)
        qkv = qkv_ref[...]
        for hd in range(N_HEADS):
            ci = TAIL_GATE0 + 4 * (2 * d) + hd
            cf = ci + 4
            q, k, v = _head_slices(qkv, hd)
            chains.append(dict(
                d=d, hd=hd, q=q, k=k, v=v, li_row=g_t[ci:ci + 1, :], b_row=cum_row[cf:cf + 1, :],
                c_col=g[:, ci:ci + 1] - cum_col[:, cf:cf + 1],
                m_prev=m_s[d:d + 1, hd:hd + 1], ct=ct_s[d, hd], n=n_s[d, hd:hd + 1, :],
                refs=(ct_s, n_s, m_s)))
    for c in chains:
        c["s"] = _nt(c["k"], c["q"])
        c["v_t"] = c["v"].T
    for c in chains:
        valid = _tri(c["d"] != 0)
        log_d = jnp.where(valid, c["b_row"] + c["c_col"], -jnp.inf)
        log_inter = c["b_row"] + c["m_prev"]
        c["m_t"] = jnp.maximum(log_inter, jnp.max(log_d, axis=0, keepdims=True))
        c["w"] = c["s"] * k_scale * jnp.exp(log_d - c["m_t"])
        c["w_inter"] = jnp.exp(log_inter - c["m_t"])
    for c in chains:
        num = (jnp.dot(c["v_t"], c["w"].astype(BF16), preferred_element_type=F32)
               + c["w_inter"] * _nt(c["ct"].astype(BF16), c["q"]))
        den = jnp.sum(c["w"], axis=0, keepdims=True) + c["w_inter"] * _row_dot(c["n"], c["q"], True)
        c["h"] = num / jnp.maximum(jnp.abs(den), jnp.exp(-c["m_t"]))
    for c in chains:
        last = CHUNK - 1 if c["d"] == 0 else 0
        c["m_new"] = c["m_t"][:, last:last + 1]
        b_last = c["b_row"][:, last:last + 1]
        w_end_row = jnp.exp(b_last - c["b_row"] + c["li_row"] - c["m_new"])
        w_end_col = jnp.exp(c["c_col"] + (b_last - c["m_new"]))
        decay = jnp.exp(b_last + c["m_prev"] - c["m_new"])
        kw = (c["k"].astype(F32) * (w_end_col * k_scale)).astype(BF16)
        c["ct_new"] = decay * c["ct"] + jnp.dot(c["v_t"], kw, preferred_element_type=F32)
        c["n_new"] = decay * c["n"] + _row_dot(w_end_row, c["k"], False) * k_scale
    for j, entry in enumerate(dirs):
        entry[3][...] = jnp.concatenate([c["h"] for c in chains[j * N_HEADS:(j + 1) * N_HEADS]], axis=0).T
    for c in chains:
        d, hd = c["d"], c["hd"]
        ct_s, n_s, m_s = c["refs"]
        ct_s[d, hd] = c["ct_new"]
        n_s[d, hd:hd + 1, :] = c["n_new"]
        m_s[d:d + 1, hd:hd + 1] = c["m_new"]


def _ret_step(dirs, lg_ref):
    k_scale = HEAD_DIM ** -0.5
    s = lax.broadcasted_iota(jnp.int32, (CHUNK, CHUNK), 0)
    t = lax.broadcasted_iota(jnp.int32, (CHUNK, CHUNK), 1)
    lane = lax.broadcasted_iota(jnp.int32, (1, CHUNK), 1)
    chains = []
    for d, qkv_ref, _, st_s in dirs:
        lag = (t - s) if d == 0 else (s - t)
        pos = (lane if d == 0 else CHUNK - 1 - lane).astype(F32)
        qkv = qkv_ref[...]
        for hd in range(N_HEADS):
            q, k, v = _head_slices(qkv, hd)
            chains.append(dict(d=d, hd=hd, q=q, k=k, v=v, lag=lag, pos=pos, st=st_s[d, hd], ref=st_s,
                               lg=_log_sigmoid(lg_ref[d:d + 1, hd:hd + 1])))
    for c in chains:
        c["s"] = _nt(c["k"], c["q"])
        c["v_t"] = c["v"].T
    for c in chains:
        intra = jnp.where(c["lag"] >= 0, jnp.exp(jnp.maximum(c["lag"], 0).astype(F32) * c["lg"]), 0.0)
        c["a"] = (c["s"] * k_scale * intra).astype(BF16)
    for c in chains:
        inter = jnp.exp((c["pos"] + 1.0) * c["lg"])
        c["o"] = jnp.dot(c["v_t"], c["a"], preferred_element_type=F32) + inter * _nt(c["st"].astype(BF16), c["q"])
    row = lax.broadcasted_iota(jnp.int32, (CHUNK, HEAD_DIM), 0)
    for c in chains:
        src_pos = (row if c["d"] == 0 else CHUNK - 1 - row).astype(F32)
        tail = jnp.exp((CHUNK - 1.0 - src_pos) * c["lg"]) * k_scale
        kw = (c["k"].astype(F32) * tail).astype(BF16)
        c["st_new"] = jnp.exp(CHUNK * c["lg"]) * c["st"] + jnp.dot(c["v_t"], kw, preferred_element_type=F32)
    for j, entry in enumerate(dirs):
        entry[2][...] = jnp.concatenate([c["o"] for c in chains[j * N_HEADS:(j + 1) * N_HEADS]], axis=0).T
    for c in chains:
        c["ref"][c["d"], c["hd"]] = c["st_new"]


def _scan_kernel(tab_ref, mqf_ref, tailf_ref, mqb_ref, tailb_ref, rqf_ref, rqb_ref, bias_ref, lg_ref,
                 c0_ref, n0_ref, m0_ref, s0_ref,
                 hf_ref, hb_ref, of_ref, ob_ref, cn_ref, nn_ref, mn_ref, sn_ref, ct_s, n_s, m_s, st_s):
    i = pl.program_id(0)
    seqs = range(ct_s.shape[0])

    @pl.when(tab_ref[3, i] == 1)
    def _():
        for p in seqs:
            _load_state_t(ct_s.at[p], c0_ref.at[p])
            _load_state_t(st_s.at[p], s0_ref.at[p])
        n_s[...] = n0_ref[...]
        m_s[...] = m0_ref[...]

    mdirs, rdirs = [], []
    for p in seqs:
        state = (ct_s.at[p], n_s.at[p], m_s.at[p])
        mdirs += [(0, mqf_ref.at[p], tailf_ref.at[p], hf_ref.at[p]) + state,
                  (1, mqb_ref.at[p], tailb_ref.at[p], hb_ref.at[p]) + state]
        rdirs += [(0, rqf_ref.at[p], of_ref.at[p], st_s.at[p]), (1, rqb_ref.at[p], ob_ref.at[p], st_s.at[p])]
    _mlstm_step(mdirs, bias_ref)
    _ret_step(rdirs, lg_ref)

    @pl.when(tab_ref[4, i] == 1)
    def _():
        for p in seqs:
            _load_state_t(cn_ref.at[p], ct_s.at[p])
            _load_state_t(sn_ref.at[p], st_s.at[p])
        nn_ref[...] = n_s[...]
        mn_ref[...] = m_s[...]


def _scan_table(st):
    nch = st["n"] // CHUNK
    rows = [(c, nch - 1 - c, g, int(c == 0), int(c == nch - 1))
            for g in range(st["nb"] // _scan_group(st)) for c in range(nch)]
    return jnp.asarray(np.array(rows, dtype=np.int32).T)


def _scan_group(st):
    return next(g for g in (4, 2, 1) if st["nb"] % g == 0)


def _scans(tab, mqkv, tail, bias_row, rqkv, decay_logit, c0, n0, m0, s0, st):
    nb, n, grp = st["nb"], st["n"], _scan_group(st)
    seq3 = lambda a: a.reshape(nb, n, a.shape[-1])
    fwd = lambda i, tab: (tab[2, i], tab[0, i], 0)
    bwd = lambda i, tab: (tab[2, i], tab[1, i], 0)
    const = lambda i, tab: (0, 0)
    rows = lambda width, where: pl.BlockSpec((grp, CHUNK, width), where)
    state = lambda a: pl.BlockSpec((grp,) + a.shape[1:], lambda i, tab: (tab[2, i],) + (0,) * (a.ndim - 1))
    states = (c0, n0, m0, s0)
    grid_spec = pltpu.PrefetchScalarGridSpec(
        num_scalar_prefetch=1,
        grid=(tab.shape[1],),
        in_specs=[rows(QKV_SLOTS, fwd), rows(LANES, fwd), rows(QKV_SLOTS, bwd), rows(LANES, bwd),
                  rows(QKV_SLOTS, fwd), rows(QKV_SLOTS, bwd),
                  pl.BlockSpec((1, LANES), const), pl.BlockSpec(decay_logit.shape, const)]
        + [state(a) for a in states],
        out_specs=[rows(GROUP_WIDTH, fwd), rows(GROUP_WIDTH, bwd), rows(GROUP_WIDTH, fwd), rows(GROUP_WIDTH, bwd)]
        + [state(a) for a in states],
        scratch_shapes=[pltpu.VMEM((grp,) + a.shape[1:], F32) for a in states],
    )
    outs = pl.pallas_call(
        _scan_kernel,
        grid_spec=grid_spec,
        out_shape=[jax.ShapeDtypeStruct((nb, n, GROUP_WIDTH), F32)] * 4
        + [jax.ShapeDtypeStruct(a.shape, F32) for a in states],
        compiler_params=_cparams(("arbitrary",)),
        name="recurrent_scans",
    )(tab, seq3(mqkv), seq3(tail), seq3(mqkv), seq3(tail), seq3(rqkv), seq3(rqkv), bias_row, decay_logit, *states)
    return [o.reshape(nb * n, GROUP_WIDTH) for o in outs[:4]] + list(outs[4:])


def _proj_kernel(x_ref, w_ref, o_ref):
    o_ref[...] = _bdot(x_ref[...], w_ref[...]).astype(o_ref.dtype)


def _project(x, w, dtype):
    return pl.pallas_call(
        _proj_kernel,
        out_shape=jax.ShapeDtypeStruct((x.shape[0], w.shape[1]), dtype),
        name="ctx_kv_proj",
    )(x, w)


def _first_argmax_mask(cur, axis, size):
    io = lax.broadcasted_iota(jnp.int32, cur.shape, axis)
    mx = jnp.max(cur, axis=axis, keepdims=True)
    ix = jnp.min(jnp.where(cur == mx, io, size), axis=axis, keepdims=True)
    return io == ix


def _route(scores_t, bias_col):
    tm = scores_t.shape[1]
    per = N_EXPERTS // N_GROUPS
    sel = scores_t + bias_col
    s3 = sel.reshape(N_GROUPS, per, tm)
    hit1 = _first_argmax_mask(s3, 1, per)
    m1 = jnp.max(s3, axis=1, keepdims=True)
    m2 = jnp.max(jnp.where(hit1, -jnp.inf, s3), axis=1, keepdims=True)
    cur = m1 + m2
    gsel = None
    for _ in range(TOPK_GROUPS):
        hit = _first_argmax_mask(cur, 0, N_GROUPS)
        gsel = hit if gsel is None else jnp.logical_or(gsel, hit)
        cur = jnp.where(hit, -jnp.inf, cur)
    cur = jnp.where(gsel, s3, -jnp.inf).reshape(N_EXPERTS, tm)
    chosen, hits = None, []
    for _ in range(TOP_K):
        hit = _first_argmax_mask(cur, 0, N_EXPERTS)
        hits.append(hit)
        chosen = hit if chosen is None else jnp.logical_or(chosen, hit)
        cur = jnp.where(hit, -jnp.inf, cur)
    w = jnp.where(chosen, scores_t, 0.0)
    return w / jnp.sum(w, axis=0, keepdims=True) * ROUTED_SCALE, chosen, hits


def _out_kernel(oa_ref, od_ref, hf_ref, hb_ref, mo_ref, of_ref, ob_ref, rg_ref,
                x_ref, mod_ref, mg_ref, npost_ref, npre_ref, wout_ref, wrt_ref, rb_ref,
                x1_ref, h2_ref, hpk_ref, eidx_ref, rank_ref, wk_ref, cnt_ref, count_s):
    @pl.when(pl.program_id(0) == 0)
    def _():
        count_s[...] = jnp.zeros(count_s.shape, F32)

    a = oa_ref[...]
    dd = od_ref[...]
    b = _rms_heads(hf_ref[...] + hb_ref[...]) * mg_ref[...] * _sigmoid(mo_ref[...])
    r = _rms_heads(of_ref[...] + ob_ref[...]) * _silu(rg_ref[...])
    mix = sum(jnp.dot(part.astype(BF16), wout_ref[j * GROUP_WIDTH:(j + 1) * GROUP_WIDTH, :],
                      preferred_element_type=F32) for j, part in enumerate((a, b, r, dd)))
    m = mod_ref[0]
    x1 = x_ref[...] + m[2:3] * (_rms(mix) * npost_ref[...])
    x1_ref[...] = x1
    h2 = (_rms(x1) * npre_ref[...] * (1.0 + m[4:5]) + m[3:4]).astype(BF16)
    h2_ref[...] = h2
    hpk_ref[...] = _pack_rows(h2)

    logits_t = lax.dot_general(wrt_ref[...], h2, (((1,), (1,)), ((), ())), preferred_element_type=F32)
    w_t, chosen, hits = _route(_sigmoid(logits_t), rb_ref[...])
    tm = w_t.shape[1]
    src = lax.broadcasted_iota(jnp.int32, (tm, tm), 0)
    dst = lax.broadcasted_iota(jnp.int32, (tm, tm), 1)
    before = jnp.where(src < dst, 1.0, 0.0).astype(BF16)
    picked = jnp.where(chosen, 1.0, 0.0)
    rank = jnp.dot(picked.astype(BF16), before, preferred_element_type=F32) + count_s[...]
    count_s[...] += jnp.sum(picked, axis=1, keepdims=True)
    cnt_ref[...] = jnp.broadcast_to(count_s[...], cnt_ref.shape)
    e_io = lax.broadcasted_iota(jnp.int32, w_t.shape, 0)
    pick = lambda hit, val: jnp.sum(jnp.where(hit, val, jnp.zeros_like(val)), axis=0, keepdims=True)
    eidx_ref[...] = jnp.concatenate([pick(h, e_io) for h in hits], axis=0)
    rank_ref[...] = jnp.concatenate([pick(h, rank) for h in hits], axis=0).astype(jnp.int32)
    wk = jnp.concatenate([pick(h, w_t) for h in hits] + [jnp.zeros((LANES - TOP_K, tm), F32)], axis=0)
    wk_ref[...] = wk.T


def _output_stage(oa, od, hf, hb, mo, of, ob, rg, x, mod, mg, npost, npre, wout, wrt, rb, st):
    t_all = x.shape[0]
    row = lambda i: (i, 0)
    const2 = lambda i: (0, 0)
    tok = lambda w: pl.BlockSpec((TM, w), row)
    full = lambda a: pl.BlockSpec(a.shape, const2)
    return pl.pallas_call(
        _out_kernel,
        grid=(t_all // TM,),
        in_specs=[tok(GROUP_WIDTH)] * 8 + [tok(D_MODEL), _mod_spec(st),
                                   full(mg), full(npost), full(npre), full(wout), full(wrt), full(rb)],
        out_specs=[tok(D_MODEL), tok(D_MODEL), tok(D_MODEL // 2),
                   pl.BlockSpec((TOP_K, TM), lambda i: (0, i)), pl.BlockSpec((TOP_K, TM), lambda i: (0, i)),
                   tok(LANES), pl.BlockSpec((N_EXPERTS, LANES), const2)],
        out_shape=[jax.ShapeDtypeStruct((t_all, D_MODEL), F32),
                   jax.ShapeDtypeStruct((t_all, D_MODEL), BF16),
                   jax.ShapeDtypeStruct((t_all, D_MODEL // 2), jnp.uint32),
                   jax.ShapeDtypeStruct((TOP_K, t_all), jnp.int32),
                   jax.ShapeDtypeStruct((TOP_K, t_all), jnp.int32),
                   jax.ShapeDtypeStruct((t_all, LANES), F32),
                   jax.ShapeDtypeStruct((N_EXPERTS, LANES), F32)],
        scratch_shapes=[pltpu.VMEM((N_EXPERTS, 1), F32)],
        compiler_params=_cparams(("arbitrary",)),
        name="output_stage_" + st["name"],
    )(oa, od, hf, hb, mo, of, ob, rg, x, mod, mg, npost, npre, wout, wrt, rb)


SC_CORES, SC_SUBCORES = 2, 16
SC_WORKERS = SC_CORES * SC_SUBCORES
SLOT_BLOCK = 512
DISPATCH_ROWS = 64
COMBINE_ROWS = 64


def _sc_mesh():
    return plsc.VectorSubcoreMesh(core_axis_name="core", subcore_axis_name="subcore")


def _sc_worker():
    return lax.axis_index("subcore") * SC_CORES + lax.axis_index("core")


def _sc_dispatch(hpk, dest, n_slots):
    t_all, width = hpk.shape
    per_worker = t_all // SC_WORKERS
    assert t_all % (SC_WORKERS * DISPATCH_ROWS) == 0

    @functools.partial(
        pl.kernel, mesh=_sc_mesh(), out_type=jax.ShapeDtypeStruct((n_slots, width), hpk.dtype),
        scratch_types=[pltpu.VMEM((DISPATCH_ROWS, width), hpk.dtype)]
        + [pltpu.VMEM((DISPATCH_ROWS,), jnp.int32)] * TOP_K + [pltpu.SemaphoreType.DMA])
    def dispatch(x_hbm, d_hbm, o_hbm, rows_v, *rest):
        idx, sem = rest[:TOP_K], rest[TOP_K]

        @pl.loop(0, per_worker // DISPATCH_ROWS)
        def _(j):
            base = _sc_worker() * per_worker + j * DISPATCH_ROWS
            pltpu.sync_copy(x_hbm.at[pl.ds(base, DISPATCH_ROWS)], rows_v)
            for k in range(TOP_K):
                pltpu.sync_copy(d_hbm.at[pl.ds(k * t_all + base, DISPATCH_ROWS)], idx[k])
            copies = [pltpu.async_copy(rows_v, o_hbm.at[idx[k]], sem) for k in range(TOP_K)]
            for c in copies:
                c.wait()

    return dispatch(hpk, dest)


def _sc_combine(yb, dest, t_all):
    width = yb.shape[1]
    per_worker = t_all // SC_WORKERS
    assert t_all % (SC_WORKERS * COMBINE_ROWS) == 0

    @functools.partial(
        pl.kernel, mesh=_sc_mesh(), out_type=jax.ShapeDtypeStruct((TOP_K, t_all, width), yb.dtype),
        scratch_types=[pltpu.VMEM((COMBINE_ROWS, width), yb.dtype)] * 2
        + [pltpu.VMEM((COMBINE_ROWS,), jnp.int32)] * TOP_K + [pltpu.SemaphoreType.DMA] * 2)
    def combine(y_hbm, d_hbm, o_hbm, rows_a, rows_b, *rest):
        idx, sems = rest[:TOP_K], rest[TOP_K:]
        bufs = (rows_a, rows_b)

        @pl.loop(0, per_worker // COMBINE_ROWS)
        def _(j):
            base = _sc_worker() * per_worker + j * COMBINE_ROWS
            for k in range(TOP_K):
                pltpu.sync_copy(d_hbm.at[pl.ds(k * t_all + base, COMBINE_ROWS)], idx[k])
            pending = pltpu.async_copy(y_hbm.at[idx[0]], bufs[0], sems[0])
            for k in range(TOP_K):
                pending.wait()
                if k + 1 < TOP_K:
                    pending = pltpu.async_copy(y_hbm.at[idx[k + 1]], bufs[(k + 1) % 2], sems[(k + 1) % 2])
                pltpu.sync_copy(bufs[k % 2], o_hbm.at[k, pl.ds(base, COMBINE_ROWS)])

    return combine(yb, dest)


def _pack_rows(x):
    bits = pltpu.bitcast(x.astype(BF16).astype(F32), jnp.uint32)
    w = x.shape[-1] // 2
    return (bits[..., :w] >> 16) | (bits[..., w:] & jnp.uint32(0xFFFF0000))


def _unpack_rows(words):
    return pltpu.bitcast(words << 16, F32), pltpu.bitcast(words & jnp.uint32(0xFFFF0000), F32)


def _expert_kernel(be_ref, used_ref, xs_ref, wg_ref, wu_ref, wd_ref, y_ref, wg_s, wu_s, wd_s):
    b = pl.program_id(0)
    fresh = jnp.logical_or(b == 0, be_ref[b] != be_ref[jnp.maximum(b - 1, 0)])

    @pl.when(jnp.logical_and(fresh, b < used_ref[0]))
    def _():
        wg_s[...] = wg_ref[0, 0].astype(BF16)
        wu_s[...] = wu_ref[0, 0].astype(BF16)
        wd_s[...] = wd_ref[0, 0].astype(BF16)

    @pl.when(b < used_ref[0])
    def _():
        lo, hi = (v.astype(BF16) for v in _unpack_rows(xs_ref[...]))
        half = D_MODEL // 2
        gate = (jnp.dot(lo, wg_s[:half, :], preferred_element_type=F32)
                + jnp.dot(hi, wg_s[half:, :], preferred_element_type=F32))
        up = (jnp.dot(lo, wu_s[:half, :], preferred_element_type=F32)
              + jnp.dot(hi, wu_s[half:, :], preferred_element_type=F32))
        y = jnp.dot((_silu(gate) * up).astype(BF16), wd_s[...], preferred_element_type=F32)
        y_ref[...] = _pack_rows(y)


def _expert_blocks(layer, xs, block_expert, blocks_used, wg, wu, wdn):
    n_slots = xs.shape[0]
    wspec = lambda shape: pl.BlockSpec((1, 1) + shape, lambda b, be, used: (layer, be[b], 0, 0))
    rows = lambda b, be, used: (jnp.minimum(b, used[0] - 1), 0)
    grid_spec = pltpu.PrefetchScalarGridSpec(
        num_scalar_prefetch=2,
        grid=(n_slots // SLOT_BLOCK,),
        in_specs=[pl.BlockSpec((SLOT_BLOCK, D_MODEL // 2), rows),
                  wspec((D_MODEL, D_EXPERT)), wspec((D_MODEL, D_EXPERT)), wspec((D_EXPERT, D_MODEL))],
        out_specs=pl.BlockSpec((SLOT_BLOCK, D_MODEL // 2), rows),
        scratch_shapes=[pltpu.VMEM((D_MODEL, D_EXPERT), BF16), pltpu.VMEM((D_MODEL, D_EXPERT), BF16),
                        pltpu.VMEM((D_EXPERT, D_MODEL), BF16)])
    return pl.pallas_call(
        _expert_kernel, grid_spec=grid_spec,
        out_shape=jax.ShapeDtypeStruct((n_slots, D_MODEL // 2), jnp.uint32),
        compiler_params=_cparams(("arbitrary",)),
        name="moe_experts",
    )(block_expert, blocks_used, xs, wg, wu, wdn)


def _moe_out_kernel(g_ref, wk_ref, h_ref, x1_ref, mod_ref, npost_ref, sg_ref, su_ref, sd_ref, o_ref):
    h = h_ref[...]
    act = _silu(jnp.dot(h, sg_ref[...], preferred_element_type=F32)) * jnp.dot(h, su_ref[...],
                                                                              preferred_element_type=F32)
    moe = jnp.dot(act.astype(BF16), sd_ref[...], preferred_element_type=F32)
    wk = wk_ref[...]
    lo, hi = 0.0, 0.0
    for k in range(TOP_K):
        g_lo, g_hi = _unpack_rows(g_ref[k])
        lo = lo + g_lo * wk[:, k:k + 1]
        hi = hi + g_hi * wk[:, k:k + 1]
    moe = moe + jnp.concatenate([lo, hi], axis=-1)
    m = mod_ref[0]
    o_ref[...] = x1_ref[...] + m[5:6] * (_rms(moe) * npost_ref[...])


def _moe_out_stage(g, wk, h2, x1, mod, npost, sg, su, sd, st):
    t_all = h2.shape[0]
    row = lambda i: (i, 0)
    full = lambda a: pl.BlockSpec(a.shape, lambda i: (0, 0))
    return pl.pallas_call(
        _moe_out_kernel,
        grid=(t_all // TM,),
        in_specs=[pl.BlockSpec((TOP_K, TM, D_MODEL // 2), lambda i: (0, i, 0)), pl.BlockSpec((TM, LANES), row),
                  pl.BlockSpec((TM, D_MODEL), row), pl.BlockSpec((TM, D_MODEL), row), _mod_spec(st),
                  full(npost), full(sg), full(su), full(sd)],
        out_specs=pl.BlockSpec((TM, D_MODEL), row),
        out_shape=jax.ShapeDtypeStruct((t_all, D_MODEL), F32),
        compiler_params=_cparams(("parallel",)),
        name="moe_combine_" + st["name"],
    )(g, wk, h2, x1, mod, npost, sg, su, sd)


def _moe_stage(layer, h2, hpk, eidx, rank, wk, counts, x1, mod, npost, wg, wu, wdn, sg, su, sd, st):
    t_all = h2.shape[0]
    n_blocks = -(-(t_all * TOP_K + N_EXPERTS * (SLOT_BLOCK - 1)) // SLOT_BLOCK)
    cnt = counts[:, 0].astype(jnp.int32)
    padded = (cnt + SLOT_BLOCK - 1) // SLOT_BLOCK * SLOT_BLOCK
    pad_end = jnp.cumsum(padded)
    pad_start = pad_end - padded
    experts = jnp.arange(N_EXPERTS, dtype=jnp.int32)[:, None, None]
    dest = rank + jnp.sum(jnp.where(eidx[None] == experts, pad_start[:, None, None], 0), axis=0)
    dest = dest.reshape(TOP_K * t_all)
    block_start = jnp.arange(n_blocks, dtype=jnp.int32) * SLOT_BLOCK
    block_expert = jnp.minimum(jnp.sum((pad_end[None, :] <= block_start[:, None]).astype(jnp.int32), axis=1),
                               N_EXPERTS - 1)
    blocks_used = (pad_end[-1:] // SLOT_BLOCK).astype(jnp.int32)

    xs = _sc_dispatch(hpk, dest, n_blocks * SLOT_BLOCK)
    yb = _expert_blocks(layer, xs, block_expert, blocks_used, wg, wu, wdn)
    g = _sc_combine(yb, dest, t_all)
    return _moe_out_stage(g, wk, h2, x1, mod, npost, sg, su, sd, st)


_PERM32 = np.concatenate([np.arange(8, 16), np.arange(0, 8), np.arange(24, 32), np.arange(16, 24)])


def _rope_tables(n):
    pos = jnp.arange(n)
    quarter = QK_ROPE // 4
    inv = 1.0 / (ROPE_THETA ** (jnp.arange(quarter, dtype=F32) / quarter))
    ang_r = (pos // GRID_W).astype(F32)[:, None] * inv[None, :]
    ang_c = (pos % GRID_W).astype(F32)[:, None] * inv[None, :]
    cos32 = jnp.concatenate([jnp.cos(ang_r)] * 2 + [jnp.cos(ang_c)] * 2, axis=-1)
    sin32 = jnp.concatenate([-jnp.sin(ang_r), jnp.sin(ang_r), -jnp.sin(ang_c), jnp.sin(ang_c)], axis=-1)
    cos256, sin256 = jnp.tile(cos32, (1, 8)), jnp.tile(sin32, (1, 8))
    one, zero = jnp.ones((n, QK_NOPE), F32), jnp.zeros((n, QK_NOPE), F32)
    cosq = jnp.concatenate([one, cos32] * N_HEADS, axis=-1)
    sinq = jnp.concatenate([zero, sin32] * N_HEADS, axis=-1)
    return cos256, sin256, cosq, sinq


def _layer_weights(w_in_l, w_uq_l):
    gw = GROUP_WIDTH
    g0 = W_BEFORE_GATES
    b0 = g0 + 4 * N_HEADS
    kr0 = b0 + W_AFTER_GATES
    assert kr0 + QK_ROPE == w_in_l.shape[1]
    w_a, m_g, w_b, d_kr = w_in_l[:, :g0], w_in_l[:, g0:b0], w_in_l[:, b0:kr0], w_in_l[:, kr0:]
    perm256 = np.concatenate([_PERM32 + 32 * j for j in range(8)])
    pad = lambda w, n: jnp.pad(w, ((0, 0), (0, n - w.shape[1])))
    w_c = jnp.concatenate([pad(jnp.concatenate([d_kr, m_g], axis=-1), LANES),
                           w_a[:, perm256], w_a[:, gw + perm256], pad(d_kr[:, _PERM32], LANES)], axis=-1)
    per = QK_NOPE + QK_ROPE
    permq = np.concatenate([np.concatenate([np.arange(QK_NOPE), QK_NOPE + _PERM32]) + per * j
                            for j in range(N_HEADS)])
    return ((w_a.astype(BF16), w_b.astype(BF16), w_c.astype(BF16)), w_uq_l.astype(BF16),
            w_uq_l[:, permq].astype(BF16))


def kernel(x_prompt, x_sample, cache_diff_k, cache_diff_v, state_mlstm_C, state_mlstm_n, state_mlstm_m, state_ret_S, cache_mla_ckv, cache_mla_krope, c, c_ctx, w_mod, b_mod, norm_pre, norm_post, w_in, w_out, diff_lambda, diff_norm, mlstm_gate_bias, mlstm_norm, ret_decay_logit, mla_q_norm, mla_w_uq, mla_kv_norm, mla_w_ukv, moe_w_router, moe_router_bias, moe_w_gate, moe_w_up, moe_w_down, shared_w_gate, shared_w_up, shared_w_down):
    bp, n_p, _ = x_prompt.shape
    bs, n_s, _ = x_sample.shape
    depth = w_in.shape[0]
    past = cache_diff_k.shape[2]
    assert (bp * n_p) % TM == 0 and n_s % TM == 0 and n_p % CHUNK == 0 and n_s % min(TK, n_s) == 0
    assert past % min(TK, past) == 0 and bs + 1 <= 8 and n_s % GRID_W == 0

    streams = (dict(name="ctx", nb=bp, n=n_p, latent=False), dict(name="latent", nb=bs, n=n_s, latent=True))
    xs_by_stream = [x_prompt.reshape(bp * n_p, D_MODEL), x_sample.reshape(bs * n_s, D_MODEL)]
    cond = jnp.zeros((8, D_MODEL), F32).at[0].set(c_ctx).at[1:1 + bs].set(c)
    mod_all = _modulation(cond, w_mod, b_mod).reshape(depth, 8, 6, D_MODEL)
    tabs = _rope_tables(n_s)
    scan_tabs = [_scan_table(st) for st in streams]

    outs = [[] for _ in range(8)]
    for l in range(depth):
        lam_init = 0.8 - 0.6 * math.exp(-0.3 * l)
        mod = mod_all[l]
        w_all, wuq, wuqs = _layer_weights(w_in[l], mla_w_uq[l])
        wukv = mla_w_ukv[l].astype(BF16)
        wout, wrt, rbias = w_out[l].astype(BF16), moe_w_router[l].T.astype(BF16), moe_router_bias[l][:, None]
        shared = [w[l].astype(BF16) for w in (shared_w_gate, shared_w_up, shared_w_down)]
        lamv, dg = diff_lambda[l], diff_norm[l][:, None]
        bias_row = jnp.zeros((1, LANES), F32).at[0, TAIL_GATE0:TAIL_GATE0 + 16].set(mlstm_gate_bias[l].reshape(16))

        with_ones = lambda vt: jnp.concatenate([vt, jnp.ones((N_HEADS, VT_ONES, bs * past), BF16)], axis=1)
        ck = jnp.transpose(cache_diff_k[:, l], (2, 0, 1, 3)).reshape(N_HEADS, bs * past, HEAD_DIM).astype(BF16)
        cvt = with_ones(jnp.transpose(cache_diff_v[:, l], (2, 3, 0, 1)).reshape(N_HEADS, HEAD_DIM, bs * past)
                        .astype(BF16))
        kvc = _project(cache_mla_ckv[:, l].reshape(bs * past, KV_LORA), wukv, BF16)
        kvc = kvc.reshape(bs * past, N_HEADS, QK_NOPE + V_HEAD)
        krc = jnp.broadcast_to(cache_mla_krope[:, l].reshape(bs * past, 1, QK_ROPE).astype(BF16),
                               (bs * past, N_HEADS, QK_ROPE))
        kc = jnp.transpose(jnp.concatenate([kvc[..., :QK_NOPE], krc], axis=-1), (1, 0, 2))
        vct = with_ones(jnp.transpose(kvc[..., QK_NOPE:], (1, 2, 0)))

        for si, st in enumerate(streams):
            x, nb, n = xs_by_stream[si], st["nb"], st["n"]
            (aq1t, aq2t, ak1, ak2, avt, ak, av, mqkv, mo, rqkv, rg, tail, qmlat, ckv, kmla, vmlat) = _input_stage(
                x, mod, norm_pre[l, 0:1], w_all, tabs, mla_q_norm[l][None], wuq, wuqs,
                mla_kv_norm[l][None], wukv, st)
            if st["latent"]:
                oa = _attention_t("diff_attn_latent", [aq1t, aq2t], [ak1, ak2], avt,
                                  [ck[..., :DIFF_HALF], ck[..., DIFF_HALF:]], cvt, [lamv, dg], nb, n, past, lam_init)
                od = _attention_t("mla_attn_latent", [qmlat], [kmla], vmlat, [kc], vct, [], nb, n, past)
                states = [s[:, l] for s in (state_mlstm_C, state_mlstm_n, state_mlstm_m, state_ret_S)]
            else:
                oa = _attention_t("diff_attn_ctx", [aq1t, aq2t], [ak1, ak2], avt, None, None, [lamv, dg],
                                  nb, n, 0, lam_init)
                od = _attention_t("mla_attn_ctx", [qmlat], [kmla], vmlat, None, None, [], nb, n, 0)
                states = [jnp.zeros((nb,) + s.shape[2:], F32)
                          for s in (state_mlstm_C, state_mlstm_n, state_mlstm_m, state_ret_S)]
            hf, hb, of, ob, c_n, n_n, m_n, s_n = _scans(scan_tabs[si], mqkv, tail, bias_row, rqkv,
                                                        ret_decay_logit[l], *states, st)

            x1, h2, hpk, eidx, rank, wk, counts = _output_stage(
                oa, od, hf, hb, mo, of, ob, rg, x, mod, mlstm_norm[l][None], norm_post[l, 0:1],
                norm_pre[l, 1:2], wout, wrt, rbias, st)
            xs_by_stream[si] = _moe_stage(l, h2, hpk, eidx, rank, wk, counts, x1, mod, norm_post[l, 1:2],
                                          moe_w_gate, moe_w_up, moe_w_down, *shared, st)
            if not st["latent"]:
                new = (ak.reshape(nb, n, N_HEADS, HEAD_DIM), av.reshape(nb, n, N_HEADS, HEAD_DIM), c_n, n_n, m_n,
                       s_n, ckv.reshape(nb, n, KV_LORA), tail[:, :QK_ROPE].reshape(nb, n, QK_ROPE))
                for o, v in zip(outs, new):
                    o.append(v)

    return (xs_by_stream[0].reshape(bp, n_p, D_MODEL), xs_by_stream[1].reshape(bs, n_s, D_MODEL)) + tuple(
        jnp.stack(o, axis=1) for o in outs)
```

```python
import functools
import math

import numpy as np
import jax
import jax.numpy as jnp
from jax import lax
from jax.experimental import pallas as pl
from jax.experimental.pallas import tpu as pltpu
from jax.experimental.pallas import tpu_sc as plsc

F32 = jnp.float32
BF16 = jnp.bfloat16
HIGHEST = lax.Precision.HIGHEST

D_MODEL = 1024
GRID_W = 64
GROUP_WIDTH = 256
HEAD_DIM = 64
N_HEADS = 4
DIFF_HALF = 32
ROPE_THETA = 10000.0
Q_LORA = 256
KV_LORA = 128
QK_NOPE = 64
QK_ROPE = 32
V_HEAD = 64
N_EXPERTS = 64
TOP_K = 8
N_GROUPS = 8
TOPK_GROUPS = 4
D_EXPERT = 256
ROUTED_SCALE = 2.5
CHUNK = 128
EPS = 1e-6
Q_MLA = N_HEADS * (QK_NOPE + QK_ROPE)

LANES = 128
VMEM_LIMIT = 56 * 1024 * 1024

TM = 512
TQ = 256
TK = 4096

C_AQ, C_AK, C_AV, C_MQKV, C_MO = (0, 0), (0, 256), (0, 512), (0, 768), (0, 1536)
C_RQKV, C_RG, C_CQ, C_CKV = (1, 0), (1, 768), (1, 1024), (1, 1280)
C_TAIL, C_AQS, C_AKS, C_TAILS = (2, 0), (2, 128), (2, 384), (2, 640)
W_BEFORE_GATES, W_AFTER_GATES = 1792, 1408
TAIL_GATE0 = QK_ROPE
QKV_SLOTS = 3 * N_HEADS * LANES
VT_ONES = 16
VT_ROWS = V_HEAD + VT_ONES


def _cparams(sem, flags=None):
    return pltpu.CompilerParams(dimension_semantics=sem, vmem_limit_bytes=VMEM_LIMIT, flags=flags)


def _rms(x):
    return x * lax.rsqrt(jnp.mean(x * x, axis=-1, keepdims=True) + EPS)


def _head_mean_matrix(width):
    r = lax.broadcasted_iota(jnp.int32, (width, width), 0) // HEAD_DIM
    c = lax.broadcasted_iota(jnp.int32, (width, width), 1) // HEAD_DIM
    return jnp.where(r == c, 1.0 / HEAD_DIM, 0.0).astype(BF16)


def _rms_heads(x):
    sq = x * x
    hi = sq.astype(BF16)
    lo = (sq - hi.astype(F32)).astype(BF16)
    g = _head_mean_matrix(x.shape[-1])
    ms = jnp.dot(hi, g, preferred_element_type=F32) + jnp.dot(lo, g, preferred_element_type=F32)
    return x * lax.rsqrt(ms + EPS)


def _sigmoid(x):
    return 1.0 / (1.0 + jnp.exp(-x))


def _silu(x):
    return x * _sigmoid(x)


def _log_sigmoid(x):
    return jnp.minimum(x, 0.0) - jnp.log1p(jnp.exp(-jnp.abs(x)))


def _bdot(a, b):
    return jnp.dot(a.astype(BF16), b.astype(BF16), preferred_element_type=F32)


def _mod_kernel(c_ref, w_ref, b_ref, o_ref):
    o_ref[0] = _bdot(_silu(c_ref[...]), w_ref[0]) + b_ref[0]


def _modulation(cond, w_mod, b_mod):
    depth, _, n = w_mod.shape
    tn = 1536
    return pl.pallas_call(
        _mod_kernel,
        grid=(depth, n // tn),
        in_specs=[pl.BlockSpec((8, D_MODEL), lambda l, j: (0, 0)),
                  pl.BlockSpec((1, D_MODEL, tn), lambda l, j: (l, 0, j)),
                  pl.BlockSpec((1, 1, tn), lambda l, j: (l, 0, j))],
        out_specs=pl.BlockSpec((1, 8, tn), lambda l, j: (l, 0, j)),
        out_shape=jax.ShapeDtypeStruct((depth, 8, n), F32),
        compiler_params=_cparams(("parallel", "parallel")),
        name="adaln_mod",
    )(cond, w_mod, b_mod.reshape(depth, 1, n))


def _in_kernel(x_ref, mod_ref, npre_ref, wa_ref, wb_ref, wc_ref, cos_ref, sin_ref, cosq_ref, sinq_ref,
               qg_ref, wuq_ref, wuqs_ref, kvg_ref, wukv_ref,
               aq1_ref, aq2_ref, ak1t_ref, ak2t_ref, avh_ref, ak_ref, av_ref,
               mqkv_ref, mo_ref, rqkv_ref, rg_ref, tail_ref,
               qmla_ref, ckv_ref, kmlat_ref, vmla_ref, *, latent):
    m = mod_ref[0]
    h = (_rms(x_ref[...]) * npre_ref[...] * (1.0 + m[1:2]) + m[0:1]).astype(BF16)

    def proj(col, width):
        w_ref = (wa_ref, wb_ref, wc_ref)[col[0]]
        return jnp.dot(h, w_ref[:, col[1]:col[1] + width], preferred_element_type=F32)

    def rotated(c0, c0_swapped, width, cos, sin):
        return proj(c0, width) * cos + proj(c0_swapped, width) * sin if latent else proj(c0, width)

    cos = cos_ref[...]
    sin = sin_ref[...]
    aq = rotated(C_AQ, C_AQS, GROUP_WIDTH, cos, sin) * (DIFF_HALF ** -0.5)
    ak = rotated(C_AK, C_AKS, GROUP_WIDTH, cos, sin)
    av = proj(C_AV, GROUP_WIDTH)
    ak_ref[...] = ak
    av_ref[...] = av
    aq_t, av_t = aq.T.astype(BF16), av.T.astype(BF16)
    ones = jnp.ones((VT_ONES, aq_t.shape[1]), BF16)
    for hd in range(N_HEADS):
        lo = hd * HEAD_DIM
        aq1_ref[hd] = aq_t[lo:lo + DIFF_HALF, :]
        aq2_ref[hd] = aq_t[lo + DIFF_HALF:lo + HEAD_DIM, :]
        ak1t_ref[hd] = ak[:, lo:lo + DIFF_HALF].astype(BF16)
        ak2t_ref[hd] = ak[:, lo + DIFF_HALF:lo + HEAD_DIM].astype(BF16)
        avh_ref[hd, 0:HEAD_DIM, :] = av_t[lo:lo + HEAD_DIM, :]
        avh_ref[hd, HEAD_DIM:VT_ROWS, :] = ones

    def head_slots(x):
        x = x.astype(BF16)
        zero = jnp.zeros((x.shape[0], LANES - HEAD_DIM), BF16)
        parts = []
        for j in range(3 * N_HEADS):
            parts += [x[:, j * HEAD_DIM:(j + 1) * HEAD_DIM], zero]
        return jnp.concatenate(parts, axis=-1)

    mqkv_ref[...] = head_slots(proj(C_MQKV, 3 * GROUP_WIDTH))
    mo_ref[...] = proj(C_MO, GROUP_WIDTH)
    rqkv_ref[...] = head_slots(proj(C_RQKV, 3 * GROUP_WIDTH))
    rg_ref[...] = proj(C_RG, GROUP_WIDTH)

    tail = proj(C_TAIL, LANES)
    tail_ref[...] = tail
    kr = tail * cos[:, :LANES] + proj(C_TAILS, LANES) * sin[:, :LANES] if latent else tail
    kr = kr[:, :QK_ROPE].astype(BF16)

    cqn = (_rms(proj(C_CQ, Q_LORA)) * qg_ref[...]).astype(BF16)
    qmla = jnp.dot(cqn, wuq_ref[...], preferred_element_type=F32)
    if latent:
        qmla = qmla * cosq_ref[...] + jnp.dot(cqn, wuqs_ref[...], preferred_element_type=F32) * sinq_ref[...]
    qmla_t = (qmla * ((QK_NOPE + QK_ROPE) ** -0.5)).T.astype(BF16)
    for hd in range(N_HEADS):
        lo = hd * (QK_NOPE + QK_ROPE)
        qmla_ref[hd] = qmla_t[lo:lo + QK_NOPE + QK_ROPE, :]
    ckvn = _rms(proj(C_CKV, KV_LORA)) * kvg_ref[...]
    ckv_ref[...] = ckvn
    kv = jnp.dot(ckvn.astype(BF16), wukv_ref[...], preferred_element_type=F32)
    kv_t = kv.T.astype(BF16)
    per = QK_NOPE + V_HEAD
    for hd in range(N_HEADS):
        kmlat_ref[hd, :, 0:QK_NOPE] = kv[:, hd * per:hd * per + QK_NOPE].astype(BF16)
        kmlat_ref[hd, :, QK_NOPE:QK_NOPE + QK_ROPE] = kr
        vmla_ref[hd, 0:V_HEAD, :] = kv_t[hd * per + QK_NOPE:(hd + 1) * per, :]
        vmla_ref[hd, V_HEAD:VT_ROWS, :] = ones


def _mod_spec(st, tm=TM):
    tps = st["n"] // tm
    return pl.BlockSpec((1, 6, D_MODEL), lambda i, *_: (1 + i // tps if st["latent"] else 0, 0, 0))


def _input_stage(x, mod, npre, w_all, tabs, qg, wuq, wuqs, kvg, wukv, st):
    t_all = x.shape[0]
    tm = TM
    tps = st["n"] // tm
    row = lambda i: (i, 0)
    tab = lambda w: pl.BlockSpec((tm, w), lambda i: (i % tps if st["latent"] else 0, 0))
    hrow = lambda i: (0, i, 0)
    const2 = lambda i: (0, 0)
    tok = lambda w: pl.BlockSpec((tm, w), row)
    headed = lambda w: pl.BlockSpec((N_HEADS, tm, w), hrow)
    headed_t = lambda d: pl.BlockSpec((N_HEADS, d, tm), lambda i: (0, 0, i))
    full = lambda a: pl.BlockSpec(a.shape, const2)
    cos, sin, cosq, sinq = tabs
    out_shapes = [
        (headed_t(DIFF_HALF), (N_HEADS, DIFF_HALF, t_all), BF16),
        (headed_t(DIFF_HALF), (N_HEADS, DIFF_HALF, t_all), BF16),
        (headed(DIFF_HALF), (N_HEADS, t_all, DIFF_HALF), BF16),
        (headed(DIFF_HALF), (N_HEADS, t_all, DIFF_HALF), BF16),
        (headed_t(VT_ROWS), (N_HEADS, VT_ROWS, t_all), BF16),
        (tok(GROUP_WIDTH), (t_all, GROUP_WIDTH), F32),
        (tok(GROUP_WIDTH), (t_all, GROUP_WIDTH), F32),
        (tok(QKV_SLOTS), (t_all, QKV_SLOTS), BF16),
        (tok(GROUP_WIDTH), (t_all, GROUP_WIDTH), F32),
        (tok(QKV_SLOTS), (t_all, QKV_SLOTS), BF16),
        (tok(GROUP_WIDTH), (t_all, GROUP_WIDTH), F32),
        (tok(LANES), (t_all, LANES), F32),
        (headed_t(QK_NOPE + QK_ROPE), (N_HEADS, QK_NOPE + QK_ROPE, t_all), BF16),
        (tok(KV_LORA), (t_all, KV_LORA), F32),
        (headed(QK_NOPE + QK_ROPE), (N_HEADS, t_all, QK_NOPE + QK_ROPE), BF16),
        (headed_t(VT_ROWS), (N_HEADS, VT_ROWS, t_all), BF16),
    ]
    return pl.pallas_call(
        functools.partial(_in_kernel, latent=st["latent"]),
        grid=(t_all // tm,),
        in_specs=[tok(D_MODEL), _mod_spec(st, tm),
                  full(npre), *[full(w) for w in w_all], tab(GROUP_WIDTH), tab(GROUP_WIDTH), tab(Q_MLA), tab(Q_MLA),
                  full(qg), full(wuq), full(wuqs), full(kvg), full(wukv)],
        out_specs=[s for s, _, _ in out_shapes],
        out_shape=[jax.ShapeDtypeStruct(shp, dt) for _, shp, dt in out_shapes],
        compiler_params=_cparams(("parallel",)),
        name="input_stage_" + st["name"],
    )(x, mod, npre, *w_all, cos, sin, cosq, sinq, qg, wuq, wuqs, kvg, wukv)


def _attn_t_kernel(*refs, n_soft, n_new, n_ctx, lam_init):
    refs = list(refs)
    qt_refs, k_refs, vt_ref = refs[:n_soft], refs[n_soft:2 * n_soft], refs[2 * n_soft]
    pos = 2 * n_soft + 1
    if n_ctx:
        ck_refs, cvt_ref = refs[pos:pos + n_soft], refs[pos + n_soft]
        pos += n_soft + 1
    if n_soft == 2:
        lam_ref, g_ref = refs[pos:pos + 2]
        pos += 2
    o_ref = refs[pos]
    tq = o_ref.shape[0]
    nchain = n_soft * N_HEADS
    order = [(j, hd) for hd in range(N_HEADS) for j in range(n_soft)]

    def chunk(state, k_of, vt_of):
        ms, accs = state
        new_m, new_acc = list(ms), list(accs)
        ss = [jnp.dot(k_of(j, hd), qt_refs[j][hd], preferred_element_type=F32) for j, hd in order]
        ps, alphas = [], []
        for (j, hd), s in zip(order, ss):
            c = j * N_HEADS + hd
            s3 = s.reshape(s.shape[0] // 8, 8, tq)
            top = jnp.max(jnp.max(s3, axis=0), axis=0, keepdims=True)
            m_new = jnp.maximum(ms[c], jnp.broadcast_to(top, (8, tq)))
            ps.append(jnp.exp(s3 - m_new[None]).reshape(s.shape).astype(BF16))
            alphas.append(jnp.exp(ms[c] - m_new))
            new_m[c] = m_new
        for (j, hd), p, alpha in zip(order, ps, alphas):
            c = j * N_HEADS + hd
            scaled = (accs[c].reshape(VT_ROWS // 8, 8, tq) * alpha[None]).reshape(VT_ROWS, tq)
            new_acc[c] = scaled + jnp.dot(vt_of(hd), p, preferred_element_type=F32)
        return tuple(new_m), tuple(new_acc)

    state = (tuple(jnp.full((8, tq), -jnp.inf, F32) for _ in range(nchain)),
             tuple(jnp.zeros((VT_ROWS, tq), F32) for _ in range(nchain)))
    if n_ctx:
        cstep = min(TK, n_ctx)
        for i in range(n_ctx // cstep):
            state = chunk(state, lambda j, hd, i=i: ck_refs[j][hd, i * cstep:(i + 1) * cstep, :],
                          lambda hd, i=i: cvt_ref[hd, :, i * cstep:(i + 1) * cstep])
    step = min(TK, n_new)

    def body(i, st):
        start = pl.multiple_of(i * step, step)
        return chunk(st, lambda j, hd: k_refs[j][hd, pl.ds(start, step), :],
                     lambda hd: vt_ref[hd, :, pl.ds(start, step)])

    _, accs = lax.fori_loop(0, n_new // step, body, state)

    def normalised(c):
        num, den = accs[c][:V_HEAD], accs[c][V_HEAD:V_HEAD + 8]
        return (num.reshape(V_HEAD // 8, 8, tq) / den[None]).reshape(V_HEAD, tq)

    if n_soft == 2:
        lv = lam_ref[...]
        lam = (jnp.exp(jnp.sum(lv[0:1] * lv[1:2], axis=-1, keepdims=True))
               - jnp.exp(jnp.sum(lv[2:3] * lv[3:4], axis=-1, keepdims=True)) + lam_init)
    outs = []
    for hd in range(N_HEADS):
        out = normalised(hd)
        if n_soft == 2:
            a = out - lam * normalised(N_HEADS + hd)
            out = a * lax.rsqrt(jnp.mean(a * a, axis=0, keepdims=True) + EPS) * g_ref[...] * (1.0 - lam_init)
        outs.append(out)
    o_ref[...] = jnp.concatenate(outs, axis=0).T


def _attention_t(name, qts, ks, vt, ctx_ks, ctx_vt, extras, nb, n, n_ctx, lam_init=0.0):
    tq = min(TQ, n)
    nqt = n // tq
    n_soft = len(qts)
    in_specs = [pl.BlockSpec((N_HEADS, a.shape[1], tq), lambda b, i: (0, 0, b * nqt + i)) for a in qts]
    in_specs += [pl.BlockSpec((N_HEADS, n, a.shape[-1]), lambda b, i: (0, b, 0)) for a in ks]
    in_specs += [pl.BlockSpec((N_HEADS, VT_ROWS, n), lambda b, i: (0, 0, b))]
    args = list(qts) + list(ks) + [vt]
    if n_ctx:
        in_specs += [pl.BlockSpec((N_HEADS, n_ctx, a.shape[-1]), lambda b, i: (0, b, 0)) for a in ctx_ks]
        in_specs += [pl.BlockSpec((N_HEADS, VT_ROWS, n_ctx), lambda b, i: (0, 0, b))]
        args += list(ctx_ks) + [ctx_vt]
    in_specs += [pl.BlockSpec(a.shape, lambda b, i: (0, 0)) for a in extras]
    args += list(extras)
    return pl.pallas_call(
        functools.partial(_attn_t_kernel, n_soft=n_soft, n_new=n, n_ctx=n_ctx, lam_init=lam_init),
        grid=(nb, nqt),
        in_specs=in_specs,
        out_specs=pl.BlockSpec((tq, N_HEADS * V_HEAD), lambda b, i: (b * nqt + i, 0)),
        out_shape=jax.ShapeDtypeStruct((nb * n, N_HEADS * V_HEAD), F32),
        compiler_params=_cparams(("parallel", "parallel")),
        name=name,
    )(*args)


def _tri(lower):
    r = lax.broadcasted_iota(jnp.int32, (CHUNK, CHUNK), 0)
    c = lax.broadcasted_iota(jnp.int32, (CHUNK, CHUNK), 1)
    return (c <= r) if lower else (c >= r)


def _nt(a, b):
    return lax.dot_general(a, b, (((1,), (1,)), ((), ())), preferred_element_type=F32)


def _tn(a, b):
    return lax.dot_general(a, b, (((0,), (0,)), ((), ())), preferred_element_type=F32)


def _head_slices(qkv, hd):
    return tuple(qkv[:, (j * N_HEADS + hd) * LANES:(j * N_HEADS + hd) * LANES + HEAD_DIM] for j in range(3))


def _row_dot(row, mat, transpose_mat):
    row8 = jnp.broadcast_to(row, (8, row.shape[1])).astype(BF16)
    dims = (((1,), (1,)), ((), ())) if transpose_mat else (((1,), (0,)), ((), ()))
    return lax.dot_general(row8, mat, dims, preferred_element_type=F32)[0:1]


def _load_state_t(dst, src):
    for d in range(2):
        for hd in range(N_HEADS):
            dst[d, hd] = src[d, hd].T


def _mlstm_step(dirs, bias_ref):
    k_scale = HEAD_DIM ** -0.5
    lower = _tri(True).astype(F32)
    upper = _tri(False).astype(F32)
    chains = []
    for d, qkv_ref, tail_ref, _, ct_s, n_s, m_s in dirs:
        g = tail_ref[...] + bias_ref[...]
        ls = _log_sigmoid(g)
        g_t, ls_t = g.T, ls.T
        left, right = (lower, upper) if d == 0 else (upper, lower)
        cum_col = jnp.dot(left, ls, precision=HIGHEST, preferred_element_type=F32)
        cum_row = jnp.dot(ls_t, right, precision=HIGHEST, preferred_element_type=F32)
        qkv = qkv_ref[...]
        for hd in range(N_HEADS):
            ci = TAIL_GATE0 + 4 * (2 * d) + hd
            cf = ci + 4
            q, k, v = _head_slices(qkv, hd)
            chains.append(dict(
                d=d, hd=hd, q=q, k=k, v=v, li_row=g_t[ci:ci + 1, :], b_row=cum_row[cf:cf + 1, :],
                c_col=g[:, ci:ci + 1] - cum_col[:, cf:cf + 1],
                m_prev=m_s[d:d + 1, hd:hd + 1], ct=ct_s[d, hd], n=n_s[d, hd:hd + 1, :],
                refs=(ct_s, n_s, m_s)))
    for c in chains:
        c["s"] = _nt(c["k"], c["q"])
        c["v_t"] = c["v"].T
    for c in chains:
        valid = _tri(c["d"] != 0)
        log_d = jnp.where(valid, c["b_row"] + c["c_col"], -jnp.inf)
        log_inter = c["b_row"] + c["m_prev"]
        c["m_t"] = jnp.maximum(log_inter, jnp.max(log_d, axis=0, keepdims=True))
        c["w"] = c["s"] * k_scale * jnp.exp(log_d - c["m_t"])
        c["w_inter"] = jnp.exp(log_inter - c["m_t"])
    for c in chains:
        num = (jnp.dot(c["v_t"], c["w"].astype(BF16), preferred_element_type=F32)
               + c["w_inter"] * _nt(c["ct"].astype(BF16), c["q"]))
        den = jnp.sum(c["w"], axis=0, keepdims=True) + c["w_inter"] * _row_dot(c["n"], c["q"], True)
        c["h"] = num / jnp.maximum(jnp.abs(den), jnp.exp(-c["m_t"]))
    for c in chains:
        last = CHUNK - 1 if c["d"] == 0 else 0
        c["m_new"] = c["m_t"][:, last:last + 1]
        b_last = c["b_row"][:, last:last + 1]
        w_end_row = jnp.exp(b_last - c["b_row"] + c["li_row"] - c["m_new"])
        w_end_col = jnp.exp(c["c_col"] + (b_last - c["m_new"]))
        decay = jnp.exp(b_last + c["m_prev"] - c["m_new"])
        kw = (c["k"].astype(F32) * (w_end_col * k_scale)).astype(BF16)
        c["ct_new"] = decay * c["ct"] + jnp.dot(c["v_t"], kw, preferred_element_type=F32)
        c["n_new"] = decay * c["n"] + _row_dot(w_end_row, c["k"], False) * k_scale
    for j, entry in enumerate(dirs):
        entry[3][...] = jnp.concatenate([c["h"] for c in chains[j * N_HEADS:(j + 1) * N_HEADS]], axis=0).T
    for c in chains:
        d, hd = c["d"], c["hd"]
        ct_s, n_s, m_s = c["refs"]
        ct_s[d, hd] = c["ct_new"]
        n_s[d, hd:hd + 1, :] = c["n_new"]
        m_s[d:d + 1, hd:hd + 1] = c["m_new"]


def _ret_step(dirs, lg_ref):
    k_scale = HEAD_DIM ** -0.5
    s = lax.broadcasted_iota(jnp.int32, (CHUNK, CHUNK), 0)
    t = lax.broadcasted_iota(jnp.int32, (CHUNK, CHUNK), 1)
    lane = lax.broadcasted_iota(jnp.int32, (1, CHUNK), 1)
    chains = []
    for d, qkv_ref, _, st_s in dirs:
        lag = (t - s) if d == 0 else (s - t)
        pos = (lane if d == 0 else CHUNK - 1 - lane).astype(F32)
        qkv = qkv_ref[...]
        for hd in range(N_HEADS):
            q, k, v = _head_slices(qkv, hd)
            chains.append(dict(d=d, hd=hd, q=q, k=k, v=v, lag=lag, pos=pos, st=st_s[d, hd], ref=st_s,
                               lg=_log_sigmoid(lg_ref[d:d + 1, hd:hd + 1])))
    for c in chains:
        c["s"] = _nt(c["k"], c["q"])
        c["v_t"] = c["v"].T
    for c in chains:
        intra = jnp.where(c["lag"] >= 0, jnp.exp(jnp.maximum(c["lag"], 0).astype(F32) * c["lg"]), 0.0)
        c["a"] = (c["s"] * k_scale * intra).astype(BF16)
    for c in chains:
        inter = jnp.exp((c["pos"] + 1.0) * c["lg"])
        c["o"] = jnp.dot(c["v_t"], c["a"], preferred_element_type=F32) + inter * _nt(c["st"].astype(BF16), c["q"])
    row = lax.broadcasted_iota(jnp.int32, (CHUNK, HEAD_DIM), 0)
    for c in chains:
        src_pos = (row if c["d"] == 0 else CHUNK - 1 - row).astype(F32)
        tail = jnp.exp((CHUNK - 1.0 - src_pos) * c["lg"]) * k_scale
        kw = (c["k"].astype(F32) * tail).astype(BF16)
        c["st_new"] = jnp.exp(CHUNK * c["lg"]) * c["st"] + jnp.dot(c["v_t"], kw, preferred_element_type=F32)
    for j, entry in enumerate(dirs):
        entry[2][...] = jnp.concatenate([c["o"] for c in chains[j * N_HEADS:(j + 1) * N_HEADS]], axis=0).T
    for c in chains:
        c["ref"][c["d"], c["hd"]] = c["st_new"]


def _scan_kernel(tab_ref, mqf_ref, tailf_ref, mqb_ref, tailb_ref, rqf_ref, rqb_ref, bias_ref, lg_ref,
                 c0_ref, n0_ref, m0_ref, s0_ref,
                 hf_ref, hb_ref, of_ref, ob_ref, cn_ref, nn_ref, mn_ref, sn_ref, ct_s, n_s, m_s, st_s):
    i = pl.program_id(0)
    seqs = range(ct_s.shape[0])

    @pl.when(tab_ref[3, i] == 1)
    def _():
        for p in seqs:
            _load_state_t(ct_s.at[p], c0_ref.at[p])
            _load_state_t(st_s.at[p], s0_ref.at[p])
        n_s[...] = n0_ref[...]
        m_s[...] = m0_ref[...]

    mdirs, rdirs = [], []
    for p in seqs:
        state = (ct_s.at[p], n_s.at[p], m_s.at[p])
        mdirs += [(0, mqf_ref.at[p], tailf_ref.at[p], hf_ref.at[p]) + state,
                  (1, mqb_ref.at[p], tailb_ref.at[p], hb_ref.at[p]) + state]
        rdirs += [(0, rqf_ref.at[p], of_ref.at[p], st_s.at[p]), (1, rqb_ref.at[p], ob_ref.at[p], st_s.at[p])]
    _mlstm_step(mdirs, bias_ref)
    _ret_step(rdirs, lg_ref)

    @pl.when(tab_ref[4, i] == 1)
    def _():
        for p in seqs:
            _load_state_t(cn_ref.at[p], ct_s.at[p])
            _load_state_t(sn_ref.at[p], st_s.at[p])
        nn_ref[...] = n_s[...]
        mn_ref[...] = m_s[...]


def _scan_table(st):
    nch = st["n"] // CHUNK
    rows = [(c, nch - 1 - c, g, int(c == 0), int(c == nch - 1))
            for g in range(st["nb"] // _scan_group(st)) for c in range(nch)]
    return jnp.asarray(np.array(rows, dtype=np.int32).T)


def _scan_group(st):
    return next(g for g in (4, 2, 1) if st["nb"] % g == 0)


def _scans(tab, mqkv, tail, bias_row, rqkv, decay_logit, c0, n0, m0, s0, st):
    nb, n, grp = st["nb"], st["n"], _scan_group(st)
    seq3 = lambda a: a.reshape(nb, n, a.shape[-1])
    fwd = lambda i, tab: (tab[2, i], tab[0, i], 0)
    bwd = lambda i, tab: (tab[2, i], tab[1, i], 0)
    const = lambda i, tab: (0, 0)
    rows = lambda width, where: pl.BlockSpec((grp, CHUNK, width), where)
    state = lambda a: pl.BlockSpec((grp,) + a.shape[1:], lambda i, tab: (tab[2, i],) + (0,) * (a.ndim - 1))
    states = (c0, n0, m0, s0)
    grid_spec = pltpu.PrefetchScalarGridSpec(
        num_scalar_prefetch=1,
        grid=(tab.shape[1],),
        in_specs=[rows(QKV_SLOTS, fwd), rows(LANES, fwd), rows(QKV_SLOTS, bwd), rows(LANES, bwd),
                  rows(QKV_SLOTS, fwd), rows(QKV_SLOTS, bwd),
                  pl.BlockSpec((1, LANES), const), pl.BlockSpec(decay_logit.shape, const)]
        + [state(a) for a in states],
        out_specs=[rows(GROUP_WIDTH, fwd), rows(GROUP_WIDTH, bwd), rows(GROUP_WIDTH, fwd), rows(GROUP_WIDTH, bwd)]
        + [state(a) for a in states],
        scratch_shapes=[pltpu.VMEM((grp,) + a.shape[1:], F32) for a in states],
    )
    outs = pl.pallas_call(
        _scan_kernel,
        grid_spec=grid_spec,
        out_shape=[jax.ShapeDtypeStruct((nb, n, GROUP_WIDTH), F32)] * 4
        + [jax.ShapeDtypeStruct(a.shape, F32) for a in states],
        compiler_params=_cparams(("arbitrary",)),
        name="recurrent_scans",
    )(tab, seq3(mqkv), seq3(tail), seq3(mqkv), seq3(tail), seq3(rqkv), seq3(rqkv), bias_row, decay_logit, *states)
    return [o.reshape(nb * n, GROUP_WIDTH) for o in outs[:4]] + list(outs[4:])


def _proj_kernel(x_ref, w_ref, o_ref):
    o_ref[...] = _bdot(x_ref[...], w_ref[...]).astype(o_ref.dtype)


def _project(x, w, dtype):
    return pl.pallas_call(
        _proj_kernel,
        out_shape=jax.ShapeDtypeStruct((x.shape[0], w.shape[1]), dtype),
        name="ctx_kv_proj",
    )(x, w)


def _first_argmax_mask(cur, axis, size):
    io = lax.broadcasted_iota(jnp.int32, cur.shape, axis)
    mx = jnp.max(cur, axis=axis, keepdims=True)
    ix = jnp.min(jnp.where(cur == mx, io, size), axis=axis, keepdims=True)
    return io == ix


def _route(scores_t, bias_col):
    tm = scores_t.shape[1]
    per = N_EXPERTS // N_GROUPS
    sel = scores_t + bias_col
    s3 = sel.reshape(N_GROUPS, per, tm)
    hit1 = _first_argmax_mask(s3, 1, per)
    m1 = jnp.max(s3, axis=1, keepdims=True)
    m2 = jnp.max(jnp.where(hit1, -jnp.inf, s3), axis=1, keepdims=True)
    cur = m1 + m2
    gsel = None
    for _ in range(TOPK_GROUPS):
        hit = _first_argmax_mask(cur, 0, N_GROUPS)
        gsel = hit if gsel is None else jnp.logical_or(gsel, hit)
        cur = jnp.where(hit, -jnp.inf, cur)
    cur = jnp.where(gsel, s3, -jnp.inf).reshape(N_EXPERTS, tm)
    chosen, hits = None, []
    for _ in range(TOP_K):
        hit = _first_argmax_mask(cur, 0, N_EXPERTS)
        hits.append(hit)
        chosen = hit if chosen is None else jnp.logical_or(chosen, hit)
        cur = jnp.where(hit, -jnp.inf, cur)
    w = jnp.where(chosen, scores_t, 0.0)
    return w / jnp.sum(w, axis=0, keepdims=True) * ROUTED_SCALE, chosen, hits


def _out_kernel(oa_ref, od_ref, hf_ref, hb_ref, mo_ref, of_ref, ob_ref, rg_ref,
                x_ref, mod_ref, mg_ref, npost_ref, npre_ref, wout_ref, wrt_ref, rb_ref,
                x1_ref, h2_ref, hpk_ref, eidx_ref, rank_ref, wk_ref, cnt_ref, count_s):
    @pl.when(pl.program_id(0) == 0)
    def _():
        count_s[...] = jnp.zeros(count_s.shape, F32)

    a = oa_ref[...]
    dd = od_ref[...]
    b = _rms_heads(hf_ref[...] + hb_ref[...]) * mg_ref[...] * _sigmoid(mo_ref[...])
    r = _rms_heads(of_ref[...] + ob_ref[...]) * _silu(rg_ref[...])
    mix = sum(jnp.dot(part.astype(BF16), wout_ref[j * GROUP_WIDTH:(j + 1) * GROUP_WIDTH, :],
                      preferred_element_type=F32) for j, part in enumerate((a, b, r, dd)))
    m = mod_ref[0]
    x1 = x_ref[...] + m[2:3] * (_rms(mix) * npost_ref[...])
    x1_ref[...] = x1
    h2 = (_rms(x1) * npre_ref[...] * (1.0 + m[4:5]) + m[3:4]).astype(BF16)
    h2_ref[...] = h2
    hpk_ref[...] = _pack_rows(h2)

    logits_t = lax.dot_general(wrt_ref[...], h2, (((1,), (1,)), ((), ())), preferred_element_type=F32)
    w_t, chosen, hits = _route(_sigmoid(logits_t), rb_ref[...])
    tm = w_t.shape[1]
    src = lax.broadcasted_iota(jnp.int32, (tm, tm), 0)
    dst = lax.broadcasted_iota(jnp.int32, (tm, tm), 1)
    before = jnp.where(src < dst, 1.0, 0.0).astype(BF16)
    picked = jnp.where(chosen, 1.0, 0.0)
    rank = jnp.dot(picked.astype(BF16), before, preferred_element_type=F32) + count_s[...]
    count_s[...] += jnp.sum(picked, axis=1, keepdims=True)
    cnt_ref[...] = jnp.broadcast_to(count_s[...], cnt_ref.shape)
    e_io = lax.broadcasted_iota(jnp.int32, w_t.shape, 0)
    pick = lambda hit, val: jnp.sum(jnp.where(hit, val, jnp.zeros_like(val)), axis=0, keepdims=True)
    eidx_ref[...] = jnp.concatenate([pick(h, e_io) for h in hits], axis=0)
    rank_ref[...] = jnp.concatenate([pick(h, rank) for h in hits], axis=0).astype(jnp.int32)
    wk = jnp.concatenate([pick(h, w_t) for h in hits] + [jnp.zeros((LANES - TOP_K, tm), F32)], axis=0)
    wk_ref[...] = wk.T


def _output_stage(oa, od, hf, hb, mo, of, ob, rg, x, mod, mg, npost, npre, wout, wrt, rb, st):
    t_all = x.shape[0]
    row = lambda i: (i, 0)
    const2 = lambda i: (0, 0)
    tok = lambda w: pl.BlockSpec((TM, w), row)
    full = lambda a: pl.BlockSpec(a.shape, const2)
    return pl.pallas_call(
        _out_kernel,
        grid=(t_all // TM,),
        in_specs=[tok(GROUP_WIDTH)] * 8 + [tok(D_MODEL), _mod_spec(st),
                                   full(mg), full(npost), full(npre), full(wout), full(wrt), full(rb)],
        out_specs=[tok(D_MODEL), tok(D_MODEL), tok(D_MODEL // 2),
                   pl.BlockSpec((TOP_K, TM), lambda i: (0, i)), pl.BlockSpec((TOP_K, TM), lambda i: (0, i)),
                   tok(LANES), pl.BlockSpec((N_EXPERTS, LANES), const2)],
        out_shape=[jax.ShapeDtypeStruct((t_all, D_MODEL), F32),
                   jax.ShapeDtypeStruct((t_all, D_MODEL), BF16),
                   jax.ShapeDtypeStruct((t_all, D_MODEL // 2), jnp.uint32),
                   jax.ShapeDtypeStruct((TOP_K, t_all), jnp.int32),
                   jax.ShapeDtypeStruct((TOP_K, t_all), jnp.int32),
                   jax.ShapeDtypeStruct((t_all, LANES), F32),
                   jax.ShapeDtypeStruct((N_EXPERTS, LANES), F32)],
        scratch_shapes=[pltpu.VMEM((N_EXPERTS, 1), F32)],
        compiler_params=_cparams(("arbitrary",)),
        name="output_stage_" + st["name"],
    )(oa, od, hf, hb, mo, of, ob, rg, x, mod, mg, npost, npre, wout, wrt, rb)


SC_CORES, SC_SUBCORES = 2, 16
SC_WORKERS = SC_CORES * SC_SUBCORES
SLOT_BLOCK = 512
DISPATCH_ROWS = 64
COMBINE_ROWS = 64


def _sc_mesh():
    return plsc.VectorSubcoreMesh(core_axis_name="core", subcore_axis_name="subcore")


def _sc_worker():
    return lax.axis_index("subcore") * SC_CORES + lax.axis_index("core")


def _sc_dispatch(hpk, dest, n_slots):
    t_all, width = hpk.shape
    per_worker = t_all // SC_WORKERS
    assert t_all % (SC_WORKERS * DISPATCH_ROWS) == 0

    @functools.partial(
        pl.kernel, mesh=_sc_mesh(), out_type=jax.ShapeDtypeStruct((n_slots, width), hpk.dtype),
        scratch_types=[pltpu.VMEM((DISPATCH_ROWS, width), hpk.dtype)]
        + [pltpu.VMEM((DISPATCH_ROWS,), jnp.int32)] * TOP_K + [pltpu.SemaphoreType.DMA])
    def dispatch(x_hbm, d_hbm, o_hbm, rows_v, *rest):
        idx, sem = rest[:TOP_K], rest[TOP_K]

        @pl.loop(0, per_worker // DISPATCH_ROWS)
        def _(j):
            base = _sc_worker() * per_worker + j * DISPATCH_ROWS
            pltpu.sync_copy(x_hbm.at[pl.ds(base, DISPATCH_ROWS)], rows_v)
            for k in range(TOP_K):
                pltpu.sync_copy(d_hbm.at[pl.ds(k * t_all + base, DISPATCH_ROWS)], idx[k])
            copies = [pltpu.async_copy(rows_v, o_hbm.at[idx[k]], sem) for k in range(TOP_K)]
            for c in copies:
                c.wait()

    return dispatch(hpk, dest)


def _sc_combine(yb, dest, t_all):
    width = yb.shape[1]
    per_worker = t_all // SC_WORKERS
    assert t_all % (SC_WORKERS * COMBINE_ROWS) == 0

    @functools.partial(
        pl.kernel, mesh=_sc_mesh(), out_type=jax.ShapeDtypeStruct((TOP_K, t_all, width), yb.dtype),
        scratch_types=[pltpu.VMEM((COMBINE_ROWS, width), yb.dtype)] * 2
        + [pltpu.VMEM((COMBINE_ROWS,), jnp.int32)] * TOP_K + [pltpu.SemaphoreType.DMA] * 2)
    def combine(y_hbm, d_hbm, o_hbm, rows_a, rows_b, *rest):
        idx, sems = rest[:TOP_K], rest[TOP_K:]
        bufs = (rows_a, rows_b)

        @pl.loop(0, per_worker // COMBINE_ROWS)
        def _(j):
            base = _sc_worker() * per_worker + j * COMBINE_ROWS
            for k in range(TOP_K):
                pltpu.sync_copy(d_hbm.at[pl.ds(k * t_all + base, COMBINE_ROWS)], idx[k])
            pending = pltpu.async_copy(y_hbm.at[idx[0]], bufs[0], sems[0])
            for k in range(TOP_K):
                pending.wait()
                if k + 1 < TOP_K:
                    pending = pltpu.async_copy(y_hbm.at[idx[k + 1]], bufs[(k + 1) % 2], sems[(k + 1) % 2])
                pltpu.sync_copy(bufs[k % 2], o_hbm.at[k, pl.ds(base, COMBINE_ROWS)])

    return combine(yb, dest)


def _pack_rows(x):
    bits = pltpu.bitcast(x.astype(BF16).astype(F32), jnp.uint32)
    w = x.shape[-1] // 2
    return (bits[..., :w] >> 16) | (bits[..., w:] & jnp.uint32(0xFFFF0000))


def _unpack_rows(words):
    return pltpu.bitcast(words << 16, F32), pltpu.bitcast(words & jnp.uint32(0xFFFF0000), F32)


def _expert_kernel(be_ref, used_ref, xs_ref, wg_ref, wu_ref, wd_ref, y_ref, wg_s, wu_s, wd_s):
    b = pl.program_id(0)
    fresh = jnp.logical_or(b == 0, be_ref[b] != be_ref[jnp.maximum(b - 1, 0)])

    @pl.when(jnp.logical_and(fresh, b < used_ref[0]))
    def _():
        wg_s[...] = wg_ref[0, 0].astype(BF16)
        wu_s[...] = wu_ref[0, 0].astype(BF16)
        wd_s[...] = wd_ref[0, 0].astype(BF16)

    @pl.when(b < used_ref[0])
    def _():
        lo, hi = (v.astype(BF16) for v in _unpack_rows(xs_ref[...]))
        half = D_MODEL // 2
        gate = (jnp.dot(lo, wg_s[:half, :], preferred_element_type=F32)
                + jnp.dot(hi, wg_s[half:, :], preferred_element_type=F32))
        up = (jnp.dot(lo, wu_s[:half, :], preferred_element_type=F32)
              + jnp.dot(hi, wu_s[half:, :], preferred_element_type=F32))
        y = jnp.dot((_silu(gate) * up).astype(BF16), wd_s[...], preferred_element_type=F32)
        y_ref[...] = _pack_rows(y)


def _expert_blocks(layer, xs, block_expert, blocks_used, wg, wu, wdn):
    n_slots = xs.shape[0]
    wspec = lambda shape: pl.BlockSpec((1, 1) + shape, lambda b, be, used: (layer, be[b], 0, 0))
    rows = lambda b, be, used: (jnp.minimum(b, used[0] - 1), 0)
    grid_spec = pltpu.PrefetchScalarGridSpec(
        num_scalar_prefetch=2,
        grid=(n_slots // SLOT_BLOCK,),
        in_specs=[pl.BlockSpec((SLOT_BLOCK, D_MODEL // 2), rows),
                  wspec((D_MODEL, D_EXPERT)), wspec((D_MODEL, D_EXPERT)), wspec((D_EXPERT, D_MODEL))],
        out_specs=pl.BlockSpec((SLOT_BLOCK, D_MODEL // 2), rows),
        scratch_shapes=[pltpu.VMEM((D_MODEL, D_EXPERT), BF16), pltpu.VMEM((D_MODEL, D_EXPERT), BF16),
                        pltpu.VMEM((D_EXPERT, D_MODEL), BF16)])
    return pl.pallas_call(
        _expert_kernel, grid_spec=grid_spec,
        out_shape=jax.ShapeDtypeStruct((n_slots, D_MODEL // 2), jnp.uint32),
        compiler_params=_cparams(("arbitrary",)),
        name="moe_experts",
    )(block_expert, blocks_used, xs, wg, wu, wdn)


def _moe_out_kernel(g_ref, wk_ref, h_ref, x1_ref, mod_ref, npost_ref, sg_ref, su_ref, sd_ref, o_ref):
    h = h_ref[...]
    act = _silu(jnp.dot(h, sg_ref[...], preferred_element_type=F32)) * jnp.dot(h, su_ref[...],
                                                                              preferred_element_type=F32)
    moe = jnp.dot(act.astype(BF16), sd_ref[...], preferred_element_type=F32)
    wk = wk_ref[...]
    lo, hi = 0.0, 0.0
    for k in range(TOP_K):
        g_lo, g_hi = _unpack_rows(g_ref[k])
        lo = lo + g_lo * wk[:, k:k + 1]
        hi = hi + g_hi * wk[:, k:k + 1]
    moe = moe + jnp.concatenate([lo, hi], axis=-1)
    m = mod_ref[0]
    o_ref[...] = x1_ref[...] + m[5:6] * (_rms(moe) * npost_ref[...])


def _moe_out_stage(g, wk, h2, x1, mod, npost, sg, su, sd, st):
    t_all = h2.shape[0]
    row = lambda i: (i, 0)
    full = lambda a: pl.BlockSpec(a.shape, lambda i: (0, 0))
    return pl.pallas_call(
        _moe_out_kernel,
        grid=(t_all // TM,),
        in_specs=[pl.BlockSpec((TOP_K, TM, D_MODEL // 2), lambda i: (0, i, 0)), pl.BlockSpec((TM, LANES), row),
                  pl.BlockSpec((TM, D_MODEL), row), pl.BlockSpec((TM, D_MODEL), row), _mod_spec(st),
                  full(npost), full(sg), full(su), full(sd)],
        out_specs=pl.BlockSpec((TM, D_MODEL), row),
        out_shape=jax.ShapeDtypeStruct((t_all, D_MODEL), F32),
        compiler_params=_cparams(("parallel",)),
        name="moe_combine_" + st["name"],
    )(g, wk, h2, x1, mod, npost, sg, su, sd)


def _moe_stage(layer, h2, hpk, eidx, rank, wk, counts, x1, mod, npost, wg, wu, wdn, sg, su, sd, st):
    t_all = h2.shape[0]
    n_blocks = -(-(t_all * TOP_K + N_EXPERTS * (SLOT_BLOCK - 1)) // SLOT_BLOCK)
    cnt = counts[:, 0].astype(jnp.int32)
    padded = (cnt + SLOT_BLOCK - 1) // SLOT_BLOCK * SLOT_BLOCK
    pad_end = jnp.cumsum(padded)
    pad_start = pad_end - padded
    experts = jnp.arange(N_EXPERTS, dtype=jnp.int32)[:, None, None]
    dest = rank + jnp.sum(jnp.where(eidx[None] == experts, pad_start[:, None, None], 0), axis=0)
    dest = dest.reshape(TOP_K * t_all)
    block_start = jnp.arange(n_blocks, dtype=jnp.int32) * SLOT_BLOCK
    block_expert = jnp.minimum(jnp.sum((pad_end[None, :] <= block_start[:, None]).astype(jnp.int32), axis=1),
                               N_EXPERTS - 1)
    blocks_used = (pad_end[-1:] // SLOT_BLOCK).astype(jnp.int32)

    xs = _sc_dispatch(hpk, dest, n_blocks * SLOT_BLOCK)
    yb = _expert_blocks(layer, xs, block_expert, blocks_used, wg, wu, wdn)
    g = _sc_combine(yb, dest, t_all)
    return _moe_out_stage(g, wk, h2, x1, mod, npost, sg, su, sd, st)


_PERM32 = np.concatenate([np.arange(8, 16), np.arange(0, 8), np.arange(24, 32), np.arange(16, 24)])


def _rope_tables(n):
    f32 = np.float32
    pos = np.arange(n)
    quarter = QK_ROPE // 4
    inv = (f32(1.0) / (f32(ROPE_THETA) ** (np.arange(quarter, dtype=f32) / f32(quarter)))).astype(f32)
    ang_r = (pos // GRID_W).astype(f32)[:, None] * inv[None, :]
    ang_c = (pos % GRID_W).astype(f32)[:, None] * inv[None, :]
    cos32 = np.concatenate([np.cos(ang_r)] * 2 + [np.cos(ang_c)] * 2, axis=-1).astype(f32)
    sin32 = np.concatenate([-np.sin(ang_r), np.sin(ang_r), -np.sin(ang_c), np.sin(ang_c)], axis=-1).astype(f32)
    cos256, sin256 = np.tile(cos32, (1, 8)), np.tile(sin32, (1, 8))
    one, zero = np.ones((n, QK_NOPE), f32), np.zeros((n, QK_NOPE), f32)
    cosq = np.concatenate([one, cos32] * N_HEADS, axis=-1)
    sinq = np.concatenate([zero, sin32] * N_HEADS, axis=-1)
    return tuple(jnp.asarray(t) for t in (cos256, sin256, cosq, sinq))


def _layer_weights(w_in_l, w_uq_l):
    gw = GROUP_WIDTH
    g0 = W_BEFORE_GATES
    b0 = g0 + 4 * N_HEADS
    kr0 = b0 + W_AFTER_GATES
    assert kr0 + QK_ROPE == w_in_l.shape[1]
    w_a, m_g, w_b, d_kr = w_in_l[:, :g0], w_in_l[:, g0:b0], w_in_l[:, b0:kr0], w_in_l[:, kr0:]
    perm256 = np.concatenate([_PERM32 + 32 * j for j in range(8)])
    pad = lambda w, n: jnp.pad(w, ((0, 0), (0, n - w.shape[1])))
    w_c = jnp.concatenate([pad(jnp.concatenate([d_kr, m_g], axis=-1), LANES),
                           w_a[:, perm256], w_a[:, gw + perm256], pad(d_kr[:, _PERM32], LANES)], axis=-1)
    per = QK_NOPE + QK_ROPE
    permq = np.concatenate([np.concatenate([np.arange(QK_NOPE), QK_NOPE + _PERM32]) + per * j
                            for j in range(N_HEADS)])
    return ((w_a.astype(BF16), w_b.astype(BF16), w_c.astype(BF16)), w_uq_l.astype(BF16),
            w_uq_l[:, permq].astype(BF16))


def kernel(x_prompt, x_sample, cache_diff_k, cache_diff_v, state_mlstm_C, state_mlstm_n, state_mlstm_m, state_ret_S, cache_mla_ckv, cache_mla_krope, c, c_ctx, w_mod, b_mod, norm_pre, norm_post, w_in, w_out, diff_lambda, diff_norm, mlstm_gate_bias, mlstm_norm, ret_decay_logit, mla_q_norm, mla_w_uq, mla_kv_norm, mla_w_ukv, moe_w_router, moe_router_bias, moe_w_gate, moe_w_up, moe_w_down, shared_w_gate, shared_w_up, shared_w_down):
    bp, n_p, _ = x_prompt.shape
    bs, n_s, _ = x_sample.shape
    depth = w_in.shape[0]
    past = cache_diff_k.shape[2]
    assert (bp * n_p) % TM == 0 and n_s % TM == 0 and n_p % CHUNK == 0 and n_s % min(TK, n_s) == 0
    assert past % min(TK, past) == 0 and bs + 1 <= 8 and n_s % GRID_W == 0

    streams = (dict(name="ctx", nb=bp, n=n_p, latent=False), dict(name="latent", nb=bs, n=n_s, latent=True))
    xs_by_stream = [x_prompt.reshape(bp * n_p, D_MODEL), x_sample.reshape(bs * n_s, D_MODEL)]
    cond = jnp.zeros((8, D_MODEL), F32).at[0].set(c_ctx).at[1:1 + bs].set(c)
    mod_all = _modulation(cond, w_mod, b_mod).reshape(depth, 8, 6, D_MODEL)
    tabs = _rope_tables(n_s)
    scan_tabs = [_scan_table(st) for st in streams]

    outs = [[] for _ in range(8)]
    for l in range(depth):
        lam_init = 0.8 - 0.6 * math.exp(-0.3 * l)
        mod = mod_all[l]
        w_all, wuq, wuqs = _layer_weights(w_in[l], mla_w_uq[l])
        wukv = mla_w_ukv[l].astype(BF16)
        wout, wrt, rbias = w_out[l].astype(BF16), moe_w_router[l].T.astype(BF16), moe_router_bias[l][:, None]
        shared = [w[l].astype(BF16) for w in (shared_w_gate, shared_w_up, shared_w_down)]
        lamv, dg = diff_lambda[l], diff_norm[l][:, None]
        bias_row = jnp.zeros((1, LANES), F32).at[0, TAIL_GATE0:TAIL_GATE0 + 16].set(mlstm_gate_bias[l].reshape(16))

        with_ones = lambda vt: jnp.concatenate([vt, jnp.ones((N_HEADS, VT_ONES, bs * past), BF16)], axis=1)
        ck = jnp.transpose(cache_diff_k[:, l], (2, 0, 1, 3)).reshape(N_HEADS, bs * past, HEAD_DIM).astype(BF16)
        cvt = with_ones(jnp.transpose(cache_diff_v[:, l], (2, 3, 0, 1)).reshape(N_HEADS, HEAD_DIM, bs * past)
                        .astype(BF16))
        kvc = _project(cache_mla_ckv[:, l].reshape(bs * past, KV_LORA), wukv, BF16)
        kvc = kvc.reshape(bs * past, N_HEADS, QK_NOPE + V_HEAD)
        krc = jnp.broadcast_to(cache_mla_krope[:, l].reshape(bs * past, 1, QK_ROPE).astype(BF16),
                               (bs * past, N_HEADS, QK_ROPE))
        kc = jnp.transpose(jnp.concatenate([kvc[..., :QK_NOPE], krc], axis=-1), (1, 0, 2))
        vct = with_ones(jnp.transpose(kvc[..., QK_NOPE:], (1, 2, 0)))

        for si, st in enumerate(streams):
            x, nb, n = xs_by_stream[si], st["nb"], st["n"]
            (aq1t, aq2t, ak1, ak2, avt, ak, av, mqkv, mo, rqkv, rg, tail, qmlat, ckv, kmla, vmlat) = _input_stage(
                x, mod, norm_pre[l, 0:1], w_all, tabs, mla_q_norm[l][None], wuq, wuqs,
                mla_kv_norm[l][None], wukv, st)
            if st["latent"]:
                oa = _attention_t("diff_attn_latent", [aq1t, aq2t], [ak1, ak2], avt,
                                  [ck[..., :DIFF_HALF], ck[..., DIFF_HALF:]], cvt, [lamv, dg], nb, n, past, lam_init)
                od = _attention_t("mla_attn_latent", [qmlat], [kmla], vmlat, [kc], vct, [], nb, n, past)
                states = [s[:, l] for s in (state_mlstm_C, state_mlstm_n, state_mlstm_m, state_ret_S)]
            else:
                oa = _attention_t("diff_attn_ctx", [aq1t, aq2t], [ak1, ak2], avt, None, None, [lamv, dg],
                                  nb, n, 0, lam_init)
                od = _attention_t("mla_attn_ctx", [qmlat], [kmla], vmlat, None, None, [], nb, n, 0)
                states = [jnp.zeros((nb,) + s.shape[2:], F32)
                          for s in (state_mlstm_C, state_mlstm_n, state_mlstm_m, state_ret_S)]
            hf, hb, of, ob, c_n, n_n, m_n, s_n = _scans(scan_tabs[si], mqkv, tail, bias_row, rqkv,
                                                        ret_decay_logit[l], *states, st)

            x1, h2, hpk, eidx, rank, wk, counts = _output_stage(
                oa, od, hf, hb, mo, of, ob, rg, x, mod, mlstm_norm[l][None], norm_post[l, 0:1],
                norm_pre[l, 1:2], wout, wrt, rbias, st)
            xs_by_stream[si] = _moe_stage(l, h2, hpk, eidx, rank, wk, counts, x1, mod, norm_post[l, 1:2],
                                          moe_w_gate, moe_w_up, moe_w_down, *shared, st)
            if not st["latent"]:
                new = (ak.reshape(nb, n, N_HEADS, HEAD_DIM), av.reshape(nb, n, N_HEADS, HEAD_DIM), c_n, n_n, m_n,
                       s_n, ckv.reshape(nb, n, KV_LORA), tail[:, :QK_ROPE].reshape(nb, n, QK_ROPE))
                for o, v in zip(outs, new):
                    o.append(v)

    return (xs_by_stream[0].reshape(bp, n_p, D_MODEL), xs_by_stream[1].reshape(bs, n_s, D_MODEL)) + tuple(
        jnp.stack(o, axis=1) for o in outs)
```

```python
import functools
import math

import numpy as np
import jax
import jax.numpy as jnp
from jax import lax
from jax.experimental import pallas as pl
from jax.experimental.pallas import tpu as pltpu
from jax.experimental.pallas import tpu_sc as plsc

F32 = jnp.float32
BF16 = jnp.bfloat16
HIGHEST = lax.Precision.HIGHEST

D_MODEL = 1024
GRID_W = 64
GROUP_WIDTH = 256
HEAD_DIM = 64
N_HEADS = 4
DIFF_HALF = 32
ROPE_THETA = 10000.0
Q_LORA = 256
KV_LORA = 128
QK_NOPE = 64
QK_ROPE = 32
V_HEAD = 64
N_EXPERTS = 64
TOP_K = 8
N_GROUPS = 8
TOPK_GROUPS = 4
D_EXPERT = 256
ROUTED_SCALE = 2.5
CHUNK = 128
EPS = 1e-6
Q_MLA = N_HEADS * (QK_NOPE + QK_ROPE)

LANES = 128
VMEM_LIMIT = 56 * 1024 * 1024

TM = 512
TQ = 256
TK = 4096

C_AQ, C_AK, C_AV, C_MQKV, C_MO = (0, 0), (0, 256), (0, 512), (0, 768), (0, 1536)
C_RQKV, C_RG, C_CQ, C_CKV = (1, 0), (1, 768), (1, 1024), (1, 1280)
C_TAIL, C_AQS, C_AKS, C_TAILS = (2, 0), (2, 128), (2, 384), (2, 640)
W_BEFORE_GATES, W_AFTER_GATES = 1792, 1408
TAIL_GATE0 = QK_ROPE
QKV_SLOTS = 3 * N_HEADS * LANES
VT_ONES = 16
VT_ROWS = V_HEAD + VT_ONES


def _cparams(sem):
    return pltpu.CompilerParams(dimension_semantics=sem, vmem_limit_bytes=VMEM_LIMIT)


def _rms(x):
    return x * lax.rsqrt(jnp.mean(x * x, axis=-1, keepdims=True) + EPS)


def _head_mean_matrix(width):
    r = lax.broadcasted_iota(jnp.int32, (width, width), 0) // HEAD_DIM
    c = lax.broadcasted_iota(jnp.int32, (width, width), 1) // HEAD_DIM
    return jnp.where(r == c, 1.0 / HEAD_DIM, 0.0).astype(BF16)


def _rms_heads(x):
    sq = x * x
    hi = sq.astype(BF16)
    lo = (sq - hi.astype(F32)).astype(BF16)
    g = _head_mean_matrix(x.shape[-1])
    ms = jnp.dot(hi, g, preferred_element_type=F32) + jnp.dot(lo, g, preferred_element_type=F32)
    return x * lax.rsqrt(ms + EPS)


def _sigmoid(x):
    return 1.0 / (1.0 + jnp.exp(-x))


def _silu(x):
    return x * _sigmoid(x)


def _log_sigmoid(x):
    return jnp.minimum(x, 0.0) - jnp.log1p(jnp.exp(-jnp.abs(x)))


def _bdot(a, b):
    return jnp.dot(a.astype(BF16), b.astype(BF16), preferred_element_type=F32)


def _mod_kernel(c_ref, w_ref, b_ref, o_ref):
    o_ref[0] = _bdot(_silu(c_ref[...]), w_ref[0]) + b_ref[0]


def _modulation(cond, w_mod, b_mod):
    depth, _, n = w_mod.shape
    tn = 1536
    return pl.pallas_call(
        _mod_kernel,
        grid=(depth, n // tn),
        in_specs=[pl.BlockSpec((8, D_MODEL), lambda l, j: (0, 0)),
                  pl.BlockSpec((1, D_MODEL, tn), lambda l, j: (l, 0, j)),
                  pl.BlockSpec((1, 1, tn), lambda l, j: (l, 0, j))],
        out_specs=pl.BlockSpec((1, 8, tn), lambda l, j: (l, 0, j)),
        out_shape=jax.ShapeDtypeStruct((depth, 8, n), F32),
        compiler_params=_cparams(("parallel", "parallel")),
        name="adaln_mod",
    )(cond, w_mod, b_mod.reshape(depth, 1, n))


def _in_kernel(x_ref, mod_ref, npre_ref, wa_ref, wb_ref, wc_ref, cos_ref, sin_ref, cosq_ref, sinq_ref,
               qg_ref, wuq_ref, wuqs_ref, kvg_ref, wukv_ref,
               aq1_ref, aq2_ref, ak1t_ref, ak2t_ref, avh_ref, ak_ref, av_ref,
               mqkv_ref, mo_ref, rqkv_ref, rg_ref, tail_ref,
               qmla_ref, ckv_ref, kmlat_ref, vmla_ref, *, latent):
    m = mod_ref[0]
    h = (_rms(x_ref[...]) * npre_ref[...] * (1.0 + m[1:2]) + m[0:1]).astype(BF16)

    def proj(col, width):
        w_ref = (wa_ref, wb_ref, wc_ref)[col[0]]
        return jnp.dot(h, w_ref[:, col[1]:col[1] + width], preferred_element_type=F32)

    def rotated(c0, c0_swapped, width, cos, sin):
        return proj(c0, width) * cos + proj(c0_swapped, width) * sin if latent else proj(c0, width)

    cos = cos_ref[...]
    sin = sin_ref[...]
    aq = rotated(C_AQ, C_AQS, GROUP_WIDTH, cos, sin) * (DIFF_HALF ** -0.5)
    ak = rotated(C_AK, C_AKS, GROUP_WIDTH, cos, sin)
    av = proj(C_AV, GROUP_WIDTH)
    ak_ref[...] = ak
    av_ref[...] = av
    aq_t, av_t = aq.T.astype(BF16), av.T.astype(BF16)
    ones = jnp.ones((VT_ONES, aq_t.shape[1]), BF16)
    for hd in range(N_HEADS):
        lo = hd * HEAD_DIM
        aq1_ref[hd] = aq_t[lo:lo + DIFF_HALF, :]
        aq2_ref[hd] = aq_t[lo + DIFF_HALF:lo + HEAD_DIM, :]
        ak1t_ref[hd] = ak[:, lo:lo + DIFF_HALF].astype(BF16)
        ak2t_ref[hd] = ak[:, lo + DIFF_HALF:lo + HEAD_DIM].astype(BF16)
        avh_ref[hd, 0:HEAD_DIM, :] = av_t[lo:lo + HEAD_DIM, :]
        avh_ref[hd, HEAD_DIM:VT_ROWS, :] = ones

    def head_slots(x):
        x = x.astype(BF16)
        zero = jnp.zeros((x.shape[0], LANES - HEAD_DIM), BF16)
        parts = []
        for j in range(3 * N_HEADS):
            parts += [x[:, j * HEAD_DIM:(j + 1) * HEAD_DIM], zero]
        return jnp.concatenate(parts, axis=-1)

    mqkv_ref[...] = head_slots(proj(C_MQKV, 3 * GROUP_WIDTH))
    mo_ref[...] = proj(C_MO, GROUP_WIDTH)
    rqkv_ref[...] = head_slots(proj(C_RQKV, 3 * GROUP_WIDTH))
    rg_ref[...] = proj(C_RG, GROUP_WIDTH)

    tail = proj(C_TAIL, LANES)
    tail_ref[...] = tail
    kr = tail * cos[:, :LANES] + proj(C_TAILS, LANES) * sin[:, :LANES] if latent else tail
    kr = kr[:, :QK_ROPE].astype(BF16)

    cqn = (_rms(proj(C_CQ, Q_LORA)) * qg_ref[...]).astype(BF16)
    qmla = jnp.dot(cqn, wuq_ref[...], preferred_element_type=F32)
    if latent:
        qmla = qmla * cosq_ref[...] + jnp.dot(cqn, wuqs_ref[...], preferred_element_type=F32) * sinq_ref[...]
    qmla_t = (qmla * ((QK_NOPE + QK_ROPE) ** -0.5)).T.astype(BF16)
    for hd in range(N_HEADS):
        lo = hd * (QK_NOPE + QK_ROPE)
        qmla_ref[hd] = qmla_t[lo:lo + QK_NOPE + QK_ROPE, :]
    ckvn = _rms(proj(C_CKV, KV_LORA)) * kvg_ref[...]
    ckv_ref[...] = ckvn
    kv = jnp.dot(ckvn.astype(BF16), wukv_ref[...], preferred_element_type=F32)
    kv_t = kv.T.astype(BF16)
    per = QK_NOPE + V_HEAD
    for hd in range(N_HEADS):
        kmlat_ref[hd, :, 0:QK_NOPE] = kv[:, hd * per:hd * per + QK_NOPE].astype(BF16)
        kmlat_ref[hd, :, QK_NOPE:QK_NOPE + QK_ROPE] = kr
        vmla_ref[hd, 0:V_HEAD, :] = kv_t[hd * per + QK_NOPE:(hd + 1) * per, :]
        vmla_ref[hd, V_HEAD:VT_ROWS, :] = ones


def _mod_spec(st, tm=TM):
    tps = st["n"] // tm
    return pl.BlockSpec((1, 6, D_MODEL), lambda i, *_: (1 + i // tps if st["latent"] else 0, 0, 0))


def _input_stage(x, mod, npre, w_all, tabs, qg, wuq, wuqs, kvg, wukv, st):
    t_all = x.shape[0]
    tm = TM
    tps = st["n"] // tm
    row = lambda i: (i, 0)
    tab = lambda w: pl.BlockSpec((tm, w), lambda i: (i % tps if st["latent"] else 0, 0))
    hrow = lambda i: (0, i, 0)
    const2 = lambda i: (0, 0)
    tok = lambda w: pl.BlockSpec((tm, w), row)
    headed = lambda w: pl.BlockSpec((N_HEADS, tm, w), hrow)
    headed_t = lambda d: pl.BlockSpec((N_HEADS, d, tm), lambda i: (0, 0, i))
    full = lambda a: pl.BlockSpec(a.shape, const2)
    cos, sin, cosq, sinq = tabs
    out_shapes = [
        (headed_t(DIFF_HALF), (N_HEADS, DIFF_HALF, t_all), BF16),
        (headed_t(DIFF_HALF), (N_HEADS, DIFF_HALF, t_all), BF16),
        (headed(DIFF_HALF), (N_HEADS, t_all, DIFF_HALF), BF16),
        (headed(DIFF_HALF), (N_HEADS, t_all, DIFF_HALF), BF16),
        (headed_t(VT_ROWS), (N_HEADS, VT_ROWS, t_all), BF16),
        (tok(GROUP_WIDTH), (t_all, GROUP_WIDTH), F32),
        (tok(GROUP_WIDTH), (t_all, GROUP_WIDTH), F32),
        (tok(QKV_SLOTS), (t_all, QKV_SLOTS), BF16),
        (tok(GROUP_WIDTH), (t_all, GROUP_WIDTH), F32),
        (tok(QKV_SLOTS), (t_all, QKV_SLOTS), BF16),
        (tok(GROUP_WIDTH), (t_all, GROUP_WIDTH), F32),
        (tok(LANES), (t_all, LANES), F32),
        (headed_t(QK_NOPE + QK_ROPE), (N_HEADS, QK_NOPE + QK_ROPE, t_all), BF16),
        (tok(KV_LORA), (t_all, KV_LORA), F32),
        (headed(QK_NOPE + QK_ROPE), (N_HEADS, t_all, QK_NOPE + QK_ROPE), BF16),
        (headed_t(VT_ROWS), (N_HEADS, VT_ROWS, t_all), BF16),
    ]
    return pl.pallas_call(
        functools.partial(_in_kernel, latent=st["latent"]),
        grid=(t_all // tm,),
        in_specs=[tok(D_MODEL), _mod_spec(st, tm),
                  full(npre), *[full(w) for w in w_all], tab(GROUP_WIDTH), tab(GROUP_WIDTH), tab(Q_MLA), tab(Q_MLA),
                  full(qg), full(wuq), full(wuqs), full(kvg), full(wukv)],
        out_specs=[s for s, _, _ in out_shapes],
        out_shape=[jax.ShapeDtypeStruct(shp, dt) for _, shp, dt in out_shapes],
        compiler_params=_cparams(("parallel",)),
        name="input_stage_" + st["name"],
    )(x, mod, npre, *w_all, cos, sin, cosq, sinq, qg, wuq, wuqs, kvg, wukv)


def _attn_t_kernel(*refs, n_soft, n_new, n_ctx, lam_init):
    refs = list(refs)
    qt_refs, k_refs, vt_ref = refs[:n_soft], refs[n_soft:2 * n_soft], refs[2 * n_soft]
    pos = 2 * n_soft + 1
    if n_ctx:
        ck_refs, cvt_ref = refs[pos:pos + n_soft], refs[pos + n_soft]
        pos += n_soft + 1
    if n_soft == 2:
        lam_ref, g_ref = refs[pos:pos + 2]
        pos += 2
    o_ref = refs[pos]
    tq = o_ref.shape[0]
    nchain = n_soft * N_HEADS
    order = [(j, hd) for hd in range(N_HEADS) for j in range(n_soft)]

    def chunk(state, k_of, vt_of):
        ms, accs = state
        new_m, new_acc = list(ms), list(accs)
        ss = [jnp.dot(k_of(j, hd), qt_refs[j][hd], preferred_element_type=F32) for j, hd in order]
        ps, alphas = [], []
        for (j, hd), s in zip(order, ss):
            c = j * N_HEADS + hd
            s3 = s.reshape(s.shape[0] // 8, 8, tq)
            top = jnp.max(jnp.max(s3, axis=0), axis=0, keepdims=True)
            m_new = jnp.maximum(ms[c], jnp.broadcast_to(top, (8, tq)))
            ps.append(jnp.exp(s3 - m_new[None]).reshape(s.shape).astype(BF16))
            alphas.append(jnp.exp(ms[c] - m_new))
            new_m[c] = m_new
        for (j, hd), p, alpha in zip(order, ps, alphas):
            c = j * N_HEADS + hd
            scaled = (accs[c].reshape(VT_ROWS // 8, 8, tq) * alpha[None]).reshape(VT_ROWS, tq)
            new_acc[c] = scaled + jnp.dot(vt_of(hd), p, preferred_element_type=F32)
        return tuple(new_m), tuple(new_acc)

    state = (tuple(jnp.full((8, tq), -jnp.inf, F32) for _ in range(nchain)),
             tuple(jnp.zeros((VT_ROWS, tq), F32) for _ in range(nchain)))
    if n_ctx:
        cstep = min(TK, n_ctx)
        for i in range(n_ctx // cstep):
            state = chunk(state, lambda j, hd, i=i: ck_refs[j][hd, i * cstep:(i + 1) * cstep, :],
                          lambda hd, i=i: cvt_ref[hd, :, i * cstep:(i + 1) * cstep])
    step = min(TK, n_new)

    def body(i, st):
        start = pl.multiple_of(i * step, step)
        return chunk(st, lambda j, hd: k_refs[j][hd, pl.ds(start, step), :],
                     lambda hd: vt_ref[hd, :, pl.ds(start, step)])

    _, accs = lax.fori_loop(0, n_new // step, body, state)

    def normalised(c):
        num, den = accs[c][:V_HEAD], accs[c][V_HEAD:V_HEAD + 8]
        return (num.reshape(V_HEAD // 8, 8, tq) / den[None]).reshape(V_HEAD, tq)

    if n_soft == 2:
        lv = lam_ref[...]
        lam = (jnp.exp(jnp.sum(lv[0:1] * lv[1:2], axis=-1, keepdims=True))
               - jnp.exp(jnp.sum(lv[2:3] * lv[3:4], axis=-1, keepdims=True)) + lam_init)
    outs = []
    for hd in range(N_HEADS):
        out = normalised(hd)
        if n_soft == 2:
            a = out - lam * normalised(N_HEADS + hd)
            out = a * lax.rsqrt(jnp.mean(a * a, axis=0, keepdims=True) + EPS) * g_ref[...] * (1.0 - lam_init)
        outs.append(out)
    o_ref[...] = jnp.concatenate(outs, axis=0).T


def _attention_t(name, qts, ks, vt, ctx_ks, ctx_vt, extras, nb, n, n_ctx, lam_init=0.0):
    tq = min(TQ, n)
    nqt = n // tq
    n_soft = len(qts)
    in_specs = [pl.BlockSpec((N_HEADS, a.shape[1], tq), lambda b, i: (0, 0, b * nqt + i)) for a in qts]
    in_specs += [pl.BlockSpec((N_HEADS, n, a.shape[-1]), lambda b, i: (0, b, 0)) for a in ks]
    in_specs += [pl.BlockSpec((N_HEADS, VT_ROWS, n), lambda b, i: (0, 0, b))]
    args = list(qts) + list(ks) + [vt]
    if n_ctx:
        in_specs += [pl.BlockSpec((N_HEADS, n_ctx, a.shape[-1]), lambda b, i: (0, b, 0)) for a in ctx_ks]
        in_specs += [pl.BlockSpec((N_HEADS, VT_ROWS, n_ctx), lambda b, i: (0, 0, b))]
        args += list(ctx_ks) + [ctx_vt]
    in_specs += [pl.BlockSpec(a.shape, lambda b, i: (0, 0)) for a in extras]
    args += list(extras)
    return pl.pallas_call(
        functools.partial(_attn_t_kernel, n_soft=n_soft, n_new=n, n_ctx=n_ctx, lam_init=lam_init),
        grid=(nb, nqt),
        in_specs=in_specs,
        out_specs=pl.BlockSpec((tq, N_HEADS * V_HEAD), lambda b, i: (b * nqt + i, 0)),
        out_shape=jax.ShapeDtypeStruct((nb * n, N_HEADS * V_HEAD), F32),
        compiler_params=_cparams(("parallel", "parallel")),
        name=name,
    )(*args)


def _tri(lower):
    r = lax.broadcasted_iota(jnp.int32, (CHUNK, CHUNK), 0)
    c = lax.broadcasted_iota(jnp.int32, (CHUNK, CHUNK), 1)
    return (c <= r) if lower else (c >= r)


def _nt(a, b):
    return lax.dot_general(a, b, (((1,), (1,)), ((), ())), preferred_element_type=F32)


def _head_slices(qkv, hd):
    return tuple(qkv[:, (j * N_HEADS + hd) * LANES:(j * N_HEADS + hd) * LANES + HEAD_DIM] for j in range(3))


def _row_dot(row, mat, transpose_mat):
    row8 = jnp.broadcast_to(row, (8, row.shape[1])).astype(BF16)
    dims = (((1,), (1,)), ((), ())) if transpose_mat else (((1,), (0,)), ((), ()))
    return lax.dot_general(row8, mat, dims, preferred_element_type=F32)[0:1]


def _load_state_t(dst, src):
    for d in range(2):
        for hd in range(N_HEADS):
            dst[d, hd] = src[d, hd].T


def _mlstm_step(dirs, bias_ref):
    k_scale = HEAD_DIM ** -0.5
    lower = _tri(True).astype(F32)
    upper = _tri(False).astype(F32)
    chains = []
    for d, qkv_ref, tail_ref, _, ct_s, n_s, m_s in dirs:
        g = tail_ref[...] + bias_ref[...]
        ls = _log_sigmoid(g)
        g_t, ls_t = g.T, ls.T
        left, right = (lower, upper) if d == 0 else (upper, lower)
        cum_col = jnp.dot(left, ls, precision=HIGHEST, preferred_element_type=F32)
        cum_row = jnp.dot(ls_t, right, precision=HIGHEST, preferred_element_type=F32)
        qkv = qkv_ref[...]
        for hd in range(N_HEADS):
            ci = TAIL_GATE0 + 4 * (2 * d) + hd
            cf = ci + 4
            q, k, v = _head_slices(qkv, hd)
            chains.append(dict(
                d=d, hd=hd, q=q, k=k, v=v, li_row=g_t[ci:ci + 1, :], b_row=cum_row[cf:cf + 1, :],
                c_col=g[:, ci:ci + 1] - cum_col[:, cf:cf + 1],
                m_prev=m_s[d:d + 1, hd:hd + 1], ct=ct_s[d, hd], n=n_s[d, hd:hd + 1, :],
                refs=(ct_s, n_s, m_s)))
    for c in chains:
        c["s"] = _nt(c["k"], c["q"])
        c["v_t"] = c["v"].T
    for c in chains:
        valid = _tri(c["d"] != 0)
        log_d = jnp.where(valid, c["b_row"] + c["c_col"], -jnp.inf)
        log_inter = c["b_row"] + c["m_prev"]
        c["m_t"] = jnp.maximum(log_inter, jnp.max(log_d, axis=0, keepdims=True))
        c["w"] = c["s"] * k_scale * jnp.exp(log_d - c["m_t"])
        c["w_inter"] = jnp.exp(log_inter - c["m_t"])
    for c in chains:
        num = (jnp.dot(c["v_t"], c["w"].astype(BF16), preferred_element_type=F32)
               + c["w_inter"] * _nt(c["ct"].astype(BF16), c["q"]))
        den = jnp.sum(c["w"], axis=0, keepdims=True) + c["w_inter"] * _row_dot(c["n"], c["q"], True)
        c["h"] = num / jnp.maximum(jnp.abs(den), jnp.exp(-c["m_t"]))
    for c in chains:
        last = CHUNK - 1 if c["d"] == 0 else 0
        c["m_new"] = c["m_t"][:, last:last + 1]
        b_last = c["b_row"][:, last:last + 1]
        w_end_row = jnp.exp(b_last - c["b_row"] + c["li_row"] - c["m_new"])
        w_end_col = jnp.exp(c["c_col"] + (b_last - c["m_new"]))
        decay = jnp.exp(b_last + c["m_prev"] - c["m_new"])
        kw = (c["k"].astype(F32) * (w_end_col * k_scale)).astype(BF16)
        c["ct_new"] = decay * c["ct"] + jnp.dot(c["v_t"], kw, preferred_element_type=F32)
        c["n_new"] = decay * c["n"] + _row_dot(w_end_row, c["k"], False) * k_scale
    for j, entry in enumerate(dirs):
        entry[3][...] = jnp.concatenate([c["h"] for c in chains[j * N_HEADS:(j + 1) * N_HEADS]], axis=0).T
    for c in chains:
        d, hd = c["d"], c["hd"]
        ct_s, n_s, m_s = c["refs"]
        ct_s[d, hd] = c["ct_new"]
        n_s[d, hd:hd + 1, :] = c["n_new"]
        m_s[d:d + 1, hd:hd + 1] = c["m_new"]


def _ret_step(dirs, lg_ref):
    k_scale = HEAD_DIM ** -0.5
    s = lax.broadcasted_iota(jnp.int32, (CHUNK, CHUNK), 0)
    t = lax.broadcasted_iota(jnp.int32, (CHUNK, CHUNK), 1)
    lane = lax.broadcasted_iota(jnp.int32, (1, CHUNK), 1)
    chains = []
    for d, qkv_ref, _, st_s in dirs:
        lag = (t - s) if d == 0 else (s - t)
        pos = (lane if d == 0 else CHUNK - 1 - lane).astype(F32)
        qkv = qkv_ref[...]
        for hd in range(N_HEADS):
            q, k, v = _head_slices(qkv, hd)
            chains.append(dict(d=d, hd=hd, q=q, k=k, v=v, lag=lag, pos=pos, st=st_s[d, hd], ref=st_s,
                               lg=_log_sigmoid(lg_ref[d:d + 1, hd:hd + 1])))
    for c in chains:
        c["s"] = _nt(c["k"], c["q"])
        c["v_t"] = c["v"].T
    for c in chains:
        intra = jnp.where(c["lag"] >= 0, jnp.exp(jnp.maximum(c["lag"], 0).astype(F32) * c["lg"]), 0.0)
        c["a"] = (c["s"] * k_scale * intra).astype(BF16)
    for c in chains:
        inter = jnp.exp((c["pos"] + 1.0) * c["lg"])
        c["o"] = jnp.dot(c["v_t"], c["a"], preferred_element_type=F32) + inter * _nt(c["st"].astype(BF16), c["q"])
    row = lax.broadcasted_iota(jnp.int32, (CHUNK, HEAD_DIM), 0)
    for c in chains:
        src_pos = (row if c["d"] == 0 else CHUNK - 1 - row).astype(F32)
        tail = jnp.exp((CHUNK - 1.0 - src_pos) * c["lg"]) * k_scale
        kw = (c["k"].astype(F32) * tail).astype(BF16)
        c["st_new"] = jnp.exp(CHUNK * c["lg"]) * c["st"] + jnp.dot(c["v_t"], kw, preferred_element_type=F32)
    for j, entry in enumerate(dirs):
        entry[2][...] = jnp.concatenate([c["o"] for c in chains[j * N_HEADS:(j + 1) * N_HEADS]], axis=0).T
    for c in chains:
        c["ref"][c["d"], c["hd"]] = c["st_new"]


def _scan_kernel(tab_ref, mqf_ref, tailf_ref, mqb_ref, tailb_ref, rqf_ref, rqb_ref, bias_ref, lg_ref,
                 c0_ref, n0_ref, m0_ref, s0_ref,
                 hf_ref, hb_ref, of_ref, ob_ref, cn_ref, nn_ref, mn_ref, sn_ref, ct_s, n_s, m_s, st_s):
    i = pl.program_id(0)
    seqs = range(ct_s.shape[0])

    @pl.when(tab_ref[3, i] == 1)
    def _():
        for p in seqs:
            _load_state_t(ct_s.at[p], c0_ref.at[p])
            _load_state_t(st_s.at[p], s0_ref.at[p])
        n_s[...] = n0_ref[...]
        m_s[...] = m0_ref[...]

    mdirs, rdirs = [], []
    for p in seqs:
        state = (ct_s.at[p], n_s.at[p], m_s.at[p])
        mdirs += [(0, mqf_ref.at[p], tailf_ref.at[p], hf_ref.at[p]) + state,
                  (1, mqb_ref.at[p], tailb_ref.at[p], hb_ref.at[p]) + state]
        rdirs += [(0, rqf_ref.at[p], of_ref.at[p], st_s.at[p]), (1, rqb_ref.at[p], ob_ref.at[p], st_s.at[p])]
    _mlstm_step(mdirs, bias_ref)
    _ret_step(rdirs, lg_ref)

    @pl.when(tab_ref[4, i] == 1)
    def _():
        for p in seqs:
            _load_state_t(cn_ref.at[p], ct_s.at[p])
            _load_state_t(sn_ref.at[p], st_s.at[p])
        nn_ref[...] = n_s[...]
        mn_ref[...] = m_s[...]


def _scan_table(st):
    nch = st["n"] // CHUNK
    rows = [(c, nch - 1 - c, g, int(c == 0), int(c == nch - 1))
            for g in range(st["nb"] // _scan_group(st)) for c in range(nch)]
    return jnp.asarray(np.array(rows, dtype=np.int32).T)


def _scan_group(st):
    return next(g for g in (4, 2, 1) if st["nb"] % g == 0)


def _scans(tab, mqkv, tail, bias_row, rqkv, decay_logit, c0, n0, m0, s0, st):
    nb, n, grp = st["nb"], st["n"], _scan_group(st)
    seq3 = lambda a: a.reshape(nb, n, a.shape[-1])
    fwd = lambda i, tab: (tab[2, i], tab[0, i], 0)
    bwd = lambda i, tab: (tab[2, i], tab[1, i], 0)
    const = lambda i, tab: (0, 0)
    rows = lambda width, where: pl.BlockSpec((grp, CHUNK, width), where)
    state = lambda a: pl.BlockSpec((grp,) + a.shape[1:], lambda i, tab: (tab[2, i],) + (0,) * (a.ndim - 1))
    states = (c0, n0, m0, s0)
    grid_spec = pltpu.PrefetchScalarGridSpec(
        num_scalar_prefetch=1,
        grid=(tab.shape[1],),
        in_specs=[rows(QKV_SLOTS, fwd), rows(LANES, fwd), rows(QKV_SLOTS, bwd), rows(LANES, bwd),
                  rows(QKV_SLOTS, fwd), rows(QKV_SLOTS, bwd),
                  pl.BlockSpec((1, LANES), const), pl.BlockSpec(decay_logit.shape, const)]
        + [state(a) for a in states],
        out_specs=[rows(GROUP_WIDTH, fwd), rows(GROUP_WIDTH, bwd), rows(GROUP_WIDTH, fwd), rows(GROUP_WIDTH, bwd)]
        + [state(a) for a in states],
        scratch_shapes=[pltpu.VMEM((grp,) + a.shape[1:], F32) for a in states],
    )
    outs = pl.pallas_call(
        _scan_kernel,
        grid_spec=grid_spec,
        out_shape=[jax.ShapeDtypeStruct((nb, n, GROUP_WIDTH), F32)] * 4
        + [jax.ShapeDtypeStruct(a.shape, F32) for a in states],
        compiler_params=_cparams(("arbitrary",)),
        name="recurrent_scans",
    )(tab, seq3(mqkv), seq3(tail), seq3(mqkv), seq3(tail), seq3(rqkv), seq3(rqkv), bias_row, decay_logit, *states)
    return [o.reshape(nb * n, GROUP_WIDTH) for o in outs[:4]] + list(outs[4:])


def _proj_kernel(x_ref, w_ref, o_ref):
    o_ref[...] = _bdot(x_ref[...], w_ref[...]).astype(o_ref.dtype)


def _project(x, w, dtype):
    return pl.pallas_call(
        _proj_kernel,
        out_shape=jax.ShapeDtypeStruct((x.shape[0], w.shape[1]), dtype),
        name="ctx_kv_proj",
    )(x, w)


def _first_argmax_mask(cur, axis, size):
    io = lax.broadcasted_iota(jnp.int32, cur.shape, axis)
    mx = jnp.max(cur, axis=axis, keepdims=True)
    ix = jnp.min(jnp.where(cur == mx, io, size), axis=axis, keepdims=True)
    return io == ix


def _route(scores_t, bias_col):
    tm = scores_t.shape[1]
    per = N_EXPERTS // N_GROUPS
    sel = scores_t + bias_col
    s3 = sel.reshape(N_GROUPS, per, tm)
    hit1 = _first_argmax_mask(s3, 1, per)
    m1 = jnp.max(s3, axis=1, keepdims=True)
    m2 = jnp.max(jnp.where(hit1, -jnp.inf, s3), axis=1, keepdims=True)
    cur = m1 + m2
    gsel = None
    for _ in range(TOPK_GROUPS):
        hit = _first_argmax_mask(cur, 0, N_GROUPS)
        gsel = hit if gsel is None else jnp.logical_or(gsel, hit)
        cur = jnp.where(hit, -jnp.inf, cur)
    cur = jnp.where(gsel, s3, -jnp.inf).reshape(N_EXPERTS, tm)
    chosen, hits = None, []
    for _ in range(TOP_K):
        hit = _first_argmax_mask(cur, 0, N_EXPERTS)
        hits.append(hit)
        chosen = hit if chosen is None else jnp.logical_or(chosen, hit)
        cur = jnp.where(hit, -jnp.inf, cur)
    w = jnp.where(chosen, scores_t, 0.0)
    return w / jnp.sum(w, axis=0, keepdims=True) * ROUTED_SCALE, chosen, hits


def _out_kernel(oa_ref, od_ref, hf_ref, hb_ref, mo_ref, of_ref, ob_ref, rg_ref,
                x_ref, mod_ref, mg_ref, npost_ref, npre_ref, wout_ref, wrt_ref, rb_ref,
                x1_ref, h2_ref, hpk_ref, eidx_ref, rank_ref, wk_ref, cnt_ref, count_s):
    @pl.when(pl.program_id(0) == 0)
    def _():
        count_s[...] = jnp.zeros(count_s.shape, F32)

    a = oa_ref[...]
    dd = od_ref[...]
    b = _rms_heads(hf_ref[...] + hb_ref[...]) * mg_ref[...] * _sigmoid(mo_ref[...])
    r = _rms_heads(of_ref[...] + ob_ref[...]) * _silu(rg_ref[...])
    mix = sum(jnp.dot(part.astype(BF16), wout_ref[j * GROUP_WIDTH:(j + 1) * GROUP_WIDTH, :],
                      preferred_element_type=F32) for j, part in enumerate((a, b, r, dd)))
    m = mod_ref[0]
    x1 = x_ref[...] + m[2:3] * (_rms(mix) * npost_ref[...])
    x1_ref[...] = x1
    h2 = (_rms(x1) * npre_ref[...] * (1.0 + m[4:5]) + m[3:4]).astype(BF16)
    h2_ref[...] = h2
    hpk_ref[...] = _pack_rows(h2)

    logits_t = lax.dot_general(wrt_ref[...], h2, (((1,), (1,)), ((), ())), preferred_element_type=F32)
    w_t, chosen, hits = _route(_sigmoid(logits_t), rb_ref[...])
    tm = w_t.shape[1]
    src = lax.broadcasted_iota(jnp.int32, (tm, tm), 0)
    dst = lax.broadcasted_iota(jnp.int32, (tm, tm), 1)
    before = jnp.where(src < dst, 1.0, 0.0).astype(BF16)
    picked = jnp.where(chosen, 1.0, 0.0)
    rank = jnp.dot(picked.astype(BF16), before, preferred_element_type=F32) + count_s[...]
    count_s[...] += jnp.sum(picked, axis=1, keepdims=True)
    cnt_ref[...] = jnp.broadcast_to(count_s[...], cnt_ref.shape)
    e_io = lax.broadcasted_iota(jnp.int32, w_t.shape, 0)
    pick = lambda hit, val: jnp.sum(jnp.where(hit, val, jnp.zeros_like(val)), axis=0, keepdims=True)
    eidx_ref[...] = jnp.concatenate([pick(h, e_io) for h in hits], axis=0)
    rank_ref[...] = jnp.concatenate([pick(h, rank) for h in hits], axis=0).astype(jnp.int32)
    wk = jnp.concatenate([pick(h, w_t) for h in hits] + [jnp.zeros((LANES - TOP_K, tm), F32)], axis=0)
    wk_ref[...] = wk.T


def _output_stage(oa, od, hf, hb, mo, of, ob, rg, x, mod, mg, npost, npre, wout, wrt, rb, st):
    t_all = x.shape[0]
    row = lambda i: (i, 0)
    const2 = lambda i: (0, 0)
    tok = lambda w: pl.BlockSpec((TM, w), row)
    full = lambda a: pl.BlockSpec(a.shape, const2)
    return pl.pallas_call(
        _out_kernel,
        grid=(t_all // TM,),
        in_specs=[tok(GROUP_WIDTH)] * 8 + [tok(D_MODEL), _mod_spec(st),
                                   full(mg), full(npost), full(npre), full(wout), full(wrt), full(rb)],
        out_specs=[tok(D_MODEL), tok(D_MODEL), tok(D_MODEL // 2),
                   pl.BlockSpec((TOP_K, TM), lambda i: (0, i)), pl.BlockSpec((TOP_K, TM), lambda i: (0, i)),
                   tok(LANES), pl.BlockSpec((N_EXPERTS, LANES), const2)],
        out_shape=[jax.ShapeDtypeStruct((t_all, D_MODEL), F32),
                   jax.ShapeDtypeStruct((t_all, D_MODEL), BF16),
                   jax.ShapeDtypeStruct((t_all, D_MODEL // 2), jnp.uint32),
                   jax.ShapeDtypeStruct((TOP_K, t_all), jnp.int32),
                   jax.ShapeDtypeStruct((TOP_K, t_all), jnp.int32),
                   jax.ShapeDtypeStruct((t_all, LANES), F32),
                   jax.ShapeDtypeStruct((N_EXPERTS, LANES), F32)],
        scratch_shapes=[pltpu.VMEM((N_EXPERTS, 1), F32)],
        compiler_params=_cparams(("arbitrary",)),
        name="output_stage_" + st["name"],
    )(oa, od, hf, hb, mo, of, ob, rg, x, mod, mg, npost, npre, wout, wrt, rb)


SC_CORES, SC_SUBCORES = 2, 16
SC_WORKERS = SC_CORES * SC_SUBCORES
SLOT_BLOCK = 512
DISPATCH_ROWS = 64
COMBINE_ROWS = 64


def _sc_mesh():
    return plsc.VectorSubcoreMesh(core_axis_name="core", subcore_axis_name="subcore")


def _sc_worker():
    return lax.axis_index("subcore") * SC_CORES + lax.axis_index("core")


def _sc_dispatch(hpk, dest, n_slots):
    t_all, width = hpk.shape
    per_worker = t_all // SC_WORKERS
    assert t_all % (SC_WORKERS * DISPATCH_ROWS) == 0

    @functools.partial(
        pl.kernel, mesh=_sc_mesh(), out_type=jax.ShapeDtypeStruct((n_slots, width), hpk.dtype),
        scratch_types=[pltpu.VMEM((DISPATCH_ROWS, width), hpk.dtype)]
        + [pltpu.VMEM((DISPATCH_ROWS,), jnp.int32)] * TOP_K + [pltpu.SemaphoreType.DMA])
    def dispatch(x_hbm, d_hbm, o_hbm, rows_v, *rest):
        idx, sem = rest[:TOP_K], rest[TOP_K]

        @pl.loop(0, per_worker // DISPATCH_ROWS)
        def _(j):
            base = _sc_worker() * per_worker + j * DISPATCH_ROWS
            pltpu.sync_copy(x_hbm.at[pl.ds(base, DISPATCH_ROWS)], rows_v)
            for k in range(TOP_K):
                pltpu.sync_copy(d_hbm.at[pl.ds(k * t_all + base, DISPATCH_ROWS)], idx[k])
            copies = [pltpu.async_copy(rows_v, o_hbm.at[idx[k]], sem) for k in range(TOP_K)]
            for c in copies:
                c.wait()

    return dispatch(hpk, dest)


def _sc_combine(yb, dest, t_all):
    width = yb.shape[1]
    per_worker = t_all // SC_WORKERS
    assert t_all % (SC_WORKERS * COMBINE_ROWS) == 0

    @functools.partial(
        pl.kernel, mesh=_sc_mesh(), out_type=jax.ShapeDtypeStruct((TOP_K, t_all, width), yb.dtype),
        scratch_types=[pltpu.VMEM((COMBINE_ROWS, width), yb.dtype)] * 2
        + [pltpu.VMEM((COMBINE_ROWS,), jnp.int32)] * TOP_K + [pltpu.SemaphoreType.DMA] * 2)
    def combine(y_hbm, d_hbm, o_hbm, rows_a, rows_b, *rest):
        idx, sems = rest[:TOP_K], rest[TOP_K:]
        bufs = (rows_a, rows_b)

        @pl.loop(0, per_worker // COMBINE_ROWS)
        def _(j):
            base = _sc_worker() * per_worker + j * COMBINE_ROWS
            for k in range(TOP_K):
                pltpu.sync_copy(d_hbm.at[pl.ds(k * t_all + base, COMBINE_ROWS)], idx[k])
            pending = pltpu.async_copy(y_hbm.at[idx[0]], bufs[0], sems[0])
            for k in range(TOP_K):
                pending.wait()
                if k + 1 < TOP_K:
                    pending = pltpu.async_copy(y_hbm.at[idx[k + 1]], bufs[(k + 1) % 2], sems[(k + 1) % 2])
                pltpu.sync_copy(bufs[k % 2], o_hbm.at[k, pl.ds(base, COMBINE_ROWS)])

    return combine(yb, dest)


def _pack_rows(x):
    bits = pltpu.bitcast(x.astype(BF16).astype(F32), jnp.uint32)
    w = x.shape[-1] // 2
    return (bits[..., :w] >> 16) | (bits[..., w:] & jnp.uint32(0xFFFF0000))


def _unpack_rows(words):
    return pltpu.bitcast(words << 16, F32), pltpu.bitcast(words & jnp.uint32(0xFFFF0000), F32)


def _expert_kernel(be_ref, used_ref, xs_ref, wg_ref, wu_ref, wd_ref, y_ref, wg_s, wu_s, wd_s):
    b = pl.program_id(0)
    fresh = jnp.logical_or(b == 0, be_ref[b] != be_ref[jnp.maximum(b - 1, 0)])

    @pl.when(jnp.logical_and(fresh, b < used_ref[0]))
    def _():
        wg_s[...] = wg_ref[0, 0].astype(BF16)
        wu_s[...] = wu_ref[0, 0].astype(BF16)
        wd_s[...] = wd_ref[0, 0].astype(BF16)

    @pl.when(b < used_ref[0])
    def _():
        lo, hi = (v.astype(BF16) for v in _unpack_rows(xs_ref[...]))
        half = D_MODEL // 2
        gate = (jnp.dot(lo, wg_s[:half, :], preferred_element_type=F32)
                + jnp.dot(hi, wg_s[half:, :], preferred_element_type=F32))
        up = (jnp.dot(lo, wu_s[:half, :], preferred_element_type=F32)
              + jnp.dot(hi, wu_s[half:, :], preferred_element_type=F32))
        y = jnp.dot((_silu(gate) * up).astype(BF16), wd_s[...], preferred_element_type=F32)
        y_ref[...] = _pack_rows(y)


def _expert_blocks(layer, xs, block_expert, blocks_used, wg, wu, wdn):
    n_slots = xs.shape[0]
    wspec = lambda shape: pl.BlockSpec((1, 1) + shape, lambda b, be, used: (layer, be[b], 0, 0))
    rows = lambda b, be, used: (jnp.minimum(b, used[0] - 1), 0)
    grid_spec = pltpu.PrefetchScalarGridSpec(
        num_scalar_prefetch=2,
        grid=(n_slots // SLOT_BLOCK,),
        in_specs=[pl.BlockSpec((SLOT_BLOCK, D_MODEL // 2), rows),
                  wspec((D_MODEL, D_EXPERT)), wspec((D_MODEL, D_EXPERT)), wspec((D_EXPERT, D_MODEL))],
        out_specs=pl.BlockSpec((SLOT_BLOCK, D_MODEL // 2), rows),
        scratch_shapes=[pltpu.VMEM((D_MODEL, D_EXPERT), BF16), pltpu.VMEM((D_MODEL, D_EXPERT), BF16),
                        pltpu.VMEM((D_EXPERT, D_MODEL), BF16)])
    return pl.pallas_call(
        _expert_kernel, grid_spec=grid_spec,
        out_shape=jax.ShapeDtypeStruct((n_slots, D_MODEL // 2), jnp.uint32),
        compiler_params=_cparams(("arbitrary",)),
        name="moe_experts",
    )(block_expert, blocks_used, xs, wg, wu, wdn)


def _moe_out_kernel(g_ref, wk_ref, h_ref, x1_ref, mod_ref, npost_ref, sg_ref, su_ref, sd_ref, o_ref):
    h = h_ref[...]
    act = _silu(jnp.dot(h, sg_ref[...], preferred_element_type=F32)) * jnp.dot(h, su_ref[...],
                                                                              preferred_element_type=F32)
    moe = jnp.dot(act.astype(BF16), sd_ref[...], preferred_element_type=F32)
    wk = wk_ref[...]
    lo, hi = 0.0, 0.0
    for k in range(TOP_K):
        g_lo, g_hi = _unpack_rows(g_ref[k])
        lo = lo + g_lo * wk[:, k:k + 1]
        hi = hi + g_hi * wk[:, k:k + 1]
    moe = moe + jnp.concatenate([lo, hi], axis=-1)
    m = mod_ref[0]
    o_ref[...] = x1_ref[...] + m[5:6] * (_rms(moe) * npost_ref[...])


def _moe_out_stage(g, wk, h2, x1, mod, npost, sg, su, sd, st):
    t_all = h2.shape[0]
    row = lambda i: (i, 0)
    full = lambda a: pl.BlockSpec(a.shape, lambda i: (0, 0))
    return pl.pallas_call(
        _moe_out_kernel,
        grid=(t_all // TM,),
        in_specs=[pl.BlockSpec((TOP_K, TM, D_MODEL // 2), lambda i: (0, i, 0)), pl.BlockSpec((TM, LANES), row),
                  pl.BlockSpec((TM, D_MODEL), row), pl.BlockSpec((TM, D_MODEL), row), _mod_spec(st),
                  full(npost), full(sg), full(su), full(sd)],
        out_specs=pl.BlockSpec((TM, D_MODEL), row),
        out_shape=jax.ShapeDtypeStruct((t_all, D_MODEL), F32),
        compiler_params=_cparams(("parallel",)),
        name="moe_combine_" + st["name"],
    )(g, wk, h2, x1, mod, npost, sg, su, sd)


def _moe_stage(layer, h2, hpk, eidx, rank, wk, counts, x1, mod, npost, wg, wu, wdn, sg, su, sd, st):
    t_all = h2.shape[0]
    n_blocks = -(-(t_all * TOP_K + N_EXPERTS * (SLOT_BLOCK - 1)) // SLOT_BLOCK)
    cnt = counts[:, 0].astype(jnp.int32)
    padded = (cnt + SLOT_BLOCK - 1) // SLOT_BLOCK * SLOT_BLOCK
    pad_end = jnp.cumsum(padded)
    pad_start = pad_end - padded
    experts = jnp.arange(N_EXPERTS, dtype=jnp.int32)[:, None, None]
    dest = rank + jnp.sum(jnp.where(eidx[None] == experts, pad_start[:, None, None], 0), axis=0)
    dest = dest.reshape(TOP_K * t_all)
    block_start = jnp.arange(n_blocks, dtype=jnp.int32) * SLOT_BLOCK
    block_expert = jnp.minimum(jnp.sum((pad_end[None, :] <= block_start[:, None]).astype(jnp.int32), axis=1),
                               N_EXPERTS - 1)
    blocks_used = (pad_end[-1:] // SLOT_BLOCK).astype(jnp.int32)

    xs = _sc_dispatch(hpk, dest, n_blocks * SLOT_BLOCK)
    yb = _expert_blocks(layer, xs, block_expert, blocks_used, wg, wu, wdn)
    g = _sc_combine(yb, dest, t_all)
    return _moe_out_stage(g, wk, h2, x1, mod, npost, sg, su, sd, st)


_PERM32 = np.concatenate([np.arange(8, 16), np.arange(0, 8), np.arange(24, 32), np.arange(16, 24)])


def _rope_tables(n):
    f32 = np.float32
    pos = np.arange(n)
    quarter = QK_ROPE // 4
    inv = (f32(1.0) / (f32(ROPE_THETA) ** (np.arange(quarter, dtype=f32) / f32(quarter)))).astype(f32)
    ang_r = (pos // GRID_W).astype(f32)[:, None] * inv[None, :]
    ang_c = (pos % GRID_W).astype(f32)[:, None] * inv[None, :]
    cos32 = np.concatenate([np.cos(ang_r)] * 2 + [np.cos(ang_c)] * 2, axis=-1).astype(f32)
    sin32 = np.concatenate([-np.sin(ang_r), np.sin(ang_r), -np.sin(ang_c), np.sin(ang_c)], axis=-1).astype(f32)
    cos256, sin256 = np.tile(cos32, (1, 8)), np.tile(sin32, (1, 8))
    one, zero = np.ones((n, QK_NOPE), f32), np.zeros((n, QK_NOPE), f32)
    cosq = np.concatenate([one, cos32] * N_HEADS, axis=-1)
    sinq = np.concatenate([zero, sin32] * N_HEADS, axis=-1)
    return tuple(jnp.asarray(t) for t in (cos256, sin256, cosq, sinq))


def _layer_weights(w_in_l, w_uq_l):
    gw = GROUP_WIDTH
    g0 = W_BEFORE_GATES
    b0 = g0 + 4 * N_HEADS
    kr0 = b0 + W_AFTER_GATES
    assert kr0 + QK_ROPE == w_in_l.shape[1]
    w_a, m_g, w_b, d_kr = w_in_l[:, :g0], w_in_l[:, g0:b0], w_in_l[:, b0:kr0], w_in_l[:, kr0:]
    perm256 = np.concatenate([_PERM32 + 32 * j for j in range(8)])
    pad = lambda w, n: jnp.pad(w, ((0, 0), (0, n - w.shape[1])))
    w_c = jnp.concatenate([pad(jnp.concatenate([d_kr, m_g], axis=-1), LANES),
                           w_a[:, perm256], w_a[:, gw + perm256], pad(d_kr[:, _PERM32], LANES)], axis=-1)
    per = QK_NOPE + QK_ROPE
    permq = np.concatenate([np.concatenate([np.arange(QK_NOPE), QK_NOPE + _PERM32]) + per * j
                            for j in range(N_HEADS)])
    return ((w_a.astype(BF16), w_b.astype(BF16), w_c.astype(BF16)), w_uq_l.astype(BF16),
            w_uq_l[:, permq].astype(BF16))


def kernel(x_prompt, x_sample, cache_diff_k, cache_diff_v, state_mlstm_C, state_mlstm_n, state_mlstm_m, state_ret_S, cache_mla_ckv, cache_mla_krope, c, c_ctx, w_mod, b_mod, norm_pre, norm_post, w_in, w_out, diff_lambda, diff_norm, mlstm_gate_bias, mlstm_norm, ret_decay_logit, mla_q_norm, mla_w_uq, mla_kv_norm, mla_w_ukv, moe_w_router, moe_router_bias, moe_w_gate, moe_w_up, moe_w_down, shared_w_gate, shared_w_up, shared_w_down):
    bp, n_p, _ = x_prompt.shape
    bs, n_s, _ = x_sample.shape
    depth = w_in.shape[0]
    past = cache_diff_k.shape[2]
    assert (bp * n_p) % TM == 0 and n_s % TM == 0 and n_p % CHUNK == 0 and n_s % min(TK, n_s) == 0
    assert past % min(TK, past) == 0 and bs + 1 <= 8 and n_s % GRID_W == 0

    streams = (dict(name="ctx", nb=bp, n=n_p, latent=False), dict(name="latent", nb=bs, n=n_s, latent=True))
    xs_by_stream = [x_prompt.reshape(bp * n_p, D_MODEL), x_sample.reshape(bs * n_s, D_MODEL)]
    cond = jnp.zeros((8, D_MODEL), F32).at[0].set(c_ctx).at[1:1 + bs].set(c)
    mod_all = _modulation(cond, w_mod, b_mod).reshape(depth, 8, 6, D_MODEL)
    tabs = _rope_tables(n_s)
    scan_tabs = [_scan_table(st) for st in streams]

    outs = [[] for _ in range(8)]
    for l in range(depth):
        lam_init = 0.8 - 0.6 * math.exp(-0.3 * l)
        mod = mod_all[l]
        w_all, wuq, wuqs = _layer_weights(w_in[l], mla_w_uq[l])
        wukv = mla_w_ukv[l].astype(BF16)
        wout, wrt, rbias = w_out[l].astype(BF16), moe_w_router[l].T.astype(BF16), moe_router_bias[l][:, None]
        shared = [w[l].astype(BF16) for w in (shared_w_gate, shared_w_up, shared_w_down)]
        lamv, dg = diff_lambda[l], diff_norm[l][:, None]
        bias_row = jnp.zeros((1, LANES), F32).at[0, TAIL_GATE0:TAIL_GATE0 + 16].set(mlstm_gate_bias[l].reshape(16))

        with_ones = lambda vt: jnp.concatenate([vt, jnp.ones((N_HEADS, VT_ONES, bs * past), BF16)], axis=1)
        ck = jnp.transpose(cache_diff_k[:, l], (2, 0, 1, 3)).reshape(N_HEADS, bs * past, HEAD_DIM).astype(BF16)
        cvt = with_ones(jnp.transpose(cache_diff_v[:, l], (2, 3, 0, 1)).reshape(N_HEADS, HEAD_DIM, bs * past)
                        .astype(BF16))
        kvc = _project(cache_mla_ckv[:, l].reshape(bs * past, KV_LORA), wukv, BF16)
        kvc = kvc.reshape(bs * past, N_HEADS, QK_NOPE + V_HEAD)
        krc = jnp.broadcast_to(cache_mla_krope[:, l].reshape(bs * past, 1, QK_ROPE).astype(BF16),
                               (bs * past, N_HEADS, QK_ROPE))
        kc = jnp.transpose(jnp.concatenate([kvc[..., :QK_NOPE], krc], axis=-1), (1, 0, 2))
        vct = with_ones(jnp.transpose(kvc[..., QK_NOPE:], (1, 2, 0)))

        for si, st in enumerate(streams):
            x, nb, n = xs_by_stream[si], st["nb"], st["n"]
            (aq1t, aq2t, ak1, ak2, avt, ak, av, mqkv, mo, rqkv, rg, tail, qmlat, ckv, kmla, vmlat) = _input_stage(
                x, mod, norm_pre[l, 0:1], w_all, tabs, mla_q_norm[l][None], wuq, wuqs,
                mla_kv_norm[l][None], wukv, st)
            if st["latent"]:
                oa = _attention_t("diff_attn_latent", [aq1t, aq2t], [ak1, ak2], avt,
                                  [ck[..., :DIFF_HALF], ck[..., DIFF_HALF:]], cvt, [lamv, dg], nb, n, past, lam_init)
                od = _attention_t("mla_attn_latent", [qmlat], [kmla], vmlat, [kc], vct, [], nb, n, past)
                states = [s[:, l] for s in (state_mlstm_C, state_mlstm_n, state_mlstm_m, state_ret_S)]
            else:
                oa = _attention_t("diff_attn_ctx", [aq1t, aq2t], [ak1, ak2], avt, None, None, [lamv, dg],
                                  nb, n, 0, lam_init)
                od = _attention_t("mla_attn_ctx", [qmlat], [kmla], vmlat, None, None, [], nb, n, 0)
                states = [jnp.zeros((nb,) + s.shape[2:], F32)
                          for s in (state_mlstm_C, state_mlstm_n, state_mlstm_m, state_ret_S)]
            hf, hb, of, ob, c_n, n_n, m_n, s_n = _scans(scan_tabs[si], mqkv, tail, bias_row, rqkv,
                                                        ret_decay_logit[l], *states, st)

            x1, h2, hpk, eidx, rank, wk, counts = _output_stage(
                oa, od, hf, hb, mo, of, ob, rg, x, mod, mlstm_norm[l][None], norm_post[l, 0:1],
                norm_pre[l, 1:2], wout, wrt, rbias, st)
            xs_by_stream[si] = _moe_stage(l, h2, hpk, eidx, rank, wk, counts, x1, mod, norm_post[l, 1:2],
                                          moe_w_gate, moe_w_up, moe_w_down, *shared, st)
            if not st["latent"]:
                new = (ak.reshape(nb, n, N_HEADS, HEAD_DIM), av.reshape(nb, n, N_HEADS, HEAD_DIM), c_n, n_n, m_n,
                       s_n, ckv.reshape(nb, n, KV_LORA), tail[:, :QK_ROPE].reshape(nb, n, QK_ROPE))
                for o, v in zip(outs, new):
                    o.append(v)

    return (xs_by_stream[0].reshape(bp, n_p, D_MODEL), xs_by_stream[1].reshape(bs, n_s, D_MODEL)) + tuple(
        jnp.stack(o, axis=1) for o in outs)
```

```python
import functools
import math

import numpy as np
import jax
import jax.numpy as jnp
from jax import lax
from jax.experimental import pallas as pl
from jax.experimental.pallas import tpu as pltpu
from jax.experimental.pallas import tpu_sc as plsc

F32 = jnp.float32
BF16 = jnp.bfloat16
HIGHEST = lax.Precision.HIGHEST

D_MODEL = 1024
GRID_W = 64
GROUP_WIDTH = 256
HEAD_DIM = 64
N_HEADS = 4
DIFF_HALF = 32
ROPE_THETA = 10000.0
Q_LORA = 256
KV_LORA = 128
QK_NOPE = 64
QK_ROPE = 32
V_HEAD = 64
N_EXPERTS = 64
TOP_K = 8
N_GROUPS = 8
TOPK_GROUPS = 4
D_EXPERT = 256
ROUTED_SCALE = 2.5
CHUNK = 128
EPS = 1e-6
Q_MLA = N_HEADS * (QK_NOPE + QK_ROPE)

LANES = 128
VMEM_LIMIT = 56 * 1024 * 1024

TM = 512
TQ = 256
TK = 4096

C_AQ, C_AK, C_AV, C_MQKV, C_MO = (0, 0), (0, 256), (0, 512), (0, 768), (0, 1536)
C_RQKV, C_RG, C_CQ, C_CKV = (1, 0), (1, 768), (1, 1024), (1, 1280)
C_TAIL, C_AQS, C_AKS, C_TAILS = (2, 0), (2, 128), (2, 384), (2, 640)
W_BEFORE_GATES, W_AFTER_GATES = 1792, 1408
TAIL_GATE0 = QK_ROPE
QKV_SLOTS = 3 * N_HEADS * LANES
VT_ONES = 16
VT_ROWS = V_HEAD + VT_ONES


def _cparams(sem):
    return pltpu.CompilerParams(dimension_semantics=sem, vmem_limit_bytes=VMEM_LIMIT)


def _rms(x):
    return x * lax.rsqrt(jnp.mean(x * x, axis=-1, keepdims=True) + EPS)


def _head_mean_matrix(width):
    r = lax.broadcasted_iota(jnp.int32, (width, width), 0) // HEAD_DIM
    c = lax.broadcasted_iota(jnp.int32, (width, width), 1) // HEAD_DIM
    return jnp.where(r == c, 1.0 / HEAD_DIM, 0.0).astype(BF16)


def _rms_heads(x):
    sq = x * x
    hi = sq.astype(BF16)
    lo = (sq - hi.astype(F32)).astype(BF16)
    g = _head_mean_matrix(x.shape[-1])
    ms = jnp.dot(hi, g, preferred_element_type=F32) + jnp.dot(lo, g, preferred_element_type=F32)
    return x * lax.rsqrt(ms + EPS)


def _sigmoid(x):
    return 1.0 / (1.0 + jnp.exp(-x))


def _silu(x):
    return x * _sigmoid(x)


def _log_sigmoid(x):
    return jnp.minimum(x, 0.0) - jnp.log1p(jnp.exp(-jnp.abs(x)))


def _bdot(a, b):
    return jnp.dot(a.astype(BF16), b.astype(BF16), preferred_element_type=F32)


def _mod_kernel(c_ref, w_ref, b_ref, o_ref):
    o_ref[0] = _bdot(_silu(c_ref[...]), w_ref[0]) + b_ref[0]


def _modulation(cond, w_mod, b_mod):
    depth, _, n = w_mod.shape
    tn = 1536
    return pl.pallas_call(
        _mod_kernel,
        grid=(depth, n // tn),
        in_specs=[pl.BlockSpec((8, D_MODEL), lambda l, j: (0, 0)),
                  pl.BlockSpec((1, D_MODEL, tn), lambda l, j: (l, 0, j)),
                  pl.BlockSpec((1, 1, tn), lambda l, j: (l, 0, j))],
        out_specs=pl.BlockSpec((1, 8, tn), lambda l, j: (l, 0, j)),
        out_shape=jax.ShapeDtypeStruct((depth, 8, n), F32),
        compiler_params=_cparams(("parallel", "parallel")),
        name="adaln_mod",
    )(cond, w_mod, b_mod.reshape(depth, 1, n))


def _in_kernel(x_ref, mod_ref, npre_ref, wa_ref, wb_ref, wc_ref, cos_ref, sin_ref, cosq_ref, sinq_ref,
               qg_ref, wuq_ref, wuqs_ref, kvg_ref, wukv_ref,
               aq1_ref, aq2_ref, ak1t_ref, ak2t_ref, avh_ref, ak_ref, av_ref,
               mqkv_ref, mo_ref, rqkv_ref, rg_ref, tail_ref,
               qmla_ref, ckv_ref, kmlat_ref, vmla_ref, *, latent):
    m = mod_ref[0]
    h = (_rms(x_ref[...]) * npre_ref[...] * (1.0 + m[1:2]) + m[0:1]).astype(BF16)

    def proj(col, width):
        w_ref = (wa_ref, wb_ref, wc_ref)[col[0]]
        return jnp.dot(h, w_ref[:, col[1]:col[1] + width], preferred_element_type=F32)

    def rotated(c0, c0_swapped, width, cos, sin):
        return proj(c0, width) * cos + proj(c0_swapped, width) * sin if latent else proj(c0, width)

    cos = cos_ref[...]
    sin = sin_ref[...]
    aq = rotated(C_AQ, C_AQS, GROUP_WIDTH, cos, sin) * (DIFF_HALF ** -0.5)
    ak = rotated(C_AK, C_AKS, GROUP_WIDTH, cos, sin)
    av = proj(C_AV, GROUP_WIDTH)
    ak_ref[...] = ak
    av_ref[...] = av
    aq_t, av_t = aq.T.astype(BF16), av.T.astype(BF16)
    ones = jnp.ones((VT_ONES, aq_t.shape[1]), BF16)
    for hd in range(N_HEADS):
        lo = hd * HEAD_DIM
        aq1_ref[hd] = aq_t[lo:lo + DIFF_HALF, :]
        aq2_ref[hd] = aq_t[lo + DIFF_HALF:lo + HEAD_DIM, :]
        ak1t_ref[hd] = ak[:, lo:lo + DIFF_HALF].astype(BF16)
        ak2t_ref[hd] = ak[:, lo + DIFF_HALF:lo + HEAD_DIM].astype(BF16)
        avh_ref[hd, 0:HEAD_DIM, :] = av_t[lo:lo + HEAD_DIM, :]
        avh_ref[hd, HEAD_DIM:VT_ROWS, :] = ones

    def head_slots(x):
        x = x.astype(BF16)
        zero = jnp.zeros((x.shape[0], LANES - HEAD_DIM), BF16)
        parts = []
        for j in range(3 * N_HEADS):
            parts += [x[:, j * HEAD_DIM:(j + 1) * HEAD_DIM], zero]
        return jnp.concatenate(parts, axis=-1)

    mqkv_ref[...] = head_slots(proj(C_MQKV, 3 * GROUP_WIDTH))
    mo_ref[...] = proj(C_MO, GROUP_WIDTH)
    rqkv_ref[...] = head_slots(proj(C_RQKV, 3 * GROUP_WIDTH))
    rg_ref[...] = proj(C_RG, GROUP_WIDTH)

    tail = proj(C_TAIL, LANES)
    tail_ref[...] = tail
    kr = tail * cos[:, :LANES] + proj(C_TAILS, LANES) * sin[:, :LANES] if latent else tail
    kr = kr[:, :QK_ROPE].astype(BF16)

    cqn = (_rms(proj(C_CQ, Q_LORA)) * qg_ref[...]).astype(BF16)
    qmla = jnp.dot(cqn, wuq_ref[...], preferred_element_type=F32)
    if latent:
        qmla = qmla * cosq_ref[...] + jnp.dot(cqn, wuqs_ref[...], preferred_element_type=F32) * sinq_ref[...]
    qmla_t = (qmla * ((QK_NOPE + QK_ROPE) ** -0.5)).T.astype(BF16)
    for hd in range(N_HEADS):
        lo = hd * (QK_NOPE + QK_ROPE)
        qmla_ref[hd] = qmla_t[lo:lo + QK_NOPE + QK_ROPE, :]
    ckvn = _rms(proj(C_CKV, KV_LORA)) * kvg_ref[...]
    ckv_ref[...] = ckvn
    kv = jnp.dot(ckvn.astype(BF16), wukv_ref[...], preferred_element_type=F32)
    kv_t = kv.T.astype(BF16)
    per = QK_NOPE + V_HEAD
    for hd in range(N_HEADS):
        kmlat_ref[hd, :, 0:QK_NOPE] = kv[:, hd * per:hd * per + QK_NOPE].astype(BF16)
        kmlat_ref[hd, :, QK_NOPE:QK_NOPE + QK_ROPE] = kr
        vmla_ref[hd, 0:V_HEAD, :] = kv_t[hd * per + QK_NOPE:(hd + 1) * per, :]
        vmla_ref[hd, V_HEAD:VT_ROWS, :] = ones


def _mod_spec(st, tm=TM):
    tps = st["n"] // tm
    return pl.BlockSpec((1, 6, D_MODEL), lambda i, *_: (1 + i // tps if st["latent"] else 0, 0, 0))


def _input_stage(x, mod, npre, w_all, tabs, qg, wuq, wuqs, kvg, wukv, st):
    t_all = x.shape[0]
    tm = TM
    tps = st["n"] // tm
    row = lambda i: (i, 0)
    tab = lambda w: pl.BlockSpec((tm, w), lambda i: (i % tps if st["latent"] else 0, 0))
    hrow = lambda i: (0, i, 0)
    const2 = lambda i: (0, 0)
    tok = lambda w: pl.BlockSpec((tm, w), row)
    headed = lambda w: pl.BlockSpec((N_HEADS, tm, w), hrow)
    headed_t = lambda d: pl.BlockSpec((N_HEADS, d, tm), lambda i: (0, 0, i))
    full = lambda a: pl.BlockSpec(a.shape, const2)
    cos, sin, cosq, sinq = tabs
    out_shapes = [
        (headed_t(DIFF_HALF), (N_HEADS, DIFF_HALF, t_all), BF16),
        (headed_t(DIFF_HALF), (N_HEADS, DIFF_HALF, t_all), BF16),
        (headed(DIFF_HALF), (N_HEADS, t_all, DIFF_HALF), BF16),
        (headed(DIFF_HALF), (N_HEADS, t_all, DIFF_HALF), BF16),
        (headed_t(VT_ROWS), (N_HEADS, VT_ROWS, t_all), BF16),
        (tok(GROUP_WIDTH), (t_all, GROUP_WIDTH), F32),
        (tok(GROUP_WIDTH), (t_all, GROUP_WIDTH), F32),
        (tok(QKV_SLOTS), (t_all, QKV_SLOTS), BF16),
        (tok(GROUP_WIDTH), (t_all, GROUP_WIDTH), F32),
        (tok(QKV_SLOTS), (t_all, QKV_SLOTS), BF16),
        (tok(GROUP_WIDTH), (t_all, GROUP_WIDTH), F32),
        (tok(LANES), (t_all, LANES), F32),
        (headed_t(QK_NOPE + QK_ROPE), (N_HEADS, QK_NOPE + QK_ROPE, t_all), BF16),
        (tok(KV_LORA), (t_all, KV_LORA), F32),
        (headed(QK_NOPE + QK_ROPE), (N_HEADS, t_all, QK_NOPE + QK_ROPE), BF16),
        (headed_t(VT_ROWS), (N_HEADS, VT_ROWS, t_all), BF16),
    ]
    return pl.pallas_call(
        functools.partial(_in_kernel, latent=st["latent"]),
        grid=(t_all // tm,),
        in_specs=[tok(D_MODEL), _mod_spec(st, tm),
                  full(npre), *[full(w) for w in w_all], tab(GROUP_WIDTH), tab(GROUP_WIDTH), tab(Q_MLA), tab(Q_MLA),
                  full(qg), full(wuq), full(wuqs), full(kvg), full(wukv)],
        out_specs=[s for s, _, _ in out_shapes],
        out_shape=[jax.ShapeDtypeStruct(shp, dt) for _, shp, dt in out_shapes],
        compiler_params=_cparams(("parallel",)),
        name="input_stage_" + st["name"],
    )(x, mod, npre, *w_all, cos, sin, cosq, sinq, qg, wuq, wuqs, kvg, wukv)


def _attn_t_kernel(*refs, n_soft, n_new, n_ctx, lam_init):
    refs = list(refs)
    qt_refs, k_refs, vt_ref = refs[:n_soft], refs[n_soft:2 * n_soft], refs[2 * n_soft]
    pos = 2 * n_soft + 1
    if n_ctx:
        ck_refs, cvt_ref = refs[pos:pos + n_soft], refs[pos + n_soft]
        pos += n_soft + 1
    if n_soft == 2:
        lam_ref, g_ref = refs[pos:pos + 2]
        pos += 2
    o_ref = refs[pos]
    tq = o_ref.shape[0]
    nchain = n_soft * N_HEADS
    order = [(j, hd) for hd in range(N_HEADS) for j in range(n_soft)]

    def chunk(state, k_of, vt_of):
        ms, accs = state
        new_m, new_acc = list(ms), list(accs)
        ss = [jnp.dot(k_of(j, hd), qt_refs[j][hd], preferred_element_type=F32) for j, hd in order]
        ps, alphas = [], []
        for (j, hd), s in zip(order, ss):
            c = j * N_HEADS + hd
            s3 = s.reshape(s.shape[0] // 8, 8, tq)
            top = jnp.max(jnp.max(s3, axis=0), axis=0, keepdims=True)
            m_new = jnp.maximum(ms[c], jnp.broadcast_to(top, (8, tq)))
            ps.append(jnp.exp(s3 - m_new[None]).reshape(s.shape).astype(BF16))
            alphas.append(jnp.exp(ms[c] - m_new))
            new_m[c] = m_new
        for (j, hd), p, alpha in zip(order, ps, alphas):
            c = j * N_HEADS + hd
            scaled = (accs[c].reshape(VT_ROWS // 8, 8, tq) * alpha[None]).reshape(VT_ROWS, tq)
            new_acc[c] = scaled + jnp.dot(vt_of(hd), p, preferred_element_type=F32)
        return tuple(new_m), tuple(new_acc)

    state = (tuple(jnp.full((8, tq), -jnp.inf, F32) for _ in range(nchain)),
             tuple(jnp.zeros((VT_ROWS, tq), F32) for _ in range(nchain)))
    if n_ctx:
        cstep = min(TK, n_ctx)
        for i in range(n_ctx // cstep):
            state = chunk(state, lambda j, hd, i=i: ck_refs[j][hd, i * cstep:(i + 1) * cstep, :],
                          lambda hd, i=i: cvt_ref[hd, :, i * cstep:(i + 1) * cstep])
    step = min(TK, n_new)

    def body(i, st):
        start = pl.multiple_of(i * step, step)
        return chunk(st, lambda j, hd: k_refs[j][hd, pl.ds(start, step), :],
                     lambda hd: vt_ref[hd, :, pl.ds(start, step)])

    _, accs = lax.fori_loop(0, n_new // step, body, state)

    def normalised(c):
        num, den = accs[c][:V_HEAD], accs[c][V_HEAD:V_HEAD + 8]
        return (num.reshape(V_HEAD // 8, 8, tq) / den[None]).reshape(V_HEAD, tq)

    if n_soft == 2:
        lv = lam_ref[...]
        lam = (jnp.exp(jnp.sum(lv[0:1] * lv[1:2], axis=-1, keepdims=True))
               - jnp.exp(jnp.sum(lv[2:3] * lv[3:4], axis=-1, keepdims=True)) + lam_init)
    outs = []
    for hd in range(N_HEADS):
        out = normalised(hd)
        if n_soft == 2:
            a = out - lam * normalised(N_HEADS + hd)
            out = a * lax.rsqrt(jnp.mean(a * a, axis=0, keepdims=True) + EPS) * g_ref[...] * (1.0 - lam_init)
        outs.append(out)
    o_ref[...] = jnp.concatenate(outs, axis=0).T


def _attention_t(name, qts, ks, vt, ctx_ks, ctx_vt, extras, nb, n, n_ctx, lam_init=0.0):
    tq = min(TQ, n)
    nqt = n // tq
    n_soft = len(qts)
    in_specs = [pl.BlockSpec((N_HEADS, a.shape[1], tq), lambda b, i: (0, 0, b * nqt + i)) for a in qts]
    in_specs += [pl.BlockSpec((N_HEADS, n, a.shape[-1]), lambda b, i: (0, b, 0)) for a in ks]
    in_specs += [pl.BlockSpec((N_HEADS, VT_ROWS, n), lambda b, i: (0, 0, b))]
    args = list(qts) + list(ks) + [vt]
    if n_ctx:
        in_specs += [pl.BlockSpec((N_HEADS, n_ctx, a.shape[-1]), lambda b, i: (0, b, 0)) for a in ctx_ks]
        in_specs += [pl.BlockSpec((N_HEADS, VT_ROWS, n_ctx), lambda b, i: (0, 0, b))]
        args += list(ctx_ks) + [ctx_vt]
    in_specs += [pl.BlockSpec(a.shape, lambda b, i: (0, 0)) for a in extras]
    args += list(extras)
    return pl.pallas_call(
        functools.partial(_attn_t_kernel, n_soft=n_soft, n_new=n, n_ctx=n_ctx, lam_init=lam_init),
        grid=(nb, nqt),
        in_specs=in_specs,
        out_specs=pl.BlockSpec((tq, N_HEADS * V_HEAD), lambda b, i: (b * nqt + i, 0)),
        out_shape=jax.ShapeDtypeStruct((nb * n, N_HEADS * V_HEAD), F32),
        compiler_params=_cparams(("parallel", "parallel")),
        name=name,
    )(*args)


def _tri(lower):
    r = lax.broadcasted_iota(jnp.int32, (CHUNK, CHUNK), 0)
    c = lax.broadcasted_iota(jnp.int32, (CHUNK, CHUNK), 1)
    return (c <= r) if lower else (c >= r)


def _nt(a, b):
    return lax.dot_general(a, b, (((1,), (1,)), ((), ())), preferred_element_type=F32)


def _head_slices(qkv, hd):
    return tuple(qkv[:, (j * N_HEADS + hd) * LANES:(j * N_HEADS + hd) * LANES + HEAD_DIM] for j in range(3))


def _row_dot(row, mat, transpose_mat):
    row8 = jnp.broadcast_to(row, (8, row.shape[1])).astype(BF16)
    dims = (((1,), (1,)), ((), ())) if transpose_mat else (((1,), (0,)), ((), ()))
    return lax.dot_general(row8, mat, dims, preferred_element_type=F32)[0:1]


def _load_state_t(dst, src):
    for d in range(2):
        for hd in range(N_HEADS):
            dst[d, hd] = src[d, hd].T


def _mlstm_step(dirs, bias_ref):
    k_scale = HEAD_DIM ** -0.5
    lower = _tri(True).astype(F32)
    upper = _tri(False).astype(F32)
    chains = []
    for d, qkv_ref, tail_ref, _, ct_s, n_s, m_s in dirs:
        g = tail_ref[...] + bias_ref[...]
        ls = _log_sigmoid(g)
        g_t, ls_t = g.T, ls.T
        left, right = (lower, upper) if d == 0 else (upper, lower)
        cum_col = jnp.dot(left, ls, precision=HIGHEST, preferred_element_type=F32)
        cum_row = jnp.dot(ls_t, right, precision=HIGHEST, preferred_element_type=F32)
        qkv = qkv_ref[...]
        for hd in range(N_HEADS):
            ci = TAIL_GATE0 + 4 * (2 * d) + hd
            cf = ci + 4
            q, k, v = _head_slices(qkv, hd)
            chains.append(dict(
                d=d, hd=hd, q=q, k=k, v=v, li_row=g_t[ci:ci + 1, :], b_row=cum_row[cf:cf + 1, :],
                c_col=g[:, ci:ci + 1] - cum_col[:, cf:cf + 1],
                m_prev=m_s[d:d + 1, hd:hd + 1], ct=ct_s[d, hd], n=n_s[d, hd:hd + 1, :],
                refs=(ct_s, n_s, m_s)))
    for c in chains:
        c["s"] = _nt(c["k"], c["q"])
        c["v_t"] = c["v"].T
    for c in chains:
        valid = _tri(c["d"] != 0)
        log_d = jnp.where(valid, c["b_row"] + c["c_col"], -jnp.inf)
        log_inter = c["b_row"] + c["m_prev"]
        c["m_t"] = jnp.maximum(log_inter, jnp.max(log_d, axis=0, keepdims=True))
        c["w"] = c["s"] * k_scale * jnp.exp(log_d - c["m_t"])
        c["w_inter"] = jnp.exp(log_inter - c["m_t"])
    for c in chains:
        num = (jnp.dot(c["v_t"], c["w"].astype(BF16), preferred_element_type=F32)
               + c["w_inter"] * _nt(c["ct"].astype(BF16), c["q"]))
        den = jnp.sum(c["w"], axis=0, keepdims=True) + c["w_inter"] * _row_dot(c["n"], c["q"], True)
        c["h"] = num / jnp.maximum(jnp.abs(den), jnp.exp(-c["m_t"]))
    for c in chains:
        last = CHUNK - 1 if c["d"] == 0 else 0
        c["m_new"] = c["m_t"][:, last:last + 1]
        b_last = c["b_row"][:, last:last + 1]
        w_end_row = jnp.exp(b_last - c["b_row"] + c["li_row"] - c["m_new"])
        w_end_col = jnp.exp(c["c_col"] + (b_last - c["m_new"]))
        decay = jnp.exp(b_last + c["m_prev"] - c["m_new"])
        kw = (c["k"].astype(F32) * (w_end_col * k_scale)).astype(BF16)
        c["ct_new"] = decay * c["ct"] + jnp.dot(c["v_t"], kw, preferred_element_type=F32)
        c["n_new"] = decay * c["n"] + _row_dot(w_end_row, c["k"], False) * k_scale
    for j, entry in enumerate(dirs):
        entry[3][...] = jnp.concatenate([c["h"] for c in chains[j * N_HEADS:(j + 1) * N_HEADS]], axis=0).T
    for c in chains:
        d, hd = c["d"], c["hd"]
        ct_s, n_s, m_s = c["refs"]
        ct_s[d, hd] = c["ct_new"]
        n_s[d, hd:hd + 1, :] = c["n_new"]
        m_s[d:d + 1, hd:hd + 1] = c["m_new"]


def _ret_step(dirs, lg_ref):
    k_scale = HEAD_DIM ** -0.5
    s = lax.broadcasted_iota(jnp.int32, (CHUNK, CHUNK), 0)
    t = lax.broadcasted_iota(jnp.int32, (CHUNK, CHUNK), 1)
    lane = lax.broadcasted_iota(jnp.int32, (1, CHUNK), 1)
    chains = []
    for d, qkv_ref, _, st_s in dirs:
        lag = (t - s) if d == 0 else (s - t)
        pos = (lane if d == 0 else CHUNK - 1 - lane).astype(F32)
        qkv = qkv_ref[...]
        for hd in range(N_HEADS):
            q, k, v = _head_slices(qkv, hd)
            chains.append(dict(d=d, hd=hd, q=q, k=k, v=v, lag=lag, pos=pos, st=st_s[d, hd], ref=st_s,
                               lg=_log_sigmoid(lg_ref[d:d + 1, hd:hd + 1])))
    for c in chains:
        c["s"] = _nt(c["k"], c["q"])
        c["v_t"] = c["v"].T
    for c in chains:
        intra = jnp.where(c["lag"] >= 0, jnp.exp(jnp.maximum(c["lag"], 0).astype(F32) * c["lg"]), 0.0)
        c["a"] = (c["s"] * k_scale * intra).astype(BF16)
    for c in chains:
        inter = jnp.exp((c["pos"] + 1.0) * c["lg"])
        c["o"] = jnp.dot(c["v_t"], c["a"], preferred_element_type=F32) + inter * _nt(c["st"].astype(BF16), c["q"])
    row = lax.broadcasted_iota(jnp.int32, (CHUNK, HEAD_DIM), 0)
    for c in chains:
        src_pos = (row if c["d"] == 0 else CHUNK - 1 - row).astype(F32)
        tail = jnp.exp((CHUNK - 1.0 - src_pos) * c["lg"]) * k_scale
        kw = (c["k"].astype(F32) * tail).astype(BF16)
        c["st_new"] = jnp.exp(CHUNK * c["lg"]) * c["st"] + jnp.dot(c["v_t"], kw, preferred_element_type=F32)
    for j, entry in enumerate(dirs):
        entry[2][...] = jnp.concatenate([c["o"] for c in chains[j * N_HEADS:(j + 1) * N_HEADS]], axis=0).T
    for c in chains:
        c["ref"][c["d"], c["hd"]] = c["st_new"]


def _scan_kernel(tab_ref, mqf_ref, tailf_ref, mqb_ref, tailb_ref, rqf_ref, rqb_ref, bias_ref, lg_ref,
                 c0_ref, n0_ref, m0_ref, s0_ref,
                 hf_ref, hb_ref, of_ref, ob_ref, cn_ref, nn_ref, mn_ref, sn_ref, ct_s, n_s, m_s, st_s):
    i = pl.program_id(0)
    seqs = range(ct_s.shape[0])

    @pl.when(tab_ref[3, i] == 1)
    def _():
        for p in seqs:
            _load_state_t(ct_s.at[p], c0_ref.at[p])
            _load_state_t(st_s.at[p], s0_ref.at[p])
        n_s[...] = n0_ref[...]
        m_s[...] = m0_ref[...]

    mdirs, rdirs = [], []
    for p in seqs:
        state = (ct_s.at[p], n_s.at[p], m_s.at[p])
        mdirs += [(0, mqf_ref.at[p], tailf_ref.at[p], hf_ref.at[p]) + state,
                  (1, mqb_ref.at[p], tailb_ref.at[p], hb_ref.at[p]) + state]
        rdirs += [(0, rqf_ref.at[p], of_ref.at[p], st_s.at[p]), (1, rqb_ref.at[p], ob_ref.at[p], st_s.at[p])]
    _mlstm_step(mdirs, bias_ref)
    _ret_step(rdirs, lg_ref)

    @pl.when(tab_ref[4, i] == 1)
    def _():
        for p in seqs:
            _load_state_t(cn_ref.at[p], ct_s.at[p])
            _load_state_t(sn_ref.at[p], st_s.at[p])
        nn_ref[...] = n_s[...]
        mn_ref[...] = m_s[...]


def _scan_table(st):
    nch = st["n"] // CHUNK
    rows = [(c, nch - 1 - c, g, int(c == 0), int(c == nch - 1))
            for g in range(st["nb"] // _scan_group(st)) for c in range(nch)]
    return jnp.asarray(np.array(rows, dtype=np.int32).T)


def _scan_group(st):
    return next(g for g in (4, 2, 1) if st["nb"] % g == 0)


def _scans(tab, mqkv, tail, bias_row, rqkv, decay_logit, c0, n0, m0, s0, st):
    nb, n, grp = st["nb"], st["n"], _scan_group(st)
    seq3 = lambda a: a.reshape(nb, n, a.shape[-1])
    fwd = lambda i, tab: (tab[2, i], tab[0, i], 0)
    bwd = lambda i, tab: (tab[2, i], tab[1, i], 0)
    const = lambda i, tab: (0, 0)
    rows = lambda width, where: pl.BlockSpec((grp, CHUNK, width), where)
    state = lambda a: pl.BlockSpec((grp,) + a.shape[1:], lambda i, tab: (tab[2, i],) + (0,) * (a.ndim - 1))
    states = (c0, n0, m0, s0)
    grid_spec = pltpu.PrefetchScalarGridSpec(
        num_scalar_prefetch=1,
        grid=(tab.shape[1],),
        in_specs=[rows(QKV_SLOTS, fwd), rows(LANES, fwd), rows(QKV_SLOTS, bwd), rows(LANES, bwd),
                  rows(QKV_SLOTS, fwd), rows(QKV_SLOTS, bwd),
                  pl.BlockSpec((1, LANES), const), pl.BlockSpec(decay_logit.shape, const)]
        + [state(a) for a in states],
        out_specs=[rows(GROUP_WIDTH, fwd), rows(GROUP_WIDTH, bwd), rows(GROUP_WIDTH, fwd), rows(GROUP_WIDTH, bwd)]
        + [state(a) for a in states],
        scratch_shapes=[pltpu.VMEM((grp,) + a.shape[1:], F32) for a in states],
    )
    outs = pl.pallas_call(
        _scan_kernel,
        grid_spec=grid_spec,
        out_shape=[jax.ShapeDtypeStruct((nb, n, GROUP_WIDTH), F32)] * 4
        + [jax.ShapeDtypeStruct(a.shape, F32) for a in states],
        compiler_params=_cparams(("arbitrary",)),
        name="recurrent_scans",
    )(tab, seq3(mqkv), seq3(tail), seq3(mqkv), seq3(tail), seq3(rqkv), seq3(rqkv), bias_row, decay_logit, *states)
    return [o.reshape(nb * n, GROUP_WIDTH) for o in outs[:4]] + list(outs[4:])


def _proj_kernel(x_ref, w_ref, o_ref):
    o_ref[...] = _bdot(x_ref[...], w_ref[...]).astype(o_ref.dtype)


def _project(x, w, dtype):
    return pl.pallas_call(
        _proj_kernel,
        out_shape=jax.ShapeDtypeStruct((x.shape[0], w.shape[1]), dtype),
        name="ctx_kv_proj",
    )(x, w)


def _first_argmax_mask(cur, axis, size):
    io = lax.broadcasted_iota(jnp.int32, cur.shape, axis)
    mx = jnp.max(cur, axis=axis, keepdims=True)
    ix = jnp.min(jnp.where(cur == mx, io, size), axis=axis, keepdims=True)
    return io == ix


def _route(scores_t, bias_col):
    tm = scores_t.shape[1]
    per = N_EXPERTS // N_GROUPS
    sel = scores_t + bias_col
    s3 = sel.reshape(N_GROUPS, per, tm)
    hit1 = _first_argmax_mask(s3, 1, per)
    m1 = jnp.max(s3, axis=1, keepdims=True)
    m2 = jnp.max(jnp.where(hit1, -jnp.inf, s3), axis=1, keepdims=True)
    cur = m1 + m2
    gsel = None
    for _ in range(TOPK_GROUPS):
        hit = _first_argmax_mask(cur, 0, N_GROUPS)
        gsel = hit if gsel is None else jnp.logical_or(gsel, hit)
        cur = jnp.where(hit, -jnp.inf, cur)
    cur = jnp.where(gsel, s3, -jnp.inf).reshape(N_EXPERTS, tm)
    chosen, hits = None, []
    for _ in range(TOP_K):
        hit = _first_argmax_mask(cur, 0, N_EXPERTS)
        hits.append(hit)
        chosen = hit if chosen is None else jnp.logical_or(chosen, hit)
        cur = jnp.where(hit, -jnp.inf, cur)
    w = jnp.where(chosen, scores_t, 0.0)
    return w / jnp.sum(w, axis=0, keepdims=True) * ROUTED_SCALE, chosen, hits


def _out_kernel(oa_ref, od_ref, hf_ref, hb_ref, mo_ref, of_ref, ob_ref, rg_ref,
                x_ref, mod_ref, mg_ref, npost_ref, npre_ref, wout_ref, wrt_ref, rb_ref,
                x1_ref, h2_ref, hpk_ref, eidx_ref, rank_ref, wk_ref, cnt_ref, count_s):
    @pl.when(pl.program_id(0) == 0)
    def _():
        count_s[...] = jnp.zeros(count_s.shape, F32)

    a = oa_ref[...]
    dd = od_ref[...]
    b = _rms_heads(hf_ref[...] + hb_ref[...]) * mg_ref[...] * _sigmoid(mo_ref[...])
    r = _rms_heads(of_ref[...] + ob_ref[...]) * _silu(rg_ref[...])
    mix = sum(jnp.dot(part.astype(BF16), wout_ref[j * GROUP_WIDTH:(j + 1) * GROUP_WIDTH, :],
                      preferred_element_type=F32) for j, part in enumerate((a, b, r, dd)))
    m = mod_ref[0]
    x1 = x_ref[...] + m[2:3] * (_rms(mix) * npost_ref[...])
    x1_ref[...] = x1
    h2 = (_rms(x1) * npre_ref[...] * (1.0 + m[4:5]) + m[3:4]).astype(BF16)
    h2_ref[...] = h2
    hpk_ref[...] = _pack_rows(h2)

    logits_t = lax.dot_general(wrt_ref[...], h2, (((1,), (1,)), ((), ())), preferred_element_type=F32)
    w_t, chosen, hits = _route(_sigmoid(logits_t), rb_ref[...])
    tm = w_t.shape[1]
    src = lax.broadcasted_iota(jnp.int32, (tm, tm), 0)
    dst = lax.broadcasted_iota(jnp.int32, (tm, tm), 1)
    before = jnp.where(src < dst, 1.0, 0.0).astype(BF16)
    picked = jnp.where(chosen, 1.0, 0.0)
    rank = jnp.dot(picked.astype(BF16), before, preferred_element_type=F32) + count_s[...]
    count_s[...] += jnp.sum(picked, axis=1, keepdims=True)
    cnt_ref[...] = jnp.broadcast_to(count_s[...], cnt_ref.shape)
    e_io = lax.broadcasted_iota(jnp.int32, w_t.shape, 0)
    pick = lambda hit, val: jnp.sum(jnp.where(hit, val, jnp.zeros_like(val)), axis=0, keepdims=True)
    eidx_ref[...] = jnp.concatenate([pick(h, e_io) for h in hits], axis=0)
    rank_ref[...] = jnp.concatenate([pick(h, rank) for h in hits], axis=0).astype(jnp.int32)
    wk = jnp.concatenate([pick(h, w_t) for h in hits] + [jnp.zeros((LANES - TOP_K, tm), F32)], axis=0)
    wk_ref[...] = wk.T


def _output_stage(oa, od, hf, hb, mo, of, ob, rg, x, mod, mg, npost, npre, wout, wrt, rb, st):
    t_all = x.shape[0]
    row = lambda i: (i, 0)
    const2 = lambda i: (0, 0)
    tok = lambda w: pl.BlockSpec((TM, w), row)
    full = lambda a: pl.BlockSpec(a.shape, const2)
    return pl.pallas_call(
        _out_kernel,
        grid=(t_all // TM,),
        in_specs=[tok(GROUP_WIDTH)] * 8 + [tok(D_MODEL), _mod_spec(st),
                                   full(mg), full(npost), full(npre), full(wout), full(wrt), full(rb)],
        out_specs=[tok(D_MODEL), tok(D_MODEL), tok(D_MODEL // 2),
                   pl.BlockSpec((TOP_K, TM), lambda i: (0, i)), pl.BlockSpec((TOP_K, TM), lambda i: (0, i)),
                   tok(LANES), pl.BlockSpec((N_EXPERTS, LANES), const2)],
        out_shape=[jax.ShapeDtypeStruct((t_all, D_MODEL), F32),
                   jax.ShapeDtypeStruct((t_all, D_MODEL), BF16),
                   jax.ShapeDtypeStruct((t_all, D_MODEL // 2), jnp.uint32),
                   jax.ShapeDtypeStruct((TOP_K, t_all), jnp.int32),
                   jax.ShapeDtypeStruct((TOP_K, t_all), jnp.int32),
                   jax.ShapeDtypeStruct((t_all, LANES), F32),
                   jax.ShapeDtypeStruct((N_EXPERTS, LANES), F32)],
        scratch_shapes=[pltpu.VMEM((N_EXPERTS, 1), F32)],
        compiler_params=_cparams(("arbitrary",)),
        name="output_stage_" + st["name"],
    )(oa, od, hf, hb, mo, of, ob, rg, x, mod, mg, npost, npre, wout, wrt, rb)


SC_CORES, SC_SUBCORES = 2, 16
SC_WORKERS = SC_CORES * SC_SUBCORES
SLOT_BLOCK = 512
DISPATCH_ROWS = 64
COMBINE_ROWS = 64


def _sc_mesh():
    return plsc.VectorSubcoreMesh(core_axis_name="core", subcore_axis_name="subcore")


def _sc_worker():
    return lax.axis_index("subcore") * SC_CORES + lax.axis_index("core")


def _sc_dispatch(hpk, dest, n_slots):
    t_all, width = hpk.shape
    per_worker = t_all // SC_WORKERS
    assert t_all % (SC_WORKERS * DISPATCH_ROWS) == 0

    @functools.partial(
        pl.kernel, mesh=_sc_mesh(), out_type=jax.ShapeDtypeStruct((n_slots, width), hpk.dtype),
        scratch_types=[pltpu.VMEM((DISPATCH_ROWS, width), hpk.dtype)]
        + [pltpu.VMEM((DISPATCH_ROWS,), jnp.int32)] * TOP_K + [pltpu.SemaphoreType.DMA])
    def dispatch(x_hbm, d_hbm, o_hbm, rows_v, *rest):
        idx, sem = rest[:TOP_K], rest[TOP_K]

        @pl.loop(0, per_worker // DISPATCH_ROWS)
        def _(j):
            base = _sc_worker() * per_worker + j * DISPATCH_ROWS
            pltpu.sync_copy(x_hbm.at[pl.ds(base, DISPATCH_ROWS)], rows_v)
            for k in range(TOP_K):
                pltpu.sync_copy(d_hbm.at[pl.ds(k * t_all + base, DISPATCH_ROWS)], idx[k])
            copies = [pltpu.async_copy(rows_v, o_hbm.at[idx[k]], sem) for k in range(TOP_K)]
            for c in copies:
                c.wait()

    return dispatch(hpk, dest)


def _sc_combine(yb, dest, t_all):
    width = yb.shape[1]
    per_worker = t_all // SC_WORKERS
    assert t_all % (SC_WORKERS * COMBINE_ROWS) == 0

    @functools.partial(
        pl.kernel, mesh=_sc_mesh(), out_type=jax.ShapeDtypeStruct((TOP_K, t_all, width), yb.dtype),
        scratch_types=[pltpu.VMEM((COMBINE_ROWS, width), yb.dtype)] * 2
        + [pltpu.VMEM((COMBINE_ROWS,), jnp.int32)] * TOP_K + [pltpu.SemaphoreType.DMA] * 2)
    def combine(y_hbm, d_hbm, o_hbm, rows_a, rows_b, *rest):
        idx, sems = rest[:TOP_K], rest[TOP_K:]
        bufs = (rows_a, rows_b)

        @pl.loop(0, per_worker // COMBINE_ROWS)
        def _(j):
            base = _sc_worker() * per_worker + j * COMBINE_ROWS
            for k in range(TOP_K):
                pltpu.sync_copy(d_hbm.at[pl.ds(k * t_all + base, COMBINE_ROWS)], idx[k])
            pending = pltpu.async_copy(y_hbm.at[idx[0]], bufs[0], sems[0])
            for k in range(TOP_K):
                pending.wait()
                if k + 1 < TOP_K:
                    pending = pltpu.async_copy(y_hbm.at[idx[k + 1]], bufs[(k + 1) % 2], sems[(k + 1) % 2])
                pltpu.sync_copy(bufs[k % 2], o_hbm.at[k, pl.ds(base, COMBINE_ROWS)])

    return combine(yb, dest)


def _pack_rows(x):
    bits = pltpu.bitcast(x.astype(BF16).astype(F32), jnp.uint32)
    w = x.shape[-1] // 2
    return (bits[..., :w] >> 16) | (bits[..., w:] & jnp.uint32(0xFFFF0000))


def _unpack_rows(words):
    return pltpu.bitcast(words << 16, F32), pltpu.bitcast(words & jnp.uint32(0xFFFF0000), F32)


def _expert_kernel(ord_ref, used_ref, eo_ref, nd_ref, xs_ref, wg_hbm, wu_hbm, wd_hbm, y_ref,
                   stage_g, stage_u, stage_d, wg_s, wu_s, wd_s, sems, *, layer):
    b = pl.program_id(0)
    nd = nd_ref[0]

    def copies(j, slot):
        e = eo_ref[j]
        return [pltpu.make_async_copy(src.at[layer, e], dst.at[slot], sems.at[slot, i])
                for i, (src, dst) in enumerate(((wg_hbm, stage_g), (wu_hbm, stage_u), (wd_hbm, stage_d)))]

    @pl.when(b == 0)
    def _():
        for c in copies(0, 0):
            c.start()

    @pl.when(jnp.logical_and(b == 0, nd > 1))
    def _():
        for c in copies(1, 1):
            c.start()

    j = ord_ref[b]
    fresh = jnp.logical_or(b == 0, j != ord_ref[jnp.maximum(b - 1, 0)])
    for slot in range(2):
        @pl.when(jnp.logical_and(jnp.logical_and(fresh, b < used_ref[0]), j % 2 == slot))
        def _(slot=slot):
            for c in copies(j, slot):
                c.wait()
            wg_s[...] = stage_g[slot].astype(BF16)
            wu_s[...] = stage_u[slot].astype(BF16)
            wd_s[...] = stage_d[slot].astype(BF16)

            @pl.when(j + 2 < nd)
            def _():
                for c in copies(j + 2, slot):
                    c.start()

    @pl.when(b < used_ref[0])
    def _():
        lo, hi = (v.astype(BF16) for v in _unpack_rows(xs_ref[...]))
        half = D_MODEL // 2
        gate = (jnp.dot(lo, wg_s[:half, :], preferred_element_type=F32)
                + jnp.dot(hi, wg_s[half:, :], preferred_element_type=F32))
        up = (jnp.dot(lo, wu_s[:half, :], preferred_element_type=F32)
              + jnp.dot(hi, wu_s[half:, :], preferred_element_type=F32))
        y = jnp.dot((_silu(gate) * up).astype(BF16), wd_s[...], preferred_element_type=F32)
        y_ref[...] = _pack_rows(y)


def _expert_blocks(layer, xs, block_ordinal, blocks_used, expert_order, n_present, wg, wu, wdn):
    n_slots = xs.shape[0]
    rows = lambda b, ordinal, used, eo, nd: (jnp.minimum(b, used[0] - 1), 0)
    grid_spec = pltpu.PrefetchScalarGridSpec(
        num_scalar_prefetch=4,
        grid=(n_slots // SLOT_BLOCK,),
        in_specs=[pl.BlockSpec((SLOT_BLOCK, D_MODEL // 2), rows)] + [pl.BlockSpec(memory_space=pl.ANY)] * 3,
        out_specs=pl.BlockSpec((SLOT_BLOCK, D_MODEL // 2), rows),
        scratch_shapes=[pltpu.VMEM((2, D_MODEL, D_EXPERT), F32), pltpu.VMEM((2, D_MODEL, D_EXPERT), F32),
                        pltpu.VMEM((2, D_EXPERT, D_MODEL), F32),
                        pltpu.VMEM((D_MODEL, D_EXPERT), BF16), pltpu.VMEM((D_MODEL, D_EXPERT), BF16),
                        pltpu.VMEM((D_EXPERT, D_MODEL), BF16), pltpu.SemaphoreType.DMA((2, 3))])
    return pl.pallas_call(
        functools.partial(_expert_kernel, layer=layer), grid_spec=grid_spec,
        out_shape=jax.ShapeDtypeStruct((n_slots, D_MODEL // 2), jnp.uint32),
        compiler_params=_cparams(("arbitrary",)),
        name="moe_experts",
    )(block_ordinal, blocks_used, expert_order, n_present, xs, wg, wu, wdn)


def _moe_out_kernel(g_ref, wk_ref, h_ref, x1_ref, mod_ref, npost_ref, sg_ref, su_ref, sd_ref, o_ref):
    h = h_ref[...]
    act = _silu(jnp.dot(h, sg_ref[...], preferred_element_type=F32)) * jnp.dot(h, su_ref[...],
                                                                              preferred_element_type=F32)
    moe = jnp.dot(act.astype(BF16), sd_ref[...], preferred_element_type=F32)
    wk = wk_ref[...]
    lo, hi = 0.0, 0.0
    for k in range(TOP_K):
        g_lo, g_hi = _unpack_rows(g_ref[k])
        lo = lo + g_lo * wk[:, k:k + 1]
        hi = hi + g_hi * wk[:, k:k + 1]
    moe = moe + jnp.concatenate([lo, hi], axis=-1)
    m = mod_ref[0]
    o_ref[...] = x1_ref[...] + m[5:6] * (_rms(moe) * npost_ref[...])


def _moe_out_stage(g, wk, h2, x1, mod, npost, sg, su, sd, st):
    t_all = h2.shape[0]
    row = lambda i: (i, 0)
    full = lambda a: pl.BlockSpec(a.shape, lambda i: (0, 0))
    return pl.pallas_call(
        _moe_out_kernel,
        grid=(t_all // TM,),
        in_specs=[pl.BlockSpec((TOP_K, TM, D_MODEL // 2), lambda i: (0, i, 0)), pl.BlockSpec((TM, LANES), row),
                  pl.BlockSpec((TM, D_MODEL), row), pl.BlockSpec((TM, D_MODEL), row), _mod_spec(st),
                  full(npost), full(sg), full(su), full(sd)],
        out_specs=pl.BlockSpec((TM, D_MODEL), row),
        out_shape=jax.ShapeDtypeStruct((t_all, D_MODEL), F32),
        compiler_params=_cparams(("parallel",)),
        name="moe_combine_" + st["name"],
    )(g, wk, h2, x1, mod, npost, sg, su, sd)


def _moe_stage(layer, h2, hpk, eidx, rank, wk, counts, x1, mod, npost, wg, wu, wdn, sg, su, sd, st):
    t_all = h2.shape[0]
    n_blocks = -(-(t_all * TOP_K + N_EXPERTS * (SLOT_BLOCK - 1)) // SLOT_BLOCK)
    cnt = counts[:, 0].astype(jnp.int32)
    padded = (cnt + SLOT_BLOCK - 1) // SLOT_BLOCK * SLOT_BLOCK
    pad_end = jnp.cumsum(padded)
    pad_start = pad_end - padded
    experts = jnp.arange(N_EXPERTS, dtype=jnp.int32)[:, None, None]
    dest = rank + jnp.sum(jnp.where(eidx[None] == experts, pad_start[:, None, None], 0), axis=0)
    dest = dest.reshape(TOP_K * t_all)
    block_start = jnp.arange(n_blocks, dtype=jnp.int32) * SLOT_BLOCK
    block_expert = jnp.minimum(jnp.sum((pad_end[None, :] <= block_start[:, None]).astype(jnp.int32), axis=1),
                               N_EXPERTS - 1)
    blocks_used = (pad_end[-1:] // SLOT_BLOCK).astype(jnp.int32)
    present = padded > 0
    expert_order = jnp.argsort(jnp.logical_not(present), stable=True).astype(jnp.int32)
    n_present = jnp.sum(present.astype(jnp.int32))[None]
    ordinal_of = jnp.cumsum(present.astype(jnp.int32)) - 1
    block_ordinal = jnp.sum(jnp.where(block_expert[:, None] == experts[:, 0, 0][None, :], ordinal_of[None, :], 0),
                            axis=1).astype(jnp.int32)

    xs = _sc_dispatch(hpk, dest, n_blocks * SLOT_BLOCK)
    yb = _expert_blocks(layer, xs, block_ordinal, blocks_used, expert_order, n_present, wg, wu, wdn)
    g = _sc_combine(yb, dest, t_all)
    return _moe_out_stage(g, wk, h2, x1, mod, npost, sg, su, sd, st)


_PERM32 = np.concatenate([np.arange(8, 16), np.arange(0, 8), np.arange(24, 32), np.arange(16, 24)])


def _rope_tables(n):
    f32 = np.float32
    pos = np.arange(n)
    quarter = QK_ROPE // 4
    inv = (f32(1.0) / (f32(ROPE_THETA) ** (np.arange(quarter, dtype=f32) / f32(quarter)))).astype(f32)
    ang_r = (pos // GRID_W).astype(f32)[:, None] * inv[None, :]
    ang_c = (pos % GRID_W).astype(f32)[:, None] * inv[None, :]
    cos32 = np.concatenate([np.cos(ang_r)] * 2 + [np.cos(ang_c)] * 2, axis=-1).astype(f32)
    sin32 = np.concatenate([-np.sin(ang_r), np.sin(ang_r), -np.sin(ang_c), np.sin(ang_c)], axis=-1).astype(f32)
    cos256, sin256 = np.tile(cos32, (1, 8)), np.tile(sin32, (1, 8))
    one, zero = np.ones((n, QK_NOPE), f32), np.zeros((n, QK_NOPE), f32)
    cosq = np.concatenate([one, cos32] * N_HEADS, axis=-1)
    sinq = np.concatenate([zero, sin32] * N_HEADS, axis=-1)
    return tuple(jnp.asarray(t) for t in (cos256, sin256, cosq, sinq))


def _layer_weights(w_in_l, w_uq_l):
    gw = GROUP_WIDTH
    g0 = W_BEFORE_GATES
    b0 = g0 + 4 * N_HEADS
    kr0 = b0 + W_AFTER_GATES
    assert kr0 + QK_ROPE == w_in_l.shape[1]
    w_a, m_g, w_b, d_kr = w_in_l[:, :g0], w_in_l[:, g0:b0], w_in_l[:, b0:kr0], w_in_l[:, kr0:]
    perm256 = np.concatenate([_PERM32 + 32 * j for j in range(8)])
    pad = lambda w, n: jnp.pad(w, ((0, 0), (0, n - w.shape[1])))
    w_c = jnp.concatenate([pad(jnp.concatenate([d_kr, m_g], axis=-1), LANES),
                           w_a[:, perm256], w_a[:, gw + perm256], pad(d_kr[:, _PERM32], LANES)], axis=-1)
    per = QK_NOPE + QK_ROPE
    permq = np.concatenate([np.concatenate([np.arange(QK_NOPE), QK_NOPE + _PERM32]) + per * j
                            for j in range(N_HEADS)])
    return ((w_a.astype(BF16), w_b.astype(BF16), w_c.astype(BF16)), w_uq_l.astype(BF16),
            w_uq_l[:, permq].astype(BF16))


def kernel(x_prompt, x_sample, cache_diff_k, cache_diff_v, state_mlstm_C, state_mlstm_n, state_mlstm_m, state_ret_S, cache_mla_ckv, cache_mla_krope, c, c_ctx, w_mod, b_mod, norm_pre, norm_post, w_in, w_out, diff_lambda, diff_norm, mlstm_gate_bias, mlstm_norm, ret_decay_logit, mla_q_norm, mla_w_uq, mla_kv_norm, mla_w_ukv, moe_w_router, moe_router_bias, moe_w_gate, moe_w_up, moe_w_down, shared_w_gate, shared_w_up, shared_w_down):
    bp, n_p, _ = x_prompt.shape
    bs, n_s, _ = x_sample.shape
    depth = w_in.shape[0]
    past = cache_diff_k.shape[2]
    assert (bp * n_p) % TM == 0 and n_s % TM == 0 and n_p % CHUNK == 0 and n_s % min(TK, n_s) == 0
    assert past % min(TK, past) == 0 and bs + 1 <= 8 and n_s % GRID_W == 0

    streams = (dict(name="ctx", nb=bp, n=n_p, latent=False), dict(name="latent", nb=bs, n=n_s, latent=True))
    xs_by_stream = [x_prompt.reshape(bp * n_p, D_MODEL), x_sample.reshape(bs * n_s, D_MODEL)]
    cond = jnp.zeros((8, D_MODEL), F32).at[0].set(c_ctx).at[1:1 + bs].set(c)
    mod_all = _modulation(cond, w_mod, b_mod).reshape(depth, 8, 6, D_MODEL)
    tabs = _rope_tables(n_s)
    scan_tabs = [_scan_table(st) for st in streams]

    outs = [[] for _ in range(8)]
    for l in range(depth):
        lam_init = 0.8 - 0.6 * math.exp(-0.3 * l)
        mod = mod_all[l]
        w_all, wuq, wuqs = _layer_weights(w_in[l], mla_w_uq[l])
        wukv = mla_w_ukv[l].astype(BF16)
        wout, wrt, rbias = w_out[l].astype(BF16), moe_w_router[l].T.astype(BF16), moe_router_bias[l][:, None]
        shared = [w[l].astype(BF16) for w in (shared_w_gate, shared_w_up, shared_w_down)]
        lamv, dg = diff_lambda[l], diff_norm[l][:, None]
        bias_row = jnp.zeros((1, LANES), F32).at[0, TAIL_GATE0:TAIL_GATE0 + 16].set(mlstm_gate_bias[l].reshape(16))

        with_ones = lambda vt: jnp.concatenate([vt, jnp.ones((N_HEADS, VT_ONES, bs * past), BF16)], axis=1)
        ck = jnp.transpose(cache_diff_k[:, l], (2, 0, 1, 3)).reshape(N_HEADS, bs * past, HEAD_DIM).astype(BF16)
        cvt = with_ones(jnp.transpose(cache_diff_v[:, l], (2, 3, 0, 1)).reshape(N_HEADS, HEAD_DIM, bs * past)
                        .astype(BF16))
        kvc = _project(cache_mla_ckv[:, l].reshape(bs * past, KV_LORA), wukv, BF16)
        kvc = kvc.reshape(bs * past, N_HEADS, QK_NOPE + V_HEAD)
        krc = jnp.broadcast_to(cache_mla_krope[:, l].reshape(bs * past, 1, QK_ROPE).astype(BF16),
                               (bs * past, N_HEADS, QK_ROPE))
        kc = jnp.transpose(jnp.concatenate([kvc[..., :QK_NOPE], krc], axis=-1), (1, 0, 2))
        vct = with_ones(jnp.transpose(kvc[..., QK_NOPE:], (1, 2, 0)))

        for si, st in enumerate(streams):
            x, nb, n = xs_by_stream[si], st["nb"], st["n"]
            (aq1t, aq2t, ak1, ak2, avt, ak, av, mqkv, mo, rqkv, rg, tail, qmlat, ckv, kmla, vmlat) = _input_stage(
                x, mod, norm_pre[l, 0:1], w_all, tabs, mla_q_norm[l][None], wuq, wuqs,
                mla_kv_norm[l][None], wukv, st)
            if st["latent"]:
                oa = _attention_t("diff_attn_latent", [aq1t, aq2t], [ak1, ak2], avt,
                                  [ck[..., :DIFF_HALF], ck[..., DIFF_HALF:]], cvt, [lamv, dg], nb, n, past, lam_init)
                od = _attention_t("mla_attn_latent", [qmlat], [kmla], vmlat, [kc], vct, [], nb, n, past)
                states = [s[:, l] for s in (state_mlstm_C, state_mlstm_n, state_mlstm_m, state_ret_S)]
            else:
                oa = _attention_t("diff_attn_ctx", [aq1t, aq2t], [ak1, ak2], avt, None, None, [lamv, dg],
                                  nb, n, 0, lam_init)
                od = _attention_t("mla_attn_ctx", [qmlat], [kmla], vmlat, None, None, [], nb, n, 0)
                states = [jnp.zeros((nb,) + s.shape[2:], F32)
                          for s in (state_mlstm_C, state_mlstm_n, state_mlstm_m, state_ret_S)]
            hf, hb, of, ob, c_n, n_n, m_n, s_n = _scans(scan_tabs[si], mqkv, tail, bias_row, rqkv,
                                                        ret_decay_logit[l], *states, st)

            x1, h2, hpk, eidx, rank, wk, counts = _output_stage(
                oa, od, hf, hb, mo, of, ob, rg, x, mod, mlstm_norm[l][None], norm_post[l, 0:1],
                norm_pre[l, 1:2], wout, wrt, rbias, st)
            xs_by_stream[si] = _moe_stage(l, h2, hpk, eidx, rank, wk, counts, x1, mod, norm_post[l, 1:2],
                                          moe_w_gate, moe_w_up, moe_w_down, *shared, st)
            if not st["latent"]:
                new = (ak.reshape(nb, n, N_HEADS, HEAD_DIM), av.reshape(nb, n, N_HEADS, HEAD_DIM), c_n, n_n, m_n,
                       s_n, ckv.reshape(nb, n, KV_LORA), tail[:, :QK_ROPE].reshape(nb, n, QK_ROPE))
                for o, v in zip(outs, new):
                    o.append(v)

    return (xs_by_stream[0].reshape(bp, n_p, D_MODEL), xs_by_stream[1].reshape(bs, n_s, D_MODEL)) + tuple(
        jnp.stack(o, axis=1) for o in outs)
```

```python
import functools
import math

import numpy as np
import jax
import jax.numpy as jnp
from jax import lax
from jax.experimental import pallas as pl
from jax.experimental.pallas import tpu as pltpu
from jax.experimental.pallas import tpu_sc as plsc

F32 = jnp.float32
BF16 = jnp.bfloat16
HIGHEST = lax.Precision.HIGHEST

D_MODEL = 1024
GRID_W = 64
GROUP_WIDTH = 256
HEAD_DIM = 64
N_HEADS = 4
DIFF_HALF = 32
ROPE_THETA = 10000.0
Q_LORA = 256
KV_LORA = 128
QK_NOPE = 64
QK_ROPE = 32
V_HEAD = 64
N_EXPERTS = 64
TOP_K = 8
N_GROUPS = 8
TOPK_GROUPS = 4
D_EXPERT = 256
ROUTED_SCALE = 2.5
CHUNK = 128
EPS = 1e-6
Q_MLA = N_HEADS * (QK_NOPE + QK_ROPE)

LANES = 128
VMEM_LIMIT = 56 * 1024 * 1024

TM = 512
TQ = 256
TK = 4096

C_AQ, C_AK, C_AV, C_MQKV, C_MO = (0, 0), (0, 256), (0, 512), (0, 768), (0, 1536)
C_RQKV, C_RG, C_CQ, C_CKV = (1, 0), (1, 768), (1, 1024), (1, 1280)
C_TAIL, C_AQS, C_AKS, C_TAILS = (2, 0), (2, 128), (2, 384), (2, 640)
W_BEFORE_GATES, W_AFTER_GATES = 1792, 1408
TAIL_GATE0 = QK_ROPE
QKV_SLOTS = 3 * N_HEADS * LANES
VT_ONES = 16
VT_ROWS = V_HEAD + VT_ONES


def _cparams(sem):
    return pltpu.CompilerParams(dimension_semantics=sem, vmem_limit_bytes=VMEM_LIMIT)


def _rms(x):
    return x * lax.rsqrt(jnp.mean(x * x, axis=-1, keepdims=True) + EPS)


def _head_mean_matrix(width):
    r = lax.broadcasted_iota(jnp.int32, (width, width), 0) // HEAD_DIM
    c = lax.broadcasted_iota(jnp.int32, (width, width), 1) // HEAD_DIM
    return jnp.where(r == c, 1.0 / HEAD_DIM, 0.0).astype(BF16)


def _rms_heads(x):
    sq = x * x
    hi = sq.astype(BF16)
    lo = (sq - hi.astype(F32)).astype(BF16)
    g = _head_mean_matrix(x.shape[-1])
    ms = jnp.dot(hi, g, preferred_element_type=F32) + jnp.dot(lo, g, preferred_element_type=F32)
    return x * lax.rsqrt(ms + EPS)


def _sigmoid(x):
    return 1.0 / (1.0 + jnp.exp(-x))


def _silu(x):
    return x * _sigmoid(x)


def _log_sigmoid(x):
    return jnp.minimum(x, 0.0) - jnp.log1p(jnp.exp(-jnp.abs(x)))


def _bdot(a, b):
    return jnp.dot(a.astype(BF16), b.astype(BF16), preferred_element_type=F32)


def _mod_kernel(c_ref, w_ref, b_ref, o_ref):
    o_ref[0] = _bdot(_silu(c_ref[...]), w_ref[0]) + b_ref[0]


def _modulation(cond, w_mod, b_mod):
    depth, _, n = w_mod.shape
    tn = 1536
    return pl.pallas_call(
        _mod_kernel,
        grid=(depth, n // tn),
        in_specs=[pl.BlockSpec((8, D_MODEL), lambda l, j: (0, 0)),
                  pl.BlockSpec((1, D_MODEL, tn), lambda l, j: (l, 0, j)),
                  pl.BlockSpec((1, 1, tn), lambda l, j: (l, 0, j))],
        out_specs=pl.BlockSpec((1, 8, tn), lambda l, j: (l, 0, j)),
        out_shape=jax.ShapeDtypeStruct((depth, 8, n), F32),
        compiler_params=_cparams(("parallel", "parallel")),
        name="adaln_mod",
    )(cond, w_mod, b_mod.reshape(depth, 1, n))


def _in_kernel(x_ref, mod_ref, npre_ref, wa_ref, wb_ref, wc_ref, cos_ref, sin_ref, cosq_ref, sinq_ref,
               qg_ref, wuq_ref, wuqs_ref, kvg_ref, wukv_ref,
               aq1_ref, aq2_ref, ak1t_ref, ak2t_ref, avh_ref, ak_ref, av_ref,
               mqkv_ref, mo_ref, rqkv_ref, rg_ref, tail_ref,
               qmla_ref, ckv_ref, kmlat_ref, vmla_ref, *, latent):
    m = mod_ref[0]
    h = (_rms(x_ref[...]) * npre_ref[...] * (1.0 + m[1:2]) + m[0:1]).astype(BF16)

    def proj(col, width):
        w_ref = (wa_ref, wb_ref, wc_ref)[col[0]]
        return jnp.dot(h, w_ref[:, col[1]:col[1] + width], preferred_element_type=F32)

    def rotated(c0, c0_swapped, width, cos, sin):
        return proj(c0, width) * cos + proj(c0_swapped, width) * sin if latent else proj(c0, width)

    cos = cos_ref[...]
    sin = sin_ref[...]
    aq = rotated(C_AQ, C_AQS, GROUP_WIDTH, cos, sin) * (DIFF_HALF ** -0.5)
    ak = rotated(C_AK, C_AKS, GROUP_WIDTH, cos, sin)
    av = proj(C_AV, GROUP_WIDTH)
    ak_ref[...] = ak
    av_ref[...] = av
    aq_t, av_t = aq.T.astype(BF16), av.T.astype(BF16)
    ones = jnp.ones((VT_ONES, aq_t.shape[1]), BF16)
    for hd in range(N_HEADS):
        lo = hd * HEAD_DIM
        aq1_ref[hd] = aq_t[lo:lo + DIFF_HALF, :]
        aq2_ref[hd] = aq_t[lo + DIFF_HALF:lo + HEAD_DIM, :]
        ak1t_ref[hd] = ak[:, lo:lo + DIFF_HALF].astype(BF16)
        ak2t_ref[hd] = ak[:, lo + DIFF_HALF:lo + HEAD_DIM].astype(BF16)
        avh_ref[hd, 0:HEAD_DIM, :] = av_t[lo:lo + HEAD_DIM, :]
        avh_ref[hd, HEAD_DIM:VT_ROWS, :] = ones

    def head_slots(x):
        x = x.astype(BF16)
        zero = jnp.zeros((x.shape[0], LANES - HEAD_DIM), BF16)
        parts = []
        for j in range(3 * N_HEADS):
            parts += [x[:, j * HEAD_DIM:(j + 1) * HEAD_DIM], zero]
        return jnp.concatenate(parts, axis=-1)

    mqkv_ref[...] = head_slots(proj(C_MQKV, 3 * GROUP_WIDTH))
    mo_ref[...] = proj(C_MO, GROUP_WIDTH)
    rqkv_ref[...] = head_slots(proj(C_RQKV, 3 * GROUP_WIDTH))
    rg_ref[...] = proj(C_RG, GROUP_WIDTH)

    tail = proj(C_TAIL, LANES)
    tail_ref[...] = tail
    kr = tail * cos[:, :LANES] + proj(C_TAILS, LANES) * sin[:, :LANES] if latent else tail
    kr = kr[:, :QK_ROPE].astype(BF16)

    cqn = (_rms(proj(C_CQ, Q_LORA)) * qg_ref[...]).astype(BF16)
    qmla = jnp.dot(cqn, wuq_ref[...], preferred_element_type=F32)
    if latent:
        qmla = qmla * cosq_ref[...] + jnp.dot(cqn, wuqs_ref[...], preferred_element_type=F32) * sinq_ref[...]
    qmla_t = (qmla * ((QK_NOPE + QK_ROPE) ** -0.5)).T.astype(BF16)
    for hd in range(N_HEADS):
        lo = hd * (QK_NOPE + QK_ROPE)
        qmla_ref[hd] = qmla_t[lo:lo + QK_NOPE + QK_ROPE, :]
    ckvn = _rms(proj(C_CKV, KV_LORA)) * kvg_ref[...]
    ckv_ref[...] = ckvn
    kv = jnp.dot(ckvn.astype(BF16), wukv_ref[...], preferred_element_type=F32)
    kv_t = kv.T.astype(BF16)
    per = QK_NOPE + V_HEAD
    for hd in range(N_HEADS):
        kmlat_ref[hd, :, 0:QK_NOPE] = kv[:, hd * per:hd * per + QK_NOPE].astype(BF16)
        kmlat_ref[hd, :, QK_NOPE:QK_NOPE + QK_ROPE] = kr
        vmla_ref[hd, 0:V_HEAD, :] = kv_t[hd * per + QK_NOPE:(hd + 1) * per, :]
        vmla_ref[hd, V_HEAD:VT_ROWS, :] = ones


def _mod_spec(st, tm=TM):
    tps = st["n"] // tm
    return pl.BlockSpec((1, 6, D_MODEL), lambda i, *_: (1 + i // tps if st["latent"] else 0, 0, 0))


def _input_stage(x, mod, npre, w_all, tabs, qg, wuq, wuqs, kvg, wukv, st):
    t_all = x.shape[0]
    tm = TM
    tps = st["n"] // tm
    row = lambda i: (i, 0)
    tab = lambda w: pl.BlockSpec((tm, w), lambda i: (i % tps if st["latent"] else 0, 0))
    hrow = lambda i: (0, i, 0)
    const2 = lambda i: (0, 0)
    tok = lambda w: pl.BlockSpec((tm, w), row)
    headed = lambda w: pl.BlockSpec((N_HEADS, tm, w), hrow)
    headed_t = lambda d: pl.BlockSpec((N_HEADS, d, tm), lambda i: (0, 0, i))
    full = lambda a: pl.BlockSpec(a.shape, const2)
    cos, sin, cosq, sinq = tabs
    out_shapes = [
        (headed_t(DIFF_HALF), (N_HEADS, DIFF_HALF, t_all), BF16),
        (headed_t(DIFF_HALF), (N_HEADS, DIFF_HALF, t_all), BF16),
        (headed(DIFF_HALF), (N_HEADS, t_all, DIFF_HALF), BF16),
        (headed(DIFF_HALF), (N_HEADS, t_all, DIFF_HALF), BF16),
        (headed_t(VT_ROWS), (N_HEADS, VT_ROWS, t_all), BF16),
        (tok(GROUP_WIDTH), (t_all, GROUP_WIDTH), F32),
        (tok(GROUP_WIDTH), (t_all, GROUP_WIDTH), F32),
        (tok(QKV_SLOTS), (t_all, QKV_SLOTS), BF16),
        (tok(GROUP_WIDTH), (t_all, GROUP_WIDTH), F32),
        (tok(QKV_SLOTS), (t_all, QKV_SLOTS), BF16),
        (tok(GROUP_WIDTH), (t_all, GROUP_WIDTH), F32),
        (tok(LANES), (t_all, LANES), F32),
        (headed_t(QK_NOPE + QK_ROPE), (N_HEADS, QK_NOPE + QK_ROPE, t_all), BF16),
        (tok(KV_LORA), (t_all, KV_LORA), F32),
        (headed(QK_NOPE + QK_ROPE), (N_HEADS, t_all, QK_NOPE + QK_ROPE), BF16),
        (headed_t(VT_ROWS), (N_HEADS, VT_ROWS, t_all), BF16),
    ]
    return pl.pallas_call(
        functools.partial(_in_kernel, latent=st["latent"]),
        grid=(t_all // tm,),
        in_specs=[tok(D_MODEL), _mod_spec(st, tm),
                  full(npre), *[full(w) for w in w_all], tab(GROUP_WIDTH), tab(GROUP_WIDTH), tab(Q_MLA), tab(Q_MLA),
                  full(qg), full(wuq), full(wuqs), full(kvg), full(wukv)],
        out_specs=[s for s, _, _ in out_shapes],
        out_shape=[jax.ShapeDtypeStruct(shp, dt) for _, shp, dt in out_shapes],
        compiler_params=_cparams(("parallel",)),
        name="input_stage_" + st["name"],
    )(x, mod, npre, *w_all, cos, sin, cosq, sinq, qg, wuq, wuqs, kvg, wukv)


def _attn_t_kernel(*refs, n_soft, n_new, n_ctx, lam_init):
    refs = list(refs)
    qt_refs, k_refs, vt_ref = refs[:n_soft], refs[n_soft:2 * n_soft], refs[2 * n_soft]
    pos = 2 * n_soft + 1
    if n_ctx:
        ck_refs, cvt_ref = refs[pos:pos + n_soft], refs[pos + n_soft]
        pos += n_soft + 1
    if n_soft == 2:
        lam_ref, g_ref = refs[pos:pos + 2]
        pos += 2
    o_ref = refs[pos]
    tq = o_ref.shape[0]
    nchain = n_soft * N_HEADS
    order = [(j, hd) for hd in range(N_HEADS) for j in range(n_soft)]

    def chunk(state, k_of, vt_of):
        ms, accs = state
        new_m, new_acc = list(ms), list(accs)
        ss = [jnp.dot(k_of(j, hd), qt_refs[j][hd], preferred_element_type=F32) for j, hd in order]
        ps, alphas = [], []
        for (j, hd), s in zip(order, ss):
            c = j * N_HEADS + hd
            s3 = s.reshape(s.shape[0] // 8, 8, tq)
            top = jnp.max(jnp.max(s3, axis=0), axis=0, keepdims=True)
            m_new = jnp.maximum(ms[c], jnp.broadcast_to(top, (8, tq)))
            ps.append(jnp.exp(s3 - m_new[None]).reshape(s.shape).astype(BF16))
            alphas.append(jnp.exp(ms[c] - m_new))
            new_m[c] = m_new
        for (j, hd), p, alpha in zip(order, ps, alphas):
            c = j * N_HEADS + hd
            scaled = (accs[c].reshape(VT_ROWS // 8, 8, tq) * alpha[None]).reshape(VT_ROWS, tq)
            new_acc[c] = scaled + jnp.dot(vt_of(hd), p, preferred_element_type=F32)
        return tuple(new_m), tuple(new_acc)

    state = (tuple(jnp.full((8, tq), -jnp.inf, F32) for _ in range(nchain)),
             tuple(jnp.zeros((VT_ROWS, tq), F32) for _ in range(nchain)))
    if n_ctx:
        cstep = min(TK, n_ctx)
        for i in range(n_ctx // cstep):
            state = chunk(state, lambda j, hd, i=i: ck_refs[j][hd, i * cstep:(i + 1) * cstep, :],
                          lambda hd, i=i: cvt_ref[hd, :, i * cstep:(i + 1) * cstep])
    step = min(TK, n_new)

    def body(i, st):
        start = pl.multiple_of(i * step, step)
        return chunk(st, lambda j, hd: k_refs[j][hd, pl.ds(start, step), :],
                     lambda hd: vt_ref[hd, :, pl.ds(start, step)])

    _, accs = lax.fori_loop(0, n_new // step, body, state)

    def normalised(c):
        num, den = accs[c][:V_HEAD], accs[c][V_HEAD:V_HEAD + 8]
        return (num.reshape(V_HEAD // 8, 8, tq) / den[None]).reshape(V_HEAD, tq)

    if n_soft == 2:
        lv = lam_ref[...]
        lam = (jnp.exp(jnp.sum(lv[0:1] * lv[1:2], axis=-1, keepdims=True))
               - jnp.exp(jnp.sum(lv[2:3] * lv[3:4], axis=-1, keepdims=True)) + lam_init)
    outs = []
    for hd in range(N_HEADS):
        out = normalised(hd)
        if n_soft == 2:
            a = out - lam * normalised(N_HEADS + hd)
            out = a * lax.rsqrt(jnp.mean(a * a, axis=0, keepdims=True) + EPS) * g_ref[...] * (1.0 - lam_init)
        outs.append(out)
    o_ref[...] = jnp.concatenate(outs, axis=0).T


def _attention_t(name, qts, ks, vt, ctx_ks, ctx_vt, extras, nb, n, n_ctx, lam_init=0.0):
    tq = min(TQ, n)
    nqt = n // tq
    n_soft = len(qts)
    in_specs = [pl.BlockSpec((N_HEADS, a.shape[1], tq), lambda b, i: (0, 0, b * nqt + i)) for a in qts]
    in_specs += [pl.BlockSpec((N_HEADS, n, a.shape[-1]), lambda b, i: (0, b, 0)) for a in ks]
    in_specs += [pl.BlockSpec((N_HEADS, VT_ROWS, n), lambda b, i: (0, 0, b))]
    args = list(qts) + list(ks) + [vt]
    if n_ctx:
        in_specs += [pl.BlockSpec((N_HEADS, n_ctx, a.shape[-1]), lambda b, i: (0, b, 0)) for a in ctx_ks]
        in_specs += [pl.BlockSpec((N_HEADS, VT_ROWS, n_ctx), lambda b, i: (0, 0, b))]
        args += list(ctx_ks) + [ctx_vt]
    in_specs += [pl.BlockSpec(a.shape, lambda b, i: (0, 0)) for a in extras]
    args += list(extras)
    return pl.pallas_call(
        functools.partial(_attn_t_kernel, n_soft=n_soft, n_new=n, n_ctx=n_ctx, lam_init=lam_init),
        grid=(nb, nqt),
        in_specs=in_specs,
        out_specs=pl.BlockSpec((tq, N_HEADS * V_HEAD), lambda b, i: (b * nqt + i, 0)),
        out_shape=jax.ShapeDtypeStruct((nb * n, N_HEADS * V_HEAD), F32),
        compiler_params=_cparams(("parallel", "parallel")),
        name=name,
    )(*args)


def _tri(lower):
    r = lax.broadcasted_iota(jnp.int32, (CHUNK, CHUNK), 0)
    c = lax.broadcasted_iota(jnp.int32, (CHUNK, CHUNK), 1)
    return (c <= r) if lower else (c >= r)


def _nt(a, b):
    return lax.dot_general(a, b, (((1,), (1,)), ((), ())), preferred_element_type=F32)


def _head_slices(qkv, hd):
    return tuple(qkv[:, (j * N_HEADS + hd) * LANES:(j * N_HEADS + hd) * LANES + HEAD_DIM] for j in range(3))


def _row_dot(row, mat, transpose_mat):
    row8 = jnp.broadcast_to(row, (8, row.shape[1])).astype(BF16)
    dims = (((1,), (1,)), ((), ())) if transpose_mat else (((1,), (0,)), ((), ()))
    return lax.dot_general(row8, mat, dims, preferred_element_type=F32)[0:1]


def _load_state_t(dst, src):
    for d in range(2):
        for hd in range(N_HEADS):
            dst[d, hd] = src[d, hd].T


def _mlstm_step(dirs, bias_ref):
    k_scale = HEAD_DIM ** -0.5
    lower = _tri(True).astype(F32)
    upper = _tri(False).astype(F32)
    chains = []
    for d, qkv_ref, tail_ref, _, ct_s, n_s, m_s in dirs:
        g = tail_ref[...] + bias_ref[...]
        ls = _log_sigmoid(g)
        g_t, ls_t = g.T, ls.T
        left, right = (lower, upper) if d == 0 else (upper, lower)
        cum_col = jnp.dot(left, ls, precision=HIGHEST, preferred_element_type=F32)
        cum_row = jnp.dot(ls_t, right, precision=HIGHEST, preferred_element_type=F32)
        qkv = qkv_ref[...]
        for hd in range(N_HEADS):
            ci = TAIL_GATE0 + 4 * (2 * d) + hd
            cf = ci + 4
            q, k, v = _head_slices(qkv, hd)
            chains.append(dict(
                d=d, hd=hd, q=q, k=k, v=v, li_row=g_t[ci:ci + 1, :], b_row=cum_row[cf:cf + 1, :],
                c_col=g[:, ci:ci + 1] - cum_col[:, cf:cf + 1],
                m_prev=m_s[d:d + 1, hd:hd + 1], ct=ct_s[d, hd], n=n_s[d, hd:hd + 1, :],
                refs=(ct_s, n_s, m_s)))
    for c in chains:
        c["s"] = _nt(c["k"], c["q"])
        c["v_t"] = c["v"].T
    for c in chains:
        valid = _tri(c["d"] != 0)
        log_d = jnp.where(valid, c["b_row"] + c["c_col"], -jnp.inf)
        log_inter = c["b_row"] + c["m_prev"]
        c["m_t"] = jnp.maximum(log_inter, jnp.max(log_d, axis=0, keepdims=True))
        c["w"] = c["s"] * k_scale * jnp.exp(log_d - c["m_t"])
        c["w_inter"] = jnp.exp(log_inter - c["m_t"])
    for c in chains:
        num = (jnp.dot(c["v_t"], c["w"].astype(BF16), preferred_element_type=F32)
               + c["w_inter"] * _nt(c["ct"].astype(BF16), c["q"]))
        den = jnp.sum(c["w"], axis=0, keepdims=True) + c["w_inter"] * _row_dot(c["n"], c["q"], True)
        c["h"] = num / jnp.maximum(jnp.abs(den), jnp.exp(-c["m_t"]))
    for c in chains:
        last = CHUNK - 1 if c["d"] == 0 else 0
        c["m_new"] = c["m_t"][:, last:last + 1]
        b_last = c["b_row"][:, last:last + 1]
        w_end_row = jnp.exp(b_last - c["b_row"] + c["li_row"] - c["m_new"])
        w_end_col = jnp.exp(c["c_col"] + (b_last - c["m_new"]))
        decay = jnp.exp(b_last + c["m_prev"] - c["m_new"])
        kw = (c["k"].astype(F32) * (w_end_col * k_scale)).astype(BF16)
        c["ct_new"] = decay * c["ct"] + jnp.dot(c["v_t"], kw, preferred_element_type=F32)
        c["n_new"] = decay * c["n"] + _row_dot(w_end_row, c["k"], False) * k_scale
    for j, entry in enumerate(dirs):
        entry[3][...] = jnp.concatenate([c["h"] for c in chains[j * N_HEADS:(j + 1) * N_HEADS]], axis=0).T
    for c in chains:
        d, hd = c["d"], c["hd"]
        ct_s, n_s, m_s = c["refs"]
        ct_s[d, hd] = c["ct_new"]
        n_s[d, hd:hd + 1, :] = c["n_new"]
        m_s[d:d + 1, hd:hd + 1] = c["m_new"]


def _ret_step(dirs, lg_ref):
    k_scale = HEAD_DIM ** -0.5
    s = lax.broadcasted_iota(jnp.int32, (CHUNK, CHUNK), 0)
    t = lax.broadcasted_iota(jnp.int32, (CHUNK, CHUNK), 1)
    lane = lax.broadcasted_iota(jnp.int32, (1, CHUNK), 1)
    chains = []
    for d, qkv_ref, _, st_s in dirs:
        lag = (t - s) if d == 0 else (s - t)
        pos = (lane if d == 0 else CHUNK - 1 - lane).astype(F32)
        qkv = qkv_ref[...]
        for hd in range(N_HEADS):
            q, k, v = _head_slices(qkv, hd)
            chains.append(dict(d=d, hd=hd, q=q, k=k, v=v, lag=lag, pos=pos, st=st_s[d, hd], ref=st_s,
                               lg=_log_sigmoid(lg_ref[d:d + 1, hd:hd + 1])))
    for c in chains:
        c["s"] = _nt(c["k"], c["q"])
        c["v_t"] = c["v"].T
    for c in chains:
        intra = jnp.where(c["lag"] >= 0, jnp.exp(jnp.maximum(c["lag"], 0).astype(F32) * c["lg"]), 0.0)
        c["a"] = (c["s"] * k_scale * intra).astype(BF16)
    for c in chains:
        inter = jnp.exp((c["pos"] + 1.0) * c["lg"])
        c["o"] = jnp.dot(c["v_t"], c["a"], preferred_element_type=F32) + inter * _nt(c["st"].astype(BF16), c["q"])
    row = lax.broadcasted_iota(jnp.int32, (CHUNK, HEAD_DIM), 0)
    for c in chains:
        src_pos = (row if c["d"] == 0 else CHUNK - 1 - row).astype(F32)
        tail = jnp.exp((CHUNK - 1.0 - src_pos) * c["lg"]) * k_scale
        kw = (c["k"].astype(F32) * tail).astype(BF16)
        c["st_new"] = jnp.exp(CHUNK * c["lg"]) * c["st"] + jnp.dot(c["v_t"], kw, preferred_element_type=F32)
    for j, entry in enumerate(dirs):
        entry[2][...] = jnp.concatenate([c["o"] for c in chains[j * N_HEADS:(j + 1) * N_HEADS]], axis=0).T
    for c in chains:
        c["ref"][c["d"], c["hd"]] = c["st_new"]


def _scan_kernel(tab_ref, mqf_ref, tailf_ref, mqb_ref, tailb_ref, rqf_ref, rqb_ref, bias_ref, lg_ref,
                 c0_ref, n0_ref, m0_ref, s0_ref,
                 hf_ref, hb_ref, of_ref, ob_ref, cn_ref, nn_ref, mn_ref, sn_ref, ct_s, n_s, m_s, st_s):
    i = pl.program_id(0)
    seqs = range(ct_s.shape[0])

    @pl.when(tab_ref[3, i] == 1)
    def _():
        for p in seqs:
            _load_state_t(ct_s.at[p], c0_ref.at[p])
            _load_state_t(st_s.at[p], s0_ref.at[p])
        n_s[...] = n0_ref[...]
        m_s[...] = m0_ref[...]

    mdirs, rdirs = [], []
    for p in seqs:
        state = (ct_s.at[p], n_s.at[p], m_s.at[p])
        mdirs += [(0, mqf_ref.at[p], tailf_ref.at[p], hf_ref.at[p]) + state,
                  (1, mqb_ref.at[p], tailb_ref.at[p], hb_ref.at[p]) + state]
        rdirs += [(0, rqf_ref.at[p], of_ref.at[p], st_s.at[p]), (1, rqb_ref.at[p], ob_ref.at[p], st_s.at[p])]
    _mlstm_step(mdirs, bias_ref)
    _ret_step(rdirs, lg_ref)

    @pl.when(tab_ref[4, i] == 1)
    def _():
        for p in seqs:
            _load_state_t(cn_ref.at[p], ct_s.at[p])
            _load_state_t(sn_ref.at[p], st_s.at[p])
        nn_ref[...] = n_s[...]
        mn_ref[...] = m_s[...]


def _scan_table(st):
    nch = st["n"] // CHUNK
    rows = [(c, nch - 1 - c, g, int(c == 0), int(c == nch - 1))
            for g in range(st["nb"] // _scan_group(st)) for c in range(nch)]
    return jnp.asarray(np.array(rows, dtype=np.int32).T)


def _scan_group(st):
    return next(g for g in (4, 2, 1) if st["nb"] % g == 0)


def _scans(tab, mqkv, tail, bias_row, rqkv, decay_logit, c0, n0, m0, s0, st):
    nb, n, grp = st["nb"], st["n"], _scan_group(st)
    seq3 = lambda a: a.reshape(nb, n, a.shape[-1])
    fwd = lambda i, tab: (tab[2, i], tab[0, i], 0)
    bwd = lambda i, tab: (tab[2, i], tab[1, i], 0)
    const = lambda i, tab: (0, 0)
    rows = lambda width, where: pl.BlockSpec((grp, CHUNK, width), where)
    state = lambda a: pl.BlockSpec((grp,) + a.shape[1:], lambda i, tab: (tab[2, i],) + (0,) * (a.ndim - 1))
    states = (c0, n0, m0, s0)
    grid_spec = pltpu.PrefetchScalarGridSpec(
        num_scalar_prefetch=1,
        grid=(tab.shape[1],),
        in_specs=[rows(QKV_SLOTS, fwd), rows(LANES, fwd), rows(QKV_SLOTS, bwd), rows(LANES, bwd),
                  rows(QKV_SLOTS, fwd), rows(QKV_SLOTS, bwd),
                  pl.BlockSpec((1, LANES), const), pl.BlockSpec(decay_logit.shape, const)]
        + [state(a) for a in states],
        out_specs=[rows(GROUP_WIDTH, fwd), rows(GROUP_WIDTH, bwd), rows(GROUP_WIDTH, fwd), rows(GROUP_WIDTH, bwd)]
        + [state(a) for a in states],
        scratch_shapes=[pltpu.VMEM((grp,) + a.shape[1:], F32) for a in states],
    )
    outs = pl.pallas_call(
        _scan_kernel,
        grid_spec=grid_spec,
        out_shape=[jax.ShapeDtypeStruct((nb, n, GROUP_WIDTH), F32)] * 4
        + [jax.ShapeDtypeStruct(a.shape, F32) for a in states],
        compiler_params=_cparams(("arbitrary",)),
        name="recurrent_scans",
    )(tab, seq3(mqkv), seq3(tail), seq3(mqkv), seq3(tail), seq3(rqkv), seq3(rqkv), bias_row, decay_logit, *states)
    return [o.reshape(nb * n, GROUP_WIDTH) for o in outs[:4]] + list(outs[4:])


def _proj_kernel(x_ref, w_ref, o_ref):
    o_ref[...] = _bdot(x_ref[...], w_ref[...]).astype(o_ref.dtype)


def _project(x, w, dtype):
    return pl.pallas_call(
        _proj_kernel,
        out_shape=jax.ShapeDtypeStruct((x.shape[0], w.shape[1]), dtype),
        name="ctx_kv_proj",
    )(x, w)


def _first_argmax_mask(cur, axis, size):
    io = lax.broadcasted_iota(jnp.int32, cur.shape, axis)
    mx = jnp.max(cur, axis=axis, keepdims=True)
    ix = jnp.min(jnp.where(cur == mx, io, size), axis=axis, keepdims=True)
    return io == ix


def _route(scores_t, bias_col):
    tm = scores_t.shape[1]
    per = N_EXPERTS // N_GROUPS
    sel = scores_t + bias_col
    s3 = sel.reshape(N_GROUPS, per, tm)
    hit1 = _first_argmax_mask(s3, 1, per)
    m1 = jnp.max(s3, axis=1, keepdims=True)
    m2 = jnp.max(jnp.where(hit1, -jnp.inf, s3), axis=1, keepdims=True)
    cur = m1 + m2
    gsel = None
    for _ in range(TOPK_GROUPS):
        hit = _first_argmax_mask(cur, 0, N_GROUPS)
        gsel = hit if gsel is None else jnp.logical_or(gsel, hit)
        cur = jnp.where(hit, -jnp.inf, cur)
    cur = jnp.where(gsel, s3, -jnp.inf).reshape(N_EXPERTS, tm)
    chosen, hits = None, []
    for _ in range(TOP_K):
        hit = _first_argmax_mask(cur, 0, N_EXPERTS)
        hits.append(hit)
        chosen = hit if chosen is None else jnp.logical_or(chosen, hit)
        cur = jnp.where(hit, -jnp.inf, cur)
    w = jnp.where(chosen, scores_t, 0.0)
    return w / jnp.sum(w, axis=0, keepdims=True) * ROUTED_SCALE, chosen, hits


def _out_kernel(oa_ref, od_ref, hf_ref, hb_ref, mo_ref, of_ref, ob_ref, rg_ref,
                x_ref, mod_ref, mg_ref, npost_ref, npre_ref, wout_ref, wrt_ref, rb_ref,
                x1_ref, h2_ref, hpk_ref, eidx_ref, rank_ref, wk_ref, cnt_ref, count_s):
    @pl.when(pl.program_id(0) == 0)
    def _():
        count_s[...] = jnp.zeros(count_s.shape, F32)

    a = oa_ref[...]
    dd = od_ref[...]
    b = _rms_heads(hf_ref[...] + hb_ref[...]) * mg_ref[...] * _sigmoid(mo_ref[...])
    r = _rms_heads(of_ref[...] + ob_ref[...]) * _silu(rg_ref[...])
    mix = sum(jnp.dot(part.astype(BF16), wout_ref[j * GROUP_WIDTH:(j + 1) * GROUP_WIDTH, :],
                      preferred_element_type=F32) for j, part in enumerate((a, b, r, dd)))
    m = mod_ref[0]
    x1 = x_ref[...] + m[2:3] * (_rms(mix) * npost_ref[...])
    x1_ref[...] = x1
    h2 = (_rms(x1) * npre_ref[...] * (1.0 + m[4:5]) + m[3:4]).astype(BF16)
    h2_ref[...] = h2
    hpk_ref[...] = _pack_rows(h2)

    logits_t = lax.dot_general(wrt_ref[...], h2, (((1,), (1,)), ((), ())), preferred_element_type=F32)
    w_t, chosen, hits = _route(_sigmoid(logits_t), rb_ref[...])
    tm = w_t.shape[1]
    src = lax.broadcasted_iota(jnp.int32, (tm, tm), 0)
    dst = lax.broadcasted_iota(jnp.int32, (tm, tm), 1)
    before = jnp.where(src < dst, 1.0, 0.0).astype(BF16)
    picked = jnp.where(chosen, 1.0, 0.0)
    rank = jnp.dot(picked.astype(BF16), before, preferred_element_type=F32) + count_s[...]
    count_s[...] += jnp.sum(picked, axis=1, keepdims=True)
    cnt_ref[...] = jnp.broadcast_to(count_s[...], cnt_ref.shape)
    e_io = lax.broadcasted_iota(jnp.int32, w_t.shape, 0)
    pick = lambda hit, val: jnp.sum(jnp.where(hit, val, jnp.zeros_like(val)), axis=0, keepdims=True)
    eidx_ref[...] = jnp.concatenate([pick(h, e_io) for h in hits], axis=0)
    rank_ref[...] = jnp.concatenate([pick(h, rank) for h in hits], axis=0).astype(jnp.int32)
    wk = jnp.concatenate([pick(h, w_t) for h in hits] + [jnp.zeros((LANES - TOP_K, tm), F32)], axis=0)
    wk_ref[...] = wk.T


def _output_stage(oa, od, hf, hb, mo, of, ob, rg, x, mod, mg, npost, npre, wout, wrt, rb, st):
    t_all = x.shape[0]
    row = lambda i: (i, 0)
    const2 = lambda i: (0, 0)
    tok = lambda w: pl.BlockSpec((TM, w), row)
    full = lambda a: pl.BlockSpec(a.shape, const2)
    return pl.pallas_call(
        _out_kernel,
        grid=(t_all // TM,),
        in_specs=[tok(GROUP_WIDTH)] * 8 + [tok(D_MODEL), _mod_spec(st),
                                   full(mg), full(npost), full(npre), full(wout), full(wrt), full(rb)],
        out_specs=[tok(D_MODEL), tok(D_MODEL), tok(D_MODEL // 2),
                   pl.BlockSpec((TOP_K, TM), lambda i: (0, i)), pl.BlockSpec((TOP_K, TM), lambda i: (0, i)),
                   tok(LANES), pl.BlockSpec((N_EXPERTS, LANES), const2)],
        out_shape=[jax.ShapeDtypeStruct((t_all, D_MODEL), F32),
                   jax.ShapeDtypeStruct((t_all, D_MODEL), BF16),
                   jax.ShapeDtypeStruct((t_all, D_MODEL // 2), jnp.uint32),
                   jax.ShapeDtypeStruct((TOP_K, t_all), jnp.int32),
                   jax.ShapeDtypeStruct((TOP_K, t_all), jnp.int32),
                   jax.ShapeDtypeStruct((t_all, LANES), F32),
                   jax.ShapeDtypeStruct((N_EXPERTS, LANES), F32)],
        scratch_shapes=[pltpu.VMEM((N_EXPERTS, 1), F32)],
        compiler_params=_cparams(("arbitrary",)),
        name="output_stage_" + st["name"],
    )(oa, od, hf, hb, mo, of, ob, rg, x, mod, mg, npost, npre, wout, wrt, rb)


SC_CORES, SC_SUBCORES = 2, 16
SC_WORKERS = SC_CORES * SC_SUBCORES
SLOT_BLOCK = 512
DISPATCH_ROWS = 64
COMBINE_ROWS = 64
WEIGHT_RING = 3


def _sc_mesh():
    return plsc.VectorSubcoreMesh(core_axis_name="core", subcore_axis_name="subcore")


def _sc_worker():
    return lax.axis_index("subcore") * SC_CORES + lax.axis_index("core")


def _sc_dispatch(hpk, dest, n_slots):
    t_all, width = hpk.shape
    per_worker = t_all // SC_WORKERS
    assert t_all % (SC_WORKERS * DISPATCH_ROWS) == 0

    @functools.partial(
        pl.kernel, mesh=_sc_mesh(), out_type=jax.ShapeDtypeStruct((n_slots, width), hpk.dtype),
        scratch_types=[pltpu.VMEM((DISPATCH_ROWS, width), hpk.dtype)]
        + [pltpu.VMEM((DISPATCH_ROWS,), jnp.int32)] * TOP_K + [pltpu.SemaphoreType.DMA])
    def dispatch(x_hbm, d_hbm, o_hbm, rows_v, *rest):
        idx, sem = rest[:TOP_K], rest[TOP_K]

        @pl.loop(0, per_worker // DISPATCH_ROWS)
        def _(j):
            base = _sc_worker() * per_worker + j * DISPATCH_ROWS
            pltpu.sync_copy(x_hbm.at[pl.ds(base, DISPATCH_ROWS)], rows_v)
            for k in range(TOP_K):
                pltpu.sync_copy(d_hbm.at[pl.ds(k * t_all + base, DISPATCH_ROWS)], idx[k])
            copies = [pltpu.async_copy(rows_v, o_hbm.at[idx[k]], sem) for k in range(TOP_K)]
            for c in copies:
                c.wait()

    return dispatch(hpk, dest)


def _sc_combine(yb, dest, t_all):
    width = yb.shape[1]
    per_worker = t_all // SC_WORKERS
    assert t_all % (SC_WORKERS * COMBINE_ROWS) == 0

    @functools.partial(
        pl.kernel, mesh=_sc_mesh(), out_type=jax.ShapeDtypeStruct((TOP_K, t_all, width), yb.dtype),
        scratch_types=[pltpu.VMEM((COMBINE_ROWS, width), yb.dtype)] * 2
        + [pltpu.VMEM((COMBINE_ROWS,), jnp.int32)] * TOP_K + [pltpu.SemaphoreType.DMA] * 2)
    def combine(y_hbm, d_hbm, o_hbm, rows_a, rows_b, *rest):
        idx, sems = rest[:TOP_K], rest[TOP_K:]
        bufs = (rows_a, rows_b)

        @pl.loop(0, per_worker // COMBINE_ROWS)
        def _(j):
            base = _sc_worker() * per_worker + j * COMBINE_ROWS
            for k in range(TOP_K):
                pltpu.sync_copy(d_hbm.at[pl.ds(k * t_all + base, COMBINE_ROWS)], idx[k])
            pending = pltpu.async_copy(y_hbm.at[idx[0]], bufs[0], sems[0])
            for k in range(TOP_K):
                pending.wait()
                if k + 1 < TOP_K:
                    pending = pltpu.async_copy(y_hbm.at[idx[k + 1]], bufs[(k + 1) % 2], sems[(k + 1) % 2])
                pltpu.sync_copy(bufs[k % 2], o_hbm.at[k, pl.ds(base, COMBINE_ROWS)])

    return combine(yb, dest)


def _pack_rows(x):
    bits = pltpu.bitcast(x.astype(BF16).astype(F32), jnp.uint32)
    w = x.shape[-1] // 2
    return (bits[..., :w] >> 16) | (bits[..., w:] & jnp.uint32(0xFFFF0000))


def _unpack_rows(words):
    return pltpu.bitcast(words << 16, F32), pltpu.bitcast(words & jnp.uint32(0xFFFF0000), F32)


def _expert_kernel(ord_ref, used_ref, eo_ref, nd_ref, xs_ref, wg_hbm, wu_hbm, wd_hbm, y_ref,
                   stage_g, stage_u, stage_d, wg_s, wu_s, wd_s, sems, *, layer):
    b = pl.program_id(0)
    nd = nd_ref[0]

    def copies(j, slot):
        e = eo_ref[j]
        return [pltpu.make_async_copy(src.at[layer, e], dst.at[slot], sems.at[slot, i])
                for i, (src, dst) in enumerate(((wg_hbm, stage_g), (wu_hbm, stage_u), (wd_hbm, stage_d)))]

    for first in range(WEIGHT_RING):
        @pl.when(jnp.logical_and(b == 0, first < nd))
        def _(first=first):
            for c in copies(first, first):
                c.start()

    j = ord_ref[b]
    fresh = jnp.logical_or(b == 0, j != ord_ref[jnp.maximum(b - 1, 0)])
    for slot in range(WEIGHT_RING):
        @pl.when(jnp.logical_and(jnp.logical_and(fresh, b < used_ref[0]), j % WEIGHT_RING == slot))
        def _(slot=slot):
            for c in copies(j, slot):
                c.wait()
            wg_s[...] = stage_g[slot].astype(BF16)
            wu_s[...] = stage_u[slot].astype(BF16)
            wd_s[...] = stage_d[slot].astype(BF16)

            @pl.when(j + WEIGHT_RING < nd)
            def _():
                for c in copies(j + WEIGHT_RING, slot):
                    c.start()

    @pl.when(b < used_ref[0])
    def _():
        lo, hi = (v.astype(BF16) for v in _unpack_rows(xs_ref[...]))
        half = D_MODEL // 2
        gate = (jnp.dot(lo, wg_s[:half, :], preferred_element_type=F32)
                + jnp.dot(hi, wg_s[half:, :], preferred_element_type=F32))
        up = (jnp.dot(lo, wu_s[:half, :], preferred_element_type=F32)
              + jnp.dot(hi, wu_s[half:, :], preferred_element_type=F32))
        y = jnp.dot((_silu(gate) * up).astype(BF16), wd_s[...], preferred_element_type=F32)
        y_ref[...] = _pack_rows(y)


def _expert_blocks(layer, xs, block_ordinal, blocks_used, expert_order, n_present, wg, wu, wdn):
    n_slots = xs.shape[0]
    rows = lambda b, ordinal, used, eo, nd: (jnp.minimum(b, used[0] - 1), 0)
    grid_spec = pltpu.PrefetchScalarGridSpec(
        num_scalar_prefetch=4,
        grid=(n_slots // SLOT_BLOCK,),
        in_specs=[pl.BlockSpec((SLOT_BLOCK, D_MODEL // 2), rows)] + [pl.BlockSpec(memory_space=pl.ANY)] * 3,
        out_specs=pl.BlockSpec((SLOT_BLOCK, D_MODEL // 2), rows),
        scratch_shapes=[pltpu.VMEM((WEIGHT_RING, D_MODEL, D_EXPERT), F32),
                        pltpu.VMEM((WEIGHT_RING, D_MODEL, D_EXPERT), F32),
                        pltpu.VMEM((WEIGHT_RING, D_EXPERT, D_MODEL), F32),
                        pltpu.VMEM((D_MODEL, D_EXPERT), BF16), pltpu.VMEM((D_MODEL, D_EXPERT), BF16),
                        pltpu.VMEM((D_EXPERT, D_MODEL), BF16), pltpu.SemaphoreType.DMA((WEIGHT_RING, 3))])
    return pl.pallas_call(
        functools.partial(_expert_kernel, layer=layer), grid_spec=grid_spec,
        out_shape=jax.ShapeDtypeStruct((n_slots, D_MODEL // 2), jnp.uint32),
        compiler_params=_cparams(("arbitrary",)),
        name="moe_experts",
    )(block_ordinal, blocks_used, expert_order, n_present, xs, wg, wu, wdn)


def _moe_out_kernel(g_ref, wk_ref, h_ref, x1_ref, mod_ref, npost_ref, sg_ref, su_ref, sd_ref, o_ref):
    h = h_ref[...]
    act = _silu(jnp.dot(h, sg_ref[...], preferred_element_type=F32)) * jnp.dot(h, su_ref[...],
                                                                              preferred_element_type=F32)
    moe = jnp.dot(act.astype(BF16), sd_ref[...], preferred_element_type=F32)
    wk = wk_ref[...]
    lo, hi = 0.0, 0.0
    for k in range(TOP_K):
        g_lo, g_hi = _unpack_rows(g_ref[k])
        lo = lo + g_lo * wk[:, k:k + 1]
        hi = hi + g_hi * wk[:, k:k + 1]
    moe = moe + jnp.concatenate([lo, hi], axis=-1)
    m = mod_ref[0]
    o_ref[...] = x1_ref[...] + m[5:6] * (_rms(moe) * npost_ref[...])


def _moe_out_stage(g, wk, h2, x1, mod, npost, sg, su, sd, st):
    t_all = h2.shape[0]
    row = lambda i: (i, 0)
    full = lambda a: pl.BlockSpec(a.shape, lambda i: (0, 0))
    return pl.pallas_call(
        _moe_out_kernel,
        grid=(t_all // TM,),
        in_specs=[pl.BlockSpec((TOP_K, TM, D_MODEL // 2), lambda i: (0, i, 0)), pl.BlockSpec((TM, LANES), row),
                  pl.BlockSpec((TM, D_MODEL), row), pl.BlockSpec((TM, D_MODEL), row), _mod_spec(st),
                  full(npost), full(sg), full(su), full(sd)],
        out_specs=pl.BlockSpec((TM, D_MODEL), row),
        out_shape=jax.ShapeDtypeStruct((t_all, D_MODEL), F32),
        compiler_params=_cparams(("parallel",)),
        name="moe_combine_" + st["name"],
    )(g, wk, h2, x1, mod, npost, sg, su, sd)


def _moe_stage(layer, h2, hpk, eidx, rank, wk, counts, x1, mod, npost, wg, wu, wdn, sg, su, sd, st):
    t_all = h2.shape[0]
    n_blocks = -(-(t_all * TOP_K + N_EXPERTS * (SLOT_BLOCK - 1)) // SLOT_BLOCK)
    cnt = counts[:, 0].astype(jnp.int32)
    padded = (cnt + SLOT_BLOCK - 1) // SLOT_BLOCK * SLOT_BLOCK
    pad_end = jnp.cumsum(padded)
    pad_start = pad_end - padded
    experts = jnp.arange(N_EXPERTS, dtype=jnp.int32)[:, None, None]
    dest = rank + jnp.sum(jnp.where(eidx[None] == experts, pad_start[:, None, None], 0), axis=0)
    dest = dest.reshape(TOP_K * t_all)
    block_start = jnp.arange(n_blocks, dtype=jnp.int32) * SLOT_BLOCK
    block_expert = jnp.minimum(jnp.sum((pad_end[None, :] <= block_start[:, None]).astype(jnp.int32), axis=1),
                               N_EXPERTS - 1)
    blocks_used = (pad_end[-1:] // SLOT_BLOCK).astype(jnp.int32)
    present = padded > 0
    expert_order = jnp.argsort(jnp.logical_not(present), stable=True).astype(jnp.int32)
    n_present = jnp.sum(present.astype(jnp.int32))[None]
    ordinal_of = jnp.cumsum(present.astype(jnp.int32)) - 1
    block_ordinal = jnp.sum(jnp.where(block_expert[:, None] == experts[:, 0, 0][None, :], ordinal_of[None, :], 0),
                            axis=1).astype(jnp.int32)

    xs = _sc_dispatch(hpk, dest, n_blocks * SLOT_BLOCK)
    yb = _expert_blocks(layer, xs, block_ordinal, blocks_used, expert_order, n_present, wg, wu, wdn)
    g = _sc_combine(yb, dest, t_all)
    return _moe_out_stage(g, wk, h2, x1, mod, npost, sg, su, sd, st)


_PERM32 = np.concatenate([np.arange(8, 16), np.arange(0, 8), np.arange(24, 32), np.arange(16, 24)])


def _rope_tables(n):
    f32 = np.float32
    pos = np.arange(n)
    quarter = QK_ROPE // 4
    inv = (f32(1.0) / (f32(ROPE_THETA) ** (np.arange(quarter, dtype=f32) / f32(quarter)))).astype(f32)
    ang_r = (pos // GRID_W).astype(f32)[:, None] * inv[None, :]
    ang_c = (pos % GRID_W).astype(f32)[:, None] * inv[None, :]
    cos32 = np.concatenate([np.cos(ang_r)] * 2 + [np.cos(ang_c)] * 2, axis=-1).astype(f32)
    sin32 = np.concatenate([-np.sin(ang_r), np.sin(ang_r), -np.sin(ang_c), np.sin(ang_c)], axis=-1).astype(f32)
    cos256, sin256 = np.tile(cos32, (1, 8)), np.tile(sin32, (1, 8))
    one, zero = np.ones((n, QK_NOPE), f32), np.zeros((n, QK_NOPE), f32)
    cosq = np.concatenate([one, cos32] * N_HEADS, axis=-1)
    sinq = np.concatenate([zero, sin32] * N_HEADS, axis=-1)
    return tuple(jnp.asarray(t) for t in (cos256, sin256, cosq, sinq))


def _layer_weights(w_in_l, w_uq_l):
    gw = GROUP_WIDTH
    g0 = W_BEFORE_GATES
    b0 = g0 + 4 * N_HEADS
    kr0 = b0 + W_AFTER_GATES
    assert kr0 + QK_ROPE == w_in_l.shape[1]
    w_a, m_g, w_b, d_kr = w_in_l[:, :g0], w_in_l[:, g0:b0], w_in_l[:, b0:kr0], w_in_l[:, kr0:]
    perm256 = np.concatenate([_PERM32 + 32 * j for j in range(8)])
    pad = lambda w, n: jnp.pad(w, ((0, 0), (0, n - w.shape[1])))
    w_c = jnp.concatenate([pad(jnp.concatenate([d_kr, m_g], axis=-1), LANES),
                           w_a[:, perm256], w_a[:, gw + perm256], pad(d_kr[:, _PERM32], LANES)], axis=-1)
    per = QK_NOPE + QK_ROPE
    permq = np.concatenate([np.concatenate([np.arange(QK_NOPE), QK_NOPE + _PERM32]) + per * j
                            for j in range(N_HEADS)])
    return ((w_a.astype(BF16), w_b.astype(BF16), w_c.astype(BF16)), w_uq_l.astype(BF16),
            w_uq_l[:, permq].astype(BF16))


def kernel(x_prompt, x_sample, cache_diff_k, cache_diff_v, state_mlstm_C, state_mlstm_n, state_mlstm_m, state_ret_S, cache_mla_ckv, cache_mla_krope, c, c_ctx, w_mod, b_mod, norm_pre, norm_post, w_in, w_out, diff_lambda, diff_norm, mlstm_gate_bias, mlstm_norm, ret_decay_logit, mla_q_norm, mla_w_uq, mla_kv_norm, mla_w_ukv, moe_w_router, moe_router_bias, moe_w_gate, moe_w_up, moe_w_down, shared_w_gate, shared_w_up, shared_w_down):
    bp, n_p, _ = x_prompt.shape
    bs, n_s, _ = x_sample.shape
    depth = w_in.shape[0]
    past = cache_diff_k.shape[2]
    assert (bp * n_p) % TM == 0 and n_s % TM == 0 and n_p % CHUNK == 0 and n_s % min(TK, n_s) == 0
    assert past % min(TK, past) == 0 and bs + 1 <= 8 and n_s % GRID_W == 0

    streams = (dict(name="ctx", nb=bp, n=n_p, latent=False), dict(name="latent", nb=bs, n=n_s, latent=True))
    xs_by_stream = [x_prompt.reshape(bp * n_p, D_MODEL), x_sample.reshape(bs * n_s, D_MODEL)]
    cond = jnp.zeros((8, D_MODEL), F32).at[0].set(c_ctx).at[1:1 + bs].set(c)
    mod_all = _modulation(cond, w_mod, b_mod).reshape(depth, 8, 6, D_MODEL)
    tabs = _rope_tables(n_s)
    scan_tabs = [_scan_table(st) for st in streams]

    outs = [[] for _ in range(8)]
    for l in range(depth):
        lam_init = 0.8 - 0.6 * math.exp(-0.3 * l)
        mod = mod_all[l]
        w_all, wuq, wuqs = _layer_weights(w_in[l], mla_w_uq[l])
        wukv = mla_w_ukv[l].astype(BF16)
        wout, wrt, rbias = w_out[l].astype(BF16), moe_w_router[l].T.astype(BF16), moe_router_bias[l][:, None]
        shared = [w[l].astype(BF16) for w in (shared_w_gate, shared_w_up, shared_w_down)]
        lamv, dg = diff_lambda[l], diff_norm[l][:, None]
        bias_row = jnp.zeros((1, LANES), F32).at[0, TAIL_GATE0:TAIL_GATE0 + 16].set(mlstm_gate_bias[l].reshape(16))

        with_ones = lambda vt: jnp.concatenate([vt, jnp.ones((N_HEADS, VT_ONES, bs * past), BF16)], axis=1)
        ck = jnp.transpose(cache_diff_k[:, l], (2, 0, 1, 3)).reshape(N_HEADS, bs * past, HEAD_DIM).astype(BF16)
        cvt = with_ones(jnp.transpose(cache_diff_v[:, l], (2, 3, 0, 1)).reshape(N_HEADS, HEAD_DIM, bs * past)
                        .astype(BF16))
        kvc = _project(cache_mla_ckv[:, l].reshape(bs * past, KV_LORA), wukv, BF16)
        kvc = kvc.reshape(bs * past, N_HEADS, QK_NOPE + V_HEAD)
        krc = jnp.broadcast_to(cache_mla_krope[:, l].reshape(bs * past, 1, QK_ROPE).astype(BF16),
                               (bs * past, N_HEADS, QK_ROPE))
        kc = jnp.transpose(jnp.concatenate([kvc[..., :QK_NOPE], krc], axis=-1), (1, 0, 2))
        vct = with_ones(jnp.transpose(kvc[..., QK_NOPE:], (1, 2, 0)))

        for si, st in enumerate(streams):
            x, nb, n = xs_by_stream[si], st["nb"], st["n"]
            (aq1t, aq2t, ak1, ak2, avt, ak, av, mqkv, mo, rqkv, rg, tail, qmlat, ckv, kmla, vmlat) = _input_stage(
                x, mod, norm_pre[l, 0:1], w_all, tabs, mla_q_norm[l][None], wuq, wuqs,
                mla_kv_norm[l][None], wukv, st)
            if st["latent"]:
                oa = _attention_t("diff_attn_latent", [aq1t, aq2t], [ak1, ak2], avt,
                                  [ck[..., :DIFF_HALF], ck[..., DIFF_HALF:]], cvt, [lamv, dg], nb, n, past, lam_init)
                od = _attention_t("mla_attn_latent", [qmlat], [kmla], vmlat, [kc], vct, [], nb, n, past)
                states = [s[:, l] for s in (state_mlstm_C, state_mlstm_n, state_mlstm_m, state_ret_S)]
            else:
                oa = _attention_t("diff_attn_ctx", [aq1t, aq2t], [ak1, ak2], avt, None, None, [lamv, dg],
                                  nb, n, 0, lam_init)
                od = _attention_t("mla_attn_ctx", [qmlat], [kmla], vmlat, None, None, [], nb, n, 0)
                states = [jnp.zeros((nb,) + s.shape[2:], F32)
                          for s in (state_mlstm_C, state_mlstm_n, state_mlstm_m, state_ret_S)]
            hf, hb, of, ob, c_n, n_n, m_n, s_n = _scans(scan_tabs[si], mqkv, tail, bias_row, rqkv,
                                                        ret_decay_logit[l], *states, st)

            x1, h2, hpk, eidx, rank, wk, counts = _output_stage(
                oa, od, hf, hb, mo, of, ob, rg, x, mod, mlstm_norm[l][None], norm_post[l, 0:1],
                norm_pre[l, 1:2], wout, wrt, rbias, st)
            xs_by_stream[si] = _moe_stage(l, h2, hpk, eidx, rank, wk, counts, x1, mod, norm_post[l, 1:2],
                                          moe_w_gate, moe_w_up, moe_w_down, *shared, st)
            if not st["latent"]:
                new = (ak.reshape(nb, n, N_HEADS, HEAD_DIM), av.reshape(nb, n, N_HEADS, HEAD_DIM), c_n, n_n, m_n,
                       s_n, ckv.reshape(nb, n, KV_LORA), tail[:, :QK_ROPE].reshape(nb, n, QK_ROPE))
                for o, v in zip(outs, new):
                    o.append(v)

    return (xs_by_stream[0].reshape(bp, n_p, D_MODEL), xs_by_stream[1].reshape(bs, n_s, D_MODEL)) + tuple(
        jnp.stack(o, axis=1) for o in outs)
```
